```python
import jax, jax.numpy as jnp
from jax import lax
import numpy as np

D_MODEL = 1024
BATCH = 8
SEQ = 8192
DEPTH = 1

HEAD_DIM = 64
ATTN_WIDTH = 3 * D_MODEL // 4
N_ATTN_HEADS = ATTN_WIDTH // HEAD_DIM
CONV_WIDTH = D_MODEL - ATTN_WIDTH
MIX_WIDTH = ATTN_WIDTH + CONV_WIDTH
IN_WIDTH = 4 * ATTN_WIDTH + 4 * CONV_WIDTH
CONV_K = 3
DILATED_PATTERNS = ((128, 1), (512, 4), (2048, 16))
BLOCK = 128
ROPE_THETA = 10000.0
NORM_EPS = 1e-6

kernel_name = "hybrid_dilated_attn_shortconv_layer"


def rmsnorm(x, g):
    xf = x.astype(jnp.float32)
    y = xf * lax.rsqrt(jnp.mean(xf * xf, axis=-1, keepdims=True) + NORM_EPS)
    return (y * g.astype(jnp.float32)).astype(x.dtype)


def rope(t, pos):
    half = t.shape[-1] // 2
    inv_freq = ROPE_THETA ** (-jnp.arange(half, dtype=jnp.float32) * 2.0 / t.shape[-1])
    ang = pos.astype(jnp.float32)[:, None] * inv_freq[None, :]
    cos = jnp.cos(ang)[None, :, None, :]
    sin = jnp.sin(ang)[None, :, None, :]
    tf = t.astype(jnp.float32)
    t1, t2 = tf[..., :half], tf[..., half:]
    out = jnp.concatenate([t1 * cos - t2 * sin, t2 * cos + t1 * sin], axis=-1)
    return out.astype(t.dtype)


def banded_window_attn(q, k, v, window_keys):
    N, L, H, hd = q.shape
    Lp = -(-L // BLOCK) * BLOCK
    padw = ((0, 0), (0, Lp - L), (0, 0), (0, 0))
    q, k, v = jnp.pad(q, padw), jnp.pad(k, padw), jnp.pad(v, padw)
    nb = Lp // BLOCK
    qb = q.reshape(N, nb, BLOCK, H, hd)
    kb = k.reshape(N, nb, BLOCK, H, hd)
    vb = v.reshape(N, nb, BLOCK, H, hd)
    shift = ((0, 0), (1, 0), (0, 0), (0, 0), (0, 0))
    kk = jnp.concatenate([jnp.pad(kb, shift)[:, :-1], kb], axis=2)
    vv = jnp.concatenate([jnp.pad(vb, shift)[:, :-1], vb], axis=2)
    s = jnp.einsum('nbqhd,nbkhd->nbhqk', qb, kk).astype(jnp.float32) * (hd ** -0.5)
    qi = jnp.arange(BLOCK)[:, None]
    kj = jnp.arange(2 * BLOCK)[None, :]
    dist = qi + BLOCK - kj
    key_abs = jnp.arange(nb)[:, None, None] * BLOCK + kj[None] - BLOCK
    valid = (dist >= 0)[None] & (dist <= window_keys)[None] & (key_abs >= 0)
    s = jnp.where(valid[None, :, None], s, -jnp.inf)
    m = jnp.max(s, axis=-1, keepdims=True)
    p = jnp.exp(s - m)
    den = jnp.sum(p, axis=-1, keepdims=True)
    lse = (m + jnp.log(den))[..., 0]
    o = jnp.einsum('nbhqk,nbkhd->nbqhd', (p / den).astype(v.dtype), vv)
    o = o.reshape(N, Lp, H, hd)[:, :L]
    lse = lse.transpose(0, 1, 3, 2).reshape(N, Lp, H)[:, :L]
    return o, lse


def dilated_window_attn(q, k, v, window, dilation):
    B, S, H, hd = q.shape
    L = S // dilation

    def to_res(t):
        return t.reshape(B, L, dilation, H, hd).transpose(0, 2, 1, 3, 4).reshape(B * dilation, L, H, hd)

    o, lse = banded_window_attn(to_res(q), to_res(k), to_res(v), window // dilation)
    o = o.reshape(B, dilation, L, H, hd).transpose(0, 2, 1, 3, 4).reshape(B, S, H, hd)
    lse = lse.reshape(B, dilation, L, H).transpose(0, 2, 1, 3).reshape(B, S, H)
    return o, lse


def causal_depthwise_conv(u, w):
    S = u.shape[1]
    up = jnp.pad(u, ((0, 0), (CONV_K - 1, 0), (0, 0)))
    y = up[:, 0:S] * w[0]
    for j in range(1, CONV_K):
        y = y + up[:, j:j + S] * w[j]
    return y


def _fwd_setup_inputs(seed: int = 0) -> dict:
    key = jax.random.key(seed)
    ks = jax.random.split(key, 6)
    x = jax.random.normal(ks[0], (BATCH, SEQ, D_MODEL), jnp.float32)
    norm_pre_g = 1.0 + 0.01 * jax.random.normal(ks[1], (D_MODEL,), jnp.float32)
    w_in = jax.random.normal(ks[2], (D_MODEL, IN_WIDTH), jnp.float32) * D_MODEL ** -0.5
    conv_w = jax.random.normal(ks[3], (CONV_K, CONV_WIDTH), jnp.float32) * CONV_K ** -0.5
    w_out = jax.random.normal(ks[4], (MIX_WIDTH, D_MODEL), jnp.float32) * MIX_WIDTH ** -0.5
    norm_post_g = 1.0 + 0.01 * jax.random.normal(ks[5], (D_MODEL,), jnp.float32)
    return {"x": x, "norm_pre_g": norm_pre_g, "w_in": w_in, "conv_w": conv_w,
            "w_out": w_out, "norm_post_g": norm_post_g}


def _fwd_reference(x, norm_pre_g, w_in, conv_w, w_out, norm_post_g):
    B, S, _ = x.shape
    pos = jnp.arange(S)
    for _layer in range(DEPTH):
        h = rmsnorm(x, norm_pre_g)
        z = jnp.einsum('bsd,de->bse', h, w_in)
        A, C = ATTN_WIDTH, CONV_WIDTH
        cuts = np.cumsum([A, A, A, A, C, C, C])
        q, k, v, g_attn, c_h, c_b, c_c, g_conv = jnp.split(z, cuts, axis=-1)

        q = rope(q.reshape(B, S, N_ATTN_HEADS, HEAD_DIM), pos)
        k = rope(k.reshape(B, S, N_ATTN_HEADS, HEAD_DIM), pos)
        v = v.reshape(B, S, N_ATTN_HEADS, HEAD_DIM)
        outs, lses = [], []
        for window, dilation in DILATED_PATTERNS:
            o_i, lse_i = dilated_window_attn(q, k, v, window, dilation)
            outs.append(o_i)
            lses.append(lse_i)
        mix_w = jax.nn.softmax(jnp.stack(lses, axis=0), axis=0)
        o = jnp.sum(mix_w[..., None] * jnp.stack(outs, axis=0).astype(jnp.float32), axis=0)
        attn_out = o.astype(x.dtype).reshape(B, S, ATTN_WIDTH) * jax.nn.silu(g_attn)

        conv_out = c_b * causal_depthwise_conv(c_c * c_h, conv_w.astype(x.dtype))
        conv_out = conv_out * jax.nn.silu(g_conv)

        mixed = jnp.concatenate([attn_out, conv_out], axis=-1)
        y = jnp.einsum('bse,ed->bsd', mixed, w_out)
        x = x + rmsnorm(y, norm_post_g)
    return x


import jax as _jax
import jax.numpy as _jnp

TWIN_FORMAT = 'train_step'
FWD_PARAMS = ['x', 'norm_pre_g', 'w_in', 'conv_w', 'w_out', 'norm_post_g']
TWIN_WEIGHTS = ['norm_pre_g', 'w_in', 'conv_w', 'w_out', 'norm_post_g']
TWIN_DIFF_INPUT = 'x'
TWIN_INPUTS = ['x', 'norm_pre_g', 'w_in', 'conv_w', 'w_out', 'norm_post_g', 'loss_target', 'm_norm_pre_g', 'm_w_in', 'm_conv_w', 'm_w_out', 'm_norm_post_g', 'v_norm_pre_g', 'v_w_in', 'v_conv_w', 'v_w_out', 'v_norm_post_g']
TWIN_OUTPUTS = ['loss', 'grad_x', 'grad_norm_pre_g', 'grad_w_in', 'grad_conv_w', 'grad_w_out', 'grad_norm_post_g', 'delta_norm_pre_g', 'delta_w_in', 'delta_conv_w', 'delta_w_out', 'delta_norm_post_g', 'new_m_norm_pre_g', 'new_m_w_in', 'new_m_conv_w', 'new_m_w_out', 'new_m_norm_post_g', 'new_v_norm_pre_g', 'new_v_w_in', 'new_v_conv_w', 'new_v_w_out', 'new_v_norm_post_g']
TWIN_LEAF_KINDS = {'loss': 'loss', 'grad_x': 'grad_x', 'grad_norm_pre_g': 'grad_w', 'grad_w_in': 'grad_w', 'grad_conv_w': 'grad_w', 'grad_w_out': 'grad_w', 'grad_norm_post_g': 'grad_w', 'delta_norm_pre_g': 'delta_w', 'delta_w_in': 'delta_w', 'delta_conv_w': 'delta_w', 'delta_w_out': 'delta_w', 'delta_norm_post_g': 'delta_w', 'new_m_norm_pre_g': 'new_m', 'new_m_w_in': 'new_m', 'new_m_conv_w': 'new_m', 'new_m_w_out': 'new_m', 'new_m_norm_post_g': 'new_m', 'new_v_norm_pre_g': 'new_v', 'new_v_w_in': 'new_v', 'new_v_conv_w': 'new_v', 'new_v_w_out': 'new_v', 'new_v_norm_post_g': 'new_v'}


def _forward(args):
    return _fwd_reference(*[args[k] for k in FWD_PARAMS])


def _output_shape():
    def fwd():
        inp = _fwd_setup_inputs(0)
        return _fwd_reference(*[inp[k] for k in FWD_PARAMS])
    out = _jax.eval_shape(fwd)
    return out.shape, out.dtype

N_MICROBATCH = 1
ADAM_LR = 0.001
ADAM_B1 = 0.9
ADAM_B2 = 0.999
ADAM_EPS = 1e-08
ADAM_WD = 0.01
ADAM_STEP = 10
PER_EXAMPLE_BATCH_AXIS = {'x': 0, 'loss_target': 0}
SHARED_INPUTS = []
_WEIGHT_DTYPES = {'norm_pre_g': _jnp.float32, 'w_in': _jnp.float32, 'conv_w': _jnp.float32, 'w_out': _jnp.float32, 'norm_post_g': _jnp.float32}
MOMENT_SCALE = {'norm_pre_g': 8.613103e-01, 'w_in': 3.855561e-01, 'conv_w': 7.505597e-01, 'w_out': 3.577796e-01, 'norm_post_g': 6.427942e+01}


def _to_microbatches(a, axis):
    t = _jnp.moveaxis(a, axis, 0)
    t = t.reshape((N_MICROBATCH, t.shape[0] // N_MICROBATCH) + t.shape[1:])
    return _jnp.moveaxis(t, 1, axis + 1)


def setup_inputs(seed: int = 0) -> dict:
    inp = _fwd_setup_inputs(seed)
    key = _jax.random.fold_in(_jax.random.key(seed), 7919)
    shape, _ = _output_shape()
    out = dict(inp)
    out["loss_target"] = _jax.random.normal(_jax.random.fold_in(key, 0), shape, _jnp.float32)
    for i, name in enumerate(TWIN_WEIGHTS):
        w = inp[name].astype(_jnp.float32)
        if MOMENT_SCALE is None:
            s = _jnp.sqrt(_jnp.mean(_jnp.square(w)) + 1e-30)
        else:
            s = MOMENT_SCALE[name]
        km, kv = _jax.random.split(_jax.random.fold_in(key, i + 1))
        out[name] = w
        out["m_" + name] = s * _jax.random.normal(km, w.shape, _jnp.float32)
        out["v_" + name] = (s * s) * _jax.random.uniform(kv, w.shape, _jnp.float32, 0.5, 1.5)
    if N_MICROBATCH > 1:
        for name, axis in PER_EXAMPLE_BATCH_AXIS.items():
            out[name] = _to_microbatches(out[name], axis)
    return {'x': out['x'], 'norm_pre_g': out['norm_pre_g'], 'w_in': out['w_in'], 'conv_w': out['conv_w'], 'w_out': out['w_out'], 'norm_post_g': out['norm_post_g'], 'loss_target': out['loss_target'], 'm_norm_pre_g': out['m_norm_pre_g'], 'm_w_in': out['m_w_in'], 'm_conv_w': out['m_conv_w'], 'm_w_out': out['m_w_out'], 'm_norm_post_g': out['m_norm_post_g'], 'v_norm_pre_g': out['v_norm_pre_g'], 'v_w_in': out['v_w_in'], 'v_conv_w': out['v_conv_w'], 'v_w_out': out['v_w_out'], 'v_norm_post_g': out['v_norm_post_g']}


def _loss(weights, diff, rest, loss_target):
    with _jax.named_scope("forward"):
        args = {**rest, TWIN_DIFF_INPUT: diff, **{k: w.astype(_WEIGHT_DTYPES[k]) for k, w in weights.items()}}
        y = _forward(args)
    with _jax.named_scope("loss_head"):
        err = _jnp.square(y.astype(_jnp.float32) - loss_target)
        return 0.5 * _jnp.sum(_jnp.mean(err, axis=-1)) if err.ndim else 0.5 * err


def _adamw(w, g, m, v):
    m = ADAM_B1 * m + (1.0 - ADAM_B1) * g
    v = ADAM_B2 * v + (1.0 - ADAM_B2) * _jnp.square(g)
    m_hat = m / (1.0 - ADAM_B1 ** ADAM_STEP)
    v_hat = v / (1.0 - ADAM_B2 ** ADAM_STEP)
    delta = -ADAM_LR * (m_hat / (_jnp.sqrt(v_hat) + ADAM_EPS) + ADAM_WD * w)
    return delta, m, v


def reference(x, norm_pre_g, w_in, conv_w, w_out, norm_post_g, loss_target, m_norm_pre_g, m_w_in, m_conv_w, m_w_out, m_norm_post_g, v_norm_pre_g, v_w_in, v_conv_w, v_w_out, v_norm_post_g):
    given = dict(x=x, norm_pre_g=norm_pre_g, w_in=w_in, conv_w=conv_w, w_out=w_out, norm_post_g=norm_post_g, loss_target=loss_target, m_norm_pre_g=m_norm_pre_g, m_w_in=m_w_in, m_conv_w=m_conv_w, m_w_out=m_w_out, m_norm_post_g=m_norm_post_g, v_norm_pre_g=v_norm_pre_g, v_w_in=v_w_in, v_conv_w=v_conv_w, v_w_out=v_w_out, v_norm_post_g=v_norm_post_g)
    weights = {n: given[n] for n in TWIN_WEIGHTS}
    shared = {n: given[n] for n in SHARED_INPUTS}
    per_example = {n: given[n] for n in ['x']}
    grad_fn = _jax.value_and_grad(_loss, argnums=(0, 1))

    def one_microbatch(ex, loss_target):
        ex = dict(ex)
        diff = ex.pop(TWIN_DIFF_INPUT)
        return grad_fn(weights, diff, {**shared, **ex}, loss_target)

    if N_MICROBATCH == 1:
        loss, (grad_w, grad_x) = one_microbatch(per_example, given["loss_target"])
    else:
        def body(carry, xs):
            loss_sum, grad_sum = carry
            l_k, (gw_k, gx_k) = one_microbatch(xs[0], xs[1])
            with _jax.named_scope("update"):
                return (loss_sum + l_k, _jax.tree.map(_jnp.add, grad_sum, gw_k)), gx_k

        init = (_jnp.zeros((), _jnp.float32), _jax.tree.map(_jnp.zeros_like, weights))
        (loss, grad_w), grad_x = _jax.lax.scan(body, init, (per_example, given["loss_target"]))
    with _jax.named_scope("update"):
        delta_w, new_m, new_v = {}, {}, {}
        for n in TWIN_WEIGHTS:
            delta_w[n], new_m[n], new_v[n] = _adamw(weights[n], grad_w[n], given["m_" + n], given["v_" + n])
    return (loss, grad_x, *[grad_w[n] for n in TWIN_WEIGHTS], *[delta_w[n] for n in TWIN_WEIGHTS],
            *[new_m[n] for n in TWIN_WEIGHTS], *[new_v[n] for n in TWIN_WEIGHTS])
```

```python
import functools

import jax
import jax.numpy as jnp
import numpy as np
from jax import lax
from jax.experimental import pallas as pl
from jax.experimental.pallas import tpu as pltpu

F32 = jnp.float32
BF16 = jnp.bfloat16

D_MODEL = 1024
HEAD_DIM = 64
ATTN_W = 768
CONV_W = 256
IN_W = 4096
REST_W = IN_W - 3 * ATTN_W
BLK = 128
N_DEV = 8
SHARD_IN = IN_W // N_DEV
SHARD_OUT = D_MODEL // N_DEV
DILATIONS = (1, 4, 16)
ROPE_THETA = 10000.0
NORM_EPS = 1e-6
NEG = -1e30

ADAM_LR = 0.001
ADAM_B1 = 0.9
ADAM_B2 = 0.999
ADAM_EPS = 1e-08
ADAM_WD = 0.01
ADAM_STEP = 10

VMEM_LIMIT = 56 * 1024 * 1024
MESH = pl.DeviceIdType.MESH


def _params(n_grid):
    return pltpu.CompilerParams(dimension_semantics=("arbitrary",) * n_grid, vmem_limit_bytes=VMEM_LIMIT)


def _resident(shape):
    zeros = (0,) * len(shape)
    return pl.BlockSpec(shape, lambda *_: zeros, pipeline_mode=pl.Buffered(1))


def _sigmoid(a):
    return 1.0 / (1.0 + jnp.exp(-a))


def _swap_halves(t, first_half):
    return jnp.where(first_half, pltpu.roll(t, BLK - 32, 1), pltpu.roll(t, 32, 1))


def _rope_tables(seq):
    half = HEAD_DIM // 2
    inv_freq = ROPE_THETA ** (-jnp.arange(half, dtype=F32) * 2.0 / HEAD_DIM)
    ang = jnp.arange(seq).astype(F32)[:, None] * inv_freq[None, :]
    cos, sin = jnp.cos(ang), jnp.sin(ang)
    return jnp.concatenate([cos] * 4, axis=1), jnp.concatenate([-sin, sin, -sin, sin], axis=1)


def _fwd_in(x, g_pre, w_in_g, cos_t, sin_t, tm=256):
    seq = x.shape[0]

    def body(x_ref, g_ref, w_ref, cos_ref, sin_ref, q_ref, k_ref, v_ref, zr_ref, h_ref, z_scr):
        xv = x_ref[...]
        r = lax.rsqrt(jnp.mean(xv * xv, axis=-1, keepdims=True) + NORM_EPS)
        h = ((xv * r) * g_ref[...]).astype(BF16)
        h_ref[...] = h
        for j in range(N_DEV):
            z_scr[:, j * SHARD_IN:(j + 1) * SHARD_IN] = jnp.dot(h, w_ref[j], preferred_element_type=F32)
        cos, sin = cos_ref[...], sin_ref[...]
        first_half = (lax.broadcasted_iota(jnp.int32, (tm, BLK), 1) & 32) == 0

        def rope(t):
            return t * cos + _swap_halves(t, first_half) * sin

        for c in range(ATTN_W // BLK):
            cols = slice(c * BLK, (c + 1) * BLK)
            q_ref[:, cols] = (rope(z_scr[:, c * BLK:(c + 1) * BLK]) * HEAD_DIM ** -0.5).astype(BF16)
            k_ref[:, cols] = rope(z_scr[:, ATTN_W + c * BLK:ATTN_W + (c + 1) * BLK]).astype(BF16)
        v_ref[...] = z_scr[:, 2 * ATTN_W:3 * ATTN_W].astype(BF16)
        zr_ref[...] = z_scr[:, 3 * ATTN_W:]

    row = lambda w: pl.BlockSpec((tm, w), lambda i: (i, 0))
    return pl.pallas_call(
        body, name="fwd_in", grid=(seq // tm,),
        in_specs=[row(D_MODEL), _resident((1, D_MODEL)), _resident((N_DEV, D_MODEL, SHARD_IN)), row(BLK), row(BLK)],
        out_specs=[row(ATTN_W), row(ATTN_W), row(ATTN_W), row(REST_W), row(D_MODEL)],
        out_shape=[jax.ShapeDtypeStruct((seq, ATTN_W), BF16)] * 3
        + [jax.ShapeDtypeStruct((seq, REST_W), F32), jax.ShapeDtypeStruct((seq, D_MODEL), BF16)],
        scratch_shapes=[pltpu.VMEM((tm, IN_W), F32)],
        compiler_params=_params(1),
    )(x, g_pre.reshape(1, D_MODEL), w_in_g, cos_t, sin_t)


def _band_mask(first_block):
    qi = lax.broadcasted_iota(jnp.int32, (2 * BLK, 2 * BLK), 0) & (BLK - 1)
    kj = lax.broadcasted_iota(jnp.int32, (2 * BLK, 2 * BLK), 1)
    valid = (kj >= qi) & (kj <= qi + BLK)
    if first_block is not None:
        valid = valid & ((kj >= BLK) | jnp.logical_not(first_block))
    return valid


def _stack_heads(t, head0):
    keep0 = head0.astype(F32).astype(BF16)
    return jnp.concatenate([t * keep0, t * (1 - keep0)], axis=0)


def _unstack_heads(t2, head0):
    return jnp.where(head0, t2[:BLK], t2[BLK:])


def _rows_per_head(a, head0):
    b = pltpu.roll(a, HEAD_DIM, 1)
    rows = jnp.concatenate([jnp.where(head0, a, b), jnp.where(head0, b, a)], axis=0)
    return jnp.concatenate([rows, rows], axis=1)


def _attn_specs(length, dil):
    tb = min(4, length // BLK)
    tile = pl.BlockSpec((tb * BLK, BLK), lambda c, t: (t, c))
    prev = pl.BlockSpec((BLK, BLK), lambda c, t: (jnp.maximum(t * tb - 1, 0), c))
    grid = (dil * ATTN_W // BLK, length // (tb * BLK))
    return tb, tile, prev, grid


def _attn_fwd(q, k, v, dil):
    seq = q.shape[0]
    length = seq // dil
    tb, tile, prev, grid = _attn_specs(length, dil)

    def body(q_ref, kc_ref, kp_ref, vc_ref, vp_ref, o_ref, lse_ref, kcat, vcat):
        t = pl.program_id(1)
        kcat[0:BLK] = kp_ref[...]
        kcat[BLK:] = kc_ref[...]
        vcat[0:BLK] = vp_ref[...]
        vcat[BLK:] = vc_ref[...]
        head0 = lax.broadcasted_iota(jnp.int32, (BLK, BLK), 1) < HEAD_DIM
        for j in range(tb):
            valid = _band_mask(t == 0 if j == 0 else None)
            q2 = _stack_heads(q_ref[j * BLK:(j + 1) * BLK, :], head0)
            kk = kcat[j * BLK:(j + 2) * BLK, :]
            s = lax.dot_general(q2, kk, (((1,), (1,)), ((), ())), preferred_element_type=F32)
            s = jnp.where(valid, s, NEG)
            m = jnp.max(s, axis=1, keepdims=True)
            p = jnp.exp(s - m)
            den = jnp.sum(p, axis=1, keepdims=True)
            o2 = jnp.dot(p.astype(BF16), vcat[j * BLK:(j + 2) * BLK, :], preferred_element_type=F32) / den
            lse2 = jnp.broadcast_to(m + jnp.log(den), (2 * BLK, BLK))
            o_ref[j * BLK:(j + 1) * BLK, :] = _unstack_heads(o2, head0)
            lse_ref[j * BLK:(j + 1) * BLK, :] = _unstack_heads(lse2, head0)

    shape = (length, dil * ATTN_W)
    o, lse = pl.pallas_call(
        body, name=f"attn_fwd_d{dil}", grid=grid,
        in_specs=[tile, tile, prev, tile, prev], out_specs=[tile, tile],
        out_shape=[jax.ShapeDtypeStruct(shape, F32)] * 2,
        scratch_shapes=[pltpu.VMEM(((tb + 1) * BLK, BLK), BF16)] * 2,
        compiler_params=_params(2),
    )(q.reshape(shape), k.reshape(shape), k.reshape(shape), v.reshape(shape), v.reshape(shape))
    return o.reshape(seq, ATTN_W), lse.reshape(seq, ATTN_W)


def _attn_bwd(q, k, v, do, lse, delta, dil):
    seq = q.shape[0]
    length = seq // dil
    tb, tile, prev, grid = _attn_specs(length, dil)
    whole = pl.BlockSpec((length, BLK), lambda c, t: (0, c))

    def body(q_ref, do_ref, lse_ref, dl_ref, kc_ref, kp_ref, vc_ref, vp_ref, dq_ref, dk_ref, dv_ref, kcat, vcat):
        t = pl.program_id(1)
        kcat[0:BLK] = kp_ref[...]
        kcat[BLK:] = kc_ref[...]
        vcat[0:BLK] = vp_ref[...]
        vcat[BLK:] = vc_ref[...]
        head0 = lax.broadcasted_iota(jnp.int32, (BLK, BLK), 1) < HEAD_DIM
        for j in range(tb):
            rows = slice(j * BLK, (j + 1) * BLK)
            valid = _band_mask(t == 0 if j == 0 else None)
            q2 = _stack_heads(q_ref[rows, :], head0)
            do2 = _stack_heads(do_ref[rows, :], head0)
            kk = kcat[j * BLK:(j + 2) * BLK, :]
            vv = vcat[j * BLK:(j + 2) * BLK, :]
            s = lax.dot_general(q2, kk, (((1,), (1,)), ((), ())), preferred_element_type=F32)
            p = jnp.where(valid, jnp.exp(s - _rows_per_head(lse_ref[rows, :], head0)), 0.0)
            dp = lax.dot_general(do2, vv, (((1,), (1,)), ((), ())), preferred_element_type=F32)
            ds = (p * (dp - _rows_per_head(dl_ref[rows, :], head0))).astype(BF16)
            dq2 = jnp.dot(ds, kk, preferred_element_type=F32)
            dq_ref[rows, :] = _unstack_heads(dq2, head0) * HEAD_DIM ** -0.5
            dk2 = lax.dot_general(ds, q2, (((0,), (0,)), ((), ())), preferred_element_type=F32)
            dv2 = lax.dot_general(p.astype(BF16), do2, (((0,), (0,)), ((), ())), preferred_element_type=F32)
            own = pl.ds(pl.multiple_of((t * tb + j) * BLK, BLK), BLK)
            dk_ref[own, :] = dk2[BLK:]
            dv_ref[own, :] = dv2[BLK:]

            def add_to_previous(j=j, dk2=dk2, dv2=dv2):
                before = pl.ds(pl.multiple_of((t * tb + j - 1) * BLK, BLK), BLK)
                dk_ref[before, :] += dk2[:BLK]
                dv_ref[before, :] += dv2[:BLK]

            if j == 0:
                pl.when(t > 0)(add_to_previous)
            else:
                add_to_previous()

    shape = (length, dil * ATTN_W)
    view = lambda a: a.reshape(shape)
    dq, dk, dv = pl.pallas_call(
        body, name=f"attn_bwd_d{dil}", grid=grid,
        in_specs=[tile, tile, tile, tile, tile, prev, tile, prev], out_specs=[tile, whole, whole],
        out_shape=[jax.ShapeDtypeStruct(shape, F32)] * 3,
        scratch_shapes=[pltpu.VMEM(((tb + 1) * BLK, BLK), BF16)] * 2,
        compiler_params=_params(2),
    )(view(q), view(do), view(lse), view(delta), view(k), view(k), view(v), view(v))
    return dq.reshape(seq, ATTN_W), dk.reshape(seq, ATTN_W), dv.reshape(seq, ATTN_W)


def _conv_taps(u, before8, tm):
    row = lax.broadcasted_iota(jnp.int32, u.shape, 0)
    last, last2 = before8[7:8, :], before8[6:7, :]
    u1 = jnp.where(row == 0, last, pltpu.roll(u, 1, 0))
    u2 = jnp.where(row == 0, last2, jnp.where(row == 1, last, pltpu.roll(u, 2, 0)))
    return u1, u2


def _attn_combine(o_parts, lse_parts, zr, conv_w, tm=256):
    seq = zr.shape[0]
    a0, h0, b0, c0, g0 = 0, ATTN_W, ATTN_W + CONV_W, ATTN_W + 2 * CONV_W, ATTN_W + 3 * CONV_W

    def body(o1, o2, o3, l1, l2, l3, zr_ref, zp_ref, w_ref, mixed_ref, o_ref, lse_ref):
        i = pl.program_id(0)
        la, lb, lc = l1[...], l2[...], l3[...]
        top = jnp.maximum(jnp.maximum(la, lb), lc)
        ea, eb, ec = jnp.exp(la - top), jnp.exp(lb - top), jnp.exp(lc - top)
        den = ea + eb + ec
        o = (ea / den) * o1[...] + (eb / den) * o2[...] + (ec / den) * o3[...]
        o_ref[...] = o
        lse_ref[...] = top + jnp.log(den)
        ga = zr_ref[:, a0:h0]
        mixed_ref[:, 0:ATTN_W] = (o * (ga * _sigmoid(ga))).astype(BF16)
        u = zr_ref[:, c0:g0] * zr_ref[:, h0:b0]
        before = jnp.where(i > 0, zp_ref[:, c0:g0] * zp_ref[:, h0:b0], 0.0)
        u1, u2 = _conv_taps(u, before, tm)
        y = u2 * w_ref[0:1, :] + u1 * w_ref[1:2, :] + u * w_ref[2:3, :]
        gc = zr_ref[:, g0:]
        mixed_ref[:, ATTN_W:] = ((zr_ref[:, b0:c0] * y) * (gc * _sigmoid(gc))).astype(BF16)

    row = lambda w: pl.BlockSpec((tm, w), lambda i: (i, 0))
    before8 = pl.BlockSpec((8, REST_W), lambda i: (jnp.maximum(i * (tm // 8) - 1, 0), 0))
    return pl.pallas_call(
        body, name="attn_combine", grid=(seq // tm,),
        in_specs=[row(ATTN_W)] * 6 + [row(REST_W), before8, _resident((3, CONV_W))],
        out_specs=[row(D_MODEL), row(ATTN_W), row(ATTN_W)],
        out_shape=[jax.ShapeDtypeStruct((seq, D_MODEL), BF16), jax.ShapeDtypeStruct((seq, ATTN_W), F32),
                   jax.ShapeDtypeStruct((seq, ATTN_W), F32)],
        compiler_params=_params(1),
    )(*o_parts, *lse_parts, zr, zr, conv_w)


def _out_loss_bwd(mixed, w_out_g, x, target, g_post, tm=256):
    seq = x.shape[0]

    def body(mx_ref, w_ref, x_ref, t_ref, g_ref, dout_ref, dy_ref, dmx_ref, dw_ref, st_ref):
        i = pl.program_id(0)
        mx = mx_ref[...]
        y = jnp.dot(mx, w_ref[...], preferred_element_type=F32)
        r = lax.rsqrt(jnp.mean(y * y, axis=-1, keepdims=True) + NORM_EPS)
        yhat = y * r
        g = g_ref[...]
        err = (x_ref[...] + yhat * g) - t_ref[...]
        dn = err * (1.0 / D_MODEL)
        dout_ref[...] = dn
        tg = dn * g
        dy = (r * (tg - yhat * jnp.mean(tg * yhat, axis=-1, keepdims=True))).astype(BF16)
        dy_ref[...] = dy
        dmx_ref[...] = lax.dot_general(dy, w_ref[...], (((1,), (1,)), ((), ())), preferred_element_type=F32)
        dw = lax.dot_general(mx, dy, (((0,), (0,)), ((), ())), preferred_element_type=F32)
        gsum = jnp.sum(dn * yhat, axis=0, keepdims=True)
        lsum = jnp.broadcast_to(0.5 / D_MODEL * jnp.sum(err * err), (1, D_MODEL))

        @pl.when(i == 0)
        def _():
            dw_ref[...] = dw
            st_ref[...] = jnp.zeros_like(st_ref)
            st_ref[0:1, :] = gsum
            st_ref[1:2, :] = lsum

        @pl.when(i > 0)
        def _():
            dw_ref[...] += dw
            st_ref[0:1, :] += gsum
            st_ref[1:2, :] += lsum

    row = lambda w: pl.BlockSpec((tm, w), lambda i: (i, 0))
    return pl.pallas_call(
        body, name="out_loss_bwd", grid=(seq // tm,),
        in_specs=[row(D_MODEL), _resident((D_MODEL, D_MODEL)), row(D_MODEL), row(D_MODEL), _resident((1, D_MODEL))],
        out_specs=[row(D_MODEL), row(D_MODEL), row(D_MODEL), pl.BlockSpec((D_MODEL, D_MODEL), lambda i: (0, 0)),
                   pl.BlockSpec((8, D_MODEL), lambda i: (0, 0))],
        out_shape=[jax.ShapeDtypeStruct((seq, D_MODEL), F32), jax.ShapeDtypeStruct((seq, D_MODEL), BF16),
                   jax.ShapeDtypeStruct((seq, D_MODEL), F32), jax.ShapeDtypeStruct((D_MODEL, D_MODEL), F32),
                   jax.ShapeDtypeStruct((8, D_MODEL), F32)],
        compiler_params=_params(1),
    )(mixed, w_out_g, x, target, g_post.reshape(1, D_MODEL))


def _head_sum(prod, same_head):
    hi = prod.astype(BF16)
    lo = (prod - hi.astype(F32)).astype(BF16)
    return (jnp.dot(hi, same_head, preferred_element_type=F32) + jnp.dot(lo, same_head, preferred_element_type=F32))


def _gate_bwd(dmixed, zr, o, conv_w, tm=256):
    seq = zr.shape[0]
    n_tiles = seq // tm
    a0, h0, b0, c0, g0 = 0, ATTN_W, ATTN_W + CONV_W, ATTN_W + 2 * CONV_W, ATTN_W + 3 * CONV_W

    def body(dm_ref, dmn_ref, zr_ref, zp_ref, zn_ref, o_ref, w_ref, do_ref, dl_ref, dz_ref, dw_ref):
        i = pl.program_id(0)
        ga = zr_ref[:, a0:h0]
        sg = _sigmoid(ga)
        dattn = dm_ref[:, 0:ATTN_W]
        ov = o_ref[...]
        do = dattn * (ga * sg)
        do_ref[...] = do.astype(BF16)
        dz_ref[:, a0:h0] = (dattn * ov * (sg * (1.0 + ga * (1.0 - sg)))).astype(BF16)
        li = lax.broadcasted_iota(jnp.int32, (BLK, BLK), 0) // HEAD_DIM
        lj = lax.broadcasted_iota(jnp.int32, (BLK, BLK), 1) // HEAD_DIM
        same_head = (li == lj).astype(BF16)
        prod = do * ov
        for c in range(ATTN_W // BLK):
            cols = slice(c * BLK, (c + 1) * BLK)
            dl_ref[:, cols] = _head_sum(prod[:, cols], same_head)

        ch, cb, cc, gc = zr_ref[:, h0:b0], zr_ref[:, b0:c0], zr_ref[:, c0:g0], zr_ref[:, g0:]
        u = cc * ch
        before = jnp.where(i > 0, zp_ref[:, c0:g0] * zp_ref[:, h0:b0], 0.0)
        u1, u2 = _conv_taps(u, before, tm)
        w0, w1, w2 = w_ref[0:1, :], w_ref[1:2, :], w_ref[2:3, :]
        y = u2 * w0 + u1 * w1 + u * w2
        sc = _sigmoid(gc)
        silu_c = gc * sc
        dconv = dm_ref[:, ATTN_W:]
        dz_ref[:, b0:c0] = (dconv * y * silu_c).astype(BF16)
        dz_ref[:, g0:] = (dconv * (cb * y) * (sc * (1.0 + gc * (1.0 - sc)))).astype(BF16)
        dy = dconv * cb * silu_c
        gn = zn_ref[:, g0:]
        after = jnp.where(i < n_tiles - 1, dmn_ref[:, ATTN_W:] * zn_ref[:, b0:c0] * (gn * _sigmoid(gn)), 0.0)
        row = lax.broadcasted_iota(jnp.int32, dy.shape, 0)
        nxt, nxt2 = after[0:1, :], after[1:2, :]
        dy1 = jnp.where(row == tm - 1, nxt, pltpu.roll(dy, tm - 1, 0))
        dy2 = jnp.where(row == tm - 1, nxt2, jnp.where(row == tm - 2, nxt, pltpu.roll(dy, tm - 2, 0)))
        du = dy * w2 + dy1 * w1 + dy2 * w0
        dz_ref[:, c0:g0] = (du * ch).astype(BF16)
        dz_ref[:, h0:b0] = (du * cc).astype(BF16)
        dws = [jnp.sum(dy * u2, axis=0, keepdims=True), jnp.sum(dy * u1, axis=0, keepdims=True),
               jnp.sum(dy * u, axis=0, keepdims=True)]

        @pl.when(i == 0)
        def _():
            dw_ref[...] = jnp.zeros_like(dw_ref)

        for n, part in enumerate(dws):
            dw_ref[n:n + 1, :] += part

    row_spec = lambda w: pl.BlockSpec((tm, w), lambda i: (i, 0))
    before8 = pl.BlockSpec((8, REST_W), lambda i: (jnp.maximum(i * (tm // 8) - 1, 0), 0))
    after8 = lambda w: pl.BlockSpec((8, w), lambda i: (jnp.minimum((i + 1) * (tm // 8), seq // 8 - 1), 0))
    return pl.pallas_call(
        body, name="gate_bwd", grid=(n_tiles,),
        in_specs=[row_spec(D_MODEL), after8(D_MODEL), row_spec(REST_W), before8, after8(REST_W), row_spec(ATTN_W),
                  _resident((3, CONV_W))],
        out_specs=[row_spec(ATTN_W), row_spec(ATTN_W), row_spec(REST_W), pl.BlockSpec((8, CONV_W), lambda i: (0, 0))],
        out_shape=[jax.ShapeDtypeStruct((seq, ATTN_W), BF16), jax.ShapeDtypeStruct((seq, ATTN_W), F32),
                   jax.ShapeDtypeStruct((seq, REST_W), BF16), jax.ShapeDtypeStruct((8, CONV_W), F32)],
        compiler_params=_params(1),
    )(dmixed, dmixed, zr, zr, zr, o, conv_w)


def _in_bwd(dqs, dks, dvs, dzr, cos_t, sin_t, x, d_out, g_pre, w_in_g, tm=256):
    seq = x.shape[0]

    def body(q1, q2, q3, k1, k2, k3, v1, v2, v3, dzr_ref, cos_ref, sin_ref, x_ref, dout_ref, g_ref, w_ref,
             dz_ref, gx_ref, st_ref):
        i = pl.program_id(0)
        cos, sin = cos_ref[...], sin_ref[...]
        first_half = (lax.broadcasted_iota(jnp.int32, (tm, BLK), 1) & 32) == 0

        def unrope(t):
            return t * cos - _swap_halves(t, first_half) * sin

        for c in range(ATTN_W // BLK):
            cols = slice(c * BLK, (c + 1) * BLK)
            dz_ref[:, c * BLK:(c + 1) * BLK] = unrope(q1[:, cols] + q2[:, cols] + q3[:, cols]).astype(BF16)
            dz_ref[:, ATTN_W + c * BLK:ATTN_W + (c + 1) * BLK] = unrope(
                k1[:, cols] + k2[:, cols] + k3[:, cols]).astype(BF16)
        dz_ref[:, 2 * ATTN_W:3 * ATTN_W] = (v1[...] + v2[...] + v3[...]).astype(BF16)
        dz_ref[:, 3 * ATTN_W:] = dzr_ref[...]
        dh = jnp.zeros((tm, D_MODEL), F32)
        for j in range(N_DEV):
            dh += lax.dot_general(dz_ref[:, j * SHARD_IN:(j + 1) * SHARD_IN], w_ref[j], (((1,), (1,)), ((), ())),
                                  preferred_element_type=F32)
        xv = x_ref[...]
        r = lax.rsqrt(jnp.mean(xv * xv, axis=-1, keepdims=True) + NORM_EPS)
        xhat = xv * r
        tg = dh * g_ref[...]
        gx_ref[...] = dout_ref[...] + r * (tg - xhat * jnp.mean(tg * xhat, axis=-1, keepdims=True))
        gsum = jnp.sum(dh * xhat, axis=0, keepdims=True)

        @pl.when(i == 0)
        def _():
            st_ref[...] = jnp.zeros_like(st_ref)

        st_ref[0:1, :] += gsum

    row = lambda w: pl.BlockSpec((tm, w), lambda i: (i, 0))
    return pl.pallas_call(
        body, name="in_bwd", grid=(seq // tm,),
        in_specs=[row(ATTN_W)] * 9 + [row(REST_W), row(BLK), row(BLK), row(D_MODEL), row(D_MODEL),
                                      _resident((1, D_MODEL)), _resident((N_DEV, D_MODEL, SHARD_IN))],
        out_specs=[row(IN_W), row(D_MODEL), pl.BlockSpec((8, D_MODEL), lambda i: (0, 0))],
        out_shape=[jax.ShapeDtypeStruct((seq, IN_W), BF16), jax.ShapeDtypeStruct((seq, D_MODEL), F32),
                   jax.ShapeDtypeStruct((8, D_MODEL), F32)],
        compiler_params=_params(1),
    )(*dqs, *dks, *dvs, dzr, cos_t, sin_t, x, d_out, g_pre.reshape(1, D_MODEL), w_in_g)


def _dw_in(h, dz, tk=512):
    seq = h.shape[0]

    def body(h_ref, dz_ref, dw_ref):
        dw = lax.dot_general(h_ref[...], dz_ref[...], (((0,), (0,)), ((), ())), preferred_element_type=F32)

        @pl.when(pl.program_id(1) == 0)
        def _():
            dw_ref[0] = dw

        @pl.when(pl.program_id(1) > 0)
        def _():
            dw_ref[0] += dw

    return pl.pallas_call(
        body, name="dw_in", grid=(N_DEV, seq // tk),
        in_specs=[pl.BlockSpec((tk, D_MODEL), lambda j, k: (k, 0)), pl.BlockSpec((tk, SHARD_IN), lambda j, k: (k, j))],
        out_specs=pl.BlockSpec((1, D_MODEL, SHARD_IN), lambda j, k: (j, 0, 0)),
        out_shape=jax.ShapeDtypeStruct((N_DEV, D_MODEL, SHARD_IN), F32),
        compiler_params=_params(2),
    )(h, dz)


def _local_step(x, target, g_pre, g_post, w_in_g, w_out_g, conv_w):
    seq = x.shape[0]
    cos_t, sin_t = _rope_tables(seq)
    q, k, v, zr, h = _fwd_in(x, g_pre, w_in_g, cos_t, sin_t)
    parts = [_attn_fwd(q, k, v, dil) for dil in DILATIONS]
    mixed, o, lse = _attn_combine([p[0] for p in parts], [p[1] for p in parts], zr, conv_w)
    d_out, dy, dmixed, dw_out, st_post = _out_loss_bwd(mixed, w_out_g, x, target, g_post)
    do, delta, dzr, dconv = _gate_bwd(dmixed, zr, o, conv_w)
    grads = [_attn_bwd(q, k, v, do, lse, delta, dil) for dil in DILATIONS]
    dz, grad_x, st_pre = _in_bwd([g[0] for g in grads], [g[1] for g in grads], [g[2] for g in grads], dzr,
                                 cos_t, sin_t, x, d_out, g_pre, w_in_g)
    dw_in = _dw_in(h, dz)
    conv_rows = jnp.pad(dconv[0:3], ((0, 0), (0, D_MODEL - CONV_W)))
    small = jnp.concatenate([st_pre[0:1], st_post[0:2], conv_rows, jnp.zeros((2, D_MODEL), F32)], axis=0)
    return grad_x, dw_in, dw_out, small


def _coords():
    return lax.axis_index("x"), lax.axis_index("y"), lax.axis_index("c")


def _peer(k):
    x, y, c = _coords()
    px = 1 - x if k & 4 else x
    py = 1 - y if k & 2 else y
    pc = 1 - c if k & 1 else c
    return (px, py, pc), 4 * px + 2 * py + pc


HBM_SPEC = pl.BlockSpec(memory_space=pltpu.HBM)
VMEM_SPEC = pl.BlockSpec(memory_space=pltpu.VMEM)


def _ag_weights(w_in, w_out, conv_w):
    def body(win_ref, wout_ref, cw_ref, gin_ref, gout_ref, gcw_ref, win_bf, wout_bf, cw_pad, send_sems, recv_sems,
             local_sems):
        x, y, c = _coords()
        me = 4 * x + 2 * y + c
        win_bf[...] = win_ref[...].astype(BF16)
        wout_bf[...] = wout_ref[...].astype(BF16)
        cw_pad[...] = jnp.zeros_like(cw_pad)
        cw_pad[0:3, 0:CONV_W // N_DEV] = cw_ref[...]
        pairs = [(win_bf, gin_ref), (wout_bf, gout_ref), (cw_pad, gcw_ref)]
        local = [pltpu.make_async_copy(src, dst.at[me], local_sems.at[n]) for n, (src, dst) in enumerate(pairs)]
        for cp in local:
            cp.start()
        sends = []
        for k in range(1, N_DEV):
            peer, _ = _peer(k)
            for n, (src, dst) in enumerate(pairs):
                sends.append(pltpu.make_async_remote_copy(src_ref=src, dst_ref=dst.at[me], send_sem=send_sems.at[k - 1, n],
                                                          recv_sem=recv_sems.at[k - 1, n], device_id=peer,
                                                          device_id_type=MESH))
        for cp in sends:
            cp.start()
        for k in range(1, N_DEV):
            peer, peer_idx = _peer(k)
            for n, (src, dst) in enumerate(pairs):
                pltpu.make_async_remote_copy(src_ref=src, dst_ref=dst.at[peer_idx], send_sem=send_sems.at[k - 1, n],
                                             recv_sem=recv_sems.at[k - 1, n], device_id=peer,
                                             device_id_type=MESH).wait_recv()
        for cp in sends:
            cp.wait_send()
        for cp in local:
            cp.wait()

    return pl.pallas_call(
        body, name="ag_weights",
        in_specs=[VMEM_SPEC, VMEM_SPEC, VMEM_SPEC], out_specs=[HBM_SPEC, HBM_SPEC, HBM_SPEC],
        out_shape=[jax.ShapeDtypeStruct((N_DEV, D_MODEL, SHARD_IN), BF16),
                   jax.ShapeDtypeStruct((N_DEV, SHARD_OUT, D_MODEL), BF16),
                   jax.ShapeDtypeStruct((N_DEV, 8, BLK), F32)],
        scratch_shapes=[pltpu.VMEM((D_MODEL, SHARD_IN), BF16), pltpu.VMEM((SHARD_OUT, D_MODEL), BF16),
                        pltpu.VMEM((8, BLK), F32), pltpu.SemaphoreType.DMA((N_DEV - 1, 3)),
                        pltpu.SemaphoreType.DMA((N_DEV - 1, 3)), pltpu.SemaphoreType.DMA((3,))],
        compiler_params=pltpu.CompilerParams(vmem_limit_bytes=VMEM_LIMIT),
    )(w_in, w_out, conv_w)


def _rs_grads(dw_in, dw_out, small):
    def body(din_ref, dout_ref, sm_ref, rin_ref, rout_ref, rsm_ref, send_sems, recv_sems, local_sems):
        x, y, c = _coords()
        me = 4 * x + 2 * y + c
        local = [pltpu.make_async_copy(din_ref.at[me], rin_ref.at[me], local_sems.at[0]),
                 pltpu.make_async_copy(dout_ref.at[me], rout_ref.at[me], local_sems.at[1]),
                 pltpu.make_async_copy(sm_ref, rsm_ref.at[me], local_sems.at[2])]
        for cp in local:
            cp.start()
        sends = []
        for k in range(1, N_DEV):
            peer, peer_idx = _peer(k)
            srcs = [din_ref.at[peer_idx], dout_ref.at[peer_idx], sm_ref]
            for n, (src, dst) in enumerate(zip(srcs, [rin_ref, rout_ref, rsm_ref])):
                sends.append(pltpu.make_async_remote_copy(src_ref=src, dst_ref=dst.at[me], send_sem=send_sems.at[k - 1, n],
                                                          recv_sem=recv_sems.at[k - 1, n], device_id=peer,
                                                          device_id_type=MESH))
        for cp in sends:
            cp.start()
        for k in range(1, N_DEV):
            peer, peer_idx = _peer(k)
            srcs = [din_ref.at[me], dout_ref.at[me], sm_ref]
            for n, (src, dst) in enumerate(zip(srcs, [rin_ref, rout_ref, rsm_ref])):
                pltpu.make_async_remote_copy(src_ref=src, dst_ref=dst.at[peer_idx], send_sem=send_sems.at[k - 1, n],
                                             recv_sem=recv_sems.at[k - 1, n], device_id=peer,
                                             device_id_type=MESH).wait_recv()
        for cp in sends:
            cp.wait_send()
        for cp in local:
            cp.wait()

    return pl.pallas_call(
        body, name="rs_grads",
        in_specs=[HBM_SPEC, HBM_SPEC, HBM_SPEC], out_specs=[HBM_SPEC, HBM_SPEC, HBM_SPEC],
        out_shape=[jax.ShapeDtypeStruct((N_DEV, D_MODEL, SHARD_IN), F32),
                   jax.ShapeDtypeStruct((N_DEV, SHARD_OUT, D_MODEL), F32),
                   jax.ShapeDtypeStruct((N_DEV, 8, D_MODEL), F32)],
        scratch_shapes=[pltpu.SemaphoreType.DMA((N_DEV - 1, 3)), pltpu.SemaphoreType.DMA((N_DEV - 1, 3)),
                        pltpu.SemaphoreType.DMA((3,))],
    )(dw_in, dw_out, small)


def _adamw_math(w, g, m, v):
    m = ADAM_B1 * m + (1.0 - ADAM_B1) * g
    v = ADAM_B2 * v + (1.0 - ADAM_B2) * (g * g)
    m_hat = m / (1.0 - ADAM_B1 ** ADAM_STEP)
    v_hat = v / (1.0 - ADAM_B2 ** ADAM_STEP)
    delta = -ADAM_LR * (m_hat / (jnp.sqrt(v_hat) + ADAM_EPS) + ADAM_WD * w)
    return delta, m, v


def _sum_slabs(ref):
    total = ref[0]
    for s in range(1, N_DEV):
        total = total + ref[s]
    return total


def _adamw_slabs(parts, w, m, v, name, tr):
    rows, cols = w.shape

    def body(p_ref, w_ref, m_ref, v_ref, g_ref, d_ref, nm_ref, nv_ref):
        g = _sum_slabs(p_ref)
        g_ref[...] = g
        d_ref[...], nm_ref[...], nv_ref[...] = _adamw_math(w_ref[...], g, m_ref[...], v_ref[...])

    tile = pl.BlockSpec((tr, cols), lambda i: (i, 0))
    return pl.pallas_call(
        body, name=name, grid=(rows // tr,),
        in_specs=[pl.BlockSpec((N_DEV, tr, cols), lambda i: (0, i, 0)), tile, tile, tile], out_specs=[tile] * 4,
        out_shape=[jax.ShapeDtypeStruct((rows, cols), F32)] * 4,
        compiler_params=_params(1),
    )(parts, w, m, v)


def _sum_small(parts):
    def body(p_ref, out_ref):
        out_ref[...] = _sum_slabs(p_ref)

    return pl.pallas_call(body, name="sum_small", out_shape=jax.ShapeDtypeStruct(parts.shape[1:], F32))(parts)


def _adamw_whole(g, w, m, v, name):
    def body(g_ref, w_ref, m_ref, v_ref, d_ref, nm_ref, nv_ref):
        d_ref[...], nm_ref[...], nv_ref[...] = _adamw_math(w_ref[...], g_ref[...], m_ref[...], v_ref[...])

    return pl.pallas_call(body, name=name, out_shape=[jax.ShapeDtypeStruct(w.shape, F32)] * 3)(g, w, m, v)


def kernel(x, norm_pre_g, w_in, conv_w, w_out, norm_post_g, loss_target, m_norm_pre_g, m_w_in, m_conv_w, m_w_out,
           m_norm_post_g, v_norm_pre_g, v_w_in, v_conv_w, v_w_out, v_norm_post_g):
    n_conv = CONV_W // N_DEV
    w_in_g, w_out_g, conv_g = _ag_weights(w_in, w_out, conv_w)
    conv_full = conv_g[:, 0:3, 0:n_conv].transpose(1, 0, 2).reshape(3, CONV_W)
    grad_x, dw_in, dw_out, small = _local_step(x[0], loss_target[0], norm_pre_g, norm_post_g, w_in_g,
                                               w_out_g.reshape(D_MODEL, D_MODEL), conv_full)
    r_in, r_out, r_small = _rs_grads(dw_in, dw_out.reshape(N_DEV, SHARD_OUT, D_MODEL), small)
    g_in, d_in, nm_in, nv_in = _adamw_slabs(r_in, w_in, m_w_in, v_w_in, "adamw_in", 256)
    g_out, d_out, nm_out, nv_out = _adamw_slabs(r_out, w_out, m_w_out, v_w_out, "adamw_out", SHARD_OUT)
    sums = _sum_small(r_small)
    g_pre, g_post, loss = sums[0], sums[1], sums[2, 0]
    me = 4 * lax.axis_index("x") + 2 * lax.axis_index("y") + lax.axis_index("c")
    g_conv = lax.dynamic_slice(sums[3:6, 0:CONV_W], (0, me * n_conv), (3, n_conv))
    vec = lambda a: a.reshape(1, D_MODEL)
    d_pre, nm_pre, nv_pre = _adamw_whole(vec(g_pre), vec(norm_pre_g), vec(m_norm_pre_g), vec(v_norm_pre_g), "adamw_pre")
    d_post, nm_post, nv_post = _adamw_whole(vec(g_post), vec(norm_post_g), vec(m_norm_post_g), vec(v_norm_post_g),
                                            "adamw_post")
    d_conv, nm_conv, nv_conv = _adamw_whole(g_conv, conv_w, m_conv_w, v_conv_w, "adamw_conv")
    flat = lambda a: a.reshape(D_MODEL)
    return (loss, grad_x[None], g_pre, g_in, g_conv, g_out, g_post,
            flat(d_pre), d_in, d_conv, d_out, flat(d_post),
            flat(nm_pre), nm_in, nm_conv, nm_out, flat(nm_post),
            flat(nv_pre), nv_in, nv_conv, nv_out, flat(nv_post))
```

```python
import functools

import jax
import jax.numpy as jnp
import numpy as np
from jax import lax
from jax.experimental import pallas as pl
from jax.experimental.pallas import tpu as pltpu

F32 = jnp.float32
BF16 = jnp.bfloat16

D_MODEL = 1024
HEAD_DIM = 64
ATTN_W = 768
CONV_W = 256
IN_W = 4096
REST_W = IN_W - 3 * ATTN_W
BLK = 128
N_DEV = 8
SHARD_IN = IN_W // N_DEV
SHARD_OUT = D_MODEL // N_DEV
DILATIONS = (1, 4, 16)
ROPE_THETA = 10000.0
NORM_EPS = 1e-6
NEG = -1e30

ADAM_LR = 0.001
ADAM_B1 = 0.9
ADAM_B2 = 0.999
ADAM_EPS = 1e-08
ADAM_WD = 0.01
ADAM_STEP = 10

VMEM_LIMIT = 56 * 1024 * 1024
MESH = pl.DeviceIdType.MESH


def _params(n_grid):
    return pltpu.CompilerParams(dimension_semantics=("arbitrary",) * n_grid, vmem_limit_bytes=VMEM_LIMIT)


def _resident(shape):
    zeros = (0,) * len(shape)
    return pl.BlockSpec(shape, lambda *_: zeros, pipeline_mode=pl.Buffered(1))


def _sigmoid(a):
    return 1.0 / (1.0 + jnp.exp(-a))


def _swap_halves(t, first_half):
    return jnp.where(first_half, pltpu.roll(t, BLK - 32, 1), pltpu.roll(t, 32, 1))


def _rope_tables(seq):
    half = HEAD_DIM // 2
    inv_freq = ROPE_THETA ** (-jnp.arange(half, dtype=F32) * 2.0 / HEAD_DIM)
    ang = jnp.arange(seq).astype(F32)[:, None] * inv_freq[None, :]
    cos, sin = jnp.cos(ang), jnp.sin(ang)
    return jnp.concatenate([cos] * 4, axis=1), jnp.concatenate([-sin, sin, -sin, sin], axis=1)


N_CHUNK = ATTN_W // BLK


def _to_residue(src, chunk0, dst_ref, dil, rows, dtype):
    for r in range(dil):
        take = pl.ds(r, rows // dil, stride=dil) if dil > 1 else slice(None)
        for c in range(N_CHUNK):
            dst_ref[:, r * ATTN_W + c * BLK:r * ATTN_W + (c + 1) * BLK] = src[chunk0 + c, take, :].astype(dtype)


def _from_residue(src_ref, dst, dil, rows, accumulate):
    for r in range(dil):
        put = pl.ds(r, rows // dil, stride=dil) if dil > 1 else slice(None)
        for c in range(N_CHUNK):
            piece = src_ref[:, r * ATTN_W + c * BLK:r * ATTN_W + (c + 1) * BLK]
            if accumulate:
                dst[c, put, :] += piece
            else:
                dst[c, put, :] = piece


def _residue_spec(tm, dil):
    return pl.BlockSpec((tm // dil, dil * ATTN_W), lambda i: (i, 0))


def _residue_shape(seq, dil, dtype):
    return jax.ShapeDtypeStruct((seq // dil, dil * ATTN_W), dtype)


def _fwd_in(x, g_pre, w_in_g, cos_t, sin_t, tm=256):
    seq = x.shape[0]
    n_dil = len(DILATIONS)

    def body(x_ref, g_ref, w_ref, cos_ref, sin_ref, *rest):
        qkv_refs, (zr_ref, ht_ref, z_scr, qkv_scr) = rest[:3 * n_dil], rest[3 * n_dil:]
        xv = x_ref[...]
        r = lax.rsqrt(jnp.mean(xv * xv, axis=-1, keepdims=True) + NORM_EPS)
        hf = (xv * r) * g_ref[...]
        ht_ref[...] = hf.T.astype(BF16)
        h = hf.astype(BF16)
        for j in range(N_DEV):
            z_scr[:, j * SHARD_IN:(j + 1) * SHARD_IN] = jnp.dot(h, w_ref[j], preferred_element_type=F32)
        cos, sin = cos_ref[...], sin_ref[...]
        first_half = (lax.broadcasted_iota(jnp.int32, (tm, BLK), 1) & 32) == 0

        def rope(t):
            return t * cos + _swap_halves(t, first_half) * sin

        for c in range(N_CHUNK):
            qkv_scr[c] = rope(z_scr[:, c * BLK:(c + 1) * BLK]) * HEAD_DIM ** -0.5
            qkv_scr[N_CHUNK + c] = rope(z_scr[:, ATTN_W + c * BLK:ATTN_W + (c + 1) * BLK])
            qkv_scr[2 * N_CHUNK + c] = z_scr[:, 2 * ATTN_W + c * BLK:2 * ATTN_W + (c + 1) * BLK]
        for n, dil in enumerate(DILATIONS):
            for a in range(3):
                _to_residue(qkv_scr, a * N_CHUNK, qkv_refs[3 * n + a], dil, tm, BF16)
        zr_ref[...] = z_scr[:, 3 * ATTN_W:]

    row = lambda w: pl.BlockSpec((tm, w), lambda i: (i, 0))
    outs = pl.pallas_call(
        body, name="fwd_in", grid=(seq // tm,),
        in_specs=[row(D_MODEL), _resident((1, D_MODEL)), _resident((N_DEV, D_MODEL, SHARD_IN)), row(BLK), row(BLK)],
        out_specs=[_residue_spec(tm, dil) for dil in DILATIONS for _ in range(3)]
        + [row(REST_W), pl.BlockSpec((D_MODEL, tm), lambda i: (0, i))],
        out_shape=[_residue_shape(seq, dil, BF16) for dil in DILATIONS for _ in range(3)]
        + [jax.ShapeDtypeStruct((seq, REST_W), F32), jax.ShapeDtypeStruct((D_MODEL, seq), BF16)],
        scratch_shapes=[pltpu.VMEM((tm, IN_W), F32), pltpu.VMEM((3 * N_CHUNK, tm, BLK), F32)],
        compiler_params=_params(1),
    )(x, g_pre.reshape(1, D_MODEL), w_in_g, cos_t, sin_t)
    qkv = [tuple(outs[3 * n:3 * n + 3]) for n in range(n_dil)]
    return qkv, outs[3 * n_dil], outs[3 * n_dil + 1]


def _band_mask(first_block):
    qi = lax.broadcasted_iota(jnp.int32, (2 * BLK, 2 * BLK), 0) & (BLK - 1)
    kj = lax.broadcasted_iota(jnp.int32, (2 * BLK, 2 * BLK), 1)
    valid = (kj >= qi) & (kj <= qi + BLK)
    if first_block is not None:
        valid = valid & ((kj >= BLK) | jnp.logical_not(first_block))
    return valid


def _stack_heads(t, head0):
    keep0 = head0.astype(F32).astype(BF16)
    return jnp.concatenate([t * keep0, t * (1 - keep0)], axis=0)


def _unstack_heads(t2, head0):
    return jnp.where(head0, t2[:BLK], t2[BLK:])


def _rows_per_head(a, head0):
    b = pltpu.roll(a, HEAD_DIM, 1)
    rows = jnp.concatenate([jnp.where(head0, a, b), jnp.where(head0, b, a)], axis=0)
    return jnp.concatenate([rows, rows], axis=1)


def _attn_specs(length, dil):
    tb = min(4, length // BLK)
    tile = pl.BlockSpec((tb * BLK, BLK), lambda c, t: (t, c))
    prev = pl.BlockSpec((BLK, BLK), lambda c, t: (jnp.maximum(t * tb - 1, 0), c))
    grid = (dil * ATTN_W // BLK, length // (tb * BLK))
    return tb, tile, prev, grid


def _attn_fwd(q, k, v, dil):
    length = q.shape[0]
    tb, tile, prev, grid = _attn_specs(length, dil)

    def body(q_ref, kc_ref, kp_ref, vc_ref, vp_ref, o_ref, lse_ref, kcat, vcat):
        t = pl.program_id(1)
        kcat[0:BLK] = kp_ref[...]
        kcat[BLK:] = kc_ref[...]
        vcat[0:BLK] = vp_ref[...]
        vcat[BLK:] = vc_ref[...]
        head0 = lax.broadcasted_iota(jnp.int32, (BLK, BLK), 1) < HEAD_DIM
        for j in range(tb):
            valid = _band_mask(t == 0 if j == 0 else None)
            q2 = _stack_heads(q_ref[j * BLK:(j + 1) * BLK, :], head0)
            kk = kcat[j * BLK:(j + 2) * BLK, :]
            s = lax.dot_general(q2, kk, (((1,), (1,)), ((), ())), preferred_element_type=F32)
            s = jnp.where(valid, s, NEG)
            m = jnp.max(s, axis=1, keepdims=True)
            p = jnp.exp(s - m)
            den = jnp.sum(p, axis=1, keepdims=True)
            o2 = jnp.dot(p.astype(BF16), vcat[j * BLK:(j + 2) * BLK, :], preferred_element_type=F32) / den
            lse2 = jnp.broadcast_to(m + jnp.log(den), (2 * BLK, BLK))
            o_ref[j * BLK:(j + 1) * BLK, :] = _unstack_heads(o2, head0)
            lse_ref[j * BLK:(j + 1) * BLK, :] = _unstack_heads(lse2, head0)

    return pl.pallas_call(
        body, name=f"attn_fwd_d{dil}", grid=grid,
        in_specs=[tile, tile, prev, tile, prev], out_specs=[tile, tile],
        out_shape=[jax.ShapeDtypeStruct(q.shape, F32)] * 2,
        scratch_shapes=[pltpu.VMEM(((tb + 1) * BLK, BLK), BF16)] * 2,
        compiler_params=_params(2),
    )(q, k, k, v, v)


def _attn_bwd(q, k, v, do, lse, delta, dil):
    length = q.shape[0]
    tb, tile, prev, grid = _attn_specs(length, dil)
    whole = pl.BlockSpec((length, BLK), lambda c, t: (0, c))

    def body(q_ref, do_ref, lse_ref, dl_ref, kc_ref, kp_ref, vc_ref, vp_ref, dq_ref, dk_ref, dv_ref, kcat, vcat):
        t = pl.program_id(1)
        kcat[0:BLK] = kp_ref[...]
        kcat[BLK:] = kc_ref[...]
        vcat[0:BLK] = vp_ref[...]
        vcat[BLK:] = vc_ref[...]
        head0 = lax.broadcasted_iota(jnp.int32, (BLK, BLK), 1) < HEAD_DIM
        for j in range(tb):
            rows = slice(j * BLK, (j + 1) * BLK)
            valid = _band_mask(t == 0 if j == 0 else None)
            q2 = _stack_heads(q_ref[rows, :], head0)
            do2 = _stack_heads(do_ref[rows, :], head0)
            kk = kcat[j * BLK:(j + 2) * BLK, :]
            vv = vcat[j * BLK:(j + 2) * BLK, :]
            s = lax.dot_general(q2, kk, (((1,), (1,)), ((), ())), preferred_element_type=F32)
            p = jnp.where(valid, jnp.exp(s - _rows_per_head(lse_ref[rows, :], head0)), 0.0)
            dp = lax.dot_general(do2, vv, (((1,), (1,)), ((), ())), preferred_element_type=F32)
            ds = (p * (dp - _rows_per_head(dl_ref[rows, :], head0))).astype(BF16)
            dq2 = jnp.dot(ds, kk, preferred_element_type=F32)
            dq_ref[rows, :] = _unstack_heads(dq2, head0) * HEAD_DIM ** -0.5
            dk2 = lax.dot_general(ds, q2, (((0,), (0,)), ((), ())), preferred_element_type=F32)
            dv2 = lax.dot_general(p.astype(BF16), do2, (((0,), (0,)), ((), ())), preferred_element_type=F32)
            own = pl.ds(pl.multiple_of((t * tb + j) * BLK, BLK), BLK)
            dk_ref[own, :] = dk2[BLK:]
            dv_ref[own, :] = dv2[BLK:]

            def add_to_previous(j=j, dk2=dk2, dv2=dv2):
                before = pl.ds(pl.multiple_of((t * tb + j - 1) * BLK, BLK), BLK)
                dk_ref[before, :] += dk2[:BLK]
                dv_ref[before, :] += dv2[:BLK]

            if j == 0:
                pl.when(t > 0)(add_to_previous)
            else:
                add_to_previous()

    return pl.pallas_call(
        body, name=f"attn_bwd_d{dil}", grid=grid,
        in_specs=[tile, tile, tile, tile, tile, prev, tile, prev], out_specs=[tile, whole, whole],
        out_shape=[jax.ShapeDtypeStruct(q.shape, F32)] * 3,
        scratch_shapes=[pltpu.VMEM(((tb + 1) * BLK, BLK), BF16)] * 2,
        compiler_params=_params(2),
    )(q, do, lse, delta, k, k, v, v)


def _conv_taps(u, before8, tm):
    row = lax.broadcasted_iota(jnp.int32, u.shape, 0)
    last, last2 = before8[7:8, :], before8[6:7, :]
    u1 = jnp.where(row == 0, last, pltpu.roll(u, 1, 0))
    u2 = jnp.where(row == 0, last2, jnp.where(row == 1, last, pltpu.roll(u, 2, 0)))
    return u1, u2


def _attn_combine(o_parts, lse_parts, zr, conv_w, tm=256):
    seq = zr.shape[0]
    a0, h0, b0, c0, g0 = 0, ATTN_W, ATTN_W + CONV_W, ATTN_W + 2 * CONV_W, ATTN_W + 3 * CONV_W

    def body(o1, o2, o3, l1, l2, l3, zr_ref, zp_ref, w_ref, mixed_ref, o_ref, lse1, lse2, lse3, *scr):
        i = pl.program_id(0)
        for src, dst, dil in zip((o2, o3, l2, l3), scr[:4], DILATIONS[1:] * 2):
            _from_residue(src, dst, dil, tm, accumulate=False)
        for c in range(N_CHUNK):
            cols = slice(c * BLK, (c + 1) * BLK)
            la, lb, lc = l1[:, cols], scr[2][c], scr[3][c]
            top = jnp.maximum(jnp.maximum(la, lb), lc)
            ea, eb, ec = jnp.exp(la - top), jnp.exp(lb - top), jnp.exp(lc - top)
            den = ea + eb + ec
            o = (ea / den) * o1[:, cols] + (eb / den) * scr[0][c] + (ec / den) * scr[1][c]
            o_ref[:, cols] = o
            scr[4][c] = top + jnp.log(den)
            ga = zr_ref[:, cols]
            mixed_ref[:, cols] = (o * (ga * _sigmoid(ga))).astype(BF16)
        for dst, dil in zip((lse1, lse2, lse3), DILATIONS):
            _to_residue(scr[4], 0, dst, dil, tm, F32)
        u = zr_ref[:, c0:g0] * zr_ref[:, h0:b0]
        before = jnp.where(i > 0, zp_ref[:, c0:g0] * zp_ref[:, h0:b0], 0.0)
        u1, u2 = _conv_taps(u, before, tm)
        y = u2 * w_ref[0:1, :] + u1 * w_ref[1:2, :] + u * w_ref[2:3, :]
        gc = zr_ref[:, g0:]
        mixed_ref[:, ATTN_W:] = ((zr_ref[:, b0:c0] * y) * (gc * _sigmoid(gc))).astype(BF16)

    row = lambda w: pl.BlockSpec((tm, w), lambda i: (i, 0))
    before8 = pl.BlockSpec((8, REST_W), lambda i: (jnp.maximum(i * (tm // 8) - 1, 0), 0))
    views = [_residue_spec(tm, dil) for dil in DILATIONS]
    outs = pl.pallas_call(
        body, name="attn_combine", grid=(seq // tm,),
        in_specs=views * 2 + [row(REST_W), before8, _resident((3, CONV_W))],
        out_specs=[row(D_MODEL), row(ATTN_W)] + views,
        out_shape=[jax.ShapeDtypeStruct((seq, D_MODEL), BF16), jax.ShapeDtypeStruct((seq, ATTN_W), F32)]
        + [_residue_shape(seq, dil, F32) for dil in DILATIONS],
        scratch_shapes=[pltpu.VMEM((N_CHUNK, tm, BLK), F32)] * 5,
        compiler_params=_params(1),
    )(*o_parts, *lse_parts, zr, zr, conv_w)
    return outs[0], outs[1], outs[2:]


def _out_loss_bwd(mixed, w_out_g, x, target, g_post, tm=256):
    seq = x.shape[0]

    def body(mx_ref, w_ref, x_ref, t_ref, g_ref, dout_ref, dy_ref, dmx_ref, dw_ref, st_ref):
        i = pl.program_id(0)
        mx = mx_ref[...]
        y = jnp.dot(mx, w_ref[...], preferred_element_type=F32)
        r = lax.rsqrt(jnp.mean(y * y, axis=-1, keepdims=True) + NORM_EPS)
        yhat = y * r
        g = g_ref[...]
        err = (x_ref[...] + yhat * g) - t_ref[...]
        dn = err * (1.0 / D_MODEL)
        dout_ref[...] = dn
        tg = dn * g
        dy = (r * (tg - yhat * jnp.mean(tg * yhat, axis=-1, keepdims=True))).astype(BF16)
        dy_ref[...] = dy
        dmx_ref[...] = lax.dot_general(dy, w_ref[...], (((1,), (1,)), ((), ())), preferred_element_type=F32)
        dw = lax.dot_general(mx, dy, (((0,), (0,)), ((), ())), preferred_element_type=F32)
        gsum = jnp.sum(dn * yhat, axis=0, keepdims=True)
        lsum = jnp.broadcast_to(0.5 / D_MODEL * jnp.sum(err * err), (1, D_MODEL))

        @pl.when(i == 0)
        def _():
            dw_ref[...] = dw
            st_ref[...] = jnp.zeros_like(st_ref)
            st_ref[0:1, :] = gsum
            st_ref[1:2, :] = lsum

        @pl.when(i > 0)
        def _():
            dw_ref[...] += dw
            st_ref[0:1, :] += gsum
            st_ref[1:2, :] += lsum

    row = lambda w: pl.BlockSpec((tm, w), lambda i: (i, 0))
    return pl.pallas_call(
        body, name="out_loss_bwd", grid=(seq // tm,),
        in_specs=[row(D_MODEL), _resident((D_MODEL, D_MODEL)), row(D_MODEL), row(D_MODEL), _resident((1, D_MODEL))],
        out_specs=[row(D_MODEL), row(D_MODEL), row(D_MODEL), pl.BlockSpec((D_MODEL, D_MODEL), lambda i: (0, 0)),
                   pl.BlockSpec((8, D_MODEL), lambda i: (0, 0))],
        out_shape=[jax.ShapeDtypeStruct((seq, D_MODEL), F32), jax.ShapeDtypeStruct((seq, D_MODEL), BF16),
                   jax.ShapeDtypeStruct((seq, D_MODEL), F32), jax.ShapeDtypeStruct((D_MODEL, D_MODEL), F32),
                   jax.ShapeDtypeStruct((8, D_MODEL), F32)],
        compiler_params=_params(1),
    )(mixed, w_out_g, x, target, g_post.reshape(1, D_MODEL))


def _head_sum(prod, same_head):
    hi = prod.astype(BF16)
    lo = (prod - hi.astype(F32)).astype(BF16)
    return (jnp.dot(hi, same_head, preferred_element_type=F32) + jnp.dot(lo, same_head, preferred_element_type=F32))


def _gate_bwd(dmixed, zr, o, conv_w, tm=256):
    seq = zr.shape[0]
    n_tiles = seq // tm
    n_dil = len(DILATIONS)
    a0, h0, b0, c0, g0 = 0, ATTN_W, ATTN_W + CONV_W, ATTN_W + 2 * CONV_W, ATTN_W + 3 * CONV_W

    def body(dm_ref, dmn_ref, zr_ref, zp_ref, zn_ref, o_ref, w_ref, *rest):
        do_refs, dl_refs = rest[:n_dil], rest[n_dil:2 * n_dil]
        dz_ref, dw_ref, do_scr, dl_scr = rest[2 * n_dil:]
        i = pl.program_id(0)
        ga = zr_ref[:, a0:h0]
        sg = _sigmoid(ga)
        dattn = dm_ref[:, 0:ATTN_W]
        ov = o_ref[...]
        do = dattn * (ga * sg)
        dz_ref[:, a0:h0] = (dattn * ov * (sg * (1.0 + ga * (1.0 - sg)))).astype(BF16)
        li = lax.broadcasted_iota(jnp.int32, (BLK, BLK), 0) // HEAD_DIM
        lj = lax.broadcasted_iota(jnp.int32, (BLK, BLK), 1) // HEAD_DIM
        same_head = (li == lj).astype(BF16)
        prod = do * ov
        for c in range(N_CHUNK):
            cols = slice(c * BLK, (c + 1) * BLK)
            do_scr[c] = do[:, cols]
            dl_scr[c] = _head_sum(prod[:, cols], same_head)
        for n, dil in enumerate(DILATIONS):
            _to_residue(do_scr, 0, do_refs[n], dil, tm, BF16)
            _to_residue(dl_scr, 0, dl_refs[n], dil, tm, F32)

        ch, cb, cc, gc = zr_ref[:, h0:b0], zr_ref[:, b0:c0], zr_ref[:, c0:g0], zr_ref[:, g0:]
        u = cc * ch
        before = jnp.where(i > 0, zp_ref[:, c0:g0] * zp_ref[:, h0:b0], 0.0)
        u1, u2 = _conv_taps(u, before, tm)
        w0, w1, w2 = w_ref[0:1, :], w_ref[1:2, :], w_ref[2:3, :]
        y = u2 * w0 + u1 * w1 + u * w2
        sc = _sigmoid(gc)
        silu_c = gc * sc
        dconv = dm_ref[:, ATTN_W:]
        dz_ref[:, b0:c0] = (dconv * y * silu_c).astype(BF16)
        dz_ref[:, g0:] = (dconv * (cb * y) * (sc * (1.0 + gc * (1.0 - sc)))).astype(BF16)
        dy = dconv * cb * silu_c
        gn = zn_ref[:, g0:]
        after = jnp.where(i < n_tiles - 1, dmn_ref[:, ATTN_W:] * zn_ref[:, b0:c0] * (gn * _sigmoid(gn)), 0.0)
        row = lax.broadcasted_iota(jnp.int32, dy.shape, 0)
        nxt, nxt2 = after[0:1, :], after[1:2, :]
        dy1 = jnp.where(row == tm - 1, nxt, pltpu.roll(dy, tm - 1, 0))
        dy2 = jnp.where(row == tm - 1, nxt2, jnp.where(row == tm - 2, nxt, pltpu.roll(dy, tm - 2, 0)))
        du = dy * w2 + dy1 * w1 + dy2 * w0
        dz_ref[:, c0:g0] = (du * ch).astype(BF16)
        dz_ref[:, h0:b0] = (du * cc).astype(BF16)
        dws = [jnp.sum(dy * u2, axis=0, keepdims=True), jnp.sum(dy * u1, axis=0, keepdims=True),
               jnp.sum(dy * u, axis=0, keepdims=True)]

        @pl.when(i == 0)
        def _():
            dw_ref[...] = jnp.zeros_like(dw_ref)

        for n, part in enumerate(dws):
            dw_ref[n:n + 1, :] += part

    row_spec = lambda w: pl.BlockSpec((tm, w), lambda i: (i, 0))
    before8 = pl.BlockSpec((8, REST_W), lambda i: (jnp.maximum(i * (tm // 8) - 1, 0), 0))
    after8 = lambda w: pl.BlockSpec((8, w), lambda i: (jnp.minimum((i + 1) * (tm // 8), seq // 8 - 1), 0))
    views = [_residue_spec(tm, dil) for dil in DILATIONS]
    outs = pl.pallas_call(
        body, name="gate_bwd", grid=(n_tiles,),
        in_specs=[row_spec(D_MODEL), after8(D_MODEL), row_spec(REST_W), before8, after8(REST_W), row_spec(ATTN_W),
                  _resident((3, CONV_W))],
        out_specs=views * 2 + [row_spec(REST_W), pl.BlockSpec((8, CONV_W), lambda i: (0, 0))],
        out_shape=[_residue_shape(seq, dil, BF16) for dil in DILATIONS]
        + [_residue_shape(seq, dil, F32) for dil in DILATIONS]
        + [jax.ShapeDtypeStruct((seq, REST_W), BF16), jax.ShapeDtypeStruct((8, CONV_W), F32)],
        scratch_shapes=[pltpu.VMEM((N_CHUNK, tm, BLK), F32)] * 2,
        compiler_params=_params(1),
    )(dmixed, dmixed, zr, zr, zr, o, conv_w)
    return outs[:n_dil], outs[n_dil:2 * n_dil], outs[2 * n_dil], outs[2 * n_dil + 1]


def _in_bwd(dqs, dks, dvs, dzr, cos_t, sin_t, x, d_out, g_pre, w_in_g, tm=256):
    seq = x.shape[0]

    def body(q1, q2, q3, k1, k2, k3, v1, v2, v3, dzr_ref, cos_ref, sin_ref, x_ref, dout_ref, g_ref, w_ref,
             dz_ref, gx_ref, st_ref, dq_scr, dk_scr, dv_scr):
        i = pl.program_id(0)
        for parts, total in (((q1, q2, q3), dq_scr), ((k1, k2, k3), dk_scr), ((v1, v2, v3), dv_scr)):
            for n, dil in enumerate(DILATIONS):
                _from_residue(parts[n], total, dil, tm, accumulate=n > 0)
        cos, sin = cos_ref[...], sin_ref[...]
        first_half = (lax.broadcasted_iota(jnp.int32, (tm, BLK), 1) & 32) == 0

        def unrope(t):
            return t * cos - _swap_halves(t, first_half) * sin

        for c in range(N_CHUNK):
            dz_ref[:, c * BLK:(c + 1) * BLK] = unrope(dq_scr[c]).astype(BF16)
            dz_ref[:, ATTN_W + c * BLK:ATTN_W + (c + 1) * BLK] = unrope(dk_scr[c]).astype(BF16)
            dz_ref[:, 2 * ATTN_W + c * BLK:2 * ATTN_W + (c + 1) * BLK] = dv_scr[c].astype(BF16)
        dz_ref[:, 3 * ATTN_W:] = dzr_ref[...]
        dh = jnp.zeros((tm, D_MODEL), F32)
        for j in range(N_DEV):
            dh += lax.dot_general(dz_ref[:, j * SHARD_IN:(j + 1) * SHARD_IN], w_ref[j], (((1,), (1,)), ((), ())),
                                  preferred_element_type=F32)
        xv = x_ref[...]
        r = lax.rsqrt(jnp.mean(xv * xv, axis=-1, keepdims=True) + NORM_EPS)
        xhat = xv * r
        tg = dh * g_ref[...]
        gx_ref[...] = dout_ref[...] + r * (tg - xhat * jnp.mean(tg * xhat, axis=-1, keepdims=True))
        gsum = jnp.sum(dh * xhat, axis=0, keepdims=True)

        @pl.when(i == 0)
        def _():
            st_ref[...] = jnp.zeros_like(st_ref)

        st_ref[0:1, :] += gsum

    row = lambda w: pl.BlockSpec((tm, w), lambda i: (i, 0))
    return pl.pallas_call(
        body, name="in_bwd", grid=(seq // tm,),
        in_specs=[_residue_spec(tm, dil) for dil in DILATIONS] * 3
        + [row(REST_W), row(BLK), row(BLK), row(D_MODEL), row(D_MODEL), _resident((1, D_MODEL)),
           _resident((N_DEV, D_MODEL, SHARD_IN))],
        out_specs=[row(IN_W), row(D_MODEL), pl.BlockSpec((8, D_MODEL), lambda i: (0, 0))],
        out_shape=[jax.ShapeDtypeStruct((seq, IN_W), BF16), jax.ShapeDtypeStruct((seq, D_MODEL), F32),
                   jax.ShapeDtypeStruct((8, D_MODEL), F32)],
        scratch_shapes=[pltpu.VMEM((N_CHUNK, tm, BLK), F32)] * 3,
        compiler_params=_params(1),
    )(*dqs, *dks, *dvs, dzr, cos_t, sin_t, x, d_out, g_pre.reshape(1, D_MODEL), w_in_g)


def _dw_in(ht, dz, tk=512):
    seq = dz.shape[0]

    def body(h_ref, dz_ref, dw_ref):
        dw = jnp.dot(h_ref[...], dz_ref[...], preferred_element_type=F32)

        @pl.when(pl.program_id(1) == 0)
        def _():
            dw_ref[0] = dw

        @pl.when(pl.program_id(1) > 0)
        def _():
            dw_ref[0] += dw

    return pl.pallas_call(
        body, name="dw_in", grid=(N_DEV, seq // tk),
        in_specs=[pl.BlockSpec((D_MODEL, tk), lambda j, k: (0, k)), pl.BlockSpec((tk, SHARD_IN), lambda j, k: (k, j))],
        out_specs=pl.BlockSpec((1, D_MODEL, SHARD_IN), lambda j, k: (j, 0, 0)),
        out_shape=jax.ShapeDtypeStruct((N_DEV, D_MODEL, SHARD_IN), F32),
        compiler_params=_params(2),
    )(ht, dz)


def _local_step(x, target, g_pre, g_post, w_in_g, w_out_g, conv_w):
    seq = x.shape[0]
    cos_t, sin_t = _rope_tables(seq)
    qkv, zr, ht = _fwd_in(x, g_pre, w_in_g, cos_t, sin_t)
    parts = [_attn_fwd(*qkv[n], dil) for n, dil in enumerate(DILATIONS)]
    mixed, o, lse = _attn_combine([p[0] for p in parts], [p[1] for p in parts], zr, conv_w)
    d_out, dy, dmixed, dw_out, st_post = _out_loss_bwd(mixed, w_out_g, x, target, g_post)
    do, delta, dzr, dconv = _gate_bwd(dmixed, zr, o, conv_w)
    grads = [_attn_bwd(*qkv[n], do[n], lse[n], delta[n], dil) for n, dil in enumerate(DILATIONS)]
    dz, grad_x, st_pre = _in_bwd([g[0] for g in grads], [g[1] for g in grads], [g[2] for g in grads], dzr,
                                 cos_t, sin_t, x, d_out, g_pre, w_in_g)
    dw_in = _dw_in(ht, dz)
    conv_rows = jnp.pad(dconv[0:3], ((0, 0), (0, D_MODEL - CONV_W)))
    small = jnp.concatenate([st_pre[0:1], st_post[0:2], conv_rows, jnp.zeros((2, D_MODEL), F32)], axis=0)
    return grad_x, dw_in, dw_out, small


def _coords():
    return lax.axis_index("x"), lax.axis_index("y"), lax.axis_index("c")


def _peer(k):
    x, y, c = _coords()
    px = 1 - x if k & 4 else x
    py = 1 - y if k & 2 else y
    pc = 1 - c if k & 1 else c
    return (px, py, pc), 4 * px + 2 * py + pc


HBM_SPEC = pl.BlockSpec(memory_space=pltpu.HBM)
VMEM_SPEC = pl.BlockSpec(memory_space=pltpu.VMEM)


def _ag_weights(w_in, w_out, conv_w):
    def body(win_ref, wout_ref, cw_ref, gin_ref, gout_ref, gcw_ref, win_bf, wout_bf, cw_pad, send_sems, recv_sems,
             local_sems):
        x, y, c = _coords()
        me = 4 * x + 2 * y + c
        win_bf[...] = win_ref[...].astype(BF16)
        wout_bf[...] = wout_ref[...].astype(BF16)
        cw_pad[...] = jnp.zeros_like(cw_pad)
        cw_pad[0:3, 0:CONV_W // N_DEV] = cw_ref[...]
        pairs = [(win_bf, gin_ref), (wout_bf, gout_ref), (cw_pad, gcw_ref)]
        local = [pltpu.make_async_copy(src, dst.at[me], local_sems.at[n]) for n, (src, dst) in enumerate(pairs)]
        for cp in local:
            cp.start()
        sends = []
        for k in range(1, N_DEV):
            peer, _ = _peer(k)
            for n, (src, dst) in enumerate(pairs):
                sends.append(pltpu.make_async_remote_copy(src_ref=src, dst_ref=dst.at[me], send_sem=send_sems.at[k - 1, n],
                                                          recv_sem=recv_sems.at[k - 1, n], device_id=peer,
                                                          device_id_type=MESH))
        for cp in sends:
            cp.start()
        for k in range(1, N_DEV):
            peer, peer_idx = _peer(k)
            for n, (src, dst) in enumerate(pairs):
                pltpu.make_async_remote_copy(src_ref=src, dst_ref=dst.at[peer_idx], send_sem=send_sems.at[k - 1, n],
                                             recv_sem=recv_sems.at[k - 1, n], device_id=peer,
                                             device_id_type=MESH).wait_recv()
        for cp in sends:
            cp.wait_send()
        for cp in local:
            cp.wait()

    return pl.pallas_call(
        body, name="ag_weights",
        in_specs=[VMEM_SPEC, VMEM_SPEC, VMEM_SPEC], out_specs=[HBM_SPEC, HBM_SPEC, HBM_SPEC],
        out_shape=[jax.ShapeDtypeStruct((N_DEV, D_MODEL, SHARD_IN), BF16),
                   jax.ShapeDtypeStruct((N_DEV, SHARD_OUT, D_MODEL), BF16),
                   jax.ShapeDtypeStruct((N_DEV, 8, BLK), F32)],
        scratch_shapes=[pltpu.VMEM((D_MODEL, SHARD_IN), BF16), pltpu.VMEM((SHARD_OUT, D_MODEL), BF16),
                        pltpu.VMEM((8, BLK), F32), pltpu.SemaphoreType.DMA((N_DEV - 1, 3)),
                        pltpu.SemaphoreType.DMA((N_DEV - 1, 3)), pltpu.SemaphoreType.DMA((3,))],
        compiler_params=pltpu.CompilerParams(vmem_limit_bytes=VMEM_LIMIT),
    )(w_in, w_out, conv_w)


def _rs_grads(dw_in, dw_out, small):
    def body(din_ref, dout_ref, sm_ref, rin_ref, rout_ref, rsm_ref, send_sems, recv_sems, local_sems):
        x, y, c = _coords()
        me = 4 * x + 2 * y + c
        local = [pltpu.make_async_copy(din_ref.at[me], rin_ref.at[me], local_sems.at[0]),
                 pltpu.make_async_copy(dout_ref.at[me], rout_ref.at[me], local_sems.at[1]),
                 pltpu.make_async_copy(sm_ref, rsm_ref.at[me], local_sems.at[2])]
        for cp in local:
            cp.start()
        sends = []
        for k in range(1, N_DEV):
            peer, peer_idx = _peer(k)
            srcs = [din_ref.at[peer_idx], dout_ref.at[peer_idx], sm_ref]
            for n, (src, dst) in enumerate(zip(srcs, [rin_ref, rout_ref, rsm_ref])):
                sends.append(pltpu.make_async_remote_copy(src_ref=src, dst_ref=dst.at[me], send_sem=send_sems.at[k - 1, n],
                                                          recv_sem=recv_sems.at[k - 1, n], device_id=peer,
                                                          device_id_type=MESH))
        for cp in sends:
            cp.start()
        for k in range(1, N_DEV):
            peer, peer_idx = _peer(k)
            srcs = [din_ref.at[me], dout_ref.at[me], sm_ref]
            for n, (src, dst) in enumerate(zip(srcs, [rin_ref, rout_ref, rsm_ref])):
                pltpu.make_async_remote_copy(src_ref=src, dst_ref=dst.at[peer_idx], send_sem=send_sems.at[k - 1, n],
                                             recv_sem=recv_sems.at[k - 1, n], device_id=peer,
                                             device_id_type=MESH).wait_recv()
        for cp in sends:
            cp.wait_send()
        for cp in local:
            cp.wait()

    return pl.pallas_call(
        body, name="rs_grads",
        in_specs=[HBM_SPEC, HBM_SPEC, HBM_SPEC], out_specs=[HBM_SPEC, HBM_SPEC, HBM_SPEC],
        out_shape=[jax.ShapeDtypeStruct((N_DEV, D_MODEL, SHARD_IN), F32),
                   jax.ShapeDtypeStruct((N_DEV, SHARD_OUT, D_MODEL), F32),
                   jax.ShapeDtypeStruct((N_DEV, 8, D_MODEL), F32)],
        scratch_shapes=[pltpu.SemaphoreType.DMA((N_DEV - 1, 3)), pltpu.SemaphoreType.DMA((N_DEV - 1, 3)),
                        pltpu.SemaphoreType.DMA((3,))],
    )(dw_in, dw_out, small)


def _adamw_math(w, g, m, v):
    m = ADAM_B1 * m + (1.0 - ADAM_B1) * g
    v = ADAM_B2 * v + (1.0 - ADAM_B2) * (g * g)
    m_hat = m / (1.0 - ADAM_B1 ** ADAM_STEP)
    v_hat = v / (1.0 - ADAM_B2 ** ADAM_STEP)
    delta = -ADAM_LR * (m_hat / (jnp.sqrt(v_hat) + ADAM_EPS) + ADAM_WD * w)
    return delta, m, v


def _sum_slabs(ref):
    total = ref[0]
    for s in range(1, N_DEV):
        total = total + ref[s]
    return total


def _adamw_slabs(parts, w, m, v, name, tr):
    rows, cols = w.shape

    def body(p_ref, w_ref, m_ref, v_ref, g_ref, d_ref, nm_ref, nv_ref):
        g = _sum_slabs(p_ref)
        g_ref[...] = g
        d_ref[...], nm_ref[...], nv_ref[...] = _adamw_math(w_ref[...], g, m_ref[...], v_ref[...])

    tile = pl.BlockSpec((tr, cols), lambda i: (i, 0))
    return pl.pallas_call(
        body, name=name, grid=(rows // tr,),
        in_specs=[pl.BlockSpec((N_DEV, tr, cols), lambda i: (0, i, 0)), tile, tile, tile], out_specs=[tile] * 4,
        out_shape=[jax.ShapeDtypeStruct((rows, cols), F32)] * 4,
        compiler_params=_params(1),
    )(parts, w, m, v)


def _sum_small(parts):
    def body(p_ref, out_ref):
        out_ref[...] = _sum_slabs(p_ref)

    return pl.pallas_call(body, name="sum_small", out_shape=jax.ShapeDtypeStruct(parts.shape[1:], F32))(parts)


def _adamw_whole(g, w, m, v, name):
    def body(g_ref, w_ref, m_ref, v_ref, d_ref, nm_ref, nv_ref):
        d_ref[...], nm_ref[...], nv_ref[...] = _adamw_math(w_ref[...], g_ref[...], m_ref[...], v_ref[...])

    return pl.pallas_call(body, name=name, out_shape=[jax.ShapeDtypeStruct(w.shape, F32)] * 3)(g, w, m, v)


def kernel(x, norm_pre_g, w_in, conv_w, w_out, norm_post_g, loss_target, m_norm_pre_g, m_w_in, m_conv_w, m_w_out,
           m_norm_post_g, v_norm_pre_g, v_w_in, v_conv_w, v_w_out, v_norm_post_g):
    n_conv = CONV_W // N_DEV
    w_in_g, w_out_g, conv_g = _ag_weights(w_in, w_out, conv_w)
    conv_full = conv_g[:, 0:3, 0:n_conv].transpose(1, 0, 2).reshape(3, CONV_W)
    grad_x, dw_in, dw_out, small = _local_step(x[0], loss_target[0], norm_pre_g, norm_post_g, w_in_g,
                                               w_out_g.reshape(D_MODEL, D_MODEL), conv_full)
    r_in, r_out, r_small = _rs_grads(dw_in, dw_out.reshape(N_DEV, SHARD_OUT, D_MODEL), small)
    g_in, d_in, nm_in, nv_in = _adamw_slabs(r_in, w_in, m_w_in, v_w_in, "adamw_in", 256)
    g_out, d_out, nm_out, nv_out = _adamw_slabs(r_out, w_out, m_w_out, v_w_out, "adamw_out", SHARD_OUT)
    sums = _sum_small(r_small)
    g_pre, g_post, loss = sums[0], sums[1], sums[2, 0]
    me = 4 * lax.axis_index("x") + 2 * lax.axis_index("y") + lax.axis_index("c")
    g_conv = lax.dynamic_slice(sums[3:6, 0:CONV_W], (0, me * n_conv), (3, n_conv))
    vec = lambda a: a.reshape(1, D_MODEL)
    d_pre, nm_pre, nv_pre = _adamw_whole(vec(g_pre), vec(norm_pre_g), vec(m_norm_pre_g), vec(v_norm_pre_g), "adamw_pre")
    d_post, nm_post, nv_post = _adamw_whole(vec(g_post), vec(norm_post_g), vec(m_norm_post_g), vec(v_norm_post_g),
                                            "adamw_post")
    d_conv, nm_conv, nv_conv = _adamw_whole(g_conv, conv_w, m_conv_w, v_conv_w, "adamw_conv")
    flat = lambda a: a.reshape(D_MODEL)
    return (loss, grad_x[None], g_pre, g_in, g_conv, g_out, g_post,
            flat(d_pre), d_in, d_conv, d_out, flat(d_post),
            flat(nm_pre), nm_in, nm_conv, nm_out, flat(nm_post),
            flat(nv_pre), nv_in, nv_conv, nv_out, flat(nv_post))
```

```python
import functools

import jax
import jax.numpy as jnp
import numpy as np
from jax import lax
from jax.experimental import pallas as pl
from jax.experimental.pallas import tpu as pltpu

F32 = jnp.float32
BF16 = jnp.bfloat16

D_MODEL = 1024
HEAD_DIM = 64
ATTN_W = 768
CONV_W = 256
IN_W = 4096
REST_W = IN_W - 3 * ATTN_W
BLK = 128
N_DEV = 8
SHARD_IN = IN_W // N_DEV
SHARD_OUT = D_MODEL // N_DEV
DILATIONS = (1, 4, 16)
ROPE_THETA = 10000.0
NORM_EPS = 1e-6
NEG = -1e30

ADAM_LR = 0.001
ADAM_B1 = 0.9
ADAM_B2 = 0.999
ADAM_EPS = 1e-08
ADAM_WD = 0.01
ADAM_STEP = 10

VMEM_LIMIT = 56 * 1024 * 1024
MESH = pl.DeviceIdType.MESH


def _params(n_grid):
    return pltpu.CompilerParams(dimension_semantics=("arbitrary",) * n_grid, vmem_limit_bytes=VMEM_LIMIT)


def _resident(shape):
    zeros = (0,) * len(shape)
    return pl.BlockSpec(shape, lambda *_: zeros, pipeline_mode=pl.Buffered(1))


def _sigmoid(a):
    return 1.0 / (1.0 + jnp.exp(-a))


def _swap_halves(t, first_half):
    return jnp.where(first_half, pltpu.roll(t, BLK - 32, 1), pltpu.roll(t, 32, 1))


def _rope_tables(seq):
    half = HEAD_DIM // 2
    inv_freq = ROPE_THETA ** (-jnp.arange(half, dtype=F32) * 2.0 / HEAD_DIM)
    ang = jnp.arange(seq).astype(F32)[:, None] * inv_freq[None, :]
    cos, sin = jnp.cos(ang), jnp.sin(ang)
    return jnp.concatenate([cos] * 4, axis=1), jnp.concatenate([-sin, sin, -sin, sin], axis=1)


N_CHUNK = ATTN_W // BLK


def _to_residue(src, chunk0, dst_ref, dil, rows, dtype):
    for r in range(dil):
        take = pl.ds(r, rows // dil, stride=dil) if dil > 1 else slice(None)
        for c in range(N_CHUNK):
            dst_ref[:, r * ATTN_W + c * BLK:r * ATTN_W + (c + 1) * BLK] = src[chunk0 + c, take, :].astype(dtype)


def _from_residue(src_ref, dst, dil, rows, accumulate):
    for r in range(dil):
        put = pl.ds(r, rows // dil, stride=dil) if dil > 1 else slice(None)
        for c in range(N_CHUNK):
            piece = src_ref[:, r * ATTN_W + c * BLK:r * ATTN_W + (c + 1) * BLK]
            if accumulate:
                dst[c, put, :] += piece
            else:
                dst[c, put, :] = piece


def _residue_spec(tm, dil):
    return pl.BlockSpec((tm // dil, dil * ATTN_W), lambda i: (i, 0))


def _residue_shape(seq, dil, dtype):
    return jax.ShapeDtypeStruct((seq // dil, dil * ATTN_W), dtype)


def _fwd_in(x, g_pre, w_in_g, cos_t, sin_t, tm=256):
    seq = x.shape[0]
    n_dil = len(DILATIONS)

    def body(x_ref, g_ref, w_ref, cos_ref, sin_ref, *rest):
        qkv_refs, (zr_ref, ht_ref, z_scr, qkv_scr) = rest[:3 * n_dil], rest[3 * n_dil:]
        xv = x_ref[...]
        r = lax.rsqrt(jnp.mean(xv * xv, axis=-1, keepdims=True) + NORM_EPS)
        hf = (xv * r) * g_ref[...]
        ht_ref[...] = hf.T.astype(BF16)
        h = hf.astype(BF16)
        for j in range(N_DEV):
            z_scr[:, j * SHARD_IN:(j + 1) * SHARD_IN] = jnp.dot(h, w_ref[j], preferred_element_type=F32)
        cos, sin = cos_ref[...], sin_ref[...]
        first_half = (lax.broadcasted_iota(jnp.int32, (tm, BLK), 1) & 32) == 0

        def rope(t):
            return t * cos + _swap_halves(t, first_half) * sin

        for c in range(N_CHUNK):
            qkv_scr[c] = rope(z_scr[:, c * BLK:(c + 1) * BLK]) * HEAD_DIM ** -0.5
            qkv_scr[N_CHUNK + c] = rope(z_scr[:, ATTN_W + c * BLK:ATTN_W + (c + 1) * BLK])
            qkv_scr[2 * N_CHUNK + c] = z_scr[:, 2 * ATTN_W + c * BLK:2 * ATTN_W + (c + 1) * BLK]
        for n, dil in enumerate(DILATIONS):
            for a in range(3):
                _to_residue(qkv_scr, a * N_CHUNK, qkv_refs[3 * n + a], dil, tm, BF16)
        zr_ref[...] = z_scr[:, 3 * ATTN_W:]

    row = lambda w: pl.BlockSpec((tm, w), lambda i: (i, 0))
    outs = pl.pallas_call(
        body, name="fwd_in", grid=(seq // tm,),
        in_specs=[row(D_MODEL), _resident((1, D_MODEL)), _resident((N_DEV, D_MODEL, SHARD_IN)), row(BLK), row(BLK)],
        out_specs=[_residue_spec(tm, dil) for dil in DILATIONS for _ in range(3)]
        + [row(REST_W), pl.BlockSpec((D_MODEL, tm), lambda i: (0, i))],
        out_shape=[_residue_shape(seq, dil, BF16) for dil in DILATIONS for _ in range(3)]
        + [jax.ShapeDtypeStruct((seq, REST_W), F32), jax.ShapeDtypeStruct((D_MODEL, seq), BF16)],
        scratch_shapes=[pltpu.VMEM((tm, IN_W), F32), pltpu.VMEM((3 * N_CHUNK, tm, BLK), F32)],
        compiler_params=_params(1),
    )(x, g_pre.reshape(1, D_MODEL), w_in_g, cos_t, sin_t)
    qkv = [tuple(outs[3 * n:3 * n + 3]) for n in range(n_dil)]
    return qkv, outs[3 * n_dil], outs[3 * n_dil + 1]


def _band_mask(first_block):
    qi = lax.broadcasted_iota(jnp.int32, (2 * BLK, 2 * BLK), 0) & (BLK - 1)
    kj = lax.broadcasted_iota(jnp.int32, (2 * BLK, 2 * BLK), 1)
    valid = (kj >= qi) & (kj <= qi + BLK)
    if first_block is not None:
        valid = valid & ((kj >= BLK) | jnp.logical_not(first_block))
    return valid


def _stack_heads(t, head0):
    keep0 = head0.astype(F32).astype(BF16)
    return jnp.concatenate([t * keep0, t * (1 - keep0)], axis=0)


def _unstack_heads(t2, head0):
    return jnp.where(head0, t2[:BLK], t2[BLK:])


def _rows_per_head(a, head0):
    b = pltpu.roll(a, HEAD_DIM, 1)
    rows = jnp.concatenate([jnp.where(head0, a, b), jnp.where(head0, b, a)], axis=0)
    return jnp.concatenate([rows, rows], axis=1)


def _attn_specs(length, dil):
    tb = min(4, length // BLK)
    tile = pl.BlockSpec((tb * BLK, BLK), lambda c, t: (t, c))
    prev = pl.BlockSpec((BLK, BLK), lambda c, t: (jnp.maximum(t * tb - 1, 0), c))
    grid = (dil * ATTN_W // BLK, length // (tb * BLK))
    return tb, tile, prev, grid


def _attn_fwd(q, k, v, dil):
    length = q.shape[0]
    tb, tile, prev, grid = _attn_specs(length, dil)

    def body(q_ref, kc_ref, kp_ref, vc_ref, vp_ref, o_ref, lse_ref, kcat, vcat):
        t = pl.program_id(1)
        kcat[0:BLK] = kp_ref[...]
        kcat[BLK:] = kc_ref[...]
        vcat[0:BLK] = vp_ref[...]
        vcat[BLK:] = vc_ref[...]
        head0 = lax.broadcasted_iota(jnp.int32, (BLK, BLK), 1) < HEAD_DIM
        for j in range(tb):
            valid = _band_mask(t == 0 if j == 0 else None)
            q2 = _stack_heads(q_ref[j * BLK:(j + 1) * BLK, :], head0)
            kk = kcat[j * BLK:(j + 2) * BLK, :]
            s = lax.dot_general(q2, kk, (((1,), (1,)), ((), ())), preferred_element_type=F32)
            s = jnp.where(valid, s, NEG)
            m = jnp.max(s, axis=1, keepdims=True)
            p = jnp.exp(s - m)
            den = jnp.sum(p, axis=1, keepdims=True)
            o2 = jnp.dot(p.astype(BF16), vcat[j * BLK:(j + 2) * BLK, :], preferred_element_type=F32) / den
            lse2 = jnp.broadcast_to(m + jnp.log(den), (2 * BLK, BLK))
            o_ref[j * BLK:(j + 1) * BLK, :] = _unstack_heads(o2, head0)
            lse_ref[j * BLK:(j + 1) * BLK, :] = _unstack_heads(lse2, head0)

    return pl.pallas_call(
        body, name=f"attn_fwd_d{dil}", grid=grid,
        in_specs=[tile, tile, prev, tile, prev], out_specs=[tile, tile],
        out_shape=[jax.ShapeDtypeStruct(q.shape, F32)] * 2,
        scratch_shapes=[pltpu.VMEM(((tb + 1) * BLK, BLK), BF16)] * 2,
        compiler_params=_params(2),
    )(q, k, k, v, v)


def _attn_bwd(q, k, v, do, lse, delta, dil):
    length = q.shape[0]
    tb, tile, prev, grid = _attn_specs(length, dil)
    whole = pl.BlockSpec((length, BLK), lambda c, t: (0, c))

    def body(q_ref, do_ref, lse_ref, dl_ref, kc_ref, kp_ref, vc_ref, vp_ref, dq_ref, dk_ref, dv_ref, kcat, vcat):
        t = pl.program_id(1)
        kcat[0:BLK] = kp_ref[...]
        kcat[BLK:] = kc_ref[...]
        vcat[0:BLK] = vp_ref[...]
        vcat[BLK:] = vc_ref[...]
        head0 = lax.broadcasted_iota(jnp.int32, (BLK, BLK), 1) < HEAD_DIM
        for j in range(tb):
            rows = slice(j * BLK, (j + 1) * BLK)
            valid = _band_mask(t == 0 if j == 0 else None)
            q2 = _stack_heads(q_ref[rows, :], head0)
            do2 = _stack_heads(do_ref[rows, :], head0)
            kk = kcat[j * BLK:(j + 2) * BLK, :]
            vv = vcat[j * BLK:(j + 2) * BLK, :]
            s = lax.dot_general(q2, kk, (((1,), (1,)), ((), ())), preferred_element_type=F32)
            p = jnp.where(valid, jnp.exp(s - _rows_per_head(lse_ref[rows, :], head0)), 0.0)
            dp = lax.dot_general(do2, vv, (((1,), (1,)), ((), ())), preferred_element_type=F32)
            ds = (p * (dp - _rows_per_head(dl_ref[rows, :], head0))).astype(BF16)
            dq2 = jnp.dot(ds, kk, preferred_element_type=F32)
            dq_ref[rows, :] = _unstack_heads(dq2, head0) * HEAD_DIM ** -0.5
            dk2 = lax.dot_general(ds, q2, (((0,), (0,)), ((), ())), preferred_element_type=F32)
            dv2 = lax.dot_general(p.astype(BF16), do2, (((0,), (0,)), ((), ())), preferred_element_type=F32)
            own = pl.ds(pl.multiple_of((t * tb + j) * BLK, BLK), BLK)
            dk_ref[own, :] = dk2[BLK:]
            dv_ref[own, :] = dv2[BLK:]

            def add_to_previous(j=j, dk2=dk2, dv2=dv2):
                before = pl.ds(pl.multiple_of((t * tb + j - 1) * BLK, BLK), BLK)
                dk_ref[before, :] += dk2[:BLK]
                dv_ref[before, :] += dv2[:BLK]

            if j == 0:
                pl.when(t > 0)(add_to_previous)
            else:
                add_to_previous()

    return pl.pallas_call(
        body, name=f"attn_bwd_d{dil}", grid=grid,
        in_specs=[tile, tile, tile, tile, tile, prev, tile, prev], out_specs=[tile, whole, whole],
        out_shape=[jax.ShapeDtypeStruct(q.shape, F32)] * 3,
        scratch_shapes=[pltpu.VMEM(((tb + 1) * BLK, BLK), BF16)] * 2,
        compiler_params=_params(2),
    )(q, do, lse, delta, k, k, v, v)


def _conv_taps(u, before8, tm):
    row = lax.broadcasted_iota(jnp.int32, u.shape, 0)
    last, last2 = before8[7:8, :], before8[6:7, :]
    u1 = jnp.where(row == 0, last, pltpu.roll(u, 1, 0))
    u2 = jnp.where(row == 0, last2, jnp.where(row == 1, last, pltpu.roll(u, 2, 0)))
    return u1, u2


def _attn_combine(o_parts, lse_parts, zr, conv_w, tm=256):
    seq = zr.shape[0]
    a0, h0, b0, c0, g0 = 0, ATTN_W, ATTN_W + CONV_W, ATTN_W + 2 * CONV_W, ATTN_W + 3 * CONV_W

    def body(o1, o2, o3, l1, l2, l3, zr_ref, zp_ref, w_ref, mixed_ref, o_ref, lse1, lse2, lse3, *scr):
        i = pl.program_id(0)
        for src, dst, dil in zip((o2, o3, l2, l3), scr[:4], DILATIONS[1:] * 2):
            _from_residue(src, dst, dil, tm, accumulate=False)
        for c in range(N_CHUNK):
            cols = slice(c * BLK, (c + 1) * BLK)
            la, lb, lc = l1[:, cols], scr[2][c], scr[3][c]
            top = jnp.maximum(jnp.maximum(la, lb), lc)
            ea, eb, ec = jnp.exp(la - top), jnp.exp(lb - top), jnp.exp(lc - top)
            den = ea + eb + ec
            o = (ea / den) * o1[:, cols] + (eb / den) * scr[0][c] + (ec / den) * scr[1][c]
            o_ref[:, cols] = o
            scr[4][c] = top + jnp.log(den)
            ga = zr_ref[:, cols]
            mixed_ref[:, cols] = (o * (ga * _sigmoid(ga))).astype(BF16)
        for dst, dil in zip((lse1, lse2, lse3), DILATIONS):
            _to_residue(scr[4], 0, dst, dil, tm, F32)
        u = zr_ref[:, c0:g0] * zr_ref[:, h0:b0]
        before = jnp.where(i > 0, zp_ref[:, c0:g0] * zp_ref[:, h0:b0], 0.0)
        u1, u2 = _conv_taps(u, before, tm)
        y = u2 * w_ref[0:1, :] + u1 * w_ref[1:2, :] + u * w_ref[2:3, :]
        gc = zr_ref[:, g0:]
        mixed_ref[:, ATTN_W:] = ((zr_ref[:, b0:c0] * y) * (gc * _sigmoid(gc))).astype(BF16)

    row = lambda w: pl.BlockSpec((tm, w), lambda i: (i, 0))
    before8 = pl.BlockSpec((8, REST_W), lambda i: (jnp.maximum(i * (tm // 8) - 1, 0), 0))
    views = [_residue_spec(tm, dil) for dil in DILATIONS]
    outs = pl.pallas_call(
        body, name="attn_combine", grid=(seq // tm,),
        in_specs=views * 2 + [row(REST_W), before8, _resident((3, CONV_W))],
        out_specs=[row(D_MODEL), row(ATTN_W)] + views,
        out_shape=[jax.ShapeDtypeStruct((seq, D_MODEL), BF16), jax.ShapeDtypeStruct((seq, ATTN_W), F32)]
        + [_residue_shape(seq, dil, F32) for dil in DILATIONS],
        scratch_shapes=[pltpu.VMEM((N_CHUNK, tm, BLK), F32)] * 5,
        compiler_params=_params(1),
    )(*o_parts, *lse_parts, zr, zr, conv_w)
    return outs[0], outs[1], outs[2:]


def _out_loss_bwd(mixed, w_out_g, x, target, g_post, tm=256):
    seq = x.shape[0]

    def body(mx_ref, w_ref, x_ref, t_ref, g_ref, dout_ref, dy_ref, dmx_ref, dw_ref, st_ref):
        i = pl.program_id(0)
        mx = mx_ref[...]
        y = jnp.dot(mx, w_ref[...], preferred_element_type=F32)
        r = lax.rsqrt(jnp.mean(y * y, axis=-1, keepdims=True) + NORM_EPS)
        yhat = y * r
        g = g_ref[...]
        err = (x_ref[...] + yhat * g) - t_ref[...]
        dn = err * (1.0 / D_MODEL)
        dout_ref[...] = dn
        tg = dn * g
        dy = (r * (tg - yhat * jnp.mean(tg * yhat, axis=-1, keepdims=True))).astype(BF16)
        dy_ref[...] = dy
        dmx_ref[...] = lax.dot_general(dy, w_ref[...], (((1,), (1,)), ((), ())), preferred_element_type=F32)
        dw = lax.dot_general(mx, dy, (((0,), (0,)), ((), ())), preferred_element_type=F32)
        gsum = jnp.sum(dn * yhat, axis=0, keepdims=True)
        lsum = jnp.broadcast_to(0.5 / D_MODEL * jnp.sum(err * err), (1, D_MODEL))

        @pl.when(i == 0)
        def _():
            dw_ref[...] = dw
            st_ref[...] = jnp.zeros_like(st_ref)
            st_ref[0:1, :] = gsum
            st_ref[1:2, :] = lsum

        @pl.when(i > 0)
        def _():
            dw_ref[...] += dw
            st_ref[0:1, :] += gsum
            st_ref[1:2, :] += lsum

    row = lambda w: pl.BlockSpec((tm, w), lambda i: (i, 0))
    return pl.pallas_call(
        body, name="out_loss_bwd", grid=(seq // tm,),
        in_specs=[row(D_MODEL), _resident((D_MODEL, D_MODEL)), row(D_MODEL), row(D_MODEL), _resident((1, D_MODEL))],
        out_specs=[row(D_MODEL), row(D_MODEL), row(D_MODEL), pl.BlockSpec((D_MODEL, D_MODEL), lambda i: (0, 0)),
                   pl.BlockSpec((8, D_MODEL), lambda i: (0, 0))],
        out_shape=[jax.ShapeDtypeStruct((seq, D_MODEL), F32), jax.ShapeDtypeStruct((seq, D_MODEL), BF16),
                   jax.ShapeDtypeStruct((seq, D_MODEL), F32), jax.ShapeDtypeStruct((D_MODEL, D_MODEL), F32),
                   jax.ShapeDtypeStruct((8, D_MODEL), F32)],
        compiler_params=_params(1),
    )(mixed, w_out_g, x, target, g_post.reshape(1, D_MODEL))


def _head_sum(prod, same_head):
    hi = prod.astype(BF16)
    lo = (prod - hi.astype(F32)).astype(BF16)
    return (jnp.dot(hi, same_head, preferred_element_type=F32) + jnp.dot(lo, same_head, preferred_element_type=F32))


def _gate_bwd(dmixed, zr, o, conv_w, tm=256):
    seq = zr.shape[0]
    n_tiles = seq // tm
    n_dil = len(DILATIONS)
    a0, h0, b0, c0, g0 = 0, ATTN_W, ATTN_W + CONV_W, ATTN_W + 2 * CONV_W, ATTN_W + 3 * CONV_W

    def body(dm_ref, dmn_ref, zr_ref, zp_ref, zn_ref, o_ref, w_ref, *rest):
        do_refs, dl_refs = rest[:n_dil], rest[n_dil:2 * n_dil]
        dz_ref, dw_ref, do_scr, dl_scr = rest[2 * n_dil:]
        i = pl.program_id(0)
        ga = zr_ref[:, a0:h0]
        sg = _sigmoid(ga)
        dattn = dm_ref[:, 0:ATTN_W]
        ov = o_ref[...]
        do = dattn * (ga * sg)
        dz_ref[:, a0:h0] = (dattn * ov * (sg * (1.0 + ga * (1.0 - sg)))).astype(BF16)
        li = lax.broadcasted_iota(jnp.int32, (BLK, BLK), 0) // HEAD_DIM
        lj = lax.broadcasted_iota(jnp.int32, (BLK, BLK), 1) // HEAD_DIM
        same_head = (li == lj).astype(BF16)
        prod = do * ov
        for c in range(N_CHUNK):
            cols = slice(c * BLK, (c + 1) * BLK)
            do_scr[c] = do[:, cols]
            dl_scr[c] = _head_sum(prod[:, cols], same_head)
        for n, dil in enumerate(DILATIONS):
            _to_residue(do_scr, 0, do_refs[n], dil, tm, BF16)
            _to_residue(dl_scr, 0, dl_refs[n], dil, tm, F32)

        ch, cb, cc, gc = zr_ref[:, h0:b0], zr_ref[:, b0:c0], zr_ref[:, c0:g0], zr_ref[:, g0:]
        u = cc * ch
        before = jnp.where(i > 0, zp_ref[:, c0:g0] * zp_ref[:, h0:b0], 0.0)
        u1, u2 = _conv_taps(u, before, tm)
        w0, w1, w2 = w_ref[0:1, :], w_ref[1:2, :], w_ref[2:3, :]
        y = u2 * w0 + u1 * w1 + u * w2
        sc = _sigmoid(gc)
        silu_c = gc * sc
        dconv = dm_ref[:, ATTN_W:]
        dz_ref[:, b0:c0] = (dconv * y * silu_c).astype(BF16)
        dz_ref[:, g0:] = (dconv * (cb * y) * (sc * (1.0 + gc * (1.0 - sc)))).astype(BF16)
        dy = dconv * cb * silu_c
        gn = zn_ref[:, g0:]
        after = jnp.where(i < n_tiles - 1, dmn_ref[:, ATTN_W:] * zn_ref[:, b0:c0] * (gn * _sigmoid(gn)), 0.0)
        row = lax.broadcasted_iota(jnp.int32, dy.shape, 0)
        nxt, nxt2 = after[0:1, :], after[1:2, :]
        dy1 = jnp.where(row == tm - 1, nxt, pltpu.roll(dy, tm - 1, 0))
        dy2 = jnp.where(row == tm - 1, nxt2, jnp.where(row == tm - 2, nxt, pltpu.roll(dy, tm - 2, 0)))
        du = dy * w2 + dy1 * w1 + dy2 * w0
        dz_ref[:, c0:g0] = (du * ch).astype(BF16)
        dz_ref[:, h0:b0] = (du * cc).astype(BF16)
        dws = [jnp.sum(dy * u2, axis=0, keepdims=True), jnp.sum(dy * u1, axis=0, keepdims=True),
               jnp.sum(dy * u, axis=0, keepdims=True)]

        @pl.when(i == 0)
        def _():
            dw_ref[...] = jnp.zeros_like(dw_ref)

        for n, part in enumerate(dws):
            dw_ref[n:n + 1, :] += part

    row_spec = lambda w: pl.BlockSpec((tm, w), lambda i: (i, 0))
    before8 = pl.BlockSpec((8, REST_W), lambda i: (jnp.maximum(i * (tm // 8) - 1, 0), 0))
    after8 = lambda w: pl.BlockSpec((8, w), lambda i: (jnp.minimum((i + 1) * (tm // 8), seq // 8 - 1), 0))
    views = [_residue_spec(tm, dil) for dil in DILATIONS]
    outs = pl.pallas_call(
        body, name="gate_bwd", grid=(n_tiles,),
        in_specs=[row_spec(D_MODEL), after8(D_MODEL), row_spec(REST_W), before8, after8(REST_W), row_spec(ATTN_W),
                  _resident((3, CONV_W))],
        out_specs=views * 2 + [row_spec(REST_W), pl.BlockSpec((8, CONV_W), lambda i: (0, 0))],
        out_shape=[_residue_shape(seq, dil, BF16) for dil in DILATIONS]
        + [_residue_shape(seq, dil, F32) for dil in DILATIONS]
        + [jax.ShapeDtypeStruct((seq, REST_W), BF16), jax.ShapeDtypeStruct((8, CONV_W), F32)],
        scratch_shapes=[pltpu.VMEM((N_CHUNK, tm, BLK), F32)] * 2,
        compiler_params=_params(1),
    )(dmixed, dmixed, zr, zr, zr, o, conv_w)
    return outs[:n_dil], outs[n_dil:2 * n_dil], outs[2 * n_dil], outs[2 * n_dil + 1]


def _in_bwd(dqs, dks, dvs, dzr, cos_t, sin_t, x, d_out, g_pre, w_in_g, tm=256):
    seq = x.shape[0]

    def body(q1, q2, q3, k1, k2, k3, v1, v2, v3, dzr_ref, cos_ref, sin_ref, x_ref, dout_ref, g_ref, w_ref,
             dz_ref, gx_ref, st_ref, dq_scr, dk_scr, dv_scr):
        i = pl.program_id(0)
        for parts, total in (((q1, q2, q3), dq_scr), ((k1, k2, k3), dk_scr), ((v1, v2, v3), dv_scr)):
            for n, dil in enumerate(DILATIONS):
                _from_residue(parts[n], total, dil, tm, accumulate=n > 0)
        cos, sin = cos_ref[...], sin_ref[...]
        first_half = (lax.broadcasted_iota(jnp.int32, (tm, BLK), 1) & 32) == 0

        def unrope(t):
            return t * cos - _swap_halves(t, first_half) * sin

        for c in range(N_CHUNK):
            dz_ref[:, c * BLK:(c + 1) * BLK] = unrope(dq_scr[c]).astype(BF16)
            dz_ref[:, ATTN_W + c * BLK:ATTN_W + (c + 1) * BLK] = unrope(dk_scr[c]).astype(BF16)
            dz_ref[:, 2 * ATTN_W + c * BLK:2 * ATTN_W + (c + 1) * BLK] = dv_scr[c].astype(BF16)
        dz_ref[:, 3 * ATTN_W:] = dzr_ref[...]
        dh = jnp.zeros((tm, D_MODEL), F32)
        for j in range(N_DEV):
            dh += lax.dot_general(dz_ref[:, j * SHARD_IN:(j + 1) * SHARD_IN], w_ref[j], (((1,), (1,)), ((), ())),
                                  preferred_element_type=F32)
        xv = x_ref[...]
        r = lax.rsqrt(jnp.mean(xv * xv, axis=-1, keepdims=True) + NORM_EPS)
        xhat = xv * r
        tg = dh * g_ref[...]
        gx_ref[...] = dout_ref[...] + r * (tg - xhat * jnp.mean(tg * xhat, axis=-1, keepdims=True))
        gsum = jnp.sum(dh * xhat, axis=0, keepdims=True)

        @pl.when(i == 0)
        def _():
            st_ref[...] = jnp.zeros_like(st_ref)

        st_ref[0:1, :] += gsum

    row = lambda w: pl.BlockSpec((tm, w), lambda i: (i, 0))
    return pl.pallas_call(
        body, name="in_bwd", grid=(seq // tm,),
        in_specs=[_residue_spec(tm, dil) for dil in DILATIONS] * 3
        + [row(REST_W), row(BLK), row(BLK), row(D_MODEL), row(D_MODEL), _resident((1, D_MODEL)),
           _resident((N_DEV, D_MODEL, SHARD_IN))],
        out_specs=[row(IN_W), row(D_MODEL), pl.BlockSpec((8, D_MODEL), lambda i: (0, 0))],
        out_shape=[jax.ShapeDtypeStruct((seq, IN_W), BF16), jax.ShapeDtypeStruct((seq, D_MODEL), F32),
                   jax.ShapeDtypeStruct((8, D_MODEL), F32)],
        scratch_shapes=[pltpu.VMEM((N_CHUNK, tm, BLK), F32)] * 3,
        compiler_params=_params(1),
    )(*dqs, *dks, *dvs, dzr, cos_t, sin_t, x, d_out, g_pre.reshape(1, D_MODEL), w_in_g)


def _local_step(x, target, g_pre, g_post, w_in_g, w_out_g, conv_w):
    seq = x.shape[0]
    cos_t, sin_t = _rope_tables(seq)
    qkv, zr, ht = _fwd_in(x, g_pre, w_in_g, cos_t, sin_t)
    parts = [_attn_fwd(*qkv[n], dil) for n, dil in enumerate(DILATIONS)]
    mixed, o, lse = _attn_combine([p[0] for p in parts], [p[1] for p in parts], zr, conv_w)
    d_out, dy, dmixed, dw_out, st_post = _out_loss_bwd(mixed, w_out_g, x, target, g_post)
    do, delta, dzr, dconv = _gate_bwd(dmixed, zr, o, conv_w)
    grads = [_attn_bwd(*qkv[n], do[n], lse[n], delta[n], dil) for n, dil in enumerate(DILATIONS)]
    dz, grad_x, st_pre = _in_bwd([g[0] for g in grads], [g[1] for g in grads], [g[2] for g in grads], dzr,
                                 cos_t, sin_t, x, d_out, g_pre, w_in_g)
    conv_rows = jnp.pad(dconv[0:3], ((0, 0), (0, D_MODEL - CONV_W)))
    small = jnp.concatenate([st_pre[0:1], st_post[0:2], conv_rows, jnp.zeros((2, D_MODEL), F32)], axis=0)
    return grad_x, ht, dz, dw_out, small


def _coords():
    return lax.axis_index("x"), lax.axis_index("y"), lax.axis_index("c")


def _peer(k):
    x, y, c = _coords()
    px = 1 - x if k & 4 else x
    py = 1 - y if k & 2 else y
    pc = 1 - c if k & 1 else c
    return (px, py, pc), 4 * px + 2 * py + pc


HBM_SPEC = pl.BlockSpec(memory_space=pltpu.HBM)
VMEM_SPEC = pl.BlockSpec(memory_space=pltpu.VMEM)


def _ag_weights(w_in, w_out, conv_w):
    def body(win_ref, wout_ref, cw_ref, gin_ref, gout_ref, gcw_ref, win_bf, wout_bf, cw_pad, send_sems, recv_sems,
             local_sems):
        x, y, c = _coords()
        me, sibling = (x, y, c), (x, y, 1 - c)
        chips = [(1 - x, y), (x, 1 - y), (1 - x, 1 - y)]
        slab = lambda px, py, pc: 4 * px + 2 * py + pc
        win_bf[...] = win_ref[...].astype(BF16)
        wout_bf[...] = wout_ref[...].astype(BF16)
        cw_pad[...] = jnp.zeros_like(cw_pad)
        cw_pad[0:3, 0:CONV_W // N_DEV] = cw_ref[...]
        mine = [win_bf, wout_bf, cw_pad]
        gathered = [gin_ref, gout_ref, gcw_ref]

        def copies(k, block, to, own=False):
            return [pltpu.make_async_remote_copy(src_ref=mine[a] if own else gathered[a].at[slab(*block)],
                                                 dst_ref=gathered[a].at[slab(*block)], send_sem=send_sems.at[k, a],
                                                 recv_sem=recv_sems.at[k, a], device_id=to, device_id_type=MESH)
                    for a in range(3)]

        local = [pltpu.make_async_copy(mine[a], gathered[a].at[slab(*me)], local_sems.at[a]) for a in range(3)]
        for cp in local:
            cp.start()
        first = copies(0, me, sibling, own=True)
        for j, chip in enumerate(chips):
            first += copies(1 + j, me, (*chip, c), own=True)
        for cp in first:
            cp.start()
        passed = []
        for j, chip in enumerate(chips):
            for cp in copies(1 + j, (*chip, c), me):
                cp.wait_recv()
            onward = copies(4 + j, (*chip, c), sibling)
            for cp in onward:
                cp.start()
            passed += onward
        for cp in copies(0, sibling, me):
            cp.wait_recv()
        for j, chip in enumerate(chips):
            for cp in copies(4 + j, (*chip, 1 - c), me):
                cp.wait_recv()
        for cp in first + passed:
            cp.wait_send()
        for cp in local:
            cp.wait()

    return pl.pallas_call(
        body, name="ag_weights",
        in_specs=[VMEM_SPEC, VMEM_SPEC, VMEM_SPEC], out_specs=[HBM_SPEC, HBM_SPEC, HBM_SPEC],
        out_shape=[jax.ShapeDtypeStruct((N_DEV, D_MODEL, SHARD_IN), BF16),
                   jax.ShapeDtypeStruct((N_DEV, SHARD_OUT, D_MODEL), BF16),
                   jax.ShapeDtypeStruct((N_DEV, 8, BLK), F32)],
        scratch_shapes=[pltpu.VMEM((D_MODEL, SHARD_IN), BF16), pltpu.VMEM((SHARD_OUT, D_MODEL), BF16),
                        pltpu.VMEM((8, BLK), F32), pltpu.SemaphoreType.DMA((N_DEV - 1, 3)),
                        pltpu.SemaphoreType.DMA((N_DEV - 1, 3)), pltpu.SemaphoreType.DMA((3,))],
        compiler_params=pltpu.CompilerParams(vmem_limit_bytes=VMEM_LIMIT),
    )(w_in, w_out, conv_w)


SEND_ORDER = (2, 4, 6, 3, 5, 7, 1)


def _dw_in_rs(ht, dz, dw_out, small):
    seq = dz.shape[0]

    def body(cols_ref, ht_ref, dz_ref, dout_ref, sm_ref, own_ref, rin_ref, rout_ref, rsm_ref, sendbuf, zero_buf,
             send_sems, recv_sems, local_sems):
        del cols_ref
        step = pl.program_id(0)
        x, y, c = _coords()
        me = 4 * x + 2 * y + c

        def copies_to(n, k, which=(0, 1, 2)):
            peer, peer_idx = _peer(k)
            pairs = [(sendbuf.at[n], rin_ref), (dout_ref.at[peer_idx], rout_ref), (sm_ref, rsm_ref)]
            return [pltpu.make_async_remote_copy(src_ref=pairs[a][0], dst_ref=pairs[a][1].at[me],
                                                 send_sem=send_sems.at[n, a], recv_sem=recv_sems.at[n, a],
                                                 device_id=peer, device_id_type=MESH) for a in which]

        def arrivals_from(n, k):
            peer, peer_idx = _peer(k)
            pairs = [(sendbuf.at[n], rin_ref), (dout_ref.at[me], rout_ref), (sm_ref, rsm_ref)]
            return [pltpu.make_async_remote_copy(src_ref=src, dst_ref=dst.at[peer_idx], send_sem=send_sems.at[n, a],
                                                 recv_sem=recv_sems.at[n, a], device_id=peer, device_id_type=MESH)
                    for a, (src, dst) in enumerate(pairs)]

        local = [pltpu.make_async_copy(zero_buf, rin_ref.at[me], local_sems.at[0]),
                 pltpu.make_async_copy(dout_ref.at[me], rout_ref.at[me], local_sems.at[1]),
                 pltpu.make_async_copy(sm_ref, rsm_ref.at[me], local_sems.at[2])]

        @pl.when(step == 0)
        def _():
            zero_buf[...] = jnp.zeros_like(zero_buf)
            for cp in local:
                cp.start()
            for n, k in enumerate(SEND_ORDER):
                for cp in copies_to(n, k, which=(1, 2)):
                    cp.start()

        dw = jnp.dot(ht_ref[...], dz_ref[...], preferred_element_type=F32)
        for n, k in enumerate(SEND_ORDER):
            @pl.when(step == n)
            def _(n=n, k=k):
                sendbuf[n] = dw.astype(BF16)
                copies_to(n, k, which=(0,))[0].start()

        @pl.when(step == N_DEV - 1)
        def _():
            own_ref[...] = dw
            for n, k in enumerate(SEND_ORDER):
                for cp in arrivals_from(n, k):
                    cp.wait_recv()
            for n, k in enumerate(SEND_ORDER):
                for cp in copies_to(n, k):
                    cp.wait_send()
            for cp in local:
                cp.wait()

    peers = [_peer(k)[1] for k in SEND_ORDER]
    me = 4 * lax.axis_index("x") + 2 * lax.axis_index("y") + lax.axis_index("c")
    cols = jnp.stack(peers + [me]).astype(jnp.int32)
    grid_spec = pltpu.PrefetchScalarGridSpec(
        num_scalar_prefetch=1, grid=(N_DEV,),
        in_specs=[pl.BlockSpec((D_MODEL, seq), lambda s, cols: (0, 0), pipeline_mode=pl.Buffered(1)),
                  pl.BlockSpec((seq, SHARD_IN), lambda s, cols: (0, cols[s])), HBM_SPEC, HBM_SPEC],
        out_specs=[pl.BlockSpec((D_MODEL, SHARD_IN), lambda s, cols: (0, 0)), HBM_SPEC, HBM_SPEC, HBM_SPEC],
        scratch_shapes=[pltpu.VMEM((N_DEV - 1, D_MODEL, SHARD_IN), BF16), pltpu.VMEM((D_MODEL, SHARD_IN), BF16),
                        pltpu.SemaphoreType.DMA((N_DEV - 1, 3)), pltpu.SemaphoreType.DMA((N_DEV - 1, 3)),
                        pltpu.SemaphoreType.DMA((3,))])
    return pl.pallas_call(
        body, name="dw_in_rs", grid_spec=grid_spec,
        out_shape=[jax.ShapeDtypeStruct((D_MODEL, SHARD_IN), F32),
                   jax.ShapeDtypeStruct((N_DEV, D_MODEL, SHARD_IN), BF16),
                   jax.ShapeDtypeStruct((N_DEV, SHARD_OUT, D_MODEL), F32),
                   jax.ShapeDtypeStruct((N_DEV, 8, D_MODEL), F32)],
        compiler_params=_params(1),
    )(cols, ht, dz, dw_out, small)


def _adamw_math(w, g, m, v):
    m = ADAM_B1 * m + (1.0 - ADAM_B1) * g
    v = ADAM_B2 * v + (1.0 - ADAM_B2) * (g * g)
    m_hat = m / (1.0 - ADAM_B1 ** ADAM_STEP)
    v_hat = v / (1.0 - ADAM_B2 ** ADAM_STEP)
    delta = -ADAM_LR * (m_hat / (jnp.sqrt(v_hat) + ADAM_EPS) + ADAM_WD * w)
    return delta, m, v


def _sum_slabs(ref, first=None):
    total = ref[0].astype(F32) if first is None else first + ref[0].astype(F32)
    for s in range(1, N_DEV):
        total = total + ref[s].astype(F32)
    return total


def _adamw_slabs(parts, own, w, m, v, name, tr):
    rows, cols = w.shape
    tile = pl.BlockSpec((tr, cols), lambda i: (i, 0))

    def body(p_ref, *refs):
        own_ref = refs[0] if own is not None else None
        w_ref, m_ref, v_ref, g_ref, d_ref, nm_ref, nv_ref = refs[-7:]
        g = _sum_slabs(p_ref, None if own_ref is None else own_ref[...])
        g_ref[...] = g
        d_ref[...], nm_ref[...], nv_ref[...] = _adamw_math(w_ref[...], g, m_ref[...], v_ref[...])

    extra = [] if own is None else [own]
    return pl.pallas_call(
        body, name=name, grid=(rows // tr,),
        in_specs=[pl.BlockSpec((N_DEV, tr, cols), lambda i: (0, i, 0))] + [tile] * (len(extra) + 3),
        out_specs=[tile] * 4,
        out_shape=[jax.ShapeDtypeStruct((rows, cols), F32)] * 4,
        compiler_params=_params(1),
    )(parts, *extra, w, m, v)


def _sum_small(parts):
    def body(p_ref, out_ref):
        out_ref[...] = _sum_slabs(p_ref)

    return pl.pallas_call(body, name="sum_small", out_shape=jax.ShapeDtypeStruct(parts.shape[1:], F32))(parts)


def _adamw_whole(g, w, m, v, name):
    def body(g_ref, w_ref, m_ref, v_ref, d_ref, nm_ref, nv_ref):
        d_ref[...], nm_ref[...], nv_ref[...] = _adamw_math(w_ref[...], g_ref[...], m_ref[...], v_ref[...])

    return pl.pallas_call(body, name=name, out_shape=[jax.ShapeDtypeStruct(w.shape, F32)] * 3)(g, w, m, v)


def kernel(x, norm_pre_g, w_in, conv_w, w_out, norm_post_g, loss_target, m_norm_pre_g, m_w_in, m_conv_w, m_w_out,
           m_norm_post_g, v_norm_pre_g, v_w_in, v_conv_w, v_w_out, v_norm_post_g):
    n_conv = CONV_W // N_DEV
    w_in_g, w_out_g, conv_g = _ag_weights(w_in, w_out, conv_w)
    conv_full = conv_g[:, 0:3, 0:n_conv].transpose(1, 0, 2).reshape(3, CONV_W)
    grad_x, ht, dz, dw_out, small = _local_step(x[0], loss_target[0], norm_pre_g, norm_post_g, w_in_g,
                                                w_out_g.reshape(D_MODEL, D_MODEL), conv_full)
    own_in, r_in, r_out, r_small = _dw_in_rs(ht, dz, dw_out.reshape(N_DEV, SHARD_OUT, D_MODEL), small)
    g_in, d_in, nm_in, nv_in = _adamw_slabs(r_in, own_in, w_in, m_w_in, v_w_in, "adamw_in", 256)
    g_out, d_out, nm_out, nv_out = _adamw_slabs(r_out, None, w_out, m_w_out, v_w_out, "adamw_out", SHARD_OUT)
    sums = _sum_small(r_small)
    g_pre, g_post, loss = sums[0], sums[1], sums[2, 0]
    me = 4 * lax.axis_index("x") + 2 * lax.axis_index("y") + lax.axis_index("c")
    g_conv = lax.dynamic_slice(sums[3:6, 0:CONV_W], (0, me * n_conv), (3, n_conv))
    vec = lambda a: a.reshape(1, D_MODEL)
    d_pre, nm_pre, nv_pre = _adamw_whole(vec(g_pre), vec(norm_pre_g), vec(m_norm_pre_g), vec(v_norm_pre_g), "adamw_pre")
    d_post, nm_post, nv_post = _adamw_whole(vec(g_post), vec(norm_post_g), vec(m_norm_post_g), vec(v_norm_post_g),
                                            "adamw_post")
    d_conv, nm_conv, nv_conv = _adamw_whole(g_conv, conv_w, m_conv_w, v_conv_w, "adamw_conv")
    flat = lambda a: a.reshape(D_MODEL)
    return (loss, grad_x[None], g_pre, g_in, g_conv, g_out, g_post,
            flat(d_pre), d_in, d_conv, d_out, flat(d_post),
            flat(nm_pre), nm_in, nm_conv, nm_out, flat(nm_post),
            flat(nv_pre), nv_in, nv_conv, nv_out, flat(nv_post))
```

```python
import functools

import jax
import jax.numpy as jnp
import numpy as np
from jax import lax
from jax.experimental import pallas as pl
from jax.experimental.pallas import tpu as pltpu

F32 = jnp.float32
BF16 = jnp.bfloat16

D_MODEL = 1024
HEAD_DIM = 64
ATTN_W = 768
CONV_W = 256
IN_W = 4096
REST_W = IN_W - 3 * ATTN_W
BLK = 128
N_DEV = 8
SHARD_IN = IN_W // N_DEV
SHARD_OUT = D_MODEL // N_DEV
DILATIONS = (1, 4, 16)
ROPE_THETA = 10000.0
NORM_EPS = 1e-6
NEG = -1e30

ADAM_LR = 0.001
ADAM_B1 = 0.9
ADAM_B2 = 0.999
ADAM_EPS = 1e-08
ADAM_WD = 0.01
ADAM_STEP = 10

VMEM_LIMIT = 56 * 1024 * 1024
MESH = pl.DeviceIdType.MESH


def _params(n_grid):
    return pltpu.CompilerParams(dimension_semantics=("arbitrary",) * n_grid, vmem_limit_bytes=VMEM_LIMIT)


def _resident(shape):
    zeros = (0,) * len(shape)
    return pl.BlockSpec(shape, lambda *_: zeros, pipeline_mode=pl.Buffered(1))


def _sigmoid(a):
    return 1.0 / (1.0 + jnp.exp(-a))


def _swap_halves(t, first_half):
    return jnp.where(first_half, pltpu.roll(t, BLK - 32, 1), pltpu.roll(t, 32, 1))


def _rope_tables(seq):
    half = HEAD_DIM // 2
    inv_freq = ROPE_THETA ** (-jnp.arange(half, dtype=F32) * 2.0 / HEAD_DIM)
    ang = jnp.arange(seq).astype(F32)[:, None] * inv_freq[None, :]
    cos, sin = jnp.cos(ang), jnp.sin(ang)
    return jnp.concatenate([cos] * 4, axis=1), jnp.concatenate([-sin, sin, -sin, sin], axis=1)


N_CHUNK = ATTN_W // BLK


def _to_residue(src, chunk0, dst_ref, dil, rows, dtype):
    for r in range(dil):
        take = pl.ds(r, rows // dil, stride=dil) if dil > 1 else slice(None)
        for c in range(N_CHUNK):
            dst_ref[:, r * ATTN_W + c * BLK:r * ATTN_W + (c + 1) * BLK] = src[chunk0 + c, take, :].astype(dtype)


def _from_residue(src_ref, dst, dil, rows, accumulate):
    for r in range(dil):
        put = pl.ds(r, rows // dil, stride=dil) if dil > 1 else slice(None)
        for c in range(N_CHUNK):
            piece = src_ref[:, r * ATTN_W + c * BLK:r * ATTN_W + (c + 1) * BLK]
            if accumulate:
                dst[c, put, :] += piece
            else:
                dst[c, put, :] = piece


def _residue_spec(tm, dil):
    return pl.BlockSpec((tm // dil, dil * ATTN_W), lambda i: (i, 0))


def _residue_shape(seq, dil, dtype):
    return jax.ShapeDtypeStruct((seq // dil, dil * ATTN_W), dtype)


def _fwd_in(x, g_pre, w_in_g, cos_t, sin_t, tm=256):
    seq = x.shape[0]
    n_dil = len(DILATIONS)

    def body(x_ref, g_ref, w_ref, cos_ref, sin_ref, *rest):
        qkv_refs, (zr_ref, ht_ref, z_scr, qkv_scr) = rest[:3 * n_dil], rest[3 * n_dil:]
        xv = x_ref[...]
        r = lax.rsqrt(jnp.mean(xv * xv, axis=-1, keepdims=True) + NORM_EPS)
        hf = (xv * r) * g_ref[...]
        ht_ref[...] = hf.T.astype(BF16)
        h = hf.astype(BF16)
        for j in range(N_DEV):
            z_scr[:, j * SHARD_IN:(j + 1) * SHARD_IN] = jnp.dot(h, w_ref[j], preferred_element_type=F32)
        cos, sin = cos_ref[...], sin_ref[...]
        first_half = (lax.broadcasted_iota(jnp.int32, (tm, BLK), 1) & 32) == 0

        def rope(t):
            return t * cos + _swap_halves(t, first_half) * sin

        for c in range(N_CHUNK):
            qkv_scr[c] = rope(z_scr[:, c * BLK:(c + 1) * BLK]) * HEAD_DIM ** -0.5
            qkv_scr[N_CHUNK + c] = rope(z_scr[:, ATTN_W + c * BLK:ATTN_W + (c + 1) * BLK])
            qkv_scr[2 * N_CHUNK + c] = z_scr[:, 2 * ATTN_W + c * BLK:2 * ATTN_W + (c + 1) * BLK]
        for n, dil in enumerate(DILATIONS):
            for a in range(3):
                _to_residue(qkv_scr, a * N_CHUNK, qkv_refs[3 * n + a], dil, tm, BF16)
        zr_ref[...] = z_scr[:, 3 * ATTN_W:]

    row = lambda w: pl.BlockSpec((tm, w), lambda i: (i, 0))
    outs = pl.pallas_call(
        body, name="fwd_in", grid=(seq // tm,),
        in_specs=[row(D_MODEL), _resident((1, D_MODEL)), _resident((N_DEV, D_MODEL, SHARD_IN)), row(BLK), row(BLK)],
        out_specs=[_residue_spec(tm, dil) for dil in DILATIONS for _ in range(3)]
        + [row(REST_W), pl.BlockSpec((D_MODEL, tm), lambda i: (0, i))],
        out_shape=[_residue_shape(seq, dil, BF16) for dil in DILATIONS for _ in range(3)]
        + [jax.ShapeDtypeStruct((seq, REST_W), F32), jax.ShapeDtypeStruct((D_MODEL, seq), BF16)],
        scratch_shapes=[pltpu.VMEM((tm, IN_W), F32), pltpu.VMEM((3 * N_CHUNK, tm, BLK), F32)],
        compiler_params=_params(1),
    )(x, g_pre.reshape(1, D_MODEL), w_in_g, cos_t, sin_t)
    qkv = [tuple(outs[3 * n:3 * n + 3]) for n in range(n_dil)]
    return qkv, outs[3 * n_dil], outs[3 * n_dil + 1]


def _band_bias(first_block):
    kj = lax.broadcasted_iota(jnp.int32, (2 * BLK, BLK), 0)
    qi = lax.broadcasted_iota(jnp.int32, (2 * BLK, BLK), 1)
    valid = (kj >= qi) & (kj <= qi + BLK)
    bias = jnp.where(valid, 0.0, NEG).astype(BF16)
    bias_first = jnp.where(valid & (kj >= BLK), 0.0, NEG).astype(BF16)
    onehot = ((kj & (BLK - 1)) == qi).astype(F32).astype(BF16)
    return onehot, bias, jnp.where(first_block, bias_first, bias)


def _stack_heads(t, head0):
    keep0 = head0.astype(F32).astype(BF16)
    return jnp.concatenate([t * keep0, t * (1 - keep0)], axis=0)


def _unstack_heads(t2, head0):
    return jnp.where(head0, t2[:BLK], t2[BLK:])


def _rows_per_head(a, head0):
    b = pltpu.roll(a, HEAD_DIM, 1)
    rows = jnp.concatenate([jnp.where(head0, a, b), jnp.where(head0, b, a)], axis=0)
    return jnp.concatenate([rows, rows], axis=1)


BLOCKS_PER_STEP = 16


def _attn_specs(length, dil):
    n_blocks = length // BLK
    tb = min(BLOCKS_PER_STEP, n_blocks)
    nc = BLOCKS_PER_STEP // tb
    assert (dil * N_CHUNK) % nc == 0 and n_blocks % tb == 0
    tile = pl.BlockSpec((tb * BLK, nc * BLK), lambda c, t: (t, c))
    prev = pl.BlockSpec((BLK, nc * BLK), lambda c, t: (jnp.maximum(t * tb - 1, 0), c))
    grid = (dil * N_CHUNK // nc, n_blocks // tb)
    return tb, nc, tile, prev, grid


def _load_keys(cat, prev_ref, cur_ref):
    cat[0:BLK] = prev_ref[...]
    cat[BLK:] = cur_ref[...]


def _attn_fwd(q, k, v, dil):
    length = q.shape[0]
    tb, nc, tile, prev, grid = _attn_specs(length, dil)

    def body(q_ref, kc_ref, kp_ref, vc_ref, vp_ref, o_ref, lse_ref, kcat, vcat):
        _load_keys(kcat, kp_ref, kc_ref)
        _load_keys(vcat, vp_ref, vc_ref)
        head0 = lax.broadcasted_iota(jnp.int32, (BLK, BLK), 1) < HEAD_DIM
        onehot, bias, bias_start = _band_bias(pl.program_id(1) == 0)
        ones = jnp.ones((2 * BLK, BLK), BF16)
        for c in range(nc):
            cols = slice(c * BLK, (c + 1) * BLK)
            for j in range(tb):
                rows = slice(j * BLK, (j + 1) * BLK)
                q2 = jnp.concatenate([_stack_heads(q_ref[rows, cols], head0), onehot], axis=1)
                kk = jnp.concatenate([kcat[j * BLK:(j + 2) * BLK, cols], bias_start if j == 0 else bias], axis=1)
                s = lax.dot_general(q2, kk, (((1,), (1,)), ((), ())), preferred_element_type=F32)
                m = jnp.max(s, axis=1, keepdims=True)
                p = jnp.exp(s - m).astype(BF16)
                vv = jnp.concatenate([vcat[j * BLK:(j + 2) * BLK, cols], ones], axis=1)
                pv = jnp.dot(p, vv, preferred_element_type=F32)
                den = pv[:, BLK:]
                o_ref[rows, cols] = _unstack_heads(pv[:, :BLK] / den, head0)
                lse_ref[rows, cols] = _unstack_heads(m + jnp.log(den), head0)

    return pl.pallas_call(
        body, name=f"attn_fwd_d{dil}", grid=grid,
        in_specs=[tile, tile, prev, tile, prev], out_specs=[tile, tile],
        out_shape=[jax.ShapeDtypeStruct(q.shape, F32)] * 2,
        scratch_shapes=[pltpu.VMEM(((tb + 1) * BLK, nc * BLK), BF16)] * 2,
        compiler_params=_params(2),
    )(q, k, k, v, v)


def _attn_bwd(q, k, v, do, lse, delta, dil):
    length = q.shape[0]
    tb, nc, tile, prev, grid = _attn_specs(length, dil)
    whole = pl.BlockSpec((length, nc * BLK), lambda c, t: (0, c))

    def body(q_ref, do_ref, lse_ref, dl_ref, kc_ref, kp_ref, vc_ref, vp_ref, dq_ref, dk_ref, dv_ref, kcat, vcat):
        t = pl.program_id(1)
        _load_keys(kcat, kp_ref, kc_ref)
        _load_keys(vcat, vp_ref, vc_ref)
        head0 = lax.broadcasted_iota(jnp.int32, (BLK, BLK), 1) < HEAD_DIM
        onehot, bias, bias_start = _band_bias(t == 0)
        for c in range(nc):
            cols = slice(c * BLK, (c + 1) * BLK)
            for j in range(tb):
                rows = slice(j * BLK, (j + 1) * BLK)
                q2 = _stack_heads(q_ref[rows, cols], head0)
                do2 = _stack_heads(do_ref[rows, cols], head0)
                kk = kcat[j * BLK:(j + 2) * BLK, cols]
                vv = vcat[j * BLK:(j + 2) * BLK, cols]
                s = lax.dot_general(jnp.concatenate([q2, onehot], axis=1),
                                    jnp.concatenate([kk, bias_start if j == 0 else bias], axis=1),
                                    (((1,), (1,)), ((), ())), preferred_element_type=F32)
                p = jnp.exp(s - _rows_per_head(lse_ref[rows, cols], head0))
                dp = lax.dot_general(do2, vv, (((1,), (1,)), ((), ())), preferred_element_type=F32)
                ds = (p * (dp - _rows_per_head(dl_ref[rows, cols], head0))).astype(BF16)
                dq2 = jnp.dot(ds, kk, preferred_element_type=F32)
                dq_ref[rows, cols] = _unstack_heads(dq2, head0) * HEAD_DIM ** -0.5
                dk2 = lax.dot_general(ds, q2, (((0,), (0,)), ((), ())), preferred_element_type=F32)
                dv2 = lax.dot_general(p.astype(BF16), do2, (((0,), (0,)), ((), ())), preferred_element_type=F32)
                own = pl.ds(pl.multiple_of((t * tb + j) * BLK, BLK), BLK)
                dk_ref[own, cols] = dk2[BLK:]
                dv_ref[own, cols] = dv2[BLK:]

                def add_to_previous(j=j, cols=cols, dk2=dk2, dv2=dv2):
                    before = pl.ds(pl.multiple_of((t * tb + j - 1) * BLK, BLK), BLK)
                    dk_ref[before, cols] += dk2[:BLK]
                    dv_ref[before, cols] += dv2[:BLK]

                if j == 0:
                    pl.when(t > 0)(add_to_previous)
                else:
                    add_to_previous()

    return pl.pallas_call(
        body, name=f"attn_bwd_d{dil}", grid=grid,
        in_specs=[tile, tile, tile, tile, tile, prev, tile, prev], out_specs=[tile, whole, whole],
        out_shape=[jax.ShapeDtypeStruct(q.shape, F32)] * 3,
        scratch_shapes=[pltpu.VMEM(((tb + 1) * BLK, nc * BLK), BF16)] * 2,
        compiler_params=_params(2),
    )(q, do, lse, delta, k, k, v, v)


def _conv_taps(u, before8, tm):
    row = lax.broadcasted_iota(jnp.int32, u.shape, 0)
    last, last2 = before8[7:8, :], before8[6:7, :]
    u1 = jnp.where(row == 0, last, pltpu.roll(u, 1, 0))
    u2 = jnp.where(row == 0, last2, jnp.where(row == 1, last, pltpu.roll(u, 2, 0)))
    return u1, u2


def _attn_combine(o_parts, lse_parts, zr, conv_w, tm=256):
    seq = zr.shape[0]
    a0, h0, b0, c0, g0 = 0, ATTN_W, ATTN_W + CONV_W, ATTN_W + 2 * CONV_W, ATTN_W + 3 * CONV_W

    def body(o1, o2, o3, l1, l2, l3, zr_ref, zp_ref, w_ref, mixed_ref, o_ref, lse1, lse2, lse3, *scr):
        i = pl.program_id(0)
        for src, dst, dil in zip((o2, o3, l2, l3), scr[:4], DILATIONS[1:] * 2):
            _from_residue(src, dst, dil, tm, accumulate=False)
        for c in range(N_CHUNK):
            cols = slice(c * BLK, (c + 1) * BLK)
            la, lb, lc = l1[:, cols], scr[2][c], scr[3][c]
            top = jnp.maximum(jnp.maximum(la, lb), lc)
            ea, eb, ec = jnp.exp(la - top), jnp.exp(lb - top), jnp.exp(lc - top)
            den = ea + eb + ec
            o = (ea / den) * o1[:, cols] + (eb / den) * scr[0][c] + (ec / den) * scr[1][c]
            o_ref[:, cols] = o
            scr[4][c] = top + jnp.log(den)
            ga = zr_ref[:, cols]
            mixed_ref[:, cols] = (o * (ga * _sigmoid(ga))).astype(BF16)
        for dst, dil in zip((lse1, lse2, lse3), DILATIONS):
            _to_residue(scr[4], 0, dst, dil, tm, F32)
        u = zr_ref[:, c0:g0] * zr_ref[:, h0:b0]
        before = jnp.where(i > 0, zp_ref[:, c0:g0] * zp_ref[:, h0:b0], 0.0)
        u1, u2 = _conv_taps(u, before, tm)
        y = u2 * w_ref[0:1, :] + u1 * w_ref[1:2, :] + u * w_ref[2:3, :]
        gc = zr_ref[:, g0:]
        mixed_ref[:, ATTN_W:] = ((zr_ref[:, b0:c0] * y) * (gc * _sigmoid(gc))).astype(BF16)

    row = lambda w: pl.BlockSpec((tm, w), lambda i: (i, 0))
    before8 = pl.BlockSpec((8, REST_W), lambda i: (jnp.maximum(i * (tm // 8) - 1, 0), 0))
    views = [_residue_spec(tm, dil) for dil in DILATIONS]
    outs = pl.pallas_call(
        body, name="attn_combine", grid=(seq // tm,),
        in_specs=views * 2 + [row(REST_W), before8, _resident((3, CONV_W))],
        out_specs=[row(D_MODEL), row(ATTN_W)] + views,
        out_shape=[jax.ShapeDtypeStruct((seq, D_MODEL), BF16), jax.ShapeDtypeStruct((seq, ATTN_W), F32)]
        + [_residue_shape(seq, dil, F32) for dil in DILATIONS],
        scratch_shapes=[pltpu.VMEM((N_CHUNK, tm, BLK), F32)] * 5,
        compiler_params=_params(1),
    )(*o_parts, *lse_parts, zr, zr, conv_w)
    return outs[0], outs[1], outs[2:]


def _out_loss_bwd(mixed, w_out_g, x, target, g_post, tm=256):
    seq = x.shape[0]

    def body(mx_ref, w_ref, x_ref, t_ref, g_ref, dout_ref, dy_ref, dmx_ref, dw_ref, st_ref):
        i = pl.program_id(0)
        mx = mx_ref[...]
        y = jnp.dot(mx, w_ref[...], preferred_element_type=F32)
        r = lax.rsqrt(jnp.mean(y * y, axis=-1, keepdims=True) + NORM_EPS)
        yhat = y * r
        g = g_ref[...]
        err = (x_ref[...] + yhat * g) - t_ref[...]
        dn = err * (1.0 / D_MODEL)
        dout_ref[...] = dn
        tg = dn * g
        dy = (r * (tg - yhat * jnp.mean(tg * yhat, axis=-1, keepdims=True))).astype(BF16)
        dy_ref[...] = dy
        dmx_ref[...] = lax.dot_general(dy, w_ref[...], (((1,), (1,)), ((), ())), preferred_element_type=F32)
        dw = lax.dot_general(mx, dy, (((0,), (0,)), ((), ())), preferred_element_type=F32)
        gsum = jnp.sum(dn * yhat, axis=0, keepdims=True)
        lsum = jnp.broadcast_to(0.5 / D_MODEL * jnp.sum(err * err), (1, D_MODEL))

        @pl.when(i == 0)
        def _():
            dw_ref[...] = dw
            st_ref[...] = jnp.zeros_like(st_ref)
            st_ref[0:1, :] = gsum
            st_ref[1:2, :] = lsum

        @pl.when(i > 0)
        def _():
            dw_ref[...] += dw
            st_ref[0:1, :] += gsum
            st_ref[1:2, :] += lsum

    row = lambda w: pl.BlockSpec((tm, w), lambda i: (i, 0))
    return pl.pallas_call(
        body, name="out_loss_bwd", grid=(seq // tm,),
        in_specs=[row(D_MODEL), _resident((D_MODEL, D_MODEL)), row(D_MODEL), row(D_MODEL), _resident((1, D_MODEL))],
        out_specs=[row(D_MODEL), row(D_MODEL), row(D_MODEL), pl.BlockSpec((D_MODEL, D_MODEL), lambda i: (0, 0)),
                   pl.BlockSpec((8, D_MODEL), lambda i: (0, 0))],
        out_shape=[jax.ShapeDtypeStruct((seq, D_MODEL), F32), jax.ShapeDtypeStruct((seq, D_MODEL), BF16),
                   jax.ShapeDtypeStruct((seq, D_MODEL), F32), jax.ShapeDtypeStruct((D_MODEL, D_MODEL), F32),
                   jax.ShapeDtypeStruct((8, D_MODEL), F32)],
        compiler_params=_params(1),
    )(mixed, w_out_g, x, target, g_post.reshape(1, D_MODEL))


def _head_sum(prod, same_head):
    hi = prod.astype(BF16)
    lo = (prod - hi.astype(F32)).astype(BF16)
    return (jnp.dot(hi, same_head, preferred_element_type=F32) + jnp.dot(lo, same_head, preferred_element_type=F32))


def _gate_bwd(dmixed, zr, o, conv_w, tm=256):
    seq = zr.shape[0]
    n_tiles = seq // tm
    n_dil = len(DILATIONS)
    a0, h0, b0, c0, g0 = 0, ATTN_W, ATTN_W + CONV_W, ATTN_W + 2 * CONV_W, ATTN_W + 3 * CONV_W

    def body(dm_ref, dmn_ref, zr_ref, zp_ref, zn_ref, o_ref, w_ref, *rest):
        do_refs, dl_refs = rest[:n_dil], rest[n_dil:2 * n_dil]
        dz_ref, dw_ref, do_scr, dl_scr = rest[2 * n_dil:]
        i = pl.program_id(0)
        ga = zr_ref[:, a0:h0]
        sg = _sigmoid(ga)
        dattn = dm_ref[:, 0:ATTN_W]
        ov = o_ref[...]
        do = dattn * (ga * sg)
        dz_ref[:, a0:h0] = (dattn * ov * (sg * (1.0 + ga * (1.0 - sg)))).astype(BF16)
        li = lax.broadcasted_iota(jnp.int32, (BLK, BLK), 0) // HEAD_DIM
        lj = lax.broadcasted_iota(jnp.int32, (BLK, BLK), 1) // HEAD_DIM
        same_head = (li == lj).astype(BF16)
        prod = do * ov
        for c in range(N_CHUNK):
            cols = slice(c * BLK, (c + 1) * BLK)
            do_scr[c] = do[:, cols]
            dl_scr[c] = _head_sum(prod[:, cols], same_head)
        for n, dil in enumerate(DILATIONS):
            _to_residue(do_scr, 0, do_refs[n], dil, tm, BF16)
            _to_residue(dl_scr, 0, dl_refs[n], dil, tm, F32)

        ch, cb, cc, gc = zr_ref[:, h0:b0], zr_ref[:, b0:c0], zr_ref[:, c0:g0], zr_ref[:, g0:]
        u = cc * ch
        before = jnp.where(i > 0, zp_ref[:, c0:g0] * zp_ref[:, h0:b0], 0.0)
        u1, u2 = _conv_taps(u, before, tm)
        w0, w1, w2 = w_ref[0:1, :], w_ref[1:2, :], w_ref[2:3, :]
        y = u2 * w0 + u1 * w1 + u * w2
        sc = _sigmoid(gc)
        silu_c = gc * sc
        dconv = dm_ref[:, ATTN_W:]
        dz_ref[:, b0:c0] = (dconv * y * silu_c).astype(BF16)
        dz_ref[:, g0:] = (dconv * (cb * y) * (sc * (1.0 + gc * (1.0 - sc)))).astype(BF16)
        dy = dconv * cb * silu_c
        gn = zn_ref[:, g0:]
        after = jnp.where(i < n_tiles - 1, dmn_ref[:, ATTN_W:] * zn_ref[:, b0:c0] * (gn * _sigmoid(gn)), 0.0)
        row = lax.broadcasted_iota(jnp.int32, dy.shape, 0)
        nxt, nxt2 = after[0:1, :], after[1:2, :]
        dy1 = jnp.where(row == tm - 1, nxt, pltpu.roll(dy, tm - 1, 0))
        dy2 = jnp.where(row == tm - 1, nxt2, jnp.where(row == tm - 2, nxt, pltpu.roll(dy, tm - 2, 0)))
        du = dy * w2 + dy1 * w1 + dy2 * w0
        dz_ref[:, c0:g0] = (du * ch).astype(BF16)
        dz_ref[:, h0:b0] = (du * cc).astype(BF16)
        dws = [jnp.sum(dy * u2, axis=0, keepdims=True), jnp.sum(dy * u1, axis=0, keepdims=True),
               jnp.sum(dy * u, axis=0, keepdims=True)]

        @pl.when(i == 0)
        def _():
            dw_ref[...] = jnp.zeros_like(dw_ref)

        for n, part in enumerate(dws):
            dw_ref[n:n + 1, :] += part

    row_spec = lambda w: pl.BlockSpec((tm, w), lambda i: (i, 0))
    before8 = pl.BlockSpec((8, REST_W), lambda i: (jnp.maximum(i * (tm // 8) - 1, 0), 0))
    after8 = lambda w: pl.BlockSpec((8, w), lambda i: (jnp.minimum((i + 1) * (tm // 8), seq // 8 - 1), 0))
    views = [_residue_spec(tm, dil) for dil in DILATIONS]
    outs = pl.pallas_call(
        body, name="gate_bwd", grid=(n_tiles,),
        in_specs=[row_spec(D_MODEL), after8(D_MODEL), row_spec(REST_W), before8, after8(REST_W), row_spec(ATTN_W),
                  _resident((3, CONV_W))],
        out_specs=views * 2 + [row_spec(REST_W), pl.BlockSpec((8, CONV_W), lambda i: (0, 0))],
        out_shape=[_residue_shape(seq, dil, BF16) for dil in DILATIONS]
        + [_residue_shape(seq, dil, F32) for dil in DILATIONS]
        + [jax.ShapeDtypeStruct((seq, REST_W), BF16), jax.ShapeDtypeStruct((8, CONV_W), F32)],
        scratch_shapes=[pltpu.VMEM((N_CHUNK, tm, BLK), F32)] * 2,
        compiler_params=_params(1),
    )(dmixed, dmixed, zr, zr, zr, o, conv_w)
    return outs[:n_dil], outs[n_dil:2 * n_dil], outs[2 * n_dil], outs[2 * n_dil + 1]


def _in_bwd(dqs, dks, dvs, dzr, cos_t, sin_t, x, d_out, g_pre, w_in_g, tm=256):
    seq = x.shape[0]

    def body(q1, q2, q3, k1, k2, k3, v1, v2, v3, dzr_ref, cos_ref, sin_ref, x_ref, dout_ref, g_ref, w_ref,
             dz_ref, gx_ref, st_ref, dq_scr, dk_scr, dv_scr):
        i = pl.program_id(0)
        for parts, total in (((q1, q2, q3), dq_scr), ((k1, k2, k3), dk_scr), ((v1, v2, v3), dv_scr)):
            for n, dil in enumerate(DILATIONS):
                _from_residue(parts[n], total, dil, tm, accumulate=n > 0)
        cos, sin = cos_ref[...], sin_ref[...]
        first_half = (lax.broadcasted_iota(jnp.int32, (tm, BLK), 1) & 32) == 0

        def unrope(t):
            return t * cos - _swap_halves(t, first_half) * sin

        for c in range(N_CHUNK):
            dz_ref[:, c * BLK:(c + 1) * BLK] = unrope(dq_scr[c]).astype(BF16)
            dz_ref[:, ATTN_W + c * BLK:ATTN_W + (c + 1) * BLK] = unrope(dk_scr[c]).astype(BF16)
            dz_ref[:, 2 * ATTN_W + c * BLK:2 * ATTN_W + (c + 1) * BLK] = dv_scr[c].astype(BF16)
        dz_ref[:, 3 * ATTN_W:] = dzr_ref[...]
        dh = jnp.zeros((tm, D_MODEL), F32)
        for j in range(N_DEV):
            dh += lax.dot_general(dz_ref[:, j * SHARD_IN:(j + 1) * SHARD_IN], w_ref[j], (((1,), (1,)), ((), ())),
                                  preferred_element_type=F32)
        xv = x_ref[...]
        r = lax.rsqrt(jnp.mean(xv * xv, axis=-1, keepdims=True) + NORM_EPS)
        xhat = xv * r
        tg = dh * g_ref[...]
        gx_ref[...] = dout_ref[...] + r * (tg - xhat * jnp.mean(tg * xhat, axis=-1, keepdims=True))
        gsum = jnp.sum(dh * xhat, axis=0, keepdims=True)

        @pl.when(i == 0)
        def _():
            st_ref[...] = jnp.zeros_like(st_ref)

        st_ref[0:1, :] += gsum

    row = lambda w: pl.BlockSpec((tm, w), lambda i: (i, 0))
    return pl.pallas_call(
        body, name="in_bwd", grid=(seq // tm,),
        in_specs=[_residue_spec(tm, dil) for dil in DILATIONS] * 3
        + [row(REST_W), row(BLK), row(BLK), row(D_MODEL), row(D_MODEL), _resident((1, D_MODEL)),
           _resident((N_DEV, D_MODEL, SHARD_IN))],
        out_specs=[row(IN_W), row(D_MODEL), pl.BlockSpec((8, D_MODEL), lambda i: (0, 0))],
        out_shape=[jax.ShapeDtypeStruct((seq, IN_W), BF16), jax.ShapeDtypeStruct((seq, D_MODEL), F32),
                   jax.ShapeDtypeStruct((8, D_MODEL), F32)],
        scratch_shapes=[pltpu.VMEM((N_CHUNK, tm, BLK), F32)] * 3,
        compiler_params=_params(1),
    )(*dqs, *dks, *dvs, dzr, cos_t, sin_t, x, d_out, g_pre.reshape(1, D_MODEL), w_in_g)


def _local_step(x, target, g_pre, g_post, w_in_g, w_out_g, conv_w):
    seq = x.shape[0]
    cos_t, sin_t = _rope_tables(seq)
    qkv, zr, ht = _fwd_in(x, g_pre, w_in_g, cos_t, sin_t)
    parts = [_attn_fwd(*qkv[n], dil) for n, dil in enumerate(DILATIONS)]
    mixed, o, lse = _attn_combine([p[0] for p in parts], [p[1] for p in parts], zr, conv_w)
    d_out, dy, dmixed, dw_out, st_post = _out_loss_bwd(mixed, w_out_g, x, target, g_post)
    do, delta, dzr, dconv = _gate_bwd(dmixed, zr, o, conv_w)
    grads = [_attn_bwd(*qkv[n], do[n], lse[n], delta[n], dil) for n, dil in enumerate(DILATIONS)]
    dz, grad_x, st_pre = _in_bwd([g[0] for g in grads], [g[1] for g in grads], [g[2] for g in grads], dzr,
                                 cos_t, sin_t, x, d_out, g_pre, w_in_g)
    conv_rows = jnp.pad(dconv[0:3], ((0, 0), (0, D_MODEL - CONV_W)))
    small = jnp.concatenate([st_pre[0:1], st_post[0:2], conv_rows, jnp.zeros((2, D_MODEL), F32)], axis=0)
    return grad_x, ht, dz, dw_out, small


def _coords():
    return lax.axis_index("x"), lax.axis_index("y"), lax.axis_index("c")


def _peer(k):
    x, y, c = _coords()
    px = 1 - x if k & 4 else x
    py = 1 - y if k & 2 else y
    pc = 1 - c if k & 1 else c
    return (px, py, pc), 4 * px + 2 * py + pc


HBM_SPEC = pl.BlockSpec(memory_space=pltpu.HBM)
VMEM_SPEC = pl.BlockSpec(memory_space=pltpu.VMEM)


def _ag_weights(w_in, w_out, conv_w):
    def body(win_ref, wout_ref, cw_ref, gin_ref, gout_ref, gcw_ref, win_bf, wout_bf, cw_pad, send_sems, recv_sems,
             local_sems):
        x, y, c = _coords()
        me, sibling = (x, y, c), (x, y, 1 - c)
        chips = [(1 - x, y), (x, 1 - y), (1 - x, 1 - y)]
        slab = lambda px, py, pc: 4 * px + 2 * py + pc
        win_bf[...] = win_ref[...].astype(BF16)
        wout_bf[...] = wout_ref[...].astype(BF16)
        cw_pad[...] = jnp.zeros_like(cw_pad)
        cw_pad[0:3, 0:CONV_W // N_DEV] = cw_ref[...]
        mine = [win_bf, wout_bf, cw_pad]
        gathered = [gin_ref, gout_ref, gcw_ref]

        def copies(k, block, to, own=False):
            return [pltpu.make_async_remote_copy(src_ref=mine[a] if own else gathered[a].at[slab(*block)],
                                                 dst_ref=gathered[a].at[slab(*block)], send_sem=send_sems.at[k, a],
                                                 recv_sem=recv_sems.at[k, a], device_id=to, device_id_type=MESH)
                    for a in range(3)]

        local = [pltpu.make_async_copy(mine[a], gathered[a].at[slab(*me)], local_sems.at[a]) for a in range(3)]
        for cp in local:
            cp.start()
        first = copies(0, me, sibling, own=True)
        for j, chip in enumerate(chips):
            first += copies(1 + j, me, (*chip, c), own=True)
        for cp in first:
            cp.start()
        passed = []
        for j, chip in enumerate(chips):
            for cp in copies(1 + j, (*chip, c), me):
                cp.wait_recv()
            onward = copies(4 + j, (*chip, c), sibling)
            for cp in onward:
                cp.start()
            passed += onward
        for cp in copies(0, sibling, me):
            cp.wait_recv()
        for j, chip in enumerate(chips):
            for cp in copies(4 + j, (*chip, 1 - c), me):
                cp.wait_recv()
        for cp in first + passed:
            cp.wait_send()
        for cp in local:
            cp.wait()

    return pl.pallas_call(
        body, name="ag_weights",
        in_specs=[VMEM_SPEC, VMEM_SPEC, VMEM_SPEC], out_specs=[HBM_SPEC, HBM_SPEC, HBM_SPEC],
        out_shape=[jax.ShapeDtypeStruct((N_DEV, D_MODEL, SHARD_IN), BF16),
                   jax.ShapeDtypeStruct((N_DEV, SHARD_OUT, D_MODEL), BF16),
                   jax.ShapeDtypeStruct((N_DEV, 8, BLK), F32)],
        scratch_shapes=[pltpu.VMEM((D_MODEL, SHARD_IN), BF16), pltpu.VMEM((SHARD_OUT, D_MODEL), BF16),
                        pltpu.VMEM((8, BLK), F32), pltpu.SemaphoreType.DMA((N_DEV - 1, 3)),
                        pltpu.SemaphoreType.DMA((N_DEV - 1, 3)), pltpu.SemaphoreType.DMA((3,))],
        compiler_params=pltpu.CompilerParams(vmem_limit_bytes=VMEM_LIMIT),
    )(w_in, w_out, conv_w)


SEND_ORDER = (2, 4, 6, 3, 5, 7, 1)


def _dw_in_rs(ht, dz, dw_out, small):
    seq = dz.shape[0]

    def body(cols_ref, ht_ref, dz_ref, dout_ref, sm_ref, own_ref, rin_ref, rout_ref, rsm_ref, sendbuf, zero_buf,
             send_sems, recv_sems, local_sems):
        del cols_ref
        step = pl.program_id(0)
        x, y, c = _coords()
        me = 4 * x + 2 * y + c

        def copies_to(n, k, which=(0, 1, 2)):
            peer, peer_idx = _peer(k)
            pairs = [(sendbuf.at[n], rin_ref), (dout_ref.at[peer_idx], rout_ref), (sm_ref, rsm_ref)]
            return [pltpu.make_async_remote_copy(src_ref=pairs[a][0], dst_ref=pairs[a][1].at[me],
                                                 send_sem=send_sems.at[n, a], recv_sem=recv_sems.at[n, a],
                                                 device_id=peer, device_id_type=MESH) for a in which]

        def arrivals_from(n, k):
            peer, peer_idx = _peer(k)
            pairs = [(sendbuf.at[n], rin_ref), (dout_ref.at[me], rout_ref), (sm_ref, rsm_ref)]
            return [pltpu.make_async_remote_copy(src_ref=src, dst_ref=dst.at[peer_idx], send_sem=send_sems.at[n, a],
                                                 recv_sem=recv_sems.at[n, a], device_id=peer, device_id_type=MESH)
                    for a, (src, dst) in enumerate(pairs)]

        local = [pltpu.make_async_copy(zero_buf, rin_ref.at[me], local_sems.at[0]),
                 pltpu.make_async_copy(dout_ref.at[me], rout_ref.at[me], local_sems.at[1]),
                 pltpu.make_async_copy(sm_ref, rsm_ref.at[me], local_sems.at[2])]

        @pl.when(step == 0)
        def _():
            zero_buf[...] = jnp.zeros_like(zero_buf)
            for cp in local:
                cp.start()
            for n, k in enumerate(SEND_ORDER):
                for cp in copies_to(n, k, which=(1, 2)):
                    cp.start()

        dw = jnp.dot(ht_ref[...], dz_ref[...], preferred_element_type=F32)
        for n, k in enumerate(SEND_ORDER):
            @pl.when(step == n)
            def _(n=n, k=k):
                sendbuf[n] = dw.astype(BF16)
                copies_to(n, k, which=(0,))[0].start()

        @pl.when(step == N_DEV - 1)
        def _():
            own_ref[...] = dw
            for n, k in enumerate(SEND_ORDER):
                for cp in arrivals_from(n, k):
                    cp.wait_recv()
            for n, k in enumerate(SEND_ORDER):
                for cp in copies_to(n, k):
                    cp.wait_send()
            for cp in local:
                cp.wait()

    peers = [_peer(k)[1] for k in SEND_ORDER]
    me = 4 * lax.axis_index("x") + 2 * lax.axis_index("y") + lax.axis_index("c")
    cols = jnp.stack(peers + [me]).astype(jnp.int32)
    grid_spec = pltpu.PrefetchScalarGridSpec(
        num_scalar_prefetch=1, grid=(N_DEV,),
        in_specs=[pl.BlockSpec((D_MODEL, seq), lambda s, cols: (0, 0), pipeline_mode=pl.Buffered(1)),
                  pl.BlockSpec((seq, SHARD_IN), lambda s, cols: (0, cols[s])), HBM_SPEC, HBM_SPEC],
        out_specs=[pl.BlockSpec((D_MODEL, SHARD_IN), lambda s, cols: (0, 0)), HBM_SPEC, HBM_SPEC, HBM_SPEC],
        scratch_shapes=[pltpu.VMEM((N_DEV - 1, D_MODEL, SHARD_IN), BF16), pltpu.VMEM((D_MODEL, SHARD_IN), BF16),
                        pltpu.SemaphoreType.DMA((N_DEV - 1, 3)), pltpu.SemaphoreType.DMA((N_DEV - 1, 3)),
                        pltpu.SemaphoreType.DMA((3,))])
    return pl.pallas_call(
        body, name="dw_in_rs", grid_spec=grid_spec,
        out_shape=[jax.ShapeDtypeStruct((D_MODEL, SHARD_IN), F32),
                   jax.ShapeDtypeStruct((N_DEV, D_MODEL, SHARD_IN), BF16),
                   jax.ShapeDtypeStruct((N_DEV, SHARD_OUT, D_MODEL), F32),
                   jax.ShapeDtypeStruct((N_DEV, 8, D_MODEL), F32)],
        compiler_params=_params(1),
    )(cols, ht, dz, dw_out, small)


def _adamw_math(w, g, m, v):
    m = ADAM_B1 * m + (1.0 - ADAM_B1) * g
    v = ADAM_B2 * v + (1.0 - ADAM_B2) * (g * g)
    m_hat = m / (1.0 - ADAM_B1 ** ADAM_STEP)
    v_hat = v / (1.0 - ADAM_B2 ** ADAM_STEP)
    delta = -ADAM_LR * (m_hat / (jnp.sqrt(v_hat) + ADAM_EPS) + ADAM_WD * w)
    return delta, m, v


def _sum_slabs(ref, first=None):
    total = ref[0].astype(F32) if first is None else first + ref[0].astype(F32)
    for s in range(1, N_DEV):
        total = total + ref[s].astype(F32)
    return total


def _adamw_slabs(parts, own, w, m, v, name, tr):
    rows, cols = w.shape
    tile = pl.BlockSpec((tr, cols), lambda i: (i, 0))

    def body(p_ref, *refs):
        own_ref = refs[0] if own is not None else None
        w_ref, m_ref, v_ref, g_ref, d_ref, nm_ref, nv_ref = refs[-7:]
        g = _sum_slabs(p_ref, None if own_ref is None else own_ref[...])
        g_ref[...] = g
        d_ref[...], nm_ref[...], nv_ref[...] = _adamw_math(w_ref[...], g, m_ref[...], v_ref[...])

    extra = [] if own is None else [own]
    return pl.pallas_call(
        body, name=name, grid=(rows // tr,),
        in_specs=[pl.BlockSpec((N_DEV, tr, cols), lambda i: (0, i, 0))] + [tile] * (len(extra) + 3),
        out_specs=[tile] * 4,
        out_shape=[jax.ShapeDtypeStruct((rows, cols), F32)] * 4,
        compiler_params=_params(1),
    )(parts, *extra, w, m, v)


def _sum_small(parts):
    def body(p_ref, out_ref):
        out_ref[...] = _sum_slabs(p_ref)

    return pl.pallas_call(body, name="sum_small", out_shape=jax.ShapeDtypeStruct(parts.shape[1:], F32))(parts)


def _adamw_whole(g, w, m, v, name):
    def body(g_ref, w_ref, m_ref, v_ref, d_ref, nm_ref, nv_ref):
        d_ref[...], nm_ref[...], nv_ref[...] = _adamw_math(w_ref[...], g_ref[...], m_ref[...], v_ref[...])

    return pl.pallas_call(body, name=name, out_shape=[jax.ShapeDtypeStruct(w.shape, F32)] * 3)(g, w, m, v)


def kernel(x, norm_pre_g, w_in, conv_w, w_out, norm_post_g, loss_target, m_norm_pre_g, m_w_in, m_conv_w, m_w_out,
           m_norm_post_g, v_norm_pre_g, v_w_in, v_conv_w, v_w_out, v_norm_post_g):
    n_conv = CONV_W // N_DEV
    w_in_g, w_out_g, conv_g = _ag_weights(w_in, w_out, conv_w)
    conv_full = conv_g[:, 0:3, 0:n_conv].transpose(1, 0, 2).reshape(3, CONV_W)
    grad_x, ht, dz, dw_out, small = _local_step(x[0], loss_target[0], norm_pre_g, norm_post_g, w_in_g,
                                                w_out_g.reshape(D_MODEL, D_MODEL), conv_full)
    own_in, r_in, r_out, r_small = _dw_in_rs(ht, dz, dw_out.reshape(N_DEV, SHARD_OUT, D_MODEL), small)
    g_in, d_in, nm_in, nv_in = _adamw_slabs(r_in, own_in, w_in, m_w_in, v_w_in, "adamw_in", 256)
    g_out, d_out, nm_out, nv_out = _adamw_slabs(r_out, None, w_out, m_w_out, v_w_out, "adamw_out", SHARD_OUT)
    sums = _sum_small(r_small)
    g_pre, g_post, loss = sums[0], sums[1], sums[2, 0]
    me = 4 * lax.axis_index("x") + 2 * lax.axis_index("y") + lax.axis_index("c")
    g_conv = lax.dynamic_slice(sums[3:6, 0:CONV_W], (0, me * n_conv), (3, n_conv))
    vec = lambda a: a.reshape(1, D_MODEL)
    d_pre, nm_pre, nv_pre = _adamw_whole(vec(g_pre), vec(norm_pre_g), vec(m_norm_pre_g), vec(v_norm_pre_g), "adamw_pre")
    d_post, nm_post, nv_post = _adamw_whole(vec(g_post), vec(norm_post_g), vec(m_norm_post_g), vec(v_norm_post_g),
                                            "adamw_post")
    d_conv, nm_conv, nv_conv = _adamw_whole(g_conv, conv_w, m_conv_w, v_conv_w, "adamw_conv")
    flat = lambda a: a.reshape(D_MODEL)
    return (loss, grad_x[None], g_pre, g_in, g_conv, g_out, g_post,
            flat(d_pre), d_in, d_conv, d_out, flat(d_post),
            flat(nm_pre), nm_in, nm_conv, nm_out, flat(nm_post),
            flat(nv_pre), nv_in, nv_conv, nv_out, flat(nv_post))
```

```python
import functools

import jax
import jax.numpy as jnp
import numpy as np
from jax import lax
from jax.experimental import pallas as pl
from jax.experimental.pallas import tpu as pltpu

F32 = jnp.float32
BF16 = jnp.bfloat16

D_MODEL = 1024
HEAD_DIM = 64
ATTN_W = 768
CONV_W = 256
IN_W = 4096
REST_W = IN_W - 3 * ATTN_W
BLK = 128
N_DEV = 8
SHARD_IN = IN_W // N_DEV
SHARD_OUT = D_MODEL // N_DEV
DILATIONS = (1, 4, 16)
ROPE_THETA = 10000.0
NORM_EPS = 1e-6
NEG = -1e30

ADAM_LR = 0.001
ADAM_B1 = 0.9
ADAM_B2 = 0.999
ADAM_EPS = 1e-08
ADAM_WD = 0.01
ADAM_STEP = 10

VMEM_LIMIT = 56 * 1024 * 1024
MESH = pl.DeviceIdType.MESH


def _params(n_grid):
    return pltpu.CompilerParams(dimension_semantics=("arbitrary",) * n_grid, vmem_limit_bytes=VMEM_LIMIT)


def _resident(shape):
    zeros = (0,) * len(shape)
    return pl.BlockSpec(shape, lambda *_: zeros, pipeline_mode=pl.Buffered(1))


def _sigmoid(a):
    return 1.0 / (1.0 + jnp.exp(-a))


def _swap_halves(t, first_half):
    return jnp.where(first_half, pltpu.roll(t, BLK - 32, 1), pltpu.roll(t, 32, 1))


def _rope_tables(seq):
    half = HEAD_DIM // 2
    inv_freq = ROPE_THETA ** (-jnp.arange(half, dtype=F32) * 2.0 / HEAD_DIM)
    ang = jnp.arange(seq).astype(F32)[:, None] * inv_freq[None, :]
    cos, sin = jnp.cos(ang), jnp.sin(ang)
    return jnp.concatenate([cos] * 4, axis=1), jnp.concatenate([-sin, sin, -sin, sin], axis=1)


N_CHUNK = ATTN_W // BLK


def _lanes(r, c):
    return slice(r * ATTN_W + c * BLK, r * ATTN_W + (c + 1) * BLK)


def _to_residues(src, chunk0, dst_refs, tmp, rows, dtype):
    assert DILATIONS == (1, 4, 16)
    dst1, dst4, dst16 = dst_refs
    n4, n16 = rows // 4, rows // 16
    for c in range(N_CHUNK):
        dst1[:, _lanes(0, c)] = src[chunk0 + c].astype(dtype)
        for r1 in range(4):
            tmp[c, r1 * n4:(r1 + 1) * n4, :] = src[chunk0 + c, pl.ds(r1, n4, stride=4), :]
        for r1 in range(4):
            dst4[:, _lanes(r1, c)] = tmp[c, r1 * n4:(r1 + 1) * n4, :].astype(dtype)
            for r2 in range(4):
                dst16[:, _lanes(4 * r2 + r1, c)] = tmp[c, pl.ds(r1 * n4 + r2, n16, stride=4), :].astype(dtype)


def _from_residue(src_ref, dst, dil, rows, accumulate, tmp=None):
    n4, n16 = rows // 4, rows // 16

    def put(where, piece):
        if accumulate:
            dst[where] += piece
        else:
            dst[where] = piece

    for c in range(N_CHUNK):
        if dil == 1:
            put((c,), src_ref[:, _lanes(0, c)])
            continue
        for r1 in range(4):
            if dil == 4:
                piece = src_ref[:, _lanes(r1, c)]
            else:
                for r2 in range(4):
                    tmp[c, pl.ds(r1 * n4 + r2, n16, stride=4), :] = src_ref[:, _lanes(4 * r2 + r1, c)]
                piece = tmp[c, r1 * n4:(r1 + 1) * n4, :]
            put((c, pl.ds(r1, n4, stride=4), slice(None)), piece)


def _residue_spec(tm, dil):
    return pl.BlockSpec((tm // dil, dil * ATTN_W), lambda i: (i, 0))


def _residue_shape(seq, dil, dtype):
    return jax.ShapeDtypeStruct((seq // dil, dil * ATTN_W), dtype)


def _fwd_in(x, g_pre, w_in_g, cos_t, sin_t, tm=256):
    seq = x.shape[0]
    n_dil = len(DILATIONS)

    def body(x_ref, g_ref, w_ref, cos_ref, sin_ref, *rest):
        qkv_refs, (zr_ref, ht_ref, z_scr, qkv_scr, tmp) = rest[:3 * n_dil], rest[3 * n_dil:]
        xv = x_ref[...]
        r = lax.rsqrt(jnp.mean(xv * xv, axis=-1, keepdims=True) + NORM_EPS)
        hf = (xv * r) * g_ref[...]
        ht_ref[...] = hf.T.astype(BF16)
        h = hf.astype(BF16)
        for j in range(N_DEV):
            z_scr[:, j * SHARD_IN:(j + 1) * SHARD_IN] = jnp.dot(h, w_ref[j], preferred_element_type=F32)
        cos, sin = cos_ref[...], sin_ref[...]
        first_half = (lax.broadcasted_iota(jnp.int32, (tm, BLK), 1) & 32) == 0

        def rope(t):
            return t * cos + _swap_halves(t, first_half) * sin

        for c in range(N_CHUNK):
            qkv_scr[c] = rope(z_scr[:, c * BLK:(c + 1) * BLK]) * HEAD_DIM ** -0.5
            qkv_scr[N_CHUNK + c] = rope(z_scr[:, ATTN_W + c * BLK:ATTN_W + (c + 1) * BLK])
            qkv_scr[2 * N_CHUNK + c] = z_scr[:, 2 * ATTN_W + c * BLK:2 * ATTN_W + (c + 1) * BLK]
        for a in range(3):
            _to_residues(qkv_scr, a * N_CHUNK, [qkv_refs[3 * n + a] for n in range(n_dil)], tmp, tm, BF16)
        zr_ref[...] = z_scr[:, 3 * ATTN_W:]

    row = lambda w: pl.BlockSpec((tm, w), lambda i: (i, 0))
    outs = pl.pallas_call(
        body, name="fwd_in", grid=(seq // tm,),
        in_specs=[row(D_MODEL), _resident((1, D_MODEL)), _resident((N_DEV, D_MODEL, SHARD_IN)), row(BLK), row(BLK)],
        out_specs=[_residue_spec(tm, dil) for dil in DILATIONS for _ in range(3)]
        + [row(REST_W), pl.BlockSpec((D_MODEL, tm), lambda i: (0, i))],
        out_shape=[_residue_shape(seq, dil, BF16) for dil in DILATIONS for _ in range(3)]
        + [jax.ShapeDtypeStruct((seq, REST_W), F32), jax.ShapeDtypeStruct((D_MODEL, seq), BF16)],
        scratch_shapes=[pltpu.VMEM((tm, IN_W), F32), pltpu.VMEM((3 * N_CHUNK, tm, BLK), F32),
                        pltpu.VMEM((N_CHUNK, tm, BLK), F32)],
        compiler_params=_params(1),
    )(x, g_pre.reshape(1, D_MODEL), w_in_g, cos_t, sin_t)
    qkv = [tuple(outs[3 * n:3 * n + 3]) for n in range(n_dil)]
    return qkv, outs[3 * n_dil], outs[3 * n_dil + 1]


def _band_bias(first_block):
    kj = lax.broadcasted_iota(jnp.int32, (2 * BLK, BLK), 0)
    qi = lax.broadcasted_iota(jnp.int32, (2 * BLK, BLK), 1)
    valid = (kj >= qi) & (kj <= qi + BLK)
    bias = jnp.where(valid, 0.0, NEG).astype(BF16)
    bias_first = jnp.where(valid & (kj >= BLK), 0.0, NEG).astype(BF16)
    onehot = ((kj & (BLK - 1)) == qi).astype(F32).astype(BF16)
    return onehot, bias, jnp.where(first_block, bias_first, bias)


def _stack_heads(t, head0):
    keep0 = head0.astype(F32).astype(BF16)
    return jnp.concatenate([t * keep0, t * (1 - keep0)], axis=0)


def _unstack_heads(t2, head0):
    return jnp.where(head0, t2[:BLK], t2[BLK:])


def _rows_per_head(a, head0):
    b = pltpu.roll(a, HEAD_DIM, 1)
    rows = jnp.concatenate([jnp.where(head0, a, b), jnp.where(head0, b, a)], axis=0)
    return jnp.concatenate([rows, rows], axis=1)


BLOCKS_PER_STEP = 16


def _attn_specs(length, dil):
    n_blocks = length // BLK
    tb = min(BLOCKS_PER_STEP, n_blocks)
    nc = BLOCKS_PER_STEP // tb
    assert (dil * N_CHUNK) % nc == 0 and n_blocks % tb == 0
    tile = pl.BlockSpec((tb * BLK, nc * BLK), lambda c, t: (t, c))
    prev = pl.BlockSpec((BLK, nc * BLK), lambda c, t: (jnp.maximum(t * tb - 1, 0), c))
    grid = (dil * N_CHUNK // nc, n_blocks // tb)
    return tb, nc, tile, prev, grid


def _load_keys(cat, prev_ref, cur_ref):
    cat[0:BLK] = prev_ref[...]
    cat[BLK:] = cur_ref[...]


def _attn_fwd(q, k, v, dil):
    length = q.shape[0]
    tb, nc, tile, prev, grid = _attn_specs(length, dil)

    def body(q_ref, kc_ref, kp_ref, vc_ref, vp_ref, o_ref, lse_ref, kcat, vcat):
        _load_keys(kcat, kp_ref, kc_ref)
        _load_keys(vcat, vp_ref, vc_ref)
        head0 = lax.broadcasted_iota(jnp.int32, (BLK, BLK), 1) < HEAD_DIM
        onehot, bias, bias_start = _band_bias(pl.program_id(1) == 0)
        ones = jnp.ones((2 * BLK, BLK), BF16)
        for c in range(nc):
            cols = slice(c * BLK, (c + 1) * BLK)
            for j in range(tb):
                rows = slice(j * BLK, (j + 1) * BLK)
                q2 = jnp.concatenate([_stack_heads(q_ref[rows, cols], head0), onehot], axis=1)
                kk = jnp.concatenate([kcat[j * BLK:(j + 2) * BLK, cols], bias_start if j == 0 else bias], axis=1)
                s = lax.dot_general(q2, kk, (((1,), (1,)), ((), ())), preferred_element_type=F32)
                m = jnp.max(s, axis=1, keepdims=True)
                p = jnp.exp(s - m).astype(BF16)
                vv = jnp.concatenate([vcat[j * BLK:(j + 2) * BLK, cols], ones], axis=1)
                pv = jnp.dot(p, vv, preferred_element_type=F32)
                den = pv[:, BLK:]
                o_ref[rows, cols] = _unstack_heads(pv[:, :BLK] / den, head0)
                lse_ref[rows, cols] = _unstack_heads(m + jnp.log(den), head0)

    return pl.pallas_call(
        body, name=f"attn_fwd_d{dil}", grid=grid,
        in_specs=[tile, tile, prev, tile, prev], out_specs=[tile, tile],
        out_shape=[jax.ShapeDtypeStruct(q.shape, F32)] * 2,
        scratch_shapes=[pltpu.VMEM(((tb + 1) * BLK, nc * BLK), BF16)] * 2,
        compiler_params=_params(2),
    )(q, k, k, v, v)


def _attn_bwd(q, k, v, do, lse, delta, dil):
    length = q.shape[0]
    tb, nc, tile, prev, grid = _attn_specs(length, dil)
    whole = pl.BlockSpec((length, nc * BLK), lambda c, t: (0, c))

    def body(q_ref, do_ref, lse_ref, dl_ref, kc_ref, kp_ref, vc_ref, vp_ref, dq_ref, dk_ref, dv_ref, kcat, vcat):
        t = pl.program_id(1)
        _load_keys(kcat, kp_ref, kc_ref)
        _load_keys(vcat, vp_ref, vc_ref)
        head0 = lax.broadcasted_iota(jnp.int32, (BLK, BLK), 1) < HEAD_DIM
        onehot, bias, bias_start = _band_bias(t == 0)
        for c in range(nc):
            cols = slice(c * BLK, (c + 1) * BLK)
            for j in range(tb):
                rows = slice(j * BLK, (j + 1) * BLK)
                q2 = _stack_heads(q_ref[rows, cols], head0)
                do2 = _stack_heads(do_ref[rows, cols], head0)
                kk = kcat[j * BLK:(j + 2) * BLK, cols]
                vv = vcat[j * BLK:(j + 2) * BLK, cols]
                s = lax.dot_general(jnp.concatenate([q2, onehot], axis=1),
                                    jnp.concatenate([kk, bias_start if j == 0 else bias], axis=1),
                                    (((1,), (1,)), ((), ())), preferred_element_type=F32)
                p = jnp.exp(s - _rows_per_head(lse_ref[rows, cols], head0))
                dp = lax.dot_general(do2, vv, (((1,), (1,)), ((), ())), preferred_element_type=F32)
                ds = (p * (dp - _rows_per_head(dl_ref[rows, cols], head0))).astype(BF16)
                dq2 = jnp.dot(ds, kk, preferred_element_type=F32)
                dq_ref[rows, cols] = _unstack_heads(dq2, head0) * HEAD_DIM ** -0.5
                dk2 = lax.dot_general(ds, q2, (((0,), (0,)), ((), ())), preferred_element_type=F32)
                dv2 = lax.dot_general(p.astype(BF16), do2, (((0,), (0,)), ((), ())), preferred_element_type=F32)
                own = pl.ds(pl.multiple_of((t * tb + j) * BLK, BLK), BLK)
                dk_ref[own, cols] = dk2[BLK:]
                dv_ref[own, cols] = dv2[BLK:]

                def add_to_previous(j=j, cols=cols, dk2=dk2, dv2=dv2):
                    before = pl.ds(pl.multiple_of((t * tb + j - 1) * BLK, BLK), BLK)
                    dk_ref[before, cols] += dk2[:BLK]
                    dv_ref[before, cols] += dv2[:BLK]

                if j == 0:
                    pl.when(t > 0)(add_to_previous)
                else:
                    add_to_previous()

    return pl.pallas_call(
        body, name=f"attn_bwd_d{dil}", grid=grid,
        in_specs=[tile, tile, tile, tile, tile, prev, tile, prev], out_specs=[tile, whole, whole],
        out_shape=[jax.ShapeDtypeStruct(q.shape, F32)] * 3,
        scratch_shapes=[pltpu.VMEM(((tb + 1) * BLK, nc * BLK), BF16)] * 2,
        compiler_params=_params(2),
    )(q, do, lse, delta, k, k, v, v)


def _conv_taps(u, before8, tm):
    row = lax.broadcasted_iota(jnp.int32, u.shape, 0)
    last, last2 = before8[7:8, :], before8[6:7, :]
    u1 = jnp.where(row == 0, last, pltpu.roll(u, 1, 0))
    u2 = jnp.where(row == 0, last2, jnp.where(row == 1, last, pltpu.roll(u, 2, 0)))
    return u1, u2


def _attn_combine(o_parts, lse_parts, zr, conv_w, tm=256):
    seq = zr.shape[0]
    a0, h0, b0, c0, g0 = 0, ATTN_W, ATTN_W + CONV_W, ATTN_W + 2 * CONV_W, ATTN_W + 3 * CONV_W

    def body(o1, o2, o3, l1, l2, l3, zr_ref, zp_ref, w_ref, mixed_ref, o_ref, lse1, lse2, lse3, *scr):
        i = pl.program_id(0)
        for src, dst, dil in zip((o2, o3, l2, l3), scr[:4], DILATIONS[1:] * 2):
            _from_residue(src, dst, dil, tm, accumulate=False, tmp=scr[5])
        for c in range(N_CHUNK):
            cols = slice(c * BLK, (c + 1) * BLK)
            la, lb, lc = l1[:, cols], scr[2][c], scr[3][c]
            top = jnp.maximum(jnp.maximum(la, lb), lc)
            ea, eb, ec = jnp.exp(la - top), jnp.exp(lb - top), jnp.exp(lc - top)
            den = ea + eb + ec
            o = (ea / den) * o1[:, cols] + (eb / den) * scr[0][c] + (ec / den) * scr[1][c]
            o_ref[:, cols] = o
            scr[4][c] = top + jnp.log(den)
            ga = zr_ref[:, cols]
            mixed_ref[:, cols] = (o * (ga * _sigmoid(ga))).astype(BF16)
        _to_residues(scr[4], 0, (lse1, lse2, lse3), scr[5], tm, F32)
        u = zr_ref[:, c0:g0] * zr_ref[:, h0:b0]
        before = jnp.where(i > 0, zp_ref[:, c0:g0] * zp_ref[:, h0:b0], 0.0)
        u1, u2 = _conv_taps(u, before, tm)
        y = u2 * w_ref[0:1, :] + u1 * w_ref[1:2, :] + u * w_ref[2:3, :]
        gc = zr_ref[:, g0:]
        mixed_ref[:, ATTN_W:] = ((zr_ref[:, b0:c0] * y) * (gc * _sigmoid(gc))).astype(BF16)

    row = lambda w: pl.BlockSpec((tm, w), lambda i: (i, 0))
    before8 = pl.BlockSpec((8, REST_W), lambda i: (jnp.maximum(i * (tm // 8) - 1, 0), 0))
    views = [_residue_spec(tm, dil) for dil in DILATIONS]
    outs = pl.pallas_call(
        body, name="attn_combine", grid=(seq // tm,),
        in_specs=views * 2 + [row(REST_W), before8, _resident((3, CONV_W))],
        out_specs=[row(D_MODEL), row(ATTN_W)] + views,
        out_shape=[jax.ShapeDtypeStruct((seq, D_MODEL), BF16), jax.ShapeDtypeStruct((seq, ATTN_W), F32)]
        + [_residue_shape(seq, dil, F32) for dil in DILATIONS],
        scratch_shapes=[pltpu.VMEM((N_CHUNK, tm, BLK), F32)] * 6,
        compiler_params=_params(1),
    )(*o_parts, *lse_parts, zr, zr, conv_w)
    return outs[0], outs[1], outs[2:]


def _out_loss_bwd(mixed, w_out_g, x, target, g_post, tm=512):
    seq = x.shape[0]

    def body(mx_ref, w_ref, x_ref, t_ref, g_ref, dout_ref, dmx_ref, dw_ref, st_ref):
        i = pl.program_id(0)
        mx = mx_ref[...]
        y = jnp.dot(mx, w_ref[...], preferred_element_type=F32)
        r = lax.rsqrt(jnp.mean(y * y, axis=-1, keepdims=True) + NORM_EPS)
        yhat = y * r
        g = g_ref[...]
        err = (x_ref[...] + yhat * g) - t_ref[...]
        dn = err * (1.0 / D_MODEL)
        dout_ref[...] = dn
        tg = dn * g
        dy = (r * (tg - yhat * jnp.mean(tg * yhat, axis=-1, keepdims=True))).astype(BF16)
        dmx_ref[...] = lax.dot_general(dy, w_ref[...], (((1,), (1,)), ((), ())), preferred_element_type=F32)
        dw = lax.dot_general(mx, dy, (((0,), (0,)), ((), ())), preferred_element_type=F32)
        gsum = jnp.sum(dn * yhat, axis=0, keepdims=True)
        lsum = jnp.broadcast_to(0.5 / D_MODEL * jnp.sum(err * err), (1, D_MODEL))

        @pl.when(i == 0)
        def _():
            dw_ref[...] = dw
            st_ref[...] = jnp.zeros_like(st_ref)
            st_ref[0:1, :] = gsum
            st_ref[1:2, :] = lsum

        @pl.when(i > 0)
        def _():
            dw_ref[...] += dw
            st_ref[0:1, :] += gsum
            st_ref[1:2, :] += lsum

    row = lambda w: pl.BlockSpec((tm, w), lambda i: (i, 0))
    return pl.pallas_call(
        body, name="out_loss_bwd", grid=(seq // tm,),
        in_specs=[row(D_MODEL), _resident((D_MODEL, D_MODEL)), row(D_MODEL), row(D_MODEL), _resident((1, D_MODEL))],
        out_specs=[row(D_MODEL), row(D_MODEL), pl.BlockSpec((D_MODEL, D_MODEL), lambda i: (0, 0)),
                   pl.BlockSpec((8, D_MODEL), lambda i: (0, 0))],
        out_shape=[jax.ShapeDtypeStruct((seq, D_MODEL), F32), jax.ShapeDtypeStruct((seq, D_MODEL), F32),
                   jax.ShapeDtypeStruct((D_MODEL, D_MODEL), F32), jax.ShapeDtypeStruct((8, D_MODEL), F32)],
        compiler_params=_params(1),
    )(mixed, w_out_g, x, target, g_post.reshape(1, D_MODEL))


def _head_sum(prod, same_head):
    hi = prod.astype(BF16)
    lo = (prod - hi.astype(F32)).astype(BF16)
    return (jnp.dot(hi, same_head, preferred_element_type=F32) + jnp.dot(lo, same_head, preferred_element_type=F32))


def _gate_bwd(dmixed, zr, o, conv_w, tm=256):
    seq = zr.shape[0]
    n_tiles = seq // tm
    n_dil = len(DILATIONS)
    a0, h0, b0, c0, g0 = 0, ATTN_W, ATTN_W + CONV_W, ATTN_W + 2 * CONV_W, ATTN_W + 3 * CONV_W

    def body(dm_ref, dmn_ref, zr_ref, zp_ref, zn_ref, o_ref, w_ref, *rest):
        do_refs, dl_refs = rest[:n_dil], rest[n_dil:2 * n_dil]
        dz_ref, dw_ref, do_scr, dl_scr, tmp = rest[2 * n_dil:]
        i = pl.program_id(0)
        ga = zr_ref[:, a0:h0]
        sg = _sigmoid(ga)
        dattn = dm_ref[:, 0:ATTN_W]
        ov = o_ref[...]
        do = dattn * (ga * sg)
        dz_ref[:, a0:h0] = (dattn * ov * (sg * (1.0 + ga * (1.0 - sg)))).astype(BF16)
        li = lax.broadcasted_iota(jnp.int32, (BLK, BLK), 0) // HEAD_DIM
        lj = lax.broadcasted_iota(jnp.int32, (BLK, BLK), 1) // HEAD_DIM
        same_head = (li == lj).astype(BF16)
        prod = do * ov
        for c in range(N_CHUNK):
            cols = slice(c * BLK, (c + 1) * BLK)
            do_scr[c] = do[:, cols]
            dl_scr[c] = _head_sum(prod[:, cols], same_head)
        _to_residues(do_scr, 0, do_refs, tmp, tm, BF16)
        _to_residues(dl_scr, 0, dl_refs, tmp, tm, F32)

        ch, cb, cc, gc = zr_ref[:, h0:b0], zr_ref[:, b0:c0], zr_ref[:, c0:g0], zr_ref[:, g0:]
        u = cc * ch
        before = jnp.where(i > 0, zp_ref[:, c0:g0] * zp_ref[:, h0:b0], 0.0)
        u1, u2 = _conv_taps(u, before, tm)
        w0, w1, w2 = w_ref[0:1, :], w_ref[1:2, :], w_ref[2:3, :]
        y = u2 * w0 + u1 * w1 + u * w2
        sc = _sigmoid(gc)
        silu_c = gc * sc
        dconv = dm_ref[:, ATTN_W:]
        dz_ref[:, b0:c0] = (dconv * y * silu_c).astype(BF16)
        dz_ref[:, g0:] = (dconv * (cb * y) * (sc * (1.0 + gc * (1.0 - sc)))).astype(BF16)
        dy = dconv * cb * silu_c
        gn = zn_ref[:, g0:]
        after = jnp.where(i < n_tiles - 1, dmn_ref[:, ATTN_W:] * zn_ref[:, b0:c0] * (gn * _sigmoid(gn)), 0.0)
        row = lax.broadcasted_iota(jnp.int32, dy.shape, 0)
        nxt, nxt2 = after[0:1, :], after[1:2, :]
        dy1 = jnp.where(row == tm - 1, nxt, pltpu.roll(dy, tm - 1, 0))
        dy2 = jnp.where(row == tm - 1, nxt2, jnp.where(row == tm - 2, nxt, pltpu.roll(dy, tm - 2, 0)))
        du = dy * w2 + dy1 * w1 + dy2 * w0
        dz_ref[:, c0:g0] = (du * ch).astype(BF16)
        dz_ref[:, h0:b0] = (du * cc).astype(BF16)
        dws = [jnp.sum(dy * u2, axis=0, keepdims=True), jnp.sum(dy * u1, axis=0, keepdims=True),
               jnp.sum(dy * u, axis=0, keepdims=True)]

        @pl.when(i == 0)
        def _():
            dw_ref[...] = jnp.zeros_like(dw_ref)

        for n, part in enumerate(dws):
            dw_ref[n:n + 1, :] += part

    row_spec = lambda w: pl.BlockSpec((tm, w), lambda i: (i, 0))
    before8 = pl.BlockSpec((8, REST_W), lambda i: (jnp.maximum(i * (tm // 8) - 1, 0), 0))
    after8 = lambda w: pl.BlockSpec((8, w), lambda i: (jnp.minimum((i + 1) * (tm // 8), seq // 8 - 1), 0))
    views = [_residue_spec(tm, dil) for dil in DILATIONS]
    outs = pl.pallas_call(
        body, name="gate_bwd", grid=(n_tiles,),
        in_specs=[row_spec(D_MODEL), after8(D_MODEL), row_spec(REST_W), before8, after8(REST_W), row_spec(ATTN_W),
                  _resident((3, CONV_W))],
        out_specs=views * 2 + [row_spec(REST_W), pl.BlockSpec((8, CONV_W), lambda i: (0, 0))],
        out_shape=[_residue_shape(seq, dil, BF16) for dil in DILATIONS]
        + [_residue_shape(seq, dil, F32) for dil in DILATIONS]
        + [jax.ShapeDtypeStruct((seq, REST_W), BF16), jax.ShapeDtypeStruct((8, CONV_W), F32)],
        scratch_shapes=[pltpu.VMEM((N_CHUNK, tm, BLK), F32)] * 3,
        compiler_params=_params(1),
    )(dmixed, dmixed, zr, zr, zr, o, conv_w)
    return outs[:n_dil], outs[n_dil:2 * n_dil], outs[2 * n_dil], outs[2 * n_dil + 1]


def _in_bwd(dqs, dks, dvs, dzr, cos_t, sin_t, x, d_out, g_pre, w_in_g, tm=256):
    seq = x.shape[0]

    def body(q1, q2, q3, k1, k2, k3, v1, v2, v3, dzr_ref, cos_ref, sin_ref, x_ref, dout_ref, g_ref, w_ref,
             dz_ref, gx_ref, st_ref, dq_scr, dk_scr, dv_scr, tmp):
        i = pl.program_id(0)
        for parts, total in (((q1, q2, q3), dq_scr), ((k1, k2, k3), dk_scr), ((v1, v2, v3), dv_scr)):
            for n, dil in enumerate(DILATIONS):
                _from_residue(parts[n], total, dil, tm, accumulate=n > 0, tmp=tmp)
        cos, sin = cos_ref[...], sin_ref[...]
        first_half = (lax.broadcasted_iota(jnp.int32, (tm, BLK), 1) & 32) == 0

        def unrope(t):
            return t * cos - _swap_halves(t, first_half) * sin

        for c in range(N_CHUNK):
            dz_ref[:, c * BLK:(c + 1) * BLK] = unrope(dq_scr[c]).astype(BF16)
            dz_ref[:, ATTN_W + c * BLK:ATTN_W + (c + 1) * BLK] = unrope(dk_scr[c]).astype(BF16)
            dz_ref[:, 2 * ATTN_W + c * BLK:2 * ATTN_W + (c + 1) * BLK] = dv_scr[c].astype(BF16)
        dz_ref[:, 3 * ATTN_W:] = dzr_ref[...]
        dh = jnp.zeros((tm, D_MODEL), F32)
        for j in range(N_DEV):
            dh += lax.dot_general(dz_ref[:, j * SHARD_IN:(j + 1) * SHARD_IN], w_ref[j], (((1,), (1,)), ((), ())),
                                  preferred_element_type=F32)
        xv = x_ref[...]
        r = lax.rsqrt(jnp.mean(xv * xv, axis=-1, keepdims=True) + NORM_EPS)
        xhat = xv * r
        tg = dh * g_ref[...]
        gx_ref[...] = dout_ref[...] + r * (tg - xhat * jnp.mean(tg * xhat, axis=-1, keepdims=True))
        gsum = jnp.sum(dh * xhat, axis=0, keepdims=True)

        @pl.when(i == 0)
        def _():
            st_ref[...] = jnp.zeros_like(st_ref)

        st_ref[0:1, :] += gsum

    row = lambda w: pl.BlockSpec((tm, w), lambda i: (i, 0))
    return pl.pallas_call(
        body, name="in_bwd", grid=(seq // tm,),
        in_specs=[_residue_spec(tm, dil) for dil in DILATIONS] * 3
        + [row(REST_W), row(BLK), row(BLK), row(D_MODEL), row(D_MODEL), _resident((1, D_MODEL)),
           _resident((N_DEV, D_MODEL, SHARD_IN))],
        out_specs=[row(IN_W), row(D_MODEL), pl.BlockSpec((8, D_MODEL), lambda i: (0, 0))],
        out_shape=[jax.ShapeDtypeStruct((seq, IN_W), BF16), jax.ShapeDtypeStruct((seq, D_MODEL), F32),
                   jax.ShapeDtypeStruct((8, D_MODEL), F32)],
        scratch_shapes=[pltpu.VMEM((N_CHUNK, tm, BLK), F32)] * 4,
        compiler_params=_params(1),
    )(*dqs, *dks, *dvs, dzr, cos_t, sin_t, x, d_out, g_pre.reshape(1, D_MODEL), w_in_g)


def _local_step(x, target, g_pre, g_post, w_in_g, w_out_g, conv_w):
    seq = x.shape[0]
    cos_t, sin_t = _rope_tables(seq)
    qkv, zr, ht = _fwd_in(x, g_pre, w_in_g, cos_t, sin_t)
    parts = [_attn_fwd(*qkv[n], dil) for n, dil in enumerate(DILATIONS)]
    mixed, o, lse = _attn_combine([p[0] for p in parts], [p[1] for p in parts], zr, conv_w)
    d_out, dmixed, dw_out, st_post = _out_loss_bwd(mixed, w_out_g, x, target, g_post)
    do, delta, dzr, dconv = _gate_bwd(dmixed, zr, o, conv_w)
    grads = [_attn_bwd(*qkv[n], do[n], lse[n], delta[n], dil) for n, dil in enumerate(DILATIONS)]
    dz, grad_x, st_pre = _in_bwd([g[0] for g in grads], [g[1] for g in grads], [g[2] for g in grads], dzr,
                                 cos_t, sin_t, x, d_out, g_pre, w_in_g)
    conv_rows = jnp.pad(dconv[0:3], ((0, 0), (0, D_MODEL - CONV_W)))
    small = jnp.concatenate([st_pre[0:1], st_post[0:2], conv_rows, jnp.zeros((2, D_MODEL), F32)], axis=0)
    return grad_x, ht, dz, dw_out, small


def _coords():
    return lax.axis_index("x"), lax.axis_index("y"), lax.axis_index("c")


def _peer(k):
    x, y, c = _coords()
    px = 1 - x if k & 4 else x
    py = 1 - y if k & 2 else y
    pc = 1 - c if k & 1 else c
    return (px, py, pc), 4 * px + 2 * py + pc


HBM_SPEC = pl.BlockSpec(memory_space=pltpu.HBM)
VMEM_SPEC = pl.BlockSpec(memory_space=pltpu.VMEM)


def _ag_weights(w_in, w_out, conv_w):
    def body(win_ref, wout_ref, cw_ref, gin_ref, gout_ref, gcw_ref, win_bf, wout_bf, cw_pad, send_sems, recv_sems,
             local_sems):
        x, y, c = _coords()
        me, sibling = (x, y, c), (x, y, 1 - c)
        chips = [(1 - x, y), (x, 1 - y), (1 - x, 1 - y)]
        slab = lambda px, py, pc: 4 * px + 2 * py + pc
        win_bf[...] = win_ref[...].astype(BF16)
        wout_bf[...] = wout_ref[...].astype(BF16)
        cw_pad[...] = jnp.zeros_like(cw_pad)
        cw_pad[0:3, 0:CONV_W // N_DEV] = cw_ref[...]
        mine = [win_bf, wout_bf, cw_pad]
        gathered = [gin_ref, gout_ref, gcw_ref]

        def copies(k, block, to, own=False):
            return [pltpu.make_async_remote_copy(src_ref=mine[a] if own else gathered[a].at[slab(*block)],
                                                 dst_ref=gathered[a].at[slab(*block)], send_sem=send_sems.at[k, a],
                                                 recv_sem=recv_sems.at[k, a], device_id=to, device_id_type=MESH)
                    for a in range(3)]

        local = [pltpu.make_async_copy(mine[a], gathered[a].at[slab(*me)], local_sems.at[a]) for a in range(3)]
        for cp in local:
            cp.start()
        first = copies(0, me, sibling, own=True)
        for j, chip in enumerate(chips):
            first += copies(1 + j, me, (*chip, c), own=True)
        for cp in first:
            cp.start()
        passed = []
        for j, chip in enumerate(chips):
            for cp in copies(1 + j, (*chip, c), me):
                cp.wait_recv()
            onward = copies(4 + j, (*chip, c), sibling)
            for cp in onward:
                cp.start()
            passed += onward
        for cp in copies(0, sibling, me):
            cp.wait_recv()
        for j, chip in enumerate(chips):
            for cp in copies(4 + j, (*chip, 1 - c), me):
                cp.wait_recv()
        for cp in first + passed:
            cp.wait_send()
        for cp in local:
            cp.wait()

    return pl.pallas_call(
        body, name="ag_weights",
        in_specs=[VMEM_SPEC, VMEM_SPEC, VMEM_SPEC], out_specs=[HBM_SPEC, HBM_SPEC, HBM_SPEC],
        out_shape=[jax.ShapeDtypeStruct((N_DEV, D_MODEL, SHARD_IN), BF16),
                   jax.ShapeDtypeStruct((N_DEV, SHARD_OUT, D_MODEL), BF16),
                   jax.ShapeDtypeStruct((N_DEV, 8, BLK), F32)],
        scratch_shapes=[pltpu.VMEM((D_MODEL, SHARD_IN), BF16), pltpu.VMEM((SHARD_OUT, D_MODEL), BF16),
                        pltpu.VMEM((8, BLK), F32), pltpu.SemaphoreType.DMA((N_DEV - 1, 3)),
                        pltpu.SemaphoreType.DMA((N_DEV - 1, 3)), pltpu.SemaphoreType.DMA((3,))],
        compiler_params=pltpu.CompilerParams(vmem_limit_bytes=VMEM_LIMIT),
    )(w_in, w_out, conv_w)


SEND_ORDER = (2, 4, 6, 3, 5, 7, 1)


def _dw_in_rs(ht, dz, dw_out, small):
    seq = dz.shape[0]

    def body(cols_ref, ht_ref, dz_ref, dout_ref, sm_ref, own_ref, rin_ref, rout_ref, rsm_ref, sendbuf, zero_buf,
             send_sems, recv_sems, local_sems):
        del cols_ref
        step = pl.program_id(0)
        x, y, c = _coords()
        me = 4 * x + 2 * y + c

        def copies_to(n, k, which=(0, 1, 2)):
            peer, peer_idx = _peer(k)
            pairs = [(sendbuf.at[n], rin_ref), (dout_ref.at[peer_idx], rout_ref), (sm_ref, rsm_ref)]
            return [pltpu.make_async_remote_copy(src_ref=pairs[a][0], dst_ref=pairs[a][1].at[me],
                                                 send_sem=send_sems.at[n, a], recv_sem=recv_sems.at[n, a],
                                                 device_id=peer, device_id_type=MESH) for a in which]

        def arrivals_from(n, k):
            peer, peer_idx = _peer(k)
            pairs = [(sendbuf.at[n], rin_ref), (dout_ref.at[me], rout_ref), (sm_ref, rsm_ref)]
            return [pltpu.make_async_remote_copy(src_ref=src, dst_ref=dst.at[peer_idx], send_sem=send_sems.at[n, a],
                                                 recv_sem=recv_sems.at[n, a], device_id=peer, device_id_type=MESH)
                    for a, (src, dst) in enumerate(pairs)]

        local = [pltpu.make_async_copy(zero_buf, rin_ref.at[me], local_sems.at[0]),
                 pltpu.make_async_copy(dout_ref.at[me], rout_ref.at[me], local_sems.at[1]),
                 pltpu.make_async_copy(sm_ref, rsm_ref.at[me], local_sems.at[2])]

        @pl.when(step == 0)
        def _():
            zero_buf[...] = jnp.zeros_like(zero_buf)
            for cp in local:
                cp.start()
            for n, k in enumerate(SEND_ORDER):
                for cp in copies_to(n, k, which=(1, 2)):
                    cp.start()

        dw = jnp.dot(ht_ref[...], dz_ref[...], preferred_element_type=F32)
        for n, k in enumerate(SEND_ORDER):
            @pl.when(step == n)
            def _(n=n, k=k):
                sendbuf[n] = dw.astype(BF16)
                copies_to(n, k, which=(0,))[0].start()

        @pl.when(step == N_DEV - 1)
        def _():
            own_ref[...] = dw
            for n, k in enumerate(SEND_ORDER):
                for cp in arrivals_from(n, k):
                    cp.wait_recv()
            for n, k in enumerate(SEND_ORDER):
                for cp in copies_to(n, k):
                    cp.wait_send()
            for cp in local:
                cp.wait()

    peers = [_peer(k)[1] for k in SEND_ORDER]
    me = 4 * lax.axis_index("x") + 2 * lax.axis_index("y") + lax.axis_index("c")
    cols = jnp.stack(peers + [me]).astype(jnp.int32)
    grid_spec = pltpu.PrefetchScalarGridSpec(
        num_scalar_prefetch=1, grid=(N_DEV,),
        in_specs=[pl.BlockSpec((D_MODEL, seq), lambda s, cols: (0, 0), pipeline_mode=pl.Buffered(1)),
                  pl.BlockSpec((seq, SHARD_IN), lambda s, cols: (0, cols[s])), HBM_SPEC, HBM_SPEC],
        out_specs=[pl.BlockSpec((D_MODEL, SHARD_IN), lambda s, cols: (0, 0)), HBM_SPEC, HBM_SPEC, HBM_SPEC],
        scratch_shapes=[pltpu.VMEM((N_DEV - 1, D_MODEL, SHARD_IN), BF16), pltpu.VMEM((D_MODEL, SHARD_IN), BF16),
                        pltpu.SemaphoreType.DMA((N_DEV - 1, 3)), pltpu.SemaphoreType.DMA((N_DEV - 1, 3)),
                        pltpu.SemaphoreType.DMA((3,))])
    return pl.pallas_call(
        body, name="dw_in_rs", grid_spec=grid_spec,
        out_shape=[jax.ShapeDtypeStruct((D_MODEL, SHARD_IN), F32),
                   jax.ShapeDtypeStruct((N_DEV, D_MODEL, SHARD_IN), BF16),
                   jax.ShapeDtypeStruct((N_DEV, SHARD_OUT, D_MODEL), F32),
                   jax.ShapeDtypeStruct((N_DEV, 8, D_MODEL), F32)],
        compiler_params=_params(1),
    )(cols, ht, dz, dw_out, small)


def _adamw_math(w, g, m, v):
    m = ADAM_B1 * m + (1.0 - ADAM_B1) * g
    v = ADAM_B2 * v + (1.0 - ADAM_B2) * (g * g)
    m_hat = m / (1.0 - ADAM_B1 ** ADAM_STEP)
    v_hat = v / (1.0 - ADAM_B2 ** ADAM_STEP)
    delta = -ADAM_LR * (m_hat / (jnp.sqrt(v_hat) + ADAM_EPS) + ADAM_WD * w)
    return delta, m, v


def _sum_slabs(ref, first=None):
    total = ref[0].astype(F32) if first is None else first + ref[0].astype(F32)
    for s in range(1, N_DEV):
        total = total + ref[s].astype(F32)
    return total


def _adamw_slabs(parts, own, w, m, v, name, tr):
    rows, cols = w.shape
    tile = pl.BlockSpec((tr, cols), lambda i: (i, 0))

    def body(p_ref, *refs):
        own_ref = refs[0] if own is not None else None
        w_ref, m_ref, v_ref, g_ref, d_ref, nm_ref, nv_ref = refs[-7:]
        g = _sum_slabs(p_ref, None if own_ref is None else own_ref[...])
        g_ref[...] = g
        d_ref[...], nm_ref[...], nv_ref[...] = _adamw_math(w_ref[...], g, m_ref[...], v_ref[...])

    extra = [] if own is None else [own]
    return pl.pallas_call(
        body, name=name, grid=(rows // tr,),
        in_specs=[pl.BlockSpec((N_DEV, tr, cols), lambda i: (0, i, 0))] + [tile] * (len(extra) + 3),
        out_specs=[tile] * 4,
        out_shape=[jax.ShapeDtypeStruct((rows, cols), F32)] * 4,
        compiler_params=_params(1),
    )(parts, *extra, w, m, v)


def _sum_small(parts):
    def body(p_ref, out_ref):
        out_ref[...] = _sum_slabs(p_ref)

    return pl.pallas_call(body, name="sum_small", out_shape=jax.ShapeDtypeStruct(parts.shape[1:], F32))(parts)


def _adamw_whole(g, w, m, v, name):
    def body(g_ref, w_ref, m_ref, v_ref, d_ref, nm_ref, nv_ref):
        d_ref[...], nm_ref[...], nv_ref[...] = _adamw_math(w_ref[...], g_ref[...], m_ref[...], v_ref[...])

    return pl.pallas_call(body, name=name, out_shape=[jax.ShapeDtypeStruct(w.shape, F32)] * 3)(g, w, m, v)


def kernel(x, norm_pre_g, w_in, conv_w, w_out, norm_post_g, loss_target, m_norm_pre_g, m_w_in, m_conv_w, m_w_out,
           m_norm_post_g, v_norm_pre_g, v_w_in, v_conv_w, v_w_out, v_norm_post_g):
    n_conv = CONV_W // N_DEV
    w_in_g, w_out_g, conv_g = _ag_weights(w_in, w_out, conv_w)
    conv_full = conv_g[:, 0:3, 0:n_conv].transpose(1, 0, 2).reshape(3, CONV_W)
    grad_x, ht, dz, dw_out, small = _local_step(x[0], loss_target[0], norm_pre_g, norm_post_g, w_in_g,
                                                w_out_g.reshape(D_MODEL, D_MODEL), conv_full)
    own_in, r_in, r_out, r_small = _dw_in_rs(ht, dz, dw_out.reshape(N_DEV, SHARD_OUT, D_MODEL), small)
    g_in, d_in, nm_in, nv_in = _adamw_slabs(r_in, own_in, w_in, m_w_in, v_w_in, "adamw_in", 256)
    g_out, d_out, nm_out, nv_out = _adamw_slabs(r_out, None, w_out, m_w_out, v_w_out, "adamw_out", SHARD_OUT)
    sums = _sum_small(r_small)
    g_pre, g_post, loss = sums[0], sums[1], sums[2, 0]
    me = 4 * lax.axis_index("x") + 2 * lax.axis_index("y") + lax.axis_index("c")
    g_conv = lax.dynamic_slice(sums[3:6, 0:CONV_W], (0, me * n_conv), (3, n_conv))
    vec = lambda a: a.reshape(1, D_MODEL)
    d_pre, nm_pre, nv_pre = _adamw_whole(vec(g_pre), vec(norm_pre_g), vec(m_norm_pre_g), vec(v_norm_pre_g), "adamw_pre")
    d_post, nm_post, nv_post = _adamw_whole(vec(g_post), vec(norm_post_g), vec(m_norm_post_g), vec(v_norm_post_g),
                                            "adamw_post")
    d_conv, nm_conv, nv_conv = _adamw_whole(g_conv, conv_w, m_conv_w, v_conv_w, "adamw_conv")
    flat = lambda a: a.reshape(D_MODEL)
    return (loss, grad_x[None], g_pre, g_in, g_conv, g_out, g_post,
            flat(d_pre), d_in, d_conv, d_out, flat(d_post),
            flat(nm_pre), nm_in, nm_conv, nm_out, flat(nm_post),
            flat(nv_pre), nv_in, nv_conv, nv_out, flat(nv_post))
```

```python
import functools

import jax
import jax.numpy as jnp
import numpy as np
from jax import lax
from jax.experimental import pallas as pl
from jax.experimental.pallas import tpu as pltpu

F32 = jnp.float32
BF16 = jnp.bfloat16

D_MODEL = 1024
HEAD_DIM = 64
ATTN_W = 768
CONV_W = 256
IN_W = 4096
REST_W = IN_W - 3 * ATTN_W
BLK = 128
N_DEV = 8
SHARD_IN = IN_W // N_DEV
SHARD_OUT = D_MODEL // N_DEV
DILATIONS = (1, 4, 16)
ROPE_THETA = 10000.0
NORM_EPS = 1e-6
NEG = -1e30

ADAM_LR = 0.001
ADAM_B1 = 0.9
ADAM_B2 = 0.999
ADAM_EPS = 1e-08
ADAM_WD = 0.01
ADAM_STEP = 10

VMEM_LIMIT = 56 * 1024 * 1024
MESH = pl.DeviceIdType.MESH


def _params(n_grid):
    return pltpu.CompilerParams(dimension_semantics=("arbitrary",) * n_grid, vmem_limit_bytes=VMEM_LIMIT)


def _resident(shape):
    zeros = (0,) * len(shape)
    return pl.BlockSpec(shape, lambda *_: zeros, pipeline_mode=pl.Buffered(1))


def _sigmoid(a):
    return 1.0 / (1.0 + jnp.exp(-a))


def _swap_halves(t, first_half):
    return jnp.where(first_half, pltpu.roll(t, BLK - 32, 1), pltpu.roll(t, 32, 1))


def _rope_tables(seq):
    half = HEAD_DIM // 2
    inv_freq = ROPE_THETA ** (-jnp.arange(half, dtype=F32) * 2.0 / HEAD_DIM)
    ang = jnp.arange(seq).astype(F32)[:, None] * inv_freq[None, :]
    cos, sin = jnp.cos(ang), jnp.sin(ang)
    return jnp.concatenate([cos] * 4, axis=1), jnp.concatenate([-sin, sin, -sin, sin], axis=1)


N_CHUNK = ATTN_W // BLK


def _lanes(r, c):
    return slice(r * ATTN_W + c * BLK, r * ATTN_W + (c + 1) * BLK)


def _to_residues(src, chunk0, dst_refs, tmp, rows, dtype):
    assert DILATIONS == (1, 4, 16)
    dst1, dst4, dst16 = dst_refs
    n4, n16 = rows // 4, rows // 16
    for c in range(N_CHUNK):
        dst1[:, _lanes(0, c)] = src[chunk0 + c].astype(dtype)
        for r1 in range(4):
            tmp[c, r1 * n4:(r1 + 1) * n4, :] = src[chunk0 + c, pl.ds(r1, n4, stride=4), :]
        for r1 in range(4):
            dst4[:, _lanes(r1, c)] = tmp[c, r1 * n4:(r1 + 1) * n4, :].astype(dtype)
            for r2 in range(4):
                dst16[:, _lanes(4 * r2 + r1, c)] = tmp[c, pl.ds(r1 * n4 + r2, n16, stride=4), :].astype(dtype)


def _from_residue(src_ref, dst, dil, rows, accumulate, tmp=None):
    n4, n16 = rows // 4, rows // 16

    def put(where, piece):
        if accumulate:
            dst[where] += piece
        else:
            dst[where] = piece

    for c in range(N_CHUNK):
        if dil == 1:
            put((c,), src_ref[:, _lanes(0, c)].astype(F32))
            continue
        for r1 in range(4):
            if dil == 4:
                piece = src_ref[:, _lanes(r1, c)].astype(F32)
            else:
                for r2 in range(4):
                    tmp[c, pl.ds(r1 * n4 + r2, n16, stride=4), :] = src_ref[:, _lanes(4 * r2 + r1, c)].astype(F32)
                piece = tmp[c, r1 * n4:(r1 + 1) * n4, :]
            put((c, pl.ds(r1, n4, stride=4), slice(None)), piece)


def _residue_spec(tm, dil):
    return pl.BlockSpec((tm // dil, dil * ATTN_W), lambda i: (i, 0))


def _residue_shape(seq, dil, dtype):
    return jax.ShapeDtypeStruct((seq // dil, dil * ATTN_W), dtype)


def _fwd_in(x, g_pre, w_in_g, cos_t, sin_t, tm=256):
    seq = x.shape[0]
    n_dil = len(DILATIONS)

    def body(x_ref, g_ref, w_ref, cos_ref, sin_ref, *rest):
        qkv_refs, (zr_ref, ht_ref, z_scr, qkv_scr, tmp) = rest[:3 * n_dil], rest[3 * n_dil:]
        xv = x_ref[...]
        r = lax.rsqrt(jnp.mean(xv * xv, axis=-1, keepdims=True) + NORM_EPS)
        hf = (xv * r) * g_ref[...]
        ht_ref[...] = hf.T.astype(BF16)
        h = hf.astype(BF16)
        for j in range(N_DEV):
            z_scr[:, j * SHARD_IN:(j + 1) * SHARD_IN] = jnp.dot(h, w_ref[j], preferred_element_type=F32)
        cos, sin = cos_ref[...], sin_ref[...]
        first_half = (lax.broadcasted_iota(jnp.int32, (tm, BLK), 1) & 32) == 0

        def rope(t):
            return t * cos + _swap_halves(t, first_half) * sin

        for c in range(N_CHUNK):
            qkv_scr[c] = rope(z_scr[:, c * BLK:(c + 1) * BLK]) * HEAD_DIM ** -0.5
            qkv_scr[N_CHUNK + c] = rope(z_scr[:, ATTN_W + c * BLK:ATTN_W + (c + 1) * BLK])
            qkv_scr[2 * N_CHUNK + c] = z_scr[:, 2 * ATTN_W + c * BLK:2 * ATTN_W + (c + 1) * BLK]
        for a in range(3):
            _to_residues(qkv_scr, a * N_CHUNK, [qkv_refs[3 * n + a] for n in range(n_dil)], tmp, tm, BF16)
        zr_ref[...] = z_scr[:, 3 * ATTN_W:]

    row = lambda w: pl.BlockSpec((tm, w), lambda i: (i, 0))
    outs = pl.pallas_call(
        body, name="fwd_in", grid=(seq // tm,),
        in_specs=[row(D_MODEL), _resident((1, D_MODEL)), _resident((N_DEV, D_MODEL, SHARD_IN)), row(BLK), row(BLK)],
        out_specs=[_residue_spec(tm, dil) for dil in DILATIONS for _ in range(3)]
        + [row(REST_W), pl.BlockSpec((D_MODEL, tm), lambda i: (0, i))],
        out_shape=[_residue_shape(seq, dil, BF16) for dil in DILATIONS for _ in range(3)]
        + [jax.ShapeDtypeStruct((seq, REST_W), F32), jax.ShapeDtypeStruct((D_MODEL, seq), BF16)],
        scratch_shapes=[pltpu.VMEM((tm, IN_W), F32), pltpu.VMEM((3 * N_CHUNK, tm, BLK), F32),
                        pltpu.VMEM((N_CHUNK, tm, BLK), F32)],
        compiler_params=_params(1),
    )(x, g_pre.reshape(1, D_MODEL), w_in_g, cos_t, sin_t)
    qkv = [tuple(outs[3 * n:3 * n + 3]) for n in range(n_dil)]
    return qkv, outs[3 * n_dil], outs[3 * n_dil + 1]


def _band_bias(first_block):
    kj = lax.broadcasted_iota(jnp.int32, (2 * BLK, BLK), 0)
    qi = lax.broadcasted_iota(jnp.int32, (2 * BLK, BLK), 1)
    valid = (kj >= qi) & (kj <= qi + BLK)
    bias = jnp.where(valid, 0.0, NEG).astype(BF16)
    bias_first = jnp.where(valid & (kj >= BLK), 0.0, NEG).astype(BF16)
    onehot = ((kj & (BLK - 1)) == qi).astype(F32).astype(BF16)
    return onehot, bias, jnp.where(first_block, bias_first, bias)


def _stack_heads(t, head0):
    keep0 = head0.astype(F32).astype(BF16)
    return jnp.concatenate([t * keep0, t * (1 - keep0)], axis=0)


def _unstack_heads(t2, head0):
    return jnp.where(head0, t2[:BLK], t2[BLK:])


def _rows_per_head(a, head0):
    b = pltpu.roll(a, HEAD_DIM, 1)
    rows = jnp.concatenate([jnp.where(head0, a, b), jnp.where(head0, b, a)], axis=0)
    return jnp.concatenate([rows, rows], axis=1)


BLOCKS_PER_STEP = 16


def _attn_specs(length, dil):
    n_blocks = length // BLK
    tb = min(BLOCKS_PER_STEP, n_blocks)
    nc = BLOCKS_PER_STEP // tb
    assert (dil * N_CHUNK) % nc == 0 and n_blocks % tb == 0
    tile = pl.BlockSpec((tb * BLK, nc * BLK), lambda c, t: (t, c))
    prev = pl.BlockSpec((BLK, nc * BLK), lambda c, t: (jnp.maximum(t * tb - 1, 0), c))
    grid = (dil * N_CHUNK // nc, n_blocks // tb)
    return tb, nc, tile, prev, grid


def _load_keys(cat, prev_ref, cur_ref):
    cat[0:BLK] = prev_ref[...]
    cat[BLK:] = cur_ref[...]


def _attn_fwd(q, k, v, dil):
    length = q.shape[0]
    tb, nc, tile, prev, grid = _attn_specs(length, dil)

    def body(q_ref, kc_ref, kp_ref, vc_ref, vp_ref, o_ref, lse_ref, kcat, vcat):
        _load_keys(kcat, kp_ref, kc_ref)
        _load_keys(vcat, vp_ref, vc_ref)
        head0 = lax.broadcasted_iota(jnp.int32, (BLK, BLK), 1) < HEAD_DIM
        onehot, bias, bias_start = _band_bias(pl.program_id(1) == 0)
        ones = jnp.ones((2 * BLK, BLK), BF16)
        for c in range(nc):
            cols = slice(c * BLK, (c + 1) * BLK)
            for j in range(tb):
                rows = slice(j * BLK, (j + 1) * BLK)
                q2 = jnp.concatenate([_stack_heads(q_ref[rows, cols], head0), onehot], axis=1)
                kk = jnp.concatenate([kcat[j * BLK:(j + 2) * BLK, cols], bias_start if j == 0 else bias], axis=1)
                s = lax.dot_general(q2, kk, (((1,), (1,)), ((), ())), preferred_element_type=F32)
                m = jnp.max(s, axis=1, keepdims=True)
                p = jnp.exp(s - m).astype(BF16)
                vv = jnp.concatenate([vcat[j * BLK:(j + 2) * BLK, cols], ones], axis=1)
                pv = jnp.dot(p, vv, preferred_element_type=F32)
                den = pv[:, BLK:]
                o_ref[rows, cols] = _unstack_heads(pv[:, :BLK] / den, head0).astype(BF16)
                lse_ref[rows, cols] = _unstack_heads(m + jnp.log(den), head0)

    return pl.pallas_call(
        body, name=f"attn_fwd_d{dil}", grid=grid,
        in_specs=[tile, tile, prev, tile, prev], out_specs=[tile, tile],
        out_shape=[jax.ShapeDtypeStruct(q.shape, BF16), jax.ShapeDtypeStruct(q.shape, F32)],
        scratch_shapes=[pltpu.VMEM(((tb + 1) * BLK, nc * BLK), BF16)] * 2,
        compiler_params=_params(2),
    )(q, k, k, v, v)


def _attn_bwd(q, k, v, do, lse, delta, dil):
    length = q.shape[0]
    tb, nc, tile, prev, grid = _attn_specs(length, dil)
    whole = pl.BlockSpec((length, nc * BLK), lambda c, t: (0, c))

    def body(q_ref, do_ref, lse_ref, dl_ref, kc_ref, kp_ref, vc_ref, vp_ref, dq_ref, dk_ref, dv_ref, kcat, vcat):
        t = pl.program_id(1)
        _load_keys(kcat, kp_ref, kc_ref)
        _load_keys(vcat, vp_ref, vc_ref)
        head0 = lax.broadcasted_iota(jnp.int32, (BLK, BLK), 1) < HEAD_DIM
        onehot, bias, bias_start = _band_bias(t == 0)
        for c in range(nc):
            cols = slice(c * BLK, (c + 1) * BLK)
            for j in range(tb):
                rows = slice(j * BLK, (j + 1) * BLK)
                q2 = _stack_heads(q_ref[rows, cols], head0)
                do2 = _stack_heads(do_ref[rows, cols], head0)
                kk = kcat[j * BLK:(j + 2) * BLK, cols]
                vv = vcat[j * BLK:(j + 2) * BLK, cols]
                s = lax.dot_general(jnp.concatenate([q2, onehot], axis=1),
                                    jnp.concatenate([kk, bias_start if j == 0 else bias], axis=1),
                                    (((1,), (1,)), ((), ())), preferred_element_type=F32)
                p = jnp.exp(s - _rows_per_head(lse_ref[rows, cols], head0))
                dp = lax.dot_general(do2, vv, (((1,), (1,)), ((), ())), preferred_element_type=F32)
                ds = (p * (dp - _rows_per_head(dl_ref[rows, cols].astype(F32), head0))).astype(BF16)
                dq2 = jnp.dot(ds, kk, preferred_element_type=F32)
                dq_ref[rows, cols] = (_unstack_heads(dq2, head0) * HEAD_DIM ** -0.5).astype(BF16)
                dk2 = lax.dot_general(ds, q2, (((0,), (0,)), ((), ())), preferred_element_type=F32)
                dv2 = lax.dot_general(p.astype(BF16), do2, (((0,), (0,)), ((), ())), preferred_element_type=F32)
                own = pl.ds(pl.multiple_of((t * tb + j) * BLK, BLK), BLK)
                dk_ref[own, cols] = dk2[BLK:].astype(BF16)
                dv_ref[own, cols] = dv2[BLK:].astype(BF16)

                def add_to_previous(j=j, cols=cols, dk2=dk2, dv2=dv2):
                    before = pl.ds(pl.multiple_of((t * tb + j - 1) * BLK, BLK), BLK)
                    dk_ref[before, cols] = (dk_ref[before, cols].astype(F32) + dk2[:BLK]).astype(BF16)
                    dv_ref[before, cols] = (dv_ref[before, cols].astype(F32) + dv2[:BLK]).astype(BF16)

                if j == 0:
                    pl.when(t > 0)(add_to_previous)
                else:
                    add_to_previous()

    return pl.pallas_call(
        body, name=f"attn_bwd_d{dil}", grid=grid,
        in_specs=[tile, tile, tile, tile, tile, prev, tile, prev], out_specs=[tile, whole, whole],
        out_shape=[jax.ShapeDtypeStruct(q.shape, BF16)] * 3,
        scratch_shapes=[pltpu.VMEM(((tb + 1) * BLK, nc * BLK), BF16)] * 2,
        compiler_params=_params(2),
    )(q, do, lse, delta, k, k, v, v)


def _conv_taps(u, before8, tm):
    row = lax.broadcasted_iota(jnp.int32, u.shape, 0)
    last, last2 = before8[7:8, :], before8[6:7, :]
    u1 = jnp.where(row == 0, last, pltpu.roll(u, 1, 0))
    u2 = jnp.where(row == 0, last2, jnp.where(row == 1, last, pltpu.roll(u, 2, 0)))
    return u1, u2


def _attn_combine(o_parts, lse_parts, zr, conv_w, tm=256):
    seq = zr.shape[0]
    a0, h0, b0, c0, g0 = 0, ATTN_W, ATTN_W + CONV_W, ATTN_W + 2 * CONV_W, ATTN_W + 3 * CONV_W

    def body(o1, o2, o3, l1, l2, l3, zr_ref, zp_ref, w_ref, mixed_ref, o_ref, lse1, lse2, lse3, *scr):
        i = pl.program_id(0)
        for src, dst, dil in zip((o2, o3, l2, l3), scr[:4], DILATIONS[1:] * 2):
            _from_residue(src, dst, dil, tm, accumulate=False, tmp=scr[5])
        for c in range(N_CHUNK):
            cols = slice(c * BLK, (c + 1) * BLK)
            la, lb, lc = l1[:, cols], scr[2][c], scr[3][c]
            top = jnp.maximum(jnp.maximum(la, lb), lc)
            ea, eb, ec = jnp.exp(la - top), jnp.exp(lb - top), jnp.exp(lc - top)
            den = ea + eb + ec
            o = (ea / den) * o1[:, cols].astype(F32) + (eb / den) * scr[0][c] + (ec / den) * scr[1][c]
            o_ref[:, cols] = o
            scr[4][c] = top + jnp.log(den)
            ga = zr_ref[:, cols]
            mixed_ref[:, cols] = (o * (ga * _sigmoid(ga))).astype(BF16)
        _to_residues(scr[4], 0, (lse1, lse2, lse3), scr[5], tm, F32)
        u = zr_ref[:, c0:g0] * zr_ref[:, h0:b0]
        before = jnp.where(i > 0, zp_ref[:, c0:g0] * zp_ref[:, h0:b0], 0.0)
        u1, u2 = _conv_taps(u, before, tm)
        y = u2 * w_ref[0:1, :] + u1 * w_ref[1:2, :] + u * w_ref[2:3, :]
        gc = zr_ref[:, g0:]
        mixed_ref[:, ATTN_W:] = ((zr_ref[:, b0:c0] * y) * (gc * _sigmoid(gc))).astype(BF16)

    row = lambda w: pl.BlockSpec((tm, w), lambda i: (i, 0))
    before8 = pl.BlockSpec((8, REST_W), lambda i: (jnp.maximum(i * (tm // 8) - 1, 0), 0))
    views = [_residue_spec(tm, dil) for dil in DILATIONS]
    outs = pl.pallas_call(
        body, name="attn_combine", grid=(seq // tm,),
        in_specs=views * 2 + [row(REST_W), before8, _resident((3, CONV_W))],
        out_specs=[row(D_MODEL), row(ATTN_W)] + views,
        out_shape=[jax.ShapeDtypeStruct((seq, D_MODEL), BF16), jax.ShapeDtypeStruct((seq, ATTN_W), F32)]
        + [_residue_shape(seq, dil, F32) for dil in DILATIONS],
        scratch_shapes=[pltpu.VMEM((N_CHUNK, tm, BLK), F32)] * 6,
        compiler_params=_params(1),
    )(*o_parts, *lse_parts, zr, zr, conv_w)
    return outs[0], outs[1], outs[2:]


def _out_loss_bwd(mixed, w_out_g, x, target, g_post, tm=512):
    seq = x.shape[0]

    def body(mx_ref, w_ref, x_ref, t_ref, g_ref, dout_ref, dmx_ref, dw_ref, st_ref):
        i = pl.program_id(0)
        mx = mx_ref[...]
        y = jnp.dot(mx, w_ref[...], preferred_element_type=F32)
        r = lax.rsqrt(jnp.mean(y * y, axis=-1, keepdims=True) + NORM_EPS)
        yhat = y * r
        g = g_ref[...]
        err = (x_ref[...] + yhat * g) - t_ref[...]
        dn = err * (1.0 / D_MODEL)
        dout_ref[...] = dn
        tg = dn * g
        dy = (r * (tg - yhat * jnp.mean(tg * yhat, axis=-1, keepdims=True))).astype(BF16)
        dmx_ref[...] = lax.dot_general(dy, w_ref[...], (((1,), (1,)), ((), ())), preferred_element_type=F32)
        dw = lax.dot_general(mx, dy, (((0,), (0,)), ((), ())), preferred_element_type=F32)
        gsum = jnp.sum(dn * yhat, axis=0, keepdims=True)
        lsum = jnp.broadcast_to(0.5 / D_MODEL * jnp.sum(err * err), (1, D_MODEL))

        @pl.when(i == 0)
        def _():
            dw_ref[...] = dw
            st_ref[...] = jnp.zeros_like(st_ref)
            st_ref[0:1, :] = gsum
            st_ref[1:2, :] = lsum

        @pl.when(i > 0)
        def _():
            dw_ref[...] += dw
            st_ref[0:1, :] += gsum
            st_ref[1:2, :] += lsum

    row = lambda w: pl.BlockSpec((tm, w), lambda i: (i, 0))
    return pl.pallas_call(
        body, name="out_loss_bwd", grid=(seq // tm,),
        in_specs=[row(D_MODEL), _resident((D_MODEL, D_MODEL)), row(D_MODEL), row(D_MODEL), _resident((1, D_MODEL))],
        out_specs=[row(D_MODEL), row(D_MODEL), pl.BlockSpec((D_MODEL, D_MODEL), lambda i: (0, 0)),
                   pl.BlockSpec((8, D_MODEL), lambda i: (0, 0))],
        out_shape=[jax.ShapeDtypeStruct((seq, D_MODEL), F32), jax.ShapeDtypeStruct((seq, D_MODEL), F32),
                   jax.ShapeDtypeStruct((D_MODEL, D_MODEL), F32), jax.ShapeDtypeStruct((8, D_MODEL), F32)],
        compiler_params=_params(1),
    )(mixed, w_out_g, x, target, g_post.reshape(1, D_MODEL))


def _head_sum(prod, same_head):
    hi = prod.astype(BF16)
    lo = (prod - hi.astype(F32)).astype(BF16)
    return (jnp.dot(hi, same_head, preferred_element_type=F32) + jnp.dot(lo, same_head, preferred_element_type=F32))


def _gate_bwd(dmixed, zr, o, conv_w, tm=256):
    seq = zr.shape[0]
    n_tiles = seq // tm
    n_dil = len(DILATIONS)
    a0, h0, b0, c0, g0 = 0, ATTN_W, ATTN_W + CONV_W, ATTN_W + 2 * CONV_W, ATTN_W + 3 * CONV_W

    def body(dm_ref, dmn_ref, zr_ref, zp_ref, zn_ref, o_ref, w_ref, *rest):
        do_refs, dl_refs = rest[:n_dil], rest[n_dil:2 * n_dil]
        dz_ref, dw_ref, do_scr, dl_scr, tmp = rest[2 * n_dil:]
        i = pl.program_id(0)
        ga = zr_ref[:, a0:h0]
        sg = _sigmoid(ga)
        dattn = dm_ref[:, 0:ATTN_W]
        ov = o_ref[...]
        do = dattn * (ga * sg)
        dz_ref[:, a0:h0] = (dattn * ov * (sg * (1.0 + ga * (1.0 - sg)))).astype(BF16)
        li = lax.broadcasted_iota(jnp.int32, (BLK, BLK), 0) // HEAD_DIM
        lj = lax.broadcasted_iota(jnp.int32, (BLK, BLK), 1) // HEAD_DIM
        same_head = (li == lj).astype(BF16)
        prod = do * ov
        for c in range(N_CHUNK):
            cols = slice(c * BLK, (c + 1) * BLK)
            do_scr[c] = do[:, cols]
            dl_scr[c] = _head_sum(prod[:, cols], same_head)
        _to_residues(do_scr, 0, do_refs, tmp, tm, BF16)
        _to_residues(dl_scr, 0, dl_refs, tmp, tm, BF16)

        ch, cb, cc, gc = zr_ref[:, h0:b0], zr_ref[:, b0:c0], zr_ref[:, c0:g0], zr_ref[:, g0:]
        u = cc * ch
        before = jnp.where(i > 0, zp_ref[:, c0:g0] * zp_ref[:, h0:b0], 0.0)
        u1, u2 = _conv_taps(u, before, tm)
        w0, w1, w2 = w_ref[0:1, :], w_ref[1:2, :], w_ref[2:3, :]
        y = u2 * w0 + u1 * w1 + u * w2
        sc = _sigmoid(gc)
        silu_c = gc * sc
        dconv = dm_ref[:, ATTN_W:]
        dz_ref[:, b0:c0] = (dconv * y * silu_c).astype(BF16)
        dz_ref[:, g0:] = (dconv * (cb * y) * (sc * (1.0 + gc * (1.0 - sc)))).astype(BF16)
        dy = dconv * cb * silu_c
        gn = zn_ref[:, g0:]
        after = jnp.where(i < n_tiles - 1, dmn_ref[:, ATTN_W:] * zn_ref[:, b0:c0] * (gn * _sigmoid(gn)), 0.0)
        row = lax.broadcasted_iota(jnp.int32, dy.shape, 0)
        nxt, nxt2 = after[0:1, :], after[1:2, :]
        dy1 = jnp.where(row == tm - 1, nxt, pltpu.roll(dy, tm - 1, 0))
        dy2 = jnp.where(row == tm - 1, nxt2, jnp.where(row == tm - 2, nxt, pltpu.roll(dy, tm - 2, 0)))
        du = dy * w2 + dy1 * w1 + dy2 * w0
        dz_ref[:, c0:g0] = (du * ch).astype(BF16)
        dz_ref[:, h0:b0] = (du * cc).astype(BF16)
        dws = [jnp.sum(dy * u2, axis=0, keepdims=True), jnp.sum(dy * u1, axis=0, keepdims=True),
               jnp.sum(dy * u, axis=0, keepdims=True)]

        @pl.when(i == 0)
        def _():
            dw_ref[...] = jnp.zeros_like(dw_ref)

        for n, part in enumerate(dws):
            dw_ref[n:n + 1, :] += part

    row_spec = lambda w: pl.BlockSpec((tm, w), lambda i: (i, 0))
    before8 = pl.BlockSpec((8, REST_W), lambda i: (jnp.maximum(i * (tm // 8) - 1, 0), 0))
    after8 = lambda w: pl.BlockSpec((8, w), lambda i: (jnp.minimum((i + 1) * (tm // 8), seq // 8 - 1), 0))
    views = [_residue_spec(tm, dil) for dil in DILATIONS]
    outs = pl.pallas_call(
        body, name="gate_bwd", grid=(n_tiles,),
        in_specs=[row_spec(D_MODEL), after8(D_MODEL), row_spec(REST_W), before8, after8(REST_W), row_spec(ATTN_W),
                  _resident((3, CONV_W))],
        out_specs=views * 2 + [row_spec(REST_W), pl.BlockSpec((8, CONV_W), lambda i: (0, 0))],
        out_shape=[_residue_shape(seq, dil, BF16) for dil in DILATIONS] * 2
        + [jax.ShapeDtypeStruct((seq, REST_W), BF16), jax.ShapeDtypeStruct((8, CONV_W), F32)],
        scratch_shapes=[pltpu.VMEM((N_CHUNK, tm, BLK), F32)] * 3,
        compiler_params=_params(1),
    )(dmixed, dmixed, zr, zr, zr, o, conv_w)
    return outs[:n_dil], outs[n_dil:2 * n_dil], outs[2 * n_dil], outs[2 * n_dil + 1]


def _in_bwd(dqs, dks, dvs, dzr, cos_t, sin_t, x, d_out, g_pre, w_in_g, tm=256):
    seq = x.shape[0]

    def body(q1, q2, q3, k1, k2, k3, v1, v2, v3, dzr_ref, cos_ref, sin_ref, x_ref, dout_ref, g_ref, w_ref,
             dz_ref, gx_ref, st_ref, dq_scr, dk_scr, dv_scr, tmp):
        i = pl.program_id(0)
        for parts, total in (((q1, q2, q3), dq_scr), ((k1, k2, k3), dk_scr), ((v1, v2, v3), dv_scr)):
            for n, dil in enumerate(DILATIONS):
                _from_residue(parts[n], total, dil, tm, accumulate=n > 0, tmp=tmp)
        cos, sin = cos_ref[...], sin_ref[...]
        first_half = (lax.broadcasted_iota(jnp.int32, (tm, BLK), 1) & 32) == 0

        def unrope(t):
            return t * cos - _swap_halves(t, first_half) * sin

        for c in range(N_CHUNK):
            dz_ref[:, c * BLK:(c + 1) * BLK] = unrope(dq_scr[c]).astype(BF16)
            dz_ref[:, ATTN_W + c * BLK:ATTN_W + (c + 1) * BLK] = unrope(dk_scr[c]).astype(BF16)
            dz_ref[:, 2 * ATTN_W + c * BLK:2 * ATTN_W + (c + 1) * BLK] = dv_scr[c].astype(BF16)
        dz_ref[:, 3 * ATTN_W:] = dzr_ref[...]
        dh = jnp.zeros((tm, D_MODEL), F32)
        for j in range(N_DEV):
            dh += lax.dot_general(dz_ref[:, j * SHARD_IN:(j + 1) * SHARD_IN], w_ref[j], (((1,), (1,)), ((), ())),
                                  preferred_element_type=F32)
        xv = x_ref[...]
        r = lax.rsqrt(jnp.mean(xv * xv, axis=-1, keepdims=True) + NORM_EPS)
        xhat = xv * r
        tg = dh * g_ref[...]
        gx_ref[...] = dout_ref[...] + r * (tg - xhat * jnp.mean(tg * xhat, axis=-1, keepdims=True))
        gsum = jnp.sum(dh * xhat, axis=0, keepdims=True)

        @pl.when(i == 0)
        def _():
            st_ref[...] = jnp.zeros_like(st_ref)

        st_ref[0:1, :] += gsum

    row = lambda w: pl.BlockSpec((tm, w), lambda i: (i, 0))
    return pl.pallas_call(
        body, name="in_bwd", grid=(seq // tm,),
        in_specs=[_residue_spec(tm, dil) for dil in DILATIONS] * 3
        + [row(REST_W), row(BLK), row(BLK), row(D_MODEL), row(D_MODEL), _resident((1, D_MODEL)),
           _resident((N_DEV, D_MODEL, SHARD_IN))],
        out_specs=[row(IN_W), row(D_MODEL), pl.BlockSpec((8, D_MODEL), lambda i: (0, 0))],
        out_shape=[jax.ShapeDtypeStruct((seq, IN_W), BF16), jax.ShapeDtypeStruct((seq, D_MODEL), F32),
                   jax.ShapeDtypeStruct((8, D_MODEL), F32)],
        scratch_shapes=[pltpu.VMEM((N_CHUNK, tm, BLK), F32)] * 4,
        compiler_params=_params(1),
    )(*dqs, *dks, *dvs, dzr, cos_t, sin_t, x, d_out, g_pre.reshape(1, D_MODEL), w_in_g)


def _local_step(x, target, g_pre, g_post, w_in_g, w_out_g, conv_w):
    seq = x.shape[0]
    cos_t, sin_t = _rope_tables(seq)
    qkv, zr, ht = _fwd_in(x, g_pre, w_in_g, cos_t, sin_t)
    parts = [_attn_fwd(*qkv[n], dil) for n, dil in enumerate(DILATIONS)]
    mixed, o, lse = _attn_combine([p[0] for p in parts], [p[1] for p in parts], zr, conv_w)
    d_out, dmixed, dw_out, st_post = _out_loss_bwd(mixed, w_out_g, x, target, g_post)
    do, delta, dzr, dconv = _gate_bwd(dmixed, zr, o, conv_w)
    grads = [_attn_bwd(*qkv[n], do[n], lse[n], delta[n], dil) for n, dil in enumerate(DILATIONS)]
    dz, grad_x, st_pre = _in_bwd([g[0] for g in grads], [g[1] for g in grads], [g[2] for g in grads], dzr,
                                 cos_t, sin_t, x, d_out, g_pre, w_in_g)
    conv_rows = jnp.pad(dconv[0:3], ((0, 0), (0, D_MODEL - CONV_W)))
    small = jnp.concatenate([st_pre[0:1], st_post[0:2], conv_rows, jnp.zeros((2, D_MODEL), F32)], axis=0)
    return grad_x, ht, dz, dw_out, small


def _coords():
    return lax.axis_index("x"), lax.axis_index("y"), lax.axis_index("c")


def _peer(k):
    x, y, c = _coords()
    px = 1 - x if k & 4 else x
    py = 1 - y if k & 2 else y
    pc = 1 - c if k & 1 else c
    return (px, py, pc), 4 * px + 2 * py + pc


HBM_SPEC = pl.BlockSpec(memory_space=pltpu.HBM)
VMEM_SPEC = pl.BlockSpec(memory_space=pltpu.VMEM)


def _ag_weights(w_in, w_out, conv_w):
    def body(win_ref, wout_ref, cw_ref, gin_ref, gout_ref, gcw_ref, win_bf, wout_bf, cw_pad, send_sems, recv_sems,
             local_sems):
        x, y, c = _coords()
        me, sibling = (x, y, c), (x, y, 1 - c)
        chips = [(1 - x, y), (x, 1 - y), (1 - x, 1 - y)]
        slab = lambda px, py, pc: 4 * px + 2 * py + pc
        win_bf[...] = win_ref[...].astype(BF16)
        wout_bf[...] = wout_ref[...].astype(BF16)
        cw_pad[...] = jnp.zeros_like(cw_pad)
        cw_pad[0:3, 0:CONV_W // N_DEV] = cw_ref[...]
        mine = [win_bf, wout_bf, cw_pad]
        gathered = [gin_ref, gout_ref, gcw_ref]

        def copies(k, block, to, own=False):
            return [pltpu.make_async_remote_copy(src_ref=mine[a] if own else gathered[a].at[slab(*block)],
                                                 dst_ref=gathered[a].at[slab(*block)], send_sem=send_sems.at[k, a],
                                                 recv_sem=recv_sems.at[k, a], device_id=to, device_id_type=MESH)
                    for a in range(3)]

        local = [pltpu.make_async_copy(mine[a], gathered[a].at[slab(*me)], local_sems.at[a]) for a in range(3)]
        for cp in local:
            cp.start()
        first = copies(0, me, sibling, own=True)
        for j, chip in enumerate(chips):
            first += copies(1 + j, me, (*chip, c), own=True)
        for cp in first:
            cp.start()
        passed = []
        for j, chip in enumerate(chips):
            for cp in copies(1 + j, (*chip, c), me):
                cp.wait_recv()
            onward = copies(4 + j, (*chip, c), sibling)
            for cp in onward:
                cp.start()
            passed += onward
        for cp in copies(0, sibling, me):
            cp.wait_recv()
        for j, chip in enumerate(chips):
            for cp in copies(4 + j, (*chip, 1 - c), me):
                cp.wait_recv()
        for cp in first + passed:
            cp.wait_send()
        for cp in local:
            cp.wait()

    return pl.pallas_call(
        body, name="ag_weights",
        in_specs=[VMEM_SPEC, VMEM_SPEC, VMEM_SPEC], out_specs=[HBM_SPEC, HBM_SPEC, HBM_SPEC],
        out_shape=[jax.ShapeDtypeStruct((N_DEV, D_MODEL, SHARD_IN), BF16),
                   jax.ShapeDtypeStruct((N_DEV, SHARD_OUT, D_MODEL), BF16),
                   jax.ShapeDtypeStruct((N_DEV, 8, BLK), F32)],
        scratch_shapes=[pltpu.VMEM((D_MODEL, SHARD_IN), BF16), pltpu.VMEM((SHARD_OUT, D_MODEL), BF16),
                        pltpu.VMEM((8, BLK), F32), pltpu.SemaphoreType.DMA((N_DEV - 1, 3)),
                        pltpu.SemaphoreType.DMA((N_DEV - 1, 3)), pltpu.SemaphoreType.DMA((3,))],
        compiler_params=pltpu.CompilerParams(vmem_limit_bytes=VMEM_LIMIT),
    )(w_in, w_out, conv_w)


SEND_ORDER = (2, 4, 6, 3, 5, 7, 1)


def _dw_in_rs(ht, dz, dw_out, small):
    seq = dz.shape[0]

    def body(cols_ref, ht_ref, dz_ref, dout_ref, sm_ref, own_ref, rin_ref, rout_ref, rsm_ref, sendbuf, zero_buf,
             send_sems, recv_sems, local_sems):
        del cols_ref
        step = pl.program_id(0)
        x, y, c = _coords()
        me = 4 * x + 2 * y + c

        def copies_to(n, k, which=(0, 1, 2)):
            peer, peer_idx = _peer(k)
            pairs = [(sendbuf.at[n], rin_ref), (dout_ref.at[peer_idx], rout_ref), (sm_ref, rsm_ref)]
            return [pltpu.make_async_remote_copy(src_ref=pairs[a][0], dst_ref=pairs[a][1].at[me],
                                                 send_sem=send_sems.at[n, a], recv_sem=recv_sems.at[n, a],
                                                 device_id=peer, device_id_type=MESH) for a in which]

        def arrivals_from(n, k):
            peer, peer_idx = _peer(k)
            pairs = [(sendbuf.at[n], rin_ref), (dout_ref.at[me], rout_ref), (sm_ref, rsm_ref)]
            return [pltpu.make_async_remote_copy(src_ref=src, dst_ref=dst.at[peer_idx], send_sem=send_sems.at[n, a],
                                                 recv_sem=recv_sems.at[n, a], device_id=peer, device_id_type=MESH)
                    for a, (src, dst) in enumerate(pairs)]

        local = [pltpu.make_async_copy(zero_buf, rin_ref.at[me], local_sems.at[0]),
                 pltpu.make_async_copy(dout_ref.at[me], rout_ref.at[me], local_sems.at[1]),
                 pltpu.make_async_copy(sm_ref, rsm_ref.at[me], local_sems.at[2])]

        @pl.when(step == 0)
        def _():
            zero_buf[...] = jnp.zeros_like(zero_buf)
            for cp in local:
                cp.start()
            for n, k in enumerate(SEND_ORDER):
                for cp in copies_to(n, k, which=(1, 2)):
                    cp.start()

        dw = jnp.dot(ht_ref[...], dz_ref[...], preferred_element_type=F32)
        for n, k in enumerate(SEND_ORDER):
            @pl.when(step == n)
            def _(n=n, k=k):
                sendbuf[n] = dw.astype(BF16)
                copies_to(n, k, which=(0,))[0].start()

        @pl.when(step == N_DEV - 1)
        def _():
            own_ref[...] = dw
            for n, k in enumerate(SEND_ORDER):
                for cp in arrivals_from(n, k):
                    cp.wait_recv()
            for n, k in enumerate(SEND_ORDER):
                for cp in copies_to(n, k):
                    cp.wait_send()
            for cp in local:
                cp.wait()

    peers = [_peer(k)[1] for k in SEND_ORDER]
    me = 4 * lax.axis_index("x") + 2 * lax.axis_index("y") + lax.axis_index("c")
    cols = jnp.stack(peers + [me]).astype(jnp.int32)
    grid_spec = pltpu.PrefetchScalarGridSpec(
        num_scalar_prefetch=1, grid=(N_DEV,),
        in_specs=[pl.BlockSpec((D_MODEL, seq), lambda s, cols: (0, 0), pipeline_mode=pl.Buffered(1)),
                  pl.BlockSpec((seq, SHARD_IN), lambda s, cols: (0, cols[s])), HBM_SPEC, HBM_SPEC],
        out_specs=[pl.BlockSpec((D_MODEL, SHARD_IN), lambda s, cols: (0, 0)), HBM_SPEC, HBM_SPEC, HBM_SPEC],
        scratch_shapes=[pltpu.VMEM((N_DEV - 1, D_MODEL, SHARD_IN), BF16), pltpu.VMEM((D_MODEL, SHARD_IN), BF16),
                        pltpu.SemaphoreType.DMA((N_DEV - 1, 3)), pltpu.SemaphoreType.DMA((N_DEV - 1, 3)),
                        pltpu.SemaphoreType.DMA((3,))])
    return pl.pallas_call(
        body, name="dw_in_rs", grid_spec=grid_spec,
        out_shape=[jax.ShapeDtypeStruct((D_MODEL, SHARD_IN), F32),
                   jax.ShapeDtypeStruct((N_DEV, D_MODEL, SHARD_IN), BF16),
                   jax.ShapeDtypeStruct((N_DEV, SHARD_OUT, D_MODEL), F32),
                   jax.ShapeDtypeStruct((N_DEV, 8, D_MODEL), F32)],
        compiler_params=_params(1),
    )(cols, ht, dz, dw_out, small)


def _adamw_math(w, g, m, v):
    m = ADAM_B1 * m + (1.0 - ADAM_B1) * g
    v = ADAM_B2 * v + (1.0 - ADAM_B2) * (g * g)
    m_hat = m / (1.0 - ADAM_B1 ** ADAM_STEP)
    v_hat = v / (1.0 - ADAM_B2 ** ADAM_STEP)
    delta = -ADAM_LR * (m_hat / (jnp.sqrt(v_hat) + ADAM_EPS) + ADAM_WD * w)
    return delta, m, v


def _sum_slabs(ref, first=None):
    total = ref[0].astype(F32) if first is None else first + ref[0].astype(F32)
    for s in range(1, N_DEV):
        total = total + ref[s].astype(F32)
    return total


def _adamw_slabs(parts, own, w, m, v, name, tr):
    rows, cols = w.shape
    tile = pl.BlockSpec((tr, cols), lambda i: (i, 0))

    def body(p_ref, *refs):
        own_ref = refs[0] if own is not None else None
        w_ref, m_ref, v_ref, g_ref, d_ref, nm_ref, nv_ref = refs[-7:]
        g = _sum_slabs(p_ref, None if own_ref is None else own_ref[...])
        g_ref[...] = g
        d_ref[...], nm_ref[...], nv_ref[...] = _adamw_math(w_ref[...], g, m_ref[...], v_ref[...])

    extra = [] if own is None else [own]
    return pl.pallas_call(
        body, name=name, grid=(rows // tr,),
        in_specs=[pl.BlockSpec((N_DEV, tr, cols), lambda i: (0, i, 0))] + [tile] * (len(extra) + 3),
        out_specs=[tile] * 4,
        out_shape=[jax.ShapeDtypeStruct((rows, cols), F32)] * 4,
        compiler_params=_params(1),
    )(parts, *extra, w, m, v)


def _sum_small(parts):
    def body(p_ref, out_ref):
        out_ref[...] = _sum_slabs(p_ref)

    return pl.pallas_call(body, name="sum_small", out_shape=jax.ShapeDtypeStruct(parts.shape[1:], F32))(parts)


def _adamw_whole(g, w, m, v, name):
    def body(g_ref, w_ref, m_ref, v_ref, d_ref, nm_ref, nv_ref):
        d_ref[...], nm_ref[...], nv_ref[...] = _adamw_math(w_ref[...], g_ref[...], m_ref[...], v_ref[...])

    return pl.pallas_call(body, name=name, out_shape=[jax.ShapeDtypeStruct(w.shape, F32)] * 3)(g, w, m, v)


def kernel(x, norm_pre_g, w_in, conv_w, w_out, norm_post_g, loss_target, m_norm_pre_g, m_w_in, m_conv_w, m_w_out,
           m_norm_post_g, v_norm_pre_g, v_w_in, v_conv_w, v_w_out, v_norm_post_g):
    n_conv = CONV_W // N_DEV
    w_in_g, w_out_g, conv_g = _ag_weights(w_in, w_out, conv_w)
    conv_full = conv_g[:, 0:3, 0:n_conv].transpose(1, 0, 2).reshape(3, CONV_W)
    grad_x, ht, dz, dw_out, small = _local_step(x[0], loss_target[0], norm_pre_g, norm_post_g, w_in_g,
                                                w_out_g.reshape(D_MODEL, D_MODEL), conv_full)
    own_in, r_in, r_out, r_small = _dw_in_rs(ht, dz, dw_out.reshape(N_DEV, SHARD_OUT, D_MODEL), small)
    g_in, d_in, nm_in, nv_in = _adamw_slabs(r_in, own_in, w_in, m_w_in, v_w_in, "adamw_in", 256)
    g_out, d_out, nm_out, nv_out = _adamw_slabs(r_out, None, w_out, m_w_out, v_w_out, "adamw_out", SHARD_OUT)
    sums = _sum_small(r_small)
    g_pre, g_post, loss = sums[0], sums[1], sums[2, 0]
    me = 4 * lax.axis_index("x") + 2 * lax.axis_index("y") + lax.axis_index("c")
    g_conv = lax.dynamic_slice(sums[3:6, 0:CONV_W], (0, me * n_conv), (3, n_conv))
    vec = lambda a: a.reshape(1, D_MODEL)
    d_pre, nm_pre, nv_pre = _adamw_whole(vec(g_pre), vec(norm_pre_g), vec(m_norm_pre_g), vec(v_norm_pre_g), "adamw_pre")
    d_post, nm_post, nv_post = _adamw_whole(vec(g_post), vec(norm_post_g), vec(m_norm_post_g), vec(v_norm_post_g),
                                            "adamw_post")
    d_conv, nm_conv, nv_conv = _adamw_whole(g_conv, conv_w, m_conv_w, v_conv_w, "adamw_conv")
    flat = lambda a: a.reshape(D_MODEL)
    return (loss, grad_x[None], g_pre, g_in, g_conv, g_out, g_post,
            flat(d_pre), d_in, d_conv, d_out, flat(d_post),
            flat(nm_pre), nm_in, nm_conv, nm_out, flat(nm_post),
            flat(nv_pre), nv_in, nv_conv, nv_out, flat(nv_post))
```

```python
import functools

import jax
import jax.numpy as jnp
import numpy as np
from jax import lax
from jax.experimental import pallas as pl
from jax.experimental.pallas import tpu as pltpu

F32 = jnp.float32
BF16 = jnp.bfloat16

D_MODEL = 1024
HEAD_DIM = 64
ATTN_W = 768
CONV_W = 256
IN_W = 4096
REST_W = IN_W - 3 * ATTN_W
BLK = 128
N_DEV = 8
SHARD_IN = IN_W // N_DEV
SHARD_OUT = D_MODEL // N_DEV
DILATIONS = (1, 4, 16)
ROPE_THETA = 10000.0
NORM_EPS = 1e-6
NEG = -1e30

ADAM_LR = 0.001
ADAM_B1 = 0.9
ADAM_B2 = 0.999
ADAM_EPS = 1e-08
ADAM_WD = 0.01
ADAM_STEP = 10

VMEM_LIMIT = 56 * 1024 * 1024
MESH = pl.DeviceIdType.MESH


def _params(n_grid):
    return pltpu.CompilerParams(dimension_semantics=("arbitrary",) * n_grid, vmem_limit_bytes=VMEM_LIMIT)


def _resident(shape):
    zeros = (0,) * len(shape)
    return pl.BlockSpec(shape, lambda *_: zeros, pipeline_mode=pl.Buffered(1))


def _sigmoid(a):
    return 1.0 / (1.0 + jnp.exp(-a))


def _swap_halves(t, first_half):
    return jnp.where(first_half, pltpu.roll(t, BLK - 32, 1), pltpu.roll(t, 32, 1))


def _rope_tables(seq):
    half = HEAD_DIM // 2
    inv_freq = ROPE_THETA ** (-jnp.arange(half, dtype=F32) * 2.0 / HEAD_DIM)
    ang = jnp.arange(seq).astype(F32)[:, None] * inv_freq[None, :]
    cos, sin = jnp.cos(ang), jnp.sin(ang)
    return jnp.concatenate([cos] * 4, axis=1), jnp.concatenate([-sin, sin, -sin, sin], axis=1)


N_CHUNK = ATTN_W // BLK


def _lanes(r, c):
    return slice(r * ATTN_W + c * BLK, r * ATTN_W + (c + 1) * BLK)


def _to_residues(src, chunk0, dst_refs, tmp, rows, dtype):
    assert DILATIONS == (1, 4, 16)
    dst1, dst4, dst16 = dst_refs
    n4, n16 = rows // 4, rows // 16
    for c in range(N_CHUNK):
        dst1[:, _lanes(0, c)] = src[chunk0 + c].astype(dtype)
        for r1 in range(4):
            tmp[c, r1 * n4:(r1 + 1) * n4, :] = src[chunk0 + c, pl.ds(r1, n4, stride=4), :]
        for r1 in range(4):
            dst4[:, _lanes(r1, c)] = tmp[c, r1 * n4:(r1 + 1) * n4, :].astype(dtype)
            for r2 in range(4):
                dst16[:, _lanes(4 * r2 + r1, c)] = tmp[c, pl.ds(r1 * n4 + r2, n16, stride=4), :].astype(dtype)


def _from_residue(src_ref, dst, dil, rows, accumulate, tmp=None):
    n4, n16 = rows // 4, rows // 16

    def put(where, piece):
        if accumulate:
            dst[where] += piece
        else:
            dst[where] = piece

    for c in range(N_CHUNK):
        if dil == 1:
            put((c,), src_ref[:, _lanes(0, c)].astype(F32))
            continue
        for r1 in range(4):
            if dil == 4:
                piece = src_ref[:, _lanes(r1, c)].astype(F32)
            else:
                for r2 in range(4):
                    tmp[c, pl.ds(r1 * n4 + r2, n16, stride=4), :] = src_ref[:, _lanes(4 * r2 + r1, c)].astype(F32)
                piece = tmp[c, r1 * n4:(r1 + 1) * n4, :]
            put((c, pl.ds(r1, n4, stride=4), slice(None)), piece)


def _residue_spec(tm, dil):
    return pl.BlockSpec((tm // dil, dil * ATTN_W), lambda i: (i, 0))


def _residue_shape(seq, dil, dtype):
    return jax.ShapeDtypeStruct((seq // dil, dil * ATTN_W), dtype)


def _fwd_in(x, g_pre, w_in_g, cos_t, sin_t, tm=256):
    seq = x.shape[0]
    n_dil = len(DILATIONS)

    def body(x_ref, g_ref, w_ref, cos_ref, sin_ref, *rest):
        qkv_refs, (zr_ref, ht_ref, z_scr, qkv_scr, tmp) = rest[:3 * n_dil], rest[3 * n_dil:]
        xv = x_ref[...]
        r = lax.rsqrt(jnp.mean(xv * xv, axis=-1, keepdims=True) + NORM_EPS)
        hf = (xv * r) * g_ref[...]
        ht_ref[...] = hf.T.astype(BF16)
        h = hf.astype(BF16)
        for j in range(N_DEV):
            z_scr[:, j * SHARD_IN:(j + 1) * SHARD_IN] = jnp.dot(h, w_ref[j], preferred_element_type=F32)
        cos, sin = cos_ref[...], sin_ref[...]
        first_half = (lax.broadcasted_iota(jnp.int32, (tm, BLK), 1) & 32) == 0

        def rope(t):
            return t * cos + _swap_halves(t, first_half) * sin

        for c in range(N_CHUNK):
            qkv_scr[c] = rope(z_scr[:, c * BLK:(c + 1) * BLK]) * HEAD_DIM ** -0.5
            qkv_scr[N_CHUNK + c] = rope(z_scr[:, ATTN_W + c * BLK:ATTN_W + (c + 1) * BLK])
            qkv_scr[2 * N_CHUNK + c] = z_scr[:, 2 * ATTN_W + c * BLK:2 * ATTN_W + (c + 1) * BLK]
        for a in range(3):
            _to_residues(qkv_scr, a * N_CHUNK, [qkv_refs[3 * n + a] for n in range(n_dil)], tmp, tm, BF16)
        zr_ref[...] = z_scr[:, 3 * ATTN_W:]

    row = lambda w: pl.BlockSpec((tm, w), lambda i: (i, 0))
    outs = pl.pallas_call(
        body, name="fwd_in", grid=(seq // tm,),
        in_specs=[row(D_MODEL), _resident((1, D_MODEL)), _resident((N_DEV, D_MODEL, SHARD_IN)), row(BLK), row(BLK)],
        out_specs=[_residue_spec(tm, dil) for dil in DILATIONS for _ in range(3)]
        + [row(REST_W), pl.BlockSpec((D_MODEL, tm), lambda i: (0, i))],
        out_shape=[_residue_shape(seq, dil, BF16) for dil in DILATIONS for _ in range(3)]
        + [jax.ShapeDtypeStruct((seq, REST_W), F32), jax.ShapeDtypeStruct((D_MODEL, seq), BF16)],
        scratch_shapes=[pltpu.VMEM((tm, IN_W), F32), pltpu.VMEM((3 * N_CHUNK, tm, BLK), F32),
                        pltpu.VMEM((N_CHUNK, tm, BLK), F32)],
        compiler_params=_params(1),
    )(x, g_pre.reshape(1, D_MODEL), w_in_g, cos_t, sin_t)
    qkv = [tuple(outs[3 * n:3 * n + 3]) for n in range(n_dil)]
    return qkv, outs[3 * n_dil], outs[3 * n_dil + 1]


def _band_bias(first_block):
    kj = lax.broadcasted_iota(jnp.int32, (2 * BLK, BLK), 0)
    qi = lax.broadcasted_iota(jnp.int32, (2 * BLK, BLK), 1)
    valid = (kj >= qi) & (kj <= qi + BLK)
    bias = jnp.where(valid, 0.0, NEG).astype(BF16)
    bias_first = jnp.where(valid & (kj >= BLK), 0.0, NEG).astype(BF16)
    onehot = ((kj & (BLK - 1)) == qi).astype(F32).astype(BF16)
    return onehot, bias, jnp.where(first_block, bias_first, bias)


def _stack_heads(t, head0):
    keep0 = head0.astype(F32).astype(BF16)
    return jnp.concatenate([t * keep0, t * (1 - keep0)], axis=0)


def _unstack_heads(t2, head0):
    return jnp.where(head0, t2[:BLK], t2[BLK:])


def _rows_per_head(a, head0):
    b = pltpu.roll(a, HEAD_DIM, 1)
    rows = jnp.concatenate([jnp.where(head0, a, b), jnp.where(head0, b, a)], axis=0)
    return jnp.concatenate([rows, rows], axis=1)


BLOCKS_PER_STEP = 16


def _attn_specs(length, dil):
    n_blocks = length // BLK
    tb = min(BLOCKS_PER_STEP, n_blocks)
    nc = BLOCKS_PER_STEP // tb
    assert (dil * N_CHUNK) % nc == 0 and n_blocks % tb == 0
    tile = pl.BlockSpec((tb * BLK, nc * BLK), lambda c, t: (t, c))
    prev = pl.BlockSpec((BLK, nc * BLK), lambda c, t: (jnp.maximum(t * tb - 1, 0), c))
    grid = (dil * N_CHUNK // nc, n_blocks // tb)
    return tb, nc, tile, prev, grid


def _load_keys(cat, prev_ref, cur_ref):
    cat[0:BLK] = prev_ref[...]
    cat[BLK:] = cur_ref[...]


def _attn_fwd(q, k, v, dil):
    length = q.shape[0]
    tb, nc, tile, prev, grid = _attn_specs(length, dil)

    def body(q_ref, kc_ref, kp_ref, vc_ref, vp_ref, o_ref, lse_ref, kcat, vcat):
        _load_keys(kcat, kp_ref, kc_ref)
        _load_keys(vcat, vp_ref, vc_ref)
        head0 = lax.broadcasted_iota(jnp.int32, (BLK, BLK), 1) < HEAD_DIM
        onehot, bias, bias_start = _band_bias(pl.program_id(1) == 0)
        ones = jnp.ones((2 * BLK, BLK), BF16)
        for c in range(nc):
            cols = slice(c * BLK, (c + 1) * BLK)
            for j in range(tb):
                rows = slice(j * BLK, (j + 1) * BLK)
                q2 = jnp.concatenate([_stack_heads(q_ref[rows, cols], head0), onehot], axis=1)
                kk = jnp.concatenate([kcat[j * BLK:(j + 2) * BLK, cols], bias_start if j == 0 else bias], axis=1)
                s = lax.dot_general(q2, kk, (((1,), (1,)), ((), ())), preferred_element_type=F32)
                m = jnp.max(s, axis=1, keepdims=True)
                p = jnp.exp(s - m).astype(BF16)
                vv = jnp.concatenate([vcat[j * BLK:(j + 2) * BLK, cols], ones], axis=1)
                pv = jnp.dot(p, vv, preferred_element_type=F32)
                den = pv[:, BLK:]
                o_ref[rows, cols] = _unstack_heads(pv[:, :BLK] / den, head0).astype(BF16)
                lse_ref[rows, cols] = _unstack_heads(m + jnp.log(den), head0)

    return pl.pallas_call(
        body, name=f"attn_fwd_d{dil}", grid=grid,
        in_specs=[tile, tile, prev, tile, prev], out_specs=[tile, tile],
        out_shape=[jax.ShapeDtypeStruct(q.shape, BF16), jax.ShapeDtypeStruct(q.shape, F32)],
        scratch_shapes=[pltpu.VMEM(((tb + 1) * BLK, nc * BLK), BF16)] * 2,
        compiler_params=_params(2),
    )(q, k, k, v, v)


def _attn_bwd(q, k, v, do, lse, delta, dil):
    length = q.shape[0]
    tb, nc, tile, prev, grid = _attn_specs(length, dil)
    whole = pl.BlockSpec((length, nc * BLK), lambda c, t: (0, c))

    def body(q_ref, do_ref, lse_ref, dl_ref, kc_ref, kp_ref, vc_ref, vp_ref, dq_ref, dk_ref, dv_ref, kcat, vcat):
        t = pl.program_id(1)
        _load_keys(kcat, kp_ref, kc_ref)
        _load_keys(vcat, vp_ref, vc_ref)
        head0 = lax.broadcasted_iota(jnp.int32, (BLK, BLK), 1) < HEAD_DIM
        onehot, bias, bias_start = _band_bias(t == 0)
        for c in range(nc):
            cols = slice(c * BLK, (c + 1) * BLK)
            for j in range(tb):
                rows = slice(j * BLK, (j + 1) * BLK)
                q2 = _stack_heads(q_ref[rows, cols], head0)
                do2 = _stack_heads(do_ref[rows, cols], head0)
                kk = kcat[j * BLK:(j + 2) * BLK, cols]
                vv = vcat[j * BLK:(j + 2) * BLK, cols]
                s = lax.dot_general(jnp.concatenate([q2, onehot], axis=1),
                                    jnp.concatenate([kk, bias_start if j == 0 else bias], axis=1),
                                    (((1,), (1,)), ((), ())), preferred_element_type=F32)
                p = jnp.exp(s - _rows_per_head(lse_ref[rows, cols], head0))
                dp = lax.dot_general(do2, vv, (((1,), (1,)), ((), ())), preferred_element_type=F32)
                ds = (p * (dp - _rows_per_head(dl_ref[rows, cols].astype(F32), head0))).astype(BF16)
                dq2 = jnp.dot(ds, kk, preferred_element_type=F32)
                dq_ref[rows, cols] = (_unstack_heads(dq2, head0) * HEAD_DIM ** -0.5).astype(BF16)
                dk2 = lax.dot_general(ds, q2, (((0,), (0,)), ((), ())), preferred_element_type=F32)
                dv2 = lax.dot_general(p.astype(BF16), do2, (((0,), (0,)), ((), ())), preferred_element_type=F32)
                own = pl.ds(pl.multiple_of((t * tb + j) * BLK, BLK), BLK)
                dk_ref[own, cols] = dk2[BLK:].astype(BF16)
                dv_ref[own, cols] = dv2[BLK:].astype(BF16)

                def add_to_previous(j=j, cols=cols, dk2=dk2, dv2=dv2):
                    before = pl.ds(pl.multiple_of((t * tb + j - 1) * BLK, BLK), BLK)
                    dk_ref[before, cols] = (dk_ref[before, cols].astype(F32) + dk2[:BLK]).astype(BF16)
                    dv_ref[before, cols] = (dv_ref[before, cols].astype(F32) + dv2[:BLK]).astype(BF16)

                if j == 0:
                    pl.when(t > 0)(add_to_previous)
                else:
                    add_to_previous()

    return pl.pallas_call(
        body, name=f"attn_bwd_d{dil}", grid=grid,
        in_specs=[tile, tile, tile, tile, tile, prev, tile, prev], out_specs=[tile, whole, whole],
        out_shape=[jax.ShapeDtypeStruct(q.shape, BF16)] * 3,
        scratch_shapes=[pltpu.VMEM(((tb + 1) * BLK, nc * BLK), BF16)] * 2,
        compiler_params=_params(2),
    )(q, do, lse, delta, k, k, v, v)


def _conv_taps(u, before8, tm):
    row = lax.broadcasted_iota(jnp.int32, u.shape, 0)
    last, last2 = before8[7:8, :], before8[6:7, :]
    u1 = jnp.where(row == 0, last, pltpu.roll(u, 1, 0))
    u2 = jnp.where(row == 0, last2, jnp.where(row == 1, last, pltpu.roll(u, 2, 0)))
    return u1, u2


def _attn_combine(o_parts, lse_parts, zr, conv_w, tm=256):
    seq = zr.shape[0]
    a0, h0, b0, c0, g0 = 0, ATTN_W, ATTN_W + CONV_W, ATTN_W + 2 * CONV_W, ATTN_W + 3 * CONV_W

    def body(o1, o2, o3, l1, l2, l3, zr_ref, zp_ref, w_ref, mixed_ref, o_ref, lse1, lse2, lse3, *scr):
        i = pl.program_id(0)
        for src, dst, dil in zip((o2, o3, l2, l3), scr[:4], DILATIONS[1:] * 2):
            _from_residue(src, dst, dil, tm, accumulate=False, tmp=scr[5])
        for c in range(N_CHUNK):
            cols = slice(c * BLK, (c + 1) * BLK)
            la, lb, lc = l1[:, cols], scr[2][c], scr[3][c]
            top = jnp.maximum(jnp.maximum(la, lb), lc)
            ea, eb, ec = jnp.exp(la - top), jnp.exp(lb - top), jnp.exp(lc - top)
            den = ea + eb + ec
            o = (ea / den) * o1[:, cols].astype(F32) + (eb / den) * scr[0][c] + (ec / den) * scr[1][c]
            o_ref[:, cols] = o
            scr[4][c] = top + jnp.log(den)
            ga = zr_ref[:, cols]
            mixed_ref[:, cols] = (o * (ga * _sigmoid(ga))).astype(BF16)
        _to_residues(scr[4], 0, (lse1, lse2, lse3), scr[5], tm, F32)
        u = zr_ref[:, c0:g0] * zr_ref[:, h0:b0]
        before = jnp.where(i > 0, zp_ref[:, c0:g0] * zp_ref[:, h0:b0], 0.0)
        u1, u2 = _conv_taps(u, before, tm)
        y = u2 * w_ref[0:1, :] + u1 * w_ref[1:2, :] + u * w_ref[2:3, :]
        gc = zr_ref[:, g0:]
        mixed_ref[:, ATTN_W:] = ((zr_ref[:, b0:c0] * y) * (gc * _sigmoid(gc))).astype(BF16)

    row = lambda w: pl.BlockSpec((tm, w), lambda i: (i, 0))
    before8 = pl.BlockSpec((8, REST_W), lambda i: (jnp.maximum(i * (tm // 8) - 1, 0), 0))
    views = [_residue_spec(tm, dil) for dil in DILATIONS]
    outs = pl.pallas_call(
        body, name="attn_combine", grid=(seq // tm,),
        in_specs=views * 2 + [row(REST_W), before8, _resident((3, CONV_W))],
        out_specs=[row(D_MODEL), row(ATTN_W)] + views,
        out_shape=[jax.ShapeDtypeStruct((seq, D_MODEL), BF16), jax.ShapeDtypeStruct((seq, ATTN_W), F32)]
        + [_residue_shape(seq, dil, F32) for dil in DILATIONS],
        scratch_shapes=[pltpu.VMEM((N_CHUNK, tm, BLK), F32)] * 6,
        compiler_params=_params(1),
    )(*o_parts, *lse_parts, zr, zr, conv_w)
    return outs[0], outs[1], outs[2:]


def _out_loss_bwd(mixed, w_out_g, x, target, g_post, tm=512):
    seq = x.shape[0]

    def body(mx_ref, w_ref, x_ref, t_ref, g_ref, dout_ref, dmx_ref, dw_ref, dwb_ref, st_ref):
        i = pl.program_id(0)
        mx = mx_ref[...]
        y = jnp.dot(mx, w_ref[...], preferred_element_type=F32)
        r = lax.rsqrt(jnp.mean(y * y, axis=-1, keepdims=True) + NORM_EPS)
        yhat = y * r
        g = g_ref[...]
        err = (x_ref[...] + yhat * g) - t_ref[...]
        dn = err * (1.0 / D_MODEL)
        dout_ref[...] = dn
        tg = dn * g
        dy = (r * (tg - yhat * jnp.mean(tg * yhat, axis=-1, keepdims=True))).astype(BF16)
        dmx_ref[...] = lax.dot_general(dy, w_ref[...], (((1,), (1,)), ((), ())), preferred_element_type=F32)
        dw = lax.dot_general(mx, dy, (((0,), (0,)), ((), ())), preferred_element_type=F32)
        gsum = jnp.sum(dn * yhat, axis=0, keepdims=True)
        lsum = jnp.broadcast_to(0.5 / D_MODEL * jnp.sum(err * err), (1, D_MODEL))

        @pl.when(i == 0)
        def _():
            dw_ref[...] = dw
            st_ref[...] = jnp.zeros_like(st_ref)
            st_ref[0:1, :] = gsum
            st_ref[1:2, :] = lsum

        @pl.when(i > 0)
        def _():
            dw_ref[...] += dw
            st_ref[0:1, :] += gsum
            st_ref[1:2, :] += lsum

        @pl.when(i == seq // tm - 1)
        def _():
            dwb_ref[...] = dw_ref[...].astype(BF16)

    row = lambda w: pl.BlockSpec((tm, w), lambda i: (i, 0))
    whole = pl.BlockSpec((D_MODEL, D_MODEL), lambda i: (0, 0))
    return pl.pallas_call(
        body, name="out_loss_bwd", grid=(seq // tm,),
        in_specs=[row(D_MODEL), _resident((D_MODEL, D_MODEL)), row(D_MODEL), row(D_MODEL), _resident((1, D_MODEL))],
        out_specs=[row(D_MODEL), row(D_MODEL), whole, whole, pl.BlockSpec((8, D_MODEL), lambda i: (0, 0))],
        out_shape=[jax.ShapeDtypeStruct((seq, D_MODEL), F32), jax.ShapeDtypeStruct((seq, D_MODEL), F32),
                   jax.ShapeDtypeStruct((D_MODEL, D_MODEL), F32), jax.ShapeDtypeStruct((D_MODEL, D_MODEL), BF16),
                   jax.ShapeDtypeStruct((8, D_MODEL), F32)],
        compiler_params=_params(1),
    )(mixed, w_out_g, x, target, g_post.reshape(1, D_MODEL))


def _head_sum(prod, same_head):
    hi = prod.astype(BF16)
    lo = (prod - hi.astype(F32)).astype(BF16)
    return (jnp.dot(hi, same_head, preferred_element_type=F32) + jnp.dot(lo, same_head, preferred_element_type=F32))


def _gate_bwd(dmixed, zr, o, conv_w, tm=256):
    seq = zr.shape[0]
    n_tiles = seq // tm
    n_dil = len(DILATIONS)
    a0, h0, b0, c0, g0 = 0, ATTN_W, ATTN_W + CONV_W, ATTN_W + 2 * CONV_W, ATTN_W + 3 * CONV_W

    def body(dm_ref, dmn_ref, zr_ref, zp_ref, zn_ref, o_ref, w_ref, *rest):
        do_refs, dl_refs = rest[:n_dil], rest[n_dil:2 * n_dil]
        dz_ref, dw_ref, do_scr, dl_scr, tmp = rest[2 * n_dil:]
        i = pl.program_id(0)
        ga = zr_ref[:, a0:h0]
        sg = _sigmoid(ga)
        dattn = dm_ref[:, 0:ATTN_W]
        ov = o_ref[...]
        do = dattn * (ga * sg)
        dz_ref[:, a0:h0] = (dattn * ov * (sg * (1.0 + ga * (1.0 - sg)))).astype(BF16)
        li = lax.broadcasted_iota(jnp.int32, (BLK, BLK), 0) // HEAD_DIM
        lj = lax.broadcasted_iota(jnp.int32, (BLK, BLK), 1) // HEAD_DIM
        same_head = (li == lj).astype(BF16)
        prod = do * ov
        for c in range(N_CHUNK):
            cols = slice(c * BLK, (c + 1) * BLK)
            do_scr[c] = do[:, cols]
            dl_scr[c] = _head_sum(prod[:, cols], same_head)
        _to_residues(do_scr, 0, do_refs, tmp, tm, BF16)
        _to_residues(dl_scr, 0, dl_refs, tmp, tm, BF16)

        ch, cb, cc, gc = zr_ref[:, h0:b0], zr_ref[:, b0:c0], zr_ref[:, c0:g0], zr_ref[:, g0:]
        u = cc * ch
        before = jnp.where(i > 0, zp_ref[:, c0:g0] * zp_ref[:, h0:b0], 0.0)
        u1, u2 = _conv_taps(u, before, tm)
        w0, w1, w2 = w_ref[0:1, :], w_ref[1:2, :], w_ref[2:3, :]
        y = u2 * w0 + u1 * w1 + u * w2
        sc = _sigmoid(gc)
        silu_c = gc * sc
        dconv = dm_ref[:, ATTN_W:]
        dz_ref[:, b0:c0] = (dconv * y * silu_c).astype(BF16)
        dz_ref[:, g0:] = (dconv * (cb * y) * (sc * (1.0 + gc * (1.0 - sc)))).astype(BF16)
        dy = dconv * cb * silu_c
        gn = zn_ref[:, g0:]
        after = jnp.where(i < n_tiles - 1, dmn_ref[:, ATTN_W:] * zn_ref[:, b0:c0] * (gn * _sigmoid(gn)), 0.0)
        row = lax.broadcasted_iota(jnp.int32, dy.shape, 0)
        nxt, nxt2 = after[0:1, :], after[1:2, :]
        dy1 = jnp.where(row == tm - 1, nxt, pltpu.roll(dy, tm - 1, 0))
        dy2 = jnp.where(row == tm - 1, nxt2, jnp.where(row == tm - 2, nxt, pltpu.roll(dy, tm - 2, 0)))
        du = dy * w2 + dy1 * w1 + dy2 * w0
        dz_ref[:, c0:g0] = (du * ch).astype(BF16)
        dz_ref[:, h0:b0] = (du * cc).astype(BF16)
        dws = [jnp.sum(dy * u2, axis=0, keepdims=True), jnp.sum(dy * u1, axis=0, keepdims=True),
               jnp.sum(dy * u, axis=0, keepdims=True)]

        @pl.when(i == 0)
        def _():
            dw_ref[...] = jnp.zeros_like(dw_ref)

        for n, part in enumerate(dws):
            dw_ref[n:n + 1, :] += part

    row_spec = lambda w: pl.BlockSpec((tm, w), lambda i: (i, 0))
    before8 = pl.BlockSpec((8, REST_W), lambda i: (jnp.maximum(i * (tm // 8) - 1, 0), 0))
    after8 = lambda w: pl.BlockSpec((8, w), lambda i: (jnp.minimum((i + 1) * (tm // 8), seq // 8 - 1), 0))
    views = [_residue_spec(tm, dil) for dil in DILATIONS]
    outs = pl.pallas_call(
        body, name="gate_bwd", grid=(n_tiles,),
        in_specs=[row_spec(D_MODEL), after8(D_MODEL), row_spec(REST_W), before8, after8(REST_W), row_spec(ATTN_W),
                  _resident((3, CONV_W))],
        out_specs=views * 2 + [row_spec(REST_W), pl.BlockSpec((8, CONV_W), lambda i: (0, 0))],
        out_shape=[_residue_shape(seq, dil, BF16) for dil in DILATIONS] * 2
        + [jax.ShapeDtypeStruct((seq, REST_W), BF16), jax.ShapeDtypeStruct((8, CONV_W), F32)],
        scratch_shapes=[pltpu.VMEM((N_CHUNK, tm, BLK), F32)] * 3,
        compiler_params=_params(1),
    )(dmixed, dmixed, zr, zr, zr, o, conv_w)
    return outs[:n_dil], outs[n_dil:2 * n_dil], outs[2 * n_dil], outs[2 * n_dil + 1]


def _in_bwd(dqs, dks, dvs, dzr, cos_t, sin_t, x, d_out, g_pre, w_in_g, tm=256):
    seq = x.shape[0]

    def body(q1, q2, q3, k1, k2, k3, v1, v2, v3, dzr_ref, cos_ref, sin_ref, x_ref, dout_ref, g_ref, w_ref,
             dz_ref, gx_ref, st_ref, dq_scr, dk_scr, dv_scr, tmp):
        i = pl.program_id(0)
        for parts, total in (((q1, q2, q3), dq_scr), ((k1, k2, k3), dk_scr), ((v1, v2, v3), dv_scr)):
            for n, dil in enumerate(DILATIONS):
                _from_residue(parts[n], total, dil, tm, accumulate=n > 0, tmp=tmp)
        cos, sin = cos_ref[...], sin_ref[...]
        first_half = (lax.broadcasted_iota(jnp.int32, (tm, BLK), 1) & 32) == 0

        def unrope(t):
            return t * cos - _swap_halves(t, first_half) * sin

        for c in range(N_CHUNK):
            dz_ref[:, c * BLK:(c + 1) * BLK] = unrope(dq_scr[c]).astype(BF16)
            dz_ref[:, ATTN_W + c * BLK:ATTN_W + (c + 1) * BLK] = unrope(dk_scr[c]).astype(BF16)
            dz_ref[:, 2 * ATTN_W + c * BLK:2 * ATTN_W + (c + 1) * BLK] = dv_scr[c].astype(BF16)
        dz_ref[:, 3 * ATTN_W:] = dzr_ref[...]
        dh = jnp.zeros((tm, D_MODEL), F32)
        for j in range(N_DEV):
            dh += lax.dot_general(dz_ref[:, j * SHARD_IN:(j + 1) * SHARD_IN], w_ref[j], (((1,), (1,)), ((), ())),
                                  preferred_element_type=F32)
        xv = x_ref[...]
        r = lax.rsqrt(jnp.mean(xv * xv, axis=-1, keepdims=True) + NORM_EPS)
        xhat = xv * r
        tg = dh * g_ref[...]
        gx_ref[...] = dout_ref[...] + r * (tg - xhat * jnp.mean(tg * xhat, axis=-1, keepdims=True))
        gsum = jnp.sum(dh * xhat, axis=0, keepdims=True)

        @pl.when(i == 0)
        def _():
            st_ref[...] = jnp.zeros_like(st_ref)

        st_ref[0:1, :] += gsum

    row = lambda w: pl.BlockSpec((tm, w), lambda i: (i, 0))
    return pl.pallas_call(
        body, name="in_bwd", grid=(seq // tm,),
        in_specs=[_residue_spec(tm, dil) for dil in DILATIONS] * 3
        + [row(REST_W), row(BLK), row(BLK), row(D_MODEL), row(D_MODEL), _resident((1, D_MODEL)),
           _resident((N_DEV, D_MODEL, SHARD_IN))],
        out_specs=[row(IN_W), row(D_MODEL), pl.BlockSpec((8, D_MODEL), lambda i: (0, 0))],
        out_shape=[jax.ShapeDtypeStruct((seq, IN_W), BF16), jax.ShapeDtypeStruct((seq, D_MODEL), F32),
                   jax.ShapeDtypeStruct((8, D_MODEL), F32)],
        scratch_shapes=[pltpu.VMEM((N_CHUNK, tm, BLK), F32)] * 4,
        compiler_params=_params(1),
    )(*dqs, *dks, *dvs, dzr, cos_t, sin_t, x, d_out, g_pre.reshape(1, D_MODEL), w_in_g)


def _local_step(x, target, g_pre, g_post, w_in_g, w_out_g, conv_w):
    seq = x.shape[0]
    cos_t, sin_t = _rope_tables(seq)
    qkv, zr, ht = _fwd_in(x, g_pre, w_in_g, cos_t, sin_t)
    parts = [_attn_fwd(*qkv[n], dil) for n, dil in enumerate(DILATIONS)]
    mixed, o, lse = _attn_combine([p[0] for p in parts], [p[1] for p in parts], zr, conv_w)
    d_out, dmixed, dw_out, dw_out_bf, st_post = _out_loss_bwd(mixed, w_out_g, x, target, g_post)
    do, delta, dzr, dconv = _gate_bwd(dmixed, zr, o, conv_w)
    grads = [_attn_bwd(*qkv[n], do[n], lse[n], delta[n], dil) for n, dil in enumerate(DILATIONS)]
    dz, grad_x, st_pre = _in_bwd([g[0] for g in grads], [g[1] for g in grads], [g[2] for g in grads], dzr,
                                 cos_t, sin_t, x, d_out, g_pre, w_in_g)
    conv_rows = jnp.pad(dconv[0:3], ((0, 0), (0, D_MODEL - CONV_W)))
    small = jnp.concatenate([st_pre[0:1], st_post[0:2], conv_rows, jnp.zeros((2, D_MODEL), F32)], axis=0)
    return grad_x, ht, dz, dw_out, dw_out_bf, small


def _coords():
    return lax.axis_index("x"), lax.axis_index("y"), lax.axis_index("c")


def _peer(k):
    x, y, c = _coords()
    px = 1 - x if k & 4 else x
    py = 1 - y if k & 2 else y
    pc = 1 - c if k & 1 else c
    return (px, py, pc), 4 * px + 2 * py + pc


HBM_SPEC = pl.BlockSpec(memory_space=pltpu.HBM)
VMEM_SPEC = pl.BlockSpec(memory_space=pltpu.VMEM)


def _ag_weights(w_in, w_out, conv_w):
    def body(win_ref, wout_ref, cw_ref, gin_ref, gout_ref, gcw_ref, win_bf, wout_bf, cw_pad, send_sems, recv_sems,
             local_sems):
        x, y, c = _coords()
        me, sibling = (x, y, c), (x, y, 1 - c)
        chips = [(1 - x, y), (x, 1 - y), (1 - x, 1 - y)]
        slab = lambda px, py, pc: 4 * px + 2 * py + pc
        win_bf[...] = win_ref[...].astype(BF16)
        wout_bf[...] = wout_ref[...].astype(BF16)
        cw_pad[...] = jnp.zeros_like(cw_pad)
        cw_pad[0:3, 0:CONV_W // N_DEV] = cw_ref[...]
        mine = [win_bf, wout_bf, cw_pad]
        gathered = [gin_ref, gout_ref, gcw_ref]

        def copies(k, block, to, own=False):
            return [pltpu.make_async_remote_copy(src_ref=mine[a] if own else gathered[a].at[slab(*block)],
                                                 dst_ref=gathered[a].at[slab(*block)], send_sem=send_sems.at[k, a],
                                                 recv_sem=recv_sems.at[k, a], device_id=to, device_id_type=MESH)
                    for a in range(3)]

        local = [pltpu.make_async_copy(mine[a], gathered[a].at[slab(*me)], local_sems.at[a]) for a in range(3)]
        for cp in local:
            cp.start()
        first = copies(0, me, sibling, own=True)
        for j, chip in enumerate(chips):
            first += copies(1 + j, me, (*chip, c), own=True)
        for cp in first:
            cp.start()
        passed = []
        for j, chip in enumerate(chips):
            for cp in copies(1 + j, (*chip, c), me):
                cp.wait_recv()
            onward = copies(4 + j, (*chip, c), sibling)
            for cp in onward:
                cp.start()
            passed += onward
        for cp in copies(0, sibling, me):
            cp.wait_recv()
        for j, chip in enumerate(chips):
            for cp in copies(4 + j, (*chip, 1 - c), me):
                cp.wait_recv()
        for cp in first + passed:
            cp.wait_send()
        for cp in local:
            cp.wait()

    return pl.pallas_call(
        body, name="ag_weights",
        in_specs=[VMEM_SPEC, VMEM_SPEC, VMEM_SPEC], out_specs=[HBM_SPEC, HBM_SPEC, HBM_SPEC],
        out_shape=[jax.ShapeDtypeStruct((N_DEV, D_MODEL, SHARD_IN), BF16),
                   jax.ShapeDtypeStruct((N_DEV, SHARD_OUT, D_MODEL), BF16),
                   jax.ShapeDtypeStruct((N_DEV, 8, BLK), F32)],
        scratch_shapes=[pltpu.VMEM((D_MODEL, SHARD_IN), BF16), pltpu.VMEM((SHARD_OUT, D_MODEL), BF16),
                        pltpu.VMEM((8, BLK), F32), pltpu.SemaphoreType.DMA((N_DEV - 1, 3)),
                        pltpu.SemaphoreType.DMA((N_DEV - 1, 3)), pltpu.SemaphoreType.DMA((3,))],
        compiler_params=pltpu.CompilerParams(vmem_limit_bytes=VMEM_LIMIT),
    )(w_in, w_out, conv_w)


def _dw_in_rs(ht, dz, dw_out, small):
    seq = dz.shape[0]

    def body(cols_ref, ht_ref, dz_ref, dout_ref, sm_ref, own_ref, rin_ref, rout_ref, rsm_ref, to_sibling, landed,
             to_chip, zero_buf, d2d_send, d2d_recv, ici_send, ici_recv, side_send, side_recv, local_sems):
        del cols_ref
        step = pl.program_id(0)
        x, y, c = _coords()
        me = 4 * x + 2 * y + c
        sibling = (x, y, 1 - c)
        chips = [(1 - x, y), (x, 1 - y), (1 - x, 1 - y)]

        def d2d(n):
            return pltpu.make_async_remote_copy(src_ref=to_sibling.at[n], dst_ref=landed.at[n], send_sem=d2d_send.at[n],
                                                recv_sem=d2d_recv.at[n], device_id=sibling, device_id_type=MESH)

        def ici(n):
            return pltpu.make_async_remote_copy(src_ref=to_chip.at[n], dst_ref=rin_ref.at[n], send_sem=ici_send.at[n],
                                                recv_sem=ici_recv.at[n], device_id=(*chips[n], c), device_id_type=MESH)

        def side(k, mine):
            peer, peer_idx = _peer(k)
            src_slab, dst_slab = (peer_idx, me) if mine else (me, peer_idx)
            pairs = [(dout_ref.at[src_slab], rout_ref.at[dst_slab]), (sm_ref, rsm_ref.at[dst_slab])]
            return [pltpu.make_async_remote_copy(src_ref=src, dst_ref=dst, send_sem=side_send.at[k - 1, a],
                                                 recv_sem=side_recv.at[k - 1, a], device_id=peer, device_id_type=MESH)
                    for a, (src, dst) in enumerate(pairs)]

        local = [pltpu.make_async_copy(zero_buf, rout_ref.at[me], local_sems.at[0]),
                 pltpu.make_async_copy(sm_ref, rsm_ref.at[me], local_sems.at[1])]

        @pl.when(step == 0)
        def _():
            zero_buf[...] = jnp.zeros_like(zero_buf)
            for cp in local:
                cp.start()
            for k in range(1, N_DEV):
                for cp in side(k, mine=True):
                    cp.start()

        dw = jnp.dot(ht_ref[...], dz_ref[...], preferred_element_type=F32)
        for n in range(4):
            @pl.when(step == n)
            def _(n=n):
                to_sibling[n] = dw.astype(BF16)
                d2d(n).start()

        for n in range(3):
            @pl.when(step == 4 + n)
            def _(n=n):
                d2d(n).wait_recv()
                to_chip[n] = (dw + landed[n].astype(F32)).astype(BF16)
                ici(n).start()

        @pl.when(step == N_DEV - 1)
        def _():
            d2d(3).wait_recv()
            own_ref[...] = dw + landed[3].astype(F32)
            for n in range(3):
                ici(n).wait_recv()
            for k in range(1, N_DEV):
                for cp in side(k, mine=False):
                    cp.wait_recv()
            for n in range(4):
                d2d(n).wait_send()
            for n in range(3):
                ici(n).wait_send()
            for k in range(1, N_DEV):
                for cp in side(k, mine=True):
                    cp.wait_send()
            for cp in local:
                cp.wait()

    x, y, c = _coords()
    chip_order = [(1 - x, y), (x, 1 - y), (1 - x, 1 - y), (x, y)]
    cols = jnp.stack([4 * px + 2 * py + pc for pc in (1 - c, c) for px, py in chip_order]).astype(jnp.int32)
    slab = (D_MODEL, SHARD_IN)
    grid_spec = pltpu.PrefetchScalarGridSpec(
        num_scalar_prefetch=1, grid=(N_DEV,),
        in_specs=[pl.BlockSpec((D_MODEL, seq), lambda s, cols: (0, 0), pipeline_mode=pl.Buffered(1)),
                  pl.BlockSpec((seq, SHARD_IN), lambda s, cols: (0, cols[s])), HBM_SPEC, HBM_SPEC],
        out_specs=[pl.BlockSpec(slab, lambda s, cols: (0, 0)), HBM_SPEC, HBM_SPEC, HBM_SPEC],
        scratch_shapes=[pltpu.VMEM((4, *slab), BF16), pltpu.VMEM((4, *slab), BF16), pltpu.VMEM((3, *slab), BF16),
                        pltpu.VMEM((SHARD_OUT, D_MODEL), BF16),
                        pltpu.SemaphoreType.DMA((4,)), pltpu.SemaphoreType.DMA((4,)),
                        pltpu.SemaphoreType.DMA((3,)), pltpu.SemaphoreType.DMA((3,)),
                        pltpu.SemaphoreType.DMA((N_DEV - 1, 2)), pltpu.SemaphoreType.DMA((N_DEV - 1, 2)),
                        pltpu.SemaphoreType.DMA((2,))])
    return pl.pallas_call(
        body, name="dw_in_rs", grid_spec=grid_spec,
        out_shape=[jax.ShapeDtypeStruct(slab, F32),
                   jax.ShapeDtypeStruct((3, *slab), BF16),
                   jax.ShapeDtypeStruct((N_DEV, SHARD_OUT, D_MODEL), BF16),
                   jax.ShapeDtypeStruct((N_DEV, 8, D_MODEL), F32)],
        compiler_params=_params(1),
    )(cols, ht, dz, dw_out, small)


def _adamw_math(w, g, m, v):
    m = ADAM_B1 * m + (1.0 - ADAM_B1) * g
    v = ADAM_B2 * v + (1.0 - ADAM_B2) * (g * g)
    m_hat = m / (1.0 - ADAM_B1 ** ADAM_STEP)
    v_hat = v / (1.0 - ADAM_B2 ** ADAM_STEP)
    delta = -ADAM_LR * (m_hat / (jnp.sqrt(v_hat) + ADAM_EPS) + ADAM_WD * w)
    return delta, m, v


def _sum_slabs(ref, first=None):
    total = ref[0].astype(F32) if first is None else first + ref[0].astype(F32)
    for s in range(1, ref.shape[0]):
        total = total + ref[s].astype(F32)
    return total


def _adamw_slabs(parts, own, w, m, v, name, tr):
    rows, cols = w.shape
    tile = pl.BlockSpec((tr, cols), lambda i: (i, 0))

    def body(p_ref, *refs):
        own_ref = refs[0] if own is not None else None
        w_ref, m_ref, v_ref, g_ref, d_ref, nm_ref, nv_ref = refs[-7:]
        g = _sum_slabs(p_ref, None if own_ref is None else own_ref[...])
        g_ref[...] = g
        d_ref[...], nm_ref[...], nv_ref[...] = _adamw_math(w_ref[...], g, m_ref[...], v_ref[...])

    extra = [] if own is None else [own]
    return pl.pallas_call(
        body, name=name, grid=(rows // tr,),
        in_specs=[pl.BlockSpec((parts.shape[0], tr, cols), lambda i: (0, i, 0))] + [tile] * (len(extra) + 3),
        out_specs=[tile] * 4,
        out_shape=[jax.ShapeDtypeStruct((rows, cols), F32)] * 4,
        compiler_params=_params(1),
    )(parts, *extra, w, m, v)


def _sum_small(parts):
    def body(p_ref, out_ref):
        out_ref[...] = _sum_slabs(p_ref)

    return pl.pallas_call(body, name="sum_small", out_shape=jax.ShapeDtypeStruct(parts.shape[1:], F32))(parts)


def _adamw_whole(g, w, m, v, name):
    def body(g_ref, w_ref, m_ref, v_ref, d_ref, nm_ref, nv_ref):
        d_ref[...], nm_ref[...], nv_ref[...] = _adamw_math(w_ref[...], g_ref[...], m_ref[...], v_ref[...])

    return pl.pallas_call(body, name=name, out_shape=[jax.ShapeDtypeStruct(w.shape, F32)] * 3)(g, w, m, v)


def kernel(x, norm_pre_g, w_in, conv_w, w_out, norm_post_g, loss_target, m_norm_pre_g, m_w_in, m_conv_w, m_w_out,
           m_norm_post_g, v_norm_pre_g, v_w_in, v_conv_w, v_w_out, v_norm_post_g):
    n_conv = CONV_W // N_DEV
    w_in_g, w_out_g, conv_g = _ag_weights(w_in, w_out, conv_w)
    conv_full = conv_g[:, 0:3, 0:n_conv].transpose(1, 0, 2).reshape(3, CONV_W)
    grad_x, ht, dz, dw_out, dw_out_bf, small = _local_step(x[0], loss_target[0], norm_pre_g, norm_post_g, w_in_g,
                                                           w_out_g.reshape(D_MODEL, D_MODEL), conv_full)
    own_in, r_in, r_out, r_small = _dw_in_rs(ht, dz, dw_out_bf.reshape(N_DEV, SHARD_OUT, D_MODEL), small)
    me = 4 * lax.axis_index("x") + 2 * lax.axis_index("y") + lax.axis_index("c")
    own_out = lax.dynamic_index_in_dim(dw_out.reshape(N_DEV, SHARD_OUT, D_MODEL), me, keepdims=False)
    g_in, d_in, nm_in, nv_in = _adamw_slabs(r_in, own_in, w_in, m_w_in, v_w_in, "adamw_in", 256)
    g_out, d_out, nm_out, nv_out = _adamw_slabs(r_out, own_out, w_out, m_w_out, v_w_out, "adamw_out", SHARD_OUT)
    sums = _sum_small(r_small)
    g_pre, g_post, loss = sums[0], sums[1], sums[2, 0]
    g_conv = lax.dynamic_slice(sums[3:6, 0:CONV_W], (0, me * n_conv), (3, n_conv))
    vec = lambda a: a.reshape(1, D_MODEL)
    d_pre, nm_pre, nv_pre = _adamw_whole(vec(g_pre), vec(norm_pre_g), vec(m_norm_pre_g), vec(v_norm_pre_g), "adamw_pre")
    d_post, nm_post, nv_post = _adamw_whole(vec(g_post), vec(norm_post_g), vec(m_norm_post_g), vec(v_norm_post_g),
                                            "adamw_post")
    d_conv, nm_conv, nv_conv = _adamw_whole(g_conv, conv_w, m_conv_w, v_conv_w, "adamw_conv")
    flat = lambda a: a.reshape(D_MODEL)
    return (loss, grad_x[None], g_pre, g_in, g_conv, g_out, g_post,
            flat(d_pre), d_in, d_conv, d_out, flat(d_post),
            flat(nm_pre), nm_in, nm_conv, nm_out, flat(nm_post),
            flat(nv_pre), nv_in, nv_conv, nv_out, flat(nv_post))
```

```python
import functools

import jax
import jax.numpy as jnp
import numpy as np
from jax import lax
from jax.experimental import pallas as pl
from jax.experimental.pallas import tpu as pltpu

F32 = jnp.float32
BF16 = jnp.bfloat16

D_MODEL = 1024
HEAD_DIM = 64
ATTN_W = 768
CONV_W = 256
IN_W = 4096
REST_W = IN_W - 3 * ATTN_W
BLK = 128
N_DEV = 8
SHARD_IN = IN_W // N_DEV
SHARD_OUT = D_MODEL // N_DEV
DILATIONS = (1, 4, 16)
ROPE_THETA = 10000.0
NORM_EPS = 1e-6
NEG = -1e30

ADAM_LR = 0.001
ADAM_B1 = 0.9
ADAM_B2 = 0.999
ADAM_EPS = 1e-08
ADAM_WD = 0.01
ADAM_STEP = 10

VMEM_LIMIT = 56 * 1024 * 1024
MESH = pl.DeviceIdType.MESH


def _params(n_grid):
    return pltpu.CompilerParams(dimension_semantics=("arbitrary",) * n_grid, vmem_limit_bytes=VMEM_LIMIT)


def _resident(shape):
    zeros = (0,) * len(shape)
    return pl.BlockSpec(shape, lambda *_: zeros, pipeline_mode=pl.Buffered(1))


def _sigmoid(a):
    return 1.0 / (1.0 + jnp.exp(-a))


def _swap_halves(t, first_half):
    return jnp.where(first_half, pltpu.roll(t, BLK - 32, 1), pltpu.roll(t, 32, 1))


def _rope_tables(seq):
    half = HEAD_DIM // 2
    inv_freq = ROPE_THETA ** (-jnp.arange(half, dtype=F32) * 2.0 / HEAD_DIM)
    ang = jnp.arange(seq).astype(F32)[:, None] * inv_freq[None, :]
    cos, sin = jnp.cos(ang), jnp.sin(ang)
    return jnp.concatenate([cos] * 4, axis=1), jnp.concatenate([-sin, sin, -sin, sin], axis=1)


N_CHUNK = ATTN_W // BLK


def _lanes(r, c):
    return slice(r * ATTN_W + c * BLK, r * ATTN_W + (c + 1) * BLK)


def _to_residues(src, chunk0, dst_refs, tmp, rows, dtype):
    assert DILATIONS == (1, 4, 16)
    dst1, dst4, dst16 = dst_refs
    n4, n16 = rows // 4, rows // 16
    for c in range(N_CHUNK):
        dst1[:, _lanes(0, c)] = src[chunk0 + c].astype(dtype)
        for r1 in range(4):
            tmp[c, r1 * n4:(r1 + 1) * n4, :] = src[chunk0 + c, pl.ds(r1, n4, stride=4), :]
        for r1 in range(4):
            dst4[:, _lanes(r1, c)] = tmp[c, r1 * n4:(r1 + 1) * n4, :].astype(dtype)
            for r2 in range(4):
                dst16[:, _lanes(4 * r2 + r1, c)] = tmp[c, pl.ds(r1 * n4 + r2, n16, stride=4), :].astype(dtype)


def _from_residue(src_ref, dst, dil, rows, accumulate, tmp=None):
    n4, n16 = rows // 4, rows // 16

    def put(where, piece):
        if accumulate:
            dst[where] += piece
        else:
            dst[where] = piece

    for c in range(N_CHUNK):
        if dil == 1:
            put((c,), src_ref[:, _lanes(0, c)].astype(F32))
            continue
        for r1 in range(4):
            if dil == 4:
                piece = src_ref[:, _lanes(r1, c)].astype(F32)
            else:
                for r2 in range(4):
                    tmp[c, pl.ds(r1 * n4 + r2, n16, stride=4), :] = src_ref[:, _lanes(4 * r2 + r1, c)].astype(F32)
                piece = tmp[c, r1 * n4:(r1 + 1) * n4, :]
            put((c, pl.ds(r1, n4, stride=4), slice(None)), piece)


def _residue_spec(tm, dil):
    return pl.BlockSpec((tm // dil, dil * ATTN_W), lambda i: (i, 0))


def _residue_shape(seq, dil, dtype):
    return jax.ShapeDtypeStruct((seq // dil, dil * ATTN_W), dtype)


def _fwd_in(x, g_pre, w_in_g, cos_t, sin_t, tm=256):
    seq = x.shape[0]
    n_dil = len(DILATIONS)

    def body(x_ref, g_ref, w_ref, cos_ref, sin_ref, *rest):
        qkv_refs, (zr_ref, ht_ref, qkv_scr, tmp) = rest[:3 * n_dil], rest[3 * n_dil:]
        xv = x_ref[...]
        r = lax.rsqrt(jnp.mean(xv * xv, axis=-1, keepdims=True) + NORM_EPS)
        hf = (xv * r) * g_ref[...]
        ht_ref[...] = hf.T.astype(BF16)
        h = hf.astype(BF16)
        cos, sin = cos_ref[...], sin_ref[...]
        first_half = (lax.broadcasted_iota(jnp.int32, (tm, BLK), 1) & 32) == 0

        def rope(t):
            return t * cos + _swap_halves(t, first_half) * sin

        def project(j):
            return jnp.dot(h, w_ref[j], preferred_element_type=F32)

        def place(j, zj):
            for n in range(SHARD_IN // BLK):
                chunk, t = j * (SHARD_IN // BLK) + n, zj[:, n * BLK:(n + 1) * BLK]
                if chunk < N_CHUNK:
                    qkv_scr[chunk] = rope(t) * HEAD_DIM ** -0.5
                elif chunk < 2 * N_CHUNK:
                    qkv_scr[chunk] = rope(t)
                elif chunk < 3 * N_CHUNK:
                    qkv_scr[chunk] = t
                else:
                    zr_ref[:, (chunk - 3 * N_CHUNK) * BLK:(chunk - 3 * N_CHUNK + 1) * BLK] = t

        ahead = project(0)
        for j in range(N_DEV):
            zj = ahead
            if j + 1 < N_DEV:
                ahead = project(j + 1)
            place(j, zj)
            for a in range(3):
                if (a + 1) * N_CHUNK - 1 in range(j * (SHARD_IN // BLK), (j + 1) * (SHARD_IN // BLK)):
                    _to_residues(qkv_scr, a * N_CHUNK, [qkv_refs[3 * n + a] for n in range(n_dil)], tmp, tm, BF16)

    row = lambda w: pl.BlockSpec((tm, w), lambda i: (i, 0))
    outs = pl.pallas_call(
        body, name="fwd_in", grid=(seq // tm,),
        in_specs=[row(D_MODEL), _resident((1, D_MODEL)), _resident((N_DEV, D_MODEL, SHARD_IN)), row(BLK), row(BLK)],
        out_specs=[_residue_spec(tm, dil) for dil in DILATIONS for _ in range(3)]
        + [row(REST_W), pl.BlockSpec((D_MODEL, tm), lambda i: (0, i))],
        out_shape=[_residue_shape(seq, dil, BF16) for dil in DILATIONS for _ in range(3)]
        + [jax.ShapeDtypeStruct((seq, REST_W), F32), jax.ShapeDtypeStruct((D_MODEL, seq), BF16)],
        scratch_shapes=[pltpu.VMEM((3 * N_CHUNK, tm, BLK), F32), pltpu.VMEM((N_CHUNK, tm, BLK), F32)],
        compiler_params=_params(1),
    )(x, g_pre.reshape(1, D_MODEL), w_in_g, cos_t, sin_t)
    qkv = [tuple(outs[3 * n:3 * n + 3]) for n in range(n_dil)]
    return qkv, outs[3 * n_dil], outs[3 * n_dil + 1]


def _band_bias(first_block):
    kj = lax.broadcasted_iota(jnp.int32, (2 * BLK, BLK), 0)
    qi = lax.broadcasted_iota(jnp.int32, (2 * BLK, BLK), 1)
    valid = (kj >= qi) & (kj <= qi + BLK)
    bias = jnp.where(valid, 0.0, NEG).astype(BF16)
    bias_first = jnp.where(valid & (kj >= BLK), 0.0, NEG).astype(BF16)
    onehot = ((kj & (BLK - 1)) == qi).astype(F32).astype(BF16)
    return onehot, bias, jnp.where(first_block, bias_first, bias)


def _stack_heads(t, head0):
    del head0
    keep0 = (lax.broadcasted_iota(jnp.int32, t.shape, 1) < HEAD_DIM).astype(F32).astype(BF16)
    return jnp.concatenate([t * keep0, t * (1 - keep0)], axis=0)


def _unstack_heads(t2, head0):
    return jnp.where(head0, t2[:BLK], t2[BLK:])


def _rows_per_head(a, head0):
    b = pltpu.roll(a, HEAD_DIM, 1)
    rows = jnp.concatenate([jnp.where(head0, a, b), jnp.where(head0, b, a)], axis=0)
    return jnp.concatenate([rows, rows], axis=1)


BLOCKS_PER_STEP = 16


def _attn_specs(length, dil):
    n_blocks = length // BLK
    tb = min(BLOCKS_PER_STEP, n_blocks)
    nc = BLOCKS_PER_STEP // tb
    assert (dil * N_CHUNK) % nc == 0 and n_blocks % tb == 0
    tile = pl.BlockSpec((tb * BLK, nc * BLK), lambda c, t: (t, c))
    prev = pl.BlockSpec((BLK, nc * BLK), lambda c, t: (jnp.maximum(t * tb - 1, 0), c))
    grid = (dil * N_CHUNK // nc, n_blocks // tb)
    return tb, nc, tile, prev, grid


def _load_keys(cat, prev_ref, cur_ref):
    cat[0:BLK] = prev_ref[...]
    cat[BLK:] = cur_ref[...]


def _attn_fwd(q, k, v, dil):
    length = q.shape[0]
    tb, nc, tile, prev, grid = _attn_specs(length, dil)

    def body(q_ref, kc_ref, kp_ref, vc_ref, vp_ref, o_ref, lse_ref, kcat, vcat):
        _load_keys(kcat, kp_ref, kc_ref)
        _load_keys(vcat, vp_ref, vc_ref)
        head0 = lax.broadcasted_iota(jnp.int32, (BLK, BLK), 1) < HEAD_DIM
        onehot, bias, bias_start = _band_bias(pl.program_id(1) == 0)
        ones = jnp.ones((2 * BLK, BLK), BF16)
        def scores(c, j):
            rows, cols = slice(j * BLK, (j + 1) * BLK), slice(c * BLK, (c + 1) * BLK)
            q2 = jnp.concatenate([_stack_heads(q_ref[rows, cols], head0), onehot], axis=1)
            kk = jnp.concatenate([kcat[j * BLK:(j + 2) * BLK, cols], bias_start if j == 0 else bias], axis=1)
            return (lax.dot_general(q2, kk, (((1,), (1,)), ((), ())), preferred_element_type=F32),)

        def probabilities(c, j, s):
            m = jnp.max(s, axis=1, keepdims=True)
            return m, jnp.exp(s - m).astype(BF16)

        def outputs(c, j, m, p):
            rows, cols = slice(j * BLK, (j + 1) * BLK), slice(c * BLK, (c + 1) * BLK)
            vv = jnp.concatenate([vcat[j * BLK:(j + 2) * BLK, cols], ones], axis=1)
            pv = jnp.dot(p, vv, preferred_element_type=F32)
            den = pv[:, BLK:]
            o_ref[rows, cols] = _unstack_heads(pv[:, :BLK] / den, head0).astype(BF16)
            lse_ref[rows, cols] = _unstack_heads(m + jnp.log(den), head0)

        units = [(c, j) for c in range(nc) for j in range(tb)]
        stage1, stage2 = {}, {}
        for n in range(len(units) + 2):
            if n < len(units):
                stage1[n] = scores(*units[n])
            if 0 <= n - 1 < len(units):
                stage2[n - 1] = probabilities(*units[n - 1], *stage1.pop(n - 1))
            if 0 <= n - 2 < len(units):
                outputs(*units[n - 2], *stage2.pop(n - 2))

    return pl.pallas_call(
        body, name=f"attn_fwd_d{dil}", grid=grid,
        in_specs=[tile, tile, prev, tile, prev], out_specs=[tile, tile],
        out_shape=[jax.ShapeDtypeStruct(q.shape, BF16), jax.ShapeDtypeStruct(q.shape, F32)],
        scratch_shapes=[pltpu.VMEM(((tb + 1) * BLK, nc * BLK), BF16)] * 2,
        compiler_params=_params(2),
    )(q, k, k, v, v)


def _attn_bwd(q, k, v, do, lse, delta, dil):
    length = q.shape[0]
    tb, nc, tile, prev, grid = _attn_specs(length, dil)
    whole = pl.BlockSpec((length, nc * BLK), lambda c, t: (0, c))

    def body(q_ref, do_ref, lse_ref, dl_ref, kc_ref, kp_ref, vc_ref, vp_ref, dq_ref, dk_ref, dv_ref, kcat, vcat):
        t = pl.program_id(1)
        _load_keys(kcat, kp_ref, kc_ref)
        _load_keys(vcat, vp_ref, vc_ref)
        head0 = lax.broadcasted_iota(jnp.int32, (BLK, BLK), 1) < HEAD_DIM
        onehot, bias, bias_start = _band_bias(t == 0)

        def scores(c, j):
            rows, cols = slice(j * BLK, (j + 1) * BLK), slice(c * BLK, (c + 1) * BLK)
            q2 = _stack_heads(q_ref[rows, cols], head0)
            do2 = _stack_heads(do_ref[rows, cols], head0)
            kk = kcat[j * BLK:(j + 2) * BLK, cols]
            s = lax.dot_general(jnp.concatenate([q2, onehot], axis=1),
                                jnp.concatenate([kk, bias_start if j == 0 else bias], axis=1),
                                (((1,), (1,)), ((), ())), preferred_element_type=F32)
            dp = lax.dot_general(do2, vcat[j * BLK:(j + 2) * BLK, cols], (((1,), (1,)), ((), ())),
                                 preferred_element_type=F32)
            return q2, do2, kk, s, dp

        def probabilities(c, j, q2, do2, kk, s, dp):
            rows, cols = slice(j * BLK, (j + 1) * BLK), slice(c * BLK, (c + 1) * BLK)
            p = jnp.exp(s - _rows_per_head(lse_ref[rows, cols], head0))
            ds = (p * (dp - _rows_per_head(dl_ref[rows, cols].astype(F32), head0))).astype(BF16)
            return q2, do2, kk, p.astype(BF16), ds

        def gradients(c, j, q2, do2, kk, p, ds):
            rows, cols = slice(j * BLK, (j + 1) * BLK), slice(c * BLK, (c + 1) * BLK)
            dq2 = jnp.dot(ds, kk, preferred_element_type=F32)
            dq_ref[rows, cols] = (_unstack_heads(dq2, head0) * HEAD_DIM ** -0.5).astype(BF16)
            dk2 = lax.dot_general(ds, q2, (((0,), (0,)), ((), ())), preferred_element_type=F32)
            dv2 = lax.dot_general(p, do2, (((0,), (0,)), ((), ())), preferred_element_type=F32)
            own = pl.ds(pl.multiple_of((t * tb + j) * BLK, BLK), BLK)
            dk_ref[own, cols] = dk2[BLK:].astype(BF16)
            dv_ref[own, cols] = dv2[BLK:].astype(BF16)

            def add_to_previous():
                before = pl.ds(pl.multiple_of((t * tb + j - 1) * BLK, BLK), BLK)
                dk_ref[before, cols] = (dk_ref[before, cols].astype(F32) + dk2[:BLK]).astype(BF16)
                dv_ref[before, cols] = (dv_ref[before, cols].astype(F32) + dv2[:BLK]).astype(BF16)

            if j == 0:
                pl.when(t > 0)(add_to_previous)
            else:
                add_to_previous()

        units = [(c, j) for c in range(nc) for j in range(tb)]
        stage1 = {0: scores(*units[0])}
        for n in range(len(units)):
            stage2 = probabilities(*units[n], *stage1.pop(n))
            if n + 1 < len(units):
                stage1[n + 1] = scores(*units[n + 1])
            gradients(*units[n], *stage2)

    return pl.pallas_call(
        body, name=f"attn_bwd_d{dil}", grid=grid,
        in_specs=[tile, tile, tile, tile, tile, prev, tile, prev], out_specs=[tile, whole, whole],
        out_shape=[jax.ShapeDtypeStruct(q.shape, BF16)] * 3,
        scratch_shapes=[pltpu.VMEM(((tb + 1) * BLK, nc * BLK), BF16)] * 2,
        compiler_params=_params(2),
    )(q, do, lse, delta, k, k, v, v)


def _conv_taps(u, before8, tm):
    row = lax.broadcasted_iota(jnp.int32, u.shape, 0)
    last, last2 = before8[7:8, :], before8[6:7, :]
    u1 = jnp.where(row == 0, last, pltpu.roll(u, 1, 0))
    u2 = jnp.where(row == 0, last2, jnp.where(row == 1, last, pltpu.roll(u, 2, 0)))
    return u1, u2


def _attn_combine(o_parts, lse_parts, zr, conv_w, tm=256):
    seq = zr.shape[0]
    a0, h0, b0, c0, g0 = 0, ATTN_W, ATTN_W + CONV_W, ATTN_W + 2 * CONV_W, ATTN_W + 3 * CONV_W

    def body(o1, o2, o3, l1, l2, l3, zr_ref, zp_ref, w_ref, mixed_ref, o_ref, lse1, lse2, lse3, *scr):
        i = pl.program_id(0)
        for src, dst, dil in zip((o2, o3, l2, l3), scr[:4], DILATIONS[1:] * 2):
            _from_residue(src, dst, dil, tm, accumulate=False, tmp=scr[5])
        for c in range(N_CHUNK):
            cols = slice(c * BLK, (c + 1) * BLK)
            la, lb, lc = l1[:, cols], scr[2][c], scr[3][c]
            top = jnp.maximum(jnp.maximum(la, lb), lc)
            ea, eb, ec = jnp.exp(la - top), jnp.exp(lb - top), jnp.exp(lc - top)
            den = ea + eb + ec
            o = (ea / den) * o1[:, cols].astype(F32) + (eb / den) * scr[0][c] + (ec / den) * scr[1][c]
            o_ref[:, cols] = o
            scr[4][c] = top + jnp.log(den)
            ga = zr_ref[:, cols]
            mixed_ref[:, cols] = (o * (ga * _sigmoid(ga))).astype(BF16)
        _to_residues(scr[4], 0, (lse1, lse2, lse3), scr[5], tm, F32)
        u = zr_ref[:, c0:g0] * zr_ref[:, h0:b0]
        before = jnp.where(i > 0, zp_ref[:, c0:g0] * zp_ref[:, h0:b0], 0.0)
        u1, u2 = _conv_taps(u, before, tm)
        y = u2 * w_ref[0:1, :] + u1 * w_ref[1:2, :] + u * w_ref[2:3, :]
        gc = zr_ref[:, g0:]
        mixed_ref[:, ATTN_W:] = ((zr_ref[:, b0:c0] * y) * (gc * _sigmoid(gc))).astype(BF16)

    row = lambda w: pl.BlockSpec((tm, w), lambda i: (i, 0))
    before8 = pl.BlockSpec((8, REST_W), lambda i: (jnp.maximum(i * (tm // 8) - 1, 0), 0))
    views = [_residue_spec(tm, dil) for dil in DILATIONS]
    outs = pl.pallas_call(
        body, name="attn_combine", grid=(seq // tm,),
        in_specs=views * 2 + [row(REST_W), before8, _resident((3, CONV_W))],
        out_specs=[row(D_MODEL), row(ATTN_W)] + views,
        out_shape=[jax.ShapeDtypeStruct((seq, D_MODEL), BF16), jax.ShapeDtypeStruct((seq, ATTN_W), F32)]
        + [_residue_shape(seq, dil, F32) for dil in DILATIONS],
        scratch_shapes=[pltpu.VMEM((N_CHUNK, tm, BLK), F32)] * 6,
        compiler_params=_params(1),
    )(*o_parts, *lse_parts, zr, zr, conv_w)
    return outs[0], outs[1], outs[2:]


def _out_loss_bwd(mixed, w_out_g, x, target, g_post, tm=512):
    seq = x.shape[0]

    def body(mx_ref, w_ref, x_ref, t_ref, g_ref, dout_ref, dmx_ref, dw_ref, dwb_ref, st_ref):
        i = pl.program_id(0)
        mx = mx_ref[...]
        y = jnp.dot(mx, w_ref[...], preferred_element_type=F32)
        r = lax.rsqrt(jnp.mean(y * y, axis=-1, keepdims=True) + NORM_EPS)
        yhat = y * r
        g = g_ref[...]
        err = (x_ref[...] + yhat * g) - t_ref[...]
        dn = err * (1.0 / D_MODEL)
        dout_ref[...] = dn
        tg = dn * g
        dy = (r * (tg - yhat * jnp.mean(tg * yhat, axis=-1, keepdims=True))).astype(BF16)
        dmx_ref[...] = lax.dot_general(dy, w_ref[...], (((1,), (1,)), ((), ())), preferred_element_type=F32)
        dw = lax.dot_general(mx, dy, (((0,), (0,)), ((), ())), preferred_element_type=F32)
        gsum = jnp.sum(dn * yhat, axis=0, keepdims=True)
        lsum = jnp.broadcast_to(0.5 / D_MODEL * jnp.sum(err * err), (1, D_MODEL))

        @pl.when(i == 0)
        def _():
            dw_ref[...] = dw
            st_ref[...] = jnp.zeros_like(st_ref)
            st_ref[0:1, :] = gsum
            st_ref[1:2, :] = lsum

        @pl.when(i > 0)
        def _():
            dw_ref[...] += dw
            st_ref[0:1, :] += gsum
            st_ref[1:2, :] += lsum

        @pl.when(i == seq // tm - 1)
        def _():
            dwb_ref[...] = dw_ref[...].astype(BF16)

    row = lambda w: pl.BlockSpec((tm, w), lambda i: (i, 0))
    whole = pl.BlockSpec((D_MODEL, D_MODEL), lambda i: (0, 0))
    return pl.pallas_call(
        body, name="out_loss_bwd", grid=(seq // tm,),
        in_specs=[row(D_MODEL), _resident((D_MODEL, D_MODEL)), row(D_MODEL), row(D_MODEL), _resident((1, D_MODEL))],
        out_specs=[row(D_MODEL), row(D_MODEL), whole, whole, pl.BlockSpec((8, D_MODEL), lambda i: (0, 0))],
        out_shape=[jax.ShapeDtypeStruct((seq, D_MODEL), F32), jax.ShapeDtypeStruct((seq, D_MODEL), F32),
                   jax.ShapeDtypeStruct((D_MODEL, D_MODEL), F32), jax.ShapeDtypeStruct((D_MODEL, D_MODEL), BF16),
                   jax.ShapeDtypeStruct((8, D_MODEL), F32)],
        compiler_params=_params(1),
    )(mixed, w_out_g, x, target, g_post.reshape(1, D_MODEL))


def _head_sum(prod, same_head):
    hi = prod.astype(BF16)
    lo = (prod - hi.astype(F32)).astype(BF16)
    return (jnp.dot(hi, same_head, preferred_element_type=F32) + jnp.dot(lo, same_head, preferred_element_type=F32))


def _gate_bwd(dmixed, zr, o, conv_w, tm=256):
    seq = zr.shape[0]
    n_tiles = seq // tm
    n_dil = len(DILATIONS)
    a0, h0, b0, c0, g0 = 0, ATTN_W, ATTN_W + CONV_W, ATTN_W + 2 * CONV_W, ATTN_W + 3 * CONV_W

    def body(dm_ref, dmn_ref, zr_ref, zp_ref, zn_ref, o_ref, w_ref, *rest):
        do_refs, dl_refs = rest[:n_dil], rest[n_dil:2 * n_dil]
        dz_ref, dw_ref, do_scr, dl_scr, tmp = rest[2 * n_dil:]
        i = pl.program_id(0)
        ga = zr_ref[:, a0:h0]
        sg = _sigmoid(ga)
        dattn = dm_ref[:, 0:ATTN_W]
        ov = o_ref[...]
        do = dattn * (ga * sg)
        dz_ref[:, a0:h0] = (dattn * ov * (sg * (1.0 + ga * (1.0 - sg)))).astype(BF16)
        li = lax.broadcasted_iota(jnp.int32, (BLK, BLK), 0) // HEAD_DIM
        lj = lax.broadcasted_iota(jnp.int32, (BLK, BLK), 1) // HEAD_DIM
        same_head = (li == lj).astype(BF16)
        prod = do * ov
        for c in range(N_CHUNK):
            cols = slice(c * BLK, (c + 1) * BLK)
            do_scr[c] = do[:, cols]
            dl_scr[c] = _head_sum(prod[:, cols], same_head)
        _to_residues(do_scr, 0, do_refs, tmp, tm, BF16)
        _to_residues(dl_scr, 0, dl_refs, tmp, tm, BF16)

        ch, cb, cc, gc = zr_ref[:, h0:b0], zr_ref[:, b0:c0], zr_ref[:, c0:g0], zr_ref[:, g0:]
        u = cc * ch
        before = jnp.where(i > 0, zp_ref[:, c0:g0] * zp_ref[:, h0:b0], 0.0)
        u1, u2 = _conv_taps(u, before, tm)
        w0, w1, w2 = w_ref[0:1, :], w_ref[1:2, :], w_ref[2:3, :]
        y = u2 * w0 + u1 * w1 + u * w2
        sc = _sigmoid(gc)
        silu_c = gc * sc
        dconv = dm_ref[:, ATTN_W:]
        dz_ref[:, b0:c0] = (dconv * y * silu_c).astype(BF16)
        dz_ref[:, g0:] = (dconv * (cb * y) * (sc * (1.0 + gc * (1.0 - sc)))).astype(BF16)
        dy = dconv * cb * silu_c
        gn = zn_ref[:, g0:]
        after = jnp.where(i < n_tiles - 1, dmn_ref[:, ATTN_W:] * zn_ref[:, b0:c0] * (gn * _sigmoid(gn)), 0.0)
        row = lax.broadcasted_iota(jnp.int32, dy.shape, 0)
        nxt, nxt2 = after[0:1, :], after[1:2, :]
        dy1 = jnp.where(row == tm - 1, nxt, pltpu.roll(dy, tm - 1, 0))
        dy2 = jnp.where(row == tm - 1, nxt2, jnp.where(row == tm - 2, nxt, pltpu.roll(dy, tm - 2, 0)))
        du = dy * w2 + dy1 * w1 + dy2 * w0
        dz_ref[:, c0:g0] = (du * ch).astype(BF16)
        dz_ref[:, h0:b0] = (du * cc).astype(BF16)
        dws = [jnp.sum(dy * u2, axis=0, keepdims=True), jnp.sum(dy * u1, axis=0, keepdims=True),
               jnp.sum(dy * u, axis=0, keepdims=True)]

        @pl.when(i == 0)
        def _():
            dw_ref[...] = jnp.zeros_like(dw_ref)

        for n, part in enumerate(dws):
            dw_ref[n:n + 1, :] += part

    row_spec = lambda w: pl.BlockSpec((tm, w), lambda i: (i, 0))
    before8 = pl.BlockSpec((8, REST_W), lambda i: (jnp.maximum(i * (tm // 8) - 1, 0), 0))
    after8 = lambda w: pl.BlockSpec((8, w), lambda i: (jnp.minimum((i + 1) * (tm // 8), seq // 8 - 1), 0))
    views = [_residue_spec(tm, dil) for dil in DILATIONS]
    outs = pl.pallas_call(
        body, name="gate_bwd", grid=(n_tiles,),
        in_specs=[row_spec(D_MODEL), after8(D_MODEL), row_spec(REST_W), before8, after8(REST_W), row_spec(ATTN_W),
                  _resident((3, CONV_W))],
        out_specs=views * 2 + [row_spec(REST_W), pl.BlockSpec((8, CONV_W), lambda i: (0, 0))],
        out_shape=[_residue_shape(seq, dil, BF16) for dil in DILATIONS] * 2
        + [jax.ShapeDtypeStruct((seq, REST_W), BF16), jax.ShapeDtypeStruct((8, CONV_W), F32)],
        scratch_shapes=[pltpu.VMEM((N_CHUNK, tm, BLK), F32)] * 3,
        compiler_params=_params(1),
    )(dmixed, dmixed, zr, zr, zr, o, conv_w)
    return outs[:n_dil], outs[n_dil:2 * n_dil], outs[2 * n_dil], outs[2 * n_dil + 1]


def _in_bwd(dqs, dks, dvs, dzr, cos_t, sin_t, x, d_out, g_pre, w_in_g, tm=256):
    seq = x.shape[0]

    def body(q1, q2, q3, k1, k2, k3, v1, v2, v3, dzr_ref, cos_ref, sin_ref, x_ref, dout_ref, g_ref, w_ref,
             dz_ref, gx_ref, st_ref, dq_scr, dk_scr, dv_scr, tmp):
        i = pl.program_id(0)
        cos, sin = cos_ref[...], sin_ref[...]
        first_half = (lax.broadcasted_iota(jnp.int32, (tm, BLK), 1) & 32) == 0
        streams = [((q1, q2, q3), dq_scr), ((k1, k2, k3), dk_scr), ((v1, v2, v3), dv_scr)]
        per_slab = SHARD_IN // BLK

        def unrope(t):
            return t * cos - _swap_halves(t, first_half) * sin

        def assemble(j):
            for chunk in range(j * per_slab, (j + 1) * per_slab):
                a, c = divmod(chunk, N_CHUNK)
                if a < 3 and c == 0:
                    parts, total = streams[a]
                    for n, dil in enumerate(DILATIONS):
                        _from_residue(parts[n], total, dil, tm, accumulate=n > 0, tmp=tmp)
                if a < 2:
                    val = unrope(streams[a][1][c]).astype(BF16)
                elif a == 2:
                    val = dv_scr[c].astype(BF16)
                else:
                    val = dzr_ref[:, (chunk - 3 * N_CHUNK) * BLK:(chunk - 3 * N_CHUNK + 1) * BLK]
                dz_ref[:, chunk * BLK:(chunk + 1) * BLK] = val
            return dz_ref[:, j * SHARD_IN:(j + 1) * SHARD_IN]

        ahead = assemble(0)
        dh = None
        for j in range(N_DEV):
            slab = ahead
            if j + 1 < N_DEV:
                ahead = assemble(j + 1)
            part = lax.dot_general(slab, w_ref[j], (((1,), (1,)), ((), ())), preferred_element_type=F32)
            dh = part if dh is None else dh + part
        xv = x_ref[...]
        r = lax.rsqrt(jnp.mean(xv * xv, axis=-1, keepdims=True) + NORM_EPS)
        xhat = xv * r
        tg = dh * g_ref[...]
        gx_ref[...] = dout_ref[...] + r * (tg - xhat * jnp.mean(tg * xhat, axis=-1, keepdims=True))
        gsum = jnp.sum(dh * xhat, axis=0, keepdims=True)

        @pl.when(i == 0)
        def _():
            st_ref[...] = jnp.zeros_like(st_ref)

        st_ref[0:1, :] += gsum

    row = lambda w: pl.BlockSpec((tm, w), lambda i: (i, 0))
    return pl.pallas_call(
        body, name="in_bwd", grid=(seq // tm,),
        in_specs=[_residue_spec(tm, dil) for dil in DILATIONS] * 3
        + [row(REST_W), row(BLK), row(BLK), row(D_MODEL), row(D_MODEL), _resident((1, D_MODEL)),
           _resident((N_DEV, D_MODEL, SHARD_IN))],
        out_specs=[row(IN_W), row(D_MODEL), pl.BlockSpec((8, D_MODEL), lambda i: (0, 0))],
        out_shape=[jax.ShapeDtypeStruct((seq, IN_W), BF16), jax.ShapeDtypeStruct((seq, D_MODEL), F32),
                   jax.ShapeDtypeStruct((8, D_MODEL), F32)],
        scratch_shapes=[pltpu.VMEM((N_CHUNK, tm, BLK), F32)] * 4,
        compiler_params=_params(1),
    )(*dqs, *dks, *dvs, dzr, cos_t, sin_t, x, d_out, g_pre.reshape(1, D_MODEL), w_in_g)


def _local_step(x, target, g_pre, g_post, w_in_g, w_out_g, conv_w):
    seq = x.shape[0]
    cos_t, sin_t = _rope_tables(seq)
    qkv, zr, ht = _fwd_in(x, g_pre, w_in_g, cos_t, sin_t)
    parts = [_attn_fwd(*qkv[n], dil) for n, dil in enumerate(DILATIONS)]
    mixed, o, lse = _attn_combine([p[0] for p in parts], [p[1] for p in parts], zr, conv_w)
    d_out, dmixed, dw_out, dw_out_bf, st_post = _out_loss_bwd(mixed, w_out_g, x, target, g_post)
    do, delta, dzr, dconv = _gate_bwd(dmixed, zr, o, conv_w)
    grads = [_attn_bwd(*qkv[n], do[n], lse[n], delta[n], dil) for n, dil in enumerate(DILATIONS)]
    dz, grad_x, st_pre = _in_bwd([g[0] for g in grads], [g[1] for g in grads], [g[2] for g in grads], dzr,
                                 cos_t, sin_t, x, d_out, g_pre, w_in_g)
    conv_rows = jnp.pad(dconv[0:3], ((0, 0), (0, D_MODEL - CONV_W)))
    small = jnp.concatenate([st_pre[0:1], st_post[0:2], conv_rows, jnp.zeros((2, D_MODEL), F32)], axis=0)
    return grad_x, ht, dz, dw_out, dw_out_bf, small


def _coords():
    return lax.axis_index("x"), lax.axis_index("y"), lax.axis_index("c")


def _peer(k):
    x, y, c = _coords()
    px = 1 - x if k & 4 else x
    py = 1 - y if k & 2 else y
    pc = 1 - c if k & 1 else c
    return (px, py, pc), 4 * px + 2 * py + pc


HBM_SPEC = pl.BlockSpec(memory_space=pltpu.HBM)
VMEM_SPEC = pl.BlockSpec(memory_space=pltpu.VMEM)


def _ag_weights(w_in, w_out, conv_w):
    def body(win_ref, wout_ref, cw_ref, gin_ref, gout_ref, gcw_ref, win_bf, wout_bf, cw_pad, send_sems, recv_sems,
             local_sems):
        x, y, c = _coords()
        me, sibling = (x, y, c), (x, y, 1 - c)
        chips = [(1 - x, y), (x, 1 - y), (1 - x, 1 - y)]
        slab = lambda px, py, pc: 4 * px + 2 * py + pc
        win_bf[...] = win_ref[...].astype(BF16)
        wout_bf[...] = wout_ref[...].astype(BF16)
        cw_pad[...] = jnp.zeros_like(cw_pad)
        cw_pad[0:3, 0:CONV_W // N_DEV] = cw_ref[...]
        mine = [win_bf, wout_bf, cw_pad]
        gathered = [gin_ref, gout_ref, gcw_ref]

        def copies(k, block, to, own=False):
            return [pltpu.make_async_remote_copy(src_ref=mine[a] if own else gathered[a].at[slab(*block)],
                                                 dst_ref=gathered[a].at[slab(*block)], send_sem=send_sems.at[k, a],
                                                 recv_sem=recv_sems.at[k, a], device_id=to, device_id_type=MESH)
                    for a in range(3)]

        local = [pltpu.make_async_copy(mine[a], gathered[a].at[slab(*me)], local_sems.at[a]) for a in range(3)]
        for cp in local:
            cp.start()
        first = copies(0, me, sibling, own=True)
        for j, chip in enumerate(chips):
            first += copies(1 + j, me, (*chip, c), own=True)
        for cp in first:
            cp.start()
        passed = []
        for j, chip in enumerate(chips):
            for cp in copies(1 + j, (*chip, c), me):
                cp.wait_recv()
            onward = copies(4 + j, (*chip, c), sibling)
            for cp in onward:
                cp.start()
            passed += onward
        for cp in copies(0, sibling, me):
            cp.wait_recv()
        for j, chip in enumerate(chips):
            for cp in copies(4 + j, (*chip, 1 - c), me):
                cp.wait_recv()
        for cp in first + passed:
            cp.wait_send()
        for cp in local:
            cp.wait()

    return pl.pallas_call(
        body, name="ag_weights",
        in_specs=[VMEM_SPEC, VMEM_SPEC, VMEM_SPEC], out_specs=[HBM_SPEC, HBM_SPEC, HBM_SPEC],
        out_shape=[jax.ShapeDtypeStruct((N_DEV, D_MODEL, SHARD_IN), BF16),
                   jax.ShapeDtypeStruct((N_DEV, SHARD_OUT, D_MODEL), BF16),
                   jax.ShapeDtypeStruct((N_DEV, 8, BLK), F32)],
        scratch_shapes=[pltpu.VMEM((D_MODEL, SHARD_IN), BF16), pltpu.VMEM((SHARD_OUT, D_MODEL), BF16),
                        pltpu.VMEM((8, BLK), F32), pltpu.SemaphoreType.DMA((N_DEV - 1, 3)),
                        pltpu.SemaphoreType.DMA((N_DEV - 1, 3)), pltpu.SemaphoreType.DMA((3,))],
        compiler_params=pltpu.CompilerParams(vmem_limit_bytes=VMEM_LIMIT),
    )(w_in, w_out, conv_w)


def _dw_in_rs(ht, dz, dw_out, small):
    seq = dz.shape[0]

    def body(cols_ref, ht_ref, dz_ref, dout_ref, sm_ref, own_ref, rin_ref, rout_ref, rsm_ref, to_sibling, landed,
             to_chip, zero_buf, d2d_send, d2d_recv, ici_send, ici_recv, side_send, side_recv, local_sems):
        del cols_ref
        step = pl.program_id(0)
        x, y, c = _coords()
        me = 4 * x + 2 * y + c
        sibling = (x, y, 1 - c)
        chips = [(1 - x, y), (x, 1 - y), (1 - x, 1 - y)]

        def d2d(n):
            return pltpu.make_async_remote_copy(src_ref=to_sibling.at[n], dst_ref=landed.at[n], send_sem=d2d_send.at[n],
                                                recv_sem=d2d_recv.at[n], device_id=sibling, device_id_type=MESH)

        def ici(n):
            return pltpu.make_async_remote_copy(src_ref=to_chip.at[n], dst_ref=rin_ref.at[n], send_sem=ici_send.at[n],
                                                recv_sem=ici_recv.at[n], device_id=(*chips[n], c), device_id_type=MESH)

        def side(k, mine):
            peer, peer_idx = _peer(k)
            src_slab, dst_slab = (peer_idx, me) if mine else (me, peer_idx)
            pairs = [(dout_ref.at[src_slab], rout_ref.at[dst_slab]), (sm_ref, rsm_ref.at[dst_slab])]
            return [pltpu.make_async_remote_copy(src_ref=src, dst_ref=dst, send_sem=side_send.at[k - 1, a],
                                                 recv_sem=side_recv.at[k - 1, a], device_id=peer, device_id_type=MESH)
                    for a, (src, dst) in enumerate(pairs)]

        local = [pltpu.make_async_copy(zero_buf, rout_ref.at[me], local_sems.at[0]),
                 pltpu.make_async_copy(sm_ref, rsm_ref.at[me], local_sems.at[1])]

        @pl.when(step == 0)
        def _():
            zero_buf[...] = jnp.zeros_like(zero_buf)
            for cp in local:
                cp.start()
            for k in range(1, N_DEV):
                for cp in side(k, mine=True):
                    cp.start()

        dw = jnp.dot(ht_ref[...], dz_ref[...], preferred_element_type=F32)
        for n in range(4):
            @pl.when(step == n)
            def _(n=n):
                to_sibling[n] = dw.astype(BF16)
                d2d(n).start()

        for n in range(3):
            @pl.when(step == 4 + n)
            def _(n=n):
                d2d(n).wait_recv()
                to_chip[n] = (dw + landed[n].astype(F32)).astype(BF16)
                ici(n).start()

        @pl.when(step == N_DEV - 1)
        def _():
            d2d(3).wait_recv()
            own_ref[...] = dw + landed[3].astype(F32)
            for n in range(3):
                ici(n).wait_recv()
            for k in range(1, N_DEV):
                for cp in side(k, mine=False):
                    cp.wait_recv()
            for n in range(4):
                d2d(n).wait_send()
            for n in range(3):
                ici(n).wait_send()
            for k in range(1, N_DEV):
                for cp in side(k, mine=True):
                    cp.wait_send()
            for cp in local:
                cp.wait()

    x, y, c = _coords()
    chip_order = [(1 - x, y), (x, 1 - y), (1 - x, 1 - y), (x, y)]
    cols = jnp.stack([4 * px + 2 * py + pc for pc in (1 - c, c) for px, py in chip_order]).astype(jnp.int32)
    slab = (D_MODEL, SHARD_IN)
    grid_spec = pltpu.PrefetchScalarGridSpec(
        num_scalar_prefetch=1, grid=(N_DEV,),
        in_specs=[pl.BlockSpec((D_MODEL, seq), lambda s, cols: (0, 0), pipeline_mode=pl.Buffered(1)),
                  pl.BlockSpec((seq, SHARD_IN), lambda s, cols: (0, cols[s])), HBM_SPEC, HBM_SPEC],
        out_specs=[pl.BlockSpec(slab, lambda s, cols: (0, 0)), HBM_SPEC, HBM_SPEC, HBM_SPEC],
        scratch_shapes=[pltpu.VMEM((4, *slab), BF16), pltpu.VMEM((4, *slab), BF16), pltpu.VMEM((3, *slab), BF16),
                        pltpu.VMEM((SHARD_OUT, D_MODEL), BF16),
                        pltpu.SemaphoreType.DMA((4,)), pltpu.SemaphoreType.DMA((4,)),
                        pltpu.SemaphoreType.DMA((3,)), pltpu.SemaphoreType.DMA((3,)),
                        pltpu.SemaphoreType.DMA((N_DEV - 1, 2)), pltpu.SemaphoreType.DMA((N_DEV - 1, 2)),
                        pltpu.SemaphoreType.DMA((2,))])
    return pl.pallas_call(
        body, name="dw_in_rs", grid_spec=grid_spec,
        out_shape=[jax.ShapeDtypeStruct(slab, F32),
                   jax.ShapeDtypeStruct((3, *slab), BF16),
                   jax.ShapeDtypeStruct((N_DEV, SHARD_OUT, D_MODEL), BF16),
                   jax.ShapeDtypeStruct((N_DEV, 8, D_MODEL), F32)],
        compiler_params=_params(1),
    )(cols, ht, dz, dw_out, small)


def _adamw_math(w, g, m, v):
    m = ADAM_B1 * m + (1.0 - ADAM_B1) * g
    v = ADAM_B2 * v + (1.0 - ADAM_B2) * (g * g)
    m_hat = m / (1.0 - ADAM_B1 ** ADAM_STEP)
    v_hat = v / (1.0 - ADAM_B2 ** ADAM_STEP)
    delta = -ADAM_LR * (m_hat / (jnp.sqrt(v_hat) + ADAM_EPS) + ADAM_WD * w)
    return delta, m, v


def _sum_slabs(ref, first=None):
    total = ref[0].astype(F32) if first is None else first + ref[0].astype(F32)
    for s in range(1, ref.shape[0]):
        total = total + ref[s].astype(F32)
    return total


def _adamw_slabs(parts, own, w, m, v, name, tr):
    rows, cols = w.shape
    tile = pl.BlockSpec((tr, cols), lambda i: (i, 0))

    def body(p_ref, *refs):
        own_ref = refs[0] if own is not None else None
        w_ref, m_ref, v_ref, g_ref, d_ref, nm_ref, nv_ref = refs[-7:]
        g = _sum_slabs(p_ref, None if own_ref is None else own_ref[...])
        g_ref[...] = g
        d_ref[...], nm_ref[...], nv_ref[...] = _adamw_math(w_ref[...], g, m_ref[...], v_ref[...])

    extra = [] if own is None else [own]
    return pl.pallas_call(
        body, name=name, grid=(rows // tr,),
        in_specs=[pl.BlockSpec((parts.shape[0], tr, cols), lambda i: (0, i, 0))] + [tile] * (len(extra) + 3),
        out_specs=[tile] * 4,
        out_shape=[jax.ShapeDtypeStruct((rows, cols), F32)] * 4,
        compiler_params=_params(1),
    )(parts, *extra, w, m, v)


def _sum_small(parts):
    def body(p_ref, out_ref):
        out_ref[...] = _sum_slabs(p_ref)

    return pl.pallas_call(body, name="sum_small", out_shape=jax.ShapeDtypeStruct(parts.shape[1:], F32))(parts)


def _adamw_whole(g, w, m, v, name):
    def body(g_ref, w_ref, m_ref, v_ref, d_ref, nm_ref, nv_ref):
        d_ref[...], nm_ref[...], nv_ref[...] = _adamw_math(w_ref[...], g_ref[...], m_ref[...], v_ref[...])

    return pl.pallas_call(body, name=name, out_shape=[jax.ShapeDtypeStruct(w.shape, F32)] * 3)(g, w, m, v)


def kernel(x, norm_pre_g, w_in, conv_w, w_out, norm_post_g, loss_target, m_norm_pre_g, m_w_in, m_conv_w, m_w_out,
           m_norm_post_g, v_norm_pre_g, v_w_in, v_conv_w, v_w_out, v_norm_post_g):
    n_conv = CONV_W // N_DEV
    w_in_g, w_out_g, conv_g = _ag_weights(w_in, w_out, conv_w)
    conv_full = conv_g[:, 0:3, 0:n_conv].transpose(1, 0, 2).reshape(3, CONV_W)
    grad_x, ht, dz, dw_out, dw_out_bf, small = _local_step(x[0], loss_target[0], norm_pre_g, norm_post_g, w_in_g,
                                                           w_out_g.reshape(D_MODEL, D_MODEL), conv_full)
    own_in, r_in, r_out, r_small = _dw_in_rs(ht, dz, dw_out_bf.reshape(N_DEV, SHARD_OUT, D_MODEL), small)
    me = 4 * lax.axis_index("x") + 2 * lax.axis_index("y") + lax.axis_index("c")
    own_out = lax.dynamic_index_in_dim(dw_out.reshape(N_DEV, SHARD_OUT, D_MODEL), me, keepdims=False)
    g_in, d_in, nm_in, nv_in = _adamw_slabs(r_in, own_in, w_in, m_w_in, v_w_in, "adamw_in", 256)
    g_out, d_out, nm_out, nv_out = _adamw_slabs(r_out, own_out, w_out, m_w_out, v_w_out, "adamw_out", SHARD_OUT)
    sums = _sum_small(r_small)
    g_pre, g_post, loss = sums[0], sums[1], sums[2, 0]
    g_conv = lax.dynamic_slice(sums[3:6, 0:CONV_W], (0, me * n_conv), (3, n_conv))
    vec = lambda a: a.reshape(1, D_MODEL)
    d_pre, nm_pre, nv_pre = _adamw_whole(vec(g_pre), vec(norm_pre_g), vec(m_norm_pre_g), vec(v_norm_pre_g), "adamw_pre")
    d_post, nm_post, nv_post = _adamw_whole(vec(g_post), vec(norm_post_g), vec(m_norm_post_g), vec(v_norm_post_g),
                                            "adamw_post")
    d_conv, nm_conv, nv_conv = _adamw_whole(g_conv, conv_w, m_conv_w, v_conv_w, "adamw_conv")
    flat = lambda a: a.reshape(D_MODEL)
    return (loss, grad_x[None], g_pre, g_in, g_conv, g_out, g_post,
            flat(d_pre), d_in, d_conv, d_out, flat(d_post),
            flat(nm_pre), nm_in, nm_conv, nm_out, flat(nm_post),
            flat(nv_pre), nv_in, nv_conv, nv_out, flat(nv_post))
```

```python
import functools

import jax
import jax.numpy as jnp
import numpy as np
from jax import lax
from jax.experimental import pallas as pl
from jax.experimental.pallas import tpu as pltpu

F32 = jnp.float32
BF16 = jnp.bfloat16

D_MODEL = 1024
HEAD_DIM = 64
ATTN_W = 768
CONV_W = 256
IN_W = 4096
REST_W = IN_W - 3 * ATTN_W
BLK = 128
N_DEV = 8
SHARD_IN = IN_W // N_DEV
SHARD_OUT = D_MODEL // N_DEV
DILATIONS = (1, 4, 16)
ROPE_THETA = 10000.0
NORM_EPS = 1e-6
NEG = -1e30

ADAM_LR = 0.001
ADAM_B1 = 0.9
ADAM_B2 = 0.999
ADAM_EPS = 1e-08
ADAM_WD = 0.01
ADAM_STEP = 10

VMEM_LIMIT = 56 * 1024 * 1024
MESH = pl.DeviceIdType.MESH


def _params(n_grid):
    return pltpu.CompilerParams(dimension_semantics=("arbitrary",) * n_grid, vmem_limit_bytes=VMEM_LIMIT)


def _resident(shape):
    zeros = (0,) * len(shape)
    return pl.BlockSpec(shape, lambda *_: zeros, pipeline_mode=pl.Buffered(1))


def _sigmoid(a):
    return 1.0 / (1.0 + jnp.exp(-a))


def _swap_halves(t, first_half):
    return jnp.where(first_half, pltpu.roll(t, BLK - 32, 1), pltpu.roll(t, 32, 1))


def _rope_tables(seq):
    half = HEAD_DIM // 2
    inv_freq = ROPE_THETA ** (-jnp.arange(half, dtype=F32) * 2.0 / HEAD_DIM)
    ang = jnp.arange(seq).astype(F32)[:, None] * inv_freq[None, :]
    cos, sin = jnp.cos(ang), jnp.sin(ang)
    return jnp.concatenate([cos] * 4, axis=1), jnp.concatenate([-sin, sin, -sin, sin], axis=1)


N_CHUNK = ATTN_W // BLK


def _lanes(r, c):
    return slice(r * ATTN_W + c * BLK, r * ATTN_W + (c + 1) * BLK)


def _to_residues(src, chunk0, dst_refs, tmp, rows, dtype):
    assert DILATIONS == (1, 4, 16)
    dst1, dst4, dst16 = dst_refs
    n4, n16 = rows // 4, rows // 16
    for c in range(N_CHUNK):
        dst1[:, _lanes(0, c)] = src[chunk0 + c].astype(dtype)
        for r1 in range(4):
            tmp[c, r1 * n4:(r1 + 1) * n4, :] = src[chunk0 + c, pl.ds(r1, n4, stride=4), :]
        for r1 in range(4):
            dst4[:, _lanes(r1, c)] = tmp[c, r1 * n4:(r1 + 1) * n4, :].astype(dtype)
            for r2 in range(4):
                dst16[:, _lanes(4 * r2 + r1, c)] = tmp[c, pl.ds(r1 * n4 + r2, n16, stride=4), :].astype(dtype)


def _from_residue(src_ref, dst, dil, rows, accumulate, tmp=None):
    n4, n16 = rows // 4, rows // 16

    def put(where, piece):
        if accumulate:
            dst[where] += piece
        else:
            dst[where] = piece

    for c in range(N_CHUNK):
        if dil == 1:
            put((c,), src_ref[:, _lanes(0, c)].astype(F32))
            continue
        for r1 in range(4):
            if dil == 4:
                piece = src_ref[:, _lanes(r1, c)].astype(F32)
            else:
                for r2 in range(4):
                    tmp[c, pl.ds(r1 * n4 + r2, n16, stride=4), :] = src_ref[:, _lanes(4 * r2 + r1, c)].astype(F32)
                piece = tmp[c, r1 * n4:(r1 + 1) * n4, :]
            put((c, pl.ds(r1, n4, stride=4), slice(None)), piece)


def _residue_spec(tm, dil):
    return pl.BlockSpec((tm // dil, dil * ATTN_W), lambda i: (i, 0))


def _residue_shape(seq, dil, dtype):
    return jax.ShapeDtypeStruct((seq // dil, dil * ATTN_W), dtype)


def _fwd_in(x, g_pre, w_in_g, cos_t, sin_t, tm=256):
    seq = x.shape[0]
    n_dil = len(DILATIONS)

    def body(x_ref, g_ref, w_ref, cos_ref, sin_ref, *rest):
        qkv_refs, (zr_ref, ht_ref, qkv_scr, tmp) = rest[:3 * n_dil], rest[3 * n_dil:]
        xv = x_ref[...]
        r = lax.rsqrt(jnp.mean(xv * xv, axis=-1, keepdims=True) + NORM_EPS)
        hf = (xv * r) * g_ref[...]
        ht_ref[...] = hf.T.astype(BF16)
        h = hf.astype(BF16)
        cos, sin = cos_ref[...], sin_ref[...]
        first_half = (lax.broadcasted_iota(jnp.int32, (tm, BLK), 1) & 32) == 0

        def rope(t):
            return t * cos + _swap_halves(t, first_half) * sin

        def project(j):
            return jnp.dot(h, w_ref[j], preferred_element_type=F32)

        def place(j, zj):
            for n in range(SHARD_IN // BLK):
                chunk, t = j * (SHARD_IN // BLK) + n, zj[:, n * BLK:(n + 1) * BLK]
                if chunk < N_CHUNK:
                    qkv_scr[chunk] = rope(t) * HEAD_DIM ** -0.5
                elif chunk < 2 * N_CHUNK:
                    qkv_scr[chunk] = rope(t)
                elif chunk < 3 * N_CHUNK:
                    qkv_scr[chunk] = t
                else:
                    zr_ref[:, (chunk - 3 * N_CHUNK) * BLK:(chunk - 3 * N_CHUNK + 1) * BLK] = t.astype(BF16)

        ahead = project(0)
        for j in range(N_DEV):
            zj = ahead
            if j + 1 < N_DEV:
                ahead = project(j + 1)
            place(j, zj)
            for a in range(3):
                if (a + 1) * N_CHUNK - 1 in range(j * (SHARD_IN // BLK), (j + 1) * (SHARD_IN // BLK)):
                    _to_residues(qkv_scr, a * N_CHUNK, [qkv_refs[3 * n + a] for n in range(n_dil)], tmp, tm, BF16)

    row = lambda w: pl.BlockSpec((tm, w), lambda i: (i, 0))
    outs = pl.pallas_call(
        body, name="fwd_in", grid=(seq // tm,),
        in_specs=[row(D_MODEL), _resident((1, D_MODEL)), _resident((N_DEV, D_MODEL, SHARD_IN)), row(BLK), row(BLK)],
        out_specs=[_residue_spec(tm, dil) for dil in DILATIONS for _ in range(3)]
        + [row(REST_W), pl.BlockSpec((D_MODEL, tm), lambda i: (0, i))],
        out_shape=[_residue_shape(seq, dil, BF16) for dil in DILATIONS for _ in range(3)]
        + [jax.ShapeDtypeStruct((seq, REST_W), BF16), jax.ShapeDtypeStruct((D_MODEL, seq), BF16)],
        scratch_shapes=[pltpu.VMEM((3 * N_CHUNK, tm, BLK), F32), pltpu.VMEM((N_CHUNK, tm, BLK), F32)],
        compiler_params=_params(1),
    )(x, g_pre.reshape(1, D_MODEL), w_in_g, cos_t, sin_t)
    qkv = [tuple(outs[3 * n:3 * n + 3]) for n in range(n_dil)]
    return qkv, outs[3 * n_dil], outs[3 * n_dil + 1]


def _band_bias(first_block):
    kj = lax.broadcasted_iota(jnp.int32, (2 * BLK, BLK), 0)
    qi = lax.broadcasted_iota(jnp.int32, (2 * BLK, BLK), 1)
    valid = (kj >= qi) & (kj <= qi + BLK)
    bias = jnp.where(valid, 0.0, NEG).astype(BF16)
    bias_first = jnp.where(valid & (kj >= BLK), 0.0, NEG).astype(BF16)
    onehot = ((kj & (BLK - 1)) == qi).astype(F32).astype(BF16)
    return onehot, bias, jnp.where(first_block, bias_first, bias)


def _stack_heads(t, head0):
    del head0
    keep0 = (lax.broadcasted_iota(jnp.int32, t.shape, 1) < HEAD_DIM).astype(F32).astype(BF16)
    return jnp.concatenate([t * keep0, t * (1 - keep0)], axis=0)


def _unstack_heads(t2, head0):
    return jnp.where(head0, t2[:BLK], t2[BLK:])


def _rows_per_head(a, head0):
    b = pltpu.roll(a, HEAD_DIM, 1)
    rows = jnp.concatenate([jnp.where(head0, a, b), jnp.where(head0, b, a)], axis=0)
    return jnp.concatenate([rows, rows], axis=1)


BLOCKS_PER_STEP = 16


def _attn_specs(length, dil):
    n_blocks = length // BLK
    tb = min(BLOCKS_PER_STEP, n_blocks)
    nc = BLOCKS_PER_STEP // tb
    assert (dil * N_CHUNK) % nc == 0 and n_blocks % tb == 0
    tile = pl.BlockSpec((tb * BLK, nc * BLK), lambda c, t: (t, c))
    prev = pl.BlockSpec((BLK, nc * BLK), lambda c, t: (jnp.maximum(t * tb - 1, 0), c))
    grid = (dil * N_CHUNK // nc, n_blocks // tb)
    return tb, nc, tile, prev, grid


def _load_keys(cat, prev_ref, cur_ref):
    cat[0:BLK] = prev_ref[...]
    cat[BLK:] = cur_ref[...]


def _attn_fwd(q, k, v, dil):
    length = q.shape[0]
    tb, nc, tile, prev, grid = _attn_specs(length, dil)

    def body(q_ref, kc_ref, kp_ref, vc_ref, vp_ref, o_ref, lse_ref, kcat, vcat):
        _load_keys(kcat, kp_ref, kc_ref)
        _load_keys(vcat, vp_ref, vc_ref)
        head0 = lax.broadcasted_iota(jnp.int32, (BLK, BLK), 1) < HEAD_DIM
        onehot, bias, bias_start = _band_bias(pl.program_id(1) == 0)
        ones = jnp.ones((2 * BLK, BLK), BF16)
        def scores(c, j):
            rows, cols = slice(j * BLK, (j + 1) * BLK), slice(c * BLK, (c + 1) * BLK)
            q2 = jnp.concatenate([_stack_heads(q_ref[rows, cols], head0), onehot], axis=1)
            kk = jnp.concatenate([kcat[j * BLK:(j + 2) * BLK, cols], bias_start if j == 0 else bias], axis=1)
            return (lax.dot_general(q2, kk, (((1,), (1,)), ((), ())), preferred_element_type=F32),)

        def probabilities(c, j, s):
            m = jnp.max(s, axis=1, keepdims=True)
            return m, jnp.exp(s - m).astype(BF16)

        def outputs(c, j, m, p):
            rows, cols = slice(j * BLK, (j + 1) * BLK), slice(c * BLK, (c + 1) * BLK)
            vv = jnp.concatenate([vcat[j * BLK:(j + 2) * BLK, cols], ones], axis=1)
            pv = jnp.dot(p, vv, preferred_element_type=F32)
            den = pv[:, BLK:]
            o_ref[rows, cols] = _unstack_heads(pv[:, :BLK] / den, head0).astype(BF16)
            lse_ref[rows, cols] = _unstack_heads(m + jnp.log(den), head0)

        units = [(c, j) for c in range(nc) for j in range(tb)]
        stage1, stage2 = {}, {}
        for n in range(len(units) + 2):
            if n < len(units):
                stage1[n] = scores(*units[n])
            if 0 <= n - 1 < len(units):
                stage2[n - 1] = probabilities(*units[n - 1], *stage1.pop(n - 1))
            if 0 <= n - 2 < len(units):
                outputs(*units[n - 2], *stage2.pop(n - 2))

    return pl.pallas_call(
        body, name=f"attn_fwd_d{dil}", grid=grid,
        in_specs=[tile, tile, prev, tile, prev], out_specs=[tile, tile],
        out_shape=[jax.ShapeDtypeStruct(q.shape, BF16), jax.ShapeDtypeStruct(q.shape, F32)],
        scratch_shapes=[pltpu.VMEM(((tb + 1) * BLK, nc * BLK), BF16)] * 2,
        compiler_params=_params(2),
    )(q, k, k, v, v)


def _attn_bwd(q, k, v, do, lse, delta, dil):
    length = q.shape[0]
    tb, nc, tile, prev, grid = _attn_specs(length, dil)
    whole = pl.BlockSpec((length, nc * BLK), lambda c, t: (0, c))

    def body(q_ref, do_ref, lse_ref, dl_ref, kc_ref, kp_ref, vc_ref, vp_ref, dq_ref, dk_ref, dv_ref, kcat, vcat):
        t = pl.program_id(1)
        _load_keys(kcat, kp_ref, kc_ref)
        _load_keys(vcat, vp_ref, vc_ref)
        head0 = lax.broadcasted_iota(jnp.int32, (BLK, BLK), 1) < HEAD_DIM
        onehot, bias, bias_start = _band_bias(t == 0)

        def scores(c, j):
            rows, cols = slice(j * BLK, (j + 1) * BLK), slice(c * BLK, (c + 1) * BLK)
            q2 = _stack_heads(q_ref[rows, cols], head0)
            do2 = _stack_heads(do_ref[rows, cols], head0)
            kk = kcat[j * BLK:(j + 2) * BLK, cols]
            s = lax.dot_general(jnp.concatenate([q2, onehot], axis=1),
                                jnp.concatenate([kk, bias_start if j == 0 else bias], axis=1),
                                (((1,), (1,)), ((), ())), preferred_element_type=F32)
            dp = lax.dot_general(do2, vcat[j * BLK:(j + 2) * BLK, cols], (((1,), (1,)), ((), ())),
                                 preferred_element_type=F32)
            return q2, do2, kk, s, dp

        def probabilities(c, j, q2, do2, kk, s, dp):
            rows, cols = slice(j * BLK, (j + 1) * BLK), slice(c * BLK, (c + 1) * BLK)
            p = jnp.exp(s - _rows_per_head(lse_ref[rows, cols], head0))
            ds = (p * (dp - _rows_per_head(dl_ref[rows, cols].astype(F32), head0))).astype(BF16)
            return q2, do2, kk, p.astype(BF16), ds

        def gradients(c, j, q2, do2, kk, p, ds):
            rows, cols = slice(j * BLK, (j + 1) * BLK), slice(c * BLK, (c + 1) * BLK)
            dq2 = jnp.dot(ds, kk, preferred_element_type=F32)
            dq_ref[rows, cols] = (_unstack_heads(dq2, head0) * HEAD_DIM ** -0.5).astype(BF16)
            dk2 = lax.dot_general(ds, q2, (((0,), (0,)), ((), ())), preferred_element_type=F32)
            dv2 = lax.dot_general(p, do2, (((0,), (0,)), ((), ())), preferred_element_type=F32)
            own = pl.ds(pl.multiple_of((t * tb + j) * BLK, BLK), BLK)
            dk_ref[own, cols] = dk2[BLK:].astype(BF16)
            dv_ref[own, cols] = dv2[BLK:].astype(BF16)

            def add_to_previous():
                before = pl.ds(pl.multiple_of((t * tb + j - 1) * BLK, BLK), BLK)
                dk_ref[before, cols] = (dk_ref[before, cols].astype(F32) + dk2[:BLK]).astype(BF16)
                dv_ref[before, cols] = (dv_ref[before, cols].astype(F32) + dv2[:BLK]).astype(BF16)

            if j == 0:
                pl.when(t > 0)(add_to_previous)
            else:
                add_to_previous()

        units = [(c, j) for c in range(nc) for j in range(tb)]
        stage1 = {0: scores(*units[0])}
        for n in range(len(units)):
            stage2 = probabilities(*units[n], *stage1.pop(n))
            if n + 1 < len(units):
                stage1[n + 1] = scores(*units[n + 1])
            gradients(*units[n], *stage2)

    return pl.pallas_call(
        body, name=f"attn_bwd_d{dil}", grid=grid,
        in_specs=[tile, tile, tile, tile, tile, prev, tile, prev], out_specs=[tile, whole, whole],
        out_shape=[jax.ShapeDtypeStruct(q.shape, BF16)] * 3,
        scratch_shapes=[pltpu.VMEM(((tb + 1) * BLK, nc * BLK), BF16)] * 2,
        compiler_params=_params(2),
    )(q, do, lse, delta, k, k, v, v)


HALO = 16


def _halo_specs(tm, seq):
    before = lambda w: pl.BlockSpec((HALO, w), lambda i: (jnp.maximum(i * (tm // HALO) - 1, 0), 0))
    after = lambda w: pl.BlockSpec((HALO, w), lambda i: (jnp.minimum((i + 1) * (tm // HALO), seq // HALO - 1), 0))
    return before, after


def _conv_taps(u, before, tm):
    row = lax.broadcasted_iota(jnp.int32, u.shape, 0)
    last, last2 = before[HALO - 1:HALO, :], before[HALO - 2:HALO - 1, :]
    u1 = jnp.where(row == 0, last, pltpu.roll(u, 1, 0))
    u2 = jnp.where(row == 0, last2, jnp.where(row == 1, last, pltpu.roll(u, 2, 0)))
    return u1, u2


def _attn_combine(o_parts, lse_parts, zr, conv_w, tm=256):
    seq = zr.shape[0]
    a0, h0, b0, c0, g0 = 0, ATTN_W, ATTN_W + CONV_W, ATTN_W + 2 * CONV_W, ATTN_W + 3 * CONV_W

    def body(o1, o2, o3, l1, l2, l3, zr_ref, zp_ref, w_ref, mixed_ref, o_ref, lse1, lse2, lse3, *scr):
        i = pl.program_id(0)
        for src, dst, dil in zip((o2, o3, l2, l3), scr[:4], DILATIONS[1:] * 2):
            _from_residue(src, dst, dil, tm, accumulate=False, tmp=scr[5])
        for c in range(N_CHUNK):
            cols = slice(c * BLK, (c + 1) * BLK)
            la, lb, lc = l1[:, cols], scr[2][c], scr[3][c]
            top = jnp.maximum(jnp.maximum(la, lb), lc)
            ea, eb, ec = jnp.exp(la - top), jnp.exp(lb - top), jnp.exp(lc - top)
            den = ea + eb + ec
            o = (ea / den) * o1[:, cols].astype(F32) + (eb / den) * scr[0][c] + (ec / den) * scr[1][c]
            o_ref[:, cols] = o.astype(BF16)
            scr[4][c] = top + jnp.log(den)
            ga = zr_ref[:, cols].astype(F32)
            mixed_ref[:, cols] = (o * (ga * _sigmoid(ga))).astype(BF16)
        _to_residues(scr[4], 0, (lse1, lse2, lse3), scr[5], tm, F32)
        part = lambda ref, lo, hi: ref[:, lo:hi].astype(F32)
        u = part(zr_ref, c0, g0) * part(zr_ref, h0, b0)
        before = jnp.where(i > 0, part(zp_ref, c0, g0) * part(zp_ref, h0, b0), 0.0)
        u1, u2 = _conv_taps(u, before, tm)
        y = u2 * w_ref[0:1, :] + u1 * w_ref[1:2, :] + u * w_ref[2:3, :]
        gc = part(zr_ref, g0, REST_W)
        mixed_ref[:, ATTN_W:] = ((part(zr_ref, b0, c0) * y) * (gc * _sigmoid(gc))).astype(BF16)

    row = lambda w: pl.BlockSpec((tm, w), lambda i: (i, 0))
    before, _ = _halo_specs(tm, seq)
    views = [_residue_spec(tm, dil) for dil in DILATIONS]
    outs = pl.pallas_call(
        body, name="attn_combine", grid=(seq // tm,),
        in_specs=views * 2 + [row(REST_W), before(REST_W), _resident((3, CONV_W))],
        out_specs=[row(D_MODEL), row(ATTN_W)] + views,
        out_shape=[jax.ShapeDtypeStruct((seq, D_MODEL), BF16), jax.ShapeDtypeStruct((seq, ATTN_W), BF16)]
        + [_residue_shape(seq, dil, F32) for dil in DILATIONS],
        scratch_shapes=[pltpu.VMEM((N_CHUNK, tm, BLK), F32)] * 6,
        compiler_params=_params(1),
    )(*o_parts, *lse_parts, zr, zr, conv_w)
    return outs[0], outs[1], outs[2:]


def _out_loss_bwd(mixed, w_out_g, x, target, g_post, tm=512):
    seq = x.shape[0]

    def body(mx_ref, w_ref, x_ref, t_ref, g_ref, dout_ref, dmx_ref, dw_ref, dwb_ref, st_ref):
        i = pl.program_id(0)
        mx = mx_ref[...]
        y = jnp.dot(mx, w_ref[...], preferred_element_type=F32)
        r = lax.rsqrt(jnp.mean(y * y, axis=-1, keepdims=True) + NORM_EPS)
        yhat = y * r
        g = g_ref[...]
        err = (x_ref[...] + yhat * g) - t_ref[...]
        dn = err * (1.0 / D_MODEL)
        dout_ref[...] = dn
        tg = dn * g
        dy = (r * (tg - yhat * jnp.mean(tg * yhat, axis=-1, keepdims=True))).astype(BF16)
        dmx_ref[...] = lax.dot_general(dy, w_ref[...], (((1,), (1,)), ((), ())),
                                       preferred_element_type=F32).astype(BF16)
        dw = lax.dot_general(mx, dy, (((0,), (0,)), ((), ())), preferred_element_type=F32)
        gsum = jnp.sum(dn * yhat, axis=0, keepdims=True)
        lsum = jnp.broadcast_to(0.5 / D_MODEL * jnp.sum(err * err), (1, D_MODEL))

        @pl.when(i == 0)
        def _():
            dw_ref[...] = dw
            st_ref[...] = jnp.zeros_like(st_ref)
            st_ref[0:1, :] = gsum
            st_ref[1:2, :] = lsum

        @pl.when(i > 0)
        def _():
            dw_ref[...] += dw
            st_ref[0:1, :] += gsum
            st_ref[1:2, :] += lsum

        @pl.when(i == seq // tm - 1)
        def _():
            dwb_ref[...] = dw_ref[...].astype(BF16)

    row = lambda w: pl.BlockSpec((tm, w), lambda i: (i, 0))
    whole = pl.BlockSpec((D_MODEL, D_MODEL), lambda i: (0, 0))
    return pl.pallas_call(
        body, name="out_loss_bwd", grid=(seq // tm,),
        in_specs=[row(D_MODEL), _resident((D_MODEL, D_MODEL)), row(D_MODEL), row(D_MODEL), _resident((1, D_MODEL))],
        out_specs=[row(D_MODEL), row(D_MODEL), whole, whole, pl.BlockSpec((8, D_MODEL), lambda i: (0, 0))],
        out_shape=[jax.ShapeDtypeStruct((seq, D_MODEL), F32), jax.ShapeDtypeStruct((seq, D_MODEL), BF16),
                   jax.ShapeDtypeStruct((D_MODEL, D_MODEL), F32), jax.ShapeDtypeStruct((D_MODEL, D_MODEL), BF16),
                   jax.ShapeDtypeStruct((8, D_MODEL), F32)],
        compiler_params=_params(1),
    )(mixed, w_out_g, x, target, g_post.reshape(1, D_MODEL))


def _head_sum(prod, same_head):
    hi = prod.astype(BF16)
    lo = (prod - hi.astype(F32)).astype(BF16)
    return (jnp.dot(hi, same_head, preferred_element_type=F32) + jnp.dot(lo, same_head, preferred_element_type=F32))


def _gate_bwd(dmixed, zr, o, conv_w, tm=256):
    seq = zr.shape[0]
    n_tiles = seq // tm
    n_dil = len(DILATIONS)
    a0, h0, b0, c0, g0 = 0, ATTN_W, ATTN_W + CONV_W, ATTN_W + 2 * CONV_W, ATTN_W + 3 * CONV_W

    def body(dm_ref, dmn_ref, zr_ref, zp_ref, zn_ref, o_ref, w_ref, *rest):
        do_refs, dl_refs = rest[:n_dil], rest[n_dil:2 * n_dil]
        dz_ref, dw_ref, do_scr, dl_scr, tmp = rest[2 * n_dil:]
        i = pl.program_id(0)
        part = lambda ref, lo, hi: ref[:, lo:hi].astype(F32)
        ga = part(zr_ref, a0, h0)
        sg = _sigmoid(ga)
        dattn = part(dm_ref, 0, ATTN_W)
        ov = o_ref[...].astype(F32)
        do = dattn * (ga * sg)
        dz_ref[:, a0:h0] = (dattn * ov * (sg * (1.0 + ga * (1.0 - sg)))).astype(BF16)
        li = lax.broadcasted_iota(jnp.int32, (BLK, BLK), 0) // HEAD_DIM
        lj = lax.broadcasted_iota(jnp.int32, (BLK, BLK), 1) // HEAD_DIM
        same_head = (li == lj).astype(BF16)
        prod = do * ov
        for c in range(N_CHUNK):
            cols = slice(c * BLK, (c + 1) * BLK)
            do_scr[c] = do[:, cols]
            dl_scr[c] = _head_sum(prod[:, cols], same_head)
        _to_residues(do_scr, 0, do_refs, tmp, tm, BF16)
        _to_residues(dl_scr, 0, dl_refs, tmp, tm, BF16)

        ch, cb, cc, gc = (part(zr_ref, lo, hi) for lo, hi in ((h0, b0), (b0, c0), (c0, g0), (g0, REST_W)))
        u = cc * ch
        before = jnp.where(i > 0, part(zp_ref, c0, g0) * part(zp_ref, h0, b0), 0.0)
        u1, u2 = _conv_taps(u, before, tm)
        w0, w1, w2 = w_ref[0:1, :], w_ref[1:2, :], w_ref[2:3, :]
        y = u2 * w0 + u1 * w1 + u * w2
        sc = _sigmoid(gc)
        silu_c = gc * sc
        dconv = part(dm_ref, ATTN_W, D_MODEL)
        dz_ref[:, b0:c0] = (dconv * y * silu_c).astype(BF16)
        dz_ref[:, g0:] = (dconv * (cb * y) * (sc * (1.0 + gc * (1.0 - sc)))).astype(BF16)
        dy = dconv * cb * silu_c
        gn = part(zn_ref, g0, REST_W)
        after = jnp.where(i < n_tiles - 1,
                          part(dmn_ref, ATTN_W, D_MODEL) * part(zn_ref, b0, c0) * (gn * _sigmoid(gn)), 0.0)
        row = lax.broadcasted_iota(jnp.int32, dy.shape, 0)
        nxt, nxt2 = after[0:1, :], after[1:2, :]
        dy1 = jnp.where(row == tm - 1, nxt, pltpu.roll(dy, tm - 1, 0))
        dy2 = jnp.where(row == tm - 1, nxt2, jnp.where(row == tm - 2, nxt, pltpu.roll(dy, tm - 2, 0)))
        du = dy * w2 + dy1 * w1 + dy2 * w0
        dz_ref[:, c0:g0] = (du * ch).astype(BF16)
        dz_ref[:, h0:b0] = (du * cc).astype(BF16)
        dws = [jnp.sum(dy * u2, axis=0, keepdims=True), jnp.sum(dy * u1, axis=0, keepdims=True),
               jnp.sum(dy * u, axis=0, keepdims=True)]

        @pl.when(i == 0)
        def _():
            dw_ref[...] = jnp.zeros_like(dw_ref)

        for n, part in enumerate(dws):
            dw_ref[n:n + 1, :] += part

    row_spec = lambda w: pl.BlockSpec((tm, w), lambda i: (i, 0))
    before, after = _halo_specs(tm, seq)
    views = [_residue_spec(tm, dil) for dil in DILATIONS]
    outs = pl.pallas_call(
        body, name="gate_bwd", grid=(n_tiles,),
        in_specs=[row_spec(D_MODEL), after(D_MODEL), row_spec(REST_W), before(REST_W), after(REST_W),
                  row_spec(ATTN_W), _resident((3, CONV_W))],
        out_specs=views * 2 + [row_spec(REST_W), pl.BlockSpec((8, CONV_W), lambda i: (0, 0))],
        out_shape=[_residue_shape(seq, dil, BF16) for dil in DILATIONS] * 2
        + [jax.ShapeDtypeStruct((seq, REST_W), BF16), jax.ShapeDtypeStruct((8, CONV_W), F32)],
        scratch_shapes=[pltpu.VMEM((N_CHUNK, tm, BLK), F32)] * 3,
        compiler_params=_params(1),
    )(dmixed, dmixed, zr, zr, zr, o, conv_w)
    return outs[:n_dil], outs[n_dil:2 * n_dil], outs[2 * n_dil], outs[2 * n_dil + 1]


def _in_bwd(dqs, dks, dvs, dzr, cos_t, sin_t, x, d_out, g_pre, w_in_g, tm=256):
    seq = x.shape[0]

    def body(q1, q2, q3, k1, k2, k3, v1, v2, v3, dzr_ref, cos_ref, sin_ref, x_ref, dout_ref, g_ref, w_ref,
             dz_ref, gx_ref, st_ref, dq_scr, dk_scr, dv_scr, tmp):
        i = pl.program_id(0)
        cos, sin = cos_ref[...], sin_ref[...]
        first_half = (lax.broadcasted_iota(jnp.int32, (tm, BLK), 1) & 32) == 0
        streams = [((q1, q2, q3), dq_scr), ((k1, k2, k3), dk_scr), ((v1, v2, v3), dv_scr)]
        per_slab = SHARD_IN // BLK

        def unrope(t):
            return t * cos - _swap_halves(t, first_half) * sin

        def assemble(j):
            for chunk in range(j * per_slab, (j + 1) * per_slab):
                a, c = divmod(chunk, N_CHUNK)
                if a < 3 and c == 0:
                    parts, total = streams[a]
                    for n, dil in enumerate(DILATIONS):
                        _from_residue(parts[n], total, dil, tm, accumulate=n > 0, tmp=tmp)
                if a < 2:
                    val = unrope(streams[a][1][c]).astype(BF16)
                elif a == 2:
                    val = dv_scr[c].astype(BF16)
                else:
                    val = dzr_ref[:, (chunk - 3 * N_CHUNK) * BLK:(chunk - 3 * N_CHUNK + 1) * BLK]
                dz_ref[:, chunk * BLK:(chunk + 1) * BLK] = val
            return dz_ref[:, j * SHARD_IN:(j + 1) * SHARD_IN]

        ahead = assemble(0)
        dh = None
        for j in range(N_DEV):
            slab = ahead
            if j + 1 < N_DEV:
                ahead = assemble(j + 1)
            part = lax.dot_general(slab, w_ref[j], (((1,), (1,)), ((), ())), preferred_element_type=F32)
            dh = part if dh is None else dh + part
        xv = x_ref[...]
        r = lax.rsqrt(jnp.mean(xv * xv, axis=-1, keepdims=True) + NORM_EPS)
        xhat = xv * r
        tg = dh * g_ref[...]
        gx_ref[...] = dout_ref[...] + r * (tg - xhat * jnp.mean(tg * xhat, axis=-1, keepdims=True))
        gsum = jnp.sum(dh * xhat, axis=0, keepdims=True)

        @pl.when(i == 0)
        def _():
            st_ref[...] = jnp.zeros_like(st_ref)

        st_ref[0:1, :] += gsum

    row = lambda w: pl.BlockSpec((tm, w), lambda i: (i, 0))
    return pl.pallas_call(
        body, name="in_bwd", grid=(seq // tm,),
        in_specs=[_residue_spec(tm, dil) for dil in DILATIONS] * 3
        + [row(REST_W), row(BLK), row(BLK), row(D_MODEL), row(D_MODEL), _resident((1, D_MODEL)),
           _resident((N_DEV, D_MODEL, SHARD_IN))],
        out_specs=[row(IN_W), row(D_MODEL), pl.BlockSpec((8, D_MODEL), lambda i: (0, 0))],
        out_shape=[jax.ShapeDtypeStruct((seq, IN_W), BF16), jax.ShapeDtypeStruct((seq, D_MODEL), F32),
                   jax.ShapeDtypeStruct((8, D_MODEL), F32)],
        scratch_shapes=[pltpu.VMEM((N_CHUNK, tm, BLK), F32)] * 4,
        compiler_params=_params(1),
    )(*dqs, *dks, *dvs, dzr, cos_t, sin_t, x, d_out, g_pre.reshape(1, D_MODEL), w_in_g)


def _local_step(x, target, g_pre, g_post, w_in_g, w_out_g, conv_w):
    seq = x.shape[0]
    cos_t, sin_t = _rope_tables(seq)
    qkv, zr, ht = _fwd_in(x, g_pre, w_in_g, cos_t, sin_t)
    parts = [_attn_fwd(*qkv[n], dil) for n, dil in enumerate(DILATIONS)]
    mixed, o, lse = _attn_combine([p[0] for p in parts], [p[1] for p in parts], zr, conv_w)
    d_out, dmixed, dw_out, dw_out_bf, st_post = _out_loss_bwd(mixed, w_out_g, x, target, g_post)
    do, delta, dzr, dconv = _gate_bwd(dmixed, zr, o, conv_w)
    grads = [_attn_bwd(*qkv[n], do[n], lse[n], delta[n], dil) for n, dil in enumerate(DILATIONS)]
    dz, grad_x, st_pre = _in_bwd([g[0] for g in grads], [g[1] for g in grads], [g[2] for g in grads], dzr,
                                 cos_t, sin_t, x, d_out, g_pre, w_in_g)
    conv_rows = jnp.pad(dconv[0:3], ((0, 0), (0, D_MODEL - CONV_W)))
    small = jnp.concatenate([st_pre[0:1], st_post[0:2], conv_rows, jnp.zeros((2, D_MODEL), F32)], axis=0)
    return grad_x, ht, dz, dw_out, dw_out_bf, small


def _coords():
    return lax.axis_index("x"), lax.axis_index("y"), lax.axis_index("c")


def _peer(k):
    x, y, c = _coords()
    px = 1 - x if k & 4 else x
    py = 1 - y if k & 2 else y
    pc = 1 - c if k & 1 else c
    return (px, py, pc), 4 * px + 2 * py + pc


HBM_SPEC = pl.BlockSpec(memory_space=pltpu.HBM)
VMEM_SPEC = pl.BlockSpec(memory_space=pltpu.VMEM)


def _ag_weights(w_in, w_out, conv_w):
    def body(win_ref, wout_ref, cw_ref, gin_ref, gout_ref, gcw_ref, win_bf, wout_bf, cw_pad, send_sems, recv_sems,
             local_sems):
        x, y, c = _coords()
        me, sibling = (x, y, c), (x, y, 1 - c)
        chips = [(1 - x, y), (x, 1 - y), (1 - x, 1 - y)]
        slab = lambda px, py, pc: 4 * px + 2 * py + pc
        win_bf[...] = win_ref[...].astype(BF16)
        wout_bf[...] = wout_ref[...].astype(BF16)
        cw_pad[...] = jnp.zeros_like(cw_pad)
        cw_pad[0:3, 0:CONV_W // N_DEV] = cw_ref[...]
        mine = [win_bf, wout_bf, cw_pad]
        gathered = [gin_ref, gout_ref, gcw_ref]

        def copies(k, block, to, own=False):
            return [pltpu.make_async_remote_copy(src_ref=mine[a] if own else gathered[a].at[slab(*block)],
                                                 dst_ref=gathered[a].at[slab(*block)], send_sem=send_sems.at[k, a],
                                                 recv_sem=recv_sems.at[k, a], device_id=to, device_id_type=MESH)
                    for a in range(3)]

        local = [pltpu.make_async_copy(mine[a], gathered[a].at[slab(*me)], local_sems.at[a]) for a in range(3)]
        for cp in local:
            cp.start()
        first = copies(0, me, sibling, own=True)
        for j, chip in enumerate(chips):
            first += copies(1 + j, me, (*chip, c), own=True)
        for cp in first:
            cp.start()
        passed = []
        for j, chip in enumerate(chips):
            for cp in copies(1 + j, (*chip, c), me):
                cp.wait_recv()
            onward = copies(4 + j, (*chip, c), sibling)
            for cp in onward:
                cp.start()
            passed += onward
        for cp in copies(0, sibling, me):
            cp.wait_recv()
        for j, chip in enumerate(chips):
            for cp in copies(4 + j, (*chip, 1 - c), me):
                cp.wait_recv()
        for cp in first + passed:
            cp.wait_send()
        for cp in local:
            cp.wait()

    return pl.pallas_call(
        body, name="ag_weights",
        in_specs=[VMEM_SPEC, VMEM_SPEC, VMEM_SPEC], out_specs=[HBM_SPEC, HBM_SPEC, HBM_SPEC],
        out_shape=[jax.ShapeDtypeStruct((N_DEV, D_MODEL, SHARD_IN), BF16),
                   jax.ShapeDtypeStruct((N_DEV, SHARD_OUT, D_MODEL), BF16),
                   jax.ShapeDtypeStruct((N_DEV, 8, BLK), F32)],
        scratch_shapes=[pltpu.VMEM((D_MODEL, SHARD_IN), BF16), pltpu.VMEM((SHARD_OUT, D_MODEL), BF16),
                        pltpu.VMEM((8, BLK), F32), pltpu.SemaphoreType.DMA((N_DEV - 1, 3)),
                        pltpu.SemaphoreType.DMA((N_DEV - 1, 3)), pltpu.SemaphoreType.DMA((3,))],
        compiler_params=pltpu.CompilerParams(vmem_limit_bytes=VMEM_LIMIT),
    )(w_in, w_out, conv_w)


def _dw_in_rs(ht, dz, dw_out, small):
    seq = dz.shape[0]

    def body(cols_ref, ht_ref, dz_ref, dout_ref, sm_ref, own_ref, rin_ref, rout_ref, rsm_ref, to_sibling, landed,
             to_chip, zero_buf, d2d_send, d2d_recv, ici_send, ici_recv, side_send, side_recv, local_sems):
        del cols_ref
        step = pl.program_id(0)
        x, y, c = _coords()
        me = 4 * x + 2 * y + c
        sibling = (x, y, 1 - c)
        chips = [(1 - x, y), (x, 1 - y), (1 - x, 1 - y)]

        def d2d(n):
            return pltpu.make_async_remote_copy(src_ref=to_sibling.at[n], dst_ref=landed.at[n], send_sem=d2d_send.at[n],
                                                recv_sem=d2d_recv.at[n], device_id=sibling, device_id_type=MESH)

        def ici(n):
            return pltpu.make_async_remote_copy(src_ref=to_chip.at[n], dst_ref=rin_ref.at[n], send_sem=ici_send.at[n],
                                                recv_sem=ici_recv.at[n], device_id=(*chips[n], c), device_id_type=MESH)

        def side(k, mine):
            peer, peer_idx = _peer(k)
            src_slab, dst_slab = (peer_idx, me) if mine else (me, peer_idx)
            pairs = [(dout_ref.at[src_slab], rout_ref.at[dst_slab]), (sm_ref, rsm_ref.at[dst_slab])]
            return [pltpu.make_async_remote_copy(src_ref=src, dst_ref=dst, send_sem=side_send.at[k - 1, a],
                                                 recv_sem=side_recv.at[k - 1, a], device_id=peer, device_id_type=MESH)
                    for a, (src, dst) in enumerate(pairs)]

        local = [pltpu.make_async_copy(zero_buf, rout_ref.at[me], local_sems.at[0]),
                 pltpu.make_async_copy(sm_ref, rsm_ref.at[me], local_sems.at[1])]

        @pl.when(step == 0)
        def _():
            zero_buf[...] = jnp.zeros_like(zero_buf)
            for cp in local:
                cp.start()
            for k in range(1, N_DEV):
                for cp in side(k, mine=True):
                    cp.start()

        dw = jnp.dot(ht_ref[...], dz_ref[...], preferred_element_type=F32)
        for n in range(4):
            @pl.when(step == n)
            def _(n=n):
                to_sibling[n] = dw.astype(BF16)
                d2d(n).start()

        for n in range(3):
            @pl.when(step == 4 + n)
            def _(n=n):
                d2d(n).wait_recv()
                to_chip[n] = (dw + landed[n].astype(F32)).astype(BF16)
                ici(n).start()

        @pl.when(step == N_DEV - 1)
        def _():
            d2d(3).wait_recv()
            own_ref[...] = dw + landed[3].astype(F32)
            for n in range(3):
                ici(n).wait_recv()
            for k in range(1, N_DEV):
                for cp in side(k, mine=False):
                    cp.wait_recv()
            for n in range(4):
                d2d(n).wait_send()
            for n in range(3):
                ici(n).wait_send()
            for k in range(1, N_DEV):
                for cp in side(k, mine=True):
                    cp.wait_send()
            for cp in local:
                cp.wait()

    x, y, c = _coords()
    chip_order = [(1 - x, y), (x, 1 - y), (1 - x, 1 - y), (x, y)]
    cols = jnp.stack([4 * px + 2 * py + pc for pc in (1 - c, c) for px, py in chip_order]).astype(jnp.int32)
    slab = (D_MODEL, SHARD_IN)
    grid_spec = pltpu.PrefetchScalarGridSpec(
        num_scalar_prefetch=1, grid=(N_DEV,),
        in_specs=[pl.BlockSpec((D_MODEL, seq), lambda s, cols: (0, 0), pipeline_mode=pl.Buffered(1)),
                  pl.BlockSpec((seq, SHARD_IN), lambda s, cols: (0, cols[s])), HBM_SPEC, HBM_SPEC],
        out_specs=[pl.BlockSpec(slab, lambda s, cols: (0, 0)), HBM_SPEC, HBM_SPEC, HBM_SPEC],
        scratch_shapes=[pltpu.VMEM((4, *slab), BF16), pltpu.VMEM((4, *slab), BF16), pltpu.VMEM((3, *slab), BF16),
                        pltpu.VMEM((SHARD_OUT, D_MODEL), BF16),
                        pltpu.SemaphoreType.DMA((4,)), pltpu.SemaphoreType.DMA((4,)),
                        pltpu.SemaphoreType.DMA((3,)), pltpu.SemaphoreType.DMA((3,)),
                        pltpu.SemaphoreType.DMA((N_DEV - 1, 2)), pltpu.SemaphoreType.DMA((N_DEV - 1, 2)),
                        pltpu.SemaphoreType.DMA((2,))])
    return pl.pallas_call(
        body, name="dw_in_rs", grid_spec=grid_spec,
        out_shape=[jax.ShapeDtypeStruct(slab, F32),
                   jax.ShapeDtypeStruct((3, *slab), BF16),
                   jax.ShapeDtypeStruct((N_DEV, SHARD_OUT, D_MODEL), BF16),
                   jax.ShapeDtypeStruct((N_DEV, 8, D_MODEL), F32)],
        compiler_params=_params(1),
    )(cols, ht, dz, dw_out, small)


def _adamw_math(w, g, m, v):
    m = ADAM_B1 * m + (1.0 - ADAM_B1) * g
    v = ADAM_B2 * v + (1.0 - ADAM_B2) * (g * g)
    m_hat = m / (1.0 - ADAM_B1 ** ADAM_STEP)
    v_hat = v / (1.0 - ADAM_B2 ** ADAM_STEP)
    delta = -ADAM_LR * (m_hat / (jnp.sqrt(v_hat) + ADAM_EPS) + ADAM_WD * w)
    return delta, m, v


def _sum_slabs(ref, first=None):
    total = ref[0].astype(F32) if first is None else first + ref[0].astype(F32)
    for s in range(1, ref.shape[0]):
        total = total + ref[s].astype(F32)
    return total


def _adamw_slabs(parts, own, w, m, v, name, tr):
    rows, cols = w.shape
    tile = pl.BlockSpec((tr, cols), lambda i: (i, 0))

    def body(p_ref, *refs):
        own_ref = refs[0] if own is not None else None
        w_ref, m_ref, v_ref, g_ref, d_ref, nm_ref, nv_ref = refs[-7:]
        g = _sum_slabs(p_ref, None if own_ref is None else own_ref[...])
        g_ref[...] = g
        d_ref[...], nm_ref[...], nv_ref[...] = _adamw_math(w_ref[...], g, m_ref[...], v_ref[...])

    extra = [] if own is None else [own]
    return pl.pallas_call(
        body, name=name, grid=(rows // tr,),
        in_specs=[pl.BlockSpec((parts.shape[0], tr, cols), lambda i: (0, i, 0))] + [tile] * (len(extra) + 3),
        out_specs=[tile] * 4,
        out_shape=[jax.ShapeDtypeStruct((rows, cols), F32)] * 4,
        compiler_params=_params(1),
    )(parts, *extra, w, m, v)


def _sum_small(parts):
    def body(p_ref, out_ref):
        out_ref[...] = _sum_slabs(p_ref)

    return pl.pallas_call(body, name="sum_small", out_shape=jax.ShapeDtypeStruct(parts.shape[1:], F32))(parts)


def _adamw_whole(g, w, m, v, name):
    def body(g_ref, w_ref, m_ref, v_ref, d_ref, nm_ref, nv_ref):
        d_ref[...], nm_ref[...], nv_ref[...] = _adamw_math(w_ref[...], g_ref[...], m_ref[...], v_ref[...])

    return pl.pallas_call(body, name=name, out_shape=[jax.ShapeDtypeStruct(w.shape, F32)] * 3)(g, w, m, v)


def kernel(x, norm_pre_g, w_in, conv_w, w_out, norm_post_g, loss_target, m_norm_pre_g, m_w_in, m_conv_w, m_w_out,
           m_norm_post_g, v_norm_pre_g, v_w_in, v_conv_w, v_w_out, v_norm_post_g):
    n_conv = CONV_W // N_DEV
    w_in_g, w_out_g, conv_g = _ag_weights(w_in, w_out, conv_w)
    conv_full = conv_g[:, 0:3, 0:n_conv].transpose(1, 0, 2).reshape(3, CONV_W)
    grad_x, ht, dz, dw_out, dw_out_bf, small = _local_step(x[0], loss_target[0], norm_pre_g, norm_post_g, w_in_g,
                                                           w_out_g.reshape(D_MODEL, D_MODEL), conv_full)
    own_in, r_in, r_out, r_small = _dw_in_rs(ht, dz, dw_out_bf.reshape(N_DEV, SHARD_OUT, D_MODEL), small)
    me = 4 * lax.axis_index("x") + 2 * lax.axis_index("y") + lax.axis_index("c")
    own_out = lax.dynamic_index_in_dim(dw_out.reshape(N_DEV, SHARD_OUT, D_MODEL), me, keepdims=False)
    g_in, d_in, nm_in, nv_in = _adamw_slabs(r_in, own_in, w_in, m_w_in, v_w_in, "adamw_in", 256)
    g_out, d_out, nm_out, nv_out = _adamw_slabs(r_out, own_out, w_out, m_w_out, v_w_out, "adamw_out", SHARD_OUT)
    sums = _sum_small(r_small)
    g_pre, g_post, loss = sums[0], sums[1], sums[2, 0]
    g_conv = lax.dynamic_slice(sums[3:6, 0:CONV_W], (0, me * n_conv), (3, n_conv))
    vec = lambda a: a.reshape(1, D_MODEL)
    d_pre, nm_pre, nv_pre = _adamw_whole(vec(g_pre), vec(norm_pre_g), vec(m_norm_pre_g), vec(v_norm_pre_g), "adamw_pre")
    d_post, nm_post, nv_post = _adamw_whole(vec(g_post), vec(norm_post_g), vec(m_norm_post_g), vec(v_norm_post_g),
                                            "adamw_post")
    d_conv, nm_conv, nv_conv = _adamw_whole(g_conv, conv_w, m_conv_w, v_conv_w, "adamw_conv")
    flat = lambda a: a.reshape(D_MODEL)
    return (loss, grad_x[None], g_pre, g_in, g_conv, g_out, g_post,
            flat(d_pre), d_in, d_conv, d_out, flat(d_post),
            flat(nm_pre), nm_in, nm_conv, nm_out, flat(nm_post),
            flat(nv_pre), nv_in, nv_conv, nv_out, flat(nv_post))
```

```python
import functools

import jax
import jax.numpy as jnp
import numpy as np
from jax import lax
from jax.experimental import pallas as pl
from jax.experimental.pallas import tpu as pltpu

F32 = jnp.float32
BF16 = jnp.bfloat16

D_MODEL = 1024
HEAD_DIM = 64
ATTN_W = 768
CONV_W = 256
IN_W = 4096
REST_W = IN_W - 3 * ATTN_W
BLK = 128
N_DEV = 8
SHARD_IN = IN_W // N_DEV
SHARD_OUT = D_MODEL // N_DEV
DILATIONS = (1, 4, 16)
ROPE_THETA = 10000.0
NORM_EPS = 1e-6
NEG = -1e30

ADAM_LR = 0.001
ADAM_B1 = 0.9
ADAM_B2 = 0.999
ADAM_EPS = 1e-08
ADAM_WD = 0.01
ADAM_STEP = 10

VMEM_LIMIT = 56 * 1024 * 1024
MESH = pl.DeviceIdType.MESH


def _params(n_grid):
    return pltpu.CompilerParams(dimension_semantics=("arbitrary",) * n_grid, vmem_limit_bytes=VMEM_LIMIT)


def _resident(shape):
    zeros = (0,) * len(shape)
    return pl.BlockSpec(shape, lambda *_: zeros, pipeline_mode=pl.Buffered(1))


def _sigmoid(a):
    return 1.0 / (1.0 + jnp.exp(-a))


def _swap_halves(t, first_half):
    return jnp.where(first_half, pltpu.roll(t, BLK - 32, 1), pltpu.roll(t, 32, 1))


def _rope_tables(seq):
    half = HEAD_DIM // 2
    inv_freq = ROPE_THETA ** (-jnp.arange(half, dtype=F32) * 2.0 / HEAD_DIM)
    ang = jnp.arange(seq).astype(F32)[:, None] * inv_freq[None, :]
    cos, sin = jnp.cos(ang), jnp.sin(ang)
    return jnp.concatenate([cos] * 4, axis=1), jnp.concatenate([-sin, sin, -sin, sin], axis=1)


N_CHUNK = ATTN_W // BLK


def _lanes(r, c):
    return slice(r * ATTN_W + c * BLK, r * ATTN_W + (c + 1) * BLK)


def _to_residues(src, chunk0, dst_refs, tmp, rows, dtype):
    assert DILATIONS == (1, 4, 16)
    dst1, dst4, dst16 = dst_refs
    n4, n16 = rows // 4, rows // 16
    for c in range(N_CHUNK):
        dst1[:, _lanes(0, c)] = src[chunk0 + c].astype(dtype)
        for r1 in range(4):
            tmp[c, r1 * n4:(r1 + 1) * n4, :] = src[chunk0 + c, pl.ds(r1, n4, stride=4), :]
        for r1 in range(4):
            dst4[:, _lanes(r1, c)] = tmp[c, r1 * n4:(r1 + 1) * n4, :].astype(dtype)
            for r2 in range(4):
                dst16[:, _lanes(4 * r2 + r1, c)] = tmp[c, pl.ds(r1 * n4 + r2, n16, stride=4), :].astype(dtype)


def _from_residue(src_ref, dst, dil, rows, accumulate, tmp=None):
    n4, n16 = rows // 4, rows // 16

    def put(where, piece):
        if accumulate:
            dst[where] += piece
        else:
            dst[where] = piece

    for c in range(N_CHUNK):
        if dil == 1:
            put((c,), src_ref[:, _lanes(0, c)].astype(F32))
            continue
        for r1 in range(4):
            if dil == 4:
                piece = src_ref[:, _lanes(r1, c)].astype(F32)
            else:
                for r2 in range(4):
                    tmp[c, pl.ds(r1 * n4 + r2, n16, stride=4), :] = src_ref[:, _lanes(4 * r2 + r1, c)].astype(F32)
                piece = tmp[c, r1 * n4:(r1 + 1) * n4, :]
            put((c, pl.ds(r1, n4, stride=4), slice(None)), piece)


def _residue_spec(tm, dil):
    return pl.BlockSpec((tm // dil, dil * ATTN_W), lambda i: (i, 0))


def _residue_shape(seq, dil, dtype):
    return jax.ShapeDtypeStruct((seq // dil, dil * ATTN_W), dtype)


def _fwd_in(x, g_pre, w_in_g, cos_t, sin_t, tm=512):
    seq = x.shape[0]
    n_dil = len(DILATIONS)

    def body(x_ref, g_ref, w_ref, cos_ref, sin_ref, *rest):
        qkv_refs, (zr_ref, ht_ref, qkv_scr, tmp) = rest[:3 * n_dil], rest[3 * n_dil:]
        xv = x_ref[...]
        r = lax.rsqrt(jnp.mean(xv * xv, axis=-1, keepdims=True) + NORM_EPS)
        hf = (xv * r) * g_ref[...]
        ht_ref[...] = hf.T.astype(BF16)
        h = hf.astype(BF16)
        cos, sin = cos_ref[...], sin_ref[...]
        first_half = (lax.broadcasted_iota(jnp.int32, (tm, BLK), 1) & 32) == 0

        def rope(t):
            return t * cos + _swap_halves(t, first_half) * sin

        def project(j):
            return jnp.dot(h, w_ref[j], preferred_element_type=F32)

        def place(j, zj):
            for n in range(SHARD_IN // BLK):
                chunk, t = j * (SHARD_IN // BLK) + n, zj[:, n * BLK:(n + 1) * BLK]
                if chunk < N_CHUNK:
                    qkv_scr[chunk] = rope(t) * HEAD_DIM ** -0.5
                elif chunk < 2 * N_CHUNK:
                    qkv_scr[chunk] = rope(t)
                elif chunk < 3 * N_CHUNK:
                    qkv_scr[chunk] = t
                else:
                    zr_ref[:, (chunk - 3 * N_CHUNK) * BLK:(chunk - 3 * N_CHUNK + 1) * BLK] = t.astype(BF16)

        ahead = project(0)
        for j in range(N_DEV):
            zj = ahead
            if j + 1 < N_DEV:
                ahead = project(j + 1)
            place(j, zj)
            for a in range(3):
                if (a + 1) * N_CHUNK - 1 in range(j * (SHARD_IN // BLK), (j + 1) * (SHARD_IN // BLK)):
                    _to_residues(qkv_scr, a * N_CHUNK, [qkv_refs[3 * n + a] for n in range(n_dil)], tmp, tm, BF16)

    row = lambda w: pl.BlockSpec((tm, w), lambda i: (i, 0))
    outs = pl.pallas_call(
        body, name="fwd_in", grid=(seq // tm,),
        in_specs=[row(D_MODEL), _resident((1, D_MODEL)), _resident((N_DEV, D_MODEL, SHARD_IN)), row(BLK), row(BLK)],
        out_specs=[_residue_spec(tm, dil) for dil in DILATIONS for _ in range(3)]
        + [row(REST_W), pl.BlockSpec((D_MODEL, tm), lambda i: (0, i))],
        out_shape=[_residue_shape(seq, dil, BF16) for dil in DILATIONS for _ in range(3)]
        + [jax.ShapeDtypeStruct((seq, REST_W), BF16), jax.ShapeDtypeStruct((D_MODEL, seq), BF16)],
        scratch_shapes=[pltpu.VMEM((3 * N_CHUNK, tm, BLK), F32), pltpu.VMEM((N_CHUNK, tm, BLK), F32)],
        compiler_params=_params(1),
    )(x, g_pre.reshape(1, D_MODEL), w_in_g, cos_t, sin_t)
    qkv = [tuple(outs[3 * n:3 * n + 3]) for n in range(n_dil)]
    return qkv, outs[3 * n_dil], outs[3 * n_dil + 1]


def _band_bias(first_block):
    kj = lax.broadcasted_iota(jnp.int32, (2 * BLK, BLK), 0)
    qi = lax.broadcasted_iota(jnp.int32, (2 * BLK, BLK), 1)
    valid = (kj >= qi) & (kj <= qi + BLK)
    bias = jnp.where(valid, 0.0, NEG).astype(BF16)
    bias_first = jnp.where(valid & (kj >= BLK), 0.0, NEG).astype(BF16)
    onehot = ((kj & (BLK - 1)) == qi).astype(F32).astype(BF16)
    return onehot, bias, jnp.where(first_block, bias_first, bias)


def _stack_heads(t, head0):
    del head0
    keep0 = (lax.broadcasted_iota(jnp.int32, t.shape, 1) < HEAD_DIM).astype(F32).astype(BF16)
    return jnp.concatenate([t * keep0, t * (1 - keep0)], axis=0)


def _unstack_heads(t2, head0):
    return jnp.where(head0, t2[:BLK], t2[BLK:])


def _rows_per_head(a, head0):
    b = pltpu.roll(a, HEAD_DIM, 1)
    rows = jnp.concatenate([jnp.where(head0, a, b), jnp.where(head0, b, a)], axis=0)
    return jnp.concatenate([rows, rows], axis=1)


BLOCKS_PER_STEP = 32


def _attn_specs(length, dil):
    n_blocks = length // BLK
    tb = min(BLOCKS_PER_STEP, n_blocks)
    nc = BLOCKS_PER_STEP // tb
    assert (dil * N_CHUNK) % nc == 0 and n_blocks % tb == 0
    tile = pl.BlockSpec((tb * BLK, nc * BLK), lambda c, t: (t, c))
    prev = pl.BlockSpec((BLK, nc * BLK), lambda c, t: (jnp.maximum(t * tb - 1, 0), c))
    grid = (dil * N_CHUNK // nc, n_blocks // tb)
    return tb, nc, tile, prev, grid


def _load_keys(cat, prev_ref, cur_ref):
    cat[0:BLK] = prev_ref[...]
    cat[BLK:] = cur_ref[...]


def _attn_fwd(q, k, v, dil):
    length = q.shape[0]
    tb, nc, tile, prev, grid = _attn_specs(length, dil)

    def body(q_ref, kc_ref, kp_ref, vc_ref, vp_ref, o_ref, lse_ref, kcat, vcat):
        _load_keys(kcat, kp_ref, kc_ref)
        _load_keys(vcat, vp_ref, vc_ref)
        head0 = lax.broadcasted_iota(jnp.int32, (BLK, BLK), 1) < HEAD_DIM
        onehot, bias, bias_start = _band_bias(pl.program_id(1) == 0)
        ones = jnp.ones((2 * BLK, BLK), BF16)
        def scores(c, j):
            rows, cols = slice(j * BLK, (j + 1) * BLK), slice(c * BLK, (c + 1) * BLK)
            q2 = jnp.concatenate([_stack_heads(q_ref[rows, cols], head0), onehot], axis=1)
            kk = jnp.concatenate([kcat[j * BLK:(j + 2) * BLK, cols], bias_start if j == 0 else bias], axis=1)
            return (lax.dot_general(q2, kk, (((1,), (1,)), ((), ())), preferred_element_type=F32),)

        def probabilities(c, j, s):
            m = jnp.max(s, axis=1, keepdims=True)
            return m, jnp.exp(s - m).astype(BF16)

        def outputs(c, j, m, p):
            rows, cols = slice(j * BLK, (j + 1) * BLK), slice(c * BLK, (c + 1) * BLK)
            vv = jnp.concatenate([vcat[j * BLK:(j + 2) * BLK, cols], ones], axis=1)
            pv = jnp.dot(p, vv, preferred_element_type=F32)
            den = pv[:, BLK:]
            o_ref[rows, cols] = _unstack_heads(pv[:, :BLK] / den, head0).astype(BF16)
            lse_ref[rows, cols] = _unstack_heads(m + jnp.log(den), head0)

        units = [(c, j) for c in range(nc) for j in range(tb)]
        stage1, stage2 = {}, {}
        for n in range(len(units) + 2):
            if n < len(units):
                stage1[n] = scores(*units[n])
            if 0 <= n - 1 < len(units):
                stage2[n - 1] = probabilities(*units[n - 1], *stage1.pop(n - 1))
            if 0 <= n - 2 < len(units):
                outputs(*units[n - 2], *stage2.pop(n - 2))

    return pl.pallas_call(
        body, name=f"attn_fwd_d{dil}", grid=grid,
        in_specs=[tile, tile, prev, tile, prev], out_specs=[tile, tile],
        out_shape=[jax.ShapeDtypeStruct(q.shape, BF16), jax.ShapeDtypeStruct(q.shape, F32)],
        scratch_shapes=[pltpu.VMEM(((tb + 1) * BLK, nc * BLK), BF16)] * 2,
        compiler_params=_params(2),
    )(q, k, k, v, v)


def _attn_bwd(q, k, v, do, lse, delta, dil):
    length = q.shape[0]
    tb, nc, tile, prev, grid = _attn_specs(length, dil)
    whole = pl.BlockSpec((length, nc * BLK), lambda c, t: (0, c))

    def body(q_ref, do_ref, lse_ref, dl_ref, kc_ref, kp_ref, vc_ref, vp_ref, dq_ref, dk_ref, dv_ref, kcat, vcat):
        t = pl.program_id(1)
        _load_keys(kcat, kp_ref, kc_ref)
        _load_keys(vcat, vp_ref, vc_ref)
        head0 = lax.broadcasted_iota(jnp.int32, (BLK, BLK), 1) < HEAD_DIM
        onehot, bias, bias_start = _band_bias(t == 0)

        def scores(c, j):
            rows, cols = slice(j * BLK, (j + 1) * BLK), slice(c * BLK, (c + 1) * BLK)
            q2 = _stack_heads(q_ref[rows, cols], head0)
            do2 = _stack_heads(do_ref[rows, cols], head0)
            kk = kcat[j * BLK:(j + 2) * BLK, cols]
            s = lax.dot_general(jnp.concatenate([q2, onehot], axis=1),
                                jnp.concatenate([kk, bias_start if j == 0 else bias], axis=1),
                                (((1,), (1,)), ((), ())), preferred_element_type=F32)
            dp = lax.dot_general(do2, vcat[j * BLK:(j + 2) * BLK, cols], (((1,), (1,)), ((), ())),
                                 preferred_element_type=F32)
            return q2, do2, kk, s, dp

        def probabilities(c, j, q2, do2, kk, s, dp):
            rows, cols = slice(j * BLK, (j + 1) * BLK), slice(c * BLK, (c + 1) * BLK)
            p = jnp.exp(s - _rows_per_head(lse_ref[rows, cols], head0))
            ds = (p * (dp - _rows_per_head(dl_ref[rows, cols].astype(F32), head0))).astype(BF16)
            return q2, do2, kk, p.astype(BF16), ds

        def gradients(c, j, q2, do2, kk, p, ds):
            rows, cols = slice(j * BLK, (j + 1) * BLK), slice(c * BLK, (c + 1) * BLK)
            dq2 = jnp.dot(ds, kk, preferred_element_type=F32)
            dq_ref[rows, cols] = (_unstack_heads(dq2, head0) * HEAD_DIM ** -0.5).astype(BF16)
            dk2 = lax.dot_general(ds, q2, (((0,), (0,)), ((), ())), preferred_element_type=F32)
            dv2 = lax.dot_general(p, do2, (((0,), (0,)), ((), ())), preferred_element_type=F32)
            own = pl.ds(pl.multiple_of((t * tb + j) * BLK, BLK), BLK)
            dk_ref[own, cols] = dk2[BLK:].astype(BF16)
            dv_ref[own, cols] = dv2[BLK:].astype(BF16)

            def add_to_previous():
                before = pl.ds(pl.multiple_of((t * tb + j - 1) * BLK, BLK), BLK)
                dk_ref[before, cols] = (dk_ref[before, cols].astype(F32) + dk2[:BLK]).astype(BF16)
                dv_ref[before, cols] = (dv_ref[before, cols].astype(F32) + dv2[:BLK]).astype(BF16)

            if j == 0:
                pl.when(t > 0)(add_to_previous)
            else:
                add_to_previous()

        units = [(c, j) for c in range(nc) for j in range(tb)]
        stage1 = {0: scores(*units[0])}
        for n in range(len(units)):
            stage2 = probabilities(*units[n], *stage1.pop(n))
            if n + 1 < len(units):
                stage1[n + 1] = scores(*units[n + 1])
            gradients(*units[n], *stage2)

    return pl.pallas_call(
        body, name=f"attn_bwd_d{dil}", grid=grid,
        in_specs=[tile, tile, tile, tile, tile, prev, tile, prev], out_specs=[tile, whole, whole],
        out_shape=[jax.ShapeDtypeStruct(q.shape, BF16)] * 3,
        scratch_shapes=[pltpu.VMEM(((tb + 1) * BLK, nc * BLK), BF16)] * 2,
        compiler_params=_params(2),
    )(q, do, lse, delta, k, k, v, v)


HALO = 16


def _halo_specs(tm, seq):
    before = lambda w: pl.BlockSpec((HALO, w), lambda i: (jnp.maximum(i * (tm // HALO) - 1, 0), 0))
    after = lambda w: pl.BlockSpec((HALO, w), lambda i: (jnp.minimum((i + 1) * (tm // HALO), seq // HALO - 1), 0))
    return before, after


def _conv_taps(u, before, tm):
    row = lax.broadcasted_iota(jnp.int32, u.shape, 0)
    last, last2 = before[HALO - 1:HALO, :], before[HALO - 2:HALO - 1, :]
    u1 = jnp.where(row == 0, last, pltpu.roll(u, 1, 0))
    u2 = jnp.where(row == 0, last2, jnp.where(row == 1, last, pltpu.roll(u, 2, 0)))
    return u1, u2


def _attn_combine(o_parts, lse_parts, zr, conv_w, tm=256):
    seq = zr.shape[0]
    a0, h0, b0, c0, g0 = 0, ATTN_W, ATTN_W + CONV_W, ATTN_W + 2 * CONV_W, ATTN_W + 3 * CONV_W

    def body(o1, o2, o3, l1, l2, l3, zr_ref, zp_ref, w_ref, mixed_ref, o_ref, lse1, lse2, lse3, *scr):
        i = pl.program_id(0)
        for src, dst, dil in zip((o2, o3, l2, l3), scr[:4], DILATIONS[1:] * 2):
            _from_residue(src, dst, dil, tm, accumulate=False, tmp=scr[5])
        for c in range(N_CHUNK):
            cols = slice(c * BLK, (c + 1) * BLK)
            la, lb, lc = l1[:, cols], scr[2][c], scr[3][c]
            top = jnp.maximum(jnp.maximum(la, lb), lc)
            ea, eb, ec = jnp.exp(la - top), jnp.exp(lb - top), jnp.exp(lc - top)
            den = ea + eb + ec
            o = (ea / den) * o1[:, cols].astype(F32) + (eb / den) * scr[0][c] + (ec / den) * scr[1][c]
            o_ref[:, cols] = o.astype(BF16)
            scr[4][c] = top + jnp.log(den)
            ga = zr_ref[:, cols].astype(F32)
            mixed_ref[:, cols] = (o * (ga * _sigmoid(ga))).astype(BF16)
        _to_residues(scr[4], 0, (lse1, lse2, lse3), scr[5], tm, F32)
        part = lambda ref, lo, hi: ref[:, lo:hi].astype(F32)
        u = part(zr_ref, c0, g0) * part(zr_ref, h0, b0)
        before = jnp.where(i > 0, part(zp_ref, c0, g0) * part(zp_ref, h0, b0), 0.0)
        u1, u2 = _conv_taps(u, before, tm)
        y = u2 * w_ref[0:1, :] + u1 * w_ref[1:2, :] + u * w_ref[2:3, :]
        gc = part(zr_ref, g0, REST_W)
        mixed_ref[:, ATTN_W:] = ((part(zr_ref, b0, c0) * y) * (gc * _sigmoid(gc))).astype(BF16)

    row = lambda w: pl.BlockSpec((tm, w), lambda i: (i, 0))
    before, _ = _halo_specs(tm, seq)
    views = [_residue_spec(tm, dil) for dil in DILATIONS]
    outs = pl.pallas_call(
        body, name="attn_combine", grid=(seq // tm,),
        in_specs=views * 2 + [row(REST_W), before(REST_W), _resident((3, CONV_W))],
        out_specs=[row(D_MODEL), row(ATTN_W)] + views,
        out_shape=[jax.ShapeDtypeStruct((seq, D_MODEL), BF16), jax.ShapeDtypeStruct((seq, ATTN_W), BF16)]
        + [_residue_shape(seq, dil, F32) for dil in DILATIONS],
        scratch_shapes=[pltpu.VMEM((N_CHUNK, tm, BLK), F32)] * 6,
        compiler_params=_params(1),
    )(*o_parts, *lse_parts, zr, zr, conv_w)
    return outs[0], outs[1], outs[2:]


def _out_loss_bwd(mixed, w_out_g, x, target, g_post, tm=512):
    seq = x.shape[0]

    def body(mx_ref, w_ref, x_ref, t_ref, g_ref, dout_ref, dmx_ref, dw_ref, dwb_ref, st_ref):
        i = pl.program_id(0)
        mx = mx_ref[...]
        y = jnp.dot(mx, w_ref[...], preferred_element_type=F32)
        r = lax.rsqrt(jnp.mean(y * y, axis=-1, keepdims=True) + NORM_EPS)
        yhat = y * r
        g = g_ref[...]
        err = (x_ref[...] + yhat * g) - t_ref[...]
        dn = err * (1.0 / D_MODEL)
        dout_ref[...] = dn
        tg = dn * g
        dy = (r * (tg - yhat * jnp.mean(tg * yhat, axis=-1, keepdims=True))).astype(BF16)
        dmx_ref[...] = lax.dot_general(dy, w_ref[...], (((1,), (1,)), ((), ())),
                                       preferred_element_type=F32).astype(BF16)
        dw = lax.dot_general(mx, dy, (((0,), (0,)), ((), ())), preferred_element_type=F32)
        gsum = jnp.sum(dn * yhat, axis=0, keepdims=True)
        lsum = jnp.broadcast_to(0.5 / D_MODEL * jnp.sum(err * err), (1, D_MODEL))

        @pl.when(i == 0)
        def _():
            dw_ref[...] = dw
            st_ref[...] = jnp.zeros_like(st_ref)
            st_ref[0:1, :] = gsum
            st_ref[1:2, :] = lsum

        @pl.when(i > 0)
        def _():
            dw_ref[...] += dw
            st_ref[0:1, :] += gsum
            st_ref[1:2, :] += lsum

        @pl.when(i == seq // tm - 1)
        def _():
            dwb_ref[...] = dw_ref[...].astype(BF16)

    row = lambda w: pl.BlockSpec((tm, w), lambda i: (i, 0))
    whole = pl.BlockSpec((D_MODEL, D_MODEL), lambda i: (0, 0))
    return pl.pallas_call(
        body, name="out_loss_bwd", grid=(seq // tm,),
        in_specs=[row(D_MODEL), _resident((D_MODEL, D_MODEL)), row(D_MODEL), row(D_MODEL), _resident((1, D_MODEL))],
        out_specs=[row(D_MODEL), row(D_MODEL), whole, whole, pl.BlockSpec((8, D_MODEL), lambda i: (0, 0))],
        out_shape=[jax.ShapeDtypeStruct((seq, D_MODEL), F32), jax.ShapeDtypeStruct((seq, D_MODEL), BF16),
                   jax.ShapeDtypeStruct((D_MODEL, D_MODEL), F32), jax.ShapeDtypeStruct((D_MODEL, D_MODEL), BF16),
                   jax.ShapeDtypeStruct((8, D_MODEL), F32)],
        compiler_params=_params(1),
    )(mixed, w_out_g, x, target, g_post.reshape(1, D_MODEL))


def _head_sum(prod, same_head):
    hi = prod.astype(BF16)
    lo = (prod - hi.astype(F32)).astype(BF16)
    return (jnp.dot(hi, same_head, preferred_element_type=F32) + jnp.dot(lo, same_head, preferred_element_type=F32))


def _gate_bwd(dmixed, zr, o, conv_w, tm=256):
    seq = zr.shape[0]
    n_tiles = seq // tm
    n_dil = len(DILATIONS)
    a0, h0, b0, c0, g0 = 0, ATTN_W, ATTN_W + CONV_W, ATTN_W + 2 * CONV_W, ATTN_W + 3 * CONV_W

    def body(dm_ref, dmn_ref, zr_ref, zp_ref, zn_ref, o_ref, w_ref, *rest):
        do_refs, dl_refs = rest[:n_dil], rest[n_dil:2 * n_dil]
        dz_ref, dw_ref, do_scr, dl_scr, tmp = rest[2 * n_dil:]
        i = pl.program_id(0)
        part = lambda ref, lo, hi: ref[:, lo:hi].astype(F32)
        ga = part(zr_ref, a0, h0)
        sg = _sigmoid(ga)
        dattn = part(dm_ref, 0, ATTN_W)
        ov = o_ref[...].astype(F32)
        do = dattn * (ga * sg)
        dz_ref[:, a0:h0] = (dattn * ov * (sg * (1.0 + ga * (1.0 - sg)))).astype(BF16)
        li = lax.broadcasted_iota(jnp.int32, (BLK, BLK), 0) // HEAD_DIM
        lj = lax.broadcasted_iota(jnp.int32, (BLK, BLK), 1) // HEAD_DIM
        same_head = (li == lj).astype(BF16)
        prod = do * ov
        for c in range(N_CHUNK):
            cols = slice(c * BLK, (c + 1) * BLK)
            do_scr[c] = do[:, cols]
            dl_scr[c] = _head_sum(prod[:, cols], same_head)
        _to_residues(do_scr, 0, do_refs, tmp, tm, BF16)
        _to_residues(dl_scr, 0, dl_refs, tmp, tm, BF16)

        ch, cb, cc, gc = (part(zr_ref, lo, hi) for lo, hi in ((h0, b0), (b0, c0), (c0, g0), (g0, REST_W)))
        u = cc * ch
        before = jnp.where(i > 0, part(zp_ref, c0, g0) * part(zp_ref, h0, b0), 0.0)
        u1, u2 = _conv_taps(u, before, tm)
        w0, w1, w2 = w_ref[0:1, :], w_ref[1:2, :], w_ref[2:3, :]
        y = u2 * w0 + u1 * w1 + u * w2
        sc = _sigmoid(gc)
        silu_c = gc * sc
        dconv = part(dm_ref, ATTN_W, D_MODEL)
        dz_ref[:, b0:c0] = (dconv * y * silu_c).astype(BF16)
        dz_ref[:, g0:] = (dconv * (cb * y) * (sc * (1.0 + gc * (1.0 - sc)))).astype(BF16)
        dy = dconv * cb * silu_c
        gn = part(zn_ref, g0, REST_W)
        after = jnp.where(i < n_tiles - 1,
                          part(dmn_ref, ATTN_W, D_MODEL) * part(zn_ref, b0, c0) * (gn * _sigmoid(gn)), 0.0)
        row = lax.broadcasted_iota(jnp.int32, dy.shape, 0)
        nxt, nxt2 = after[0:1, :], after[1:2, :]
        dy1 = jnp.where(row == tm - 1, nxt, pltpu.roll(dy, tm - 1, 0))
        dy2 = jnp.where(row == tm - 1, nxt2, jnp.where(row == tm - 2, nxt, pltpu.roll(dy, tm - 2, 0)))
        du = dy * w2 + dy1 * w1 + dy2 * w0
        dz_ref[:, c0:g0] = (du * ch).astype(BF16)
        dz_ref[:, h0:b0] = (du * cc).astype(BF16)
        dws = [jnp.sum(dy * u2, axis=0, keepdims=True), jnp.sum(dy * u1, axis=0, keepdims=True),
               jnp.sum(dy * u, axis=0, keepdims=True)]

        @pl.when(i == 0)
        def _():
            dw_ref[...] = jnp.zeros_like(dw_ref)

        for n, part in enumerate(dws):
            dw_ref[n:n + 1, :] += part

    row_spec = lambda w: pl.BlockSpec((tm, w), lambda i: (i, 0))
    before, after = _halo_specs(tm, seq)
    views = [_residue_spec(tm, dil) for dil in DILATIONS]
    outs = pl.pallas_call(
        body, name="gate_bwd", grid=(n_tiles,),
        in_specs=[row_spec(D_MODEL), after(D_MODEL), row_spec(REST_W), before(REST_W), after(REST_W),
                  row_spec(ATTN_W), _resident((3, CONV_W))],
        out_specs=views * 2 + [row_spec(REST_W), pl.BlockSpec((8, CONV_W), lambda i: (0, 0))],
        out_shape=[_residue_shape(seq, dil, BF16) for dil in DILATIONS] * 2
        + [jax.ShapeDtypeStruct((seq, REST_W), BF16), jax.ShapeDtypeStruct((8, CONV_W), F32)],
        scratch_shapes=[pltpu.VMEM((N_CHUNK, tm, BLK), F32)] * 3,
        compiler_params=_params(1),
    )(dmixed, dmixed, zr, zr, zr, o, conv_w)
    return outs[:n_dil], outs[n_dil:2 * n_dil], outs[2 * n_dil], outs[2 * n_dil + 1]


def _in_bwd(dqs, dks, dvs, dzr, cos_t, sin_t, x, d_out, g_pre, w_in_g, tm=256):
    seq = x.shape[0]

    def body(q1, q2, q3, k1, k2, k3, v1, v2, v3, dzr_ref, cos_ref, sin_ref, x_ref, dout_ref, g_ref, w_ref,
             dz_ref, gx_ref, st_ref, dq_scr, dk_scr, dv_scr, tmp):
        i = pl.program_id(0)
        cos, sin = cos_ref[...], sin_ref[...]
        first_half = (lax.broadcasted_iota(jnp.int32, (tm, BLK), 1) & 32) == 0
        streams = [((q1, q2, q3), dq_scr), ((k1, k2, k3), dk_scr), ((v1, v2, v3), dv_scr)]
        per_slab = SHARD_IN // BLK

        def unrope(t):
            return t * cos - _swap_halves(t, first_half) * sin

        def assemble(j):
            for chunk in range(j * per_slab, (j + 1) * per_slab):
                a, c = divmod(chunk, N_CHUNK)
                if a < 3 and c == 0:
                    parts, total = streams[a]
                    for n, dil in enumerate(DILATIONS):
                        _from_residue(parts[n], total, dil, tm, accumulate=n > 0, tmp=tmp)
                if a < 2:
                    val = unrope(streams[a][1][c]).astype(BF16)
                elif a == 2:
                    val = dv_scr[c].astype(BF16)
                else:
                    val = dzr_ref[:, (chunk - 3 * N_CHUNK) * BLK:(chunk - 3 * N_CHUNK + 1) * BLK]
                dz_ref[:, chunk * BLK:(chunk + 1) * BLK] = val
            return dz_ref[:, j * SHARD_IN:(j + 1) * SHARD_IN]

        ahead = assemble(0)
        dh = None
        for j in range(N_DEV):
            slab = ahead
            if j + 1 < N_DEV:
                ahead = assemble(j + 1)
            part = lax.dot_general(slab, w_ref[j], (((1,), (1,)), ((), ())), preferred_element_type=F32)
            dh = part if dh is None else dh + part
        xv = x_ref[...]
        r = lax.rsqrt(jnp.mean(xv * xv, axis=-1, keepdims=True) + NORM_EPS)
        xhat = xv * r
        tg = dh * g_ref[...]
        gx_ref[...] = dout_ref[...] + r * (tg - xhat * jnp.mean(tg * xhat, axis=-1, keepdims=True))
        gsum = jnp.sum(dh * xhat, axis=0, keepdims=True)

        @pl.when(i == 0)
        def _():
            st_ref[...] = jnp.zeros_like(st_ref)

        st_ref[0:1, :] += gsum

    row = lambda w: pl.BlockSpec((tm, w), lambda i: (i, 0))
    return pl.pallas_call(
        body, name="in_bwd", grid=(seq // tm,),
        in_specs=[_residue_spec(tm, dil) for dil in DILATIONS] * 3
        + [row(REST_W), row(BLK), row(BLK), row(D_MODEL), row(D_MODEL), _resident((1, D_MODEL)),
           _resident((N_DEV, D_MODEL, SHARD_IN))],
        out_specs=[row(IN_W), row(D_MODEL), pl.BlockSpec((8, D_MODEL), lambda i: (0, 0))],
        out_shape=[jax.ShapeDtypeStruct((seq, IN_W), BF16), jax.ShapeDtypeStruct((seq, D_MODEL), F32),
                   jax.ShapeDtypeStruct((8, D_MODEL), F32)],
        scratch_shapes=[pltpu.VMEM((N_CHUNK, tm, BLK), F32)] * 4,
        compiler_params=_params(1),
    )(*dqs, *dks, *dvs, dzr, cos_t, sin_t, x, d_out, g_pre.reshape(1, D_MODEL), w_in_g)


def _local_step(x, target, g_pre, g_post, w_in_g, w_out_g, conv_w):
    seq = x.shape[0]
    cos_t, sin_t = _rope_tables(seq)
    qkv, zr, ht = _fwd_in(x, g_pre, w_in_g, cos_t, sin_t)
    parts = [_attn_fwd(*qkv[n], dil) for n, dil in enumerate(DILATIONS)]
    mixed, o, lse = _attn_combine([p[0] for p in parts], [p[1] for p in parts], zr, conv_w)
    d_out, dmixed, dw_out, dw_out_bf, st_post = _out_loss_bwd(mixed, w_out_g, x, target, g_post)
    do, delta, dzr, dconv = _gate_bwd(dmixed, zr, o, conv_w)
    grads = [_attn_bwd(*qkv[n], do[n], lse[n], delta[n], dil) for n, dil in enumerate(DILATIONS)]
    dz, grad_x, st_pre = _in_bwd([g[0] for g in grads], [g[1] for g in grads], [g[2] for g in grads], dzr,
                                 cos_t, sin_t, x, d_out, g_pre, w_in_g)
    conv_rows = jnp.pad(dconv[0:3], ((0, 0), (0, D_MODEL - CONV_W)))
    small = jnp.concatenate([st_pre[0:1], st_post[0:2], conv_rows, jnp.zeros((2, D_MODEL), F32)], axis=0)
    return grad_x, ht, dz, dw_out, dw_out_bf, small


def _coords():
    return lax.axis_index("x"), lax.axis_index("y"), lax.axis_index("c")


def _peer(k):
    x, y, c = _coords()
    px = 1 - x if k & 4 else x
    py = 1 - y if k & 2 else y
    pc = 1 - c if k & 1 else c
    return (px, py, pc), 4 * px + 2 * py + pc


HBM_SPEC = pl.BlockSpec(memory_space=pltpu.HBM)
VMEM_SPEC = pl.BlockSpec(memory_space=pltpu.VMEM)


def _ag_weights(w_in, w_out, conv_w):
    def body(win_ref, wout_ref, cw_ref, gin_ref, gout_ref, gcw_ref, win_bf, wout_bf, cw_pad, send_sems, recv_sems,
             local_sems):
        x, y, c = _coords()
        me, sibling = (x, y, c), (x, y, 1 - c)
        chips = [(1 - x, y), (x, 1 - y), (1 - x, 1 - y)]
        slab = lambda px, py, pc: 4 * px + 2 * py + pc
        win_bf[...] = win_ref[...].astype(BF16)
        wout_bf[...] = wout_ref[...].astype(BF16)
        cw_pad[...] = jnp.zeros_like(cw_pad)
        cw_pad[0:3, 0:CONV_W // N_DEV] = cw_ref[...]
        mine = [win_bf, wout_bf, cw_pad]
        gathered = [gin_ref, gout_ref, gcw_ref]

        def copies(k, block, to, own=False):
            return [pltpu.make_async_remote_copy(src_ref=mine[a] if own else gathered[a].at[slab(*block)],
                                                 dst_ref=gathered[a].at[slab(*block)], send_sem=send_sems.at[k, a],
                                                 recv_sem=recv_sems.at[k, a], device_id=to, device_id_type=MESH)
                    for a in range(3)]

        local = [pltpu.make_async_copy(mine[a], gathered[a].at[slab(*me)], local_sems.at[a]) for a in range(3)]
        for cp in local:
            cp.start()
        first = copies(0, me, sibling, own=True)
        for j, chip in enumerate(chips):
            first += copies(1 + j, me, (*chip, c), own=True)
        for cp in first:
            cp.start()
        passed = []
        for j, chip in enumerate(chips):
            for cp in copies(1 + j, (*chip, c), me):
                cp.wait_recv()
            onward = copies(4 + j, (*chip, c), sibling)
            for cp in onward:
                cp.start()
            passed += onward
        for cp in copies(0, sibling, me):
            cp.wait_recv()
        for j, chip in enumerate(chips):
            for cp in copies(4 + j, (*chip, 1 - c), me):
                cp.wait_recv()
        for cp in first + passed:
            cp.wait_send()
        for cp in local:
            cp.wait()

    return pl.pallas_call(
        body, name="ag_weights",
        in_specs=[VMEM_SPEC, VMEM_SPEC, VMEM_SPEC], out_specs=[HBM_SPEC, HBM_SPEC, HBM_SPEC],
        out_shape=[jax.ShapeDtypeStruct((N_DEV, D_MODEL, SHARD_IN), BF16),
                   jax.ShapeDtypeStruct((N_DEV, SHARD_OUT, D_MODEL), BF16),
                   jax.ShapeDtypeStruct((N_DEV, 8, BLK), F32)],
        scratch_shapes=[pltpu.VMEM((D_MODEL, SHARD_IN), BF16), pltpu.VMEM((SHARD_OUT, D_MODEL), BF16),
                        pltpu.VMEM((8, BLK), F32), pltpu.SemaphoreType.DMA((N_DEV - 1, 3)),
                        pltpu.SemaphoreType.DMA((N_DEV - 1, 3)), pltpu.SemaphoreType.DMA((3,))],
        compiler_params=pltpu.CompilerParams(vmem_limit_bytes=VMEM_LIMIT),
    )(w_in, w_out, conv_w)


def _dw_in_rs(ht, dz, dw_out, small):
    seq = dz.shape[0]

    def body(cols_ref, ht_ref, dz_ref, dout_ref, sm_ref, own_ref, rin_ref, rout_ref, rsm_ref, to_sibling, landed,
             to_chip, zero_buf, d2d_send, d2d_recv, ici_send, ici_recv, side_send, side_recv, local_sems):
        del cols_ref
        step = pl.program_id(0)
        x, y, c = _coords()
        me = 4 * x + 2 * y + c
        sibling = (x, y, 1 - c)
        chips = [(1 - x, y), (x, 1 - y), (1 - x, 1 - y)]

        def d2d(n):
            return pltpu.make_async_remote_copy(src_ref=to_sibling.at[n], dst_ref=landed.at[n], send_sem=d2d_send.at[n],
                                                recv_sem=d2d_recv.at[n], device_id=sibling, device_id_type=MESH)

        def ici(n):
            return pltpu.make_async_remote_copy(src_ref=to_chip.at[n], dst_ref=rin_ref.at[n], send_sem=ici_send.at[n],
                                                recv_sem=ici_recv.at[n], device_id=(*chips[n], c), device_id_type=MESH)

        def side(k, mine):
            peer, peer_idx = _peer(k)
            src_slab, dst_slab = (peer_idx, me) if mine else (me, peer_idx)
            pairs = [(dout_ref.at[src_slab], rout_ref.at[dst_slab]), (sm_ref, rsm_ref.at[dst_slab])]
            return [pltpu.make_async_remote_copy(src_ref=src, dst_ref=dst, send_sem=side_send.at[k - 1, a],
                                                 recv_sem=side_recv.at[k - 1, a], device_id=peer, device_id_type=MESH)
                    for a, (src, dst) in enumerate(pairs)]

        local = [pltpu.make_async_copy(zero_buf, rout_ref.at[me], local_sems.at[0]),
                 pltpu.make_async_copy(sm_ref, rsm_ref.at[me], local_sems.at[1])]

        @pl.when(step == 0)
        def _():
            zero_buf[...] = jnp.zeros_like(zero_buf)
            for cp in local:
                cp.start()
            for k in range(1, N_DEV):
                for cp in side(k, mine=True):
                    cp.start()

        dw = jnp.dot(ht_ref[...], dz_ref[...], preferred_element_type=F32)
        for n in range(4):
            @pl.when(step == n)
            def _(n=n):
                to_sibling[n] = dw.astype(BF16)
                d2d(n).start()

        for n in range(3):
            @pl.when(step == 4 + n)
            def _(n=n):
                d2d(n).wait_recv()
                to_chip[n] = (dw + landed[n].astype(F32)).astype(BF16)
                ici(n).start()

        @pl.when(step == N_DEV - 1)
        def _():
            d2d(3).wait_recv()
            own_ref[...] = dw + landed[3].astype(F32)
            for n in range(3):
                ici(n).wait_recv()
            for k in range(1, N_DEV):
                for cp in side(k, mine=False):
                    cp.wait_recv()
            for n in range(4):
                d2d(n).wait_send()
            for n in range(3):
                ici(n).wait_send()
            for k in range(1, N_DEV):
                for cp in side(k, mine=True):
                    cp.wait_send()
            for cp in local:
                cp.wait()

    x, y, c = _coords()
    chip_order = [(1 - x, y), (x, 1 - y), (1 - x, 1 - y), (x, y)]
    cols = jnp.stack([4 * px + 2 * py + pc for pc in (1 - c, c) for px, py in chip_order]).astype(jnp.int32)
    slab = (D_MODEL, SHARD_IN)
    grid_spec = pltpu.PrefetchScalarGridSpec(
        num_scalar_prefetch=1, grid=(N_DEV,),
        in_specs=[pl.BlockSpec((D_MODEL, seq), lambda s, cols: (0, 0), pipeline_mode=pl.Buffered(1)),
                  pl.BlockSpec((seq, SHARD_IN), lambda s, cols: (0, cols[s])), HBM_SPEC, HBM_SPEC],
        out_specs=[pl.BlockSpec(slab, lambda s, cols: (0, 0)), HBM_SPEC, HBM_SPEC, HBM_SPEC],
        scratch_shapes=[pltpu.VMEM((4, *slab), BF16), pltpu.VMEM((4, *slab), BF16), pltpu.VMEM((3, *slab), BF16),
                        pltpu.VMEM((SHARD_OUT, D_MODEL), BF16),
                        pltpu.SemaphoreType.DMA((4,)), pltpu.SemaphoreType.DMA((4,)),
                        pltpu.SemaphoreType.DMA((3,)), pltpu.SemaphoreType.DMA((3,)),
                        pltpu.SemaphoreType.DMA((N_DEV - 1, 2)), pltpu.SemaphoreType.DMA((N_DEV - 1, 2)),
                        pltpu.SemaphoreType.DMA((2,))])
    return pl.pallas_call(
        body, name="dw_in_rs", grid_spec=grid_spec,
        out_shape=[jax.ShapeDtypeStruct(slab, F32),
                   jax.ShapeDtypeStruct((3, *slab), BF16),
                   jax.ShapeDtypeStruct((N_DEV, SHARD_OUT, D_MODEL), BF16),
                   jax.ShapeDtypeStruct((N_DEV, 8, D_MODEL), F32)],
        compiler_params=_params(1),
    )(cols, ht, dz, dw_out, small)


def _adamw_math(w, g, m, v):
    m = ADAM_B1 * m + (1.0 - ADAM_B1) * g
    v = ADAM_B2 * v + (1.0 - ADAM_B2) * (g * g)
    m_hat = m / (1.0 - ADAM_B1 ** ADAM_STEP)
    v_hat = v / (1.0 - ADAM_B2 ** ADAM_STEP)
    delta = -ADAM_LR * (m_hat / (jnp.sqrt(v_hat) + ADAM_EPS) + ADAM_WD * w)
    return delta, m, v


def _sum_slabs(ref, first=None):
    total = ref[0].astype(F32) if first is None else first + ref[0].astype(F32)
    for s in range(1, ref.shape[0]):
        total = total + ref[s].astype(F32)
    return total


def _adamw_slabs(parts, own, w, m, v, name, tr):
    rows, cols = w.shape
    tile = pl.BlockSpec((tr, cols), lambda i: (i, 0))

    def body(p_ref, *refs):
        own_ref = refs[0] if own is not None else None
        w_ref, m_ref, v_ref, g_ref, d_ref, nm_ref, nv_ref = refs[-7:]
        g = _sum_slabs(p_ref, None if own_ref is None else own_ref[...])
        g_ref[...] = g
        d_ref[...], nm_ref[...], nv_ref[...] = _adamw_math(w_ref[...], g, m_ref[...], v_ref[...])

    extra = [] if own is None else [own]
    return pl.pallas_call(
        body, name=name, grid=(rows // tr,),
        in_specs=[pl.BlockSpec((parts.shape[0], tr, cols), lambda i: (0, i, 0))] + [tile] * (len(extra) + 3),
        out_specs=[tile] * 4,
        out_shape=[jax.ShapeDtypeStruct((rows, cols), F32)] * 4,
        compiler_params=_params(1),
    )(parts, *extra, w, m, v)


def _sum_small(parts):
    def body(p_ref, out_ref):
        out_ref[...] = _sum_slabs(p_ref)

    return pl.pallas_call(body, name="sum_small", out_shape=jax.ShapeDtypeStruct(parts.shape[1:], F32))(parts)


def _adamw_whole(g, w, m, v, name):
    def body(g_ref, w_ref, m_ref, v_ref, d_ref, nm_ref, nv_ref):
        d_ref[...], nm_ref[...], nv_ref[...] = _adamw_math(w_ref[...], g_ref[...], m_ref[...], v_ref[...])

    return pl.pallas_call(body, name=name, out_shape=[jax.ShapeDtypeStruct(w.shape, F32)] * 3)(g, w, m, v)


def kernel(x, norm_pre_g, w_in, conv_w, w_out, norm_post_g, loss_target, m_norm_pre_g, m_w_in, m_conv_w, m_w_out,
           m_norm_post_g, v_norm_pre_g, v_w_in, v_conv_w, v_w_out, v_norm_post_g):
    n_conv = CONV_W // N_DEV
    w_in_g, w_out_g, conv_g = _ag_weights(w_in, w_out, conv_w)
    conv_full = conv_g[:, 0:3, 0:n_conv].transpose(1, 0, 2).reshape(3, CONV_W)
    grad_x, ht, dz, dw_out, dw_out_bf, small = _local_step(x[0], loss_target[0], norm_pre_g, norm_post_g, w_in_g,
                                                           w_out_g.reshape(D_MODEL, D_MODEL), conv_full)
    own_in, r_in, r_out, r_small = _dw_in_rs(ht, dz, dw_out_bf.reshape(N_DEV, SHARD_OUT, D_MODEL), small)
    me = 4 * lax.axis_index("x") + 2 * lax.axis_index("y") + lax.axis_index("c")
    own_out = lax.dynamic_index_in_dim(dw_out.reshape(N_DEV, SHARD_OUT, D_MODEL), me, keepdims=False)
    g_in, d_in, nm_in, nv_in = _adamw_slabs(r_in, own_in, w_in, m_w_in, v_w_in, "adamw_in", 256)
    g_out, d_out, nm_out, nv_out = _adamw_slabs(r_out, own_out, w_out, m_w_out, v_w_out, "adamw_out", SHARD_OUT)
    sums = _sum_small(r_small)
    g_pre, g_post, loss = sums[0], sums[1], sums[2, 0]
    g_conv = lax.dynamic_slice(sums[3:6, 0:CONV_W], (0, me * n_conv), (3, n_conv))
    vec = lambda a: a.reshape(1, D_MODEL)
    d_pre, nm_pre, nv_pre = _adamw_whole(vec(g_pre), vec(norm_pre_g), vec(m_norm_pre_g), vec(v_norm_pre_g), "adamw_pre")
    d_post, nm_post, nv_post = _adamw_whole(vec(g_post), vec(norm_post_g), vec(m_norm_post_g), vec(v_norm_post_g),
                                            "adamw_post")
    d_conv, nm_conv, nv_conv = _adamw_whole(g_conv, conv_w, m_conv_w, v_conv_w, "adamw_conv")
    flat = lambda a: a.reshape(D_MODEL)
    return (loss, grad_x[None], g_pre, g_in, g_conv, g_out, g_post,
            flat(d_pre), d_in, d_conv, d_out, flat(d_post),
            flat(nm_pre), nm_in, nm_conv, nm_out, flat(nm_post),
            flat(nv_pre), nv_in, nv_conv, nv_out, flat(nv_post))
```

```python
import functools

import jax
import jax.numpy as jnp
import numpy as np
from jax import lax
from jax.experimental import pallas as pl
from jax.experimental.pallas import tpu as pltpu

F32 = jnp.float32
BF16 = jnp.bfloat16

D_MODEL = 1024
HEAD_DIM = 64
ATTN_W = 768
CONV_W = 256
IN_W = 4096
REST_W = IN_W - 3 * ATTN_W
BLK = 128
N_DEV = 8
SHARD_IN = IN_W // N_DEV
SHARD_OUT = D_MODEL // N_DEV
DILATIONS = (1, 4, 16)
ROPE_THETA = 10000.0
NORM_EPS = 1e-6
NEG = -1e30

ADAM_LR = 0.001
ADAM_B1 = 0.9
ADAM_B2 = 0.999
ADAM_EPS = 1e-08
ADAM_WD = 0.01
ADAM_STEP = 10

VMEM_LIMIT = 56 * 1024 * 1024
MESH = pl.DeviceIdType.MESH


def _params(n_grid):
    return pltpu.CompilerParams(dimension_semantics=("arbitrary",) * n_grid, vmem_limit_bytes=VMEM_LIMIT)


def _resident(shape):
    zeros = (0,) * len(shape)
    return pl.BlockSpec(shape, lambda *_: zeros, pipeline_mode=pl.Buffered(1))


def _sigmoid(a):
    return 1.0 / (1.0 + jnp.exp(-a))


def _swap_halves(t, first_half):
    return jnp.where(first_half, pltpu.roll(t, BLK - 32, 1), pltpu.roll(t, 32, 1))


def _rope_tables(seq, tm):
    half = HEAD_DIM // 2
    inv_freq = ROPE_THETA ** (-jnp.arange(half, dtype=F32) * 2.0 / HEAD_DIM)
    freq = jnp.concatenate([inv_freq] * 4)
    sign = jnp.concatenate([-jnp.ones(half, F32), jnp.ones(half, F32)] * 2)
    starts = (jnp.arange(seq // tm) * tm).astype(F32)[:, None] * freq[None, :]
    rows = jnp.arange(tm).astype(F32)[:, None] * freq[None, :]
    slab = lambda a: jnp.broadcast_to(a[:, None, :], (seq // tm, 8, BLK))
    return slab(jnp.cos(starts)), slab(jnp.sin(starts) * sign), jnp.cos(rows), jnp.sin(rows) * sign


def _rope_specs(tm):
    return [pl.BlockSpec((1, 8, BLK), lambda i: (i, 0, 0))] * 2 + [_resident((tm, BLK))] * 2


def _tile_rope(cos_start, sin_start, cos_row, sin_row):
    ca, sa, cb, sb = cos_start[0, 0:1, :], sin_start[0, 0:1, :], cos_row[...], sin_row[...]
    return ca * cb - sa * sb, sa * cb + ca * sb


N_CHUNK = ATTN_W // BLK


def _lanes(r, c):
    return slice(r * ATTN_W + c * BLK, r * ATTN_W + (c + 1) * BLK)


def _to_residues(src, chunk0, dst_refs, tmp, rows, dtype):
    assert DILATIONS == (1, 4, 16)
    dst1, dst4, dst16 = dst_refs
    n4, n16 = rows // 4, rows // 16
    for c in range(N_CHUNK):
        dst1[:, _lanes(0, c)] = src[chunk0 + c].astype(dtype)
        for r1 in range(4):
            tmp[c, r1 * n4:(r1 + 1) * n4, :] = src[chunk0 + c, pl.ds(r1, n4, stride=4), :]
        for r1 in range(4):
            dst4[:, _lanes(r1, c)] = tmp[c, r1 * n4:(r1 + 1) * n4, :].astype(dtype)
            for r2 in range(4):
                dst16[:, _lanes(4 * r2 + r1, c)] = tmp[c, pl.ds(r1 * n4 + r2, n16, stride=4), :].astype(dtype)


def _from_residue(src_ref, dst, dil, rows, accumulate, tmp=None):
    n4, n16 = rows // 4, rows // 16

    def put(where, piece):
        if accumulate:
            dst[where] += piece
        else:
            dst[where] = piece

    for c in range(N_CHUNK):
        if dil == 1:
            put((c,), src_ref[:, _lanes(0, c)].astype(F32))
            continue
        for r1 in range(4):
            if dil == 4:
                piece = src_ref[:, _lanes(r1, c)].astype(F32)
            else:
                for r2 in range(4):
                    tmp[c, pl.ds(r1 * n4 + r2, n16, stride=4), :] = src_ref[:, _lanes(4 * r2 + r1, c)].astype(F32)
                piece = tmp[c, r1 * n4:(r1 + 1) * n4, :]
            put((c, pl.ds(r1, n4, stride=4), slice(None)), piece)


def _residue_spec(tm, dil):
    return pl.BlockSpec((tm // dil, dil * ATTN_W), lambda i: (i, 0))


def _residue_shape(seq, dil, dtype):
    return jax.ShapeDtypeStruct((seq // dil, dil * ATTN_W), dtype)


def _fwd_in(x, g_pre, w_in_g, tm=512):
    seq = x.shape[0]
    n_dil = len(DILATIONS)

    def body(x_ref, g_ref, w_ref, ca_ref, sa_ref, cb_ref, sb_ref, *rest):
        qkv_refs, (zr_ref, ht_ref, qkv_scr, tmp) = rest[:3 * n_dil], rest[3 * n_dil:]
        xv = x_ref[...]
        r = lax.rsqrt(jnp.mean(xv * xv, axis=-1, keepdims=True) + NORM_EPS)
        hf = (xv * r) * g_ref[...]
        ht_ref[...] = hf.T.astype(BF16)
        h = hf.astype(BF16)
        cos, sin = _tile_rope(ca_ref, sa_ref, cb_ref, sb_ref)
        first_half = (lax.broadcasted_iota(jnp.int32, (tm, BLK), 1) & 32) == 0

        def rope(t):
            return t * cos + _swap_halves(t, first_half) * sin

        def project(j):
            return jnp.dot(h, w_ref[j], preferred_element_type=F32)

        def place(j, zj):
            for n in range(SHARD_IN // BLK):
                chunk, t = j * (SHARD_IN // BLK) + n, zj[:, n * BLK:(n + 1) * BLK]
                if chunk < N_CHUNK:
                    qkv_scr[chunk] = rope(t) * HEAD_DIM ** -0.5
                elif chunk < 2 * N_CHUNK:
                    qkv_scr[chunk] = rope(t)
                elif chunk < 3 * N_CHUNK:
                    qkv_scr[chunk] = t
                else:
                    zr_ref[:, (chunk - 3 * N_CHUNK) * BLK:(chunk - 3 * N_CHUNK + 1) * BLK] = t.astype(BF16)

        ahead = project(0)
        for j in range(N_DEV):
            zj = ahead
            if j + 1 < N_DEV:
                ahead = project(j + 1)
            place(j, zj)
            for a in range(3):
                if (a + 1) * N_CHUNK - 1 in range(j * (SHARD_IN // BLK), (j + 1) * (SHARD_IN // BLK)):
                    _to_residues(qkv_scr, a * N_CHUNK, [qkv_refs[3 * n + a] for n in range(n_dil)], tmp, tm, BF16)

    row = lambda w: pl.BlockSpec((tm, w), lambda i: (i, 0))
    outs = pl.pallas_call(
        body, name="fwd_in", grid=(seq // tm,),
        in_specs=[row(D_MODEL), _resident((1, D_MODEL)), _resident((N_DEV, D_MODEL, SHARD_IN))] + _rope_specs(tm),
        out_specs=[_residue_spec(tm, dil) for dil in DILATIONS for _ in range(3)]
        + [row(REST_W), pl.BlockSpec((D_MODEL, tm), lambda i: (0, i))],
        out_shape=[_residue_shape(seq, dil, BF16) for dil in DILATIONS for _ in range(3)]
        + [jax.ShapeDtypeStruct((seq, REST_W), BF16), jax.ShapeDtypeStruct((D_MODEL, seq), BF16)],
        scratch_shapes=[pltpu.VMEM((3 * N_CHUNK, tm, BLK), F32), pltpu.VMEM((N_CHUNK, tm, BLK), F32)],
        compiler_params=_params(1),
    )(x, g_pre.reshape(1, D_MODEL), w_in_g, *_rope_tables(seq, tm))
    qkv = [tuple(outs[3 * n:3 * n + 3]) for n in range(n_dil)]
    return qkv, outs[3 * n_dil], outs[3 * n_dil + 1]


def _band_bias(first_block):
    kj = lax.broadcasted_iota(jnp.int32, (2 * BLK, BLK), 0)
    qi = lax.broadcasted_iota(jnp.int32, (2 * BLK, BLK), 1)
    valid = (kj >= qi) & (kj <= qi + BLK)
    bias = jnp.where(valid, 0.0, NEG).astype(BF16)
    bias_first = jnp.where(valid & (kj >= BLK), 0.0, NEG).astype(BF16)
    onehot = ((kj & (BLK - 1)) == qi).astype(F32).astype(BF16)
    return onehot, bias, jnp.where(first_block, bias_first, bias)


def _stack_heads(t, head0):
    del head0
    keep0 = (lax.broadcasted_iota(jnp.int32, t.shape, 1) < HEAD_DIM).astype(F32).astype(BF16)
    return jnp.concatenate([t * keep0, t * (1 - keep0)], axis=0)


def _unstack_heads(t2, head0):
    return jnp.where(head0, t2[:BLK], t2[BLK:])


def _rows_per_head(a, head0):
    b = pltpu.roll(a, HEAD_DIM, 1)
    rows = jnp.concatenate([jnp.where(head0, a, b), jnp.where(head0, b, a)], axis=0)
    return jnp.concatenate([rows, rows], axis=1)


BLOCKS_PER_STEP = 32


def _attn_specs(length, dil):
    n_blocks = length // BLK
    tb = min(BLOCKS_PER_STEP, n_blocks)
    nc = BLOCKS_PER_STEP // tb
    assert (dil * N_CHUNK) % nc == 0 and n_blocks % tb == 0
    tile = pl.BlockSpec((tb * BLK, nc * BLK), lambda c, t: (t, c))
    prev = pl.BlockSpec((BLK, nc * BLK), lambda c, t: (jnp.maximum(t * tb - 1, 0), c))
    grid = (dil * N_CHUNK // nc, n_blocks // tb)
    return tb, nc, tile, prev, grid


def _load_keys(cat, prev_ref, cur_ref):
    cat[0:BLK] = prev_ref[...]
    cat[BLK:] = cur_ref[...]


def _attn_fwd(q, k, v, dil):
    length = q.shape[0]
    tb, nc, tile, prev, grid = _attn_specs(length, dil)

    def body(q_ref, kc_ref, kp_ref, vc_ref, vp_ref, o_ref, lse_ref, kcat, vcat):
        _load_keys(kcat, kp_ref, kc_ref)
        _load_keys(vcat, vp_ref, vc_ref)
        head0 = lax.broadcasted_iota(jnp.int32, (BLK, BLK), 1) < HEAD_DIM
        onehot, bias, bias_start = _band_bias(pl.program_id(1) == 0)
        ones = jnp.ones((2 * BLK, BLK), BF16)
        def scores(c, j):
            rows, cols = slice(j * BLK, (j + 1) * BLK), slice(c * BLK, (c + 1) * BLK)
            q2 = jnp.concatenate([_stack_heads(q_ref[rows, cols], head0), onehot], axis=1)
            kk = jnp.concatenate([kcat[j * BLK:(j + 2) * BLK, cols], bias_start if j == 0 else bias], axis=1)
            return (lax.dot_general(q2, kk, (((1,), (1,)), ((), ())), preferred_element_type=F32),)

        def probabilities(c, j, s):
            m = jnp.max(s, axis=1, keepdims=True)
            return m, jnp.exp(s - m).astype(BF16)

        def outputs(c, j, m, p):
            rows, cols = slice(j * BLK, (j + 1) * BLK), slice(c * BLK, (c + 1) * BLK)
            vv = jnp.concatenate([vcat[j * BLK:(j + 2) * BLK, cols], ones], axis=1)
            pv = jnp.dot(p, vv, preferred_element_type=F32)
            den = pv[:, BLK:]
            o_ref[rows, cols] = _unstack_heads(pv[:, :BLK] / den, head0).astype(BF16)
            lse_ref[rows, cols] = _unstack_heads(m + jnp.log(den), head0)

        units = [(c, j) for c in range(nc) for j in range(tb)]
        stage1, stage2 = {}, {}
        for n in range(len(units) + 2):
            if n < len(units):
                stage1[n] = scores(*units[n])
            if 0 <= n - 1 < len(units):
                stage2[n - 1] = probabilities(*units[n - 1], *stage1.pop(n - 1))
            if 0 <= n - 2 < len(units):
                outputs(*units[n - 2], *stage2.pop(n - 2))

    return pl.pallas_call(
        body, name=f"attn_fwd_d{dil}", grid=grid,
        in_specs=[tile, tile, prev, tile, prev], out_specs=[tile, tile],
        out_shape=[jax.ShapeDtypeStruct(q.shape, BF16), jax.ShapeDtypeStruct(q.shape, F32)],
        scratch_shapes=[pltpu.VMEM(((tb + 1) * BLK, nc * BLK), BF16)] * 2,
        compiler_params=_params(2),
    )(q, k, k, v, v)


def _attn_bwd(q, k, v, do, lse, delta, dil):
    length = q.shape[0]
    tb, nc, tile, prev, grid = _attn_specs(length, dil)
    whole = pl.BlockSpec((length, nc * BLK), lambda c, t: (0, c))

    def body(q_ref, do_ref, lse_ref, dl_ref, kc_ref, kp_ref, vc_ref, vp_ref, dq_ref, dk_ref, dv_ref, kcat, vcat):
        t = pl.program_id(1)
        _load_keys(kcat, kp_ref, kc_ref)
        _load_keys(vcat, vp_ref, vc_ref)
        head0 = lax.broadcasted_iota(jnp.int32, (BLK, BLK), 1) < HEAD_DIM
        onehot, bias, bias_start = _band_bias(t == 0)

        def scores(c, j):
            rows, cols = slice(j * BLK, (j + 1) * BLK), slice(c * BLK, (c + 1) * BLK)
            q2 = _stack_heads(q_ref[rows, cols], head0)
            do2 = _stack_heads(do_ref[rows, cols], head0)
            kk = kcat[j * BLK:(j + 2) * BLK, cols]
            s = lax.dot_general(jnp.concatenate([q2, onehot], axis=1),
                                jnp.concatenate([kk, bias_start if j == 0 else bias], axis=1),
                                (((1,), (1,)), ((), ())), preferred_element_type=F32)
            dp = lax.dot_general(do2, vcat[j * BLK:(j + 2) * BLK, cols], (((1,), (1,)), ((), ())),
                                 preferred_element_type=F32)
            return q2, do2, kk, s, dp

        def probabilities(c, j, q2, do2, kk, s, dp):
            rows, cols = slice(j * BLK, (j + 1) * BLK), slice(c * BLK, (c + 1) * BLK)
            p = jnp.exp(s - _rows_per_head(lse_ref[rows, cols], head0))
            ds = (p * (dp - _rows_per_head(dl_ref[rows, cols].astype(F32), head0))).astype(BF16)
            return q2, do2, kk, p.astype(BF16), ds

        def gradients(c, j, q2, do2, kk, p, ds):
            rows, cols = slice(j * BLK, (j + 1) * BLK), slice(c * BLK, (c + 1) * BLK)
            dq2 = jnp.dot(ds, kk, preferred_element_type=F32)
            dq_ref[rows, cols] = (_unstack_heads(dq2, head0) * HEAD_DIM ** -0.5).astype(BF16)
            dk2 = lax.dot_general(ds, q2, (((0,), (0,)), ((), ())), preferred_element_type=F32)
            dv2 = lax.dot_general(p, do2, (((0,), (0,)), ((), ())), preferred_element_type=F32)
            own = pl.ds(pl.multiple_of((t * tb + j) * BLK, BLK), BLK)
            dk_ref[own, cols] = dk2[BLK:].astype(BF16)
            dv_ref[own, cols] = dv2[BLK:].astype(BF16)

            def add_to_previous():
                before = pl.ds(pl.multiple_of((t * tb + j - 1) * BLK, BLK), BLK)
                dk_ref[before, cols] = (dk_ref[before, cols].astype(F32) + dk2[:BLK]).astype(BF16)
                dv_ref[before, cols] = (dv_ref[before, cols].astype(F32) + dv2[:BLK]).astype(BF16)

            if j == 0:
                pl.when(t > 0)(add_to_previous)
            else:
                add_to_previous()

        units = [(c, j) for c in range(nc) for j in range(tb)]
        stage1 = {0: scores(*units[0])}
        for n in range(len(units)):
            stage2 = probabilities(*units[n], *stage1.pop(n))
            if n + 1 < len(units):
                stage1[n + 1] = scores(*units[n + 1])
            gradients(*units[n], *stage2)

    return pl.pallas_call(
        body, name=f"attn_bwd_d{dil}", grid=grid,
        in_specs=[tile, tile, tile, tile, tile, prev, tile, prev], out_specs=[tile, whole, whole],
        out_shape=[jax.ShapeDtypeStruct(q.shape, BF16)] * 3,
        scratch_shapes=[pltpu.VMEM(((tb + 1) * BLK, nc * BLK), BF16)] * 2,
        compiler_params=_params(2),
    )(q, do, lse, delta, k, k, v, v)


HALO = 16


def _halo_specs(tm, seq):
    before = lambda w: pl.BlockSpec((HALO, w), lambda i: (jnp.maximum(i * (tm // HALO) - 1, 0), 0))
    after = lambda w: pl.BlockSpec((HALO, w), lambda i: (jnp.minimum((i + 1) * (tm // HALO), seq // HALO - 1), 0))
    return before, after


def _conv_taps(u, before, tm):
    row = lax.broadcasted_iota(jnp.int32, u.shape, 0)
    last, last2 = before[HALO - 1:HALO, :], before[HALO - 2:HALO - 1, :]
    u1 = jnp.where(row == 0, last, pltpu.roll(u, 1, 0))
    u2 = jnp.where(row == 0, last2, jnp.where(row == 1, last, pltpu.roll(u, 2, 0)))
    return u1, u2


def _attn_combine(o_parts, lse_parts, zr, conv_w, tm=256):
    seq = zr.shape[0]
    a0, h0, b0, c0, g0 = 0, ATTN_W, ATTN_W + CONV_W, ATTN_W + 2 * CONV_W, ATTN_W + 3 * CONV_W

    def body(o1, o2, o3, l1, l2, l3, zr_ref, zp_ref, w_ref, mixed_ref, o_ref, lse1, lse2, lse3, *scr):
        i = pl.program_id(0)
        for src, dst, dil in zip((o2, o3, l2, l3), scr[:4], DILATIONS[1:] * 2):
            _from_residue(src, dst, dil, tm, accumulate=False, tmp=scr[5])
        for c in range(N_CHUNK):
            cols = slice(c * BLK, (c + 1) * BLK)
            la, lb, lc = l1[:, cols], scr[2][c], scr[3][c]
            top = jnp.maximum(jnp.maximum(la, lb), lc)
            ea, eb, ec = jnp.exp(la - top), jnp.exp(lb - top), jnp.exp(lc - top)
            den = ea + eb + ec
            o = (ea / den) * o1[:, cols].astype(F32) + (eb / den) * scr[0][c] + (ec / den) * scr[1][c]
            o_ref[:, cols] = o.astype(BF16)
            scr[4][c] = top + jnp.log(den)
            ga = zr_ref[:, cols].astype(F32)
            mixed_ref[:, cols] = (o * (ga * _sigmoid(ga))).astype(BF16)
        _to_residues(scr[4], 0, (lse1, lse2, lse3), scr[5], tm, F32)
        part = lambda ref, lo, hi: ref[:, lo:hi].astype(F32)
        u = part(zr_ref, c0, g0) * part(zr_ref, h0, b0)
        before = jnp.where(i > 0, part(zp_ref, c0, g0) * part(zp_ref, h0, b0), 0.0)
        u1, u2 = _conv_taps(u, before, tm)
        y = u2 * w_ref[0:1, :] + u1 * w_ref[1:2, :] + u * w_ref[2:3, :]
        gc = part(zr_ref, g0, REST_W)
        mixed_ref[:, ATTN_W:] = ((part(zr_ref, b0, c0) * y) * (gc * _sigmoid(gc))).astype(BF16)

    row = lambda w: pl.BlockSpec((tm, w), lambda i: (i, 0))
    before, _ = _halo_specs(tm, seq)
    views = [_residue_spec(tm, dil) for dil in DILATIONS]
    outs = pl.pallas_call(
        body, name="attn_combine", grid=(seq // tm,),
        in_specs=views * 2 + [row(REST_W), before(REST_W), _resident((3, CONV_W))],
        out_specs=[row(D_MODEL), row(ATTN_W)] + views,
        out_shape=[jax.ShapeDtypeStruct((seq, D_MODEL), BF16), jax.ShapeDtypeStruct((seq, ATTN_W), BF16)]
        + [_residue_shape(seq, dil, F32) for dil in DILATIONS],
        scratch_shapes=[pltpu.VMEM((N_CHUNK, tm, BLK), F32)] * 6,
        compiler_params=_params(1),
    )(*o_parts, *lse_parts, zr, zr, conv_w)
    return outs[0], outs[1], outs[2:]


def _out_loss_bwd(mixed, w_out_g, x, target, g_post, tm=512):
    seq = x.shape[0]

    def body(mx_ref, w_ref, x_ref, t_ref, g_ref, dout_ref, dmx_ref, dw_ref, dwb_ref, st_ref):
        i = pl.program_id(0)
        mx = mx_ref[...]
        y = jnp.dot(mx, w_ref[...], preferred_element_type=F32)
        r = lax.rsqrt(jnp.mean(y * y, axis=-1, keepdims=True) + NORM_EPS)
        yhat = y * r
        g = g_ref[...]
        err = (x_ref[...] + yhat * g) - t_ref[...]
        dn = err * (1.0 / D_MODEL)
        dout_ref[...] = dn
        tg = dn * g
        dy = (r * (tg - yhat * jnp.mean(tg * yhat, axis=-1, keepdims=True))).astype(BF16)
        dmx_ref[...] = lax.dot_general(dy, w_ref[...], (((1,), (1,)), ((), ())),
                                       preferred_element_type=F32).astype(BF16)
        dw = lax.dot_general(mx, dy, (((0,), (0,)), ((), ())), preferred_element_type=F32)
        gsum = jnp.sum(dn * yhat, axis=0, keepdims=True)
        lsum = jnp.broadcast_to(0.5 / D_MODEL * jnp.sum(err * err), (1, D_MODEL))

        @pl.when(i == 0)
        def _():
            dw_ref[...] = dw
            st_ref[...] = jnp.zeros_like(st_ref)
            st_ref[0:1, :] = gsum
            st_ref[1:2, :] = lsum

        @pl.when(i > 0)
        def _():
            dw_ref[...] += dw
            st_ref[0:1, :] += gsum
            st_ref[1:2, :] += lsum

        @pl.when(i == seq // tm - 1)
        def _():
            dwb_ref[...] = dw_ref[...].astype(BF16)

    row = lambda w: pl.BlockSpec((tm, w), lambda i: (i, 0))
    whole = pl.BlockSpec((D_MODEL, D_MODEL), lambda i: (0, 0))
    return pl.pallas_call(
        body, name="out_loss_bwd", grid=(seq // tm,),
        in_specs=[row(D_MODEL), _resident((D_MODEL, D_MODEL)), row(D_MODEL), row(D_MODEL), _resident((1, D_MODEL))],
        out_specs=[row(D_MODEL), row(D_MODEL), whole, whole, pl.BlockSpec((8, D_MODEL), lambda i: (0, 0))],
        out_shape=[jax.ShapeDtypeStruct((seq, D_MODEL), F32), jax.ShapeDtypeStruct((seq, D_MODEL), BF16),
                   jax.ShapeDtypeStruct((D_MODEL, D_MODEL), F32), jax.ShapeDtypeStruct((D_MODEL, D_MODEL), BF16),
                   jax.ShapeDtypeStruct((8, D_MODEL), F32)],
        compiler_params=_params(1),
    )(mixed, w_out_g, x, target, g_post.reshape(1, D_MODEL))


def _head_sum(prod, same_head):
    hi = prod.astype(BF16)
    lo = (prod - hi.astype(F32)).astype(BF16)
    return (jnp.dot(hi, same_head, preferred_element_type=F32) + jnp.dot(lo, same_head, preferred_element_type=F32))


def _gate_bwd(dmixed, zr, o, conv_w, tm=256):
    seq = zr.shape[0]
    n_tiles = seq // tm
    n_dil = len(DILATIONS)
    a0, h0, b0, c0, g0 = 0, ATTN_W, ATTN_W + CONV_W, ATTN_W + 2 * CONV_W, ATTN_W + 3 * CONV_W

    def body(dm_ref, dmn_ref, zr_ref, zp_ref, zn_ref, o_ref, w_ref, *rest):
        do_refs, dl_refs = rest[:n_dil], rest[n_dil:2 * n_dil]
        dz_ref, dw_ref, do_scr, dl_scr, tmp = rest[2 * n_dil:]
        i = pl.program_id(0)
        part = lambda ref, lo, hi: ref[:, lo:hi].astype(F32)
        ga = part(zr_ref, a0, h0)
        sg = _sigmoid(ga)
        dattn = part(dm_ref, 0, ATTN_W)
        ov = o_ref[...].astype(F32)
        do = dattn * (ga * sg)
        dz_ref[:, a0:h0] = (dattn * ov * (sg * (1.0 + ga * (1.0 - sg)))).astype(BF16)
        li = lax.broadcasted_iota(jnp.int32, (BLK, BLK), 0) // HEAD_DIM
        lj = lax.broadcasted_iota(jnp.int32, (BLK, BLK), 1) // HEAD_DIM
        same_head = (li == lj).astype(BF16)
        prod = do * ov
        for c in range(N_CHUNK):
            cols = slice(c * BLK, (c + 1) * BLK)
            do_scr[c] = do[:, cols]
            dl_scr[c] = _head_sum(prod[:, cols], same_head)
        _to_residues(do_scr, 0, do_refs, tmp, tm, BF16)
        _to_residues(dl_scr, 0, dl_refs, tmp, tm, BF16)

        ch, cb, cc, gc = (part(zr_ref, lo, hi) for lo, hi in ((h0, b0), (b0, c0), (c0, g0), (g0, REST_W)))
        u = cc * ch
        before = jnp.where(i > 0, part(zp_ref, c0, g0) * part(zp_ref, h0, b0), 0.0)
        u1, u2 = _conv_taps(u, before, tm)
        w0, w1, w2 = w_ref[0:1, :], w_ref[1:2, :], w_ref[2:3, :]
        y = u2 * w0 + u1 * w1 + u * w2
        sc = _sigmoid(gc)
        silu_c = gc * sc
        dconv = part(dm_ref, ATTN_W, D_MODEL)
        dz_ref[:, b0:c0] = (dconv * y * silu_c).astype(BF16)
        dz_ref[:, g0:] = (dconv * (cb * y) * (sc * (1.0 + gc * (1.0 - sc)))).astype(BF16)
        dy = dconv * cb * silu_c
        gn = part(zn_ref, g0, REST_W)
        after = jnp.where(i < n_tiles - 1,
                          part(dmn_ref, ATTN_W, D_MODEL) * part(zn_ref, b0, c0) * (gn * _sigmoid(gn)), 0.0)
        row = lax.broadcasted_iota(jnp.int32, dy.shape, 0)
        nxt, nxt2 = after[0:1, :], after[1:2, :]
        dy1 = jnp.where(row == tm - 1, nxt, pltpu.roll(dy, tm - 1, 0))
        dy2 = jnp.where(row == tm - 1, nxt2, jnp.where(row == tm - 2, nxt, pltpu.roll(dy, tm - 2, 0)))
        du = dy * w2 + dy1 * w1 + dy2 * w0
        dz_ref[:, c0:g0] = (du * ch).astype(BF16)
        dz_ref[:, h0:b0] = (du * cc).astype(BF16)
        dws = [jnp.sum(dy * u2, axis=0, keepdims=True), jnp.sum(dy * u1, axis=0, keepdims=True),
               jnp.sum(dy * u, axis=0, keepdims=True)]

        @pl.when(i == 0)
        def _():
            dw_ref[...] = jnp.zeros_like(dw_ref)

        for n, part in enumerate(dws):
            dw_ref[n:n + 1, :] += part

    row_spec = lambda w: pl.BlockSpec((tm, w), lambda i: (i, 0))
    before, after = _halo_specs(tm, seq)
    views = [_residue_spec(tm, dil) for dil in DILATIONS]
    outs = pl.pallas_call(
        body, name="gate_bwd", grid=(n_tiles,),
        in_specs=[row_spec(D_MODEL), after(D_MODEL), row_spec(REST_W), before(REST_W), after(REST_W),
                  row_spec(ATTN_W), _resident((3, CONV_W))],
        out_specs=views * 2 + [row_spec(REST_W), pl.BlockSpec((8, CONV_W), lambda i: (0, 0))],
        out_shape=[_residue_shape(seq, dil, BF16) for dil in DILATIONS] * 2
        + [jax.ShapeDtypeStruct((seq, REST_W), BF16), jax.ShapeDtypeStruct((8, CONV_W), F32)],
        scratch_shapes=[pltpu.VMEM((N_CHUNK, tm, BLK), F32)] * 3,
        compiler_params=_params(1),
    )(dmixed, dmixed, zr, zr, zr, o, conv_w)
    return outs[:n_dil], outs[n_dil:2 * n_dil], outs[2 * n_dil], outs[2 * n_dil + 1]


def _in_bwd(dqs, dks, dvs, dzr, x, d_out, g_pre, w_in_g, tm=256):
    seq = x.shape[0]

    def body(q1, q2, q3, k1, k2, k3, v1, v2, v3, dzr_ref, ca_ref, sa_ref, cb_ref, sb_ref, x_ref, dout_ref, g_ref,
             w_ref, dz_ref, gx_ref, st_ref, dq_scr, dk_scr, dv_scr, tmp):
        i = pl.program_id(0)
        cos, sin = _tile_rope(ca_ref, sa_ref, cb_ref, sb_ref)
        first_half = (lax.broadcasted_iota(jnp.int32, (tm, BLK), 1) & 32) == 0
        streams = [((q1, q2, q3), dq_scr), ((k1, k2, k3), dk_scr), ((v1, v2, v3), dv_scr)]
        per_slab = SHARD_IN // BLK

        def unrope(t):
            return t * cos - _swap_halves(t, first_half) * sin

        def assemble(j):
            for chunk in range(j * per_slab, (j + 1) * per_slab):
                a, c = divmod(chunk, N_CHUNK)
                if a < 3 and c == 0:
                    parts, total = streams[a]
                    for n, dil in enumerate(DILATIONS):
                        _from_residue(parts[n], total, dil, tm, accumulate=n > 0, tmp=tmp)
                if a < 2:
                    val = unrope(streams[a][1][c]).astype(BF16)
                elif a == 2:
                    val = dv_scr[c].astype(BF16)
                else:
                    val = dzr_ref[:, (chunk - 3 * N_CHUNK) * BLK:(chunk - 3 * N_CHUNK + 1) * BLK]
                dz_ref[:, chunk * BLK:(chunk + 1) * BLK] = val
            return dz_ref[:, j * SHARD_IN:(j + 1) * SHARD_IN]

        ahead = assemble(0)
        dh = None
        for j in range(N_DEV):
            slab = ahead
            if j + 1 < N_DEV:
                ahead = assemble(j + 1)
            part = lax.dot_general(slab, w_ref[j], (((1,), (1,)), ((), ())), preferred_element_type=F32)
            dh = part if dh is None else dh + part
        xv = x_ref[...]
        r = lax.rsqrt(jnp.mean(xv * xv, axis=-1, keepdims=True) + NORM_EPS)
        xhat = xv * r
        tg = dh * g_ref[...]
        gx_ref[...] = dout_ref[...] + r * (tg - xhat * jnp.mean(tg * xhat, axis=-1, keepdims=True))
        gsum = jnp.sum(dh * xhat, axis=0, keepdims=True)

        @pl.when(i == 0)
        def _():
            st_ref[...] = jnp.zeros_like(st_ref)

        st_ref[0:1, :] += gsum

    row = lambda w: pl.BlockSpec((tm, w), lambda i: (i, 0))
    return pl.pallas_call(
        body, name="in_bwd", grid=(seq // tm,),
        in_specs=[_residue_spec(tm, dil) for dil in DILATIONS] * 3
        + [row(REST_W)] + _rope_specs(tm) + [row(D_MODEL), row(D_MODEL), _resident((1, D_MODEL)),
                                             _resident((N_DEV, D_MODEL, SHARD_IN))],
        out_specs=[row(IN_W), row(D_MODEL), pl.BlockSpec((8, D_MODEL), lambda i: (0, 0))],
        out_shape=[jax.ShapeDtypeStruct((seq, IN_W), BF16), jax.ShapeDtypeStruct((seq, D_MODEL), F32),
                   jax.ShapeDtypeStruct((8, D_MODEL), F32)],
        scratch_shapes=[pltpu.VMEM((N_CHUNK, tm, BLK), F32)] * 4,
        compiler_params=_params(1),
    )(*dqs, *dks, *dvs, dzr, *_rope_tables(seq, tm), x, d_out, g_pre.reshape(1, D_MODEL), w_in_g)


def _local_step(x, target, g_pre, g_post, w_in_g, w_out_g, conv_w):
    qkv, zr, ht = _fwd_in(x, g_pre, w_in_g)
    parts = [_attn_fwd(*qkv[n], dil) for n, dil in enumerate(DILATIONS)]
    mixed, o, lse = _attn_combine([p[0] for p in parts], [p[1] for p in parts], zr, conv_w)
    d_out, dmixed, dw_out, dw_out_bf, st_post = _out_loss_bwd(mixed, w_out_g, x, target, g_post)
    do, delta, dzr, dconv = _gate_bwd(dmixed, zr, o, conv_w)
    grads = [_attn_bwd(*qkv[n], do[n], lse[n], delta[n], dil) for n, dil in enumerate(DILATIONS)]
    dz, grad_x, st_pre = _in_bwd([g[0] for g in grads], [g[1] for g in grads], [g[2] for g in grads], dzr,
                                 x, d_out, g_pre, w_in_g)
    conv_rows = jnp.pad(dconv[0:3], ((0, 0), (0, D_MODEL - CONV_W)))
    small = jnp.concatenate([st_pre[0:1], st_post[0:2], conv_rows, jnp.zeros((2, D_MODEL), F32)], axis=0)
    return grad_x, ht, dz, dw_out, dw_out_bf, small


def _coords():
    return lax.axis_index("x"), lax.axis_index("y"), lax.axis_index("c")


def _peer(k):
    x, y, c = _coords()
    px = 1 - x if k & 4 else x
    py = 1 - y if k & 2 else y
    pc = 1 - c if k & 1 else c
    return (px, py, pc), 4 * px + 2 * py + pc


HBM_SPEC = pl.BlockSpec(memory_space=pltpu.HBM)
VMEM_SPEC = pl.BlockSpec(memory_space=pltpu.VMEM)


def _ag_weights(w_in, w_out, conv_w):
    def body(win_ref, wout_ref, cw_ref, gin_ref, gout_ref, gcw_ref, win_bf, wout_bf, cw_pad, send_sems, recv_sems,
             local_sems):
        x, y, c = _coords()
        me, sibling = (x, y, c), (x, y, 1 - c)
        chips = [(1 - x, y), (x, 1 - y), (1 - x, 1 - y)]
        slab = lambda px, py, pc: 4 * px + 2 * py + pc
        win_bf[...] = win_ref[...].astype(BF16)
        wout_bf[...] = wout_ref[...].astype(BF16)
        cw_pad[...] = jnp.zeros_like(cw_pad)
        cw_pad[0:3, 0:CONV_W // N_DEV] = cw_ref[...]
        mine = [win_bf, wout_bf, cw_pad]
        gathered = [gin_ref, gout_ref, gcw_ref]

        def copies(k, block, to, own=False):
            return [pltpu.make_async_remote_copy(src_ref=mine[a] if own else gathered[a].at[slab(*block)],
                                                 dst_ref=gathered[a].at[slab(*block)], send_sem=send_sems.at[k, a],
                                                 recv_sem=recv_sems.at[k, a], device_id=to, device_id_type=MESH)
                    for a in range(3)]

        local = [pltpu.make_async_copy(mine[a], gathered[a].at[slab(*me)], local_sems.at[a]) for a in range(3)]
        for cp in local:
            cp.start()
        first = copies(0, me, sibling, own=True)
        for j, chip in enumerate(chips):
            first += copies(1 + j, me, (*chip, c), own=True)
        for cp in first:
            cp.start()
        passed = []
        for j, chip in enumerate(chips):
            for cp in copies(1 + j, (*chip, c), me):
                cp.wait_recv()
            onward = copies(4 + j, (*chip, c), sibling)
            for cp in onward:
                cp.start()
            passed += onward
        for cp in copies(0, sibling, me):
            cp.wait_recv()
        for j, chip in enumerate(chips):
            for cp in copies(4 + j, (*chip, 1 - c), me):
                cp.wait_recv()
        for cp in first + passed:
            cp.wait_send()
        for cp in local:
            cp.wait()

    return pl.pallas_call(
        body, name="ag_weights",
        in_specs=[VMEM_SPEC, VMEM_SPEC, VMEM_SPEC], out_specs=[HBM_SPEC, HBM_SPEC, HBM_SPEC],
        out_shape=[jax.ShapeDtypeStruct((N_DEV, D_MODEL, SHARD_IN), BF16),
                   jax.ShapeDtypeStruct((N_DEV, SHARD_OUT, D_MODEL), BF16),
                   jax.ShapeDtypeStruct((N_DEV, 8, BLK), F32)],
        scratch_shapes=[pltpu.VMEM((D_MODEL, SHARD_IN), BF16), pltpu.VMEM((SHARD_OUT, D_MODEL), BF16),
                        pltpu.VMEM((8, BLK), F32), pltpu.SemaphoreType.DMA((N_DEV - 1, 3)),
                        pltpu.SemaphoreType.DMA((N_DEV - 1, 3)), pltpu.SemaphoreType.DMA((3,))],
        compiler_params=pltpu.CompilerParams(vmem_limit_bytes=VMEM_LIMIT),
    )(w_in, w_out, conv_w)


def _dw_in_rs(ht, dz, dw_out, small):
    seq = dz.shape[0]

    def body(cols_ref, ht_ref, dz_ref, dout_ref, sm_ref, own_ref, rin_ref, rout_ref, rsm_ref, to_sibling, landed,
             to_chip, zero_buf, d2d_send, d2d_recv, ici_send, ici_recv, side_send, side_recv, local_sems):
        del cols_ref
        step = pl.program_id(0)
        x, y, c = _coords()
        me = 4 * x + 2 * y + c
        sibling = (x, y, 1 - c)
        chips = [(1 - x, y), (x, 1 - y), (1 - x, 1 - y)]

        def d2d(n):
            return pltpu.make_async_remote_copy(src_ref=to_sibling.at[n], dst_ref=landed.at[n], send_sem=d2d_send.at[n],
                                                recv_sem=d2d_recv.at[n], device_id=sibling, device_id_type=MESH)

        def ici(n):
            return pltpu.make_async_remote_copy(src_ref=to_chip.at[n], dst_ref=rin_ref.at[n], send_sem=ici_send.at[n],
                                                recv_sem=ici_recv.at[n], device_id=(*chips[n], c), device_id_type=MESH)

        def side(k, mine):
            peer, peer_idx = _peer(k)
            src_slab, dst_slab = (peer_idx, me) if mine else (me, peer_idx)
            pairs = [(dout_ref.at[src_slab], rout_ref.at[dst_slab]), (sm_ref, rsm_ref.at[dst_slab])]
            return [pltpu.make_async_remote_copy(src_ref=src, dst_ref=dst, send_sem=side_send.at[k - 1, a],
                                                 recv_sem=side_recv.at[k - 1, a], device_id=peer, device_id_type=MESH)
                    for a, (src, dst) in enumerate(pairs)]

        local = [pltpu.make_async_copy(zero_buf, rout_ref.at[me], local_sems.at[0]),
                 pltpu.make_async_copy(sm_ref, rsm_ref.at[me], local_sems.at[1])]

        @pl.when(step == 0)
        def _():
            zero_buf[...] = jnp.zeros_like(zero_buf)
            for cp in local:
                cp.start()
            for k in range(1, N_DEV):
                for cp in side(k, mine=True):
                    cp.start()

        dw = jnp.dot(ht_ref[...], dz_ref[...], preferred_element_type=F32)
        for n in range(4):
            @pl.when(step == n)
            def _(n=n):
                to_sibling[n] = dw.astype(BF16)
                d2d(n).start()

        for n in range(3):
            @pl.when(step == 4 + n)
            def _(n=n):
                d2d(n).wait_recv()
                to_chip[n] = (dw + landed[n].astype(F32)).astype(BF16)
                ici(n).start()

        @pl.when(step == N_DEV - 1)
        def _():
            d2d(3).wait_recv()
            own_ref[...] = dw + landed[3].astype(F32)
            for n in range(3):
                ici(n).wait_recv()
            for k in range(1, N_DEV):
                for cp in side(k, mine=False):
                    cp.wait_recv()
            for n in range(4):
                d2d(n).wait_send()
            for n in range(3):
                ici(n).wait_send()
            for k in range(1, N_DEV):
                for cp in side(k, mine=True):
                    cp.wait_send()
            for cp in local:
                cp.wait()

    x, y, c = _coords()
    chip_order = [(1 - x, y), (x, 1 - y), (1 - x, 1 - y), (x, y)]
    cols = jnp.stack([4 * px + 2 * py + pc for pc in (1 - c, c) for px, py in chip_order]).astype(jnp.int32)
    slab = (D_MODEL, SHARD_IN)
    grid_spec = pltpu.PrefetchScalarGridSpec(
        num_scalar_prefetch=1, grid=(N_DEV,),
        in_specs=[pl.BlockSpec((D_MODEL, seq), lambda s, cols: (0, 0), pipeline_mode=pl.Buffered(1)),
                  pl.BlockSpec((seq, SHARD_IN), lambda s, cols: (0, cols[s])), HBM_SPEC, HBM_SPEC],
        out_specs=[pl.BlockSpec(slab, lambda s, cols: (0, 0)), HBM_SPEC, HBM_SPEC, HBM_SPEC],
        scratch_shapes=[pltpu.VMEM((4, *slab), BF16), pltpu.VMEM((4, *slab), BF16), pltpu.VMEM((3, *slab), BF16),
                        pltpu.VMEM((SHARD_OUT, D_MODEL), BF16),
                        pltpu.SemaphoreType.DMA((4,)), pltpu.SemaphoreType.DMA((4,)),
                        pltpu.SemaphoreType.DMA((3,)), pltpu.SemaphoreType.DMA((3,)),
                        pltpu.SemaphoreType.DMA((N_DEV - 1, 2)), pltpu.SemaphoreType.DMA((N_DEV - 1, 2)),
                        pltpu.SemaphoreType.DMA((2,))])
    return pl.pallas_call(
        body, name="dw_in_rs", grid_spec=grid_spec,
        out_shape=[jax.ShapeDtypeStruct(slab, F32),
                   jax.ShapeDtypeStruct((3, *slab), BF16),
                   jax.ShapeDtypeStruct((N_DEV, SHARD_OUT, D_MODEL), BF16),
                   jax.ShapeDtypeStruct((N_DEV, 8, D_MODEL), F32)],
        compiler_params=_params(1),
    )(cols, ht, dz, dw_out, small)


def _adamw_math(w, g, m, v):
    m = ADAM_B1 * m + (1.0 - ADAM_B1) * g
    v = ADAM_B2 * v + (1.0 - ADAM_B2) * (g * g)
    m_hat = m / (1.0 - ADAM_B1 ** ADAM_STEP)
    v_hat = v / (1.0 - ADAM_B2 ** ADAM_STEP)
    delta = -ADAM_LR * (m_hat / (jnp.sqrt(v_hat) + ADAM_EPS) + ADAM_WD * w)
    return delta, m, v


def _sum_slabs(ref, first=None):
    total = ref[0].astype(F32) if first is None else first + ref[0].astype(F32)
    for s in range(1, ref.shape[0]):
        total = total + ref[s].astype(F32)
    return total


def _adamw_slabs(parts, own, w, m, v, name, tr):
    rows, cols = w.shape
    tile = pl.BlockSpec((tr, cols), lambda i: (i, 0))

    def body(p_ref, *refs):
        own_ref = refs[0] if own is not None else None
        w_ref, m_ref, v_ref, g_ref, d_ref, nm_ref, nv_ref = refs[-7:]
        g = _sum_slabs(p_ref, None if own_ref is None else own_ref[...])
        g_ref[...] = g
        d_ref[...], nm_ref[...], nv_ref[...] = _adamw_math(w_ref[...], g, m_ref[...], v_ref[...])

    extra = [] if own is None else [own]
    return pl.pallas_call(
        body, name=name, grid=(rows // tr,),
        in_specs=[pl.BlockSpec((parts.shape[0], tr, cols), lambda i: (0, i, 0))] + [tile] * (len(extra) + 3),
        out_specs=[tile] * 4,
        out_shape=[jax.ShapeDtypeStruct((rows, cols), F32)] * 4,
        compiler_params=_params(1),
    )(parts, *extra, w, m, v)


def _sum_small(parts):
    def body(p_ref, out_ref):
        out_ref[...] = _sum_slabs(p_ref)

    return pl.pallas_call(body, name="sum_small", out_shape=jax.ShapeDtypeStruct(parts.shape[1:], F32))(parts)


def _adamw_whole(g, w, m, v, name):
    def body(g_ref, w_ref, m_ref, v_ref, d_ref, nm_ref, nv_ref):
        d_ref[...], nm_ref[...], nv_ref[...] = _adamw_math(w_ref[...], g_ref[...], m_ref[...], v_ref[...])

    return pl.pallas_call(body, name=name, out_shape=[jax.ShapeDtypeStruct(w.shape, F32)] * 3)(g, w, m, v)


def kernel(x, norm_pre_g, w_in, conv_w, w_out, norm_post_g, loss_target, m_norm_pre_g, m_w_in, m_conv_w, m_w_out,
           m_norm_post_g, v_norm_pre_g, v_w_in, v_conv_w, v_w_out, v_norm_post_g):
    n_conv = CONV_W // N_DEV
    w_in_g, w_out_g, conv_g = _ag_weights(w_in, w_out, conv_w)
    conv_full = conv_g[:, 0:3, 0:n_conv].transpose(1, 0, 2).reshape(3, CONV_W)
    grad_x, ht, dz, dw_out, dw_out_bf, small = _local_step(x[0], loss_target[0], norm_pre_g, norm_post_g, w_in_g,
                                                           w_out_g.reshape(D_MODEL, D_MODEL), conv_full)
    own_in, r_in, r_out, r_small = _dw_in_rs(ht, dz, dw_out_bf.reshape(N_DEV, SHARD_OUT, D_MODEL), small)
    me = 4 * lax.axis_index("x") + 2 * lax.axis_index("y") + lax.axis_index("c")
    own_out = lax.dynamic_index_in_dim(dw_out.reshape(N_DEV, SHARD_OUT, D_MODEL), me, keepdims=False)
    g_in, d_in, nm_in, nv_in = _adamw_slabs(r_in, own_in, w_in, m_w_in, v_w_in, "adamw_in", 256)
    g_out, d_out, nm_out, nv_out = _adamw_slabs(r_out, own_out, w_out, m_w_out, v_w_out, "adamw_out", SHARD_OUT)
    sums = _sum_small(r_small)
    g_pre, g_post, loss = sums[0], sums[1], sums[2, 0]
    g_conv = lax.dynamic_slice(sums[3:6, 0:CONV_W], (0, me * n_conv), (3, n_conv))
    vec = lambda a: a.reshape(1, D_MODEL)
    d_pre, nm_pre, nv_pre = _adamw_whole(vec(g_pre), vec(norm_pre_g), vec(m_norm_pre_g), vec(v_norm_pre_g), "adamw_pre")
    d_post, nm_post, nv_post = _adamw_whole(vec(g_post), vec(norm_post_g), vec(m_norm_post_g), vec(v_norm_post_g),
                                            "adamw_post")
    d_conv, nm_conv, nv_conv = _adamw_whole(g_conv, conv_w, m_conv_w, v_conv_w, "adamw_conv")
    flat = lambda a: a.reshape(D_MODEL)
    return (loss, grad_x[None], g_pre, g_in, g_conv, g_out, g_post,
            flat(d_pre), d_in, d_conv, d_out, flat(d_post),
            flat(nm_pre), nm_in, nm_conv, nm_out, flat(nm_post),
            flat(nv_pre), nv_in, nv_conv, nv_out, flat(nv_post))
```

```python
import functools

import jax
import jax.numpy as jnp
import numpy as np
from jax import lax
from jax.experimental import pallas as pl
from jax.experimental.pallas import tpu as pltpu

F32 = jnp.float32
BF16 = jnp.bfloat16

D_MODEL = 1024
HEAD_DIM = 64
ATTN_W = 768
CONV_W = 256
IN_W = 4096
REST_W = IN_W - 3 * ATTN_W
BLK = 128
N_DEV = 8
SHARD_IN = IN_W // N_DEV
SHARD_OUT = D_MODEL // N_DEV
DILATIONS = (1, 4, 16)
ROPE_THETA = 10000.0
NORM_EPS = 1e-6
NEG = -1e30

ADAM_LR = 0.001
ADAM_B1 = 0.9
ADAM_B2 = 0.999
ADAM_EPS = 1e-08
ADAM_WD = 0.01
ADAM_STEP = 10

VMEM_LIMIT = 56 * 1024 * 1024
MESH = pl.DeviceIdType.MESH


def _params(n_grid):
    return pltpu.CompilerParams(dimension_semantics=("arbitrary",) * n_grid, vmem_limit_bytes=VMEM_LIMIT)


def _resident(shape):
    zeros = (0,) * len(shape)
    return pl.BlockSpec(shape, lambda *_: zeros, pipeline_mode=pl.Buffered(1))


def _sigmoid(a):
    return 1.0 / (1.0 + jnp.exp(-a))


def _swap_halves(t, first_half):
    return jnp.where(first_half, pltpu.roll(t, BLK - 32, 1), pltpu.roll(t, 32, 1))


def _rope_tables(seq, tm):
    half = HEAD_DIM // 2
    inv_freq = ROPE_THETA ** (-jnp.arange(half, dtype=F32) * 2.0 / HEAD_DIM)
    freq = jnp.concatenate([inv_freq] * 4)
    sign = jnp.concatenate([-jnp.ones(half, F32), jnp.ones(half, F32)] * 2)
    starts = (jnp.arange(seq // tm) * tm).astype(F32)[:, None] * freq[None, :]
    rows = jnp.arange(tm).astype(F32)[:, None] * freq[None, :]
    slab = lambda a: jnp.broadcast_to(a[:, None, :], (seq // tm, 8, BLK))
    return slab(jnp.cos(starts)), slab(jnp.sin(starts) * sign), jnp.cos(rows), jnp.sin(rows) * sign


def _rope_specs(tm):
    return [pl.BlockSpec((1, 8, BLK), lambda i: (i, 0, 0))] * 2 + [_resident((tm, BLK))] * 2


def _tile_rope(cos_start, sin_start, cos_row, sin_row):
    ca, sa, cb, sb = cos_start[0, 0:1, :], sin_start[0, 0:1, :], cos_row[...], sin_row[...]
    return ca * cb - sa * sb, sa * cb + ca * sb


N_CHUNK = ATTN_W // BLK


def _lanes(r, c):
    return slice(r * ATTN_W + c * BLK, r * ATTN_W + (c + 1) * BLK)


def _to_residues(src, chunk0, dst_refs, tmp, rows, dtype):
    assert DILATIONS == (1, 4, 16)
    dst1, dst4, dst16 = dst_refs
    n4, n16 = rows // 4, rows // 16
    for c in range(N_CHUNK):
        dst1[:, _lanes(0, c)] = src[chunk0 + c].astype(dtype)
        for r1 in range(4):
            tmp[c, r1 * n4:(r1 + 1) * n4, :] = src[chunk0 + c, pl.ds(r1, n4, stride=4), :]
        for r1 in range(4):
            dst4[:, _lanes(r1, c)] = tmp[c, r1 * n4:(r1 + 1) * n4, :].astype(dtype)
            for r2 in range(4):
                dst16[:, _lanes(4 * r2 + r1, c)] = tmp[c, pl.ds(r1 * n4 + r2, n16, stride=4), :].astype(dtype)


def _from_residue(src_ref, dst, dil, rows, accumulate, tmp=None):
    n4, n16 = rows // 4, rows // 16

    def put(where, piece):
        if accumulate:
            dst[where] += piece
        else:
            dst[where] = piece

    for c in range(N_CHUNK):
        if dil == 1:
            put((c,), src_ref[:, _lanes(0, c)].astype(F32))
            continue
        for r1 in range(4):
            if dil == 4:
                piece = src_ref[:, _lanes(r1, c)].astype(F32)
            else:
                for r2 in range(4):
                    tmp[c, pl.ds(r1 * n4 + r2, n16, stride=4), :] = src_ref[:, _lanes(4 * r2 + r1, c)].astype(F32)
                piece = tmp[c, r1 * n4:(r1 + 1) * n4, :]
            put((c, pl.ds(r1, n4, stride=4), slice(None)), piece)


def _residue_spec(tm, dil):
    return pl.BlockSpec((tm // dil, dil * ATTN_W), lambda i: (i, 0))


def _residue_shape(seq, dil, dtype):
    return jax.ShapeDtypeStruct((seq // dil, dil * ATTN_W), dtype)


def _fwd_in(x, g_pre, w_in_g, tm=512):
    seq = x.shape[0]
    n_dil = len(DILATIONS)

    def body(x_ref, g_ref, w_ref, ca_ref, sa_ref, cb_ref, sb_ref, *rest):
        qkv_refs, (zr_ref, ht_ref, qkv_scr, tmp) = rest[:3 * n_dil], rest[3 * n_dil:]
        xv = x_ref[...]
        r = lax.rsqrt(jnp.mean(xv * xv, axis=-1, keepdims=True) + NORM_EPS)
        hf = (xv * r) * g_ref[...]
        ht_ref[...] = hf.T.astype(BF16)
        h = hf.astype(BF16)
        cos, sin = _tile_rope(ca_ref, sa_ref, cb_ref, sb_ref)
        first_half = (lax.broadcasted_iota(jnp.int32, (tm, BLK), 1) & 32) == 0

        def rope(t):
            return t * cos + _swap_halves(t, first_half) * sin

        def project(j):
            return jnp.dot(h, w_ref[j], preferred_element_type=F32)

        def place(j, zj):
            for n in range(SHARD_IN // BLK):
                chunk, t = j * (SHARD_IN // BLK) + n, zj[:, n * BLK:(n + 1) * BLK]
                if chunk < N_CHUNK:
                    qkv_scr[chunk] = rope(t) * HEAD_DIM ** -0.5
                elif chunk < 2 * N_CHUNK:
                    qkv_scr[chunk] = rope(t)
                elif chunk < 3 * N_CHUNK:
                    qkv_scr[chunk] = t
                else:
                    zr_ref[:, (chunk - 3 * N_CHUNK) * BLK:(chunk - 3 * N_CHUNK + 1) * BLK] = t.astype(BF16)

        ahead = project(0)
        for j in range(N_DEV):
            zj = ahead
            if j + 1 < N_DEV:
                ahead = project(j + 1)
            place(j, zj)
            for a in range(3):
                if (a + 1) * N_CHUNK - 1 in range(j * (SHARD_IN // BLK), (j + 1) * (SHARD_IN // BLK)):
                    _to_residues(qkv_scr, a * N_CHUNK, [qkv_refs[3 * n + a] for n in range(n_dil)], tmp, tm, BF16)

    row = lambda w: pl.BlockSpec((tm, w), lambda i: (i, 0))
    outs = pl.pallas_call(
        body, name="fwd_in", grid=(seq // tm,),
        in_specs=[row(D_MODEL), _resident((1, D_MODEL)), _resident((N_DEV, D_MODEL, SHARD_IN))] + _rope_specs(tm),
        out_specs=[_residue_spec(tm, dil) for dil in DILATIONS for _ in range(3)]
        + [row(REST_W), pl.BlockSpec((D_MODEL, tm), lambda i: (0, i))],
        out_shape=[_residue_shape(seq, dil, BF16) for dil in DILATIONS for _ in range(3)]
        + [jax.ShapeDtypeStruct((seq, REST_W), BF16), jax.ShapeDtypeStruct((D_MODEL, seq), BF16)],
        scratch_shapes=[pltpu.VMEM((3 * N_CHUNK, tm, BLK), F32), pltpu.VMEM((N_CHUNK, tm, BLK), F32)],
        compiler_params=_params(1),
    )(x, g_pre.reshape(1, D_MODEL), w_in_g, *_rope_tables(seq, tm))
    qkv = [tuple(outs[3 * n:3 * n + 3]) for n in range(n_dil)]
    return qkv, outs[3 * n_dil], outs[3 * n_dil + 1]


def _band_bias(first_block):
    kj = lax.broadcasted_iota(jnp.int32, (2 * BLK, BLK), 0)
    qi = lax.broadcasted_iota(jnp.int32, (2 * BLK, BLK), 1)
    valid = (kj >= qi) & (kj <= qi + BLK)
    bias = jnp.where(valid, 0.0, NEG).astype(BF16)
    bias_first = jnp.where(valid & (kj >= BLK), 0.0, NEG).astype(BF16)
    onehot = ((kj & (BLK - 1)) == qi).astype(F32).astype(BF16)
    return onehot, bias, jnp.where(first_block, bias_first, bias)


def _stack_heads(t, head0):
    del head0
    keep0 = (lax.broadcasted_iota(jnp.int32, t.shape, 1) < HEAD_DIM).astype(F32).astype(BF16)
    return jnp.concatenate([t * keep0, t * (1 - keep0)], axis=0)


def _unstack_heads(t2, head0):
    return jnp.where(head0, t2[:BLK], t2[BLK:])


def _rows_per_head(a, head0):
    b = pltpu.roll(a, HEAD_DIM, 1)
    rows = jnp.concatenate([jnp.where(head0, a, b), jnp.where(head0, b, a)], axis=0)
    return jnp.concatenate([rows, rows], axis=1)


BLOCKS_PER_STEP = 32


def _attn_specs(length, dil):
    n_blocks = length // BLK
    tb = min(BLOCKS_PER_STEP, n_blocks)
    nc = BLOCKS_PER_STEP // tb
    assert (dil * N_CHUNK) % nc == 0 and n_blocks % tb == 0
    tile = pl.BlockSpec((tb * BLK, nc * BLK), lambda c, t: (t, c))
    prev = pl.BlockSpec((BLK, nc * BLK), lambda c, t: (jnp.maximum(t * tb - 1, 0), c))
    grid = (dil * N_CHUNK // nc, n_blocks // tb)
    return tb, nc, tile, prev, grid


def _load_keys(cat, prev_ref, cur_ref):
    cat[0:BLK] = prev_ref[...]
    cat[BLK:] = cur_ref[...]


def _attn_fwd(q, k, v, dil):
    length = q.shape[0]
    tb, nc, tile, prev, grid = _attn_specs(length, dil)

    def body(q_ref, kc_ref, kp_ref, vc_ref, vp_ref, o_ref, lse_ref, kcat, vcat):
        _load_keys(kcat, kp_ref, kc_ref)
        _load_keys(vcat, vp_ref, vc_ref)
        head0 = lax.broadcasted_iota(jnp.int32, (BLK, BLK), 1) < HEAD_DIM
        onehot, bias, bias_start = _band_bias(pl.program_id(1) == 0)
        ones = jnp.ones((2 * BLK, BLK), BF16)
        def scores(c, j):
            rows, cols = slice(j * BLK, (j + 1) * BLK), slice(c * BLK, (c + 1) * BLK)
            q2 = jnp.concatenate([_stack_heads(q_ref[rows, cols], head0), onehot], axis=1)
            kk = jnp.concatenate([kcat[j * BLK:(j + 2) * BLK, cols], bias_start if j == 0 else bias], axis=1)
            return (lax.dot_general(q2, kk, (((1,), (1,)), ((), ())), preferred_element_type=F32),)

        def probabilities(c, j, s):
            m = jnp.max(s, axis=1, keepdims=True)
            return m, jnp.exp(s - m).astype(BF16)

        def outputs(c, j, m, p):
            rows, cols = slice(j * BLK, (j + 1) * BLK), slice(c * BLK, (c + 1) * BLK)
            vv = jnp.concatenate([vcat[j * BLK:(j + 2) * BLK, cols], ones], axis=1)
            pv = jnp.dot(p, vv, preferred_element_type=F32)
            den = pv[:, BLK:]
            o_ref[rows, cols] = _unstack_heads(pv[:, :BLK] / den, head0).astype(BF16)
            lse_ref[rows, cols] = _unstack_heads(m + jnp.log(den), head0)

        units = [(c, j) for c in range(nc) for j in range(tb)]
        stage1, stage2 = {}, {}
        for n in range(len(units) + 2):
            if n < len(units):
                stage1[n] = scores(*units[n])
            if 0 <= n - 1 < len(units):
                stage2[n - 1] = probabilities(*units[n - 1], *stage1.pop(n - 1))
            if 0 <= n - 2 < len(units):
                outputs(*units[n - 2], *stage2.pop(n - 2))

    return pl.pallas_call(
        body, name=f"attn_fwd_d{dil}", grid=grid,
        in_specs=[tile, tile, prev, tile, prev], out_specs=[tile, tile],
        out_shape=[jax.ShapeDtypeStruct(q.shape, BF16), jax.ShapeDtypeStruct(q.shape, F32)],
        scratch_shapes=[pltpu.VMEM(((tb + 1) * BLK, nc * BLK), BF16)] * 2,
        compiler_params=_params(2),
    )(q, k, k, v, v)


def _attn_bwd(q, k, v, do, lse, delta, dil):
    length = q.shape[0]
    tb, nc, tile, prev, grid = _attn_specs(length, dil)
    whole = pl.BlockSpec((length, nc * BLK), lambda c, t: (0, c))

    def body(q_ref, do_ref, lse_ref, dl_ref, kc_ref, kp_ref, vc_ref, vp_ref, dq_ref, dk_ref, dv_ref, kcat, vcat):
        t = pl.program_id(1)
        _load_keys(kcat, kp_ref, kc_ref)
        _load_keys(vcat, vp_ref, vc_ref)
        head0 = lax.broadcasted_iota(jnp.int32, (BLK, BLK), 1) < HEAD_DIM
        onehot, bias, bias_start = _band_bias(t == 0)

        def scores(c, j):
            rows, cols = slice(j * BLK, (j + 1) * BLK), slice(c * BLK, (c + 1) * BLK)
            q2 = _stack_heads(q_ref[rows, cols], head0)
            do2 = _stack_heads(do_ref[rows, cols], head0)
            kk = kcat[j * BLK:(j + 2) * BLK, cols]
            s = lax.dot_general(jnp.concatenate([q2, onehot], axis=1),
                                jnp.concatenate([kk, bias_start if j == 0 else bias], axis=1),
                                (((1,), (1,)), ((), ())), preferred_element_type=F32)
            dp = lax.dot_general(do2, vcat[j * BLK:(j + 2) * BLK, cols], (((1,), (1,)), ((), ())),
                                 preferred_element_type=F32)
            return q2, do2, kk, s, dp

        def probabilities(c, j, q2, do2, kk, s, dp):
            rows, cols = slice(j * BLK, (j + 1) * BLK), slice(c * BLK, (c + 1) * BLK)
            p = jnp.exp(s - _rows_per_head(lse_ref[rows, cols], head0))
            ds = (p * (dp - _rows_per_head(dl_ref[rows, cols].astype(F32), head0))).astype(BF16)
            return q2, do2, kk, p.astype(BF16), ds

        def gradients(c, j, q2, do2, kk, p, ds):
            rows, cols = slice(j * BLK, (j + 1) * BLK), slice(c * BLK, (c + 1) * BLK)
            dq2 = jnp.dot(ds, kk, preferred_element_type=F32)
            dq_ref[rows, cols] = (_unstack_heads(dq2, head0) * HEAD_DIM ** -0.5).astype(BF16)
            dk2 = lax.dot_general(ds, q2, (((0,), (0,)), ((), ())), preferred_element_type=F32)
            dv2 = lax.dot_general(p, do2, (((0,), (0,)), ((), ())), preferred_element_type=F32)
            own = pl.ds(pl.multiple_of((t * tb + j) * BLK, BLK), BLK)
            dk_ref[own, cols] = dk2[BLK:].astype(BF16)
            dv_ref[own, cols] = dv2[BLK:].astype(BF16)

            def add_to_previous():
                before = pl.ds(pl.multiple_of((t * tb + j - 1) * BLK, BLK), BLK)
                dk_ref[before, cols] = (dk_ref[before, cols].astype(F32) + dk2[:BLK]).astype(BF16)
                dv_ref[before, cols] = (dv_ref[before, cols].astype(F32) + dv2[:BLK]).astype(BF16)

            if j == 0:
                pl.when(t > 0)(add_to_previous)
            else:
                add_to_previous()

        units = [(c, j) for c in range(nc) for j in range(tb)]
        stage1 = {0: scores(*units[0])}
        for n in range(len(units)):
            stage2 = probabilities(*units[n], *stage1.pop(n))
            if n + 1 < len(units):
                stage1[n + 1] = scores(*units[n + 1])
            gradients(*units[n], *stage2)

    return pl.pallas_call(
        body, name=f"attn_bwd_d{dil}", grid=grid,
        in_specs=[tile, tile, tile, tile, tile, prev, tile, prev], out_specs=[tile, whole, whole],
        out_shape=[jax.ShapeDtypeStruct(q.shape, BF16)] * 3,
        scratch_shapes=[pltpu.VMEM(((tb + 1) * BLK, nc * BLK), BF16)] * 2,
        compiler_params=_params(2),
    )(q, do, lse, delta, k, k, v, v)


HALO = 16


def _halo_specs(tm, seq):
    before = lambda w: pl.BlockSpec((HALO, w), lambda i: (jnp.maximum(i * (tm // HALO) - 1, 0), 0))
    after = lambda w: pl.BlockSpec((HALO, w), lambda i: (jnp.minimum((i + 1) * (tm // HALO), seq // HALO - 1), 0))
    return before, after


def _conv_taps(u, before, tm):
    row = lax.broadcasted_iota(jnp.int32, u.shape, 0)
    last, last2 = before[HALO - 1:HALO, :], before[HALO - 2:HALO - 1, :]
    u1 = jnp.where(row == 0, last, pltpu.roll(u, 1, 0))
    u2 = jnp.where(row == 0, last2, jnp.where(row == 1, last, pltpu.roll(u, 2, 0)))
    return u1, u2


def _attn_combine(o_parts, lse_parts, zr, conv_w, tm=256):
    seq = zr.shape[0]
    a0, h0, b0, c0, g0 = 0, ATTN_W, ATTN_W + CONV_W, ATTN_W + 2 * CONV_W, ATTN_W + 3 * CONV_W

    def body(o1, o2, o3, l1, l2, l3, zr_ref, zp_ref, w_ref, mixed_ref, o_ref, lse1, lse2, lse3, *scr):
        i = pl.program_id(0)
        for src, dst, dil in zip((o2, o3, l2, l3), scr[:4], DILATIONS[1:] * 2):
            _from_residue(src, dst, dil, tm, accumulate=False, tmp=scr[5])
        for c in range(N_CHUNK):
            cols = slice(c * BLK, (c + 1) * BLK)
            la, lb, lc = l1[:, cols], scr[2][c], scr[3][c]
            top = jnp.maximum(jnp.maximum(la, lb), lc)
            ea, eb, ec = jnp.exp(la - top), jnp.exp(lb - top), jnp.exp(lc - top)
            den = ea + eb + ec
            o = (ea / den) * o1[:, cols].astype(F32) + (eb / den) * scr[0][c] + (ec / den) * scr[1][c]
            o_ref[:, cols] = o.astype(BF16)
            scr[4][c] = top + jnp.log(den)
            ga = zr_ref[:, cols].astype(F32)
            mixed_ref[:, cols] = (o * (ga * _sigmoid(ga))).astype(BF16)
        _to_residues(scr[4], 0, (lse1, lse2, lse3), scr[5], tm, F32)
        part = lambda ref, lo, hi: ref[:, lo:hi].astype(F32)
        u = part(zr_ref, c0, g0) * part(zr_ref, h0, b0)
        before = jnp.where(i > 0, part(zp_ref, c0, g0) * part(zp_ref, h0, b0), 0.0)
        u1, u2 = _conv_taps(u, before, tm)
        y = u2 * w_ref[0:1, :] + u1 * w_ref[1:2, :] + u * w_ref[2:3, :]
        gc = part(zr_ref, g0, REST_W)
        mixed_ref[:, ATTN_W:] = ((part(zr_ref, b0, c0) * y) * (gc * _sigmoid(gc))).astype(BF16)

    row = lambda w: pl.BlockSpec((tm, w), lambda i: (i, 0))
    before, _ = _halo_specs(tm, seq)
    views = [_residue_spec(tm, dil) for dil in DILATIONS]
    outs = pl.pallas_call(
        body, name="attn_combine", grid=(seq // tm,),
        in_specs=views * 2 + [row(REST_W), before(REST_W), _resident((3, CONV_W))],
        out_specs=[row(D_MODEL), row(ATTN_W)] + views,
        out_shape=[jax.ShapeDtypeStruct((seq, D_MODEL), BF16), jax.ShapeDtypeStruct((seq, ATTN_W), BF16)]
        + [_residue_shape(seq, dil, F32) for dil in DILATIONS],
        scratch_shapes=[pltpu.VMEM((N_CHUNK, tm, BLK), F32)] * 6,
        compiler_params=_params(1),
    )(*o_parts, *lse_parts, zr, zr, conv_w)
    return outs[0], outs[1], outs[2:]


def _out_loss_bwd(mixed, w_out_g, x, target, g_post, tm=512):
    seq = x.shape[0]

    def body(mx_ref, w_ref, x_ref, t_ref, g_ref, dout_ref, dmx_ref, dw_ref, dwb_ref, st_ref):
        i = pl.program_id(0)
        mx = mx_ref[...]
        y = jnp.dot(mx, w_ref[...], preferred_element_type=F32)
        r = lax.rsqrt(jnp.mean(y * y, axis=-1, keepdims=True) + NORM_EPS)
        yhat = y * r
        g = g_ref[...]
        err = (x_ref[...] + yhat * g) - t_ref[...]
        dn = err * (1.0 / D_MODEL)
        dout_ref[...] = dn
        tg = dn * g
        dy = (r * (tg - yhat * jnp.mean(tg * yhat, axis=-1, keepdims=True))).astype(BF16)
        dmx_ref[...] = lax.dot_general(dy, w_ref[...], (((1,), (1,)), ((), ())),
                                       preferred_element_type=F32).astype(BF16)
        dw = lax.dot_general(mx, dy, (((0,), (0,)), ((), ())), preferred_element_type=F32)
        gsum = jnp.sum(dn * yhat, axis=0, keepdims=True)
        lsum = jnp.broadcast_to(0.5 / D_MODEL * jnp.sum(err * err), (1, D_MODEL))

        @pl.when(i == 0)
        def _():
            dw_ref[...] = dw
            st_ref[...] = jnp.zeros_like(st_ref)
            st_ref[0:1, :] = gsum
            st_ref[1:2, :] = lsum

        @pl.when(i > 0)
        def _():
            dw_ref[...] += dw
            st_ref[0:1, :] += gsum
            st_ref[1:2, :] += lsum

        @pl.when(i == seq // tm - 1)
        def _():
            dwb_ref[...] = dw_ref[...].astype(BF16)

    row = lambda w: pl.BlockSpec((tm, w), lambda i: (i, 0))
    whole = pl.BlockSpec((D_MODEL, D_MODEL), lambda i: (0, 0))
    return pl.pallas_call(
        body, name="out_loss_bwd", grid=(seq // tm,),
        in_specs=[row(D_MODEL), _resident((D_MODEL, D_MODEL)), row(D_MODEL), row(D_MODEL), _resident((1, D_MODEL))],
        out_specs=[row(D_MODEL), row(D_MODEL), whole, whole, pl.BlockSpec((8, D_MODEL), lambda i: (0, 0))],
        out_shape=[jax.ShapeDtypeStruct((seq, D_MODEL), F32), jax.ShapeDtypeStruct((seq, D_MODEL), BF16),
                   jax.ShapeDtypeStruct((D_MODEL, D_MODEL), F32), jax.ShapeDtypeStruct((D_MODEL, D_MODEL), BF16),
                   jax.ShapeDtypeStruct((8, D_MODEL), F32)],
        compiler_params=_params(1),
    )(mixed, w_out_g, x, target, g_post.reshape(1, D_MODEL))


def _head_sum(prod, same_head):
    hi = prod.astype(BF16)
    lo = (prod - hi.astype(F32)).astype(BF16)
    return (jnp.dot(hi, same_head, preferred_element_type=F32) + jnp.dot(lo, same_head, preferred_element_type=F32))


def _gate_bwd(dmixed, zr, o, conv_w, tm=256):
    seq = zr.shape[0]
    n_tiles = seq // tm
    n_dil = len(DILATIONS)
    a0, h0, b0, c0, g0 = 0, ATTN_W, ATTN_W + CONV_W, ATTN_W + 2 * CONV_W, ATTN_W + 3 * CONV_W

    def body(dm_ref, dmn_ref, zr_ref, zp_ref, zn_ref, o_ref, w_ref, *rest):
        do_refs, dl_refs = rest[:n_dil], rest[n_dil:2 * n_dil]
        dz_ref, dw_ref, do_scr, dl_scr, tmp = rest[2 * n_dil:]
        i = pl.program_id(0)
        part = lambda ref, lo, hi: ref[:, lo:hi].astype(F32)
        ga = part(zr_ref, a0, h0)
        sg = _sigmoid(ga)
        dattn = part(dm_ref, 0, ATTN_W)
        ov = o_ref[...].astype(F32)
        do = dattn * (ga * sg)
        dz_ref[:, a0:h0] = (dattn * ov * (sg * (1.0 + ga * (1.0 - sg)))).astype(BF16)
        li = lax.broadcasted_iota(jnp.int32, (BLK, BLK), 0) // HEAD_DIM
        lj = lax.broadcasted_iota(jnp.int32, (BLK, BLK), 1) // HEAD_DIM
        same_head = (li == lj).astype(BF16)
        prod = do * ov
        for c in range(N_CHUNK):
            cols = slice(c * BLK, (c + 1) * BLK)
            do_scr[c] = do[:, cols]
            dl_scr[c] = _head_sum(prod[:, cols], same_head)
        _to_residues(do_scr, 0, do_refs, tmp, tm, BF16)
        _to_residues(dl_scr, 0, dl_refs, tmp, tm, BF16)

        ch, cb, cc, gc = (part(zr_ref, lo, hi) for lo, hi in ((h0, b0), (b0, c0), (c0, g0), (g0, REST_W)))
        u = cc * ch
        before = jnp.where(i > 0, part(zp_ref, c0, g0) * part(zp_ref, h0, b0), 0.0)
        u1, u2 = _conv_taps(u, before, tm)
        w0, w1, w2 = w_ref[0:1, :], w_ref[1:2, :], w_ref[2:3, :]
        y = u2 * w0 + u1 * w1 + u * w2
        sc = _sigmoid(gc)
        silu_c = gc * sc
        dconv = part(dm_ref, ATTN_W, D_MODEL)
        dz_ref[:, b0:c0] = (dconv * y * silu_c).astype(BF16)
        dz_ref[:, g0:] = (dconv * (cb * y) * (sc * (1.0 + gc * (1.0 - sc)))).astype(BF16)
        dy = dconv * cb * silu_c
        gn = part(zn_ref, g0, REST_W)
        after = jnp.where(i < n_tiles - 1,
                          part(dmn_ref, ATTN_W, D_MODEL) * part(zn_ref, b0, c0) * (gn * _sigmoid(gn)), 0.0)
        row = lax.broadcasted_iota(jnp.int32, dy.shape, 0)
        nxt, nxt2 = after[0:1, :], after[1:2, :]
        dy1 = jnp.where(row == tm - 1, nxt, pltpu.roll(dy, tm - 1, 0))
        dy2 = jnp.where(row == tm - 1, nxt2, jnp.where(row == tm - 2, nxt, pltpu.roll(dy, tm - 2, 0)))
        du = dy * w2 + dy1 * w1 + dy2 * w0
        dz_ref[:, c0:g0] = (du * ch).astype(BF16)
        dz_ref[:, h0:b0] = (du * cc).astype(BF16)
        dws = [jnp.sum(dy * u2, axis=0, keepdims=True), jnp.sum(dy * u1, axis=0, keepdims=True),
               jnp.sum(dy * u, axis=0, keepdims=True)]

        @pl.when(i == 0)
        def _():
            dw_ref[...] = jnp.zeros_like(dw_ref)

        for n, part in enumerate(dws):
            dw_ref[n:n + 1, :] += part

    row_spec = lambda w: pl.BlockSpec((tm, w), lambda i: (i, 0))
    before, after = _halo_specs(tm, seq)
    views = [_residue_spec(tm, dil) for dil in DILATIONS]
    outs = pl.pallas_call(
        body, name="gate_bwd", grid=(n_tiles,),
        in_specs=[row_spec(D_MODEL), after(D_MODEL), row_spec(REST_W), before(REST_W), after(REST_W),
                  row_spec(ATTN_W), _resident((3, CONV_W))],
        out_specs=views * 2 + [row_spec(REST_W), pl.BlockSpec((8, CONV_W), lambda i: (0, 0))],
        out_shape=[_residue_shape(seq, dil, BF16) for dil in DILATIONS] * 2
        + [jax.ShapeDtypeStruct((seq, REST_W), BF16), jax.ShapeDtypeStruct((8, CONV_W), F32)],
        scratch_shapes=[pltpu.VMEM((N_CHUNK, tm, BLK), F32)] * 3,
        compiler_params=_params(1),
    )(dmixed, dmixed, zr, zr, zr, o, conv_w)
    return outs[:n_dil], outs[n_dil:2 * n_dil], outs[2 * n_dil], outs[2 * n_dil + 1]


def _in_bwd(dqs, dks, dvs, dzr, x, d_out, g_pre, w_in_g, tm=256):
    seq = x.shape[0]

    def body(q1, q2, q3, k1, k2, k3, v1, v2, v3, dzr_ref, ca_ref, sa_ref, cb_ref, sb_ref, x_ref, dout_ref, g_ref,
             w_ref, dz_ref, gx_ref, st_ref, dq_scr, dk_scr, dv_scr, tmp):
        i = pl.program_id(0)
        cos, sin = _tile_rope(ca_ref, sa_ref, cb_ref, sb_ref)
        first_half = (lax.broadcasted_iota(jnp.int32, (tm, BLK), 1) & 32) == 0
        streams = [((q1, q2, q3), dq_scr), ((k1, k2, k3), dk_scr), ((v1, v2, v3), dv_scr)]
        per_slab = SHARD_IN // BLK

        def unrope(t):
            return t * cos - _swap_halves(t, first_half) * sin

        def assemble(j):
            for chunk in range(j * per_slab, (j + 1) * per_slab):
                a, c = divmod(chunk, N_CHUNK)
                if a < 3 and c == 0:
                    parts, total = streams[a]
                    for n, dil in enumerate(DILATIONS):
                        _from_residue(parts[n], total, dil, tm, accumulate=n > 0, tmp=tmp)
                if a < 2:
                    val = unrope(streams[a][1][c]).astype(BF16)
                elif a == 2:
                    val = dv_scr[c].astype(BF16)
                else:
                    val = dzr_ref[:, (chunk - 3 * N_CHUNK) * BLK:(chunk - 3 * N_CHUNK + 1) * BLK]
                dz_ref[:, chunk * BLK:(chunk + 1) * BLK] = val
            return dz_ref[:, j * SHARD_IN:(j + 1) * SHARD_IN]

        ahead = assemble(0)
        dh = None
        for j in range(N_DEV):
            slab = ahead
            if j + 1 < N_DEV:
                ahead = assemble(j + 1)
            part = lax.dot_general(slab, w_ref[j], (((1,), (1,)), ((), ())), preferred_element_type=F32)
            dh = part if dh is None else dh + part
        xv = x_ref[...]
        r = lax.rsqrt(jnp.mean(xv * xv, axis=-1, keepdims=True) + NORM_EPS)
        xhat = xv * r
        tg = dh * g_ref[...]
        gx_ref[...] = dout_ref[...] + r * (tg - xhat * jnp.mean(tg * xhat, axis=-1, keepdims=True))
        gsum = jnp.sum(dh * xhat, axis=0, keepdims=True)

        @pl.when(i == 0)
        def _():
            st_ref[...] = jnp.zeros_like(st_ref)

        st_ref[0:1, :] += gsum

    row = lambda w: pl.BlockSpec((tm, w), lambda i: (i, 0))
    return pl.pallas_call(
        body, name="in_bwd", grid=(seq // tm,),
        in_specs=[_residue_spec(tm, dil) for dil in DILATIONS] * 3
        + [row(REST_W)] + _rope_specs(tm) + [row(D_MODEL), row(D_MODEL), _resident((1, D_MODEL)),
                                             _resident((N_DEV, D_MODEL, SHARD_IN))],
        out_specs=[row(IN_W), row(D_MODEL), pl.BlockSpec((8, D_MODEL), lambda i: (0, 0))],
        out_shape=[jax.ShapeDtypeStruct((seq, IN_W), BF16), jax.ShapeDtypeStruct((seq, D_MODEL), F32),
                   jax.ShapeDtypeStruct((8, D_MODEL), F32)],
        scratch_shapes=[pltpu.VMEM((N_CHUNK, tm, BLK), F32)] * 4,
        compiler_params=_params(1),
    )(*dqs, *dks, *dvs, dzr, *_rope_tables(seq, tm), x, d_out, g_pre.reshape(1, D_MODEL), w_in_g)


def _local_step(x, target, g_pre, g_post, w_in_g, w_out_g, conv_w):
    qkv, zr, ht = _fwd_in(x, g_pre, w_in_g)
    parts = [_attn_fwd(*qkv[n], dil) for n, dil in enumerate(DILATIONS)]
    mixed, o, lse = _attn_combine([p[0] for p in parts], [p[1] for p in parts], zr, conv_w)
    d_out, dmixed, dw_out, dw_out_bf, st_post = _out_loss_bwd(mixed, w_out_g, x, target, g_post)
    do, delta, dzr, dconv = _gate_bwd(dmixed, zr, o, conv_w)
    grads = [_attn_bwd(*qkv[n], do[n], lse[n], delta[n], dil) for n, dil in enumerate(DILATIONS)]
    dz, grad_x, st_pre = _in_bwd([g[0] for g in grads], [g[1] for g in grads], [g[2] for g in grads], dzr,
                                 x, d_out, g_pre, w_in_g)
    conv_rows = jnp.pad(dconv[0:3], ((0, 0), (0, D_MODEL - CONV_W)))
    small = jnp.concatenate([st_pre[0:1], st_post[0:2], conv_rows, jnp.zeros((2, D_MODEL), F32)], axis=0)
    return grad_x, ht, dz, dw_out, dw_out_bf, small


def _coords():
    return lax.axis_index("x"), lax.axis_index("y"), lax.axis_index("c")


def _peer(k):
    x, y, c = _coords()
    px = 1 - x if k & 4 else x
    py = 1 - y if k & 2 else y
    pc = 1 - c if k & 1 else c
    return (px, py, pc), 4 * px + 2 * py + pc


HBM_SPEC = pl.BlockSpec(memory_space=pltpu.HBM)
VMEM_SPEC = pl.BlockSpec(memory_space=pltpu.VMEM)


def _ag_weights(w_in, w_out, conv_w):
    def body(win_ref, wout_ref, cw_ref, gin_ref, gout_ref, gcw_ref, win_bf, wout_bf, cw_pad, send_sems, recv_sems,
             local_sems):
        x, y, c = _coords()
        me, sibling = (x, y, c), (x, y, 1 - c)
        flip = lambda v, yes: v + yes - 2 * v * yes
        x_nbr, y_nbr, diagonal = (1 - x, y, c), (x, 1 - y, c), (1 - x, 1 - y, c)
        relay_from = (flip(x, 1 - c), flip(y, c), c)
        relay_to = (flip(x, c), flip(y, 1 - c), c)
        slab = lambda px, py, pc: 4 * px + 2 * py + pc
        win_bf[...] = win_ref[...].astype(BF16)
        wout_bf[...] = wout_ref[...].astype(BF16)
        cw_pad[...] = jnp.zeros_like(cw_pad)
        cw_pad[0:3, 0:CONV_W // N_DEV] = cw_ref[...]
        mine = [win_bf, wout_bf, cw_pad]
        gathered = [gin_ref, gout_ref, gcw_ref]

        def copies(k, block, to, own=False):
            return [pltpu.make_async_remote_copy(src_ref=mine[a] if own else gathered[a].at[slab(*block)],
                                                 dst_ref=gathered[a].at[slab(*block)], send_sem=send_sems.at[k, a],
                                                 recv_sem=recv_sems.at[k, a], device_id=to, device_id_type=MESH)
                    for a in range(3)]

        local = [pltpu.make_async_copy(mine[a], gathered[a].at[slab(*me)], local_sems.at[a]) for a in range(3)]
        for cp in local:
            cp.start()
        started = copies(0, me, sibling, own=True) + copies(1, me, x_nbr, own=True) + copies(2, me, y_nbr, own=True)
        for cp in started:
            cp.start()
        for cp in copies(1, x_nbr, me) + copies(2, y_nbr, me):
            cp.wait_recv()
        onward = copies(3, relay_from, relay_to) + copies(4, x_nbr, sibling) + copies(5, y_nbr, sibling)
        for cp in onward:
            cp.start()
        for cp in copies(3, diagonal, me):
            cp.wait_recv()
        last = copies(6, diagonal, sibling)
        for cp in last:
            cp.start()
        for cp in copies(0, sibling, me):
            cp.wait_recv()
        for k, origin in ((4, (1 - x, y, 1 - c)), (5, (x, 1 - y, 1 - c)), (6, (1 - x, 1 - y, 1 - c))):
            for cp in copies(k, origin, me):
                cp.wait_recv()
        for cp in started + onward + last:
            cp.wait_send()
        for cp in local:
            cp.wait()

    return pl.pallas_call(
        body, name="ag_weights",
        in_specs=[VMEM_SPEC, VMEM_SPEC, VMEM_SPEC], out_specs=[HBM_SPEC, HBM_SPEC, HBM_SPEC],
        out_shape=[jax.ShapeDtypeStruct((N_DEV, D_MODEL, SHARD_IN), BF16),
                   jax.ShapeDtypeStruct((N_DEV, SHARD_OUT, D_MODEL), BF16),
                   jax.ShapeDtypeStruct((N_DEV, 8, BLK), F32)],
        scratch_shapes=[pltpu.VMEM((D_MODEL, SHARD_IN), BF16), pltpu.VMEM((SHARD_OUT, D_MODEL), BF16),
                        pltpu.VMEM((8, BLK), F32), pltpu.SemaphoreType.DMA((N_DEV - 1, 3)),
                        pltpu.SemaphoreType.DMA((N_DEV - 1, 3)), pltpu.SemaphoreType.DMA((3,))],
        compiler_params=pltpu.CompilerParams(vmem_limit_bytes=VMEM_LIMIT),
    )(w_in, w_out, conv_w)


def _dw_in_rs(ht, dz, dw_out, small):
    seq = dz.shape[0]

    def body(cols_ref, ht_ref, dz_ref, dout_ref, sm_ref, own_ref, rin_ref, rout_ref, rsm_ref, to_sibling, landed,
             to_chip, zero_buf, d2d_send, d2d_recv, ici_send, ici_recv, side_send, side_recv, local_sems):
        del cols_ref
        step = pl.program_id(0)
        x, y, c = _coords()
        me = 4 * x + 2 * y + c
        sibling = (x, y, 1 - c)
        chips = [(1 - x, y), (x, 1 - y), (1 - x, 1 - y)]

        def d2d(n):
            return pltpu.make_async_remote_copy(src_ref=to_sibling.at[n], dst_ref=landed.at[n], send_sem=d2d_send.at[n],
                                                recv_sem=d2d_recv.at[n], device_id=sibling, device_id_type=MESH)

        def ici(n):
            return pltpu.make_async_remote_copy(src_ref=to_chip.at[n], dst_ref=rin_ref.at[n], send_sem=ici_send.at[n],
                                                recv_sem=ici_recv.at[n], device_id=(*chips[n], c), device_id_type=MESH)

        def side(k, mine):
            peer, peer_idx = _peer(k)
            src_slab, dst_slab = (peer_idx, me) if mine else (me, peer_idx)
            pairs = [(dout_ref.at[src_slab], rout_ref.at[dst_slab]), (sm_ref, rsm_ref.at[dst_slab])]
            return [pltpu.make_async_remote_copy(src_ref=src, dst_ref=dst, send_sem=side_send.at[k - 1, a],
                                                 recv_sem=side_recv.at[k - 1, a], device_id=peer, device_id_type=MESH)
                    for a, (src, dst) in enumerate(pairs)]

        local = [pltpu.make_async_copy(zero_buf, rout_ref.at[me], local_sems.at[0]),
                 pltpu.make_async_copy(sm_ref, rsm_ref.at[me], local_sems.at[1])]

        @pl.when(step == 0)
        def _():
            zero_buf[...] = jnp.zeros_like(zero_buf)
            for cp in local:
                cp.start()
            for k in range(1, N_DEV):
                for cp in side(k, mine=True):
                    cp.start()

        dw = jnp.dot(ht_ref[...], dz_ref[...], preferred_element_type=F32)
        for n, at in zip(range(4), (0, 1, 2, N_DEV - 2)):
            @pl.when(step == at)
            def _(n=n):
                to_sibling[n] = dw.astype(BF16)
                d2d(n).start()

        for n in range(3):
            @pl.when(step == 3 + n)
            def _(n=n):
                d2d(n).wait_recv()
                to_chip[n] = (dw + landed[n].astype(F32)).astype(BF16)
                ici(n).start()

        @pl.when(step == N_DEV - 1)
        def _():
            d2d(3).wait_recv()
            own_ref[...] = dw + landed[3].astype(F32)
            for n in range(3):
                ici(n).wait_recv()
            for k in range(1, N_DEV):
                for cp in side(k, mine=False):
                    cp.wait_recv()
            for n in range(4):
                d2d(n).wait_send()
            for n in range(3):
                ici(n).wait_send()
            for k in range(1, N_DEV):
                for cp in side(k, mine=True):
                    cp.wait_send()
            for cp in local:
                cp.wait()

    x, y, c = _coords()
    others = [(1 - x, y), (x, 1 - y), (1 - x, 1 - y)]
    order = [(*chip, 1 - c) for chip in others] + [(*chip, c) for chip in others] + [(x, y, 1 - c), (x, y, c)]
    cols = jnp.stack([4 * px + 2 * py + pc for px, py, pc in order]).astype(jnp.int32)
    slab = (D_MODEL, SHARD_IN)
    grid_spec = pltpu.PrefetchScalarGridSpec(
        num_scalar_prefetch=1, grid=(N_DEV,),
        in_specs=[pl.BlockSpec((D_MODEL, seq), lambda s, cols: (0, 0), pipeline_mode=pl.Buffered(1)),
                  pl.BlockSpec((seq, SHARD_IN), lambda s, cols: (0, cols[s])), HBM_SPEC, HBM_SPEC],
        out_specs=[pl.BlockSpec(slab, lambda s, cols: (0, 0)), HBM_SPEC, HBM_SPEC, HBM_SPEC],
        scratch_shapes=[pltpu.VMEM((4, *slab), BF16), pltpu.VMEM((4, *slab), BF16), pltpu.VMEM((3, *slab), BF16),
                        pltpu.VMEM((SHARD_OUT, D_MODEL), BF16),
                        pltpu.SemaphoreType.DMA((4,)), pltpu.SemaphoreType.DMA((4,)),
                        pltpu.SemaphoreType.DMA((3,)), pltpu.SemaphoreType.DMA((3,)),
                        pltpu.SemaphoreType.DMA((N_DEV - 1, 2)), pltpu.SemaphoreType.DMA((N_DEV - 1, 2)),
                        pltpu.SemaphoreType.DMA((2,))])
    return pl.pallas_call(
        body, name="dw_in_rs", grid_spec=grid_spec,
        out_shape=[jax.ShapeDtypeStruct(slab, F32),
                   jax.ShapeDtypeStruct((3, *slab), BF16),
                   jax.ShapeDtypeStruct((N_DEV, SHARD_OUT, D_MODEL), BF16),
                   jax.ShapeDtypeStruct((N_DEV, 8, D_MODEL), F32)],
        compiler_params=_params(1),
    )(cols, ht, dz, dw_out, small)


def _adamw_math(w, g, m, v):
    m = ADAM_B1 * m + (1.0 - ADAM_B1) * g
    v = ADAM_B2 * v + (1.0 - ADAM_B2) * (g * g)
    m_hat = m / (1.0 - ADAM_B1 ** ADAM_STEP)
    v_hat = v / (1.0 - ADAM_B2 ** ADAM_STEP)
    delta = -ADAM_LR * (m_hat / (jnp.sqrt(v_hat) + ADAM_EPS) + ADAM_WD * w)
    return delta, m, v


def _sum_slabs(ref, first=None):
    total = ref[0].astype(F32) if first is None else first + ref[0].astype(F32)
    for s in range(1, ref.shape[0]):
        total = total + ref[s].astype(F32)
    return total


def _adamw_slabs(parts, own, w, m, v, name, tr):
    rows, cols = w.shape
    tile = pl.BlockSpec((tr, cols), lambda i: (i, 0))

    def body(p_ref, *refs):
        own_ref = refs[0] if own is not None else None
        w_ref, m_ref, v_ref, g_ref, d_ref, nm_ref, nv_ref = refs[-7:]
        g = _sum_slabs(p_ref, None if own_ref is None else own_ref[...])
        g_ref[...] = g
        d_ref[...], nm_ref[...], nv_ref[...] = _adamw_math(w_ref[...], g, m_ref[...], v_ref[...])

    extra = [] if own is None else [own]
    return pl.pallas_call(
        body, name=name, grid=(rows // tr,),
        in_specs=[pl.BlockSpec((parts.shape[0], tr, cols), lambda i: (0, i, 0))] + [tile] * (len(extra) + 3),
        out_specs=[tile] * 4,
        out_shape=[jax.ShapeDtypeStruct((rows, cols), F32)] * 4,
        compiler_params=_params(1),
    )(parts, *extra, w, m, v)


def _sum_small(parts):
    def body(p_ref, out_ref):
        out_ref[...] = _sum_slabs(p_ref)

    return pl.pallas_call(body, name="sum_small", out_shape=jax.ShapeDtypeStruct(parts.shape[1:], F32))(parts)


def _adamw_whole(g, w, m, v, name):
    def body(g_ref, w_ref, m_ref, v_ref, d_ref, nm_ref, nv_ref):
        d_ref[...], nm_ref[...], nv_ref[...] = _adamw_math(w_ref[...], g_ref[...], m_ref[...], v_ref[...])

    return pl.pallas_call(body, name=name, out_shape=[jax.ShapeDtypeStruct(w.shape, F32)] * 3)(g, w, m, v)


def kernel(x, norm_pre_g, w_in, conv_w, w_out, norm_post_g, loss_target, m_norm_pre_g, m_w_in, m_conv_w, m_w_out,
           m_norm_post_g, v_norm_pre_g, v_w_in, v_conv_w, v_w_out, v_norm_post_g):
    n_conv = CONV_W // N_DEV
    w_in_g, w_out_g, conv_g = _ag_weights(w_in, w_out, conv_w)
    conv_full = conv_g[:, 0:3, 0:n_conv].transpose(1, 0, 2).reshape(3, CONV_W)
    grad_x, ht, dz, dw_out, dw_out_bf, small = _local_step(x[0], loss_target[0], norm_pre_g, norm_post_g, w_in_g,
                                                           w_out_g.reshape(D_MODEL, D_MODEL), conv_full)
    own_in, r_in, r_out, r_small = _dw_in_rs(ht, dz, dw_out_bf.reshape(N_DEV, SHARD_OUT, D_MODEL), small)
    me = 4 * lax.axis_index("x") + 2 * lax.axis_index("y") + lax.axis_index("c")
    own_out = lax.dynamic_index_in_dim(dw_out.reshape(N_DEV, SHARD_OUT, D_MODEL), me, keepdims=False)
    g_in, d_in, nm_in, nv_in = _adamw_slabs(r_in, own_in, w_in, m_w_in, v_w_in, "adamw_in", 256)
    g_out, d_out, nm_out, nv_out = _adamw_slabs(r_out, own_out, w_out, m_w_out, v_w_out, "adamw_out", SHARD_OUT)
    sums = _sum_small(r_small)
    g_pre, g_post, loss = sums[0], sums[1], sums[2, 0]
    g_conv = lax.dynamic_slice(sums[3:6, 0:CONV_W], (0, me * n_conv), (3, n_conv))
    vec = lambda a: a.reshape(1, D_MODEL)
    d_pre, nm_pre, nv_pre = _adamw_whole(vec(g_pre), vec(norm_pre_g), vec(m_norm_pre_g), vec(v_norm_pre_g), "adamw_pre")
    d_post, nm_post, nv_post = _adamw_whole(vec(g_post), vec(norm_post_g), vec(m_norm_post_g), vec(v_norm_post_g),
                                            "adamw_post")
    d_conv, nm_conv, nv_conv = _adamw_whole(g_conv, conv_w, m_conv_w, v_conv_w, "adamw_conv")
    flat = lambda a: a.reshape(D_MODEL)
    return (loss, grad_x[None], g_pre, g_in, g_conv, g_out, g_post,
            flat(d_pre), d_in, d_conv, d_out, flat(d_post),
            flat(nm_pre), nm_in, nm_conv, nm_out, flat(nm_post),
            flat(nv_pre), nv_in, nv_conv, nv_out, flat(nv_post))
```

```python
import functools

import jax
import jax.numpy as jnp
import numpy as np
from jax import lax
from jax.experimental import pallas as pl
from jax.experimental.pallas import tpu as pltpu

F32 = jnp.float32
BF16 = jnp.bfloat16

D_MODEL = 1024
HEAD_DIM = 64
ATTN_W = 768
CONV_W = 256
IN_W = 4096
REST_W = IN_W - 3 * ATTN_W
BLK = 128
N_DEV = 8
SHARD_IN = IN_W // N_DEV
SHARD_OUT = D_MODEL // N_DEV
DILATIONS = (1, 4, 16)
ROPE_THETA = 10000.0
NORM_EPS = 1e-6
NEG = -1e30

ADAM_LR = 0.001
ADAM_B1 = 0.9
ADAM_B2 = 0.999
ADAM_EPS = 1e-08
ADAM_WD = 0.01
ADAM_STEP = 10

VMEM_LIMIT = 56 * 1024 * 1024
MESH = pl.DeviceIdType.MESH


def _params(n_grid):
    return pltpu.CompilerParams(dimension_semantics=("arbitrary",) * n_grid, vmem_limit_bytes=VMEM_LIMIT)


def _resident(shape):
    zeros = (0,) * len(shape)
    return pl.BlockSpec(shape, lambda *_: zeros, pipeline_mode=pl.Buffered(1))


def _sigmoid(a):
    return 1.0 / (1.0 + jnp.exp(-a))


def _swap_halves(t, first_half):
    return jnp.where(first_half, pltpu.roll(t, BLK - 32, 1), pltpu.roll(t, 32, 1))


def _rope_tables(seq, tm):
    half = HEAD_DIM // 2
    inv_freq = ROPE_THETA ** (-jnp.arange(half, dtype=F32) * 2.0 / HEAD_DIM)
    freq = jnp.concatenate([inv_freq] * 4)
    sign = jnp.concatenate([-jnp.ones(half, F32), jnp.ones(half, F32)] * 2)
    starts = (jnp.arange(seq // tm) * tm).astype(F32)[:, None] * freq[None, :]
    rows = jnp.arange(tm).astype(F32)[:, None] * freq[None, :]
    slab = lambda a: jnp.broadcast_to(a[:, None, :], (seq // tm, 8, BLK))
    return slab(jnp.cos(starts)), slab(jnp.sin(starts) * sign), jnp.cos(rows), jnp.sin(rows) * sign


def _rope_specs(tm):
    return [pl.BlockSpec((1, 8, BLK), lambda i: (i, 0, 0))] * 2 + [_resident((tm, BLK))] * 2


def _tile_rope(cos_start, sin_start, cos_row, sin_row):
    ca, sa, cb, sb = cos_start[0, 0:1, :], sin_start[0, 0:1, :], cos_row[...], sin_row[...]
    return ca * cb - sa * sb, sa * cb + ca * sb


N_CHUNK = ATTN_W // BLK


def _lanes(r, c):
    return slice(r * ATTN_W + c * BLK, r * ATTN_W + (c + 1) * BLK)


def _to_residues(src, chunk0, dst_refs, tmp, rows, dtype):
    assert DILATIONS == (1, 4, 16)
    dst1, dst4, dst16 = dst_refs
    n4, n16 = rows // 4, rows // 16
    for c in range(N_CHUNK):
        dst1[:, _lanes(0, c)] = src[chunk0 + c].astype(dtype)
        for r1 in range(4):
            tmp[c, r1 * n4:(r1 + 1) * n4, :] = src[chunk0 + c, pl.ds(r1, n4, stride=4), :]
        for r1 in range(4):
            dst4[:, _lanes(r1, c)] = tmp[c, r1 * n4:(r1 + 1) * n4, :].astype(dtype)
            for r2 in range(4):
                dst16[:, _lanes(4 * r2 + r1, c)] = tmp[c, pl.ds(r1 * n4 + r2, n16, stride=4), :].astype(dtype)


def _from_residue(src_ref, dst, dil, rows, accumulate, tmp=None):
    n4, n16 = rows // 4, rows // 16

    def put(where, piece):
        if accumulate:
            dst[where] += piece
        else:
            dst[where] = piece

    for c in range(N_CHUNK):
        if dil == 1:
            put((c,), src_ref[:, _lanes(0, c)].astype(F32))
            continue
        for r1 in range(4):
            if dil == 4:
                piece = src_ref[:, _lanes(r1, c)].astype(F32)
            else:
                for r2 in range(4):
                    tmp[c, pl.ds(r1 * n4 + r2, n16, stride=4), :] = src_ref[:, _lanes(4 * r2 + r1, c)].astype(F32)
                piece = tmp[c, r1 * n4:(r1 + 1) * n4, :]
            put((c, pl.ds(r1, n4, stride=4), slice(None)), piece)


def _residue_spec(tm, dil):
    return pl.BlockSpec((tm // dil, dil * ATTN_W), lambda i: (i, 0))


def _residue_shape(seq, dil, dtype):
    return jax.ShapeDtypeStruct((seq // dil, dil * ATTN_W), dtype)


def _fwd_in(x, g_pre, w_in_g, tm=512):
    seq = x.shape[0]
    n_dil = len(DILATIONS)

    def body(x_ref, g_ref, w_ref, ca_ref, sa_ref, cb_ref, sb_ref, *rest):
        qkv_refs, (zr_ref, ht_ref, qkv_scr, tmp) = rest[:3 * n_dil], rest[3 * n_dil:]
        xv = x_ref[...]
        r = lax.rsqrt(jnp.mean(xv * xv, axis=-1, keepdims=True) + NORM_EPS)
        hf = (xv * r) * g_ref[...]
        ht_ref[...] = hf.T.astype(BF16)
        h = hf.astype(BF16)
        cos, sin = _tile_rope(ca_ref, sa_ref, cb_ref, sb_ref)
        first_half = (lax.broadcasted_iota(jnp.int32, (tm, BLK), 1) & 32) == 0

        def rope(t):
            return t * cos + _swap_halves(t, first_half) * sin

        def project(j):
            return jnp.dot(h, w_ref[j], preferred_element_type=F32)

        def place(j, zj):
            for n in range(SHARD_IN // BLK):
                chunk, t = j * (SHARD_IN // BLK) + n, zj[:, n * BLK:(n + 1) * BLK]
                if chunk < N_CHUNK:
                    qkv_scr[chunk] = rope(t) * HEAD_DIM ** -0.5
                elif chunk < 2 * N_CHUNK:
                    qkv_scr[chunk] = rope(t)
                elif chunk < 3 * N_CHUNK:
                    qkv_scr[chunk] = t
                else:
                    zr_ref[:, (chunk - 3 * N_CHUNK) * BLK:(chunk - 3 * N_CHUNK + 1) * BLK] = t.astype(BF16)

        ahead = project(0)
        for j in range(N_DEV):
            zj = ahead
            if j + 1 < N_DEV:
                ahead = project(j + 1)
            place(j, zj)
            for a in range(3):
                if (a + 1) * N_CHUNK - 1 in range(j * (SHARD_IN // BLK), (j + 1) * (SHARD_IN // BLK)):
                    _to_residues(qkv_scr, a * N_CHUNK, [qkv_refs[3 * n + a] for n in range(n_dil)], tmp, tm, BF16)

    row = lambda w: pl.BlockSpec((tm, w), lambda i: (i, 0))
    outs = pl.pallas_call(
        body, name="fwd_in", grid=(seq // tm,),
        in_specs=[row(D_MODEL), _resident((1, D_MODEL)), _resident((N_DEV, D_MODEL, SHARD_IN))] + _rope_specs(tm),
        out_specs=[_residue_spec(tm, dil) for dil in DILATIONS for _ in range(3)]
        + [row(REST_W), pl.BlockSpec((D_MODEL, tm), lambda i: (0, i))],
        out_shape=[_residue_shape(seq, dil, BF16) for dil in DILATIONS for _ in range(3)]
        + [jax.ShapeDtypeStruct((seq, REST_W), BF16), jax.ShapeDtypeStruct((D_MODEL, seq), BF16)],
        scratch_shapes=[pltpu.VMEM((3 * N_CHUNK, tm, BLK), F32), pltpu.VMEM((N_CHUNK, tm, BLK), F32)],
        compiler_params=_params(1),
    )(x, g_pre.reshape(1, D_MODEL), w_in_g, *_rope_tables(seq, tm))
    qkv = [tuple(outs[3 * n:3 * n + 3]) for n in range(n_dil)]
    return qkv, outs[3 * n_dil], outs[3 * n_dil + 1]


def _band_bias(first_block):
    kj = lax.broadcasted_iota(jnp.int32, (2 * BLK, BLK), 0)
    qi = lax.broadcasted_iota(jnp.int32, (2 * BLK, BLK), 1)
    valid = (kj >= qi) & (kj <= qi + BLK)
    bias = jnp.where(valid, 0.0, NEG).astype(BF16)
    bias_first = jnp.where(valid & (kj >= BLK), 0.0, NEG).astype(BF16)
    onehot = ((kj & (BLK - 1)) == qi).astype(F32).astype(BF16)
    return onehot, bias, jnp.where(first_block, bias_first, bias)


def _stack_heads(t, head0):
    del head0
    keep0 = (lax.broadcasted_iota(jnp.int32, t.shape, 1) < HEAD_DIM).astype(F32).astype(BF16)
    return jnp.concatenate([t * keep0, t * (1 - keep0)], axis=0)


def _unstack_heads(t2, head0):
    return jnp.where(head0, t2[:BLK], t2[BLK:])


def _rows_per_head(a, head0):
    b = pltpu.roll(a, HEAD_DIM, 1)
    rows = jnp.concatenate([jnp.where(head0, a, b), jnp.where(head0, b, a)], axis=0)
    return jnp.concatenate([rows, rows], axis=1)


BLOCKS_PER_STEP = 32


def _attn_specs(length, dil):
    n_blocks = length // BLK
    tb = min(BLOCKS_PER_STEP, n_blocks)
    nc = BLOCKS_PER_STEP // tb
    assert (dil * N_CHUNK) % nc == 0 and n_blocks % tb == 0
    tile = pl.BlockSpec((tb * BLK, nc * BLK), lambda c, t: (t, c))
    prev = pl.BlockSpec((BLK, nc * BLK), lambda c, t: (jnp.maximum(t * tb - 1, 0), c))
    grid = (dil * N_CHUNK // nc, n_blocks // tb)
    return tb, nc, tile, prev, grid


def _load_keys(cat, prev_ref, cur_ref):
    cat[0:BLK] = prev_ref[...]
    cat[BLK:] = cur_ref[...]


def _attn_fwd(q, k, v, dil):
    length = q.shape[0]
    tb, nc, tile, prev, grid = _attn_specs(length, dil)

    def body(q_ref, kc_ref, kp_ref, vc_ref, vp_ref, o_ref, lse_ref, kcat, vcat):
        _load_keys(kcat, kp_ref, kc_ref)
        _load_keys(vcat, vp_ref, vc_ref)
        head0 = lax.broadcasted_iota(jnp.int32, (BLK, BLK), 1) < HEAD_DIM
        onehot, bias, bias_start = _band_bias(pl.program_id(1) == 0)
        ones = jnp.ones((2 * BLK, BLK), BF16)
        def scores(c, j):
            rows, cols = slice(j * BLK, (j + 1) * BLK), slice(c * BLK, (c + 1) * BLK)
            q2 = jnp.concatenate([_stack_heads(q_ref[rows, cols], head0), onehot], axis=1)
            kk = jnp.concatenate([kcat[j * BLK:(j + 2) * BLK, cols], bias_start if j == 0 else bias], axis=1)
            return (lax.dot_general(q2, kk, (((1,), (1,)), ((), ())), preferred_element_type=F32),)

        def probabilities(c, j, s):
            m = jnp.max(s, axis=1, keepdims=True)
            return m, jnp.exp(s - m).astype(BF16)

        def outputs(c, j, m, p):
            rows, cols = slice(j * BLK, (j + 1) * BLK), slice(c * BLK, (c + 1) * BLK)
            vv = jnp.concatenate([vcat[j * BLK:(j + 2) * BLK, cols], ones], axis=1)
            pv = jnp.dot(p, vv, preferred_element_type=F32)
            den = pv[:, BLK:]
            o_ref[rows, cols] = _unstack_heads(pv[:, :BLK] / den, head0).astype(BF16)
            lse_ref[rows, cols] = _unstack_heads(m + jnp.log(den), head0)

        units = [(c, j) for c in range(nc) for j in range(tb)]
        stage1, stage2 = {}, {}
        for n in range(len(units) + 2):
            if n < len(units):
                stage1[n] = scores(*units[n])
            if 0 <= n - 1 < len(units):
                stage2[n - 1] = probabilities(*units[n - 1], *stage1.pop(n - 1))
            if 0 <= n - 2 < len(units):
                outputs(*units[n - 2], *stage2.pop(n - 2))

    return pl.pallas_call(
        body, name=f"attn_fwd_d{dil}", grid=grid,
        in_specs=[tile, tile, prev, tile, prev], out_specs=[tile, tile],
        out_shape=[jax.ShapeDtypeStruct(q.shape, BF16), jax.ShapeDtypeStruct(q.shape, F32)],
        scratch_shapes=[pltpu.VMEM(((tb + 1) * BLK, nc * BLK), BF16)] * 2,
        compiler_params=_params(2),
    )(q, k, k, v, v)


def _attn_bwd(q, k, v, do, lse, delta, dil):
    length = q.shape[0]
    tb, nc, tile, prev, grid = _attn_specs(length, dil)
    whole = pl.BlockSpec((length, nc * BLK), lambda c, t: (0, c))

    def body(q_ref, do_ref, lse_ref, dl_ref, kc_ref, kp_ref, vc_ref, vp_ref, dq_ref, dk_ref, dv_ref, kcat, vcat):
        t = pl.program_id(1)
        _load_keys(kcat, kp_ref, kc_ref)
        _load_keys(vcat, vp_ref, vc_ref)
        head0 = lax.broadcasted_iota(jnp.int32, (BLK, BLK), 1) < HEAD_DIM
        onehot, bias, bias_start = _band_bias(t == 0)

        def scores(c, j):
            rows, cols = slice(j * BLK, (j + 1) * BLK), slice(c * BLK, (c + 1) * BLK)
            q2 = _stack_heads(q_ref[rows, cols], head0)
            do2 = _stack_heads(do_ref[rows, cols], head0)
            kk = kcat[j * BLK:(j + 2) * BLK, cols]
            s = lax.dot_general(jnp.concatenate([q2, onehot], axis=1),
                                jnp.concatenate([kk, bias_start if j == 0 else bias], axis=1),
                                (((1,), (1,)), ((), ())), preferred_element_type=F32)
            dp = lax.dot_general(do2, vcat[j * BLK:(j + 2) * BLK, cols], (((1,), (1,)), ((), ())),
                                 preferred_element_type=F32)
            return q2, do2, kk, s, dp

        def probabilities(c, j, q2, do2, kk, s, dp):
            rows, cols = slice(j * BLK, (j + 1) * BLK), slice(c * BLK, (c + 1) * BLK)
            p = jnp.exp(s - _rows_per_head(lse_ref[rows, cols], head0))
            ds = (p * (dp - _rows_per_head(dl_ref[rows, cols].astype(F32), head0))).astype(BF16)
            return q2, do2, kk, p.astype(BF16), ds

        def gradients(c, j, q2, do2, kk, p, ds):
            rows, cols = slice(j * BLK, (j + 1) * BLK), slice(c * BLK, (c + 1) * BLK)
            dq2 = jnp.dot(ds, kk, preferred_element_type=F32)
            dq_ref[rows, cols] = (_unstack_heads(dq2, head0) * HEAD_DIM ** -0.5).astype(BF16)
            dk2 = lax.dot_general(ds, q2, (((0,), (0,)), ((), ())), preferred_element_type=F32)
            dv2 = lax.dot_general(p, do2, (((0,), (0,)), ((), ())), preferred_element_type=F32)
            own = pl.ds(pl.multiple_of((t * tb + j) * BLK, BLK), BLK)
            dk_ref[own, cols] = dk2[BLK:].astype(BF16)
            dv_ref[own, cols] = dv2[BLK:].astype(BF16)

            def add_to_previous():
                before = pl.ds(pl.multiple_of((t * tb + j - 1) * BLK, BLK), BLK)
                dk_ref[before, cols] = (dk_ref[before, cols].astype(F32) + dk2[:BLK]).astype(BF16)
                dv_ref[before, cols] = (dv_ref[before, cols].astype(F32) + dv2[:BLK]).astype(BF16)

            if j > 0:
                add_to_previous()
            elif grid[1] > 1:
                pl.when(t > 0)(add_to_previous)

        units = [(c, j) for c in range(nc) for j in range(tb)]
        stage1 = {0: scores(*units[0])}
        for n in range(len(units)):
            stage2 = probabilities(*units[n], *stage1.pop(n))
            if n + 1 < len(units):
                stage1[n + 1] = scores(*units[n + 1])
            gradients(*units[n], *stage2)

    return pl.pallas_call(
        body, name=f"attn_bwd_d{dil}", grid=grid,
        in_specs=[tile, tile, tile, tile, tile, prev, tile, prev], out_specs=[tile, whole, whole],
        out_shape=[jax.ShapeDtypeStruct(q.shape, BF16)] * 3,
        scratch_shapes=[pltpu.VMEM(((tb + 1) * BLK, nc * BLK), BF16)] * 2,
        compiler_params=_params(2),
    )(q, do, lse, delta, k, k, v, v)


HALO = 16


def _halo_specs(tm, seq):
    before = lambda w: pl.BlockSpec((HALO, w), lambda i: (jnp.maximum(i * (tm // HALO) - 1, 0), 0))
    after = lambda w: pl.BlockSpec((HALO, w), lambda i: (jnp.minimum((i + 1) * (tm // HALO), seq // HALO - 1), 0))
    return before, after


def _conv_taps(u, before, tm):
    row = lax.broadcasted_iota(jnp.int32, u.shape, 0)
    last, last2 = before[HALO - 1:HALO, :], before[HALO - 2:HALO - 1, :]
    u1 = jnp.where(row == 0, last, pltpu.roll(u, 1, 0))
    u2 = jnp.where(row == 0, last2, jnp.where(row == 1, last, pltpu.roll(u, 2, 0)))
    return u1, u2


def _attn_combine(o_parts, lse_parts, zr, conv_w, tm=256):
    seq = zr.shape[0]
    a0, h0, b0, c0, g0 = 0, ATTN_W, ATTN_W + CONV_W, ATTN_W + 2 * CONV_W, ATTN_W + 3 * CONV_W

    def body(o1, o2, o3, l1, l2, l3, zr_ref, zp_ref, w_ref, mixed_ref, o_ref, lse1, lse2, lse3, *scr):
        i = pl.program_id(0)
        for src, dst, dil in zip((o2, o3, l2, l3), scr[:4], DILATIONS[1:] * 2):
            _from_residue(src, dst, dil, tm, accumulate=False, tmp=scr[5])
        for c in range(N_CHUNK):
            cols = slice(c * BLK, (c + 1) * BLK)
            la, lb, lc = l1[:, cols], scr[2][c], scr[3][c]
            top = jnp.maximum(jnp.maximum(la, lb), lc)
            ea, eb, ec = jnp.exp(la - top), jnp.exp(lb - top), jnp.exp(lc - top)
            den = ea + eb + ec
            o = (ea / den) * o1[:, cols].astype(F32) + (eb / den) * scr[0][c] + (ec / den) * scr[1][c]
            o_ref[:, cols] = o.astype(BF16)
            scr[4][c] = top + jnp.log(den)
            ga = zr_ref[:, cols].astype(F32)
            mixed_ref[:, cols] = (o * (ga * _sigmoid(ga))).astype(BF16)
        _to_residues(scr[4], 0, (lse1, lse2, lse3), scr[5], tm, F32)
        part = lambda ref, lo, hi: ref[:, lo:hi].astype(F32)
        u = part(zr_ref, c0, g0) * part(zr_ref, h0, b0)
        before = jnp.where(i > 0, part(zp_ref, c0, g0) * part(zp_ref, h0, b0), 0.0)
        u1, u2 = _conv_taps(u, before, tm)
        y = u2 * w_ref[0:1, :] + u1 * w_ref[1:2, :] + u * w_ref[2:3, :]
        gc = part(zr_ref, g0, REST_W)
        mixed_ref[:, ATTN_W:] = ((part(zr_ref, b0, c0) * y) * (gc * _sigmoid(gc))).astype(BF16)

    row = lambda w: pl.BlockSpec((tm, w), lambda i: (i, 0))
    before, _ = _halo_specs(tm, seq)
    views = [_residue_spec(tm, dil) for dil in DILATIONS]
    outs = pl.pallas_call(
        body, name="attn_combine", grid=(seq // tm,),
        in_specs=views * 2 + [row(REST_W), before(REST_W), _resident((3, CONV_W))],
        out_specs=[row(D_MODEL), row(ATTN_W)] + views,
        out_shape=[jax.ShapeDtypeStruct((seq, D_MODEL), BF16), jax.ShapeDtypeStruct((seq, ATTN_W), BF16)]
        + [_residue_shape(seq, dil, F32) for dil in DILATIONS],
        scratch_shapes=[pltpu.VMEM((N_CHUNK, tm, BLK), F32)] * 6,
        compiler_params=_params(1),
    )(*o_parts, *lse_parts, zr, zr, conv_w)
    return outs[0], outs[1], outs[2:]


def _out_loss_bwd(mixed, w_out_g, x, target, g_post, tm=512, n_parts=2):
    seq = x.shape[0]

    def body(mx_ref, w_ref, x_ref, t_ref, g_ref, dout_ref, dmx_ref, dw_ref, dwb_ref, st_ref):
        i = pl.program_id(0)
        g = g_ref[...]
        parts = [slice(n * (tm // n_parts), (n + 1) * (tm // n_parts)) for n in range(n_parts)]

        def project(rows):
            return jnp.dot(mx_ref[rows, :], w_ref[...], preferred_element_type=F32)

        def head(rows, y):
            r = lax.rsqrt(jnp.mean(y * y, axis=-1, keepdims=True) + NORM_EPS)
            yhat = y * r
            err = (x_ref[rows, :] + yhat * g) - t_ref[rows, :]
            dn = err * (1.0 / D_MODEL)
            dout_ref[rows, :] = dn
            tg = dn * g
            dy = (r * (tg - yhat * jnp.mean(tg * yhat, axis=-1, keepdims=True))).astype(BF16)
            dmx_ref[rows, :] = lax.dot_general(dy, w_ref[...], (((1,), (1,)), ((), ())),
                                               preferred_element_type=F32).astype(BF16)
            return dy, jnp.sum(dn * yhat, axis=0, keepdims=True), jnp.sum(err * err)

        ahead, done = project(parts[0]), []
        for n, rows in enumerate(parts):
            y = ahead
            if n + 1 < n_parts:
                ahead = project(parts[n + 1])
            done.append(head(rows, y))
        dy = jnp.concatenate([d[0] for d in done], axis=0)
        dw = lax.dot_general(mx_ref[...], dy, (((0,), (0,)), ((), ())), preferred_element_type=F32)
        gsum = functools.reduce(lambda a, b: a + b, [d[1] for d in done])
        lsum = jnp.broadcast_to(0.5 / D_MODEL * functools.reduce(lambda a, b: a + b, [d[2] for d in done]),
                                (1, D_MODEL))

        @pl.when(i == 0)
        def _():
            dw_ref[...] = dw
            st_ref[...] = jnp.zeros_like(st_ref)
            st_ref[0:1, :] = gsum
            st_ref[1:2, :] = lsum

        @pl.when(i > 0)
        def _():
            dw_ref[...] += dw
            st_ref[0:1, :] += gsum
            st_ref[1:2, :] += lsum

        @pl.when(i == seq // tm - 1)
        def _():
            dwb_ref[...] = dw_ref[...].astype(BF16)

    row = lambda w: pl.BlockSpec((tm, w), lambda i: (i, 0))
    whole = pl.BlockSpec((D_MODEL, D_MODEL), lambda i: (0, 0))
    return pl.pallas_call(
        body, name="out_loss_bwd", grid=(seq // tm,),
        in_specs=[row(D_MODEL), _resident((D_MODEL, D_MODEL)), row(D_MODEL), row(D_MODEL), _resident((1, D_MODEL))],
        out_specs=[row(D_MODEL), row(D_MODEL), whole, whole, pl.BlockSpec((8, D_MODEL), lambda i: (0, 0))],
        out_shape=[jax.ShapeDtypeStruct((seq, D_MODEL), F32), jax.ShapeDtypeStruct((seq, D_MODEL), BF16),
                   jax.ShapeDtypeStruct((D_MODEL, D_MODEL), F32), jax.ShapeDtypeStruct((D_MODEL, D_MODEL), BF16),
                   jax.ShapeDtypeStruct((8, D_MODEL), F32)],
        compiler_params=_params(1),
    )(mixed, w_out_g, x, target, g_post.reshape(1, D_MODEL))


def _head_sum(prod, same_head):
    hi = prod.astype(BF16)
    lo = (prod - hi.astype(F32)).astype(BF16)
    return (jnp.dot(hi, same_head, preferred_element_type=F32) + jnp.dot(lo, same_head, preferred_element_type=F32))


def _gate_bwd(dmixed, zr, o, conv_w, tm=256):
    seq = zr.shape[0]
    n_tiles = seq // tm
    n_dil = len(DILATIONS)
    a0, h0, b0, c0, g0 = 0, ATTN_W, ATTN_W + CONV_W, ATTN_W + 2 * CONV_W, ATTN_W + 3 * CONV_W

    def body(dm_ref, dmn_ref, zr_ref, zp_ref, zn_ref, o_ref, w_ref, *rest):
        do_refs, dl_refs = rest[:n_dil], rest[n_dil:2 * n_dil]
        dz_ref, dw_ref, do_scr, dl_scr, tmp = rest[2 * n_dil:]
        i = pl.program_id(0)
        part = lambda ref, lo, hi: ref[:, lo:hi].astype(F32)
        ga = part(zr_ref, a0, h0)
        sg = _sigmoid(ga)
        dattn = part(dm_ref, 0, ATTN_W)
        ov = o_ref[...].astype(F32)
        do = dattn * (ga * sg)
        dz_ref[:, a0:h0] = (dattn * ov * (sg * (1.0 + ga * (1.0 - sg)))).astype(BF16)
        li = lax.broadcasted_iota(jnp.int32, (BLK, BLK), 0) // HEAD_DIM
        lj = lax.broadcasted_iota(jnp.int32, (BLK, BLK), 1) // HEAD_DIM
        same_head = (li == lj).astype(BF16)
        prod = do * ov
        for c in range(N_CHUNK):
            cols = slice(c * BLK, (c + 1) * BLK)
            do_scr[c] = do[:, cols]
            dl_scr[c] = _head_sum(prod[:, cols], same_head)
        _to_residues(do_scr, 0, do_refs, tmp, tm, BF16)
        _to_residues(dl_scr, 0, dl_refs, tmp, tm, BF16)

        ch, cb, cc, gc = (part(zr_ref, lo, hi) for lo, hi in ((h0, b0), (b0, c0), (c0, g0), (g0, REST_W)))
        u = cc * ch
        before = jnp.where(i > 0, part(zp_ref, c0, g0) * part(zp_ref, h0, b0), 0.0)
        u1, u2 = _conv_taps(u, before, tm)
        w0, w1, w2 = w_ref[0:1, :], w_ref[1:2, :], w_ref[2:3, :]
        y = u2 * w0 + u1 * w1 + u * w2
        sc = _sigmoid(gc)
        silu_c = gc * sc
        dconv = part(dm_ref, ATTN_W, D_MODEL)
        dz_ref[:, b0:c0] = (dconv * y * silu_c).astype(BF16)
        dz_ref[:, g0:] = (dconv * (cb * y) * (sc * (1.0 + gc * (1.0 - sc)))).astype(BF16)
        dy = dconv * cb * silu_c
        gn = part(zn_ref, g0, REST_W)
        after = jnp.where(i < n_tiles - 1,
                          part(dmn_ref, ATTN_W, D_MODEL) * part(zn_ref, b0, c0) * (gn * _sigmoid(gn)), 0.0)
        row = lax.broadcasted_iota(jnp.int32, dy.shape, 0)
        nxt, nxt2 = after[0:1, :], after[1:2, :]
        dy1 = jnp.where(row == tm - 1, nxt, pltpu.roll(dy, tm - 1, 0))
        dy2 = jnp.where(row == tm - 1, nxt2, jnp.where(row == tm - 2, nxt, pltpu.roll(dy, tm - 2, 0)))
        du = dy * w2 + dy1 * w1 + dy2 * w0
        dz_ref[:, c0:g0] = (du * ch).astype(BF16)
        dz_ref[:, h0:b0] = (du * cc).astype(BF16)
        dws = [jnp.sum(dy * u2, axis=0, keepdims=True), jnp.sum(dy * u1, axis=0, keepdims=True),
               jnp.sum(dy * u, axis=0, keepdims=True)]

        @pl.when(i == 0)
        def _():
            dw_ref[...] = jnp.zeros_like(dw_ref)

        for n, part in enumerate(dws):
            dw_ref[n:n + 1, :] += part

    row_spec = lambda w: pl.BlockSpec((tm, w), lambda i: (i, 0))
    before, after = _halo_specs(tm, seq)
    views = [_residue_spec(tm, dil) for dil in DILATIONS]
    outs = pl.pallas_call(
        body, name="gate_bwd", grid=(n_tiles,),
        in_specs=[row_spec(D_MODEL), after(D_MODEL), row_spec(REST_W), before(REST_W), after(REST_W),
                  row_spec(ATTN_W), _resident((3, CONV_W))],
        out_specs=views * 2 + [row_spec(REST_W), pl.BlockSpec((8, CONV_W), lambda i: (0, 0))],
        out_shape=[_residue_shape(seq, dil, BF16) for dil in DILATIONS] * 2
        + [jax.ShapeDtypeStruct((seq, REST_W), BF16), jax.ShapeDtypeStruct((8, CONV_W), F32)],
        scratch_shapes=[pltpu.VMEM((N_CHUNK, tm, BLK), F32)] * 3,
        compiler_params=_params(1),
    )(dmixed, dmixed, zr, zr, zr, o, conv_w)
    return outs[:n_dil], outs[n_dil:2 * n_dil], outs[2 * n_dil], outs[2 * n_dil + 1]


def _in_bwd(dqs, dks, dvs, dzr, x, d_out, g_pre, w_in_g, tm=256):
    seq = x.shape[0]

    def body(q1, q2, q3, k1, k2, k3, v1, v2, v3, dzr_ref, ca_ref, sa_ref, cb_ref, sb_ref, x_ref, dout_ref, g_ref,
             w_ref, dz_ref, gx_ref, st_ref, dq_scr, dk_scr, dv_scr, tmp):
        i = pl.program_id(0)
        cos, sin = _tile_rope(ca_ref, sa_ref, cb_ref, sb_ref)
        first_half = (lax.broadcasted_iota(jnp.int32, (tm, BLK), 1) & 32) == 0
        streams = [((q1, q2, q3), dq_scr), ((k1, k2, k3), dk_scr), ((v1, v2, v3), dv_scr)]
        per_slab = SHARD_IN // BLK

        def unrope(t):
            return t * cos - _swap_halves(t, first_half) * sin

        def assemble(j):
            for chunk in range(j * per_slab, (j + 1) * per_slab):
                a, c = divmod(chunk, N_CHUNK)
                if a < 3 and c == 0:
                    parts, total = streams[a]
                    for n, dil in enumerate(DILATIONS):
                        _from_residue(parts[n], total, dil, tm, accumulate=n > 0, tmp=tmp)
                if a < 2:
                    val = unrope(streams[a][1][c]).astype(BF16)
                elif a == 2:
                    val = dv_scr[c].astype(BF16)
                else:
                    val = dzr_ref[:, (chunk - 3 * N_CHUNK) * BLK:(chunk - 3 * N_CHUNK + 1) * BLK]
                dz_ref[:, chunk * BLK:(chunk + 1) * BLK] = val
            return dz_ref[:, j * SHARD_IN:(j + 1) * SHARD_IN]

        ahead = assemble(0)
        dh = None
        for j in range(N_DEV):
            slab = ahead
            if j + 1 < N_DEV:
                ahead = assemble(j + 1)
            part = lax.dot_general(slab, w_ref[j], (((1,), (1,)), ((), ())), preferred_element_type=F32)
            dh = part if dh is None else dh + part
        xv = x_ref[...]
        r = lax.rsqrt(jnp.mean(xv * xv, axis=-1, keepdims=True) + NORM_EPS)
        xhat = xv * r
        tg = dh * g_ref[...]
        gx_ref[...] = dout_ref[...] + r * (tg - xhat * jnp.mean(tg * xhat, axis=-1, keepdims=True))
        gsum = jnp.sum(dh * xhat, axis=0, keepdims=True)

        @pl.when(i == 0)
        def _():
            st_ref[...] = jnp.zeros_like(st_ref)

        st_ref[0:1, :] += gsum

    row = lambda w: pl.BlockSpec((tm, w), lambda i: (i, 0))
    return pl.pallas_call(
        body, name="in_bwd", grid=(seq // tm,),
        in_specs=[_residue_spec(tm, dil) for dil in DILATIONS] * 3
        + [row(REST_W)] + _rope_specs(tm) + [row(D_MODEL), row(D_MODEL), _resident((1, D_MODEL)),
                                             _resident((N_DEV, D_MODEL, SHARD_IN))],
        out_specs=[row(IN_W), row(D_MODEL), pl.BlockSpec((8, D_MODEL), lambda i: (0, 0))],
        out_shape=[jax.ShapeDtypeStruct((seq, IN_W), BF16), jax.ShapeDtypeStruct((seq, D_MODEL), F32),
                   jax.ShapeDtypeStruct((8, D_MODEL), F32)],
        scratch_shapes=[pltpu.VMEM((N_CHUNK, tm, BLK), F32)] * 4,
        compiler_params=_params(1),
    )(*dqs, *dks, *dvs, dzr, *_rope_tables(seq, tm), x, d_out, g_pre.reshape(1, D_MODEL), w_in_g)


def _local_step(x, target, g_pre, g_post, w_in_g, w_out_g, conv_w):
    qkv, zr, ht = _fwd_in(x, g_pre, w_in_g)
    parts = [_attn_fwd(*qkv[n], dil) for n, dil in enumerate(DILATIONS)]
    mixed, o, lse = _attn_combine([p[0] for p in parts], [p[1] for p in parts], zr, conv_w)
    d_out, dmixed, dw_out, dw_out_bf, st_post = _out_loss_bwd(mixed, w_out_g, x, target, g_post)
    do, delta, dzr, dconv = _gate_bwd(dmixed, zr, o, conv_w)
    grads = [_attn_bwd(*qkv[n], do[n], lse[n], delta[n], dil) for n, dil in enumerate(DILATIONS)]
    dz, grad_x, st_pre = _in_bwd([g[0] for g in grads], [g[1] for g in grads], [g[2] for g in grads], dzr,
                                 x, d_out, g_pre, w_in_g)
    conv_rows = jnp.pad(dconv[0:3], ((0, 0), (0, D_MODEL - CONV_W)))
    small = jnp.concatenate([st_pre[0:1], st_post[0:2], conv_rows, jnp.zeros((2, D_MODEL), F32)], axis=0)
    return grad_x, ht, dz, dw_out, dw_out_bf, small


def _coords():
    return lax.axis_index("x"), lax.axis_index("y"), lax.axis_index("c")


def _peer(k):
    x, y, c = _coords()
    px = 1 - x if k & 4 else x
    py = 1 - y if k & 2 else y
    pc = 1 - c if k & 1 else c
    return (px, py, pc), 4 * px + 2 * py + pc


HBM_SPEC = pl.BlockSpec(memory_space=pltpu.HBM)
VMEM_SPEC = pl.BlockSpec(memory_space=pltpu.VMEM)


def _ag_weights(w_in, w_out, conv_w):
    def body(win_ref, wout_ref, cw_ref, gin_ref, gout_ref, gcw_ref, win_bf, wout_bf, cw_pad, send_sems, recv_sems,
             local_sems):
        x, y, c = _coords()
        me, sibling = (x, y, c), (x, y, 1 - c)
        flip = lambda v, yes: v + yes - 2 * v * yes
        x_nbr, y_nbr, diagonal = (1 - x, y, c), (x, 1 - y, c), (1 - x, 1 - y, c)
        relay_from = (flip(x, 1 - c), flip(y, c), c)
        relay_to = (flip(x, c), flip(y, 1 - c), c)
        slab = lambda px, py, pc: 4 * px + 2 * py + pc
        win_bf[...] = win_ref[...].astype(BF16)
        wout_bf[...] = wout_ref[...].astype(BF16)
        cw_pad[...] = jnp.zeros_like(cw_pad)
        cw_pad[0:3, 0:CONV_W // N_DEV] = cw_ref[...]
        mine = [win_bf, wout_bf, cw_pad]
        gathered = [gin_ref, gout_ref, gcw_ref]

        def copies(k, block, to, own=False):
            return [pltpu.make_async_remote_copy(src_ref=mine[a] if own else gathered[a].at[slab(*block)],
                                                 dst_ref=gathered[a].at[slab(*block)], send_sem=send_sems.at[k, a],
                                                 recv_sem=recv_sems.at[k, a], device_id=to, device_id_type=MESH)
                    for a in range(3)]

        local = [pltpu.make_async_copy(mine[a], gathered[a].at[slab(*me)], local_sems.at[a]) for a in range(3)]
        for cp in local:
            cp.start()
        started = copies(0, me, sibling, own=True) + copies(1, me, x_nbr, own=True) + copies(2, me, y_nbr, own=True)
        for cp in started:
            cp.start()
        for cp in copies(1, x_nbr, me) + copies(2, y_nbr, me):
            cp.wait_recv()
        onward = copies(3, relay_from, relay_to) + copies(4, x_nbr, sibling) + copies(5, y_nbr, sibling)
        for cp in onward:
            cp.start()
        for cp in copies(3, diagonal, me):
            cp.wait_recv()
        last = copies(6, diagonal, sibling)
        for cp in last:
            cp.start()
        for cp in copies(0, sibling, me):
            cp.wait_recv()
        for k, origin in ((4, (1 - x, y, 1 - c)), (5, (x, 1 - y, 1 - c)), (6, (1 - x, 1 - y, 1 - c))):
            for cp in copies(k, origin, me):
                cp.wait_recv()
        for cp in started + onward + last:
            cp.wait_send()
        for cp in local:
            cp.wait()

    return pl.pallas_call(
        body, name="ag_weights",
        in_specs=[VMEM_SPEC, VMEM_SPEC, VMEM_SPEC], out_specs=[HBM_SPEC, HBM_SPEC, HBM_SPEC],
        out_shape=[jax.ShapeDtypeStruct((N_DEV, D_MODEL, SHARD_IN), BF16),
                   jax.ShapeDtypeStruct((N_DEV, SHARD_OUT, D_MODEL), BF16),
                   jax.ShapeDtypeStruct((N_DEV, 8, BLK), F32)],
        scratch_shapes=[pltpu.VMEM((D_MODEL, SHARD_IN), BF16), pltpu.VMEM((SHARD_OUT, D_MODEL), BF16),
                        pltpu.VMEM((8, BLK), F32), pltpu.SemaphoreType.DMA((N_DEV - 1, 3)),
                        pltpu.SemaphoreType.DMA((N_DEV - 1, 3)), pltpu.SemaphoreType.DMA((3,))],
        compiler_params=pltpu.CompilerParams(vmem_limit_bytes=VMEM_LIMIT),
    )(w_in, w_out, conv_w)


def _dw_in_rs(ht, dz, dw_out, small):
    seq = dz.shape[0]

    def body(cols_ref, ht_ref, dz_ref, dout_ref, sm_ref, own_ref, rin_ref, rout_ref, rsm_ref, to_sibling, landed,
             to_chip, zero_buf, d2d_send, d2d_recv, ici_send, ici_recv, side_send, side_recv, local_sems):
        del cols_ref
        step = pl.program_id(0)
        x, y, c = _coords()
        me = 4 * x + 2 * y + c
        sibling = (x, y, 1 - c)
        chips = [(1 - x, y), (x, 1 - y), (1 - x, 1 - y)]

        def d2d(n):
            return pltpu.make_async_remote_copy(src_ref=to_sibling.at[n], dst_ref=landed.at[n], send_sem=d2d_send.at[n],
                                                recv_sem=d2d_recv.at[n], device_id=sibling, device_id_type=MESH)

        def ici(n):
            return pltpu.make_async_remote_copy(src_ref=to_chip.at[n], dst_ref=rin_ref.at[n], send_sem=ici_send.at[n],
                                                recv_sem=ici_recv.at[n], device_id=(*chips[n], c), device_id_type=MESH)

        def side(k, mine):
            peer, peer_idx = _peer(k)
            src_slab, dst_slab = (peer_idx, me) if mine else (me, peer_idx)
            pairs = [(dout_ref.at[src_slab], rout_ref.at[dst_slab]), (sm_ref, rsm_ref.at[dst_slab])]
            return [pltpu.make_async_remote_copy(src_ref=src, dst_ref=dst, send_sem=side_send.at[k - 1, a],
                                                 recv_sem=side_recv.at[k - 1, a], device_id=peer, device_id_type=MESH)
                    for a, (src, dst) in enumerate(pairs)]

        local = [pltpu.make_async_copy(zero_buf, rout_ref.at[me], local_sems.at[0]),
                 pltpu.make_async_copy(sm_ref, rsm_ref.at[me], local_sems.at[1])]

        @pl.when(step == 0)
        def _():
            zero_buf[...] = jnp.zeros_like(zero_buf)
            for cp in local:
                cp.start()
            for k in range(1, N_DEV):
                for cp in side(k, mine=True):
                    cp.start()

        dw = jnp.dot(ht_ref[...], dz_ref[...], preferred_element_type=F32)
        for n, at in zip(range(4), (0, 1, 2, N_DEV - 2)):
            @pl.when(step == at)
            def _(n=n):
                to_sibling[n] = dw.astype(BF16)
                d2d(n).start()

        for n in range(3):
            @pl.when(step == 3 + n)
            def _(n=n):
                d2d(n).wait_recv()
                to_chip[n] = (dw + landed[n].astype(F32)).astype(BF16)
                ici(n).start()

        @pl.when(step == N_DEV - 1)
        def _():
            d2d(3).wait_recv()
            own_ref[...] = dw + landed[3].astype(F32)
            for n in range(3):
                ici(n).wait_recv()
            for k in range(1, N_DEV):
                for cp in side(k, mine=False):
                    cp.wait_recv()
            for n in range(4):
                d2d(n).wait_send()
            for n in range(3):
                ici(n).wait_send()
            for k in range(1, N_DEV):
                for cp in side(k, mine=True):
                    cp.wait_send()
            for cp in local:
                cp.wait()

    x, y, c = _coords()
    others = [(1 - x, y), (x, 1 - y), (1 - x, 1 - y)]
    order = [(*chip, 1 - c) for chip in others] + [(*chip, c) for chip in others] + [(x, y, 1 - c), (x, y, c)]
    cols = jnp.stack([4 * px + 2 * py + pc for px, py, pc in order]).astype(jnp.int32)
    slab = (D_MODEL, SHARD_IN)
    grid_spec = pltpu.PrefetchScalarGridSpec(
        num_scalar_prefetch=1, grid=(N_DEV,),
        in_specs=[pl.BlockSpec((D_MODEL, seq), lambda s, cols: (0, 0), pipeline_mode=pl.Buffered(1)),
                  pl.BlockSpec((seq, SHARD_IN), lambda s, cols: (0, cols[s])), HBM_SPEC, HBM_SPEC],
        out_specs=[pl.BlockSpec(slab, lambda s, cols: (0, 0)), HBM_SPEC, HBM_SPEC, HBM_SPEC],
        scratch_shapes=[pltpu.VMEM((4, *slab), BF16), pltpu.VMEM((4, *slab), BF16), pltpu.VMEM((3, *slab), BF16),
                        pltpu.VMEM((SHARD_OUT, D_MODEL), BF16),
                        pltpu.SemaphoreType.DMA((4,)), pltpu.SemaphoreType.DMA((4,)),
                        pltpu.SemaphoreType.DMA((3,)), pltpu.SemaphoreType.DMA((3,)),
                        pltpu.SemaphoreType.DMA((N_DEV - 1, 2)), pltpu.SemaphoreType.DMA((N_DEV - 1, 2)),
                        pltpu.SemaphoreType.DMA((2,))])
    return pl.pallas_call(
        body, name="dw_in_rs", grid_spec=grid_spec,
        out_shape=[jax.ShapeDtypeStruct(slab, F32),
                   jax.ShapeDtypeStruct((3, *slab), BF16),
                   jax.ShapeDtypeStruct((N_DEV, SHARD_OUT, D_MODEL), BF16),
                   jax.ShapeDtypeStruct((N_DEV, 8, D_MODEL), F32)],
        compiler_params=_params(1),
    )(cols, ht, dz, dw_out, small)


def _adamw_math(w, g, m, v):
    m = ADAM_B1 * m + (1.0 - ADAM_B1) * g
    v = ADAM_B2 * v + (1.0 - ADAM_B2) * (g * g)
    m_hat = m / (1.0 - ADAM_B1 ** ADAM_STEP)
    v_hat = v / (1.0 - ADAM_B2 ** ADAM_STEP)
    delta = -ADAM_LR * (m_hat / (jnp.sqrt(v_hat) + ADAM_EPS) + ADAM_WD * w)
    return delta, m, v


def _sum_slabs(ref, first=None):
    total = ref[0].astype(F32) if first is None else first + ref[0].astype(F32)
    for s in range(1, ref.shape[0]):
        total = total + ref[s].astype(F32)
    return total


def _adamw_slabs(parts, own, w, m, v, name, tr):
    rows, cols = w.shape
    tile = pl.BlockSpec((tr, cols), lambda i: (i, 0))

    def body(p_ref, *refs):
        own_ref = refs[0] if own is not None else None
        w_ref, m_ref, v_ref, g_ref, d_ref, nm_ref, nv_ref = refs[-7:]
        g = _sum_slabs(p_ref, None if own_ref is None else own_ref[...])
        g_ref[...] = g
        d_ref[...], nm_ref[...], nv_ref[...] = _adamw_math(w_ref[...], g, m_ref[...], v_ref[...])

    extra = [] if own is None else [own]
    return pl.pallas_call(
        body, name=name, grid=(rows // tr,),
        in_specs=[pl.BlockSpec((parts.shape[0], tr, cols), lambda i: (0, i, 0))] + [tile] * (len(extra) + 3),
        out_specs=[tile] * 4,
        out_shape=[jax.ShapeDtypeStruct((rows, cols), F32)] * 4,
        compiler_params=_params(1),
    )(parts, *extra, w, m, v)


def _sum_small(parts):
    def body(p_ref, out_ref):
        out_ref[...] = _sum_slabs(p_ref)

    return pl.pallas_call(body, name="sum_small", out_shape=jax.ShapeDtypeStruct(parts.shape[1:], F32))(parts)


def _adamw_whole(g, w, m, v, name):
    def body(g_ref, w_ref, m_ref, v_ref, d_ref, nm_ref, nv_ref):
        d_ref[...], nm_ref[...], nv_ref[...] = _adamw_math(w_ref[...], g_ref[...], m_ref[...], v_ref[...])

    return pl.pallas_call(body, name=name, out_shape=[jax.ShapeDtypeStruct(w.shape, F32)] * 3)(g, w, m, v)


def kernel(x, norm_pre_g, w_in, conv_w, w_out, norm_post_g, loss_target, m_norm_pre_g, m_w_in, m_conv_w, m_w_out,
           m_norm_post_g, v_norm_pre_g, v_w_in, v_conv_w, v_w_out, v_norm_post_g):
    n_conv = CONV_W // N_DEV
    w_in_g, w_out_g, conv_g = _ag_weights(w_in, w_out, conv_w)
    conv_full = conv_g[:, 0:3, 0:n_conv].transpose(1, 0, 2).reshape(3, CONV_W)
    grad_x, ht, dz, dw_out, dw_out_bf, small = _local_step(x[0], loss_target[0], norm_pre_g, norm_post_g, w_in_g,
                                                           w_out_g.reshape(D_MODEL, D_MODEL), conv_full)
    own_in, r_in, r_out, r_small = _dw_in_rs(ht, dz, dw_out_bf.reshape(N_DEV, SHARD_OUT, D_MODEL), small)
    me = 4 * lax.axis_index("x") + 2 * lax.axis_index("y") + lax.axis_index("c")
    own_out = lax.dynamic_index_in_dim(dw_out.reshape(N_DEV, SHARD_OUT, D_MODEL), me, keepdims=False)
    g_in, d_in, nm_in, nv_in = _adamw_slabs(r_in, own_in, w_in, m_w_in, v_w_in, "adamw_in", 256)
    g_out, d_out, nm_out, nv_out = _adamw_slabs(r_out, own_out, w_out, m_w_out, v_w_out, "adamw_out", SHARD_OUT)
    sums = _sum_small(r_small)
    g_pre, g_post, loss = sums[0], sums[1], sums[2, 0]
    g_conv = lax.dynamic_slice(sums[3:6, 0:CONV_W], (0, me * n_conv), (3, n_conv))
    vec = lambda a: a.reshape(1, D_MODEL)
    d_pre, nm_pre, nv_pre = _adamw_whole(vec(g_pre), vec(norm_pre_g), vec(m_norm_pre_g), vec(v_norm_pre_g), "adamw_pre")
    d_post, nm_post, nv_post = _adamw_whole(vec(g_post), vec(norm_post_g), vec(m_norm_post_g), vec(v_norm_post_g),
                                            "adamw_post")
    d_conv, nm_conv, nv_conv = _adamw_whole(g_conv, conv_w, m_conv_w, v_conv_w, "adamw_conv")
    flat = lambda a: a.reshape(D_MODEL)
    return (loss, grad_x[None], g_pre, g_in, g_conv, g_out, g_post,
            flat(d_pre), d_in, d_conv, d_out, flat(d_post),
            flat(nm_pre), nm_in, nm_conv, nm_out, flat(nm_post),
            flat(nv_pre), nv_in, nv_conv, nv_out, flat(nv_post))
```

```python
import functools

import jax
import jax.numpy as jnp
import numpy as np
from jax import lax
from jax.experimental import pallas as pl
from jax.experimental.pallas import tpu as pltpu

F32 = jnp.float32
BF16 = jnp.bfloat16

D_MODEL = 1024
HEAD_DIM = 64
ATTN_W = 768
CONV_W = 256
IN_W = 4096
REST_W = IN_W - 3 * ATTN_W
BLK = 128
N_DEV = 8
SHARD_IN = IN_W // N_DEV
SHARD_OUT = D_MODEL // N_DEV
DILATIONS = (1, 4, 16)
ROPE_THETA = 10000.0
NORM_EPS = 1e-6
NEG = -1e30

ADAM_LR = 0.001
ADAM_B1 = 0.9
ADAM_B2 = 0.999
ADAM_EPS = 1e-08
ADAM_WD = 0.01
ADAM_STEP = 10

VMEM_LIMIT = 56 * 1024 * 1024
MESH = pl.DeviceIdType.MESH


def _params(n_grid):
    return pltpu.CompilerParams(dimension_semantics=("arbitrary",) * n_grid, vmem_limit_bytes=VMEM_LIMIT)


def _resident(shape):
    zeros = (0,) * len(shape)
    return pl.BlockSpec(shape, lambda *_: zeros, pipeline_mode=pl.Buffered(1))


def _sigmoid(a):
    return 1.0 / (1.0 + jnp.exp(-a))


def _swap_halves(t, first_half):
    return jnp.where(first_half, pltpu.roll(t, BLK - 32, 1), pltpu.roll(t, 32, 1))


def _rope_tables(seq, tm):
    half = HEAD_DIM // 2
    inv_freq = ROPE_THETA ** (-jnp.arange(half, dtype=F32) * 2.0 / HEAD_DIM)
    freq = jnp.concatenate([inv_freq] * 4)
    sign = jnp.concatenate([-jnp.ones(half, F32), jnp.ones(half, F32)] * 2)
    starts = (jnp.arange(seq // tm) * tm).astype(F32)[:, None] * freq[None, :]
    rows = jnp.arange(tm).astype(F32)[:, None] * freq[None, :]
    slab = lambda a: jnp.broadcast_to(a[:, None, :], (seq // tm, 8, BLK))
    return slab(jnp.cos(starts)), slab(jnp.sin(starts) * sign), jnp.cos(rows), jnp.sin(rows) * sign


def _rope_specs(tm):
    return [pl.BlockSpec((1, 8, BLK), lambda i: (i, 0, 0))] * 2 + [_resident((tm, BLK))] * 2


def _tile_rope(cos_start, sin_start, cos_row, sin_row):
    ca, sa, cb, sb = cos_start[0, 0:1, :], sin_start[0, 0:1, :], cos_row[...], sin_row[...]
    return ca * cb - sa * sb, sa * cb + ca * sb


N_CHUNK = ATTN_W // BLK


def _lanes(r, c):
    return slice(r * ATTN_W + c * BLK, r * ATTN_W + (c + 1) * BLK)


def _to_residues(src, chunk0, dst_refs, tmp, rows, dtype):
    assert DILATIONS == (1, 4, 16)
    dst1, dst4, dst16 = dst_refs
    n4, n16 = rows // 4, rows // 16
    for c in range(N_CHUNK):
        dst1[:, _lanes(0, c)] = src[chunk0 + c].astype(dtype)
        for r1 in range(4):
            tmp[c, r1 * n4:(r1 + 1) * n4, :] = src[chunk0 + c, pl.ds(r1, n4, stride=4), :]
        for r1 in range(4):
            dst4[:, _lanes(r1, c)] = tmp[c, r1 * n4:(r1 + 1) * n4, :].astype(dtype)
            for r2 in range(4):
                dst16[:, _lanes(4 * r2 + r1, c)] = tmp[c, pl.ds(r1 * n4 + r2, n16, stride=4), :].astype(dtype)


def _from_residue(src_ref, dst, dil, rows, accumulate, tmp=None):
    n4, n16 = rows // 4, rows // 16

    def put(where, piece):
        if accumulate:
            dst[where] += piece
        else:
            dst[where] = piece

    for c in range(N_CHUNK):
        if dil == 1:
            put((c,), src_ref[:, _lanes(0, c)].astype(F32))
            continue
        for r1 in range(4):
            if dil == 4:
                piece = src_ref[:, _lanes(r1, c)].astype(F32)
            else:
                for r2 in range(4):
                    tmp[c, pl.ds(r1 * n4 + r2, n16, stride=4), :] = src_ref[:, _lanes(4 * r2 + r1, c)].astype(F32)
                piece = tmp[c, r1 * n4:(r1 + 1) * n4, :]
            put((c, pl.ds(r1, n4, stride=4), slice(None)), piece)


def _residue_spec(tm, dil):
    return pl.BlockSpec((tm // dil, dil * ATTN_W), lambda i: (i, 0))


def _residue_shape(seq, dil, dtype):
    return jax.ShapeDtypeStruct((seq // dil, dil * ATTN_W), dtype)


def _fwd_in(x, g_pre, w_in_g, tm=512):
    seq = x.shape[0]
    n_dil = len(DILATIONS)

    def body(x_ref, g_ref, w_ref, ca_ref, sa_ref, cb_ref, sb_ref, *rest):
        qkv_refs, (zr_ref, ht_ref, qkv_scr, tmp) = rest[:3 * n_dil], rest[3 * n_dil:]
        xv = x_ref[...]
        r = lax.rsqrt(jnp.mean(xv * xv, axis=-1, keepdims=True) + NORM_EPS)
        hf = (xv * r) * g_ref[...]
        ht_ref[...] = hf.T.astype(BF16)
        h = hf.astype(BF16)
        cos, sin = _tile_rope(ca_ref, sa_ref, cb_ref, sb_ref)
        first_half = (lax.broadcasted_iota(jnp.int32, (tm, BLK), 1) & 32) == 0

        def rope(t):
            return t * cos + _swap_halves(t, first_half) * sin

        def project(j):
            return jnp.dot(h, w_ref[j], preferred_element_type=F32)

        def place(j, zj):
            for n in range(SHARD_IN // BLK):
                chunk, t = j * (SHARD_IN // BLK) + n, zj[:, n * BLK:(n + 1) * BLK]
                if chunk < N_CHUNK:
                    qkv_scr[chunk] = rope(t) * HEAD_DIM ** -0.5
                elif chunk < 2 * N_CHUNK:
                    qkv_scr[chunk] = rope(t)
                elif chunk < 3 * N_CHUNK:
                    qkv_scr[chunk] = t
                else:
                    zr_ref[:, (chunk - 3 * N_CHUNK) * BLK:(chunk - 3 * N_CHUNK + 1) * BLK] = t.astype(BF16)

        ahead = project(0)
        for j in range(N_DEV):
            zj = ahead
            if j + 1 < N_DEV:
                ahead = project(j + 1)
            place(j, zj)
            for a in range(3):
                if (a + 1) * N_CHUNK - 1 in range(j * (SHARD_IN // BLK), (j + 1) * (SHARD_IN // BLK)):
                    _to_residues(qkv_scr, a * N_CHUNK, [qkv_refs[3 * n + a] for n in range(n_dil)], tmp, tm, BF16)

    row = lambda w: pl.BlockSpec((tm, w), lambda i: (i, 0))
    outs = pl.pallas_call(
        body, name="fwd_in", grid=(seq // tm,),
        in_specs=[row(D_MODEL), _resident((1, D_MODEL)), _resident((N_DEV, D_MODEL, SHARD_IN))] + _rope_specs(tm),
        out_specs=[_residue_spec(tm, dil) for dil in DILATIONS for _ in range(3)]
        + [row(REST_W), pl.BlockSpec((D_MODEL, tm), lambda i: (0, i))],
        out_shape=[_residue_shape(seq, dil, BF16) for dil in DILATIONS for _ in range(3)]
        + [jax.ShapeDtypeStruct((seq, REST_W), BF16), jax.ShapeDtypeStruct((D_MODEL, seq), BF16)],
        scratch_shapes=[pltpu.VMEM((3 * N_CHUNK, tm, BLK), F32), pltpu.VMEM((N_CHUNK, tm, BLK), F32)],
        compiler_params=_params(1),
    )(x, g_pre.reshape(1, D_MODEL), w_in_g, *_rope_tables(seq, tm))
    qkv = [tuple(outs[3 * n:3 * n + 3]) for n in range(n_dil)]
    return qkv, outs[3 * n_dil], outs[3 * n_dil + 1]


def _band_bias(first_block):
    kj = lax.broadcasted_iota(jnp.int32, (2 * BLK, BLK), 0)
    qi = lax.broadcasted_iota(jnp.int32, (2 * BLK, BLK), 1)
    valid = (kj >= qi) & (kj <= qi + BLK)
    bias = jnp.where(valid, 0.0, NEG).astype(BF16)
    bias_first = jnp.where(valid & (kj >= BLK), 0.0, NEG).astype(BF16)
    onehot = ((kj & (BLK - 1)) == qi).astype(F32).astype(BF16)
    return onehot, bias, jnp.where(first_block, bias_first, bias)


def _stack_heads(t, head0):
    del head0
    keep0 = (lax.broadcasted_iota(jnp.int32, t.shape, 1) < HEAD_DIM).astype(F32).astype(BF16)
    return jnp.concatenate([t * keep0, t * (1 - keep0)], axis=0)


def _unstack_heads(t2, head0):
    return jnp.where(head0, t2[:BLK], t2[BLK:])


def _rows_per_head(a, head0):
    b = pltpu.roll(a, HEAD_DIM, 1)
    rows = jnp.concatenate([jnp.where(head0, a, b), jnp.where(head0, b, a)], axis=0)
    return jnp.concatenate([rows, rows], axis=1)


BLOCKS_PER_STEP = 32


def _attn_specs(length, dil):
    n_blocks = length // BLK
    tb = min(BLOCKS_PER_STEP, n_blocks)
    nc = BLOCKS_PER_STEP // tb
    assert (dil * N_CHUNK) % nc == 0 and n_blocks % tb == 0
    tile = pl.BlockSpec((tb * BLK, nc * BLK), lambda c, t: (t, c))
    prev = pl.BlockSpec((BLK, nc * BLK), lambda c, t: (jnp.maximum(t * tb - 1, 0), c))
    grid = (dil * N_CHUNK // nc, n_blocks // tb)
    return tb, nc, tile, prev, grid


def _window(prev_ref, cur_ref, j, cols):
    if j == 0:
        return jnp.concatenate([prev_ref[:, cols], cur_ref[0:BLK, cols]], axis=0)
    return cur_ref[(j - 1) * BLK:(j + 1) * BLK, cols]


def _attn_fwd(q, k, v, dil):
    length = q.shape[0]
    tb, nc, tile, prev, grid = _attn_specs(length, dil)

    def body(q_ref, kc_ref, kp_ref, vc_ref, vp_ref, o_ref, lse_ref):
        head0 = lax.broadcasted_iota(jnp.int32, (BLK, BLK), 1) < HEAD_DIM
        onehot, bias, bias_start = _band_bias(pl.program_id(1) == 0)
        ones = jnp.ones((2 * BLK, BLK), BF16)
        def scores(c, j):
            rows, cols = slice(j * BLK, (j + 1) * BLK), slice(c * BLK, (c + 1) * BLK)
            q2 = jnp.concatenate([_stack_heads(q_ref[rows, cols], head0), onehot], axis=1)
            kk = jnp.concatenate([_window(kp_ref, kc_ref, j, cols), bias_start if j == 0 else bias], axis=1)
            return (lax.dot_general(q2, kk, (((1,), (1,)), ((), ())), preferred_element_type=F32),)

        def probabilities(c, j, s):
            m = jnp.max(s, axis=1, keepdims=True)
            return m, jnp.exp(s - m).astype(BF16)

        def outputs(c, j, m, p):
            rows, cols = slice(j * BLK, (j + 1) * BLK), slice(c * BLK, (c + 1) * BLK)
            vv = jnp.concatenate([_window(vp_ref, vc_ref, j, cols), ones], axis=1)
            pv = jnp.dot(p, vv, preferred_element_type=F32)
            den = pv[:, BLK:]
            o_ref[rows, cols] = _unstack_heads(pv[:, :BLK] / den, head0).astype(BF16)
            lse_ref[rows, cols] = _unstack_heads(m + jnp.log(den), head0)

        units = [(c, j) for c in range(nc) for j in range(tb)]
        stage1, stage2 = {}, {}
        for n in range(len(units) + 2):
            if n < len(units):
                stage1[n] = scores(*units[n])
            if 0 <= n - 1 < len(units):
                stage2[n - 1] = probabilities(*units[n - 1], *stage1.pop(n - 1))
            if 0 <= n - 2 < len(units):
                outputs(*units[n - 2], *stage2.pop(n - 2))

    return pl.pallas_call(
        body, name=f"attn_fwd_d{dil}", grid=grid,
        in_specs=[tile, tile, prev, tile, prev], out_specs=[tile, tile],
        out_shape=[jax.ShapeDtypeStruct(q.shape, BF16), jax.ShapeDtypeStruct(q.shape, F32)],
        compiler_params=_params(2),
    )(q, k, k, v, v)


def _attn_bwd(q, k, v, do, lse, delta, dil):
    length = q.shape[0]
    tb, nc, tile, prev, grid = _attn_specs(length, dil)
    whole = pl.BlockSpec((length, nc * BLK), lambda c, t: (0, c))

    def body(q_ref, do_ref, lse_ref, dl_ref, kc_ref, kp_ref, vc_ref, vp_ref, dq_ref, dk_ref, dv_ref):
        t = pl.program_id(1)
        head0 = lax.broadcasted_iota(jnp.int32, (BLK, BLK), 1) < HEAD_DIM
        onehot, bias, bias_start = _band_bias(t == 0)

        def scores(c, j):
            rows, cols = slice(j * BLK, (j + 1) * BLK), slice(c * BLK, (c + 1) * BLK)
            q2 = _stack_heads(q_ref[rows, cols], head0)
            do2 = _stack_heads(do_ref[rows, cols], head0)
            kk = _window(kp_ref, kc_ref, j, cols)
            s = lax.dot_general(jnp.concatenate([q2, onehot], axis=1),
                                jnp.concatenate([kk, bias_start if j == 0 else bias], axis=1),
                                (((1,), (1,)), ((), ())), preferred_element_type=F32)
            dp = lax.dot_general(do2, _window(vp_ref, vc_ref, j, cols), (((1,), (1,)), ((), ())),
                                 preferred_element_type=F32)
            return q2, do2, kk, s, dp

        def probabilities(c, j, q2, do2, kk, s, dp):
            rows, cols = slice(j * BLK, (j + 1) * BLK), slice(c * BLK, (c + 1) * BLK)
            p = jnp.exp(s - _rows_per_head(lse_ref[rows, cols], head0))
            ds = (p * (dp - _rows_per_head(dl_ref[rows, cols].astype(F32), head0))).astype(BF16)
            return q2, do2, kk, p.astype(BF16), ds

        def gradients(c, j, q2, do2, kk, p, ds):
            rows, cols = slice(j * BLK, (j + 1) * BLK), slice(c * BLK, (c + 1) * BLK)
            dq2 = jnp.dot(ds, kk, preferred_element_type=F32)
            dq_ref[rows, cols] = (_unstack_heads(dq2, head0) * HEAD_DIM ** -0.5).astype(BF16)
            dk2 = lax.dot_general(ds, q2, (((0,), (0,)), ((), ())), preferred_element_type=F32)
            dv2 = lax.dot_general(p, do2, (((0,), (0,)), ((), ())), preferred_element_type=F32)
            own = pl.ds(pl.multiple_of((t * tb + j) * BLK, BLK), BLK)
            dk_ref[own, cols] = dk2[BLK:].astype(BF16)
            dv_ref[own, cols] = dv2[BLK:].astype(BF16)

            def add_to_previous():
                before = pl.ds(pl.multiple_of((t * tb + j - 1) * BLK, BLK), BLK)
                dk_ref[before, cols] = (dk_ref[before, cols].astype(F32) + dk2[:BLK]).astype(BF16)
                dv_ref[before, cols] = (dv_ref[before, cols].astype(F32) + dv2[:BLK]).astype(BF16)

            if j > 0:
                add_to_previous()
            elif grid[1] > 1:
                pl.when(t > 0)(add_to_previous)

        units = [(c, j) for c in range(nc) for j in range(tb)]
        stage1 = {0: scores(*units[0])}
        for n in range(len(units)):
            stage2 = probabilities(*units[n], *stage1.pop(n))
            if n + 1 < len(units):
                stage1[n + 1] = scores(*units[n + 1])
            gradients(*units[n], *stage2)

    return pl.pallas_call(
        body, name=f"attn_bwd_d{dil}", grid=grid,
        in_specs=[tile, tile, tile, tile, tile, prev, tile, prev], out_specs=[tile, whole, whole],
        out_shape=[jax.ShapeDtypeStruct(q.shape, BF16)] * 3,
        compiler_params=_params(2),
    )(q, do, lse, delta, k, k, v, v)


HALO = 16


def _halo_specs(tm, seq):
    before = lambda w: pl.BlockSpec((HALO, w), lambda i: (jnp.maximum(i * (tm // HALO) - 1, 0), 0))
    after = lambda w: pl.BlockSpec((HALO, w), lambda i: (jnp.minimum((i + 1) * (tm // HALO), seq // HALO - 1), 0))
    return before, after


def _conv_taps(u, before, tm):
    row = lax.broadcasted_iota(jnp.int32, u.shape, 0)
    last, last2 = before[HALO - 1:HALO, :], before[HALO - 2:HALO - 1, :]
    u1 = jnp.where(row == 0, last, pltpu.roll(u, 1, 0))
    u2 = jnp.where(row == 0, last2, jnp.where(row == 1, last, pltpu.roll(u, 2, 0)))
    return u1, u2


def _attn_combine(o_parts, lse_parts, zr, conv_w, tm=256):
    seq = zr.shape[0]
    a0, h0, b0, c0, g0 = 0, ATTN_W, ATTN_W + CONV_W, ATTN_W + 2 * CONV_W, ATTN_W + 3 * CONV_W

    def body(o1, o2, o3, l1, l2, l3, zr_ref, zp_ref, w_ref, mixed_ref, o_ref, lse1, lse2, lse3, *scr):
        i = pl.program_id(0)
        for src, dst, dil in zip((o2, o3, l2, l3), scr[:4], DILATIONS[1:] * 2):
            _from_residue(src, dst, dil, tm, accumulate=False, tmp=scr[5])
        for c in range(N_CHUNK):
            cols = slice(c * BLK, (c + 1) * BLK)
            la, lb, lc = l1[:, cols], scr[2][c], scr[3][c]
            top = jnp.maximum(jnp.maximum(la, lb), lc)
            ea, eb, ec = jnp.exp(la - top), jnp.exp(lb - top), jnp.exp(lc - top)
            den = ea + eb + ec
            inv = 1.0 / den
            o = (ea * inv) * o1[:, cols].astype(F32) + (eb * inv) * scr[0][c] + (ec * inv) * scr[1][c]
            o_ref[:, cols] = o.astype(BF16)
            scr[4][c] = top + jnp.log(den)
            ga = zr_ref[:, cols].astype(F32)
            mixed_ref[:, cols] = (o * (ga * _sigmoid(ga))).astype(BF16)
        _to_residues(scr[4], 0, (lse1, lse2, lse3), scr[5], tm, F32)
        part = lambda ref, lo, hi: ref[:, lo:hi].astype(F32)
        u = part(zr_ref, c0, g0) * part(zr_ref, h0, b0)
        before = jnp.where(i > 0, part(zp_ref, c0, g0) * part(zp_ref, h0, b0), 0.0)
        u1, u2 = _conv_taps(u, before, tm)
        y = u2 * w_ref[0:1, :] + u1 * w_ref[1:2, :] + u * w_ref[2:3, :]
        gc = part(zr_ref, g0, REST_W)
        mixed_ref[:, ATTN_W:] = ((part(zr_ref, b0, c0) * y) * (gc * _sigmoid(gc))).astype(BF16)

    row = lambda w: pl.BlockSpec((tm, w), lambda i: (i, 0))
    before, _ = _halo_specs(tm, seq)
    views = [_residue_spec(tm, dil) for dil in DILATIONS]
    outs = pl.pallas_call(
        body, name="attn_combine", grid=(seq // tm,),
        in_specs=views * 2 + [row(REST_W), before(REST_W), _resident((3, CONV_W))],
        out_specs=[row(D_MODEL), row(ATTN_W)] + views,
        out_shape=[jax.ShapeDtypeStruct((seq, D_MODEL), BF16), jax.ShapeDtypeStruct((seq, ATTN_W), BF16)]
        + [_residue_shape(seq, dil, F32) for dil in DILATIONS],
        scratch_shapes=[pltpu.VMEM((N_CHUNK, tm, BLK), F32)] * 6,
        compiler_params=_params(1),
    )(*o_parts, *lse_parts, zr, zr, conv_w)
    return outs[0], outs[1], outs[2:]


def _out_loss_bwd(mixed, w_out_g, x, target, g_post, tm=512, n_parts=2):
    seq = x.shape[0]

    def body(mx_ref, w_ref, x_ref, t_ref, g_ref, dout_ref, dmx_ref, dw_ref, dwb_ref, st_ref):
        i = pl.program_id(0)
        g = g_ref[...]
        parts = [slice(n * (tm // n_parts), (n + 1) * (tm // n_parts)) for n in range(n_parts)]

        def project(rows):
            return jnp.dot(mx_ref[rows, :], w_ref[...], preferred_element_type=F32)

        def head(rows, y):
            r = lax.rsqrt(jnp.mean(y * y, axis=-1, keepdims=True) + NORM_EPS)
            yhat = y * r
            err = (x_ref[rows, :] + yhat * g) - t_ref[rows, :]
            dn = err * (1.0 / D_MODEL)
            dout_ref[rows, :] = dn
            tg = dn * g
            dy = (r * (tg - yhat * jnp.mean(tg * yhat, axis=-1, keepdims=True))).astype(BF16)
            dmx_ref[rows, :] = lax.dot_general(dy, w_ref[...], (((1,), (1,)), ((), ())),
                                               preferred_element_type=F32).astype(BF16)
            return dy, jnp.sum(dn * yhat, axis=0, keepdims=True), jnp.sum(err * err)

        ahead, done = project(parts[0]), []
        for n, rows in enumerate(parts):
            y = ahead
            if n + 1 < n_parts:
                ahead = project(parts[n + 1])
            done.append(head(rows, y))
        dy = jnp.concatenate([d[0] for d in done], axis=0)
        dw = lax.dot_general(mx_ref[...], dy, (((0,), (0,)), ((), ())), preferred_element_type=F32)
        gsum = functools.reduce(lambda a, b: a + b, [d[1] for d in done])
        lsum = jnp.broadcast_to(0.5 / D_MODEL * functools.reduce(lambda a, b: a + b, [d[2] for d in done]),
                                (1, D_MODEL))

        @pl.when(i == 0)
        def _():
            dw_ref[...] = dw
            st_ref[...] = jnp.zeros_like(st_ref)
            st_ref[0:1, :] = gsum
            st_ref[1:2, :] = lsum

        @pl.when(i > 0)
        def _():
            dw_ref[...] += dw
            st_ref[0:1, :] += gsum
            st_ref[1:2, :] += lsum

        @pl.when(i == seq // tm - 1)
        def _():
            dwb_ref[...] = dw_ref[...].astype(BF16)

    row = lambda w: pl.BlockSpec((tm, w), lambda i: (i, 0))
    whole = pl.BlockSpec((D_MODEL, D_MODEL), lambda i: (0, 0))
    return pl.pallas_call(
        body, name="out_loss_bwd", grid=(seq // tm,),
        in_specs=[row(D_MODEL), _resident((D_MODEL, D_MODEL)), row(D_MODEL), row(D_MODEL), _resident((1, D_MODEL))],
        out_specs=[row(D_MODEL), row(D_MODEL), whole, whole, pl.BlockSpec((8, D_MODEL), lambda i: (0, 0))],
        out_shape=[jax.ShapeDtypeStruct((seq, D_MODEL), F32), jax.ShapeDtypeStruct((seq, D_MODEL), BF16),
                   jax.ShapeDtypeStruct((D_MODEL, D_MODEL), F32), jax.ShapeDtypeStruct((D_MODEL, D_MODEL), BF16),
                   jax.ShapeDtypeStruct((8, D_MODEL), F32)],
        compiler_params=_params(1),
    )(mixed, w_out_g, x, target, g_post.reshape(1, D_MODEL))


def _head_sum(prod, same_head):
    hi = prod.astype(BF16)
    lo = (prod - hi.astype(F32)).astype(BF16)
    return (jnp.dot(hi, same_head, preferred_element_type=F32) + jnp.dot(lo, same_head, preferred_element_type=F32))


def _gate_bwd(dmixed, zr, o, conv_w, tm=256):
    seq = zr.shape[0]
    n_tiles = seq // tm
    n_dil = len(DILATIONS)
    a0, h0, b0, c0, g0 = 0, ATTN_W, ATTN_W + CONV_W, ATTN_W + 2 * CONV_W, ATTN_W + 3 * CONV_W

    def body(dm_ref, dmn_ref, zr_ref, zp_ref, zn_ref, o_ref, w_ref, *rest):
        do_refs, dl_refs = rest[:n_dil], rest[n_dil:2 * n_dil]
        dz_ref, dw_ref, do_scr, dl_scr, tmp = rest[2 * n_dil:]
        i = pl.program_id(0)
        part = lambda ref, lo, hi: ref[:, lo:hi].astype(F32)
        ga = part(zr_ref, a0, h0)
        sg = _sigmoid(ga)
        dattn = part(dm_ref, 0, ATTN_W)
        ov = o_ref[...].astype(F32)
        do = dattn * (ga * sg)
        dz_ref[:, a0:h0] = (dattn * ov * (sg * (1.0 + ga * (1.0 - sg)))).astype(BF16)
        li = lax.broadcasted_iota(jnp.int32, (BLK, BLK), 0) // HEAD_DIM
        lj = lax.broadcasted_iota(jnp.int32, (BLK, BLK), 1) // HEAD_DIM
        same_head = (li == lj).astype(BF16)
        prod = do * ov
        for c in range(N_CHUNK):
            cols = slice(c * BLK, (c + 1) * BLK)
            do_scr[c] = do[:, cols]
            dl_scr[c] = _head_sum(prod[:, cols], same_head)
        _to_residues(do_scr, 0, do_refs, tmp, tm, BF16)
        _to_residues(dl_scr, 0, dl_refs, tmp, tm, BF16)

        ch, cb, cc, gc = (part(zr_ref, lo, hi) for lo, hi in ((h0, b0), (b0, c0), (c0, g0), (g0, REST_W)))
        u = cc * ch
        before = jnp.where(i > 0, part(zp_ref, c0, g0) * part(zp_ref, h0, b0), 0.0)
        u1, u2 = _conv_taps(u, before, tm)
        w0, w1, w2 = w_ref[0:1, :], w_ref[1:2, :], w_ref[2:3, :]
        y = u2 * w0 + u1 * w1 + u * w2
        sc = _sigmoid(gc)
        silu_c = gc * sc
        dconv = part(dm_ref, ATTN_W, D_MODEL)
        dz_ref[:, b0:c0] = (dconv * y * silu_c).astype(BF16)
        dz_ref[:, g0:] = (dconv * (cb * y) * (sc * (1.0 + gc * (1.0 - sc)))).astype(BF16)
        dy = dconv * cb * silu_c
        gn = part(zn_ref, g0, REST_W)
        after = jnp.where(i < n_tiles - 1,
                          part(dmn_ref, ATTN_W, D_MODEL) * part(zn_ref, b0, c0) * (gn * _sigmoid(gn)), 0.0)
        row = lax.broadcasted_iota(jnp.int32, dy.shape, 0)
        nxt, nxt2 = after[0:1, :], after[1:2, :]
        dy1 = jnp.where(row == tm - 1, nxt, pltpu.roll(dy, tm - 1, 0))
        dy2 = jnp.where(row == tm - 1, nxt2, jnp.where(row == tm - 2, nxt, pltpu.roll(dy, tm - 2, 0)))
        du = dy * w2 + dy1 * w1 + dy2 * w0
        dz_ref[:, c0:g0] = (du * ch).astype(BF16)
        dz_ref[:, h0:b0] = (du * cc).astype(BF16)
        dws = [jnp.sum(dy * u2, axis=0, keepdims=True), jnp.sum(dy * u1, axis=0, keepdims=True),
               jnp.sum(dy * u, axis=0, keepdims=True)]

        @pl.when(i == 0)
        def _():
            dw_ref[...] = jnp.zeros_like(dw_ref)

        for n, part in enumerate(dws):
            dw_ref[n:n + 1, :] += part

    row_spec = lambda w: pl.BlockSpec((tm, w), lambda i: (i, 0))
    before, after = _halo_specs(tm, seq)
    views = [_residue_spec(tm, dil) for dil in DILATIONS]
    outs = pl.pallas_call(
        body, name="gate_bwd", grid=(n_tiles,),
        in_specs=[row_spec(D_MODEL), after(D_MODEL), row_spec(REST_W), before(REST_W), after(REST_W),
                  row_spec(ATTN_W), _resident((3, CONV_W))],
        out_specs=views * 2 + [row_spec(REST_W), pl.BlockSpec((8, CONV_W), lambda i: (0, 0))],
        out_shape=[_residue_shape(seq, dil, BF16) for dil in DILATIONS] * 2
        + [jax.ShapeDtypeStruct((seq, REST_W), BF16), jax.ShapeDtypeStruct((8, CONV_W), F32)],
        scratch_shapes=[pltpu.VMEM((N_CHUNK, tm, BLK), F32)] * 3,
        compiler_params=_params(1),
    )(dmixed, dmixed, zr, zr, zr, o, conv_w)
    return outs[:n_dil], outs[n_dil:2 * n_dil], outs[2 * n_dil], outs[2 * n_dil + 1]


def _in_bwd(dqs, dks, dvs, dzr, x, d_out, g_pre, w_in_g, tm=256):
    seq = x.shape[0]

    def body(q1, q2, q3, k1, k2, k3, v1, v2, v3, dzr_ref, ca_ref, sa_ref, cb_ref, sb_ref, x_ref, dout_ref, g_ref,
             w_ref, dz_ref, gx_ref, st_ref, *scratch):
        i = pl.program_id(0)
        cos, sin = _tile_rope(ca_ref, sa_ref, cb_ref, sb_ref)
        first_half = (lax.broadcasted_iota(jnp.int32, (tm, BLK), 1) & 32) == 0
        streams = [(q1, q2, q3), (k1, k2, k3), (v1, v2, v3)]
        from4, from16, tmp = scratch[0:3], scratch[3:6], scratch[6]
        per_slab = SHARD_IN // BLK

        def unrope(t):
            return t * cos - _swap_halves(t, first_half) * sin

        def to_positions(a):
            _from_residue(streams[a][1], from4[a], 4, tm, accumulate=False)
            _from_residue(streams[a][2], from16[a], 16, tm, accumulate=False, tmp=tmp)

        def assemble(j):
            for chunk in range(j * per_slab, (j + 1) * per_slab):
                a, c = divmod(chunk, N_CHUNK)
                if a < 3:
                    total = streams[a][0][:, _lanes(0, c)].astype(F32) + from4[a][c] + from16[a][c]
                    val = (unrope(total) if a < 2 else total).astype(BF16)
                else:
                    val = dzr_ref[:, (chunk - 3 * N_CHUNK) * BLK:(chunk - 3 * N_CHUNK + 1) * BLK]
                dz_ref[:, chunk * BLK:(chunk + 1) * BLK] = val
            return dz_ref[:, j * SHARD_IN:(j + 1) * SHARD_IN]

        order = [j for j in range(N_DEV) if j * per_slab >= 3 * N_CHUNK]
        order += [j for j in range(N_DEV) if j not in order]
        assert order[2] * per_slab >= 3 * N_CHUNK
        ahead = assemble(order[0])
        dh = None
        for n, j in enumerate(order):
            part = lax.dot_general(ahead, w_ref[j], (((1,), (1,)), ((), ())), preferred_element_type=F32)
            if n < 3:
                to_positions(n)
            if n + 1 < N_DEV:
                ahead = assemble(order[n + 1])
            dh = part if dh is None else dh + part
        xv = x_ref[...]
        r = lax.rsqrt(jnp.mean(xv * xv, axis=-1, keepdims=True) + NORM_EPS)
        xhat = xv * r
        tg = dh * g_ref[...]
        gx_ref[...] = dout_ref[...] + r * (tg - xhat * jnp.mean(tg * xhat, axis=-1, keepdims=True))
        gsum = jnp.sum(dh * xhat, axis=0, keepdims=True)

        @pl.when(i == 0)
        def _():
            st_ref[...] = jnp.zeros_like(st_ref)

        st_ref[0:1, :] += gsum

    row = lambda w: pl.BlockSpec((tm, w), lambda i: (i, 0))
    return pl.pallas_call(
        body, name="in_bwd", grid=(seq // tm,),
        in_specs=[_residue_spec(tm, dil) for dil in DILATIONS] * 3
        + [row(REST_W)] + _rope_specs(tm) + [row(D_MODEL), row(D_MODEL), _resident((1, D_MODEL)),
                                             _resident((N_DEV, D_MODEL, SHARD_IN))],
        out_specs=[row(IN_W), row(D_MODEL), pl.BlockSpec((8, D_MODEL), lambda i: (0, 0))],
        out_shape=[jax.ShapeDtypeStruct((seq, IN_W), BF16), jax.ShapeDtypeStruct((seq, D_MODEL), F32),
                   jax.ShapeDtypeStruct((8, D_MODEL), F32)],
        scratch_shapes=[pltpu.VMEM((N_CHUNK, tm, BLK), F32)] * 7,
        compiler_params=_params(1),
    )(*dqs, *dks, *dvs, dzr, *_rope_tables(seq, tm), x, d_out, g_pre.reshape(1, D_MODEL), w_in_g)


def _local_step(x, target, g_pre, g_post, w_in_g, w_out_g, conv_w):
    qkv, zr, ht = _fwd_in(x, g_pre, w_in_g)
    parts = [_attn_fwd(*qkv[n], dil) for n, dil in enumerate(DILATIONS)]
    mixed, o, lse = _attn_combine([p[0] for p in parts], [p[1] for p in parts], zr, conv_w)
    d_out, dmixed, dw_out, dw_out_bf, st_post = _out_loss_bwd(mixed, w_out_g, x, target, g_post)
    do, delta, dzr, dconv = _gate_bwd(dmixed, zr, o, conv_w)
    grads = [_attn_bwd(*qkv[n], do[n], lse[n], delta[n], dil) for n, dil in enumerate(DILATIONS)]
    dz, grad_x, st_pre = _in_bwd([g[0] for g in grads], [g[1] for g in grads], [g[2] for g in grads], dzr,
                                 x, d_out, g_pre, w_in_g)
    conv_rows = jnp.pad(dconv[0:3], ((0, 0), (0, D_MODEL - CONV_W)))
    small = jnp.concatenate([st_pre[0:1], st_post[0:2], conv_rows, jnp.zeros((2, D_MODEL), F32)], axis=0)
    return grad_x, ht, dz, dw_out, dw_out_bf, small


def _coords():
    return lax.axis_index("x"), lax.axis_index("y"), lax.axis_index("c")


def _peer(k):
    x, y, c = _coords()
    px = 1 - x if k & 4 else x
    py = 1 - y if k & 2 else y
    pc = 1 - c if k & 1 else c
    return (px, py, pc), 4 * px + 2 * py + pc


HBM_SPEC = pl.BlockSpec(memory_space=pltpu.HBM)
VMEM_SPEC = pl.BlockSpec(memory_space=pltpu.VMEM)


def _ag_weights(w_in, w_out, conv_w):
    def body(win_ref, wout_ref, cw_ref, gin_ref, gout_ref, gcw_ref, win_bf, wout_bf, cw_pad, send_sems, recv_sems,
             local_sems):
        x, y, c = _coords()
        me, sibling = (x, y, c), (x, y, 1 - c)
        flip = lambda v, yes: v + yes - 2 * v * yes
        x_nbr, y_nbr, diagonal = (1 - x, y, c), (x, 1 - y, c), (1 - x, 1 - y, c)
        relay_from = (flip(x, 1 - c), flip(y, c), c)
        relay_to = (flip(x, c), flip(y, 1 - c), c)
        slab = lambda px, py, pc: 4 * px + 2 * py + pc
        win_bf[...] = win_ref[...].astype(BF16)
        wout_bf[...] = wout_ref[...].astype(BF16)
        cw_pad[...] = jnp.zeros_like(cw_pad)
        cw_pad[0:3, 0:CONV_W // N_DEV] = cw_ref[...]
        mine = [win_bf, wout_bf, cw_pad]
        gathered = [gin_ref, gout_ref, gcw_ref]

        def copies(k, block, to, own=False):
            return [pltpu.make_async_remote_copy(src_ref=mine[a] if own else gathered[a].at[slab(*block)],
                                                 dst_ref=gathered[a].at[slab(*block)], send_sem=send_sems.at[k, a],
                                                 recv_sem=recv_sems.at[k, a], device_id=to, device_id_type=MESH)
                    for a in range(3)]

        local = [pltpu.make_async_copy(mine[a], gathered[a].at[slab(*me)], local_sems.at[a]) for a in range(3)]
        for cp in local:
            cp.start()
        started = copies(0, me, sibling, own=True) + copies(1, me, x_nbr, own=True) + copies(2, me, y_nbr, own=True)
        for cp in started:
            cp.start()
        for cp in copies(1, x_nbr, me) + copies(2, y_nbr, me):
            cp.wait_recv()
        onward = copies(3, relay_from, relay_to) + copies(4, x_nbr, sibling) + copies(5, y_nbr, sibling)
        for cp in onward:
            cp.start()
        for cp in copies(3, diagonal, me):
            cp.wait_recv()
        last = copies(6, diagonal, sibling)
        for cp in last:
            cp.start()
        for cp in copies(0, sibling, me):
            cp.wait_recv()
        for k, origin in ((4, (1 - x, y, 1 - c)), (5, (x, 1 - y, 1 - c)), (6, (1 - x, 1 - y, 1 - c))):
            for cp in copies(k, origin, me):
                cp.wait_recv()
        for cp in started + onward + last:
            cp.wait_send()
        for cp in local:
            cp.wait()

    return pl.pallas_call(
        body, name="ag_weights",
        in_specs=[VMEM_SPEC, VMEM_SPEC, VMEM_SPEC], out_specs=[HBM_SPEC, HBM_SPEC, HBM_SPEC],
        out_shape=[jax.ShapeDtypeStruct((N_DEV, D_MODEL, SHARD_IN), BF16),
                   jax.ShapeDtypeStruct((N_DEV, SHARD_OUT, D_MODEL), BF16),
                   jax.ShapeDtypeStruct((N_DEV, 8, BLK), F32)],
        scratch_shapes=[pltpu.VMEM((D_MODEL, SHARD_IN), BF16), pltpu.VMEM((SHARD_OUT, D_MODEL), BF16),
                        pltpu.VMEM((8, BLK), F32), pltpu.SemaphoreType.DMA((N_DEV - 1, 3)),
                        pltpu.SemaphoreType.DMA((N_DEV - 1, 3)), pltpu.SemaphoreType.DMA((3,))],
        compiler_params=pltpu.CompilerParams(vmem_limit_bytes=VMEM_LIMIT),
    )(w_in, w_out, conv_w)


def _dw_in_rs(ht, dz, dw_out, small):
    seq = dz.shape[0]

    def body(cols_ref, ht_ref, dz_ref, dout_ref, sm_ref, own_ref, rin_ref, rout_ref, rsm_ref, to_sibling, landed,
             to_chip, zero_buf, d2d_send, d2d_recv, ici_send, ici_recv, side_send, side_recv, local_sems):
        del cols_ref
        step = pl.program_id(0)
        x, y, c = _coords()
        me = 4 * x + 2 * y + c
        sibling = (x, y, 1 - c)
        chips = [(1 - x, y), (x, 1 - y), (1 - x, 1 - y)]

        def d2d(n):
            return pltpu.make_async_remote_copy(src_ref=to_sibling.at[n], dst_ref=landed.at[n], send_sem=d2d_send.at[n],
                                                recv_sem=d2d_recv.at[n], device_id=sibling, device_id_type=MESH)

        def ici(n):
            return pltpu.make_async_remote_copy(src_ref=to_chip.at[n], dst_ref=rin_ref.at[n], send_sem=ici_send.at[n],
                                                recv_sem=ici_recv.at[n], device_id=(*chips[n], c), device_id_type=MESH)

        def side(k, mine):
            peer, peer_idx = _peer(k)
            src_slab, dst_slab = (peer_idx, me) if mine else (me, peer_idx)
            pairs = [(dout_ref.at[src_slab], rout_ref.at[dst_slab]), (sm_ref, rsm_ref.at[dst_slab])]
            return [pltpu.make_async_remote_copy(src_ref=src, dst_ref=dst, send_sem=side_send.at[k - 1, a],
                                                 recv_sem=side_recv.at[k - 1, a], device_id=peer, device_id_type=MESH)
                    for a, (src, dst) in enumerate(pairs)]

        local = [pltpu.make_async_copy(zero_buf, rout_ref.at[me], local_sems.at[0]),
                 pltpu.make_async_copy(sm_ref, rsm_ref.at[me], local_sems.at[1])]

        @pl.when(step == 0)
        def _():
            zero_buf[...] = jnp.zeros_like(zero_buf)
            for cp in local:
                cp.start()
            for k in range(1, N_DEV):
                for cp in side(k, mine=True):
                    cp.start()

        dw = jnp.dot(ht_ref[...], dz_ref[...], preferred_element_type=F32)
        for n, at in zip(range(4), (0, 1, 2, N_DEV - 2)):
            @pl.when(step == at)
            def _(n=n):
                to_sibling[n] = dw.astype(BF16)
                d2d(n).start()

        for n in range(3):
            @pl.when(step == 3 + n)
            def _(n=n):
                d2d(n).wait_recv()
                to_chip[n] = (dw + landed[n].astype(F32)).astype(BF16)
                ici(n).start()

        @pl.when(step == N_DEV - 1)
        def _():
            d2d(3).wait_recv()
            own_ref[...] = dw + landed[3].astype(F32)
            for n in range(3):
                ici(n).wait_recv()
            for k in range(1, N_DEV):
                for cp in side(k, mine=False):
                    cp.wait_recv()
            for n in range(4):
                d2d(n).wait_send()
            for n in range(3):
                ici(n).wait_send()
            for k in range(1, N_DEV):
                for cp in side(k, mine=True):
                    cp.wait_send()
            for cp in local:
                cp.wait()

    x, y, c = _coords()
    others = [(1 - x, y), (x, 1 - y), (1 - x, 1 - y)]
    order = [(*chip, 1 - c) for chip in others] + [(*chip, c) for chip in others] + [(x, y, 1 - c), (x, y, c)]
    cols = jnp.stack([4 * px + 2 * py + pc for px, py, pc in order]).astype(jnp.int32)
    slab = (D_MODEL, SHARD_IN)
    grid_spec = pltpu.PrefetchScalarGridSpec(
        num_scalar_prefetch=1, grid=(N_DEV,),
        in_specs=[pl.BlockSpec((D_MODEL, seq), lambda s, cols: (0, 0), pipeline_mode=pl.Buffered(1)),
                  pl.BlockSpec((seq, SHARD_IN), lambda s, cols: (0, cols[s])), HBM_SPEC, HBM_SPEC],
        out_specs=[pl.BlockSpec(slab, lambda s, cols: (0, 0)), HBM_SPEC, HBM_SPEC, HBM_SPEC],
        scratch_shapes=[pltpu.VMEM((4, *slab), BF16), pltpu.VMEM((4, *slab), BF16), pltpu.VMEM((3, *slab), BF16),
                        pltpu.VMEM((SHARD_OUT, D_MODEL), BF16),
                        pltpu.SemaphoreType.DMA((4,)), pltpu.SemaphoreType.DMA((4,)),
                        pltpu.SemaphoreType.DMA((3,)), pltpu.SemaphoreType.DMA((3,)),
                        pltpu.SemaphoreType.DMA((N_DEV - 1, 2)), pltpu.SemaphoreType.DMA((N_DEV - 1, 2)),
                        pltpu.SemaphoreType.DMA((2,))])
    return pl.pallas_call(
        body, name="dw_in_rs", grid_spec=grid_spec,
        out_shape=[jax.ShapeDtypeStruct(slab, F32),
                   jax.ShapeDtypeStruct((3, *slab), BF16),
                   jax.ShapeDtypeStruct((N_DEV, SHARD_OUT, D_MODEL), BF16),
                   jax.ShapeDtypeStruct((N_DEV, 8, D_MODEL), F32)],
        compiler_params=_params(1),
    )(cols, ht, dz, dw_out, small)


def _adamw_math(w, g, m, v):
    m = ADAM_B1 * m + (1.0 - ADAM_B1) * g
    v = ADAM_B2 * v + (1.0 - ADAM_B2) * (g * g)
    m_hat = m / (1.0 - ADAM_B1 ** ADAM_STEP)
    v_hat = v / (1.0 - ADAM_B2 ** ADAM_STEP)
    delta = -ADAM_LR * (m_hat / (jnp.sqrt(v_hat) + ADAM_EPS) + ADAM_WD * w)
    return delta, m, v


def _sum_slabs(ref, first=None):
    total = ref[0].astype(F32) if first is None else first + ref[0].astype(F32)
    for s in range(1, ref.shape[0]):
        total = total + ref[s].astype(F32)
    return total


def _adamw_slabs(parts, own, w, m, v, name, tr):
    rows, cols = w.shape
    tile = pl.BlockSpec((tr, cols), lambda i: (i, 0))

    def body(p_ref, *refs):
        own_ref = refs[0] if own is not None else None
        w_ref, m_ref, v_ref, g_ref, d_ref, nm_ref, nv_ref = refs[-7:]
        g = _sum_slabs(p_ref, None if own_ref is None else own_ref[...])
        g_ref[...] = g
        d_ref[...], nm_ref[...], nv_ref[...] = _adamw_math(w_ref[...], g, m_ref[...], v_ref[...])

    extra = [] if own is None else [own]
    return pl.pallas_call(
        body, name=name, grid=(rows // tr,),
        in_specs=[pl.BlockSpec((parts.shape[0], tr, cols), lambda i: (0, i, 0))] + [tile] * (len(extra) + 3),
        out_specs=[tile] * 4,
        out_shape=[jax.ShapeDtypeStruct((rows, cols), F32)] * 4,
        compiler_params=_params(1),
    )(parts, *extra, w, m, v)


def _sum_small(parts):
    def body(p_ref, out_ref):
        out_ref[...] = _sum_slabs(p_ref)

    return pl.pallas_call(body, name="sum_small", out_shape=jax.ShapeDtypeStruct(parts.shape[1:], F32))(parts)


def _adamw_whole(g, w, m, v, name):
    def body(g_ref, w_ref, m_ref, v_ref, d_ref, nm_ref, nv_ref):
        d_ref[...], nm_ref[...], nv_ref[...] = _adamw_math(w_ref[...], g_ref[...], m_ref[...], v_ref[...])

    return pl.pallas_call(body, name=name, out_shape=[jax.ShapeDtypeStruct(w.shape, F32)] * 3)(g, w, m, v)


def kernel(x, norm_pre_g, w_in, conv_w, w_out, norm_post_g, loss_target, m_norm_pre_g, m_w_in, m_conv_w, m_w_out,
           m_norm_post_g, v_norm_pre_g, v_w_in, v_conv_w, v_w_out, v_norm_post_g):
    n_conv = CONV_W // N_DEV
    w_in_g, w_out_g, conv_g = _ag_weights(w_in, w_out, conv_w)
    conv_full = conv_g[:, 0:3, 0:n_conv].transpose(1, 0, 2).reshape(3, CONV_W)
    grad_x, ht, dz, dw_out, dw_out_bf, small = _local_step(x[0], loss_target[0], norm_pre_g, norm_post_g, w_in_g,
                                                           w_out_g.reshape(D_MODEL, D_MODEL), conv_full)
    own_in, r_in, r_out, r_small = _dw_in_rs(ht, dz, dw_out_bf.reshape(N_DEV, SHARD_OUT, D_MODEL), small)
    me = 4 * lax.axis_index("x") + 2 * lax.axis_index("y") + lax.axis_index("c")
    own_out = lax.dynamic_index_in_dim(dw_out.reshape(N_DEV, SHARD_OUT, D_MODEL), me, keepdims=False)
    g_in, d_in, nm_in, nv_in = _adamw_slabs(r_in, own_in, w_in, m_w_in, v_w_in, "adamw_in", 256)
    g_out, d_out, nm_out, nv_out = _adamw_slabs(r_out, own_out, w_out, m_w_out, v_w_out, "adamw_out", SHARD_OUT)
    sums = _sum_small(r_small)
    g_pre, g_post, loss = sums[0], sums[1], sums[2, 0]
    g_conv = lax.dynamic_slice(sums[3:6, 0:CONV_W], (0, me * n_conv), (3, n_conv))
    vec = lambda a: a.reshape(1, D_MODEL)
    d_pre, nm_pre, nv_pre = _adamw_whole(vec(g_pre), vec(norm_pre_g), vec(m_norm_pre_g), vec(v_norm_pre_g), "adamw_pre")
    d_post, nm_post, nv_post = _adamw_whole(vec(g_post), vec(norm_post_g), vec(m_norm_post_g), vec(v_norm_post_g),
                                            "adamw_post")
    d_conv, nm_conv, nv_conv = _adamw_whole(g_conv, conv_w, m_conv_w, v_conv_w, "adamw_conv")
    flat = lambda a: a.reshape(D_MODEL)
    return (loss, grad_x[None], g_pre, g_in, g_conv, g_out, g_post,
            flat(d_pre), d_in, d_conv, d_out, flat(d_post),
            flat(nm_pre), nm_in, nm_conv, nm_out, flat(nm_post),
            flat(nv_pre), nv_in, nv_conv, nv_out, flat(nv_post))
```

```python
import functools

import jax
import jax.numpy as jnp
from jax import lax
from jax.experimental import pallas as pl
from jax.experimental.pallas import tpu as pltpu

F32 = jnp.float32
BF16 = jnp.bfloat16

D_MODEL = 1024
HEAD_DIM = 64
ATTN_W = 768
CONV_W = 256
IN_W = 4096
REST_W = IN_W - 3 * ATTN_W
BLK = 128
N_DEV = 8
SHARD_IN = IN_W // N_DEV
SHARD_OUT = D_MODEL // N_DEV
DILATIONS = (1, 4, 16)
ROPE_THETA = 10000.0
NORM_EPS = 1e-6
NEG = -1e30

ADAM_LR = 0.001
ADAM_B1 = 0.9
ADAM_B2 = 0.999
ADAM_EPS = 1e-08
ADAM_WD = 0.01
ADAM_STEP = 10

VMEM_LIMIT = 56 * 1024 * 1024
MESH = pl.DeviceIdType.MESH


def _params(n_grid):
    return pltpu.CompilerParams(dimension_semantics=("arbitrary",) * n_grid, vmem_limit_bytes=VMEM_LIMIT)


def _resident(shape):
    zeros = (0,) * len(shape)
    return pl.BlockSpec(shape, lambda *_: zeros, pipeline_mode=pl.Buffered(1))


def _sigmoid(a):
    return 1.0 / (1.0 + jnp.exp(-a))


def _swap_halves(t, first_half):
    return jnp.where(first_half, pltpu.roll(t, BLK - 32, 1), pltpu.roll(t, 32, 1))


def _rope_tables(seq, tm):
    half = HEAD_DIM // 2
    inv_freq = ROPE_THETA ** (-jnp.arange(half, dtype=F32) * 2.0 / HEAD_DIM)
    freq = jnp.concatenate([inv_freq] * 4)
    sign = jnp.concatenate([-jnp.ones(half, F32), jnp.ones(half, F32)] * 2)
    starts = (jnp.arange(seq // tm) * tm).astype(F32)[:, None] * freq[None, :]
    rows = jnp.arange(tm).astype(F32)[:, None] * freq[None, :]
    slab = lambda a: jnp.broadcast_to(a[:, None, :], (seq // tm, 8, BLK))
    return slab(jnp.cos(starts)), slab(jnp.sin(starts) * sign), jnp.cos(rows), jnp.sin(rows) * sign


def _rope_specs(tm):
    return [pl.BlockSpec((1, 8, BLK), lambda i: (i, 0, 0))] * 2 + [_resident((tm, BLK))] * 2


def _tile_rope(cos_start, sin_start, cos_row, sin_row):
    ca, sa, cb, sb = cos_start[0, 0:1, :], sin_start[0, 0:1, :], cos_row[...], sin_row[...]
    return ca * cb - sa * sb, sa * cb + ca * sb


N_CHUNK = ATTN_W // BLK


def _lanes(r, c):
    return slice(r * ATTN_W + c * BLK, r * ATTN_W + (c + 1) * BLK)


def _to_residues(src, chunk0, dst_refs, tmp, rows, dtype):
    assert DILATIONS == (1, 4, 16)
    dst1, dst4, dst16 = dst_refs
    n4, n16 = rows // 4, rows // 16
    for c in range(N_CHUNK):
        dst1[:, _lanes(0, c)] = src[chunk0 + c].astype(dtype)
        for r1 in range(4):
            tmp[c, r1 * n4:(r1 + 1) * n4, :] = src[chunk0 + c, pl.ds(r1, n4, stride=4), :]
        for r1 in range(4):
            dst4[:, _lanes(r1, c)] = tmp[c, r1 * n4:(r1 + 1) * n4, :].astype(dtype)
            for r2 in range(4):
                dst16[:, _lanes(4 * r2 + r1, c)] = tmp[c, pl.ds(r1 * n4 + r2, n16, stride=4), :].astype(dtype)


def _from_residue(src_ref, dst, dil, rows, accumulate, tmp=None):
    n4, n16 = rows // 4, rows // 16

    def put(where, piece):
        if accumulate:
            dst[where] += piece
        else:
            dst[where] = piece

    for c in range(N_CHUNK):
        if dil == 1:
            put((c,), src_ref[:, _lanes(0, c)].astype(F32))
            continue
        for r1 in range(4):
            if dil == 4:
                piece = src_ref[:, _lanes(r1, c)].astype(F32)
            else:
                for r2 in range(4):
                    tmp[c, pl.ds(r1 * n4 + r2, n16, stride=4), :] = src_ref[:, _lanes(4 * r2 + r1, c)].astype(F32)
                piece = tmp[c, r1 * n4:(r1 + 1) * n4, :]
            put((c, pl.ds(r1, n4, stride=4), slice(None)), piece)


def _residue_spec(tm, dil):
    return pl.BlockSpec((tm // dil, dil * ATTN_W), lambda i: (i, 0))


def _residue_shape(seq, dil, dtype):
    return jax.ShapeDtypeStruct((seq // dil, dil * ATTN_W), dtype)


def _fwd_in(x, g_pre, w_in_g, tm=512):
    seq = x.shape[0]
    n_dil = len(DILATIONS)

    def body(x_ref, g_ref, w_ref, ca_ref, sa_ref, cb_ref, sb_ref, *rest):
        qkv_refs, (zr_ref, ht_ref, qkv_scr, tmp) = rest[:3 * n_dil], rest[3 * n_dil:]
        xv = x_ref[...]
        r = lax.rsqrt(jnp.mean(xv * xv, axis=-1, keepdims=True) + NORM_EPS)
        hf = (xv * r) * g_ref[...]
        ht_ref[...] = hf.T.astype(BF16)
        h = hf.astype(BF16)
        cos, sin = _tile_rope(ca_ref, sa_ref, cb_ref, sb_ref)
        first_half = (lax.broadcasted_iota(jnp.int32, (tm, BLK), 1) & 32) == 0

        def rope(t):
            return t * cos + _swap_halves(t, first_half) * sin

        def project(j):
            return jnp.dot(h, w_ref[j], preferred_element_type=F32)

        def place(j, zj):
            for n in range(SHARD_IN // BLK):
                chunk, t = j * (SHARD_IN // BLK) + n, zj[:, n * BLK:(n + 1) * BLK]
                if chunk < N_CHUNK:
                    qkv_scr[chunk] = rope(t) * HEAD_DIM ** -0.5
                elif chunk < 2 * N_CHUNK:
                    qkv_scr[chunk] = rope(t)
                elif chunk < 3 * N_CHUNK:
                    qkv_scr[chunk] = t
                else:
                    zr_ref[:, (chunk - 3 * N_CHUNK) * BLK:(chunk - 3 * N_CHUNK + 1) * BLK] = t.astype(BF16)

        ahead = project(0)
        for j in range(N_DEV):
            zj = ahead
            if j + 1 < N_DEV:
                ahead = project(j + 1)
            place(j, zj)
            for a in range(3):
                if (a + 1) * N_CHUNK - 1 in range(j * (SHARD_IN // BLK), (j + 1) * (SHARD_IN // BLK)):
                    _to_residues(qkv_scr, a * N_CHUNK, [qkv_refs[3 * n + a] for n in range(n_dil)], tmp, tm, BF16)

    row = lambda w: pl.BlockSpec((tm, w), lambda i: (i, 0))
    outs = pl.pallas_call(
        body, name="fwd_in", grid=(seq // tm,),
        in_specs=[row(D_MODEL), _resident((1, D_MODEL)), _resident((N_DEV, D_MODEL, SHARD_IN))] + _rope_specs(tm),
        out_specs=[_residue_spec(tm, dil) for dil in DILATIONS for _ in range(3)]
        + [row(REST_W), pl.BlockSpec((D_MODEL, tm), lambda i: (0, i))],
        out_shape=[_residue_shape(seq, dil, BF16) for dil in DILATIONS for _ in range(3)]
        + [jax.ShapeDtypeStruct((seq, REST_W), BF16), jax.ShapeDtypeStruct((D_MODEL, seq), BF16)],
        scratch_shapes=[pltpu.VMEM((3 * N_CHUNK, tm, BLK), F32), pltpu.VMEM((N_CHUNK, tm, BLK), F32)],
        compiler_params=_params(1),
    )(x, g_pre.reshape(1, D_MODEL), w_in_g, *_rope_tables(seq, tm))
    qkv = [tuple(outs[3 * n:3 * n + 3]) for n in range(n_dil)]
    return qkv, outs[3 * n_dil], outs[3 * n_dil + 1]


def _band_bias(first_block):
    kj = lax.broadcasted_iota(jnp.int32, (2 * BLK, BLK), 0)
    qi = lax.broadcasted_iota(jnp.int32, (2 * BLK, BLK), 1)
    valid = (kj >= qi) & (kj <= qi + BLK)
    bias = jnp.where(valid, 0.0, NEG).astype(BF16)
    bias_first = jnp.where(valid & (kj >= BLK), 0.0, NEG).astype(BF16)
    onehot = ((kj & (BLK - 1)) == qi).astype(F32).astype(BF16)
    return onehot, bias, jnp.where(first_block, bias_first, bias)


def _stack_heads(t):
    keep0 = (lax.broadcasted_iota(jnp.int32, t.shape, 1) < HEAD_DIM).astype(F32).astype(BF16)
    return jnp.concatenate([t * keep0, t * (1 - keep0)], axis=0)


def _unstack_heads(t2, head0):
    return jnp.where(head0, t2[:BLK], t2[BLK:])


def _rows_per_head(a, head0):
    b = pltpu.roll(a, HEAD_DIM, 1)
    rows = jnp.concatenate([jnp.where(head0, a, b), jnp.where(head0, b, a)], axis=0)
    return jnp.concatenate([rows, rows], axis=1)


BLOCKS_PER_STEP = 32


def _attn_specs(length, dil, max_cols=8):
    n_blocks = length // BLK
    tb = min(BLOCKS_PER_STEP, n_blocks)
    nc = min(BLOCKS_PER_STEP // tb, max_cols)
    assert (dil * N_CHUNK) % nc == 0 and n_blocks % tb == 0
    tile = pl.BlockSpec((tb * BLK, nc * BLK), lambda c, t: (t, c))
    prev = pl.BlockSpec((BLK, nc * BLK), lambda c, t: (jnp.maximum(t * tb - 1, 0), c))
    grid = (dil * N_CHUNK // nc, n_blocks // tb)
    return tb, nc, tile, prev, grid


def _window(prev_ref, cur_ref, j, cols):
    if j == 0:
        return jnp.concatenate([prev_ref[:, cols], cur_ref[0:BLK, cols]], axis=0)
    return cur_ref[(j - 1) * BLK:(j + 1) * BLK, cols]


def _attn_fwd(q, k, v, dil):
    length = q.shape[0]
    tb, nc, tile, prev, grid = _attn_specs(length, dil)

    def body(q_ref, kc_ref, kp_ref, vc_ref, vp_ref, o_ref, lse_ref):
        head0 = lax.broadcasted_iota(jnp.int32, (BLK, BLK), 1) < HEAD_DIM
        onehot, bias, bias_start = _band_bias(pl.program_id(1) == 0)
        ones = jnp.ones((2 * BLK, BLK), BF16)
        def scores(c, j):
            rows, cols = slice(j * BLK, (j + 1) * BLK), slice(c * BLK, (c + 1) * BLK)
            q2 = jnp.concatenate([_stack_heads(q_ref[rows, cols]), onehot], axis=1)
            kk = jnp.concatenate([_window(kp_ref, kc_ref, j, cols), bias_start if j == 0 else bias], axis=1)
            return (lax.dot_general(q2, kk, (((1,), (1,)), ((), ())), preferred_element_type=F32),)

        def probabilities(c, j, s):
            m = jnp.max(s, axis=1, keepdims=True)
            return m, jnp.exp(s - m).astype(BF16)

        def outputs(c, j, m, p):
            rows, cols = slice(j * BLK, (j + 1) * BLK), slice(c * BLK, (c + 1) * BLK)
            vv = jnp.concatenate([_window(vp_ref, vc_ref, j, cols), ones], axis=1)
            pv = jnp.dot(p, vv, preferred_element_type=F32)
            den = pv[:, BLK:]
            o_ref[rows, cols] = _unstack_heads(pv[:, :BLK] / den, head0).astype(BF16)
            lse_ref[rows, cols] = _unstack_heads(m + jnp.log(den), head0)

        units = [(c, j) for c in range(nc) for j in range(tb)]
        stage1, stage2 = {}, {}
        for n in range(len(units) + 2):
            if n < len(units):
                stage1[n] = scores(*units[n])
            if 0 <= n - 1 < len(units):
                stage2[n - 1] = probabilities(*units[n - 1], *stage1.pop(n - 1))
            if 0 <= n - 2 < len(units):
                outputs(*units[n - 2], *stage2.pop(n - 2))

    return pl.pallas_call(
        body, name=f"attn_fwd_d{dil}", grid=grid,
        in_specs=[tile, tile, prev, tile, prev], out_specs=[tile, tile],
        out_shape=[jax.ShapeDtypeStruct(q.shape, BF16), jax.ShapeDtypeStruct(q.shape, F32)],
        compiler_params=_params(2),
    )(q, k, k, v, v)


def _attn_bwd(q, k, v, do, lse, delta, dil):
    length = q.shape[0]
    tb, nc, tile, prev, grid = _attn_specs(length, dil, max_cols=4)
    whole = pl.BlockSpec((length, nc * BLK), lambda c, t: (0, c))

    def body(q_ref, do_ref, lse_ref, dl_ref, kc_ref, kp_ref, vc_ref, vp_ref, dq_ref, dk_ref, dv_ref):
        t = pl.program_id(1)
        head0 = lax.broadcasted_iota(jnp.int32, (BLK, BLK), 1) < HEAD_DIM
        onehot, bias, bias_start = _band_bias(t == 0)

        def scores(c, j):
            rows, cols = slice(j * BLK, (j + 1) * BLK), slice(c * BLK, (c + 1) * BLK)
            q2 = _stack_heads(q_ref[rows, cols])
            do2 = _stack_heads(do_ref[rows, cols])
            kk = _window(kp_ref, kc_ref, j, cols)
            s = lax.dot_general(jnp.concatenate([q2, onehot], axis=1),
                                jnp.concatenate([kk, bias_start if j == 0 else bias], axis=1),
                                (((1,), (1,)), ((), ())), preferred_element_type=F32)
            dp = lax.dot_general(do2, _window(vp_ref, vc_ref, j, cols), (((1,), (1,)), ((), ())),
                                 preferred_element_type=F32)
            return q2, do2, kk, s, dp

        def probabilities(c, j, q2, do2, kk, s, dp):
            rows, cols = slice(j * BLK, (j + 1) * BLK), slice(c * BLK, (c + 1) * BLK)
            p = jnp.exp(s - _rows_per_head(lse_ref[rows, cols], head0))
            ds = (p * (dp - _rows_per_head(dl_ref[rows, cols].astype(F32), head0))).astype(BF16)
            return q2, do2, kk, p.astype(BF16), ds

        def gradients(c, j, q2, do2, kk, p, ds):
            rows, cols = slice(j * BLK, (j + 1) * BLK), slice(c * BLK, (c + 1) * BLK)
            dq2 = jnp.dot(ds, kk, preferred_element_type=F32)
            dq_ref[rows, cols] = (_unstack_heads(dq2, head0) * HEAD_DIM ** -0.5).astype(BF16)
            dk2 = lax.dot_general(ds, q2, (((0,), (0,)), ((), ())), preferred_element_type=F32)
            dv2 = lax.dot_general(p, do2, (((0,), (0,)), ((), ())), preferred_element_type=F32)
            own = pl.ds(pl.multiple_of((t * tb + j) * BLK, BLK), BLK)
            dk_ref[own, cols] = dk2[BLK:].astype(BF16)
            dv_ref[own, cols] = dv2[BLK:].astype(BF16)

            def add_to_previous():
                before = pl.ds(pl.multiple_of((t * tb + j - 1) * BLK, BLK), BLK)
                dk_ref[before, cols] = (dk_ref[before, cols].astype(F32) + dk2[:BLK]).astype(BF16)
                dv_ref[before, cols] = (dv_ref[before, cols].astype(F32) + dv2[:BLK]).astype(BF16)

            if j > 0:
                add_to_previous()
            elif grid[1] > 1:
                pl.when(t > 0)(add_to_previous)

        units = [(c, j) for c in range(nc) for j in range(tb)]
        stage1 = {0: scores(*units[0])}
        for n in range(len(units)):
            stage2 = probabilities(*units[n], *stage1.pop(n))
            if n + 1 < len(units):
                stage1[n + 1] = scores(*units[n + 1])
            gradients(*units[n], *stage2)

    return pl.pallas_call(
        body, name=f"attn_bwd_d{dil}", grid=grid,
        in_specs=[tile, tile, tile, tile, tile, prev, tile, prev], out_specs=[tile, whole, whole],
        out_shape=[jax.ShapeDtypeStruct(q.shape, BF16)] * 3,
        compiler_params=_params(2),
    )(q, do, lse, delta, k, k, v, v)


HALO = 16


def _halo_specs(tm, seq):
    before = lambda w: pl.BlockSpec((HALO, w), lambda i: (jnp.maximum(i * (tm // HALO) - 1, 0), 0))
    after = lambda w: pl.BlockSpec((HALO, w), lambda i: (jnp.minimum((i + 1) * (tm // HALO), seq // HALO - 1), 0))
    return before, after


def _conv_taps(u, before, tm):
    row = lax.broadcasted_iota(jnp.int32, u.shape, 0)
    last, last2 = before[HALO - 1:HALO, :], before[HALO - 2:HALO - 1, :]
    u1 = jnp.where(row == 0, last, pltpu.roll(u, 1, 0))
    u2 = jnp.where(row == 0, last2, jnp.where(row == 1, last, pltpu.roll(u, 2, 0)))
    return u1, u2


def _attn_combine(o_parts, lse_parts, zr, conv_w, tm=256):
    seq = zr.shape[0]
    a0, h0, b0, c0, g0 = 0, ATTN_W, ATTN_W + CONV_W, ATTN_W + 2 * CONV_W, ATTN_W + 3 * CONV_W

    def body(o1, o2, o3, l1, l2, l3, zr_ref, zp_ref, w_ref, mixed_ref, o_ref, lse1, lse2, lse3, *scr):
        i = pl.program_id(0)
        for src, dst, dil in zip((o2, o3, l2, l3), scr[:4], DILATIONS[1:] * 2):
            _from_residue(src, dst, dil, tm, accumulate=False, tmp=scr[5])
        for c in range(N_CHUNK):
            cols = slice(c * BLK, (c + 1) * BLK)
            la, lb, lc = l1[:, cols], scr[2][c], scr[3][c]
            top = jnp.maximum(jnp.maximum(la, lb), lc)
            ea, eb, ec = jnp.exp(la - top), jnp.exp(lb - top), jnp.exp(lc - top)
            den = ea + eb + ec
            inv = 1.0 / den
            o = (ea * inv) * o1[:, cols].astype(F32) + (eb * inv) * scr[0][c] + (ec * inv) * scr[1][c]
            o_ref[:, cols] = o.astype(BF16)
            scr[4][c] = top + jnp.log(den)
            ga = zr_ref[:, cols].astype(F32)
            mixed_ref[:, cols] = (o * (ga * _sigmoid(ga))).astype(BF16)
        _to_residues(scr[4], 0, (lse1, lse2, lse3), scr[5], tm, F32)
        part = lambda ref, lo, hi: ref[:, lo:hi].astype(F32)
        u = part(zr_ref, c0, g0) * part(zr_ref, h0, b0)
        before = jnp.where(i > 0, part(zp_ref, c0, g0) * part(zp_ref, h0, b0), 0.0)
        u1, u2 = _conv_taps(u, before, tm)
        y = u2 * w_ref[0:1, :] + u1 * w_ref[1:2, :] + u * w_ref[2:3, :]
        gc = part(zr_ref, g0, REST_W)
        mixed_ref[:, ATTN_W:] = ((part(zr_ref, b0, c0) * y) * (gc * _sigmoid(gc))).astype(BF16)

    row = lambda w: pl.BlockSpec((tm, w), lambda i: (i, 0))
    before, _ = _halo_specs(tm, seq)
    views = [_residue_spec(tm, dil) for dil in DILATIONS]
    outs = pl.pallas_call(
        body, name="attn_combine", grid=(seq // tm,),
        in_specs=views * 2 + [row(REST_W), before(REST_W), _resident((3, CONV_W))],
        out_specs=[row(D_MODEL), row(ATTN_W)] + views,
        out_shape=[jax.ShapeDtypeStruct((seq, D_MODEL), BF16), jax.ShapeDtypeStruct((seq, ATTN_W), BF16)]
        + [_residue_shape(seq, dil, F32) for dil in DILATIONS],
        scratch_shapes=[pltpu.VMEM((N_CHUNK, tm, BLK), F32)] * 6,
        compiler_params=_params(1),
    )(*o_parts, *lse_parts, zr, zr, conv_w)
    return outs[0], outs[1], outs[2:]


def _out_loss_bwd(mixed, w_out_g, x, target, g_post, tm=512, n_parts=2):
    seq = x.shape[0]

    def body(mx_ref, w_ref, x_ref, t_ref, g_ref, dout_ref, dmx_ref, dw_ref, dwb_ref, st_ref):
        i = pl.program_id(0)
        g = g_ref[...]
        parts = [slice(n * (tm // n_parts), (n + 1) * (tm // n_parts)) for n in range(n_parts)]

        def project(rows):
            return jnp.dot(mx_ref[rows, :], w_ref[...], preferred_element_type=F32)

        def head(rows, y):
            r = lax.rsqrt(jnp.mean(y * y, axis=-1, keepdims=True) + NORM_EPS)
            yhat = y * r
            err = (x_ref[rows, :] + yhat * g) - t_ref[rows, :]
            dn = err * (1.0 / D_MODEL)
            dout_ref[rows, :] = dn
            tg = dn * g
            dy = (r * (tg - yhat * jnp.mean(tg * yhat, axis=-1, keepdims=True))).astype(BF16)
            dmx_ref[rows, :] = lax.dot_general(dy, w_ref[...], (((1,), (1,)), ((), ())),
                                               preferred_element_type=F32).astype(BF16)
            return dy, jnp.sum(dn * yhat, axis=0, keepdims=True), jnp.sum(err * err)

        ahead, done = project(parts[0]), []
        for n, rows in enumerate(parts):
            y = ahead
            if n + 1 < n_parts:
                ahead = project(parts[n + 1])
            done.append(head(rows, y))
        dy = jnp.concatenate([d[0] for d in done], axis=0)
        dw = lax.dot_general(mx_ref[...], dy, (((0,), (0,)), ((), ())), preferred_element_type=F32)
        gsum = functools.reduce(lambda a, b: a + b, [d[1] for d in done])
        lsum = jnp.broadcast_to(0.5 / D_MODEL * functools.reduce(lambda a, b: a + b, [d[2] for d in done]),
                                (1, D_MODEL))

        @pl.when(i == 0)
        def _():
            dw_ref[...] = dw
            st_ref[...] = jnp.zeros_like(st_ref)
            st_ref[0:1, :] = gsum
            st_ref[1:2, :] = lsum

        @pl.when(i > 0)
        def _():
            dw_ref[...] += dw
            st_ref[0:1, :] += gsum
            st_ref[1:2, :] += lsum

        @pl.when(i == seq // tm - 1)
        def _():
            dwb_ref[...] = dw_ref[...].astype(BF16)

    row = lambda w: pl.BlockSpec((tm, w), lambda i: (i, 0))
    whole = pl.BlockSpec((D_MODEL, D_MODEL), lambda i: (0, 0))
    return pl.pallas_call(
        body, name="out_loss_bwd", grid=(seq // tm,),
        in_specs=[row(D_MODEL), _resident((D_MODEL, D_MODEL)), row(D_MODEL), row(D_MODEL), _resident((1, D_MODEL))],
        out_specs=[row(D_MODEL), row(D_MODEL), whole, whole, pl.BlockSpec((8, D_MODEL), lambda i: (0, 0))],
        out_shape=[jax.ShapeDtypeStruct((seq, D_MODEL), F32), jax.ShapeDtypeStruct((seq, D_MODEL), BF16),
                   jax.ShapeDtypeStruct((D_MODEL, D_MODEL), F32), jax.ShapeDtypeStruct((D_MODEL, D_MODEL), BF16),
                   jax.ShapeDtypeStruct((8, D_MODEL), F32)],
        compiler_params=_params(1),
    )(mixed, w_out_g, x, target, g_post.reshape(1, D_MODEL))


def _head_sum(prod, same_head):
    hi = prod.astype(BF16)
    lo = (prod - hi.astype(F32)).astype(BF16)
    return (jnp.dot(hi, same_head, preferred_element_type=F32) + jnp.dot(lo, same_head, preferred_element_type=F32))


def _gate_bwd(dmixed, zr, o, conv_w, tm=256):
    seq = zr.shape[0]
    n_tiles = seq // tm
    n_dil = len(DILATIONS)
    a0, h0, b0, c0, g0 = 0, ATTN_W, ATTN_W + CONV_W, ATTN_W + 2 * CONV_W, ATTN_W + 3 * CONV_W

    def body(dm_ref, dmn_ref, zr_ref, zp_ref, zn_ref, o_ref, w_ref, *rest):
        do_refs, dl_refs = rest[:n_dil], rest[n_dil:2 * n_dil]
        dz_ref, dw_ref, do_scr, dl_scr, tmp = rest[2 * n_dil:]
        i = pl.program_id(0)
        part = lambda ref, lo, hi: ref[:, lo:hi].astype(F32)
        ga = part(zr_ref, a0, h0)
        sg = _sigmoid(ga)
        dattn = part(dm_ref, 0, ATTN_W)
        ov = o_ref[...].astype(F32)
        do = dattn * (ga * sg)
        dz_ref[:, a0:h0] = (dattn * ov * (sg * (1.0 + ga * (1.0 - sg)))).astype(BF16)
        li = lax.broadcasted_iota(jnp.int32, (BLK, BLK), 0) // HEAD_DIM
        lj = lax.broadcasted_iota(jnp.int32, (BLK, BLK), 1) // HEAD_DIM
        same_head = (li == lj).astype(BF16)
        prod = do * ov
        for c in range(N_CHUNK):
            cols = slice(c * BLK, (c + 1) * BLK)
            do_scr[c] = do[:, cols]
            dl_scr[c] = _head_sum(prod[:, cols], same_head)
        _to_residues(do_scr, 0, do_refs, tmp, tm, BF16)
        _to_residues(dl_scr, 0, dl_refs, tmp, tm, BF16)

        ch, cb, cc, gc = (part(zr_ref, lo, hi) for lo, hi in ((h0, b0), (b0, c0), (c0, g0), (g0, REST_W)))
        u = cc * ch
        before = jnp.where(i > 0, part(zp_ref, c0, g0) * part(zp_ref, h0, b0), 0.0)
        u1, u2 = _conv_taps(u, before, tm)
        w0, w1, w2 = w_ref[0:1, :], w_ref[1:2, :], w_ref[2:3, :]
        y = u2 * w0 + u1 * w1 + u * w2
        sc = _sigmoid(gc)
        silu_c = gc * sc
        dconv = part(dm_ref, ATTN_W, D_MODEL)
        dz_ref[:, b0:c0] = (dconv * y * silu_c).astype(BF16)
        dz_ref[:, g0:] = (dconv * (cb * y) * (sc * (1.0 + gc * (1.0 - sc)))).astype(BF16)
        dy = dconv * cb * silu_c
        gn = part(zn_ref, g0, REST_W)
        after = jnp.where(i < n_tiles - 1,
                          part(dmn_ref, ATTN_W, D_MODEL) * part(zn_ref, b0, c0) * (gn * _sigmoid(gn)), 0.0)
        row = lax.broadcasted_iota(jnp.int32, dy.shape, 0)
        nxt, nxt2 = after[0:1, :], after[1:2, :]
        dy1 = jnp.where(row == tm - 1, nxt, pltpu.roll(dy, tm - 1, 0))
        dy2 = jnp.where(row == tm - 1, nxt2, jnp.where(row == tm - 2, nxt, pltpu.roll(dy, tm - 2, 0)))
        du = dy * w2 + dy1 * w1 + dy2 * w0
        dz_ref[:, c0:g0] = (du * ch).astype(BF16)
        dz_ref[:, h0:b0] = (du * cc).astype(BF16)
        dws = [jnp.sum(dy * u2, axis=0, keepdims=True), jnp.sum(dy * u1, axis=0, keepdims=True),
               jnp.sum(dy * u, axis=0, keepdims=True)]

        @pl.when(i == 0)
        def _():
            dw_ref[...] = jnp.zeros_like(dw_ref)

        for n, part in enumerate(dws):
            dw_ref[n:n + 1, :] += part

    row_spec = lambda w: pl.BlockSpec((tm, w), lambda i: (i, 0))
    before, after = _halo_specs(tm, seq)
    views = [_residue_spec(tm, dil) for dil in DILATIONS]
    outs = pl.pallas_call(
        body, name="gate_bwd", grid=(n_tiles,),
        in_specs=[row_spec(D_MODEL), after(D_MODEL), row_spec(REST_W), before(REST_W), after(REST_W),
                  row_spec(ATTN_W), _resident((3, CONV_W))],
        out_specs=views * 2 + [row_spec(REST_W), pl.BlockSpec((8, CONV_W), lambda i: (0, 0))],
        out_shape=[_residue_shape(seq, dil, BF16) for dil in DILATIONS] * 2
        + [jax.ShapeDtypeStruct((seq, REST_W), BF16), jax.ShapeDtypeStruct((8, CONV_W), F32)],
        scratch_shapes=[pltpu.VMEM((N_CHUNK, tm, BLK), F32)] * 3,
        compiler_params=_params(1),
    )(dmixed, dmixed, zr, zr, zr, o, conv_w)
    return outs[:n_dil], outs[n_dil:2 * n_dil], outs[2 * n_dil], outs[2 * n_dil + 1]


def _in_bwd(dqs, dks, dvs, dzr, x, d_out, g_pre, w_in_g, tm=256):
    seq = x.shape[0]

    def body(q1, q2, q3, k1, k2, k3, v1, v2, v3, dzr_ref, ca_ref, sa_ref, cb_ref, sb_ref, x_ref, dout_ref, g_ref,
             w_ref, dz_ref, gx_ref, st_ref, *scratch):
        i = pl.program_id(0)
        cos, sin = _tile_rope(ca_ref, sa_ref, cb_ref, sb_ref)
        first_half = (lax.broadcasted_iota(jnp.int32, (tm, BLK), 1) & 32) == 0
        streams = [(q1, q2, q3), (k1, k2, k3), (v1, v2, v3)]
        from4, from16, tmp = scratch[0:3], scratch[3:6], scratch[6]
        per_slab = SHARD_IN // BLK

        def unrope(t):
            return t * cos - _swap_halves(t, first_half) * sin

        def to_positions(a):
            _from_residue(streams[a][1], from4[a], 4, tm, accumulate=False)
            _from_residue(streams[a][2], from16[a], 16, tm, accumulate=False, tmp=tmp)

        def assemble(j):
            for chunk in range(j * per_slab, (j + 1) * per_slab):
                a, c = divmod(chunk, N_CHUNK)
                if a < 3:
                    total = streams[a][0][:, _lanes(0, c)].astype(F32) + from4[a][c] + from16[a][c]
                    val = (unrope(total) if a < 2 else total).astype(BF16)
                else:
                    val = dzr_ref[:, (chunk - 3 * N_CHUNK) * BLK:(chunk - 3 * N_CHUNK + 1) * BLK]
                dz_ref[:, chunk * BLK:(chunk + 1) * BLK] = val
            return dz_ref[:, j * SHARD_IN:(j + 1) * SHARD_IN]

        order = [j for j in range(N_DEV) if j * per_slab >= 3 * N_CHUNK]
        order += [j for j in range(N_DEV) if j not in order]
        assert order[2] * per_slab >= 3 * N_CHUNK
        ahead = assemble(order[0])
        dh = None
        for n, j in enumerate(order):
            part = lax.dot_general(ahead, w_ref[j], (((1,), (1,)), ((), ())), preferred_element_type=F32)
            if n < 3:
                to_positions(n)
            if n + 1 < N_DEV:
                ahead = assemble(order[n + 1])
            dh = part if dh is None else dh + part
        xv = x_ref[...]
        r = lax.rsqrt(jnp.mean(xv * xv, axis=-1, keepdims=True) + NORM_EPS)
        xhat = xv * r
        tg = dh * g_ref[...]
        gx_ref[...] = dout_ref[...] + r * (tg - xhat * jnp.mean(tg * xhat, axis=-1, keepdims=True))
        gsum = jnp.sum(dh * xhat, axis=0, keepdims=True)

        @pl.when(i == 0)
        def _():
            st_ref[...] = jnp.zeros_like(st_ref)

        st_ref[0:1, :] += gsum

    row = lambda w: pl.BlockSpec((tm, w), lambda i: (i, 0))
    return pl.pallas_call(
        body, name="in_bwd", grid=(seq // tm,),
        in_specs=[_residue_spec(tm, dil) for dil in DILATIONS] * 3
        + [row(REST_W)] + _rope_specs(tm) + [row(D_MODEL), row(D_MODEL), _resident((1, D_MODEL)),
                                             _resident((N_DEV, D_MODEL, SHARD_IN))],
        out_specs=[row(IN_W), row(D_MODEL), pl.BlockSpec((8, D_MODEL), lambda i: (0, 0))],
        out_shape=[jax.ShapeDtypeStruct((seq, IN_W), BF16), jax.ShapeDtypeStruct((seq, D_MODEL), F32),
                   jax.ShapeDtypeStruct((8, D_MODEL), F32)],
        scratch_shapes=[pltpu.VMEM((N_CHUNK, tm, BLK), F32)] * 7,
        compiler_params=_params(1),
    )(*dqs, *dks, *dvs, dzr, *_rope_tables(seq, tm), x, d_out, g_pre.reshape(1, D_MODEL), w_in_g)


def _local_step(x, target, g_pre, g_post, w_in_g, w_out_g, conv_w):
    qkv, zr, ht = _fwd_in(x, g_pre, w_in_g)
    parts = [_attn_fwd(*qkv[n], dil) for n, dil in enumerate(DILATIONS)]
    mixed, o, lse = _attn_combine([p[0] for p in parts], [p[1] for p in parts], zr, conv_w)
    d_out, dmixed, dw_out, dw_out_bf, st_post = _out_loss_bwd(mixed, w_out_g, x, target, g_post)
    do, delta, dzr, dconv = _gate_bwd(dmixed, zr, o, conv_w)
    grads = [_attn_bwd(*qkv[n], do[n], lse[n], delta[n], dil) for n, dil in enumerate(DILATIONS)]
    dz, grad_x, st_pre = _in_bwd([g[0] for g in grads], [g[1] for g in grads], [g[2] for g in grads], dzr,
                                 x, d_out, g_pre, w_in_g)
    conv_rows = jnp.pad(dconv[0:3], ((0, 0), (0, D_MODEL - CONV_W)))
    small = jnp.concatenate([st_pre[0:1], st_post[0:2], conv_rows, jnp.zeros((2, D_MODEL), F32)], axis=0)
    return grad_x, ht, dz, dw_out, dw_out_bf, small


def _coords():
    return lax.axis_index("x"), lax.axis_index("y"), lax.axis_index("c")


def _peer(k):
    x, y, c = _coords()
    px = 1 - x if k & 4 else x
    py = 1 - y if k & 2 else y
    pc = 1 - c if k & 1 else c
    return (px, py, pc), 4 * px + 2 * py + pc


HBM_SPEC = pl.BlockSpec(memory_space=pltpu.HBM)
VMEM_SPEC = pl.BlockSpec(memory_space=pltpu.VMEM)


def _ag_weights(w_in, w_out, conv_w):
    def body(win_ref, wout_ref, cw_ref, gin_ref, gout_ref, gcw_ref, win_bf, wout_bf, cw_pad, send_sems, recv_sems,
             local_sems):
        x, y, c = _coords()
        me, sibling = (x, y, c), (x, y, 1 - c)
        flip = lambda v, yes: v + yes - 2 * v * yes
        x_nbr, y_nbr, diagonal = (1 - x, y, c), (x, 1 - y, c), (1 - x, 1 - y, c)
        relay_from = (flip(x, 1 - c), flip(y, c), c)
        relay_to = (flip(x, c), flip(y, 1 - c), c)
        slab = lambda px, py, pc: 4 * px + 2 * py + pc
        win_bf[...] = win_ref[...].astype(BF16)
        wout_bf[...] = wout_ref[...].astype(BF16)
        cw_pad[...] = jnp.zeros_like(cw_pad)
        cw_pad[0:3, 0:CONV_W // N_DEV] = cw_ref[...]
        mine = [win_bf, wout_bf, cw_pad]
        gathered = [gin_ref, gout_ref, gcw_ref]

        def copies(k, block, to, own=False):
            return [pltpu.make_async_remote_copy(src_ref=mine[a] if own else gathered[a].at[slab(*block)],
                                                 dst_ref=gathered[a].at[slab(*block)], send_sem=send_sems.at[k, a],
                                                 recv_sem=recv_sems.at[k, a], device_id=to, device_id_type=MESH)
                    for a in range(3)]

        local = [pltpu.make_async_copy(mine[a], gathered[a].at[slab(*me)], local_sems.at[a]) for a in range(3)]
        for cp in local:
            cp.start()
        started = copies(0, me, sibling, own=True) + copies(1, me, x_nbr, own=True) + copies(2, me, y_nbr, own=True)
        for cp in started:
            cp.start()
        for cp in copies(1, x_nbr, me) + copies(2, y_nbr, me):
            cp.wait_recv()
        onward = copies(3, relay_from, relay_to) + copies(4, x_nbr, sibling) + copies(5, y_nbr, sibling)
        for cp in onward:
            cp.start()
        for cp in copies(3, diagonal, me):
            cp.wait_recv()
        last = copies(6, diagonal, sibling)
        for cp in last:
            cp.start()
        for cp in copies(0, sibling, me):
            cp.wait_recv()
        for k, origin in ((4, (1 - x, y, 1 - c)), (5, (x, 1 - y, 1 - c)), (6, (1 - x, 1 - y, 1 - c))):
            for cp in copies(k, origin, me):
                cp.wait_recv()
        for cp in started + onward + last:
            cp.wait_send()
        for cp in local:
            cp.wait()

    return pl.pallas_call(
        body, name="ag_weights",
        in_specs=[VMEM_SPEC, VMEM_SPEC, VMEM_SPEC], out_specs=[HBM_SPEC, HBM_SPEC, HBM_SPEC],
        out_shape=[jax.ShapeDtypeStruct((N_DEV, D_MODEL, SHARD_IN), BF16),
                   jax.ShapeDtypeStruct((N_DEV, SHARD_OUT, D_MODEL), BF16),
                   jax.ShapeDtypeStruct((N_DEV, 8, BLK), F32)],
        scratch_shapes=[pltpu.VMEM((D_MODEL, SHARD_IN), BF16), pltpu.VMEM((SHARD_OUT, D_MODEL), BF16),
                        pltpu.VMEM((8, BLK), F32), pltpu.SemaphoreType.DMA((N_DEV - 1, 3)),
                        pltpu.SemaphoreType.DMA((N_DEV - 1, 3)), pltpu.SemaphoreType.DMA((3,))],
        compiler_params=pltpu.CompilerParams(vmem_limit_bytes=VMEM_LIMIT),
    )(w_in, w_out, conv_w)


def _dw_in_rs(ht, dz, dw_out, small):
    seq = dz.shape[0]

    def body(cols_ref, ht_ref, dz_ref, dout_ref, sm_ref, own_ref, rin_ref, rout_ref, rsm_ref, to_sibling, landed,
             to_chip, zero_buf, d2d_send, d2d_recv, ici_send, ici_recv, side_send, side_recv, local_sems):
        del cols_ref
        step = pl.program_id(0)
        x, y, c = _coords()
        me = 4 * x + 2 * y + c
        sibling = (x, y, 1 - c)
        chips = [(1 - x, y), (x, 1 - y), (1 - x, 1 - y)]

        def d2d(n):
            return pltpu.make_async_remote_copy(src_ref=to_sibling.at[n], dst_ref=landed.at[n], send_sem=d2d_send.at[n],
                                                recv_sem=d2d_recv.at[n], device_id=sibling, device_id_type=MESH)

        def ici(n):
            return pltpu.make_async_remote_copy(src_ref=to_chip.at[n], dst_ref=rin_ref.at[n], send_sem=ici_send.at[n],
                                                recv_sem=ici_recv.at[n], device_id=(*chips[n], c), device_id_type=MESH)

        def side(k, mine):
            peer, peer_idx = _peer(k)
            src_slab, dst_slab = (peer_idx, me) if mine else (me, peer_idx)
            pairs = [(dout_ref.at[src_slab], rout_ref.at[dst_slab]), (sm_ref, rsm_ref.at[dst_slab])]
            return [pltpu.make_async_remote_copy(src_ref=src, dst_ref=dst, send_sem=side_send.at[k - 1, a],
                                                 recv_sem=side_recv.at[k - 1, a], device_id=peer, device_id_type=MESH)
                    for a, (src, dst) in enumerate(pairs)]

        local = [pltpu.make_async_copy(zero_buf, rout_ref.at[me], local_sems.at[0]),
                 pltpu.make_async_copy(sm_ref, rsm_ref.at[me], local_sems.at[1])]

        @pl.when(step == 0)
        def _():
            zero_buf[...] = jnp.zeros_like(zero_buf)
            for cp in local:
                cp.start()
            for k in range(1, N_DEV):
                for cp in side(k, mine=True):
                    cp.start()

        dw = jnp.dot(ht_ref[...], dz_ref[...], preferred_element_type=F32)
        for n, at in zip(range(4), (0, 1, 2, N_DEV - 2)):
            @pl.when(step == at)
            def _(n=n):
                to_sibling[n] = dw.astype(BF16)
                d2d(n).start()

        for n in range(3):
            @pl.when(step == 3 + n)
            def _(n=n):
                d2d(n).wait_recv()
                to_chip[n] = (dw + landed[n].astype(F32)).astype(BF16)
                ici(n).start()

        @pl.when(step == N_DEV - 1)
        def _():
            d2d(3).wait_recv()
            own_ref[...] = dw + landed[3].astype(F32)
            for n in range(3):
                ici(n).wait_recv()
            for k in range(1, N_DEV):
                for cp in side(k, mine=False):
                    cp.wait_recv()
            for n in range(4):
                d2d(n).wait_send()
            for n in range(3):
                ici(n).wait_send()
            for k in range(1, N_DEV):
                for cp in side(k, mine=True):
                    cp.wait_send()
            for cp in local:
                cp.wait()

    x, y, c = _coords()
    others = [(1 - x, y), (x, 1 - y), (1 - x, 1 - y)]
    order = [(*chip, 1 - c) for chip in others] + [(*chip, c) for chip in others] + [(x, y, 1 - c), (x, y, c)]
    cols = jnp.stack([4 * px + 2 * py + pc for px, py, pc in order]).astype(jnp.int32)
    slab = (D_MODEL, SHARD_IN)
    grid_spec = pltpu.PrefetchScalarGridSpec(
        num_scalar_prefetch=1, grid=(N_DEV,),
        in_specs=[pl.BlockSpec((D_MODEL, seq), lambda s, cols: (0, 0), pipeline_mode=pl.Buffered(1)),
                  pl.BlockSpec((seq, SHARD_IN), lambda s, cols: (0, cols[s])), HBM_SPEC, HBM_SPEC],
        out_specs=[pl.BlockSpec(slab, lambda s, cols: (0, 0)), HBM_SPEC, HBM_SPEC, HBM_SPEC],
        scratch_shapes=[pltpu.VMEM((4, *slab), BF16), pltpu.VMEM((4, *slab), BF16), pltpu.VMEM((3, *slab), BF16),
                        pltpu.VMEM((SHARD_OUT, D_MODEL), BF16),
                        pltpu.SemaphoreType.DMA((4,)), pltpu.SemaphoreType.DMA((4,)),
                        pltpu.SemaphoreType.DMA((3,)), pltpu.SemaphoreType.DMA((3,)),
                        pltpu.SemaphoreType.DMA((N_DEV - 1, 2)), pltpu.SemaphoreType.DMA((N_DEV - 1, 2)),
                        pltpu.SemaphoreType.DMA((2,))])
    return pl.pallas_call(
        body, name="dw_in_rs", grid_spec=grid_spec,
        out_shape=[jax.ShapeDtypeStruct(slab, F32),
                   jax.ShapeDtypeStruct((3, *slab), BF16),
                   jax.ShapeDtypeStruct((N_DEV, SHARD_OUT, D_MODEL), BF16),
                   jax.ShapeDtypeStruct((N_DEV, 8, D_MODEL), F32)],
        compiler_params=_params(1),
    )(cols, ht, dz, dw_out, small)


def _adamw_math(w, g, m, v):
    m = ADAM_B1 * m + (1.0 - ADAM_B1) * g
    v = ADAM_B2 * v + (1.0 - ADAM_B2) * (g * g)
    m_hat = m / (1.0 - ADAM_B1 ** ADAM_STEP)
    v_hat = v / (1.0 - ADAM_B2 ** ADAM_STEP)
    delta = -ADAM_LR * (m_hat / (jnp.sqrt(v_hat) + ADAM_EPS) + ADAM_WD * w)
    return delta, m, v


def _sum_slabs(ref, first=None):
    total = ref[0].astype(F32) if first is None else first + ref[0].astype(F32)
    for s in range(1, ref.shape[0]):
        total = total + ref[s].astype(F32)
    return total


def _adamw_slabs(parts, own, w, m, v, name, tr):
    rows, cols = w.shape
    tile = pl.BlockSpec((tr, cols), lambda i: (i, 0))

    def body(p_ref, *refs):
        own_ref = refs[0] if own is not None else None
        w_ref, m_ref, v_ref, g_ref, d_ref, nm_ref, nv_ref = refs[-7:]
        g = _sum_slabs(p_ref, None if own_ref is None else own_ref[...])
        g_ref[...] = g
        d_ref[...], nm_ref[...], nv_ref[...] = _adamw_math(w_ref[...], g, m_ref[...], v_ref[...])

    extra = [] if own is None else [own]
    return pl.pallas_call(
        body, name=name, grid=(rows // tr,),
        in_specs=[pl.BlockSpec((parts.shape[0], tr, cols), lambda i: (0, i, 0))] + [tile] * (len(extra) + 3),
        out_specs=[tile] * 4,
        out_shape=[jax.ShapeDtypeStruct((rows, cols), F32)] * 4,
        compiler_params=_params(1),
    )(parts, *extra, w, m, v)


def _sum_small(parts):
    def body(p_ref, out_ref):
        out_ref[...] = _sum_slabs(p_ref)

    return pl.pallas_call(body, name="sum_small", out_shape=jax.ShapeDtypeStruct(parts.shape[1:], F32))(parts)


def _adamw_whole(g, w, m, v, name):
    def body(g_ref, w_ref, m_ref, v_ref, d_ref, nm_ref, nv_ref):
        d_ref[...], nm_ref[...], nv_ref[...] = _adamw_math(w_ref[...], g_ref[...], m_ref[...], v_ref[...])

    return pl.pallas_call(body, name=name, out_shape=[jax.ShapeDtypeStruct(w.shape, F32)] * 3)(g, w, m, v)


def kernel(x, norm_pre_g, w_in, conv_w, w_out, norm_post_g, loss_target, m_norm_pre_g, m_w_in, m_conv_w, m_w_out,
           m_norm_post_g, v_norm_pre_g, v_w_in, v_conv_w, v_w_out, v_norm_post_g):
    n_conv = CONV_W // N_DEV
    w_in_g, w_out_g, conv_g = _ag_weights(w_in, w_out, conv_w)
    conv_full = conv_g[:, 0:3, 0:n_conv].transpose(1, 0, 2).reshape(3, CONV_W)
    grad_x, ht, dz, dw_out, dw_out_bf, small = _local_step(x[0], loss_target[0], norm_pre_g, norm_post_g, w_in_g,
                                                           w_out_g.reshape(D_MODEL, D_MODEL), conv_full)
    own_in, r_in, r_out, r_small = _dw_in_rs(ht, dz, dw_out_bf.reshape(N_DEV, SHARD_OUT, D_MODEL), small)
    me = 4 * lax.axis_index("x") + 2 * lax.axis_index("y") + lax.axis_index("c")
    own_out = lax.dynamic_index_in_dim(dw_out.reshape(N_DEV, SHARD_OUT, D_MODEL), me, keepdims=False)
    g_in, d_in, nm_in, nv_in = _adamw_slabs(r_in, own_in, w_in, m_w_in, v_w_in, "adamw_in", 256)
    g_out, d_out, nm_out, nv_out = _adamw_slabs(r_out, own_out, w_out, m_w_out, v_w_out, "adamw_out", SHARD_OUT)
    sums = _sum_small(r_small)
    g_pre, g_post, loss = sums[0], sums[1], sums[2, 0]
    g_conv = lax.dynamic_slice(sums[3:6, 0:CONV_W], (0, me * n_conv), (3, n_conv))
    vec = lambda a: a.reshape(1, D_MODEL)
    d_pre, nm_pre, nv_pre = _adamw_whole(vec(g_pre), vec(norm_pre_g), vec(m_norm_pre_g), vec(v_norm_pre_g), "adamw_pre")
    d_post, nm_post, nv_post = _adamw_whole(vec(g_post), vec(norm_post_g), vec(m_norm_post_g), vec(v_norm_post_g),
                                            "adamw_post")
    d_conv, nm_conv, nv_conv = _adamw_whole(g_conv, conv_w, m_conv_w, v_conv_w, "adamw_conv")
    flat = lambda a: a.reshape(D_MODEL)
    return (loss, grad_x[None], g_pre, g_in, g_conv, g_out, g_post,
            flat(d_pre), d_in, d_conv, d_out, flat(d_post),
            flat(nm_pre), nm_in, nm_conv, nm_out, flat(nm_post),
            flat(nv_pre), nv_in, nv_conv, nv_out, flat(nv_post))
```

```python
import functools

import jax
import jax.numpy as jnp
from jax import lax
from jax.experimental import pallas as pl
from jax.experimental.pallas import tpu as pltpu

F32 = jnp.float32
BF16 = jnp.bfloat16

D_MODEL = 1024
HEAD_DIM = 64
ATTN_W = 768
CONV_W = 256
IN_W = 4096
REST_W = IN_W - 3 * ATTN_W
BLK = 128
N_DEV = 8
SHARD_IN = IN_W // N_DEV
SHARD_OUT = D_MODEL // N_DEV
DILATIONS = (1, 4, 16)
ROPE_THETA = 10000.0
NORM_EPS = 1e-6
NEG = -1e30

ADAM_LR = 0.001
ADAM_B1 = 0.9
ADAM_B2 = 0.999
ADAM_EPS = 1e-08
ADAM_WD = 0.01
ADAM_STEP = 10

VMEM_LIMIT = 56 * 1024 * 1024
MESH = pl.DeviceIdType.MESH


def _params(n_grid):
    return pltpu.CompilerParams(dimension_semantics=("arbitrary",) * n_grid, vmem_limit_bytes=VMEM_LIMIT)


def _resident(shape):
    zeros = (0,) * len(shape)
    return pl.BlockSpec(shape, lambda *_: zeros, pipeline_mode=pl.Buffered(1))


def _sigmoid(a):
    return 1.0 / (1.0 + jnp.exp(-a))


def _swap_halves(t, first_half):
    return jnp.where(first_half, pltpu.roll(t, BLK - 32, 1), pltpu.roll(t, 32, 1))


def _rope_tables(seq, tm):
    half = HEAD_DIM // 2
    inv_freq = ROPE_THETA ** (-jnp.arange(half, dtype=F32) * 2.0 / HEAD_DIM)
    freq = jnp.concatenate([inv_freq] * 4)
    sign = jnp.concatenate([-jnp.ones(half, F32), jnp.ones(half, F32)] * 2)
    starts = (jnp.arange(seq // tm) * tm).astype(F32)[:, None] * freq[None, :]
    rows = jnp.arange(tm).astype(F32)[:, None] * freq[None, :]
    slab = lambda a: jnp.broadcast_to(a[:, None, :], (seq // tm, 8, BLK))
    return slab(jnp.cos(starts)), slab(jnp.sin(starts) * sign), jnp.cos(rows), jnp.sin(rows) * sign


def _rope_specs(tm):
    return [pl.BlockSpec((1, 8, BLK), lambda i: (i, 0, 0))] * 2 + [_resident((tm, BLK))] * 2


def _tile_rope(cos_start, sin_start, cos_row, sin_row):
    ca, sa, cb, sb = cos_start[0, 0:1, :], sin_start[0, 0:1, :], cos_row[...], sin_row[...]
    return ca * cb - sa * sb, sa * cb + ca * sb


N_CHUNK = ATTN_W // BLK


def _lanes(r, c):
    return slice(r * ATTN_W + c * BLK, r * ATTN_W + (c + 1) * BLK)


def _to_residues(src, chunk0, dst_refs, tmp, rows, dtype):
    assert DILATIONS == (1, 4, 16)
    dst1, dst4, dst16 = dst_refs
    n4, n16 = rows // 4, rows // 16
    for c in range(N_CHUNK):
        dst1[:, _lanes(0, c)] = src[chunk0 + c].astype(dtype)
        for r1 in range(4):
            tmp[c, r1 * n4:(r1 + 1) * n4, :] = src[chunk0 + c, pl.ds(r1, n4, stride=4), :]
        for r1 in range(4):
            dst4[:, _lanes(r1, c)] = tmp[c, r1 * n4:(r1 + 1) * n4, :].astype(dtype)
            for r2 in range(4):
                dst16[:, _lanes(4 * r2 + r1, c)] = tmp[c, pl.ds(r1 * n4 + r2, n16, stride=4), :].astype(dtype)


def _from_residue(src_ref, dst, dil, rows, accumulate, tmp=None):
    n4, n16 = rows // 4, rows // 16

    def put(where, piece):
        if accumulate:
            dst[where] += piece
        else:
            dst[where] = piece

    for c in range(N_CHUNK):
        if dil == 1:
            put((c,), src_ref[:, _lanes(0, c)].astype(F32))
            continue
        for r1 in range(4):
            if dil == 4:
                piece = src_ref[:, _lanes(r1, c)].astype(F32)
            else:
                for r2 in range(4):
                    tmp[c, pl.ds(r1 * n4 + r2, n16, stride=4), :] = src_ref[:, _lanes(4 * r2 + r1, c)].astype(F32)
                piece = tmp[c, r1 * n4:(r1 + 1) * n4, :]
            put((c, pl.ds(r1, n4, stride=4), slice(None)), piece)


def _residue_spec(tm, dil):
    return pl.BlockSpec((tm // dil, dil * ATTN_W), lambda i: (i, 0))


def _residue_shape(seq, dil, dtype):
    return jax.ShapeDtypeStruct((seq // dil, dil * ATTN_W), dtype)


def _fwd_in(x, g_pre, w_in_g, tm=512):
    seq = x.shape[0]
    n_dil = len(DILATIONS)

    def body(x_ref, g_ref, w_ref, ca_ref, sa_ref, cb_ref, sb_ref, *rest):
        qkv_refs, (zr_ref, ht_ref, qkv_scr, tmp) = rest[:3 * n_dil], rest[3 * n_dil:]
        xv = x_ref[...]
        r = lax.rsqrt(jnp.mean(xv * xv, axis=-1, keepdims=True) + NORM_EPS)
        hf = (xv * r) * g_ref[...]
        h = hf.astype(BF16)
        ht_ref[...] = h.T
        cos, sin = _tile_rope(ca_ref, sa_ref, cb_ref, sb_ref)
        first_half = (lax.broadcasted_iota(jnp.int32, (tm, BLK), 1) & 32) == 0

        def rope(t):
            return t * cos + _swap_halves(t, first_half) * sin

        def project(j):
            return jnp.dot(h, w_ref[j], preferred_element_type=F32)

        def place(j, zj):
            for n in range(SHARD_IN // BLK):
                chunk, t = j * (SHARD_IN // BLK) + n, zj[:, n * BLK:(n + 1) * BLK]
                if chunk < N_CHUNK:
                    qkv_scr[chunk] = rope(t) * HEAD_DIM ** -0.5
                elif chunk < 2 * N_CHUNK:
                    qkv_scr[chunk] = rope(t)
                elif chunk < 3 * N_CHUNK:
                    qkv_scr[chunk] = t
                else:
                    zr_ref[:, (chunk - 3 * N_CHUNK) * BLK:(chunk - 3 * N_CHUNK + 1) * BLK] = t.astype(BF16)

        ahead = project(0)
        for j in range(N_DEV):
            zj = ahead
            if j + 1 < N_DEV:
                ahead = project(j + 1)
            place(j, zj)
            for a in range(3):
                if (a + 1) * N_CHUNK - 1 in range(j * (SHARD_IN // BLK), (j + 1) * (SHARD_IN // BLK)):
                    _to_residues(qkv_scr, a * N_CHUNK, [qkv_refs[3 * n + a] for n in range(n_dil)], tmp, tm, BF16)

    row = lambda w: pl.BlockSpec((tm, w), lambda i: (i, 0))
    outs = pl.pallas_call(
        body, name="fwd_in", grid=(seq // tm,),
        in_specs=[row(D_MODEL), _resident((1, D_MODEL)), _resident((N_DEV, D_MODEL, SHARD_IN))] + _rope_specs(tm),
        out_specs=[_residue_spec(tm, dil) for dil in DILATIONS for _ in range(3)]
        + [row(REST_W), pl.BlockSpec((D_MODEL, tm), lambda i: (0, i))],
        out_shape=[_residue_shape(seq, dil, BF16) for dil in DILATIONS for _ in range(3)]
        + [jax.ShapeDtypeStruct((seq, REST_W), BF16), jax.ShapeDtypeStruct((D_MODEL, seq), BF16)],
        scratch_shapes=[pltpu.VMEM((3 * N_CHUNK, tm, BLK), F32), pltpu.VMEM((N_CHUNK, tm, BLK), F32)],
        compiler_params=_params(1),
    )(x, g_pre.reshape(1, D_MODEL), w_in_g, *_rope_tables(seq, tm))
    qkv = [tuple(outs[3 * n:3 * n + 3]) for n in range(n_dil)]
    return qkv, outs[3 * n_dil], outs[3 * n_dil + 1]


def _band_bias(first_block):
    kj = lax.broadcasted_iota(jnp.int32, (2 * BLK, BLK), 0)
    qi = lax.broadcasted_iota(jnp.int32, (2 * BLK, BLK), 1)
    valid = (kj >= qi) & (kj <= qi + BLK)
    bias = jnp.where(valid, 0.0, NEG).astype(BF16)
    bias_first = jnp.where(valid & (kj >= BLK), 0.0, NEG).astype(BF16)
    onehot = ((kj & (BLK - 1)) == qi).astype(F32).astype(BF16)
    return onehot, bias, jnp.where(first_block, bias_first, bias)


def _stack_heads(t):
    keep0 = (lax.broadcasted_iota(jnp.int32, t.shape, 1) < HEAD_DIM).astype(F32).astype(BF16)
    return jnp.concatenate([t * keep0, t * (1 - keep0)], axis=0)


def _unstack_heads(t2, head0):
    return jnp.where(head0, t2[:BLK], t2[BLK:])


def _rows_per_head(a, head0):
    b = pltpu.roll(a, HEAD_DIM, 1)
    rows = jnp.concatenate([jnp.where(head0, a, b), jnp.where(head0, b, a)], axis=0)
    return jnp.concatenate([rows, rows], axis=1)


BLOCKS_PER_STEP = 32


def _attn_specs(length, dil, max_cols=8, units=BLOCKS_PER_STEP):
    n_blocks = length // BLK
    tb = min(units, n_blocks)
    nc = min(units // tb, max_cols)
    assert (dil * N_CHUNK) % nc == 0 and n_blocks % tb == 0
    tile = pl.BlockSpec((tb * BLK, nc * BLK), lambda c, t: (t, c))
    prev = pl.BlockSpec((BLK, nc * BLK), lambda c, t: (jnp.maximum(t * tb - 1, 0), c))
    grid = (dil * N_CHUNK // nc, n_blocks // tb)
    return tb, nc, tile, prev, grid


def _window(prev_ref, cur_ref, j, cols):
    if j == 0:
        return jnp.concatenate([prev_ref[:, cols], cur_ref[0:BLK, cols]], axis=0)
    return cur_ref[(j - 1) * BLK:(j + 1) * BLK, cols]


def _attn_fwd(q, k, v, dil):
    length = q.shape[0]
    tb, nc, tile, prev, grid = _attn_specs(length, dil)

    def body(q_ref, kc_ref, kp_ref, vc_ref, vp_ref, o_ref, lse_ref):
        head0 = lax.broadcasted_iota(jnp.int32, (BLK, BLK), 1) < HEAD_DIM
        onehot, bias, bias_start = _band_bias(pl.program_id(1) == 0)
        ones = jnp.ones((2 * BLK, BLK), BF16)
        def scores(c, j):
            rows, cols = slice(j * BLK, (j + 1) * BLK), slice(c * BLK, (c + 1) * BLK)
            q2 = jnp.concatenate([_stack_heads(q_ref[rows, cols]), onehot], axis=1)
            kk = jnp.concatenate([_window(kp_ref, kc_ref, j, cols), bias_start if j == 0 else bias], axis=1)
            return (lax.dot_general(q2, kk, (((1,), (1,)), ((), ())), preferred_element_type=F32),)

        def probabilities(c, j, s):
            m = jnp.max(s, axis=1, keepdims=True)
            return m, jnp.exp(s - m).astype(BF16)

        def outputs(c, j, m, p):
            rows, cols = slice(j * BLK, (j + 1) * BLK), slice(c * BLK, (c + 1) * BLK)
            vv = jnp.concatenate([_window(vp_ref, vc_ref, j, cols), ones], axis=1)
            pv = jnp.dot(p, vv, preferred_element_type=F32)
            den = pv[:, BLK:]
            o_ref[rows, cols] = _unstack_heads(pv[:, :BLK] / den, head0).astype(BF16)
            lse_ref[rows, cols] = _unstack_heads(m + jnp.log(den), head0)

        units = [(c, j) for c in range(nc) for j in range(tb)]
        stage1, stage2 = {}, {}
        for n in range(len(units) + 2):
            if n < len(units):
                stage1[n] = scores(*units[n])
            if 0 <= n - 1 < len(units):
                stage2[n - 1] = probabilities(*units[n - 1], *stage1.pop(n - 1))
            if 0 <= n - 2 < len(units):
                outputs(*units[n - 2], *stage2.pop(n - 2))

    return pl.pallas_call(
        body, name=f"attn_fwd_d{dil}", grid=grid,
        in_specs=[tile, tile, prev, tile, prev], out_specs=[tile, tile],
        out_shape=[jax.ShapeDtypeStruct(q.shape, BF16), jax.ShapeDtypeStruct(q.shape, F32)],
        compiler_params=_params(2),
    )(q, k, k, v, v)


def _attn_bwd(q, k, v, do, lse, delta, dil):
    length = q.shape[0]
    tb, nc, tile, prev, grid = _attn_specs(length, dil, max_cols=4, units=2 * BLOCKS_PER_STEP)
    whole = pl.BlockSpec((length, nc * BLK), lambda c, t: (0, c))

    def body(q_ref, do_ref, lse_ref, dl_ref, kc_ref, kp_ref, vc_ref, vp_ref, dq_ref, dk_ref, dv_ref):
        t = pl.program_id(1)
        head0 = lax.broadcasted_iota(jnp.int32, (BLK, BLK), 1) < HEAD_DIM
        onehot, bias, bias_start = _band_bias(t == 0)

        def scores(c, j):
            rows, cols = slice(j * BLK, (j + 1) * BLK), slice(c * BLK, (c + 1) * BLK)
            q2 = _stack_heads(q_ref[rows, cols])
            do2 = _stack_heads(do_ref[rows, cols])
            kk = _window(kp_ref, kc_ref, j, cols)
            s = lax.dot_general(jnp.concatenate([q2, onehot], axis=1),
                                jnp.concatenate([kk, bias_start if j == 0 else bias], axis=1),
                                (((1,), (1,)), ((), ())), preferred_element_type=F32)
            dp = lax.dot_general(do2, _window(vp_ref, vc_ref, j, cols), (((1,), (1,)), ((), ())),
                                 preferred_element_type=F32)
            return q2, do2, kk, s, dp

        def probabilities(c, j, q2, do2, kk, s, dp):
            rows, cols = slice(j * BLK, (j + 1) * BLK), slice(c * BLK, (c + 1) * BLK)
            p = jnp.exp(s - _rows_per_head(lse_ref[rows, cols], head0))
            ds = (p * (dp - _rows_per_head(dl_ref[rows, cols].astype(F32), head0))).astype(BF16)
            return q2, do2, kk, p.astype(BF16), ds

        def gradients(c, j, q2, do2, kk, p, ds):
            rows, cols = slice(j * BLK, (j + 1) * BLK), slice(c * BLK, (c + 1) * BLK)
            dq2 = jnp.dot(ds, kk, preferred_element_type=F32)
            dq_ref[rows, cols] = (_unstack_heads(dq2, head0) * HEAD_DIM ** -0.5).astype(BF16)
            dk2 = lax.dot_general(ds, q2, (((0,), (0,)), ((), ())), preferred_element_type=F32)
            dv2 = lax.dot_general(p, do2, (((0,), (0,)), ((), ())), preferred_element_type=F32)
            own = pl.ds(pl.multiple_of((t * tb + j) * BLK, BLK), BLK)
            dk_ref[own, cols] = dk2[BLK:].astype(BF16)
            dv_ref[own, cols] = dv2[BLK:].astype(BF16)

            def add_to_previous():
                before = pl.ds(pl.multiple_of((t * tb + j - 1) * BLK, BLK), BLK)
                dk_ref[before, cols] = (dk_ref[before, cols].astype(F32) + dk2[:BLK]).astype(BF16)
                dv_ref[before, cols] = (dv_ref[before, cols].astype(F32) + dv2[:BLK]).astype(BF16)

            if j > 0:
                add_to_previous()
            elif grid[1] > 1:
                pl.when(t > 0)(add_to_previous)

        units = [(c, j) for c in range(nc) for j in range(tb)]
        stage1 = {0: scores(*units[0])}
        for n in range(len(units)):
            stage2 = probabilities(*units[n], *stage1.pop(n))
            if n + 1 < len(units):
                stage1[n + 1] = scores(*units[n + 1])
            gradients(*units[n], *stage2)

    return pl.pallas_call(
        body, name=f"attn_bwd_d{dil}", grid=grid,
        in_specs=[tile, tile, tile, tile, tile, prev, tile, prev], out_specs=[tile, whole, whole],
        out_shape=[jax.ShapeDtypeStruct(q.shape, BF16)] * 3,
        compiler_params=_params(2),
    )(q, do, lse, delta, k, k, v, v)


HALO = 16


def _halo_specs(tm, seq):
    before = lambda w: pl.BlockSpec((HALO, w), lambda i: (jnp.maximum(i * (tm // HALO) - 1, 0), 0))
    after = lambda w: pl.BlockSpec((HALO, w), lambda i: (jnp.minimum((i + 1) * (tm // HALO), seq // HALO - 1), 0))
    return before, after


def _conv_taps(u, before, tm):
    row = lax.broadcasted_iota(jnp.int32, u.shape, 0)
    last, last2 = before[HALO - 1:HALO, :], before[HALO - 2:HALO - 1, :]
    u1 = jnp.where(row == 0, last, pltpu.roll(u, 1, 0))
    u2 = jnp.where(row == 0, last2, jnp.where(row == 1, last, pltpu.roll(u, 2, 0)))
    return u1, u2


def _attn_combine(o_parts, lse_parts, zr, conv_w, tm=256):
    seq = zr.shape[0]
    a0, h0, b0, c0, g0 = 0, ATTN_W, ATTN_W + CONV_W, ATTN_W + 2 * CONV_W, ATTN_W + 3 * CONV_W

    def body(o1, o2, o3, l1, l2, l3, zr_ref, zp_ref, w_ref, mixed_ref, o_ref, lse1, lse2, lse3, *scr):
        i = pl.program_id(0)
        for src, dst, dil in zip((o2, o3, l2, l3), scr[:4], DILATIONS[1:] * 2):
            _from_residue(src, dst, dil, tm, accumulate=False, tmp=scr[5])
        for c in range(N_CHUNK):
            cols = slice(c * BLK, (c + 1) * BLK)
            la, lb, lc = l1[:, cols], scr[2][c], scr[3][c]
            top = jnp.maximum(jnp.maximum(la, lb), lc)
            ea, eb, ec = jnp.exp(la - top), jnp.exp(lb - top), jnp.exp(lc - top)
            den = ea + eb + ec
            inv = 1.0 / den
            o = (ea * inv) * o1[:, cols].astype(F32) + (eb * inv) * scr[0][c] + (ec * inv) * scr[1][c]
            o_ref[:, cols] = o.astype(BF16)
            scr[4][c] = top + jnp.log(den)
            ga = zr_ref[:, cols].astype(F32)
            mixed_ref[:, cols] = (o * (ga * _sigmoid(ga))).astype(BF16)
        _to_residues(scr[4], 0, (lse1, lse2, lse3), scr[5], tm, F32)
        part = lambda ref, lo, hi: ref[:, lo:hi].astype(F32)
        u = part(zr_ref, c0, g0) * part(zr_ref, h0, b0)
        before = jnp.where(i > 0, part(zp_ref, c0, g0) * part(zp_ref, h0, b0), 0.0)
        u1, u2 = _conv_taps(u, before, tm)
        y = u2 * w_ref[0:1, :] + u1 * w_ref[1:2, :] + u * w_ref[2:3, :]
        gc = part(zr_ref, g0, REST_W)
        mixed_ref[:, ATTN_W:] = ((part(zr_ref, b0, c0) * y) * (gc * _sigmoid(gc))).astype(BF16)

    row = lambda w: pl.BlockSpec((tm, w), lambda i: (i, 0))
    before, _ = _halo_specs(tm, seq)
    views = [_residue_spec(tm, dil) for dil in DILATIONS]
    outs = pl.pallas_call(
        body, name="attn_combine", grid=(seq // tm,),
        in_specs=views * 2 + [row(REST_W), before(REST_W), _resident((3, CONV_W))],
        out_specs=[row(D_MODEL), row(ATTN_W)] + views,
        out_shape=[jax.ShapeDtypeStruct((seq, D_MODEL), BF16), jax.ShapeDtypeStruct((seq, ATTN_W), BF16)]
        + [_residue_shape(seq, dil, F32) for dil in DILATIONS],
        scratch_shapes=[pltpu.VMEM((N_CHUNK, tm, BLK), F32)] * 6,
        compiler_params=_params(1),
    )(*o_parts, *lse_parts, zr, zr, conv_w)
    return outs[0], outs[1], outs[2:]


def _out_loss_bwd(mixed, w_out_g, x, target, g_post, tm=512, n_parts=2):
    seq = x.shape[0]

    def body(mx_ref, w_ref, x_ref, t_ref, g_ref, dout_ref, dmx_ref, dw_ref, dwb_ref, st_ref):
        i = pl.program_id(0)
        g = g_ref[...]
        parts = [slice(n * (tm // n_parts), (n + 1) * (tm // n_parts)) for n in range(n_parts)]

        def project(rows):
            return jnp.dot(mx_ref[rows, :], w_ref[...], preferred_element_type=F32)

        def head(rows, y):
            r = lax.rsqrt(jnp.mean(y * y, axis=-1, keepdims=True) + NORM_EPS)
            yhat = y * r
            err = (x_ref[rows, :] + yhat * g) - t_ref[rows, :]
            dn = err * (1.0 / D_MODEL)
            dout_ref[rows, :] = dn
            tg = dn * g
            dy = (r * (tg - yhat * jnp.mean(tg * yhat, axis=-1, keepdims=True))).astype(BF16)
            dmx_ref[rows, :] = lax.dot_general(dy, w_ref[...], (((1,), (1,)), ((), ())),
                                               preferred_element_type=F32).astype(BF16)
            return dy, jnp.sum(dn * yhat, axis=0, keepdims=True), jnp.sum(err * err)

        ahead, done = project(parts[0]), []
        for n, rows in enumerate(parts):
            y = ahead
            if n + 1 < n_parts:
                ahead = project(parts[n + 1])
            done.append(head(rows, y))
        dy = jnp.concatenate([d[0] for d in done], axis=0)
        dw = lax.dot_general(mx_ref[...], dy, (((0,), (0,)), ((), ())), preferred_element_type=F32)
        gsum = functools.reduce(lambda a, b: a + b, [d[1] for d in done])
        lsum = jnp.broadcast_to(0.5 / D_MODEL * functools.reduce(lambda a, b: a + b, [d[2] for d in done]),
                                (1, D_MODEL))

        @pl.when(i == 0)
        def _():
            dw_ref[...] = dw
            st_ref[...] = jnp.zeros_like(st_ref)
            st_ref[0:1, :] = gsum
            st_ref[1:2, :] = lsum

        @pl.when(i > 0)
        def _():
            dw_ref[...] += dw
            st_ref[0:1, :] += gsum
            st_ref[1:2, :] += lsum

        @pl.when(i == seq // tm - 1)
        def _():
            dwb_ref[...] = dw_ref[...].astype(BF16)

    row = lambda w: pl.BlockSpec((tm, w), lambda i: (i, 0))
    whole = pl.BlockSpec((D_MODEL, D_MODEL), lambda i: (0, 0))
    return pl.pallas_call(
        body, name="out_loss_bwd", grid=(seq // tm,),
        in_specs=[row(D_MODEL), _resident((D_MODEL, D_MODEL)), row(D_MODEL), row(D_MODEL), _resident((1, D_MODEL))],
        out_specs=[row(D_MODEL), row(D_MODEL), whole, whole, pl.BlockSpec((8, D_MODEL), lambda i: (0, 0))],
        out_shape=[jax.ShapeDtypeStruct((seq, D_MODEL), F32), jax.ShapeDtypeStruct((seq, D_MODEL), BF16),
                   jax.ShapeDtypeStruct((D_MODEL, D_MODEL), F32), jax.ShapeDtypeStruct((D_MODEL, D_MODEL), BF16),
                   jax.ShapeDtypeStruct((8, D_MODEL), F32)],
        compiler_params=_params(1),
    )(mixed, w_out_g, x, target, g_post.reshape(1, D_MODEL))


def _head_sum(prod, same_head):
    hi = prod.astype(BF16)
    lo = (prod - hi.astype(F32)).astype(BF16)
    return (jnp.dot(hi, same_head, preferred_element_type=F32) + jnp.dot(lo, same_head, preferred_element_type=F32))


def _gate_bwd(dmixed, zr, o, conv_w, tm=256):
    seq = zr.shape[0]
    n_tiles = seq // tm
    n_dil = len(DILATIONS)
    a0, h0, b0, c0, g0 = 0, ATTN_W, ATTN_W + CONV_W, ATTN_W + 2 * CONV_W, ATTN_W + 3 * CONV_W

    def body(dm_ref, dmn_ref, zr_ref, zp_ref, zn_ref, o_ref, w_ref, *rest):
        do_refs, dl_refs = rest[:n_dil], rest[n_dil:2 * n_dil]
        dz_ref, dw_ref, do_scr, dl_scr, tmp = rest[2 * n_dil:]
        i = pl.program_id(0)
        part = lambda ref, lo, hi: ref[:, lo:hi].astype(F32)
        ga = part(zr_ref, a0, h0)
        sg = _sigmoid(ga)
        dattn = part(dm_ref, 0, ATTN_W)
        ov = o_ref[...].astype(F32)
        do = dattn * (ga * sg)
        dz_ref[:, a0:h0] = (dattn * ov * (sg * (1.0 + ga * (1.0 - sg)))).astype(BF16)
        li = lax.broadcasted_iota(jnp.int32, (BLK, BLK), 0) // HEAD_DIM
        lj = lax.broadcasted_iota(jnp.int32, (BLK, BLK), 1) // HEAD_DIM
        same_head = (li == lj).astype(BF16)
        prod = do * ov
        for c in range(N_CHUNK):
            cols = slice(c * BLK, (c + 1) * BLK)
            do_scr[c] = do[:, cols]
            dl_scr[c] = _head_sum(prod[:, cols], same_head)
        _to_residues(do_scr, 0, do_refs, tmp, tm, BF16)
        _to_residues(dl_scr, 0, dl_refs, tmp, tm, BF16)

        ch, cb, cc, gc = (part(zr_ref, lo, hi) for lo, hi in ((h0, b0), (b0, c0), (c0, g0), (g0, REST_W)))
        u = cc * ch
        before = jnp.where(i > 0, part(zp_ref, c0, g0) * part(zp_ref, h0, b0), 0.0)
        u1, u2 = _conv_taps(u, before, tm)
        w0, w1, w2 = w_ref[0:1, :], w_ref[1:2, :], w_ref[2:3, :]
        y = u2 * w0 + u1 * w1 + u * w2
        sc = _sigmoid(gc)
        silu_c = gc * sc
        dconv = part(dm_ref, ATTN_W, D_MODEL)
        dz_ref[:, b0:c0] = (dconv * y * silu_c).astype(BF16)
        dz_ref[:, g0:] = (dconv * (cb * y) * (sc * (1.0 + gc * (1.0 - sc)))).astype(BF16)
        dy = dconv * cb * silu_c
        gn = part(zn_ref, g0, REST_W)
        after = jnp.where(i < n_tiles - 1,
                          part(dmn_ref, ATTN_W, D_MODEL) * part(zn_ref, b0, c0) * (gn * _sigmoid(gn)), 0.0)
        row = lax.broadcasted_iota(jnp.int32, dy.shape, 0)
        nxt, nxt2 = after[0:1, :], after[1:2, :]
        dy1 = jnp.where(row == tm - 1, nxt, pltpu.roll(dy, tm - 1, 0))
        dy2 = jnp.where(row == tm - 1, nxt2, jnp.where(row == tm - 2, nxt, pltpu.roll(dy, tm - 2, 0)))
        du = dy * w2 + dy1 * w1 + dy2 * w0
        dz_ref[:, c0:g0] = (du * ch).astype(BF16)
        dz_ref[:, h0:b0] = (du * cc).astype(BF16)
        dws = [jnp.sum(dy * u2, axis=0, keepdims=True), jnp.sum(dy * u1, axis=0, keepdims=True),
               jnp.sum(dy * u, axis=0, keepdims=True)]

        @pl.when(i == 0)
        def _():
            dw_ref[...] = jnp.zeros_like(dw_ref)

        for n, part in enumerate(dws):
            dw_ref[n:n + 1, :] += part

    row_spec = lambda w: pl.BlockSpec((tm, w), lambda i: (i, 0))
    before, after = _halo_specs(tm, seq)
    views = [_residue_spec(tm, dil) for dil in DILATIONS]
    outs = pl.pallas_call(
        body, name="gate_bwd", grid=(n_tiles,),
        in_specs=[row_spec(D_MODEL), after(D_MODEL), row_spec(REST_W), before(REST_W), after(REST_W),
                  row_spec(ATTN_W), _resident((3, CONV_W))],
        out_specs=views * 2 + [row_spec(REST_W), pl.BlockSpec((8, CONV_W), lambda i: (0, 0))],
        out_shape=[_residue_shape(seq, dil, BF16) for dil in DILATIONS] * 2
        + [jax.ShapeDtypeStruct((seq, REST_W), BF16), jax.ShapeDtypeStruct((8, CONV_W), F32)],
        scratch_shapes=[pltpu.VMEM((N_CHUNK, tm, BLK), F32)] * 3,
        compiler_params=_params(1),
    )(dmixed, dmixed, zr, zr, zr, o, conv_w)
    return outs[:n_dil], outs[n_dil:2 * n_dil], outs[2 * n_dil], outs[2 * n_dil + 1]


def _in_bwd(dqs, dks, dvs, dzr, x, d_out, g_pre, w_in_g, tm=256):
    seq = x.shape[0]

    def body(q1, q2, q3, k1, k2, k3, v1, v2, v3, dzr_ref, ca_ref, sa_ref, cb_ref, sb_ref, x_ref, dout_ref, g_ref,
             w_ref, dz_ref, gx_ref, st_ref, *scratch):
        i = pl.program_id(0)
        cos, sin = _tile_rope(ca_ref, sa_ref, cb_ref, sb_ref)
        first_half = (lax.broadcasted_iota(jnp.int32, (tm, BLK), 1) & 32) == 0
        streams = [(q1, q2, q3), (k1, k2, k3), (v1, v2, v3)]
        from4, from16, tmp = scratch[0:3], scratch[3:6], scratch[6]
        per_slab = SHARD_IN // BLK

        def unrope(t):
            return t * cos - _swap_halves(t, first_half) * sin

        def to_positions(a):
            _from_residue(streams[a][1], from4[a], 4, tm, accumulate=False)
            _from_residue(streams[a][2], from16[a], 16, tm, accumulate=False, tmp=tmp)

        def assemble(j):
            for chunk in range(j * per_slab, (j + 1) * per_slab):
                a, c = divmod(chunk, N_CHUNK)
                if a < 3:
                    total = streams[a][0][:, _lanes(0, c)].astype(F32) + from4[a][c] + from16[a][c]
                    val = (unrope(total) if a < 2 else total).astype(BF16)
                else:
                    val = dzr_ref[:, (chunk - 3 * N_CHUNK) * BLK:(chunk - 3 * N_CHUNK + 1) * BLK]
                dz_ref[:, chunk * BLK:(chunk + 1) * BLK] = val
            return dz_ref[:, j * SHARD_IN:(j + 1) * SHARD_IN]

        order = [j for j in range(N_DEV) if j * per_slab >= 3 * N_CHUNK]
        order += [j for j in range(N_DEV) if j not in order]
        assert order[2] * per_slab >= 3 * N_CHUNK
        ahead = assemble(order[0])
        dh = None
        for n, j in enumerate(order):
            part = lax.dot_general(ahead, w_ref[j], (((1,), (1,)), ((), ())), preferred_element_type=F32)
            if n < 3:
                to_positions(n)
            if n + 1 < N_DEV:
                ahead = assemble(order[n + 1])
            dh = part if dh is None else dh + part
        xv = x_ref[...]
        r = lax.rsqrt(jnp.mean(xv * xv, axis=-1, keepdims=True) + NORM_EPS)
        xhat = xv * r
        tg = dh * g_ref[...]
        gx_ref[...] = dout_ref[...] + r * (tg - xhat * jnp.mean(tg * xhat, axis=-1, keepdims=True))
        gsum = jnp.sum(dh * xhat, axis=0, keepdims=True)

        @pl.when(i == 0)
        def _():
            st_ref[...] = jnp.zeros_like(st_ref)

        st_ref[0:1, :] += gsum

    row = lambda w: pl.BlockSpec((tm, w), lambda i: (i, 0))
    return pl.pallas_call(
        body, name="in_bwd", grid=(seq // tm,),
        in_specs=[_residue_spec(tm, dil) for dil in DILATIONS] * 3
        + [row(REST_W)] + _rope_specs(tm) + [row(D_MODEL), row(D_MODEL), _resident((1, D_MODEL)),
                                             _resident((N_DEV, D_MODEL, SHARD_IN))],
        out_specs=[row(IN_W), row(D_MODEL), pl.BlockSpec((8, D_MODEL), lambda i: (0, 0))],
        out_shape=[jax.ShapeDtypeStruct((seq, IN_W), BF16), jax.ShapeDtypeStruct((seq, D_MODEL), F32),
                   jax.ShapeDtypeStruct((8, D_MODEL), F32)],
        scratch_shapes=[pltpu.VMEM((N_CHUNK, tm, BLK), F32)] * 7,
        compiler_params=_params(1),
    )(*dqs, *dks, *dvs, dzr, *_rope_tables(seq, tm), x, d_out, g_pre.reshape(1, D_MODEL), w_in_g)


def _local_step(x, target, g_pre, g_post, w_in_g, w_out_g, conv_w):
    qkv, zr, ht = _fwd_in(x, g_pre, w_in_g)
    parts = [_attn_fwd(*qkv[n], dil) for n, dil in enumerate(DILATIONS)]
    mixed, o, lse = _attn_combine([p[0] for p in parts], [p[1] for p in parts], zr, conv_w)
    d_out, dmixed, dw_out, dw_out_bf, st_post = _out_loss_bwd(mixed, w_out_g, x, target, g_post)
    do, delta, dzr, dconv = _gate_bwd(dmixed, zr, o, conv_w)
    grads = [_attn_bwd(*qkv[n], do[n], lse[n], delta[n], dil) for n, dil in enumerate(DILATIONS)]
    dz, grad_x, st_pre = _in_bwd([g[0] for g in grads], [g[1] for g in grads], [g[2] for g in grads], dzr,
                                 x, d_out, g_pre, w_in_g)
    conv_rows = jnp.pad(dconv[0:3], ((0, 0), (0, D_MODEL - CONV_W)))
    small = jnp.concatenate([st_pre[0:1], st_post[0:2], conv_rows, jnp.zeros((2, D_MODEL), F32)], axis=0)
    return grad_x, ht, dz, dw_out, dw_out_bf, small


def _coords():
    return lax.axis_index("x"), lax.axis_index("y"), lax.axis_index("c")


def _peer(k):
    x, y, c = _coords()
    px = 1 - x if k & 4 else x
    py = 1 - y if k & 2 else y
    pc = 1 - c if k & 1 else c
    return (px, py, pc), 4 * px + 2 * py + pc


HBM_SPEC = pl.BlockSpec(memory_space=pltpu.HBM)
VMEM_SPEC = pl.BlockSpec(memory_space=pltpu.VMEM)


def _ag_weights(w_in, w_out, conv_w):
    def body(win_ref, wout_ref, cw_ref, gin_ref, gout_ref, gcw_ref, win_bf, wout_bf, cw_pad, send_sems, recv_sems,
             local_sems):
        x, y, c = _coords()
        me, sibling = (x, y, c), (x, y, 1 - c)
        flip = lambda v, yes: v + yes - 2 * v * yes
        x_nbr, y_nbr, diagonal = (1 - x, y, c), (x, 1 - y, c), (1 - x, 1 - y, c)
        relay_from = (flip(x, 1 - c), flip(y, c), c)
        relay_to = (flip(x, c), flip(y, 1 - c), c)
        slab = lambda px, py, pc: 4 * px + 2 * py + pc
        win_bf[...] = win_ref[...].astype(BF16)
        wout_bf[...] = wout_ref[...].astype(BF16)
        cw_pad[...] = jnp.zeros_like(cw_pad)
        cw_pad[0:3, 0:CONV_W // N_DEV] = cw_ref[...]
        mine = [win_bf, wout_bf, cw_pad]
        gathered = [gin_ref, gout_ref, gcw_ref]

        def copies(k, block, to, own=False):
            return [pltpu.make_async_remote_copy(src_ref=mine[a] if own else gathered[a].at[slab(*block)],
                                                 dst_ref=gathered[a].at[slab(*block)], send_sem=send_sems.at[k, a],
                                                 recv_sem=recv_sems.at[k, a], device_id=to, device_id_type=MESH)
                    for a in range(3)]

        local = [pltpu.make_async_copy(mine[a], gathered[a].at[slab(*me)], local_sems.at[a]) for a in range(3)]
        for cp in local:
            cp.start()
        started = copies(0, me, sibling, own=True) + copies(1, me, x_nbr, own=True) + copies(2, me, y_nbr, own=True)
        for cp in started:
            cp.start()
        for cp in copies(1, x_nbr, me) + copies(2, y_nbr, me):
            cp.wait_recv()
        onward = copies(3, relay_from, relay_to) + copies(4, x_nbr, sibling) + copies(5, y_nbr, sibling)
        for cp in onward:
            cp.start()
        for cp in copies(3, diagonal, me):
            cp.wait_recv()
        last = copies(6, diagonal, sibling)
        for cp in last:
            cp.start()
        for cp in copies(0, sibling, me):
            cp.wait_recv()
        for k, origin in ((4, (1 - x, y, 1 - c)), (5, (x, 1 - y, 1 - c)), (6, (1 - x, 1 - y, 1 - c))):
            for cp in copies(k, origin, me):
                cp.wait_recv()
        for cp in started + onward + last:
            cp.wait_send()
        for cp in local:
            cp.wait()

    return pl.pallas_call(
        body, name="ag_weights",
        in_specs=[VMEM_SPEC, VMEM_SPEC, VMEM_SPEC], out_specs=[HBM_SPEC, HBM_SPEC, HBM_SPEC],
        out_shape=[jax.ShapeDtypeStruct((N_DEV, D_MODEL, SHARD_IN), BF16),
                   jax.ShapeDtypeStruct((N_DEV, SHARD_OUT, D_MODEL), BF16),
                   jax.ShapeDtypeStruct((N_DEV, 8, BLK), F32)],
        scratch_shapes=[pltpu.VMEM((D_MODEL, SHARD_IN), BF16), pltpu.VMEM((SHARD_OUT, D_MODEL), BF16),
                        pltpu.VMEM((8, BLK), F32), pltpu.SemaphoreType.DMA((N_DEV - 1, 3)),
                        pltpu.SemaphoreType.DMA((N_DEV - 1, 3)), pltpu.SemaphoreType.DMA((3,))],
        compiler_params=pltpu.CompilerParams(vmem_limit_bytes=VMEM_LIMIT),
    )(w_in, w_out, conv_w)


def _dw_in_rs(ht, dz, dw_out, small):
    seq = dz.shape[0]

    def body(cols_ref, ht_ref, dz_ref, dout_ref, sm_ref, own_ref, rin_ref, rout_ref, rsm_ref, to_sibling, landed,
             to_chip, zero_buf, d2d_send, d2d_recv, ici_send, ici_recv, side_send, side_recv, local_sems):
        del cols_ref
        step = pl.program_id(0)
        x, y, c = _coords()
        me = 4 * x + 2 * y + c
        sibling = (x, y, 1 - c)
        chips = [(1 - x, y), (x, 1 - y), (1 - x, 1 - y)]

        def d2d(n):
            return pltpu.make_async_remote_copy(src_ref=to_sibling.at[n], dst_ref=landed.at[n], send_sem=d2d_send.at[n],
                                                recv_sem=d2d_recv.at[n], device_id=sibling, device_id_type=MESH)

        def ici(n):
            return pltpu.make_async_remote_copy(src_ref=to_chip.at[n], dst_ref=rin_ref.at[n], send_sem=ici_send.at[n],
                                                recv_sem=ici_recv.at[n], device_id=(*chips[n], c), device_id_type=MESH)

        def side(k, mine):
            peer, peer_idx = _peer(k)
            src_slab, dst_slab = (peer_idx, me) if mine else (me, peer_idx)
            pairs = [(dout_ref.at[src_slab], rout_ref.at[dst_slab]), (sm_ref, rsm_ref.at[dst_slab])]
            return [pltpu.make_async_remote_copy(src_ref=src, dst_ref=dst, send_sem=side_send.at[k - 1, a],
                                                 recv_sem=side_recv.at[k - 1, a], device_id=peer, device_id_type=MESH)
                    for a, (src, dst) in enumerate(pairs)]

        local = [pltpu.make_async_copy(zero_buf, rout_ref.at[me], local_sems.at[0]),
                 pltpu.make_async_copy(sm_ref, rsm_ref.at[me], local_sems.at[1])]

        @pl.when(step == 0)
        def _():
            zero_buf[...] = jnp.zeros_like(zero_buf)
            for cp in local:
                cp.start()
            for k in range(1, N_DEV):
                for cp in side(k, mine=True):
                    cp.start()

        dw = jnp.dot(ht_ref[...], dz_ref[...], preferred_element_type=F32)
        for n, at in zip(range(4), (0, 1, 2, N_DEV - 2)):
            @pl.when(step == at)
            def _(n=n):
                to_sibling[n] = dw.astype(BF16)
                d2d(n).start()

        for n in range(3):
            @pl.when(step == 3 + n)
            def _(n=n):
                d2d(n).wait_recv()
                to_chip[n] = (dw + landed[n].astype(F32)).astype(BF16)
                ici(n).start()

        @pl.when(step == N_DEV - 1)
        def _():
            d2d(3).wait_recv()
            own_ref[...] = dw + landed[3].astype(F32)
            for n in range(3):
                ici(n).wait_recv()
            for k in range(1, N_DEV):
                for cp in side(k, mine=False):
                    cp.wait_recv()
            for n in range(4):
                d2d(n).wait_send()
            for n in range(3):
                ici(n).wait_send()
            for k in range(1, N_DEV):
                for cp in side(k, mine=True):
                    cp.wait_send()
            for cp in local:
                cp.wait()

    x, y, c = _coords()
    others = [(1 - x, y), (x, 1 - y), (1 - x, 1 - y)]
    order = [(*chip, 1 - c) for chip in others] + [(*chip, c) for chip in others] + [(x, y, 1 - c), (x, y, c)]
    cols = jnp.stack([4 * px + 2 * py + pc for px, py, pc in order]).astype(jnp.int32)
    slab = (D_MODEL, SHARD_IN)
    grid_spec = pltpu.PrefetchScalarGridSpec(
        num_scalar_prefetch=1, grid=(N_DEV,),
        in_specs=[pl.BlockSpec((D_MODEL, seq), lambda s, cols: (0, 0), pipeline_mode=pl.Buffered(1)),
                  pl.BlockSpec((seq, SHARD_IN), lambda s, cols: (0, cols[s])), HBM_SPEC, HBM_SPEC],
        out_specs=[pl.BlockSpec(slab, lambda s, cols: (0, 0)), HBM_SPEC, HBM_SPEC, HBM_SPEC],
        scratch_shapes=[pltpu.VMEM((4, *slab), BF16), pltpu.VMEM((4, *slab), BF16), pltpu.VMEM((3, *slab), BF16),
                        pltpu.VMEM((SHARD_OUT, D_MODEL), BF16),
                        pltpu.SemaphoreType.DMA((4,)), pltpu.SemaphoreType.DMA((4,)),
                        pltpu.SemaphoreType.DMA((3,)), pltpu.SemaphoreType.DMA((3,)),
                        pltpu.SemaphoreType.DMA((N_DEV - 1, 2)), pltpu.SemaphoreType.DMA((N_DEV - 1, 2)),
                        pltpu.SemaphoreType.DMA((2,))])
    return pl.pallas_call(
        body, name="dw_in_rs", grid_spec=grid_spec,
        out_shape=[jax.ShapeDtypeStruct(slab, F32),
                   jax.ShapeDtypeStruct((3, *slab), BF16),
                   jax.ShapeDtypeStruct((N_DEV, SHARD_OUT, D_MODEL), BF16),
                   jax.ShapeDtypeStruct((N_DEV, 8, D_MODEL), F32)],
        compiler_params=_params(1),
    )(cols, ht, dz, dw_out, small)


def _adamw_math(w, g, m, v):
    m = ADAM_B1 * m + (1.0 - ADAM_B1) * g
    v = ADAM_B2 * v + (1.0 - ADAM_B2) * (g * g)
    m_hat = m / (1.0 - ADAM_B1 ** ADAM_STEP)
    v_hat = v / (1.0 - ADAM_B2 ** ADAM_STEP)
    delta = -ADAM_LR * (m_hat / (jnp.sqrt(v_hat) + ADAM_EPS) + ADAM_WD * w)
    return delta, m, v


def _sum_slabs(ref, first=None):
    total = ref[0].astype(F32) if first is None else first + ref[0].astype(F32)
    for s in range(1, ref.shape[0]):
        total = total + ref[s].astype(F32)
    return total


def _adamw_slabs(parts, own, w, m, v, name, tr):
    rows, cols = w.shape
    tile = pl.BlockSpec((tr, cols), lambda i: (i, 0))

    def body(p_ref, *refs):
        own_ref = refs[0] if own is not None else None
        w_ref, m_ref, v_ref, g_ref, d_ref, nm_ref, nv_ref = refs[-7:]
        g = _sum_slabs(p_ref, None if own_ref is None else own_ref[...])
        g_ref[...] = g
        d_ref[...], nm_ref[...], nv_ref[...] = _adamw_math(w_ref[...], g, m_ref[...], v_ref[...])

    extra = [] if own is None else [own]
    return pl.pallas_call(
        body, name=name, grid=(rows // tr,),
        in_specs=[pl.BlockSpec((parts.shape[0], tr, cols), lambda i: (0, i, 0))] + [tile] * (len(extra) + 3),
        out_specs=[tile] * 4,
        out_shape=[jax.ShapeDtypeStruct((rows, cols), F32)] * 4,
        compiler_params=_params(1),
    )(parts, *extra, w, m, v)


def _sum_small(parts):
    def body(p_ref, out_ref):
        out_ref[...] = _sum_slabs(p_ref)

    return pl.pallas_call(body, name="sum_small", out_shape=jax.ShapeDtypeStruct(parts.shape[1:], F32))(parts)


def _adamw_whole(g, w, m, v, name):
    def body(g_ref, w_ref, m_ref, v_ref, d_ref, nm_ref, nv_ref):
        d_ref[...], nm_ref[...], nv_ref[...] = _adamw_math(w_ref[...], g_ref[...], m_ref[...], v_ref[...])

    return pl.pallas_call(body, name=name, out_shape=[jax.ShapeDtypeStruct(w.shape, F32)] * 3)(g, w, m, v)


def kernel(x, norm_pre_g, w_in, conv_w, w_out, norm_post_g, loss_target, m_norm_pre_g, m_w_in, m_conv_w, m_w_out,
           m_norm_post_g, v_norm_pre_g, v_w_in, v_conv_w, v_w_out, v_norm_post_g):
    n_conv = CONV_W // N_DEV
    w_in_g, w_out_g, conv_g = _ag_weights(w_in, w_out, conv_w)
    conv_full = conv_g[:, 0:3, 0:n_conv].transpose(1, 0, 2).reshape(3, CONV_W)
    grad_x, ht, dz, dw_out, dw_out_bf, small = _local_step(x[0], loss_target[0], norm_pre_g, norm_post_g, w_in_g,
                                                           w_out_g.reshape(D_MODEL, D_MODEL), conv_full)
    own_in, r_in, r_out, r_small = _dw_in_rs(ht, dz, dw_out_bf.reshape(N_DEV, SHARD_OUT, D_MODEL), small)
    me = 4 * lax.axis_index("x") + 2 * lax.axis_index("y") + lax.axis_index("c")
    own_out = lax.dynamic_index_in_dim(dw_out.reshape(N_DEV, SHARD_OUT, D_MODEL), me, keepdims=False)
    g_in, d_in, nm_in, nv_in = _adamw_slabs(r_in, own_in, w_in, m_w_in, v_w_in, "adamw_in", 256)
    g_out, d_out, nm_out, nv_out = _adamw_slabs(r_out, own_out, w_out, m_w_out, v_w_out, "adamw_out", SHARD_OUT)
    sums = _sum_small(r_small)
    g_pre, g_post, loss = sums[0], sums[1], sums[2, 0]
    g_conv = lax.dynamic_slice(sums[3:6, 0:CONV_W], (0, me * n_conv), (3, n_conv))
    vec = lambda a: a.reshape(1, D_MODEL)
    d_pre, nm_pre, nv_pre = _adamw_whole(vec(g_pre), vec(norm_pre_g), vec(m_norm_pre_g), vec(v_norm_pre_g), "adamw_pre")
    d_post, nm_post, nv_post = _adamw_whole(vec(g_post), vec(norm_post_g), vec(m_norm_post_g), vec(v_norm_post_g),
                                            "adamw_post")
    d_conv, nm_conv, nv_conv = _adamw_whole(g_conv, conv_w, m_conv_w, v_conv_w, "adamw_conv")
    flat = lambda a: a.reshape(D_MODEL)
    return (loss, grad_x[None], g_pre, g_in, g_conv, g_out, g_post,
            flat(d_pre), d_in, d_conv, d_out, flat(d_post),
            flat(nm_pre), nm_in, nm_conv, nm_out, flat(nm_post),
            flat(nv_pre), nv_in, nv_conv, nv_out, flat(nv_post))
```

```python
import functools

import jax
import jax.numpy as jnp
from jax import lax
from jax.experimental import pallas as pl
from jax.experimental.pallas import tpu as pltpu

F32 = jnp.float32
BF16 = jnp.bfloat16

D_MODEL = 1024
HEAD_DIM = 64
ATTN_W = 768
CONV_W = 256
IN_W = 4096
REST_W = IN_W - 3 * ATTN_W
BLK = 128
N_DEV = 8
SHARD_IN = IN_W // N_DEV
SHARD_OUT = D_MODEL // N_DEV
DILATIONS = (1, 4, 16)
ROPE_THETA = 10000.0
NORM_EPS = 1e-6
NEG = -1e30

ADAM_LR = 0.001
ADAM_B1 = 0.9
ADAM_B2 = 0.999
ADAM_EPS = 1e-08
ADAM_WD = 0.01
ADAM_STEP = 10

VMEM_LIMIT = 56 * 1024 * 1024
MESH = pl.DeviceIdType.MESH


def _params(n_grid):
    return pltpu.CompilerParams(dimension_semantics=("arbitrary",) * n_grid, vmem_limit_bytes=VMEM_LIMIT)


def _resident(shape):
    zeros = (0,) * len(shape)
    return pl.BlockSpec(shape, lambda *_: zeros, pipeline_mode=pl.Buffered(1))


def _sigmoid(a):
    return 1.0 / (1.0 + jnp.exp(-a))


def _swap_halves(t, first_half):
    return jnp.where(first_half, pltpu.roll(t, BLK - 32, 1), pltpu.roll(t, 32, 1))


def _rope_tables(seq, tm):
    half = HEAD_DIM // 2
    inv_freq = ROPE_THETA ** (-jnp.arange(half, dtype=F32) * 2.0 / HEAD_DIM)
    freq = jnp.concatenate([inv_freq] * 4)
    sign = jnp.concatenate([-jnp.ones(half, F32), jnp.ones(half, F32)] * 2)
    starts = (jnp.arange(seq // tm) * tm).astype(F32)[:, None] * freq[None, :]
    rows = jnp.arange(tm).astype(F32)[:, None] * freq[None, :]
    slab = lambda a: jnp.broadcast_to(a[:, None, :], (seq // tm, 8, BLK))
    return slab(jnp.cos(starts)), slab(jnp.sin(starts) * sign), jnp.cos(rows), jnp.sin(rows) * sign


def _rope_specs(tm):
    return [pl.BlockSpec((1, 8, BLK), lambda i: (i, 0, 0))] * 2 + [_resident((tm, BLK))] * 2


def _tile_rope(cos_start, sin_start, cos_row, sin_row):
    ca, sa, cb, sb = cos_start[0, 0:1, :], sin_start[0, 0:1, :], cos_row[...], sin_row[...]
    return ca * cb - sa * sb, sa * cb + ca * sb


N_CHUNK = ATTN_W // BLK


def _lanes(r, c):
    return slice(r * ATTN_W + c * BLK, r * ATTN_W + (c + 1) * BLK)


def _to_residues(src, chunk0, dst_refs, tmp, rows, dtype):
    assert DILATIONS == (1, 4, 16)
    dst1, dst4, dst16 = dst_refs
    n4, n16 = rows // 4, rows // 16
    for c in range(N_CHUNK):
        dst1[:, _lanes(0, c)] = src[chunk0 + c].astype(dtype)
        for r1 in range(4):
            tmp[c, r1 * n4:(r1 + 1) * n4, :] = src[chunk0 + c, pl.ds(r1, n4, stride=4), :]
        for r1 in range(4):
            dst4[:, _lanes(r1, c)] = tmp[c, r1 * n4:(r1 + 1) * n4, :].astype(dtype)
            for r2 in range(4):
                dst16[:, _lanes(4 * r2 + r1, c)] = tmp[c, pl.ds(r1 * n4 + r2, n16, stride=4), :].astype(dtype)


def _from_residue(src_ref, dst, dil, rows, accumulate, tmp=None):
    n4, n16 = rows // 4, rows // 16

    def put(where, piece):
        if accumulate:
            dst[where] += piece
        else:
            dst[where] = piece

    for c in range(N_CHUNK):
        if dil == 1:
            put((c,), src_ref[:, _lanes(0, c)].astype(F32))
            continue
        for r1 in range(4):
            if dil == 4:
                piece = src_ref[:, _lanes(r1, c)].astype(F32)
            else:
                for r2 in range(4):
                    tmp[c, pl.ds(r1 * n4 + r2, n16, stride=4), :] = src_ref[:, _lanes(4 * r2 + r1, c)].astype(F32)
                piece = tmp[c, r1 * n4:(r1 + 1) * n4, :]
            put((c, pl.ds(r1, n4, stride=4), slice(None)), piece)


def _residue_spec(tm, dil):
    return pl.BlockSpec((tm // dil, dil * ATTN_W), lambda i: (i, 0))


def _residue_shape(seq, dil, dtype):
    return jax.ShapeDtypeStruct((seq // dil, dil * ATTN_W), dtype)


def _fwd_in(x, g_pre, w_in_g, tm=512):
    seq = x.shape[0]
    n_dil = len(DILATIONS)

    def body(x_ref, g_ref, w_ref, ca_ref, sa_ref, cb_ref, sb_ref, *rest):
        qkv_refs, (zr_ref, ht_ref, qkv_scr, tmp) = rest[:3 * n_dil], rest[3 * n_dil:]
        xv = x_ref[...]
        r = lax.rsqrt(jnp.mean(xv * xv, axis=-1, keepdims=True) + NORM_EPS)
        hf = (xv * r) * g_ref[...]
        h = hf.astype(BF16)
        ht_ref[...] = h.T
        cos, sin = _tile_rope(ca_ref, sa_ref, cb_ref, sb_ref)
        first_half = (lax.broadcasted_iota(jnp.int32, (tm, BLK), 1) & 32) == 0

        def rope(t):
            return t * cos + _swap_halves(t, first_half) * sin

        def project(j):
            return jnp.dot(h, w_ref[j], preferred_element_type=F32)

        def place(j, zj):
            for n in range(SHARD_IN // BLK):
                chunk, t = j * (SHARD_IN // BLK) + n, zj[:, n * BLK:(n + 1) * BLK]
                if chunk < N_CHUNK:
                    qkv_scr[chunk] = rope(t) * HEAD_DIM ** -0.5
                elif chunk < 2 * N_CHUNK:
                    qkv_scr[chunk] = rope(t)
                elif chunk < 3 * N_CHUNK:
                    qkv_scr[chunk] = t
                else:
                    zr_ref[:, (chunk - 3 * N_CHUNK) * BLK:(chunk - 3 * N_CHUNK + 1) * BLK] = t.astype(BF16)

        ahead = project(0)
        for j in range(N_DEV):
            zj = ahead
            if j + 1 < N_DEV:
                ahead = project(j + 1)
            place(j, zj)
            for a in range(3):
                if (a + 1) * N_CHUNK - 1 in range(j * (SHARD_IN // BLK), (j + 1) * (SHARD_IN // BLK)):
                    _to_residues(qkv_scr, a * N_CHUNK, [qkv_refs[3 * n + a] for n in range(n_dil)], tmp, tm, BF16)

    row = lambda w: pl.BlockSpec((tm, w), lambda i: (i, 0))
    outs = pl.pallas_call(
        body, name="fwd_in", grid=(seq // tm,),
        in_specs=[row(D_MODEL), _resident((1, D_MODEL)), _resident((N_DEV, D_MODEL, SHARD_IN))] + _rope_specs(tm),
        out_specs=[_residue_spec(tm, dil) for dil in DILATIONS for _ in range(3)]
        + [row(REST_W), pl.BlockSpec((D_MODEL, tm), lambda i: (0, i))],
        out_shape=[_residue_shape(seq, dil, BF16) for dil in DILATIONS for _ in range(3)]
        + [jax.ShapeDtypeStruct((seq, REST_W), BF16), jax.ShapeDtypeStruct((D_MODEL, seq), BF16)],
        scratch_shapes=[pltpu.VMEM((3 * N_CHUNK, tm, BLK), F32), pltpu.VMEM((N_CHUNK, tm, BLK), F32)],
        compiler_params=_params(1),
    )(x, g_pre.reshape(1, D_MODEL), w_in_g, *_rope_tables(seq, tm))
    qkv = [tuple(outs[3 * n:3 * n + 3]) for n in range(n_dil)]
    return qkv, outs[3 * n_dil], outs[3 * n_dil + 1]


def _band_bias(first_block):
    kj = lax.broadcasted_iota(jnp.int32, (2 * BLK, BLK), 0)
    qi = lax.broadcasted_iota(jnp.int32, (2 * BLK, BLK), 1)
    valid = (kj >= qi) & (kj <= qi + BLK)
    bias = jnp.where(valid, 0.0, NEG).astype(BF16)
    bias_first = jnp.where(valid & (kj >= BLK), 0.0, NEG).astype(BF16)
    onehot = ((kj & (BLK - 1)) == qi).astype(F32).astype(BF16)
    return onehot, bias, jnp.where(first_block, bias_first, bias)


def _stack_heads(t):
    keep0 = (lax.broadcasted_iota(jnp.int32, t.shape, 1) < HEAD_DIM).astype(F32).astype(BF16)
    return jnp.concatenate([t * keep0, t * (1 - keep0)], axis=0)


def _unstack_heads(t2, head0):
    return jnp.where(head0, t2[:BLK], t2[BLK:])


def _rows_per_head(a, head0):
    b = pltpu.roll(a, HEAD_DIM, 1)
    rows = jnp.concatenate([jnp.where(head0, a, b), jnp.where(head0, b, a)], axis=0)
    return jnp.concatenate([rows, rows], axis=1)


BLOCKS_PER_STEP = 32


def _attn_specs(length, dil, max_cols=8, units=BLOCKS_PER_STEP):
    n_blocks = length // BLK
    tb = min(units, n_blocks)
    nc = max(n for n in range(1, min(units // tb, max_cols) + 1) if (dil * N_CHUNK) % n == 0)
    assert n_blocks % tb == 0
    tile = pl.BlockSpec((tb * BLK, nc * BLK), lambda c, t: (t, c))
    prev = pl.BlockSpec((BLK, nc * BLK), lambda c, t: (jnp.maximum(t * tb - 1, 0), c))
    grid = (dil * N_CHUNK // nc, n_blocks // tb)
    return tb, nc, tile, prev, grid


def _window(prev_ref, cur_ref, j, cols):
    if j == 0:
        return jnp.concatenate([prev_ref[:, cols], cur_ref[0:BLK, cols]], axis=0)
    return cur_ref[(j - 1) * BLK:(j + 1) * BLK, cols]


def _attn_fwd(q, k, v, dil):
    length = q.shape[0]
    tb, nc, tile, prev, grid = _attn_specs(length, dil)

    def body(q_ref, kc_ref, kp_ref, vc_ref, vp_ref, o_ref, lse_ref):
        head0 = lax.broadcasted_iota(jnp.int32, (BLK, BLK), 1) < HEAD_DIM
        onehot, bias, bias_start = _band_bias(pl.program_id(1) == 0)
        ones = jnp.ones((2 * BLK, BLK), BF16)
        def scores(c, j):
            rows, cols = slice(j * BLK, (j + 1) * BLK), slice(c * BLK, (c + 1) * BLK)
            q2 = jnp.concatenate([_stack_heads(q_ref[rows, cols]), onehot], axis=1)
            kk = jnp.concatenate([_window(kp_ref, kc_ref, j, cols), bias_start if j == 0 else bias], axis=1)
            return (lax.dot_general(q2, kk, (((1,), (1,)), ((), ())), preferred_element_type=F32),)

        def probabilities(c, j, s):
            m = jnp.max(s, axis=1, keepdims=True)
            return m, jnp.exp(s - m).astype(BF16)

        def outputs(c, j, m, p):
            rows, cols = slice(j * BLK, (j + 1) * BLK), slice(c * BLK, (c + 1) * BLK)
            vv = jnp.concatenate([_window(vp_ref, vc_ref, j, cols), ones], axis=1)
            pv = jnp.dot(p, vv, preferred_element_type=F32)
            den = pv[:, BLK:]
            o_ref[rows, cols] = _unstack_heads(pv[:, :BLK] / den, head0).astype(BF16)
            lse_ref[rows, cols] = _unstack_heads(m + jnp.log(den), head0)

        units = [(c, j) for c in range(nc) for j in range(tb)]
        stage1, stage2 = {}, {}
        for n in range(len(units) + 2):
            if n < len(units):
                stage1[n] = scores(*units[n])
            if 0 <= n - 1 < len(units):
                stage2[n - 1] = probabilities(*units[n - 1], *stage1.pop(n - 1))
            if 0 <= n - 2 < len(units):
                outputs(*units[n - 2], *stage2.pop(n - 2))

    return pl.pallas_call(
        body, name=f"attn_fwd_d{dil}", grid=grid,
        in_specs=[tile, tile, prev, tile, prev], out_specs=[tile, tile],
        out_shape=[jax.ShapeDtypeStruct(q.shape, BF16), jax.ShapeDtypeStruct(q.shape, F32)],
        compiler_params=_params(2),
    )(q, k, k, v, v)


def _attn_bwd(q, k, v, do, lse, delta, dil):
    length = q.shape[0]
    tb, nc, tile, prev, grid = _attn_specs(length, dil, max_cols=4)
    whole = pl.BlockSpec((length, nc * BLK), lambda c, t: (0, c))

    def body(q_ref, do_ref, lse_ref, dl_ref, kc_ref, kp_ref, vc_ref, vp_ref, dq_ref, dk_ref, dv_ref):
        t = pl.program_id(1)
        head0 = lax.broadcasted_iota(jnp.int32, (BLK, BLK), 1) < HEAD_DIM
        onehot, bias, bias_start = _band_bias(t == 0)

        def scores(c, j):
            rows, cols = slice(j * BLK, (j + 1) * BLK), slice(c * BLK, (c + 1) * BLK)
            q2 = _stack_heads(q_ref[rows, cols])
            do2 = _stack_heads(do_ref[rows, cols])
            kk = _window(kp_ref, kc_ref, j, cols)
            s = lax.dot_general(jnp.concatenate([q2, onehot], axis=1),
                                jnp.concatenate([kk, bias_start if j == 0 else bias], axis=1),
                                (((1,), (1,)), ((), ())), preferred_element_type=F32)
            dp = lax.dot_general(do2, _window(vp_ref, vc_ref, j, cols), (((1,), (1,)), ((), ())),
                                 preferred_element_type=F32)
            return q2, do2, kk, s, dp

        def probabilities(c, j, q2, do2, kk, s, dp):
            rows, cols = slice(j * BLK, (j + 1) * BLK), slice(c * BLK, (c + 1) * BLK)
            p = jnp.exp(s - _rows_per_head(lse_ref[rows, cols], head0))
            ds = (p * (dp - _rows_per_head(dl_ref[rows, cols].astype(F32), head0))).astype(BF16)
            return q2, do2, kk, p.astype(BF16), ds

        def gradients(c, j, q2, do2, kk, p, ds):
            rows, cols = slice(j * BLK, (j + 1) * BLK), slice(c * BLK, (c + 1) * BLK)
            dq2 = jnp.dot(ds, kk, preferred_element_type=F32)
            dq_ref[rows, cols] = (_unstack_heads(dq2, head0) * HEAD_DIM ** -0.5).astype(BF16)
            dk2 = lax.dot_general(ds, q2, (((0,), (0,)), ((), ())), preferred_element_type=F32)
            dv2 = lax.dot_general(p, do2, (((0,), (0,)), ((), ())), preferred_element_type=F32)
            own = pl.ds(pl.multiple_of((t * tb + j) * BLK, BLK), BLK)
            dk_ref[own, cols] = dk2[BLK:].astype(BF16)
            dv_ref[own, cols] = dv2[BLK:].astype(BF16)

            def add_to_previous():
                before = pl.ds(pl.multiple_of((t * tb + j - 1) * BLK, BLK), BLK)
                dk_ref[before, cols] = (dk_ref[before, cols].astype(F32) + dk2[:BLK]).astype(BF16)
                dv_ref[before, cols] = (dv_ref[before, cols].astype(F32) + dv2[:BLK]).astype(BF16)

            if j > 0:
                add_to_previous()
            elif grid[1] > 1:
                pl.when(t > 0)(add_to_previous)

        units = [(c, j) for c in range(nc) for j in range(tb)]
        stage1 = {0: scores(*units[0])}
        for n in range(len(units)):
            stage2 = probabilities(*units[n], *stage1.pop(n))
            if n + 1 < len(units):
                stage1[n + 1] = scores(*units[n + 1])
            gradients(*units[n], *stage2)

    return pl.pallas_call(
        body, name=f"attn_bwd_d{dil}", grid=grid,
        in_specs=[tile, tile, tile, tile, tile, prev, tile, prev], out_specs=[tile, whole, whole],
        out_shape=[jax.ShapeDtypeStruct(q.shape, BF16)] * 3,
        compiler_params=_params(2),
    )(q, do, lse, delta, k, k, v, v)


HALO = 16


def _halo_specs(tm, seq):
    before = lambda w: pl.BlockSpec((HALO, w), lambda i: (jnp.maximum(i * (tm // HALO) - 1, 0), 0))
    after = lambda w: pl.BlockSpec((HALO, w), lambda i: (jnp.minimum((i + 1) * (tm // HALO), seq // HALO - 1), 0))
    return before, after


def _conv_taps(u, before, tm):
    row = lax.broadcasted_iota(jnp.int32, u.shape, 0)
    last, last2 = before[HALO - 1:HALO, :], before[HALO - 2:HALO - 1, :]
    u1 = jnp.where(row == 0, last, pltpu.roll(u, 1, 0))
    u2 = jnp.where(row == 0, last2, jnp.where(row == 1, last, pltpu.roll(u, 2, 0)))
    return u1, u2


def _attn_combine(o_parts, lse_parts, zr, conv_w, tm=256):
    seq = zr.shape[0]
    a0, h0, b0, c0, g0 = 0, ATTN_W, ATTN_W + CONV_W, ATTN_W + 2 * CONV_W, ATTN_W + 3 * CONV_W

    def body(o1, o2, o3, l1, l2, l3, zr_ref, zp_ref, w_ref, mixed_ref, o_ref, lse1, lse2, lse3, *scr):
        i = pl.program_id(0)
        for src, dst, dil in zip((o2, o3, l2, l3), scr[:4], DILATIONS[1:] * 2):
            _from_residue(src, dst, dil, tm, accumulate=False, tmp=scr[5])
        for c in range(N_CHUNK):
            cols = slice(c * BLK, (c + 1) * BLK)
            la, lb, lc = l1[:, cols], scr[2][c], scr[3][c]
            top = jnp.maximum(jnp.maximum(la, lb), lc)
            ea, eb, ec = jnp.exp(la - top), jnp.exp(lb - top), jnp.exp(lc - top)
            den = ea + eb + ec
            inv = 1.0 / den
            o = (ea * inv) * o1[:, cols].astype(F32) + (eb * inv) * scr[0][c] + (ec * inv) * scr[1][c]
            o_ref[:, cols] = o.astype(BF16)
            scr[4][c] = top + jnp.log(den)
            ga = zr_ref[:, cols].astype(F32)
            mixed_ref[:, cols] = (o * (ga * _sigmoid(ga))).astype(BF16)
        _to_residues(scr[4], 0, (lse1, lse2, lse3), scr[5], tm, F32)
        part = lambda ref, lo, hi: ref[:, lo:hi].astype(F32)
        u = part(zr_ref, c0, g0) * part(zr_ref, h0, b0)
        before = jnp.where(i > 0, part(zp_ref, c0, g0) * part(zp_ref, h0, b0), 0.0)
        u1, u2 = _conv_taps(u, before, tm)
        y = u2 * w_ref[0:1, :] + u1 * w_ref[1:2, :] + u * w_ref[2:3, :]
        gc = part(zr_ref, g0, REST_W)
        mixed_ref[:, ATTN_W:] = ((part(zr_ref, b0, c0) * y) * (gc * _sigmoid(gc))).astype(BF16)

    row = lambda w: pl.BlockSpec((tm, w), lambda i: (i, 0))
    before, _ = _halo_specs(tm, seq)
    views = [_residue_spec(tm, dil) for dil in DILATIONS]
    outs = pl.pallas_call(
        body, name="attn_combine", grid=(seq // tm,),
        in_specs=views * 2 + [row(REST_W), before(REST_W), _resident((3, CONV_W))],
        out_specs=[row(D_MODEL), row(ATTN_W)] + views,
        out_shape=[jax.ShapeDtypeStruct((seq, D_MODEL), BF16), jax.ShapeDtypeStruct((seq, ATTN_W), BF16)]
        + [_residue_shape(seq, dil, F32) for dil in DILATIONS],
        scratch_shapes=[pltpu.VMEM((N_CHUNK, tm, BLK), F32)] * 6,
        compiler_params=_params(1),
    )(*o_parts, *lse_parts, zr, zr, conv_w)
    return outs[0], outs[1], outs[2:]


def _out_loss_bwd(mixed, w_out_g, x, target, g_post, tm=512, n_parts=2):
    seq = x.shape[0]

    def body(mx_ref, w_ref, x_ref, t_ref, g_ref, dout_ref, dmx_ref, dw_ref, dwb_ref, st_ref):
        i = pl.program_id(0)
        g = g_ref[...]
        parts = [slice(n * (tm // n_parts), (n + 1) * (tm // n_parts)) for n in range(n_parts)]

        def project(rows):
            return jnp.dot(mx_ref[rows, :], w_ref[...], preferred_element_type=F32)

        def head(rows, y):
            r = lax.rsqrt(jnp.mean(y * y, axis=-1, keepdims=True) + NORM_EPS)
            yhat = y * r
            err = (x_ref[rows, :] + yhat * g) - t_ref[rows, :]
            dn = err * (1.0 / D_MODEL)
            dout_ref[rows, :] = dn
            tg = dn * g
            dy = (r * (tg - yhat * jnp.mean(tg * yhat, axis=-1, keepdims=True))).astype(BF16)
            dmx_ref[rows, :] = lax.dot_general(dy, w_ref[...], (((1,), (1,)), ((), ())),
                                               preferred_element_type=F32).astype(BF16)
            return dy, jnp.sum(dn * yhat, axis=0, keepdims=True), jnp.sum(err * err)

        ahead, done = project(parts[0]), []
        for n, rows in enumerate(parts):
            y = ahead
            if n + 1 < n_parts:
                ahead = project(parts[n + 1])
            done.append(head(rows, y))
        dy = jnp.concatenate([d[0] for d in done], axis=0)
        dw = lax.dot_general(mx_ref[...], dy, (((0,), (0,)), ((), ())), preferred_element_type=F32)
        gsum = functools.reduce(lambda a, b: a + b, [d[1] for d in done])
        lsum = jnp.broadcast_to(0.5 / D_MODEL * functools.reduce(lambda a, b: a + b, [d[2] for d in done]),
                                (1, D_MODEL))

        @pl.when(i == 0)
        def _():
            dw_ref[...] = dw
            st_ref[...] = jnp.zeros_like(st_ref)
            st_ref[0:1, :] = gsum
            st_ref[1:2, :] = lsum

        @pl.when(i > 0)
        def _():
            dw_ref[...] += dw
            st_ref[0:1, :] += gsum
            st_ref[1:2, :] += lsum

        @pl.when(i == seq // tm - 1)
        def _():
            dwb_ref[...] = dw_ref[...].astype(BF16)

    row = lambda w: pl.BlockSpec((tm, w), lambda i: (i, 0))
    whole = pl.BlockSpec((D_MODEL, D_MODEL), lambda i: (0, 0))
    return pl.pallas_call(
        body, name="out_loss_bwd", grid=(seq // tm,),
        in_specs=[row(D_MODEL), _resident((D_MODEL, D_MODEL)), row(D_MODEL), row(D_MODEL), _resident((1, D_MODEL))],
        out_specs=[row(D_MODEL), row(D_MODEL), whole, whole, pl.BlockSpec((8, D_MODEL), lambda i: (0, 0))],
        out_shape=[jax.ShapeDtypeStruct((seq, D_MODEL), F32), jax.ShapeDtypeStruct((seq, D_MODEL), BF16),
                   jax.ShapeDtypeStruct((D_MODEL, D_MODEL), F32), jax.ShapeDtypeStruct((D_MODEL, D_MODEL), BF16),
                   jax.ShapeDtypeStruct((8, D_MODEL), F32)],
        compiler_params=_params(1),
    )(mixed, w_out_g, x, target, g_post.reshape(1, D_MODEL))


def _head_sum(prod, same_head):
    hi = prod.astype(BF16)
    lo = (prod - hi.astype(F32)).astype(BF16)
    return (jnp.dot(hi, same_head, preferred_element_type=F32) + jnp.dot(lo, same_head, preferred_element_type=F32))


def _gate_bwd(dmixed, zr, o, conv_w, tm=256):
    seq = zr.shape[0]
    n_tiles = seq // tm
    n_dil = len(DILATIONS)
    a0, h0, b0, c0, g0 = 0, ATTN_W, ATTN_W + CONV_W, ATTN_W + 2 * CONV_W, ATTN_W + 3 * CONV_W

    def body(dm_ref, dmn_ref, zr_ref, zp_ref, zn_ref, o_ref, w_ref, *rest):
        do_refs, dl_refs = rest[:n_dil], rest[n_dil:2 * n_dil]
        dz_ref, dw_ref, do_scr, dl_scr, tmp = rest[2 * n_dil:]
        i = pl.program_id(0)
        part = lambda ref, lo, hi: ref[:, lo:hi].astype(F32)
        ga = part(zr_ref, a0, h0)
        sg = _sigmoid(ga)
        dattn = part(dm_ref, 0, ATTN_W)
        ov = o_ref[...].astype(F32)
        do = dattn * (ga * sg)
        dz_ref[:, a0:h0] = (dattn * ov * (sg * (1.0 + ga * (1.0 - sg)))).astype(BF16)
        li = lax.broadcasted_iota(jnp.int32, (BLK, BLK), 0) // HEAD_DIM
        lj = lax.broadcasted_iota(jnp.int32, (BLK, BLK), 1) // HEAD_DIM
        same_head = (li == lj).astype(BF16)
        prod = do * ov
        for c in range(N_CHUNK):
            cols = slice(c * BLK, (c + 1) * BLK)
            do_scr[c] = do[:, cols]
            dl_scr[c] = _head_sum(prod[:, cols], same_head)
        _to_residues(do_scr, 0, do_refs, tmp, tm, BF16)
        _to_residues(dl_scr, 0, dl_refs, tmp, tm, BF16)

        ch, cb, cc, gc = (part(zr_ref, lo, hi) for lo, hi in ((h0, b0), (b0, c0), (c0, g0), (g0, REST_W)))
        u = cc * ch
        before = jnp.where(i > 0, part(zp_ref, c0, g0) * part(zp_ref, h0, b0), 0.0)
        u1, u2 = _conv_taps(u, before, tm)
        w0, w1, w2 = w_ref[0:1, :], w_ref[1:2, :], w_ref[2:3, :]
        y = u2 * w0 + u1 * w1 + u * w2
        sc = _sigmoid(gc)
        silu_c = gc * sc
        dconv = part(dm_ref, ATTN_W, D_MODEL)
        dz_ref[:, b0:c0] = (dconv * y * silu_c).astype(BF16)
        dz_ref[:, g0:] = (dconv * (cb * y) * (sc * (1.0 + gc * (1.0 - sc)))).astype(BF16)
        dy = dconv * cb * silu_c
        gn = part(zn_ref, g0, REST_W)
        after = jnp.where(i < n_tiles - 1,
                          part(dmn_ref, ATTN_W, D_MODEL) * part(zn_ref, b0, c0) * (gn * _sigmoid(gn)), 0.0)
        row = lax.broadcasted_iota(jnp.int32, dy.shape, 0)
        nxt, nxt2 = after[0:1, :], after[1:2, :]
        dy1 = jnp.where(row == tm - 1, nxt, pltpu.roll(dy, tm - 1, 0))
        dy2 = jnp.where(row == tm - 1, nxt2, jnp.where(row == tm - 2, nxt, pltpu.roll(dy, tm - 2, 0)))
        du = dy * w2 + dy1 * w1 + dy2 * w0
        dz_ref[:, c0:g0] = (du * ch).astype(BF16)
        dz_ref[:, h0:b0] = (du * cc).astype(BF16)
        dws = [jnp.sum(dy * u2, axis=0, keepdims=True), jnp.sum(dy * u1, axis=0, keepdims=True),
               jnp.sum(dy * u, axis=0, keepdims=True)]

        @pl.when(i == 0)
        def _():
            dw_ref[...] = jnp.zeros_like(dw_ref)

        for n, part in enumerate(dws):
            dw_ref[n:n + 1, :] += part

    row_spec = lambda w: pl.BlockSpec((tm, w), lambda i: (i, 0))
    before, after = _halo_specs(tm, seq)
    views = [_residue_spec(tm, dil) for dil in DILATIONS]
    outs = pl.pallas_call(
        body, name="gate_bwd", grid=(n_tiles,),
        in_specs=[row_spec(D_MODEL), after(D_MODEL), row_spec(REST_W), before(REST_W), after(REST_W),
                  row_spec(ATTN_W), _resident((3, CONV_W))],
        out_specs=views * 2 + [row_spec(REST_W), pl.BlockSpec((8, CONV_W), lambda i: (0, 0))],
        out_shape=[_residue_shape(seq, dil, BF16) for dil in DILATIONS] * 2
        + [jax.ShapeDtypeStruct((seq, REST_W), BF16), jax.ShapeDtypeStruct((8, CONV_W), F32)],
        scratch_shapes=[pltpu.VMEM((N_CHUNK, tm, BLK), F32)] * 3,
        compiler_params=_params(1),
    )(dmixed, dmixed, zr, zr, zr, o, conv_w)
    return outs[:n_dil], outs[n_dil:2 * n_dil], outs[2 * n_dil], outs[2 * n_dil + 1]


def _in_bwd(dqs, dks, dvs, dzr, x, d_out, g_pre, w_in_g, tm=256):
    seq = x.shape[0]

    def body(q1, q2, q3, k1, k2, k3, v1, v2, v3, dzr_ref, ca_ref, sa_ref, cb_ref, sb_ref, x_ref, dout_ref, g_ref,
             w_ref, dz_ref, gx_ref, st_ref, *scratch):
        i = pl.program_id(0)
        cos, sin = _tile_rope(ca_ref, sa_ref, cb_ref, sb_ref)
        first_half = (lax.broadcasted_iota(jnp.int32, (tm, BLK), 1) & 32) == 0
        streams = [(q1, q2, q3), (k1, k2, k3), (v1, v2, v3)]
        from4, from16, tmp = scratch[0:3], scratch[3:6], scratch[6]
        per_slab = SHARD_IN // BLK

        def unrope(t):
            return t * cos - _swap_halves(t, first_half) * sin

        def to_positions(a):
            _from_residue(streams[a][1], from4[a], 4, tm, accumulate=False)
            _from_residue(streams[a][2], from16[a], 16, tm, accumulate=False, tmp=tmp)

        def assemble(j):
            for chunk in range(j * per_slab, (j + 1) * per_slab):
                a, c = divmod(chunk, N_CHUNK)
                if a < 3:
                    total = streams[a][0][:, _lanes(0, c)].astype(F32) + from4[a][c] + from16[a][c]
                    val = (unrope(total) if a < 2 else total).astype(BF16)
                else:
                    val = dzr_ref[:, (chunk - 3 * N_CHUNK) * BLK:(chunk - 3 * N_CHUNK + 1) * BLK]
                dz_ref[:, chunk * BLK:(chunk + 1) * BLK] = val
            return dz_ref[:, j * SHARD_IN:(j + 1) * SHARD_IN]

        order = [j for j in range(N_DEV) if j * per_slab >= 3 * N_CHUNK]
        order += [j for j in range(N_DEV) if j not in order]
        assert order[2] * per_slab >= 3 * N_CHUNK
        ahead = assemble(order[0])
        dh = None
        for n, j in enumerate(order):
            part = lax.dot_general(ahead, w_ref[j], (((1,), (1,)), ((), ())), preferred_element_type=F32)
            if n < 3:
                to_positions(n)
            if n + 1 < N_DEV:
                ahead = assemble(order[n + 1])
            dh = part if dh is None else dh + part
        xv = x_ref[...]
        r = lax.rsqrt(jnp.mean(xv * xv, axis=-1, keepdims=True) + NORM_EPS)
        xhat = xv * r
        tg = dh * g_ref[...]
        gx_ref[...] = dout_ref[...] + r * (tg - xhat * jnp.mean(tg * xhat, axis=-1, keepdims=True))
        gsum = jnp.sum(dh * xhat, axis=0, keepdims=True)

        @pl.when(i == 0)
        def _():
            st_ref[...] = jnp.zeros_like(st_ref)

        st_ref[0:1, :] += gsum

    row = lambda w: pl.BlockSpec((tm, w), lambda i: (i, 0))
    return pl.pallas_call(
        body, name="in_bwd", grid=(seq // tm,),
        in_specs=[_residue_spec(tm, dil) for dil in DILATIONS] * 3
        + [row(REST_W)] + _rope_specs(tm) + [row(D_MODEL), row(D_MODEL), _resident((1, D_MODEL)),
                                             _resident((N_DEV, D_MODEL, SHARD_IN))],
        out_specs=[row(IN_W), row(D_MODEL), pl.BlockSpec((8, D_MODEL), lambda i: (0, 0))],
        out_shape=[jax.ShapeDtypeStruct((seq, IN_W), BF16), jax.ShapeDtypeStruct((seq, D_MODEL), F32),
                   jax.ShapeDtypeStruct((8, D_MODEL), F32)],
        scratch_shapes=[pltpu.VMEM((N_CHUNK, tm, BLK), F32)] * 7,
        compiler_params=_params(1),
    )(*dqs, *dks, *dvs, dzr, *_rope_tables(seq, tm), x, d_out, g_pre.reshape(1, D_MODEL), w_in_g)


def _local_step(x, target, g_pre, g_post, w_in_g, w_out_g, conv_w):
    qkv, zr, ht = _fwd_in(x, g_pre, w_in_g)
    parts = [_attn_fwd(*qkv[n], dil) for n, dil in enumerate(DILATIONS)]
    mixed, o, lse = _attn_combine([p[0] for p in parts], [p[1] for p in parts], zr, conv_w)
    d_out, dmixed, dw_out, dw_out_bf, st_post = _out_loss_bwd(mixed, w_out_g, x, target, g_post)
    do, delta, dzr, dconv = _gate_bwd(dmixed, zr, o, conv_w)
    grads = [_attn_bwd(*qkv[n], do[n], lse[n], delta[n], dil) for n, dil in enumerate(DILATIONS)]
    dz, grad_x, st_pre = _in_bwd([g[0] for g in grads], [g[1] for g in grads], [g[2] for g in grads], dzr,
                                 x, d_out, g_pre, w_in_g)
    conv_rows = jnp.pad(dconv[0:3], ((0, 0), (0, D_MODEL - CONV_W)))
    small = jnp.concatenate([st_pre[0:1], st_post[0:2], conv_rows, jnp.zeros((2, D_MODEL), F32)], axis=0)
    return grad_x, ht, dz, dw_out, dw_out_bf, small


def _coords():
    return lax.axis_index("x"), lax.axis_index("y"), lax.axis_index("c")


def _peer(k):
    x, y, c = _coords()
    px = 1 - x if k & 4 else x
    py = 1 - y if k & 2 else y
    pc = 1 - c if k & 1 else c
    return (px, py, pc), 4 * px + 2 * py + pc


HBM_SPEC = pl.BlockSpec(memory_space=pltpu.HBM)
VMEM_SPEC = pl.BlockSpec(memory_space=pltpu.VMEM)


def _ag_weights(w_in, w_out, conv_w):
    def body(win_ref, wout_ref, cw_ref, gin_ref, gout_ref, gcw_ref, win_bf, wout_bf, cw_pad, send_sems, recv_sems,
             local_sems):
        x, y, c = _coords()
        me, sibling = (x, y, c), (x, y, 1 - c)
        flip = lambda v, yes: v + yes - 2 * v * yes
        x_nbr, y_nbr, diagonal = (1 - x, y, c), (x, 1 - y, c), (1 - x, 1 - y, c)
        relay_from = (flip(x, 1 - c), flip(y, c), c)
        relay_to = (flip(x, c), flip(y, 1 - c), c)
        slab = lambda px, py, pc: 4 * px + 2 * py + pc
        win_bf[...] = win_ref[...].astype(BF16)
        wout_bf[...] = wout_ref[...].astype(BF16)
        cw_pad[...] = jnp.zeros_like(cw_pad)
        cw_pad[0:3, 0:CONV_W // N_DEV] = cw_ref[...]
        mine = [win_bf, wout_bf, cw_pad]
        gathered = [gin_ref, gout_ref, gcw_ref]

        def copies(k, block, to, own=False):
            return [pltpu.make_async_remote_copy(src_ref=mine[a] if own else gathered[a].at[slab(*block)],
                                                 dst_ref=gathered[a].at[slab(*block)], send_sem=send_sems.at[k, a],
                                                 recv_sem=recv_sems.at[k, a], device_id=to, device_id_type=MESH)
                    for a in range(3)]

        local = [pltpu.make_async_copy(mine[a], gathered[a].at[slab(*me)], local_sems.at[a]) for a in range(3)]
        for cp in local:
            cp.start()
        started = copies(0, me, sibling, own=True) + copies(1, me, x_nbr, own=True) + copies(2, me, y_nbr, own=True)
        for cp in started:
            cp.start()
        for cp in copies(1, x_nbr, me) + copies(2, y_nbr, me):
            cp.wait_recv()
        onward = copies(3, relay_from, relay_to) + copies(4, x_nbr, sibling) + copies(5, y_nbr, sibling)
        for cp in onward:
            cp.start()
        for cp in copies(3, diagonal, me):
            cp.wait_recv()
        last = copies(6, diagonal, sibling)
        for cp in last:
            cp.start()
        for cp in copies(0, sibling, me):
            cp.wait_recv()
        for k, origin in ((4, (1 - x, y, 1 - c)), (5, (x, 1 - y, 1 - c)), (6, (1 - x, 1 - y, 1 - c))):
            for cp in copies(k, origin, me):
                cp.wait_recv()
        for cp in started + onward + last:
            cp.wait_send()
        for cp in local:
            cp.wait()

    return pl.pallas_call(
        body, name="ag_weights",
        in_specs=[VMEM_SPEC, VMEM_SPEC, VMEM_SPEC], out_specs=[HBM_SPEC, HBM_SPEC, HBM_SPEC],
        out_shape=[jax.ShapeDtypeStruct((N_DEV, D_MODEL, SHARD_IN), BF16),
                   jax.ShapeDtypeStruct((N_DEV, SHARD_OUT, D_MODEL), BF16),
                   jax.ShapeDtypeStruct((N_DEV, 8, BLK), F32)],
        scratch_shapes=[pltpu.VMEM((D_MODEL, SHARD_IN), BF16), pltpu.VMEM((SHARD_OUT, D_MODEL), BF16),
                        pltpu.VMEM((8, BLK), F32), pltpu.SemaphoreType.DMA((N_DEV - 1, 3)),
                        pltpu.SemaphoreType.DMA((N_DEV - 1, 3)), pltpu.SemaphoreType.DMA((3,))],
        compiler_params=pltpu.CompilerParams(vmem_limit_bytes=VMEM_LIMIT),
    )(w_in, w_out, conv_w)


def _dw_in_rs(ht, dz, dw_out, small):
    seq = dz.shape[0]

    def body(cols_ref, ht_ref, dz_ref, dout_ref, sm_ref, own_ref, rin_ref, rout_ref, rsm_ref, to_sibling, landed,
             to_chip, zero_buf, d2d_send, d2d_recv, ici_send, ici_recv, side_send, side_recv, local_sems):
        del cols_ref
        step = pl.program_id(0)
        x, y, c = _coords()
        me = 4 * x + 2 * y + c
        sibling = (x, y, 1 - c)
        chips = [(1 - x, y), (x, 1 - y), (1 - x, 1 - y)]

        def d2d(n):
            return pltpu.make_async_remote_copy(src_ref=to_sibling.at[n], dst_ref=landed.at[n], send_sem=d2d_send.at[n],
                                                recv_sem=d2d_recv.at[n], device_id=sibling, device_id_type=MESH)

        def ici(n):
            return pltpu.make_async_remote_copy(src_ref=to_chip.at[n], dst_ref=rin_ref.at[n], send_sem=ici_send.at[n],
                                                recv_sem=ici_recv.at[n], device_id=(*chips[n], c), device_id_type=MESH)

        def side(k, mine):
            peer, peer_idx = _peer(k)
            src_slab, dst_slab = (peer_idx, me) if mine else (me, peer_idx)
            pairs = [(dout_ref.at[src_slab], rout_ref.at[dst_slab]), (sm_ref, rsm_ref.at[dst_slab])]
            return [pltpu.make_async_remote_copy(src_ref=src, dst_ref=dst, send_sem=side_send.at[k - 1, a],
                                                 recv_sem=side_recv.at[k - 1, a], device_id=peer, device_id_type=MESH)
                    for a, (src, dst) in enumerate(pairs)]

        local = [pltpu.make_async_copy(zero_buf, rout_ref.at[me], local_sems.at[0]),
                 pltpu.make_async_copy(sm_ref, rsm_ref.at[me], local_sems.at[1])]

        @pl.when(step == 0)
        def _():
            zero_buf[...] = jnp.zeros_like(zero_buf)
            for cp in local:
                cp.start()
            for k in range(1, N_DEV):
                for cp in side(k, mine=True):
                    cp.start()

        dw = jnp.dot(ht_ref[...], dz_ref[...], preferred_element_type=F32)
        for n, at in zip(range(4), (0, 1, 2, N_DEV - 2)):
            @pl.when(step == at)
            def _(n=n):
                to_sibling[n] = dw.astype(BF16)
                d2d(n).start()

        for n in range(3):
            @pl.when(step == 3 + n)
            def _(n=n):
                d2d(n).wait_recv()
                to_chip[n] = (dw + landed[n].astype(F32)).astype(BF16)
                ici(n).start()

        @pl.when(step == N_DEV - 1)
        def _():
            d2d(3).wait_recv()
            own_ref[...] = dw + landed[3].astype(F32)
            for n in range(3):
                ici(n).wait_recv()
            for k in range(1, N_DEV):
                for cp in side(k, mine=False):
                    cp.wait_recv()
            for n in range(4):
                d2d(n).wait_send()
            for n in range(3):
                ici(n).wait_send()
            for k in range(1, N_DEV):
                for cp in side(k, mine=True):
                    cp.wait_send()
            for cp in local:
                cp.wait()

    x, y, c = _coords()
    others = [(1 - x, y), (x, 1 - y), (1 - x, 1 - y)]
    order = [(*chip, 1 - c) for chip in others] + [(*chip, c) for chip in others] + [(x, y, 1 - c), (x, y, c)]
    cols = jnp.stack([4 * px + 2 * py + pc for px, py, pc in order]).astype(jnp.int32)
    slab = (D_MODEL, SHARD_IN)
    grid_spec = pltpu.PrefetchScalarGridSpec(
        num_scalar_prefetch=1, grid=(N_DEV,),
        in_specs=[pl.BlockSpec((D_MODEL, seq), lambda s, cols: (0, 0), pipeline_mode=pl.Buffered(1)),
                  pl.BlockSpec((seq, SHARD_IN), lambda s, cols: (0, cols[s])), HBM_SPEC, HBM_SPEC],
        out_specs=[pl.BlockSpec(slab, lambda s, cols: (0, 0)), HBM_SPEC, HBM_SPEC, HBM_SPEC],
        scratch_shapes=[pltpu.VMEM((4, *slab), BF16), pltpu.VMEM((4, *slab), BF16), pltpu.VMEM((3, *slab), BF16),
                        pltpu.VMEM((SHARD_OUT, D_MODEL), BF16),
                        pltpu.SemaphoreType.DMA((4,)), pltpu.SemaphoreType.DMA((4,)),
                        pltpu.SemaphoreType.DMA((3,)), pltpu.SemaphoreType.DMA((3,)),
                        pltpu.SemaphoreType.DMA((N_DEV - 1, 2)), pltpu.SemaphoreType.DMA((N_DEV - 1, 2)),
                        pltpu.SemaphoreType.DMA((2,))])
    return pl.pallas_call(
        body, name="dw_in_rs", grid_spec=grid_spec,
        out_shape=[jax.ShapeDtypeStruct(slab, F32),
                   jax.ShapeDtypeStruct((3, *slab), BF16),
                   jax.ShapeDtypeStruct((N_DEV, SHARD_OUT, D_MODEL), BF16),
                   jax.ShapeDtypeStruct((N_DEV, 8, D_MODEL), F32)],
        compiler_params=_params(1),
    )(cols, ht, dz, dw_out, small)


def _adamw_math(w, g, m, v):
    m = ADAM_B1 * m + (1.0 - ADAM_B1) * g
    v = ADAM_B2 * v + (1.0 - ADAM_B2) * (g * g)
    m_hat = m / (1.0 - ADAM_B1 ** ADAM_STEP)
    v_hat = v / (1.0 - ADAM_B2 ** ADAM_STEP)
    delta = -ADAM_LR * (m_hat / (jnp.sqrt(v_hat) + ADAM_EPS) + ADAM_WD * w)
    return delta, m, v


def _sum_slabs(ref, first=None):
    total = ref[0].astype(F32) if first is None else first + ref[0].astype(F32)
    for s in range(1, ref.shape[0]):
        total = total + ref[s].astype(F32)
    return total


def _adamw_slabs(parts, own, w, m, v, name, tr):
    rows, cols = w.shape
    tile = pl.BlockSpec((tr, cols), lambda i: (i, 0))

    def body(p_ref, *refs):
        own_ref = refs[0] if own is not None else None
        w_ref, m_ref, v_ref, g_ref, d_ref, nm_ref, nv_ref = refs[-7:]
        g = _sum_slabs(p_ref, None if own_ref is None else own_ref[...])
        g_ref[...] = g
        d_ref[...], nm_ref[...], nv_ref[...] = _adamw_math(w_ref[...], g, m_ref[...], v_ref[...])

    extra = [] if own is None else [own]
    return pl.pallas_call(
        body, name=name, grid=(rows // tr,),
        in_specs=[pl.BlockSpec((parts.shape[0], tr, cols), lambda i: (0, i, 0))] + [tile] * (len(extra) + 3),
        out_specs=[tile] * 4,
        out_shape=[jax.ShapeDtypeStruct((rows, cols), F32)] * 4,
        compiler_params=_params(1),
    )(parts, *extra, w, m, v)


def _sum_small(parts):
    def body(p_ref, out_ref):
        out_ref[...] = _sum_slabs(p_ref)

    return pl.pallas_call(body, name="sum_small", out_shape=jax.ShapeDtypeStruct(parts.shape[1:], F32))(parts)


def _adamw_whole(g, w, m, v, name):
    def body(g_ref, w_ref, m_ref, v_ref, d_ref, nm_ref, nv_ref):
        d_ref[...], nm_ref[...], nv_ref[...] = _adamw_math(w_ref[...], g_ref[...], m_ref[...], v_ref[...])

    return pl.pallas_call(body, name=name, out_shape=[jax.ShapeDtypeStruct(w.shape, F32)] * 3)(g, w, m, v)


def kernel(x, norm_pre_g, w_in, conv_w, w_out, norm_post_g, loss_target, m_norm_pre_g, m_w_in, m_conv_w, m_w_out,
           m_norm_post_g, v_norm_pre_g, v_w_in, v_conv_w, v_w_out, v_norm_post_g):
    n_conv = CONV_W // N_DEV
    w_in_g, w_out_g, conv_g = _ag_weights(w_in, w_out, conv_w)
    conv_full = conv_g[:, 0:3, 0:n_conv].transpose(1, 0, 2).reshape(3, CONV_W)
    grad_x, ht, dz, dw_out, dw_out_bf, small = _local_step(x[0], loss_target[0], norm_pre_g, norm_post_g, w_in_g,
                                                           w_out_g.reshape(D_MODEL, D_MODEL), conv_full)
    own_in, r_in, r_out, r_small = _dw_in_rs(ht, dz, dw_out_bf.reshape(N_DEV, SHARD_OUT, D_MODEL), small)
    me = 4 * lax.axis_index("x") + 2 * lax.axis_index("y") + lax.axis_index("c")
    own_out = lax.dynamic_index_in_dim(dw_out.reshape(N_DEV, SHARD_OUT, D_MODEL), me, keepdims=False)
    g_in, d_in, nm_in, nv_in = _adamw_slabs(r_in, own_in, w_in, m_w_in, v_w_in, "adamw_in", 256)
    g_out, d_out, nm_out, nv_out = _adamw_slabs(r_out, own_out, w_out, m_w_out, v_w_out, "adamw_out", SHARD_OUT)
    sums = _sum_small(r_small)
    g_pre, g_post, loss = sums[0], sums[1], sums[2, 0]
    g_conv = lax.dynamic_slice(sums[3:6, 0:CONV_W], (0, me * n_conv), (3, n_conv))
    vec = lambda a: a.reshape(1, D_MODEL)
    d_pre, nm_pre, nv_pre = _adamw_whole(vec(g_pre), vec(norm_pre_g), vec(m_norm_pre_g), vec(v_norm_pre_g), "adamw_pre")
    d_post, nm_post, nv_post = _adamw_whole(vec(g_post), vec(norm_post_g), vec(m_norm_post_g), vec(v_norm_post_g),
                                            "adamw_post")
    d_conv, nm_conv, nv_conv = _adamw_whole(g_conv, conv_w, m_conv_w, v_conv_w, "adamw_conv")
    flat = lambda a: a.reshape(D_MODEL)
    return (loss, grad_x[None], g_pre, g_in, g_conv, g_out, g_post,
            flat(d_pre), d_in, d_conv, d_out, flat(d_post),
            flat(nm_pre), nm_in, nm_conv, nm_out, flat(nm_post),
            flat(nv_pre), nv_in, nv_conv, nv_out, flat(nv_post))
```

```python
import functools

import jax
import jax.numpy as jnp
from jax import lax
from jax.experimental import pallas as pl
from jax.experimental.pallas import tpu as pltpu

F32 = jnp.float32
BF16 = jnp.bfloat16

D_MODEL = 1024
HEAD_DIM = 64
ATTN_W = 768
CONV_W = 256
IN_W = 4096
REST_W = IN_W - 3 * ATTN_W
BLK = 128
N_DEV = 8
SHARD_IN = IN_W // N_DEV
SHARD_OUT = D_MODEL // N_DEV
DILATIONS = (1, 4, 16)
ROPE_THETA = 10000.0
NORM_EPS = 1e-6
NEG = -1e30

ADAM_LR = 0.001
ADAM_B1 = 0.9
ADAM_B2 = 0.999
ADAM_EPS = 1e-08
ADAM_WD = 0.01
ADAM_STEP = 10

VMEM_LIMIT = 56 * 1024 * 1024
MESH = pl.DeviceIdType.MESH


def _params(n_grid):
    return pltpu.CompilerParams(dimension_semantics=("arbitrary",) * n_grid, vmem_limit_bytes=VMEM_LIMIT)


def _resident(shape):
    zeros = (0,) * len(shape)
    return pl.BlockSpec(shape, lambda *_: zeros, pipeline_mode=pl.Buffered(1))


def _sigmoid(a):
    return 1.0 / (1.0 + jnp.exp(-a))


def _swap_halves(t, first_half):
    return jnp.where(first_half, pltpu.roll(t, BLK - 32, 1), pltpu.roll(t, 32, 1))


def _rope_tables(seq, tm):
    half = HEAD_DIM // 2
    inv_freq = ROPE_THETA ** (-jnp.arange(half, dtype=F32) * 2.0 / HEAD_DIM)
    freq = jnp.concatenate([inv_freq] * 4)
    sign = jnp.concatenate([-jnp.ones(half, F32), jnp.ones(half, F32)] * 2)
    starts = (jnp.arange(seq // tm) * tm).astype(F32)[:, None] * freq[None, :]
    rows = jnp.arange(tm).astype(F32)[:, None] * freq[None, :]
    slab = lambda a: jnp.broadcast_to(a[:, None, :], (seq // tm, 8, BLK))
    return slab(jnp.cos(starts)), slab(jnp.sin(starts) * sign), jnp.cos(rows), jnp.sin(rows) * sign


def _rope_specs(tm):
    return [pl.BlockSpec((1, 8, BLK), lambda i: (i, 0, 0))] * 2 + [_resident((tm, BLK))] * 2


def _tile_rope(cos_start, sin_start, cos_row, sin_row):
    ca, sa, cb, sb = cos_start[0, 0:1, :], sin_start[0, 0:1, :], cos_row[...], sin_row[...]
    return ca * cb - sa * sb, sa * cb + ca * sb


N_CHUNK = ATTN_W // BLK


def _lanes(r, c):
    return slice(r * ATTN_W + c * BLK, r * ATTN_W + (c + 1) * BLK)


def _to_residues(src, chunk0, dst_refs, tmp, rows, dtype):
    assert DILATIONS == (1, 4, 16)
    dst1, dst4, dst16 = dst_refs
    n4, n16 = rows // 4, rows // 16
    for c in range(N_CHUNK):
        dst1[:, _lanes(0, c)] = src[chunk0 + c].astype(dtype)
        for r1 in range(4):
            tmp[c, r1 * n4:(r1 + 1) * n4, :] = src[chunk0 + c, pl.ds(r1, n4, stride=4), :]
        for r1 in range(4):
            dst4[:, _lanes(r1, c)] = tmp[c, r1 * n4:(r1 + 1) * n4, :].astype(dtype)
            for r2 in range(4):
                dst16[:, _lanes(4 * r2 + r1, c)] = tmp[c, pl.ds(r1 * n4 + r2, n16, stride=4), :].astype(dtype)


def _from_residue(src_ref, dst, dil, rows, accumulate, tmp=None):
    n4, n16 = rows // 4, rows // 16

    def put(where, piece):
        if accumulate:
            dst[where] += piece
        else:
            dst[where] = piece

    for c in range(N_CHUNK):
        if dil == 1:
            put((c,), src_ref[:, _lanes(0, c)].astype(F32))
            continue
        for r1 in range(4):
            if dil == 4:
                piece = src_ref[:, _lanes(r1, c)].astype(F32)
            else:
                for r2 in range(4):
                    tmp[c, pl.ds(r1 * n4 + r2, n16, stride=4), :] = src_ref[:, _lanes(4 * r2 + r1, c)].astype(F32)
                piece = tmp[c, r1 * n4:(r1 + 1) * n4, :]
            put((c, pl.ds(r1, n4, stride=4), slice(None)), piece)


def _residue_spec(tm, dil):
    return pl.BlockSpec((tm // dil, dil * ATTN_W), lambda i: (i, 0))


def _residue_shape(seq, dil, dtype):
    return jax.ShapeDtypeStruct((seq // dil, dil * ATTN_W), dtype)


def _fwd_in(x, g_pre, w_in_g, tm=512):
    seq = x.shape[0]
    n_dil = len(DILATIONS)

    def body(x_ref, g_ref, w_ref, ca_ref, sa_ref, cb_ref, sb_ref, *rest):
        qkv_refs, (zr_ref, ht_ref, qkv_scr, tmp) = rest[:3 * n_dil], rest[3 * n_dil:]
        xv = x_ref[...]
        r = lax.rsqrt(jnp.mean(xv * xv, axis=-1, keepdims=True) + NORM_EPS)
        hf = (xv * r) * g_ref[...]
        h = hf.astype(BF16)
        ht_ref[...] = h.T
        cos, sin = _tile_rope(ca_ref, sa_ref, cb_ref, sb_ref)
        first_half = (lax.broadcasted_iota(jnp.int32, (tm, BLK), 1) & 32) == 0

        def rope(t):
            return t * cos + _swap_halves(t, first_half) * sin

        def project(j):
            return jnp.dot(h, w_ref[j], preferred_element_type=F32)

        def place(j, zj):
            for n in range(SHARD_IN // BLK):
                chunk, t = j * (SHARD_IN // BLK) + n, zj[:, n * BLK:(n + 1) * BLK]
                if chunk < N_CHUNK:
                    qkv_scr[chunk] = rope(t) * HEAD_DIM ** -0.5
                elif chunk < 2 * N_CHUNK:
                    qkv_scr[chunk] = rope(t)
                elif chunk < 3 * N_CHUNK:
                    qkv_scr[chunk] = t
                else:
                    zr_ref[:, (chunk - 3 * N_CHUNK) * BLK:(chunk - 3 * N_CHUNK + 1) * BLK] = t.astype(BF16)

        ahead = project(0)
        for j in range(N_DEV):
            zj = ahead
            if j + 1 < N_DEV:
                ahead = project(j + 1)
            place(j, zj)
            for a in range(3):
                if (a + 1) * N_CHUNK - 1 in range(j * (SHARD_IN // BLK), (j + 1) * (SHARD_IN // BLK)):
                    _to_residues(qkv_scr, a * N_CHUNK, [qkv_refs[3 * n + a] for n in range(n_dil)], tmp, tm, BF16)

    row = lambda w: pl.BlockSpec((tm, w), lambda i: (i, 0))
    outs = pl.pallas_call(
        body, name="fwd_in", grid=(seq // tm,),
        in_specs=[row(D_MODEL), _resident((1, D_MODEL)), _resident((N_DEV, D_MODEL, SHARD_IN))] + _rope_specs(tm),
        out_specs=[_residue_spec(tm, dil) for dil in DILATIONS for _ in range(3)]
        + [row(REST_W), pl.BlockSpec((D_MODEL, tm), lambda i: (0, i))],
        out_shape=[_residue_shape(seq, dil, BF16) for dil in DILATIONS for _ in range(3)]
        + [jax.ShapeDtypeStruct((seq, REST_W), BF16), jax.ShapeDtypeStruct((D_MODEL, seq), BF16)],
        scratch_shapes=[pltpu.VMEM((3 * N_CHUNK, tm, BLK), F32), pltpu.VMEM((N_CHUNK, tm, BLK), F32)],
        compiler_params=_params(1),
    )(x, g_pre.reshape(1, D_MODEL), w_in_g, *_rope_tables(seq, tm))
    qkv = [tuple(outs[3 * n:3 * n + 3]) for n in range(n_dil)]
    return qkv, outs[3 * n_dil], outs[3 * n_dil + 1]


def _band_bias(first_block):
    kj = lax.broadcasted_iota(jnp.int32, (2 * BLK, BLK), 0)
    qi = lax.broadcasted_iota(jnp.int32, (2 * BLK, BLK), 1)
    valid = (kj >= qi) & (kj <= qi + BLK)
    bias = jnp.where(valid, 0.0, NEG).astype(BF16)
    bias_first = jnp.where(valid & (kj >= BLK), 0.0, NEG).astype(BF16)
    onehot = ((kj & (BLK - 1)) == qi).astype(F32).astype(BF16)
    return onehot, bias, jnp.where(first_block, bias_first, bias)


def _stack_heads(t):
    keep0 = (lax.broadcasted_iota(jnp.int32, t.shape, 1) < HEAD_DIM).astype(F32).astype(BF16)
    return jnp.concatenate([t * keep0, t * (1 - keep0)], axis=0)


def _unstack_heads(t2, head0):
    return jnp.where(head0, t2[:BLK], t2[BLK:])


def _rows_per_head(a, head0):
    b = pltpu.roll(a, HEAD_DIM, 1)
    rows = jnp.concatenate([jnp.where(head0, a, b), jnp.where(head0, b, a)], axis=0)
    return jnp.concatenate([rows, rows], axis=1)


BLOCKS_PER_STEP = 32


def _attn_specs(length, dil, max_cols=8, units=BLOCKS_PER_STEP):
    n_blocks = length // BLK
    tb = min(units, n_blocks)
    nc = max(n for n in range(1, min(units // tb, max_cols) + 1) if (dil * N_CHUNK) % n == 0)
    assert n_blocks % tb == 0
    tile = pl.BlockSpec((tb * BLK, nc * BLK), lambda c, t: (t, c))
    prev = pl.BlockSpec((BLK, nc * BLK), lambda c, t: (jnp.maximum(t * tb - 1, 0), c))
    grid = (dil * N_CHUNK // nc, n_blocks // tb)
    return tb, nc, tile, prev, grid


def _window(prev_ref, cur_ref, j, cols):
    if j == 0:
        return jnp.concatenate([prev_ref[:, cols], cur_ref[0:BLK, cols]], axis=0)
    return cur_ref[(j - 1) * BLK:(j + 1) * BLK, cols]


def _attn_fwd(q, k, v, dil):
    length = q.shape[0]
    tb, nc, tile, prev, grid = _attn_specs(length, dil)

    def body(q_ref, kc_ref, kp_ref, vc_ref, vp_ref, o_ref, lse_ref):
        head0 = lax.broadcasted_iota(jnp.int32, (BLK, BLK), 1) < HEAD_DIM
        onehot, bias, bias_start = _band_bias(pl.program_id(1) == 0)
        ones = jnp.ones((2 * BLK, BLK), BF16)
        def scores(c, j):
            rows, cols = slice(j * BLK, (j + 1) * BLK), slice(c * BLK, (c + 1) * BLK)
            q2 = jnp.concatenate([_stack_heads(q_ref[rows, cols]), onehot], axis=1)
            kk = jnp.concatenate([_window(kp_ref, kc_ref, j, cols), bias_start if j == 0 else bias], axis=1)
            return (lax.dot_general(q2, kk, (((1,), (1,)), ((), ())), preferred_element_type=F32),)

        def probabilities(c, j, s):
            m = jnp.max(s, axis=1, keepdims=True)
            return m, jnp.exp(s - m).astype(BF16)

        def outputs(c, j, m, p):
            rows, cols = slice(j * BLK, (j + 1) * BLK), slice(c * BLK, (c + 1) * BLK)
            vv = jnp.concatenate([_window(vp_ref, vc_ref, j, cols), ones], axis=1)
            pv = jnp.dot(p, vv, preferred_element_type=F32)
            den = pv[:, BLK:]
            o_ref[rows, cols] = _unstack_heads(pv[:, :BLK] / den, head0).astype(BF16)
            lse_ref[rows, cols] = _unstack_heads(m + jnp.log(den), head0)

        units = [(c, j) for c in range(nc) for j in range(tb)]
        stage1, stage2 = {}, {}
        for n in range(len(units) + 2):
            if n < len(units):
                stage1[n] = scores(*units[n])
            if 0 <= n - 1 < len(units):
                stage2[n - 1] = probabilities(*units[n - 1], *stage1.pop(n - 1))
            if 0 <= n - 2 < len(units):
                outputs(*units[n - 2], *stage2.pop(n - 2))

    return pl.pallas_call(
        body, name=f"attn_fwd_d{dil}", grid=grid,
        in_specs=[tile, tile, prev, tile, prev], out_specs=[tile, tile],
        out_shape=[jax.ShapeDtypeStruct(q.shape, BF16), jax.ShapeDtypeStruct(q.shape, F32)],
        compiler_params=_params(2),
    )(q, k, k, v, v)


def _attn_bwd(q, k, v, do, lse, delta, dil):
    length = q.shape[0]
    tb, nc, tile, prev, grid = _attn_specs(length, dil, max_cols=4)
    whole = pl.BlockSpec((length, nc * BLK), lambda c, t: (0, c))

    def body(q_ref, do_ref, lse_ref, dl_ref, kc_ref, kp_ref, vc_ref, vp_ref, dq_ref, dk_ref, dv_ref):
        t = pl.program_id(1)
        head0 = lax.broadcasted_iota(jnp.int32, (BLK, BLK), 1) < HEAD_DIM
        onehot, bias, bias_start = _band_bias(t == 0)

        def scores(c, j):
            rows, cols = slice(j * BLK, (j + 1) * BLK), slice(c * BLK, (c + 1) * BLK)
            q2 = _stack_heads(q_ref[rows, cols])
            do2 = _stack_heads(do_ref[rows, cols])
            kk = _window(kp_ref, kc_ref, j, cols)
            s = lax.dot_general(jnp.concatenate([q2, onehot], axis=1),
                                jnp.concatenate([kk, bias_start if j == 0 else bias], axis=1),
                                (((1,), (1,)), ((), ())), preferred_element_type=F32)
            dp = lax.dot_general(do2, _window(vp_ref, vc_ref, j, cols), (((1,), (1,)), ((), ())),
                                 preferred_element_type=F32)
            return q2, do2, kk, s, dp

        def probabilities(c, j, q2, do2, kk, s, dp):
            rows, cols = slice(j * BLK, (j + 1) * BLK), slice(c * BLK, (c + 1) * BLK)
            p = jnp.exp(s - _rows_per_head(lse_ref[rows, cols], head0))
            ds = (p * (dp - _rows_per_head(dl_ref[rows, cols].astype(F32), head0))).astype(BF16)
            return q2, do2, kk, p.astype(BF16), ds

        def gradients(c, j, q2, do2, kk, p, ds):
            rows, cols = slice(j * BLK, (j + 1) * BLK), slice(c * BLK, (c + 1) * BLK)
            dq2 = jnp.dot(ds, kk, preferred_element_type=F32)
            dq_ref[rows, cols] = (_unstack_heads(dq2, head0) * HEAD_DIM ** -0.5).astype(BF16)
            dk2 = lax.dot_general(ds, q2, (((0,), (0,)), ((), ())), preferred_element_type=F32)
            dv2 = lax.dot_general(p, do2, (((0,), (0,)), ((), ())), preferred_element_type=F32)
            own = pl.ds(pl.multiple_of((t * tb + j) * BLK, BLK), BLK)
            dk_ref[own, cols] = dk2[BLK:].astype(BF16)
            dv_ref[own, cols] = dv2[BLK:].astype(BF16)

            def add_to_previous():
                before = pl.ds(pl.multiple_of((t * tb + j - 1) * BLK, BLK), BLK)
                dk_ref[before, cols] = (dk_ref[before, cols].astype(F32) + dk2[:BLK]).astype(BF16)
                dv_ref[before, cols] = (dv_ref[before, cols].astype(F32) + dv2[:BLK]).astype(BF16)

            if j > 0:
                add_to_previous()
            elif grid[1] > 1:
                pl.when(t > 0)(add_to_previous)

        units = [(c, j) for c in range(nc) for j in range(tb)]
        stage1 = {0: scores(*units[0])}
        for n in range(len(units)):
            stage2 = probabilities(*units[n], *stage1.pop(n))
            if n + 1 < len(units):
                stage1[n + 1] = scores(*units[n + 1])
            gradients(*units[n], *stage2)

    return pl.pallas_call(
        body, name=f"attn_bwd_d{dil}", grid=grid,
        in_specs=[tile, tile, tile, tile, tile, prev, tile, prev], out_specs=[tile, whole, whole],
        out_shape=[jax.ShapeDtypeStruct(q.shape, BF16)] * 3,
        compiler_params=_params(2),
    )(q, do, lse, delta, k, k, v, v)


HALO = 16


def _halo_specs(tm, seq):
    before = lambda w: pl.BlockSpec((HALO, w), lambda i: (jnp.maximum(i * (tm // HALO) - 1, 0), 0))
    after = lambda w: pl.BlockSpec((HALO, w), lambda i: (jnp.minimum((i + 1) * (tm // HALO), seq // HALO - 1), 0))
    return before, after


def _conv_taps(u, before, tm):
    row = lax.broadcasted_iota(jnp.int32, u.shape, 0)
    last, last2 = before[HALO - 1:HALO, :], before[HALO - 2:HALO - 1, :]
    u1 = jnp.where(row == 0, last, pltpu.roll(u, 1, 0))
    u2 = jnp.where(row == 0, last2, jnp.where(row == 1, last, pltpu.roll(u, 2, 0)))
    return u1, u2


def _attn_combine(o_parts, lse_parts, zr, conv_w, tm=256):
    seq = zr.shape[0]
    a0, h0, b0, c0, g0 = 0, ATTN_W, ATTN_W + CONV_W, ATTN_W + 2 * CONV_W, ATTN_W + 3 * CONV_W

    def body(o1, o2, o3, l1, l2, l3, zr_ref, zp_ref, w_ref, mixed_ref, o_ref, lse1, lse2, lse3, *scr):
        i = pl.program_id(0)
        for src, dst, dil in zip((o2, o3, l2, l3), scr[:4], DILATIONS[1:] * 2):
            _from_residue(src, dst, dil, tm, accumulate=False, tmp=scr[5])
        for c in range(N_CHUNK):
            cols = slice(c * BLK, (c + 1) * BLK)
            la, lb, lc = l1[:, cols], scr[2][c], scr[3][c]
            top = jnp.maximum(jnp.maximum(la, lb), lc)
            ea, eb, ec = jnp.exp(la - top), jnp.exp(lb - top), jnp.exp(lc - top)
            den = ea + eb + ec
            inv = 1.0 / den
            o = (ea * inv) * o1[:, cols].astype(F32) + (eb * inv) * scr[0][c] + (ec * inv) * scr[1][c]
            o_ref[:, cols] = o.astype(BF16)
            scr[4][c] = top + jnp.log(den)
            ga = zr_ref[:, cols].astype(F32)
            mixed_ref[:, cols] = (o * (ga * _sigmoid(ga))).astype(BF16)
        _to_residues(scr[4], 0, (lse1, lse2, lse3), scr[5], tm, F32)
        part = lambda ref, lo, hi: ref[:, lo:hi].astype(F32)
        u = part(zr_ref, c0, g0) * part(zr_ref, h0, b0)
        before = jnp.where(i > 0, part(zp_ref, c0, g0) * part(zp_ref, h0, b0), 0.0)
        u1, u2 = _conv_taps(u, before, tm)
        y = u2 * w_ref[0:1, :] + u1 * w_ref[1:2, :] + u * w_ref[2:3, :]
        gc = part(zr_ref, g0, REST_W)
        mixed_ref[:, ATTN_W:] = ((part(zr_ref, b0, c0) * y) * (gc * _sigmoid(gc))).astype(BF16)

    row = lambda w: pl.BlockSpec((tm, w), lambda i: (i, 0))
    before, _ = _halo_specs(tm, seq)
    views = [_residue_spec(tm, dil) for dil in DILATIONS]
    outs = pl.pallas_call(
        body, name="attn_combine", grid=(seq // tm,),
        in_specs=views * 2 + [row(REST_W), before(REST_W), _resident((3, CONV_W))],
        out_specs=[row(D_MODEL), row(ATTN_W)] + views,
        out_shape=[jax.ShapeDtypeStruct((seq, D_MODEL), BF16), jax.ShapeDtypeStruct((seq, ATTN_W), BF16)]
        + [_residue_shape(seq, dil, F32) for dil in DILATIONS],
        scratch_shapes=[pltpu.VMEM((N_CHUNK, tm, BLK), F32)] * 6,
        compiler_params=_params(1),
    )(*o_parts, *lse_parts, zr, zr, conv_w)
    return outs[0], outs[1], outs[2:]


def _out_loss_bwd(mixed, w_out_g, x, target, g_post, tm=512, n_parts=2):
    seq = x.shape[0]

    def body(mx_ref, w_ref, x_ref, t_ref, g_ref, dout_ref, dmx_ref, dw_ref, dwb_ref, st_ref):
        i = pl.program_id(0)
        g = g_ref[...]
        parts = [slice(n * (tm // n_parts), (n + 1) * (tm // n_parts)) for n in range(n_parts)]

        def project(rows):
            return jnp.dot(mx_ref[rows, :], w_ref[...], preferred_element_type=F32)

        def head(rows, y):
            r = lax.rsqrt(jnp.mean(y * y, axis=-1, keepdims=True) + NORM_EPS)
            yhat = y * r
            err = (x_ref[rows, :] + yhat * g) - t_ref[rows, :]
            dn = err * (1.0 / D_MODEL)
            dout_ref[rows, :] = dn
            tg = dn * g
            dy = (r * (tg - yhat * jnp.mean(tg * yhat, axis=-1, keepdims=True))).astype(BF16)
            dmx_ref[rows, :] = lax.dot_general(dy, w_ref[...], (((1,), (1,)), ((), ())),
                                               preferred_element_type=F32).astype(BF16)
            return dy, jnp.sum(dn * yhat, axis=0, keepdims=True), jnp.sum(err * err)

        ahead, done = project(parts[0]), []
        for n, rows in enumerate(parts):
            y = ahead
            if n + 1 < n_parts:
                ahead = project(parts[n + 1])
            done.append(head(rows, y))
        dy = jnp.concatenate([d[0] for d in done], axis=0)
        dw = lax.dot_general(mx_ref[...], dy, (((0,), (0,)), ((), ())), preferred_element_type=F32)
        gsum = functools.reduce(lambda a, b: a + b, [d[1] for d in done])
        lsum = jnp.broadcast_to(0.5 / D_MODEL * functools.reduce(lambda a, b: a + b, [d[2] for d in done]),
                                (1, D_MODEL))

        @pl.when(i == 0)
        def _():
            dw_ref[...] = dw
            st_ref[...] = jnp.zeros_like(st_ref)
            st_ref[0:1, :] = gsum
            st_ref[1:2, :] = lsum

        @pl.when(i > 0)
        def _():
            dw_ref[...] += dw
            st_ref[0:1, :] += gsum
            st_ref[1:2, :] += lsum

        @pl.when(i == seq // tm - 1)
        def _():
            dwb_ref[...] = dw_ref[...].astype(BF16)

    row = lambda w: pl.BlockSpec((tm, w), lambda i: (i, 0))
    whole = pl.BlockSpec((D_MODEL, D_MODEL), lambda i: (0, 0))
    return pl.pallas_call(
        body, name="out_loss_bwd", grid=(seq // tm,),
        in_specs=[row(D_MODEL), _resident((D_MODEL, D_MODEL)), row(D_MODEL), row(D_MODEL), _resident((1, D_MODEL))],
        out_specs=[row(D_MODEL), row(D_MODEL), whole, whole, pl.BlockSpec((8, D_MODEL), lambda i: (0, 0))],
        out_shape=[jax.ShapeDtypeStruct((seq, D_MODEL), F32), jax.ShapeDtypeStruct((seq, D_MODEL), BF16),
                   jax.ShapeDtypeStruct((D_MODEL, D_MODEL), F32), jax.ShapeDtypeStruct((D_MODEL, D_MODEL), BF16),
                   jax.ShapeDtypeStruct((8, D_MODEL), F32)],
        compiler_params=_params(1),
    )(mixed, w_out_g, x, target, g_post.reshape(1, D_MODEL))


def _head_sum(prod, same_head):
    hi = prod.astype(BF16)
    lo = (prod - hi.astype(F32)).astype(BF16)
    return (jnp.dot(hi, same_head, preferred_element_type=F32) + jnp.dot(lo, same_head, preferred_element_type=F32))


def _gate_bwd(dmixed, zr, o, conv_w, tm=256):
    seq = zr.shape[0]
    n_tiles = seq // tm
    n_dil = len(DILATIONS)
    a0, h0, b0, c0, g0 = 0, ATTN_W, ATTN_W + CONV_W, ATTN_W + 2 * CONV_W, ATTN_W + 3 * CONV_W

    def body(dm_ref, dmn_ref, zr_ref, zp_ref, zn_ref, o_ref, w_ref, *rest):
        do_refs, dl_refs = rest[:n_dil], rest[n_dil:2 * n_dil]
        dz_ref, dw_ref, do_scr, dl_scr, tmp = rest[2 * n_dil:]
        i = pl.program_id(0)
        part = lambda ref, lo, hi: ref[:, lo:hi].astype(F32)
        ga = part(zr_ref, a0, h0)
        sg = _sigmoid(ga)
        dattn = part(dm_ref, 0, ATTN_W)
        ov = o_ref[...].astype(F32)
        do = dattn * (ga * sg)
        dz_ref[:, a0:h0] = (dattn * ov * (sg * (1.0 + ga * (1.0 - sg)))).astype(BF16)
        li = lax.broadcasted_iota(jnp.int32, (BLK, BLK), 0) // HEAD_DIM
        lj = lax.broadcasted_iota(jnp.int32, (BLK, BLK), 1) // HEAD_DIM
        same_head = (li == lj).astype(BF16)
        prod = do * ov
        for c in range(N_CHUNK):
            cols = slice(c * BLK, (c + 1) * BLK)
            do_scr[c] = do[:, cols]
            dl_scr[c] = _head_sum(prod[:, cols], same_head)
        _to_residues(do_scr, 0, do_refs, tmp, tm, BF16)
        _to_residues(dl_scr, 0, dl_refs, tmp, tm, BF16)

        ch, cb, cc, gc = (part(zr_ref, lo, hi) for lo, hi in ((h0, b0), (b0, c0), (c0, g0), (g0, REST_W)))
        u = cc * ch
        before = jnp.where(i > 0, part(zp_ref, c0, g0) * part(zp_ref, h0, b0), 0.0)
        u1, u2 = _conv_taps(u, before, tm)
        w0, w1, w2 = w_ref[0:1, :], w_ref[1:2, :], w_ref[2:3, :]
        y = u2 * w0 + u1 * w1 + u * w2
        sc = _sigmoid(gc)
        silu_c = gc * sc
        dconv = part(dm_ref, ATTN_W, D_MODEL)
        dz_ref[:, b0:c0] = (dconv * y * silu_c).astype(BF16)
        dz_ref[:, g0:] = (dconv * (cb * y) * (sc * (1.0 + gc * (1.0 - sc)))).astype(BF16)
        dy = dconv * cb * silu_c
        gn = part(zn_ref, g0, REST_W)
        after = jnp.where(i < n_tiles - 1,
                          part(dmn_ref, ATTN_W, D_MODEL) * part(zn_ref, b0, c0) * (gn * _sigmoid(gn)), 0.0)
        row = lax.broadcasted_iota(jnp.int32, dy.shape, 0)
        nxt, nxt2 = after[0:1, :], after[1:2, :]
        dy1 = jnp.where(row == tm - 1, nxt, pltpu.roll(dy, tm - 1, 0))
        dy2 = jnp.where(row == tm - 1, nxt2, jnp.where(row == tm - 2, nxt, pltpu.roll(dy, tm - 2, 0)))
        du = dy * w2 + dy1 * w1 + dy2 * w0
        dz_ref[:, c0:g0] = (du * ch).astype(BF16)
        dz_ref[:, h0:b0] = (du * cc).astype(BF16)
        dws = [jnp.sum(dy * u2, axis=0, keepdims=True), jnp.sum(dy * u1, axis=0, keepdims=True),
               jnp.sum(dy * u, axis=0, keepdims=True)]

        @pl.when(i == 0)
        def _():
            dw_ref[...] = jnp.zeros_like(dw_ref)

        for n, part in enumerate(dws):
            dw_ref[n:n + 1, :] += part

    row_spec = lambda w: pl.BlockSpec((tm, w), lambda i: (i, 0))
    before, after = _halo_specs(tm, seq)
    views = [_residue_spec(tm, dil) for dil in DILATIONS]
    outs = pl.pallas_call(
        body, name="gate_bwd", grid=(n_tiles,),
        in_specs=[row_spec(D_MODEL), after(D_MODEL), row_spec(REST_W), before(REST_W), after(REST_W),
                  row_spec(ATTN_W), _resident((3, CONV_W))],
        out_specs=views * 2 + [row_spec(REST_W), pl.BlockSpec((8, CONV_W), lambda i: (0, 0))],
        out_shape=[_residue_shape(seq, dil, BF16) for dil in DILATIONS] * 2
        + [jax.ShapeDtypeStruct((seq, REST_W), BF16), jax.ShapeDtypeStruct((8, CONV_W), F32)],
        scratch_shapes=[pltpu.VMEM((N_CHUNK, tm, BLK), F32)] * 3,
        compiler_params=_params(1),
    )(dmixed, dmixed, zr, zr, zr, o, conv_w)
    return outs[:n_dil], outs[n_dil:2 * n_dil], outs[2 * n_dil], outs[2 * n_dil + 1]


def _in_bwd(dqs, dks, dvs, dzr, x, d_out, g_pre, w_in_g, tm=256):
    seq = x.shape[0]

    def body(q1, q2, q3, k1, k2, k3, v1, v2, v3, dzr_ref, ca_ref, sa_ref, cb_ref, sb_ref, x_ref, dout_ref, g_ref,
             w_ref, dz_ref, gx_ref, st_ref, *scratch):
        i = pl.program_id(0)
        cos, sin = _tile_rope(ca_ref, sa_ref, cb_ref, sb_ref)
        first_half = (lax.broadcasted_iota(jnp.int32, (tm, BLK), 1) & 32) == 0
        streams = [(q1, q2, q3), (k1, k2, k3), (v1, v2, v3)]
        from4, from16, tmp = scratch[0:3], scratch[3:6], scratch[6]
        per_slab = SHARD_IN // BLK

        def unrope(t):
            return t * cos - _swap_halves(t, first_half) * sin

        def to_positions(a):
            _from_residue(streams[a][1], from4[a], 4, tm, accumulate=False)
            _from_residue(streams[a][2], from16[a], 16, tm, accumulate=False, tmp=tmp)

        def assemble(j):
            for chunk in range(j * per_slab, (j + 1) * per_slab):
                a, c = divmod(chunk, N_CHUNK)
                if a < 3:
                    total = streams[a][0][:, _lanes(0, c)].astype(F32) + from4[a][c] + from16[a][c]
                    val = (unrope(total) if a < 2 else total).astype(BF16)
                else:
                    val = dzr_ref[:, (chunk - 3 * N_CHUNK) * BLK:(chunk - 3 * N_CHUNK + 1) * BLK]
                dz_ref[:, chunk * BLK:(chunk + 1) * BLK] = val
            return dz_ref[:, j * SHARD_IN:(j + 1) * SHARD_IN]

        order = [j for j in range(N_DEV) if j * per_slab >= 3 * N_CHUNK]
        order += [j for j in range(N_DEV) if j not in order]
        assert order[2] * per_slab >= 3 * N_CHUNK
        ahead = assemble(order[0])
        dh = None
        for n, j in enumerate(order):
            part = lax.dot_general(ahead, w_ref[j], (((1,), (1,)), ((), ())), preferred_element_type=F32)
            if n < 3:
                to_positions(n)
            if n + 1 < N_DEV:
                ahead = assemble(order[n + 1])
            dh = part if dh is None else dh + part
        xv = x_ref[...]
        r = lax.rsqrt(jnp.mean(xv * xv, axis=-1, keepdims=True) + NORM_EPS)
        xhat = xv * r
        tg = dh * g_ref[...]
        gx_ref[...] = dout_ref[...] + r * (tg - xhat * jnp.mean(tg * xhat, axis=-1, keepdims=True))
        gsum = jnp.sum(dh * xhat, axis=0, keepdims=True)

        @pl.when(i == 0)
        def _():
            st_ref[...] = jnp.zeros_like(st_ref)

        st_ref[0:1, :] += gsum

    row = lambda w: pl.BlockSpec((tm, w), lambda i: (i, 0))
    return pl.pallas_call(
        body, name="in_bwd", grid=(seq // tm,),
        in_specs=[_residue_spec(tm, dil) for dil in DILATIONS] * 3
        + [row(REST_W)] + _rope_specs(tm) + [row(D_MODEL), row(D_MODEL), _resident((1, D_MODEL)),
                                             _resident((N_DEV, D_MODEL, SHARD_IN))],
        out_specs=[row(IN_W), row(D_MODEL), pl.BlockSpec((8, D_MODEL), lambda i: (0, 0))],
        out_shape=[jax.ShapeDtypeStruct((seq, IN_W), BF16), jax.ShapeDtypeStruct((seq, D_MODEL), F32),
                   jax.ShapeDtypeStruct((8, D_MODEL), F32)],
        scratch_shapes=[pltpu.VMEM((N_CHUNK, tm, BLK), F32)] * 7,
        compiler_params=_params(1),
    )(*dqs, *dks, *dvs, dzr, *_rope_tables(seq, tm), x, d_out, g_pre.reshape(1, D_MODEL), w_in_g)


def _local_step(x, target, g_pre, g_post, w_in_g, w_out_g, conv_w):
    qkv, zr, ht = _fwd_in(x, g_pre, w_in_g)
    parts = [_attn_fwd(*qkv[n], dil) for n, dil in enumerate(DILATIONS)]
    mixed, o, lse = _attn_combine([p[0] for p in parts], [p[1] for p in parts], zr, conv_w)
    d_out, dmixed, dw_out, dw_out_bf, st_post = _out_loss_bwd(mixed, w_out_g, x, target, g_post)
    do, delta, dzr, dconv = _gate_bwd(dmixed, zr, o, conv_w)
    grads = [_attn_bwd(*qkv[n], do[n], lse[n], delta[n], dil) for n, dil in enumerate(DILATIONS)]
    dz, grad_x, st_pre = _in_bwd([g[0] for g in grads], [g[1] for g in grads], [g[2] for g in grads], dzr,
                                 x, d_out, g_pre, w_in_g)
    conv_rows = jnp.pad(dconv[0:3], ((0, 0), (0, D_MODEL - CONV_W)))
    small = jnp.concatenate([st_pre[0:1], st_post[0:2], conv_rows, jnp.zeros((2, D_MODEL), F32)], axis=0)
    return grad_x, ht, dz, dw_out, dw_out_bf, small


def _coords():
    return lax.axis_index("x"), lax.axis_index("y"), lax.axis_index("c")


def _peer(k):
    x, y, c = _coords()
    px = 1 - x if k & 4 else x
    py = 1 - y if k & 2 else y
    pc = 1 - c if k & 1 else c
    return (px, py, pc), 4 * px + 2 * py + pc


HBM_SPEC = pl.BlockSpec(memory_space=pltpu.HBM)
VMEM_SPEC = pl.BlockSpec(memory_space=pltpu.VMEM)


def _ag_weights(w_in, w_out, conv_w):
    def body(win_ref, wout_ref, cw_ref, gin_ref, gout_ref, gcw_ref, win_bf, wout_bf, cw_pad, send_sems, recv_sems,
             local_sems):
        x, y, c = _coords()
        me, sibling = (x, y, c), (x, y, 1 - c)
        flip = lambda v, yes: v + yes - 2 * v * yes
        x_nbr, y_nbr, diagonal = (1 - x, y, c), (x, 1 - y, c), (1 - x, 1 - y, c)
        relay_from = (flip(x, 1 - c), flip(y, c), c)
        relay_to = (flip(x, c), flip(y, 1 - c), c)
        slab = lambda px, py, pc: 4 * px + 2 * py + pc
        win_bf[...] = win_ref[...].astype(BF16)
        wout_bf[...] = wout_ref[...].astype(BF16)
        cw_pad[...] = jnp.zeros_like(cw_pad)
        cw_pad[0:3, 0:CONV_W // N_DEV] = cw_ref[...]
        mine = [win_bf, wout_bf, cw_pad]
        gathered = [gin_ref, gout_ref, gcw_ref]

        def copies(k, block, to, own=False):
            return [pltpu.make_async_remote_copy(src_ref=mine[a] if own else gathered[a].at[slab(*block)],
                                                 dst_ref=gathered[a].at[slab(*block)], send_sem=send_sems.at[k, a],
                                                 recv_sem=recv_sems.at[k, a], device_id=to, device_id_type=MESH)
                    for a in range(3)]

        local = [pltpu.make_async_copy(mine[a], gathered[a].at[slab(*me)], local_sems.at[a]) for a in range(3)]
        for cp in local:
            cp.start()
        started = copies(0, me, sibling, own=True) + copies(1, me, x_nbr, own=True) + copies(2, me, y_nbr, own=True)
        for cp in started:
            cp.start()
        for cp in copies(1, x_nbr, me) + copies(2, y_nbr, me):
            cp.wait_recv()
        onward = copies(3, relay_from, relay_to) + copies(4, x_nbr, sibling) + copies(5, y_nbr, sibling)
        for cp in onward:
            cp.start()
        for cp in copies(3, diagonal, me):
            cp.wait_recv()
        last = copies(6, diagonal, sibling)
        for cp in last:
            cp.start()
        for cp in copies(0, sibling, me):
            cp.wait_recv()
        for k, origin in ((4, (1 - x, y, 1 - c)), (5, (x, 1 - y, 1 - c)), (6, (1 - x, 1 - y, 1 - c))):
            for cp in copies(k, origin, me):
                cp.wait_recv()
        for cp in started + onward + last:
            cp.wait_send()
        for cp in local:
            cp.wait()

    return pl.pallas_call(
        body, name="ag_weights",
        in_specs=[VMEM_SPEC, VMEM_SPEC, VMEM_SPEC], out_specs=[HBM_SPEC, HBM_SPEC, HBM_SPEC],
        out_shape=[jax.ShapeDtypeStruct((N_DEV, D_MODEL, SHARD_IN), BF16),
                   jax.ShapeDtypeStruct((N_DEV, SHARD_OUT, D_MODEL), BF16),
                   jax.ShapeDtypeStruct((N_DEV, 8, BLK), F32)],
        scratch_shapes=[pltpu.VMEM((D_MODEL, SHARD_IN), BF16), pltpu.VMEM((SHARD_OUT, D_MODEL), BF16),
                        pltpu.VMEM((8, BLK), F32), pltpu.SemaphoreType.DMA((N_DEV - 1, 3)),
                        pltpu.SemaphoreType.DMA((N_DEV - 1, 3)), pltpu.SemaphoreType.DMA((3,))],
        compiler_params=pltpu.CompilerParams(vmem_limit_bytes=VMEM_LIMIT),
    )(w_in, w_out, conv_w)


def _dw_in_rs(ht, dz, dw_out, small):
    seq = dz.shape[0]

    def body(cols_ref, ht_ref, dz_ref, dout_ref, sm_ref, own_ref, rin_ref, rout_ref, rsm_ref, to_sibling, landed,
             to_chip, zero_buf, d2d_send, d2d_recv, ici_send, ici_recv, side_send, side_recv, local_sems):
        del cols_ref
        step = pl.program_id(0)
        x, y, c = _coords()
        me = 4 * x + 2 * y + c
        sibling = (x, y, 1 - c)
        chips = [(1 - x, y), (x, 1 - y), (1 - x, 1 - y)]

        def d2d(n):
            return pltpu.make_async_remote_copy(src_ref=to_sibling.at[n], dst_ref=landed.at[n], send_sem=d2d_send.at[n],
                                                recv_sem=d2d_recv.at[n], device_id=sibling, device_id_type=MESH)

        def ici(n):
            return pltpu.make_async_remote_copy(src_ref=to_chip.at[n], dst_ref=rin_ref.at[n], send_sem=ici_send.at[n],
                                                recv_sem=ici_recv.at[n], device_id=(*chips[n], c), device_id_type=MESH)

        def side(k, mine):
            peer, peer_idx = _peer(k)
            src_slab, dst_slab = (peer_idx, me) if mine else (me, peer_idx)
            pairs = [(dout_ref.at[src_slab], rout_ref.at[dst_slab]), (sm_ref, rsm_ref.at[dst_slab])]
            return [pltpu.make_async_remote_copy(src_ref=src, dst_ref=dst, send_sem=side_send.at[k - 1, a],
                                                 recv_sem=side_recv.at[k - 1, a], device_id=peer, device_id_type=MESH)
                    for a, (src, dst) in enumerate(pairs)]

        local = [pltpu.make_async_copy(zero_buf, rout_ref.at[me], local_sems.at[0]),
                 pltpu.make_async_copy(sm_ref, rsm_ref.at[me], local_sems.at[1])]

        @pl.when(step == 0)
        def _():
            zero_buf[...] = jnp.zeros_like(zero_buf)
            for cp in local:
                cp.start()
            for k in range(1, N_DEV):
                for cp in side(k, mine=True):
                    cp.start()

        dw = jnp.dot(ht_ref[...], dz_ref[...], preferred_element_type=F32)
        for n, at in zip(range(4), (0, 1, 2, N_DEV - 2)):
            @pl.when(step == at)
            def _(n=n):
                to_sibling[n] = dw.astype(BF16)
                d2d(n).start()

        for n in range(3):
            @pl.when(step == 3 + n)
            def _(n=n):
                d2d(n).wait_recv()
                to_chip[n] = (dw + landed[n].astype(F32)).astype(BF16)
                ici(n).start()

        @pl.when(step == N_DEV - 1)
        def _():
            d2d(3).wait_recv()
            own_ref[...] = dw + landed[3].astype(F32)
            for n in range(3):
                ici(n).wait_recv()
            for k in range(1, N_DEV):
                for cp in side(k, mine=False):
                    cp.wait_recv()
            for n in range(4):
                d2d(n).wait_send()
            for n in range(3):
                ici(n).wait_send()
            for k in range(1, N_DEV):
                for cp in side(k, mine=True):
                    cp.wait_send()
            for cp in local:
                cp.wait()

    x, y, c = _coords()
    others = [(1 - x, y), (x, 1 - y), (1 - x, 1 - y)]
    order = [(*chip, 1 - c) for chip in others] + [(*chip, c) for chip in others] + [(x, y, 1 - c), (x, y, c)]
    cols = jnp.stack([4 * px + 2 * py + pc for px, py, pc in order]).astype(jnp.int32)
    slab = (D_MODEL, SHARD_IN)
    grid_spec = pltpu.PrefetchScalarGridSpec(
        num_scalar_prefetch=1, grid=(N_DEV,),
        in_specs=[pl.BlockSpec((D_MODEL, seq), lambda s, cols: (0, 0), pipeline_mode=pl.Buffered(1)),
                  pl.BlockSpec((seq, SHARD_IN), lambda s, cols: (0, cols[s])), HBM_SPEC, HBM_SPEC],
        out_specs=[pl.BlockSpec(slab, lambda s, cols: (0, 0)), HBM_SPEC, HBM_SPEC, HBM_SPEC],
        scratch_shapes=[pltpu.VMEM((4, *slab), BF16), pltpu.VMEM((4, *slab), BF16), pltpu.VMEM((3, *slab), BF16),
                        pltpu.VMEM((SHARD_OUT, D_MODEL), BF16),
                        pltpu.SemaphoreType.DMA((4,)), pltpu.SemaphoreType.DMA((4,)),
                        pltpu.SemaphoreType.DMA((3,)), pltpu.SemaphoreType.DMA((3,)),
                        pltpu.SemaphoreType.DMA((N_DEV - 1, 2)), pltpu.SemaphoreType.DMA((N_DEV - 1, 2)),
                        pltpu.SemaphoreType.DMA((2,))])
    return pl.pallas_call(
        body, name="dw_in_rs", grid_spec=grid_spec,
        out_shape=[jax.ShapeDtypeStruct(slab, F32),
                   jax.ShapeDtypeStruct((3, *slab), BF16),
                   jax.ShapeDtypeStruct((N_DEV, SHARD_OUT, D_MODEL), BF16),
                   jax.ShapeDtypeStruct((N_DEV, 8, D_MODEL), F32)],
        compiler_params=_params(1),
    )(cols, ht, dz, dw_out, small)


def _adamw_math(w, g, m, v):
    m = ADAM_B1 * m + (1.0 - ADAM_B1) * g
    v = ADAM_B2 * v + (1.0 - ADAM_B2) * (g * g)
    m_hat = m / (1.0 - ADAM_B1 ** ADAM_STEP)
    v_hat = v / (1.0 - ADAM_B2 ** ADAM_STEP)
    delta = -ADAM_LR * (m_hat / (jnp.sqrt(v_hat) + ADAM_EPS) + ADAM_WD * w)
    return delta, m, v


def _sum_slabs(ref, first=None):
    total = ref[0].astype(F32) if first is None else first + ref[0].astype(F32)
    for s in range(1, ref.shape[0]):
        total = total + ref[s].astype(F32)
    return total


def _adamw_slabs(parts, own, w, m, v, name, tr):
    rows, cols = w.shape
    tile = pl.BlockSpec((tr, cols), lambda i: (i, 0))

    def body(p_ref, *refs):
        own_ref = refs[0] if own is not None else None
        w_ref, m_ref, v_ref, g_ref, d_ref, nm_ref, nv_ref = refs[-7:]
        g = _sum_slabs(p_ref, None if own_ref is None else own_ref[...])
        g_ref[...] = g
        d_ref[...], nm_ref[...], nv_ref[...] = _adamw_math(w_ref[...], g, m_ref[...], v_ref[...])

    extra = [] if own is None else [own]
    return pl.pallas_call(
        body, name=name, grid=(rows // tr,),
        in_specs=[pl.BlockSpec((parts.shape[0], tr, cols), lambda i: (0, i, 0))] + [tile] * (len(extra) + 3),
        out_specs=[tile] * 4,
        out_shape=[jax.ShapeDtypeStruct((rows, cols), F32)] * 4,
        compiler_params=_params(1),
    )(parts, *extra, w, m, v)


def _adamw_small(parts, me, pre, post, conv):
    n_conv = CONV_W // N_DEV

    def body(me_ref, p_ref, *refs):
        ins, outs = refs[:9], refs[9:]
        sums = _sum_slabs(p_ref)
        outs[0][...] = sums
        for n, g in enumerate((sums[0:1], sums[1:2])):
            w_ref, m_ref, v_ref = ins[3 * n:3 * n + 3]
            for out, val in zip(outs[1 + 3 * n:4 + 3 * n], _adamw_math(w_ref[...], g, m_ref[...], v_ref[...])):
                out[...] = val
        mine = pltpu.roll(sums[:, 0:CONV_W], (CONV_W - me_ref[0] * n_conv) % CONV_W, 1)[3:6, 0:n_conv]
        w_ref, m_ref, v_ref = ins[6:9]
        outs[7][...] = mine
        for out, val in zip(outs[8:11], _adamw_math(w_ref[...], mine, m_ref[...], v_ref[...])):
            out[...] = val

    row = jax.ShapeDtypeStruct((1, D_MODEL), F32)
    small = jax.ShapeDtypeStruct((3, n_conv), F32)
    return pl.pallas_call(
        body, name="adamw_small",
        in_specs=[pl.BlockSpec(memory_space=pltpu.SMEM)] + [VMEM_SPEC] * 10,
        out_shape=[jax.ShapeDtypeStruct(parts.shape[1:], F32)] + [row] * 6 + [small] * 4,
    )(me.reshape(1).astype(jnp.int32), parts, *pre, *post, *conv)


def kernel(x, norm_pre_g, w_in, conv_w, w_out, norm_post_g, loss_target, m_norm_pre_g, m_w_in, m_conv_w, m_w_out,
           m_norm_post_g, v_norm_pre_g, v_w_in, v_conv_w, v_w_out, v_norm_post_g):
    n_conv = CONV_W // N_DEV
    w_in_g, w_out_g, conv_g = _ag_weights(w_in, w_out, conv_w)
    conv_full = conv_g[:, 0:3, 0:n_conv].transpose(1, 0, 2).reshape(3, CONV_W)
    grad_x, ht, dz, dw_out, dw_out_bf, small = _local_step(x[0], loss_target[0], norm_pre_g, norm_post_g, w_in_g,
                                                           w_out_g.reshape(D_MODEL, D_MODEL), conv_full)
    own_in, r_in, r_out, r_small = _dw_in_rs(ht, dz, dw_out_bf.reshape(N_DEV, SHARD_OUT, D_MODEL), small)
    me = 4 * lax.axis_index("x") + 2 * lax.axis_index("y") + lax.axis_index("c")
    own_out = lax.dynamic_index_in_dim(dw_out.reshape(N_DEV, SHARD_OUT, D_MODEL), me, keepdims=False)
    g_in, d_in, nm_in, nv_in = _adamw_slabs(r_in, own_in, w_in, m_w_in, v_w_in, "adamw_in", 256)
    g_out, d_out, nm_out, nv_out = _adamw_slabs(r_out, own_out, w_out, m_w_out, v_w_out, "adamw_out", SHARD_OUT)
    vec = lambda a: a.reshape(1, D_MODEL)
    (sums, d_pre, nm_pre, nv_pre, d_post, nm_post, nv_post, g_conv, d_conv, nm_conv, nv_conv) = _adamw_small(
        r_small, me, [vec(a) for a in (norm_pre_g, m_norm_pre_g, v_norm_pre_g)],
        [vec(a) for a in (norm_post_g, m_norm_post_g, v_norm_post_g)], (conv_w, m_conv_w, v_conv_w))
    g_pre, g_post, loss = sums[0], sums[1], sums[2, 0]
    flat = lambda a: a.reshape(D_MODEL)
    return (loss, grad_x[None], g_pre, g_in, g_conv, g_out, g_post,
            flat(d_pre), d_in, d_conv, d_out, flat(d_post),
            flat(nm_pre), nm_in, nm_conv, nm_out, flat(nm_post),
            flat(nv_pre), nv_in, nv_conv, nv_out, flat(nv_post))
```

```python
import functools

import jax
import jax.numpy as jnp
from jax import lax
from jax.experimental import pallas as pl
from jax.experimental.pallas import tpu as pltpu

F32 = jnp.float32
BF16 = jnp.bfloat16

D_MODEL = 1024
HEAD_DIM = 64
ATTN_W = 768
CONV_W = 256
IN_W = 4096
REST_W = IN_W - 3 * ATTN_W
BLK = 128
N_DEV = 8
SHARD_IN = IN_W // N_DEV
SHARD_OUT = D_MODEL // N_DEV
DILATIONS = (1, 4, 16)
ROPE_THETA = 10000.0
NORM_EPS = 1e-6
NEG = -1e30

ADAM_LR = 0.001
ADAM_B1 = 0.9
ADAM_B2 = 0.999
ADAM_EPS = 1e-08
ADAM_WD = 0.01
ADAM_STEP = 10

VMEM_LIMIT = 56 * 1024 * 1024
MESH = pl.DeviceIdType.MESH


def _params(n_grid):
    return pltpu.CompilerParams(dimension_semantics=("arbitrary",) * n_grid, vmem_limit_bytes=VMEM_LIMIT)


def _resident(shape):
    zeros = (0,) * len(shape)
    return pl.BlockSpec(shape, lambda *_: zeros, pipeline_mode=pl.Buffered(1))


def _sigmoid(a):
    return 1.0 / (1.0 + jnp.exp(-a))


def _swap_halves(t, first_half):
    return jnp.where(first_half, pltpu.roll(t, BLK - 32, 1), pltpu.roll(t, 32, 1))


def _rope_tables(seq, tm):
    half = HEAD_DIM // 2
    inv_freq = ROPE_THETA ** (-jnp.arange(half, dtype=F32) * 2.0 / HEAD_DIM)
    freq = jnp.concatenate([inv_freq] * 4)
    sign = jnp.concatenate([-jnp.ones(half, F32), jnp.ones(half, F32)] * 2)
    starts = (jnp.arange(seq // tm) * tm).astype(F32)[:, None] * freq[None, :]
    rows = jnp.arange(tm).astype(F32)[:, None] * freq[None, :]
    slab = lambda a: jnp.broadcast_to(a[:, None, :], (seq // tm, 8, BLK))
    return slab(jnp.cos(starts)), slab(jnp.sin(starts) * sign), jnp.cos(rows), jnp.sin(rows) * sign


def _rope_specs(tm):
    return [pl.BlockSpec((1, 8, BLK), lambda i: (i, 0, 0))] * 2 + [_resident((tm, BLK))] * 2


def _tile_rope(cos_start, sin_start, cos_row, sin_row):
    ca, sa, cb, sb = cos_start[0, 0:1, :], sin_start[0, 0:1, :], cos_row[...], sin_row[...]
    return ca * cb - sa * sb, sa * cb + ca * sb


N_CHUNK = ATTN_W // BLK


def _lanes(r, c):
    return slice(r * ATTN_W + c * BLK, r * ATTN_W + (c + 1) * BLK)


def _to_residues(src, chunk0, dst_refs, tmp, rows, dtype):
    assert DILATIONS == (1, 4, 16)
    dst1, dst4, dst16 = dst_refs
    n4, n16 = rows // 4, rows // 16
    for c in range(N_CHUNK):
        dst1[:, _lanes(0, c)] = src[chunk0 + c].astype(dtype)
        for r1 in range(4):
            tmp[c, r1 * n4:(r1 + 1) * n4, :] = src[chunk0 + c, pl.ds(r1, n4, stride=4), :]
        for r1 in range(4):
            dst4[:, _lanes(r1, c)] = tmp[c, r1 * n4:(r1 + 1) * n4, :].astype(dtype)
            for r2 in range(4):
                dst16[:, _lanes(4 * r2 + r1, c)] = tmp[c, pl.ds(r1 * n4 + r2, n16, stride=4), :].astype(dtype)


def _from_residue(src_ref, dst, dil, rows, accumulate, tmp=None):
    n4, n16 = rows // 4, rows // 16

    def put(where, piece):
        if accumulate:
            dst[where] += piece
        else:
            dst[where] = piece

    for c in range(N_CHUNK):
        if dil == 1:
            put((c,), src_ref[:, _lanes(0, c)].astype(F32))
            continue
        for r1 in range(4):
            if dil == 4:
                piece = src_ref[:, _lanes(r1, c)].astype(F32)
            else:
                for r2 in range(4):
                    tmp[c, pl.ds(r1 * n4 + r2, n16, stride=4), :] = src_ref[:, _lanes(4 * r2 + r1, c)].astype(F32)
                piece = tmp[c, r1 * n4:(r1 + 1) * n4, :]
            put((c, pl.ds(r1, n4, stride=4), slice(None)), piece)


def _residue_spec(tm, dil):
    return pl.BlockSpec((tm // dil, dil * ATTN_W), lambda i: (i, 0))


def _residue_shape(seq, dil, dtype):
    return jax.ShapeDtypeStruct((seq // dil, dil * ATTN_W), dtype)


def _fwd_in(x, g_pre, w_in_g, tm=512):
    seq = x.shape[0]
    n_dil = len(DILATIONS)

    def body(x_ref, g_ref, w_ref, ca_ref, sa_ref, cb_ref, sb_ref, *rest):
        qkv_refs, (zr_ref, ht_ref, qkv_scr, tmp) = rest[:3 * n_dil], rest[3 * n_dil:]
        xv = x_ref[...]
        r = lax.rsqrt(jnp.mean(xv * xv, axis=-1, keepdims=True) + NORM_EPS)
        hf = (xv * r) * g_ref[...]
        h = hf.astype(BF16)
        ht_ref[...] = h.T
        cos, sin = _tile_rope(ca_ref, sa_ref, cb_ref, sb_ref)
        first_half = (lax.broadcasted_iota(jnp.int32, (tm, BLK), 1) & 32) == 0

        def rope(t):
            return t * cos + _swap_halves(t, first_half) * sin

        def project(j):
            return jnp.dot(h, w_ref[j], preferred_element_type=F32)

        def place(j, zj):
            for n in range(SHARD_IN // BLK):
                chunk, t = j * (SHARD_IN // BLK) + n, zj[:, n * BLK:(n + 1) * BLK]
                if chunk < N_CHUNK:
                    qkv_scr[chunk] = rope(t) * HEAD_DIM ** -0.5
                elif chunk < 2 * N_CHUNK:
                    qkv_scr[chunk] = rope(t)
                elif chunk < 3 * N_CHUNK:
                    qkv_scr[chunk] = t
                else:
                    zr_ref[:, (chunk - 3 * N_CHUNK) * BLK:(chunk - 3 * N_CHUNK + 1) * BLK] = t.astype(BF16)

        ahead = project(0)
        for j in range(N_DEV):
            zj = ahead
            if j + 1 < N_DEV:
                ahead = project(j + 1)
            place(j, zj)
            for a in range(3):
                if (a + 1) * N_CHUNK - 1 in range(j * (SHARD_IN // BLK), (j + 1) * (SHARD_IN // BLK)):
                    _to_residues(qkv_scr, a * N_CHUNK, [qkv_refs[3 * n + a] for n in range(n_dil)], tmp, tm, BF16)

    row = lambda w: pl.BlockSpec((tm, w), lambda i: (i, 0))
    outs = pl.pallas_call(
        body, name="fwd_in", grid=(seq // tm,),
        in_specs=[row(D_MODEL), _resident((1, D_MODEL)), _resident((N_DEV, D_MODEL, SHARD_IN))] + _rope_specs(tm),
        out_specs=[_residue_spec(tm, dil) for dil in DILATIONS for _ in range(3)]
        + [row(REST_W), pl.BlockSpec((D_MODEL, tm), lambda i: (0, i))],
        out_shape=[_residue_shape(seq, dil, BF16) for dil in DILATIONS for _ in range(3)]
        + [jax.ShapeDtypeStruct((seq, REST_W), BF16), jax.ShapeDtypeStruct((D_MODEL, seq), BF16)],
        scratch_shapes=[pltpu.VMEM((3 * N_CHUNK, tm, BLK), F32), pltpu.VMEM((N_CHUNK, tm, BLK), F32)],
        compiler_params=_params(1),
    )(x, g_pre.reshape(1, D_MODEL), w_in_g, *_rope_tables(seq, tm))
    qkv = [tuple(outs[3 * n:3 * n + 3]) for n in range(n_dil)]
    return qkv, outs[3 * n_dil], outs[3 * n_dil + 1]


def _band_bias(first_block):
    kj = lax.broadcasted_iota(jnp.int32, (2 * BLK, BLK), 0)
    qi = lax.broadcasted_iota(jnp.int32, (2 * BLK, BLK), 1)
    valid = (kj >= qi) & (kj <= qi + BLK)
    bias = jnp.where(valid, 0.0, NEG).astype(BF16)
    bias_first = jnp.where(valid & (kj >= BLK), 0.0, NEG).astype(BF16)
    onehot = ((kj & (BLK - 1)) == qi).astype(F32).astype(BF16)
    return onehot, bias, jnp.where(first_block, bias_first, bias)


def _stack_heads(t):
    keep0 = (lax.broadcasted_iota(jnp.int32, t.shape, 1) < HEAD_DIM).astype(F32).astype(BF16)
    return jnp.concatenate([t * keep0, t * (1 - keep0)], axis=0)


def _unstack_heads(t2, head0):
    return jnp.where(head0, t2[:BLK], t2[BLK:])


def _rows_per_head(a, head0):
    b = pltpu.roll(a, HEAD_DIM, 1)
    rows = jnp.concatenate([jnp.where(head0, a, b), jnp.where(head0, b, a)], axis=0)
    return jnp.concatenate([rows, rows], axis=1)


BLOCKS_PER_STEP = 32


def _attn_specs(length, dil, max_cols=8, units=BLOCKS_PER_STEP):
    n_blocks = length // BLK
    tb = min(units, n_blocks)
    nc = max(n for n in range(1, min(units // tb, max_cols) + 1) if (dil * N_CHUNK) % n == 0)
    assert n_blocks % tb == 0
    tile = pl.BlockSpec((tb * BLK, nc * BLK), lambda c, t: (t, c))
    prev = pl.BlockSpec((BLK, nc * BLK), lambda c, t: (jnp.maximum(t * tb - 1, 0), c))
    grid = (dil * N_CHUNK // nc, n_blocks // tb)
    return tb, nc, tile, prev, grid


def _window(prev_ref, cur_ref, j, cols):
    if j == 0:
        return jnp.concatenate([prev_ref[:, cols], cur_ref[0:BLK, cols]], axis=0)
    return cur_ref[(j - 1) * BLK:(j + 1) * BLK, cols]


def _attn_fwd(q, k, v, dil):
    length = q.shape[0]
    tb, nc, tile, prev, grid = _attn_specs(length, dil)

    def body(q_ref, kc_ref, kp_ref, vc_ref, vp_ref, o_ref, lse_ref):
        head0 = lax.broadcasted_iota(jnp.int32, (BLK, BLK), 1) < HEAD_DIM
        onehot, bias, bias_start = _band_bias(pl.program_id(1) == 0)
        ones = jnp.ones((2 * BLK, BLK), BF16)
        def scores(c, j):
            rows, cols = slice(j * BLK, (j + 1) * BLK), slice(c * BLK, (c + 1) * BLK)
            q2 = jnp.concatenate([_stack_heads(q_ref[rows, cols]), onehot], axis=1)
            kk = jnp.concatenate([_window(kp_ref, kc_ref, j, cols), bias_start if j == 0 else bias], axis=1)
            return (lax.dot_general(q2, kk, (((1,), (1,)), ((), ())), preferred_element_type=F32),)

        def probabilities(c, j, s):
            m = jnp.max(s, axis=1, keepdims=True)
            return m, jnp.exp(s - m).astype(BF16)

        def outputs(c, j, m, p):
            rows, cols = slice(j * BLK, (j + 1) * BLK), slice(c * BLK, (c + 1) * BLK)
            vv = jnp.concatenate([_window(vp_ref, vc_ref, j, cols), ones], axis=1)
            pv = jnp.dot(p, vv, preferred_element_type=F32)
            den = pv[:, BLK:]
            o_ref[rows, cols] = _unstack_heads(pv[:, :BLK] / den, head0).astype(BF16)
            lse_ref[rows, cols] = _unstack_heads(m + jnp.log(den), head0)

        units = [(c, j) for c in range(nc) for j in range(tb)]
        stage1, stage2 = {}, {}
        for n in range(len(units) + 2):
            if n < len(units):
                stage1[n] = scores(*units[n])
            if 0 <= n - 1 < len(units):
                stage2[n - 1] = probabilities(*units[n - 1], *stage1.pop(n - 1))
            if 0 <= n - 2 < len(units):
                outputs(*units[n - 2], *stage2.pop(n - 2))

    return pl.pallas_call(
        body, name=f"attn_fwd_d{dil}", grid=grid,
        in_specs=[tile, tile, prev, tile, prev], out_specs=[tile, tile],
        out_shape=[jax.ShapeDtypeStruct(q.shape, BF16), jax.ShapeDtypeStruct(q.shape, F32)],
        compiler_params=_params(2),
    )(q, k, k, v, v)


def _attn_bwd(q, k, v, do, lse, delta, dil):
    length = q.shape[0]
    tb, nc, tile, prev, grid = _attn_specs(length, dil, max_cols=4)
    whole = pl.BlockSpec((length, nc * BLK), lambda c, t: (0, c))

    def body(q_ref, do_ref, lse_ref, dl_ref, kc_ref, kp_ref, vc_ref, vp_ref, dq_ref, dk_ref, dv_ref):
        t = pl.program_id(1)
        head0 = lax.broadcasted_iota(jnp.int32, (BLK, BLK), 1) < HEAD_DIM
        onehot, bias, bias_start = _band_bias(t == 0)

        def scores(c, j):
            rows, cols = slice(j * BLK, (j + 1) * BLK), slice(c * BLK, (c + 1) * BLK)
            q2 = _stack_heads(q_ref[rows, cols])
            do2 = _stack_heads(do_ref[rows, cols])
            kk = _window(kp_ref, kc_ref, j, cols)
            s = lax.dot_general(jnp.concatenate([q2, onehot], axis=1),
                                jnp.concatenate([kk, bias_start if j == 0 else bias], axis=1),
                                (((1,), (1,)), ((), ())), preferred_element_type=F32)
            dp = lax.dot_general(do2, _window(vp_ref, vc_ref, j, cols), (((1,), (1,)), ((), ())),
                                 preferred_element_type=F32)
            return q2, do2, kk, s, dp

        def probabilities(c, j, q2, do2, kk, s, dp):
            rows, cols = slice(j * BLK, (j + 1) * BLK), slice(c * BLK, (c + 1) * BLK)
            p = jnp.exp(s - _rows_per_head(lse_ref[rows, cols], head0))
            ds = (p * (dp - _rows_per_head(dl_ref[rows, cols].astype(F32), head0))).astype(BF16)
            return q2, do2, kk, p.astype(BF16), ds

        def gradients(c, j, q2, do2, kk, p, ds):
            rows, cols = slice(j * BLK, (j + 1) * BLK), slice(c * BLK, (c + 1) * BLK)
            dq2 = jnp.dot(ds, kk, preferred_element_type=F32)
            dq_ref[rows, cols] = (_unstack_heads(dq2, head0) * HEAD_DIM ** -0.5).astype(BF16)
            dk2 = lax.dot_general(ds, q2, (((0,), (0,)), ((), ())), preferred_element_type=F32)
            dv2 = lax.dot_general(p, do2, (((0,), (0,)), ((), ())), preferred_element_type=F32)
            own = pl.ds(pl.multiple_of((t * tb + j) * BLK, BLK), BLK)
            dk_ref[own, cols] = dk2[BLK:].astype(BF16)
            dv_ref[own, cols] = dv2[BLK:].astype(BF16)

            def add_to_previous():
                before = pl.ds(pl.multiple_of((t * tb + j - 1) * BLK, BLK), BLK)
                dk_ref[before, cols] = (dk_ref[before, cols].astype(F32) + dk2[:BLK]).astype(BF16)
                dv_ref[before, cols] = (dv_ref[before, cols].astype(F32) + dv2[:BLK]).astype(BF16)

            if j > 0:
                add_to_previous()
            elif grid[1] > 1:
                pl.when(t > 0)(add_to_previous)

        units = [(c, j) for c in range(nc) for j in range(tb)]
        stage1 = {0: scores(*units[0])}
        for n in range(len(units)):
            stage2 = probabilities(*units[n], *stage1.pop(n))
            if n + 1 < len(units):
                stage1[n + 1] = scores(*units[n + 1])
            gradients(*units[n], *stage2)

    return pl.pallas_call(
        body, name=f"attn_bwd_d{dil}", grid=grid,
        in_specs=[tile, tile, tile, tile, tile, prev, tile, prev], out_specs=[tile, whole, whole],
        out_shape=[jax.ShapeDtypeStruct(q.shape, BF16)] * 3,
        compiler_params=_params(2),
    )(q, do, lse, delta, k, k, v, v)


HALO = 16


def _halo_specs(tm, seq):
    before = lambda w: pl.BlockSpec((HALO, w), lambda i: (jnp.maximum(i * (tm // HALO) - 1, 0), 0))
    after = lambda w: pl.BlockSpec((HALO, w), lambda i: (jnp.minimum((i + 1) * (tm // HALO), seq // HALO - 1), 0))
    return before, after


def _conv_taps(u, before, tm):
    row = lax.broadcasted_iota(jnp.int32, u.shape, 0)
    last, last2 = before[HALO - 1:HALO, :], before[HALO - 2:HALO - 1, :]
    u1 = jnp.where(row == 0, last, pltpu.roll(u, 1, 0))
    u2 = jnp.where(row == 0, last2, jnp.where(row == 1, last, pltpu.roll(u, 2, 0)))
    return u1, u2


def _attn_combine(o_parts, lse_parts, zr, conv_w, tm=256):
    seq = zr.shape[0]
    a0, h0, b0, c0, g0 = 0, ATTN_W, ATTN_W + CONV_W, ATTN_W + 2 * CONV_W, ATTN_W + 3 * CONV_W

    def body(o1, o2, o3, l1, l2, l3, zr_ref, zp_ref, w_ref, mixed_ref, o_ref, lse1, lse2, lse3, *scr):
        i = pl.program_id(0)
        for src, dst, dil in zip((o2, o3, l2, l3), scr[:4], DILATIONS[1:] * 2):
            _from_residue(src, dst, dil, tm, accumulate=False, tmp=scr[5])
        for c in range(N_CHUNK):
            cols = slice(c * BLK, (c + 1) * BLK)
            la, lb, lc = l1[:, cols], scr[2][c], scr[3][c]
            top = jnp.maximum(jnp.maximum(la, lb), lc)
            ea, eb, ec = jnp.exp(la - top), jnp.exp(lb - top), jnp.exp(lc - top)
            den = ea + eb + ec
            inv = 1.0 / den
            o = (ea * inv) * o1[:, cols].astype(F32) + (eb * inv) * scr[0][c] + (ec * inv) * scr[1][c]
            o_ref[:, cols] = o.astype(BF16)
            scr[4][c] = top + jnp.log(den)
            ga = zr_ref[:, cols].astype(F32)
            mixed_ref[:, cols] = (o * (ga * _sigmoid(ga))).astype(BF16)
        _to_residues(scr[4], 0, (lse1, lse2, lse3), scr[5], tm, F32)
        part = lambda ref, lo, hi: ref[:, lo:hi].astype(F32)
        u = part(zr_ref, c0, g0) * part(zr_ref, h0, b0)
        before = jnp.where(i > 0, part(zp_ref, c0, g0) * part(zp_ref, h0, b0), 0.0)
        u1, u2 = _conv_taps(u, before, tm)
        y = u2 * w_ref[0:1, :] + u1 * w_ref[1:2, :] + u * w_ref[2:3, :]
        gc = part(zr_ref, g0, REST_W)
        mixed_ref[:, ATTN_W:] = ((part(zr_ref, b0, c0) * y) * (gc * _sigmoid(gc))).astype(BF16)

    row = lambda w: pl.BlockSpec((tm, w), lambda i: (i, 0))
    before, _ = _halo_specs(tm, seq)
    views = [_residue_spec(tm, dil) for dil in DILATIONS]
    outs = pl.pallas_call(
        body, name="attn_combine", grid=(seq // tm,),
        in_specs=views * 2 + [row(REST_W), before(REST_W), _resident((3, CONV_W))],
        out_specs=[row(D_MODEL), row(ATTN_W)] + views,
        out_shape=[jax.ShapeDtypeStruct((seq, D_MODEL), BF16), jax.ShapeDtypeStruct((seq, ATTN_W), BF16)]
        + [_residue_shape(seq, dil, F32) for dil in DILATIONS],
        scratch_shapes=[pltpu.VMEM((N_CHUNK, tm, BLK), F32)] * 6,
        compiler_params=_params(1),
    )(*o_parts, *lse_parts, zr, zr, conv_w)
    return outs[0], outs[1], outs[2:]


def _out_loss_bwd(mixed, w_out_g, x, target, g_post, tm=512, n_parts=2):
    seq = x.shape[0]

    def body(mx_ref, w_ref, x_ref, t_ref, g_ref, dout_ref, dmx_ref, dw_ref, dwb_ref, st_ref):
        i = pl.program_id(0)
        g = g_ref[...]
        parts = [slice(n * (tm // n_parts), (n + 1) * (tm // n_parts)) for n in range(n_parts)]

        def project(rows):
            return jnp.dot(mx_ref[rows, :], w_ref[...], preferred_element_type=F32)

        def head(rows, y):
            r = lax.rsqrt(jnp.mean(y * y, axis=-1, keepdims=True) + NORM_EPS)
            yhat = y * r
            err = (x_ref[rows, :] + yhat * g) - t_ref[rows, :]
            dn = err * (1.0 / D_MODEL)
            dout_ref[rows, :] = dn
            tg = dn * g
            dy = (r * (tg - yhat * jnp.mean(tg * yhat, axis=-1, keepdims=True))).astype(BF16)
            dmx_ref[rows, :] = lax.dot_general(dy, w_ref[...], (((1,), (1,)), ((), ())),
                                               preferred_element_type=F32).astype(BF16)
            return dy, jnp.sum(dn * yhat, axis=0, keepdims=True), jnp.sum(err * err)

        ahead, done = project(parts[0]), []
        for n, rows in enumerate(parts):
            y = ahead
            if n + 1 < n_parts:
                ahead = project(parts[n + 1])
            done.append(head(rows, y))
        dy = jnp.concatenate([d[0] for d in done], axis=0)
        dw = lax.dot_general(mx_ref[...], dy, (((0,), (0,)), ((), ())), preferred_element_type=F32)
        gsum = functools.reduce(lambda a, b: a + b, [d[1] for d in done])
        lsum = jnp.broadcast_to(0.5 / D_MODEL * functools.reduce(lambda a, b: a + b, [d[2] for d in done]),
                                (1, D_MODEL))

        @pl.when(i == 0)
        def _():
            dw_ref[...] = dw
            st_ref[...] = jnp.zeros_like(st_ref)
            st_ref[0:1, :] = gsum
            st_ref[1:2, :] = lsum

        @pl.when(i > 0)
        def _():
            dw_ref[...] += dw
            st_ref[0:1, :] += gsum
            st_ref[1:2, :] += lsum

        @pl.when(i == seq // tm - 1)
        def _():
            dwb_ref[...] = dw_ref[...].astype(BF16)

    row = lambda w: pl.BlockSpec((tm, w), lambda i: (i, 0))
    whole = pl.BlockSpec((D_MODEL, D_MODEL), lambda i: (0, 0))
    return pl.pallas_call(
        body, name="out_loss_bwd", grid=(seq // tm,),
        in_specs=[row(D_MODEL), _resident((D_MODEL, D_MODEL)), row(D_MODEL), row(D_MODEL), _resident((1, D_MODEL))],
        out_specs=[row(D_MODEL), row(D_MODEL), whole, whole, pl.BlockSpec((8, D_MODEL), lambda i: (0, 0))],
        out_shape=[jax.ShapeDtypeStruct((seq, D_MODEL), F32), jax.ShapeDtypeStruct((seq, D_MODEL), BF16),
                   jax.ShapeDtypeStruct((D_MODEL, D_MODEL), F32), jax.ShapeDtypeStruct((D_MODEL, D_MODEL), BF16),
                   jax.ShapeDtypeStruct((8, D_MODEL), F32)],
        compiler_params=_params(1),
    )(mixed, w_out_g, x, target, g_post.reshape(1, D_MODEL))


def _head_sum(prod, same_head):
    hi = prod.astype(BF16)
    lo = (prod - hi.astype(F32)).astype(BF16)
    return (jnp.dot(hi, same_head, preferred_element_type=F32) + jnp.dot(lo, same_head, preferred_element_type=F32))


def _gate_bwd(dmixed, zr, o, conv_w, tm=256):
    seq = zr.shape[0]
    n_tiles = seq // tm
    n_dil = len(DILATIONS)
    a0, h0, b0, c0, g0 = 0, ATTN_W, ATTN_W + CONV_W, ATTN_W + 2 * CONV_W, ATTN_W + 3 * CONV_W

    def body(dm_ref, dmn_ref, zr_ref, zp_ref, zn_ref, o_ref, w_ref, *rest):
        do_refs, dl_refs = rest[:n_dil], rest[n_dil:2 * n_dil]
        dz_ref, dw_ref, do_scr, dl_scr, tmp = rest[2 * n_dil:]
        i = pl.program_id(0)
        part = lambda ref, lo, hi: ref[:, lo:hi].astype(F32)
        ga = part(zr_ref, a0, h0)
        sg = _sigmoid(ga)
        dattn = part(dm_ref, 0, ATTN_W)
        ov = o_ref[...].astype(F32)
        do = dattn * (ga * sg)
        dz_ref[:, a0:h0] = (dattn * ov * (sg * (1.0 + ga * (1.0 - sg)))).astype(BF16)
        li = lax.broadcasted_iota(jnp.int32, (BLK, BLK), 0) // HEAD_DIM
        lj = lax.broadcasted_iota(jnp.int32, (BLK, BLK), 1) // HEAD_DIM
        same_head = (li == lj).astype(BF16)
        prod = do * ov
        for c in range(N_CHUNK):
            cols = slice(c * BLK, (c + 1) * BLK)
            do_scr[c] = do[:, cols]
            dl_scr[c] = _head_sum(prod[:, cols], same_head)
        _to_residues(do_scr, 0, do_refs, tmp, tm, BF16)
        _to_residues(dl_scr, 0, dl_refs, tmp, tm, BF16)

        ch, cb, cc, gc = (part(zr_ref, lo, hi) for lo, hi in ((h0, b0), (b0, c0), (c0, g0), (g0, REST_W)))
        u = cc * ch
        before = jnp.where(i > 0, part(zp_ref, c0, g0) * part(zp_ref, h0, b0), 0.0)
        u1, u2 = _conv_taps(u, before, tm)
        w0, w1, w2 = w_ref[0:1, :], w_ref[1:2, :], w_ref[2:3, :]
        y = u2 * w0 + u1 * w1 + u * w2
        sc = _sigmoid(gc)
        silu_c = gc * sc
        dconv = part(dm_ref, ATTN_W, D_MODEL)
        dz_ref[:, b0:c0] = (dconv * y * silu_c).astype(BF16)
        dz_ref[:, g0:] = (dconv * (cb * y) * (sc * (1.0 + gc * (1.0 - sc)))).astype(BF16)
        dy = dconv * cb * silu_c
        gn = part(zn_ref, g0, REST_W)
        after = jnp.where(i < n_tiles - 1,
                          part(dmn_ref, ATTN_W, D_MODEL) * part(zn_ref, b0, c0) * (gn * _sigmoid(gn)), 0.0)
        row = lax.broadcasted_iota(jnp.int32, dy.shape, 0)
        nxt, nxt2 = after[0:1, :], after[1:2, :]
        dy1 = jnp.where(row == tm - 1, nxt, pltpu.roll(dy, tm - 1, 0))
        dy2 = jnp.where(row == tm - 1, nxt2, jnp.where(row == tm - 2, nxt, pltpu.roll(dy, tm - 2, 0)))
        du = dy * w2 + dy1 * w1 + dy2 * w0
        dz_ref[:, c0:g0] = (du * ch).astype(BF16)
        dz_ref[:, h0:b0] = (du * cc).astype(BF16)
        dws = [jnp.sum(dy * u2, axis=0, keepdims=True), jnp.sum(dy * u1, axis=0, keepdims=True),
               jnp.sum(dy * u, axis=0, keepdims=True)]

        @pl.when(i == 0)
        def _():
            dw_ref[...] = jnp.zeros_like(dw_ref)

        for n, part in enumerate(dws):
            dw_ref[n:n + 1, :] += part

    row_spec = lambda w: pl.BlockSpec((tm, w), lambda i: (i, 0))
    before, after = _halo_specs(tm, seq)
    views = [_residue_spec(tm, dil) for dil in DILATIONS]
    outs = pl.pallas_call(
        body, name="gate_bwd", grid=(n_tiles,),
        in_specs=[row_spec(D_MODEL), after(D_MODEL), row_spec(REST_W), before(REST_W), after(REST_W),
                  row_spec(ATTN_W), _resident((3, CONV_W))],
        out_specs=views * 2 + [row_spec(REST_W), pl.BlockSpec((8, CONV_W), lambda i: (0, 0))],
        out_shape=[_residue_shape(seq, dil, BF16) for dil in DILATIONS] * 2
        + [jax.ShapeDtypeStruct((seq, REST_W), BF16), jax.ShapeDtypeStruct((8, CONV_W), F32)],
        scratch_shapes=[pltpu.VMEM((N_CHUNK, tm, BLK), F32)] * 3,
        compiler_params=_params(1),
    )(dmixed, dmixed, zr, zr, zr, o, conv_w)
    return outs[:n_dil], outs[n_dil:2 * n_dil], outs[2 * n_dil], outs[2 * n_dil + 1]


def _in_bwd(dqs, dks, dvs, dzr, x, d_out, g_pre, w_in_g, st_post, dconv, tm=256):
    seq = x.shape[0]

    def body(q1, q2, q3, k1, k2, k3, v1, v2, v3, dzr_ref, ca_ref, sa_ref, cb_ref, sb_ref, x_ref, dout_ref, g_ref,
             w_ref, post_ref, dconv_ref, dz_ref, gx_ref, st_ref, *scratch):
        i = pl.program_id(0)
        cos, sin = _tile_rope(ca_ref, sa_ref, cb_ref, sb_ref)
        first_half = (lax.broadcasted_iota(jnp.int32, (tm, BLK), 1) & 32) == 0
        streams = [(q1, q2, q3), (k1, k2, k3), (v1, v2, v3)]
        from4, from16, tmp = scratch[0:3], scratch[3:6], scratch[6]
        per_slab = SHARD_IN // BLK

        def unrope(t):
            return t * cos - _swap_halves(t, first_half) * sin

        def to_positions(a):
            _from_residue(streams[a][1], from4[a], 4, tm, accumulate=False)
            _from_residue(streams[a][2], from16[a], 16, tm, accumulate=False, tmp=tmp)

        def assemble(j):
            for chunk in range(j * per_slab, (j + 1) * per_slab):
                a, c = divmod(chunk, N_CHUNK)
                if a < 3:
                    total = streams[a][0][:, _lanes(0, c)].astype(F32) + from4[a][c] + from16[a][c]
                    val = (unrope(total) if a < 2 else total).astype(BF16)
                else:
                    val = dzr_ref[:, (chunk - 3 * N_CHUNK) * BLK:(chunk - 3 * N_CHUNK + 1) * BLK]
                dz_ref[:, chunk * BLK:(chunk + 1) * BLK] = val
            return dz_ref[:, j * SHARD_IN:(j + 1) * SHARD_IN]

        order = [j for j in range(N_DEV) if j * per_slab >= 3 * N_CHUNK]
        order += [j for j in range(N_DEV) if j not in order]
        assert order[2] * per_slab >= 3 * N_CHUNK
        ahead = assemble(order[0])
        dh = None
        for n, j in enumerate(order):
            part = lax.dot_general(ahead, w_ref[j], (((1,), (1,)), ((), ())), preferred_element_type=F32)
            if n < 3:
                to_positions(n)
            if n + 1 < N_DEV:
                ahead = assemble(order[n + 1])
            dh = part if dh is None else dh + part
        xv = x_ref[...]
        r = lax.rsqrt(jnp.mean(xv * xv, axis=-1, keepdims=True) + NORM_EPS)
        xhat = xv * r
        tg = dh * g_ref[...]
        gx_ref[...] = dout_ref[...] + r * (tg - xhat * jnp.mean(tg * xhat, axis=-1, keepdims=True))
        gsum = jnp.sum(dh * xhat, axis=0, keepdims=True)

        @pl.when(i == 0)
        def _():
            st_ref[...] = jnp.zeros_like(st_ref)
            st_ref[1:3, :] = post_ref[0:2, :]
            st_ref[3:6, 0:CONV_W] = dconv_ref[0:3, :]

        st_ref[0:1, :] += gsum

    row = lambda w: pl.BlockSpec((tm, w), lambda i: (i, 0))
    return pl.pallas_call(
        body, name="in_bwd", grid=(seq // tm,),
        in_specs=[_residue_spec(tm, dil) for dil in DILATIONS] * 3
        + [row(REST_W)] + _rope_specs(tm) + [row(D_MODEL), row(D_MODEL), _resident((1, D_MODEL)),
                                             _resident((N_DEV, D_MODEL, SHARD_IN)), _resident((8, D_MODEL)),
                                             _resident((8, CONV_W))],
        out_specs=[row(IN_W), row(D_MODEL), pl.BlockSpec((8, D_MODEL), lambda i: (0, 0))],
        out_shape=[jax.ShapeDtypeStruct((seq, IN_W), BF16), jax.ShapeDtypeStruct((seq, D_MODEL), F32),
                   jax.ShapeDtypeStruct((8, D_MODEL), F32)],
        scratch_shapes=[pltpu.VMEM((N_CHUNK, tm, BLK), F32)] * 7,
        compiler_params=_params(1),
    )(*dqs, *dks, *dvs, dzr, *_rope_tables(seq, tm), x, d_out, g_pre.reshape(1, D_MODEL), w_in_g, st_post, dconv)


def _local_step(x, target, g_pre, g_post, w_in_g, w_out_g, conv_w):
    qkv, zr, ht = _fwd_in(x, g_pre, w_in_g)
    parts = [_attn_fwd(*qkv[n], dil) for n, dil in enumerate(DILATIONS)]
    mixed, o, lse = _attn_combine([p[0] for p in parts], [p[1] for p in parts], zr, conv_w)
    d_out, dmixed, dw_out, dw_out_bf, st_post = _out_loss_bwd(mixed, w_out_g, x, target, g_post)
    do, delta, dzr, dconv = _gate_bwd(dmixed, zr, o, conv_w)
    grads = [_attn_bwd(*qkv[n], do[n], lse[n], delta[n], dil) for n, dil in enumerate(DILATIONS)]
    dz, grad_x, small = _in_bwd([g[0] for g in grads], [g[1] for g in grads], [g[2] for g in grads], dzr,
                                x, d_out, g_pre, w_in_g, st_post, dconv)
    return grad_x, ht, dz, dw_out, dw_out_bf, small


def _coords():
    return lax.axis_index("x"), lax.axis_index("y"), lax.axis_index("c")


def _peer(k):
    x, y, c = _coords()
    px = 1 - x if k & 4 else x
    py = 1 - y if k & 2 else y
    pc = 1 - c if k & 1 else c
    return (px, py, pc), 4 * px + 2 * py + pc


HBM_SPEC = pl.BlockSpec(memory_space=pltpu.HBM)
VMEM_SPEC = pl.BlockSpec(memory_space=pltpu.VMEM)


def _ag_weights(w_in, w_out, conv_w):
    def body(win_ref, wout_ref, cw_ref, gin_ref, gout_ref, gcw_ref, win_bf, wout_bf, cw_pad, send_sems, recv_sems,
             local_sems):
        x, y, c = _coords()
        me, sibling = (x, y, c), (x, y, 1 - c)
        flip = lambda v, yes: v + yes - 2 * v * yes
        x_nbr, y_nbr, diagonal = (1 - x, y, c), (x, 1 - y, c), (1 - x, 1 - y, c)
        relay_from = (flip(x, 1 - c), flip(y, c), c)
        relay_to = (flip(x, c), flip(y, 1 - c), c)
        slab = lambda px, py, pc: 4 * px + 2 * py + pc
        win_bf[...] = win_ref[...].astype(BF16)
        wout_bf[...] = wout_ref[...].astype(BF16)
        cw_pad[...] = jnp.zeros_like(cw_pad)
        cw_pad[0:3, 0:CONV_W // N_DEV] = cw_ref[...]
        mine = [win_bf, wout_bf, cw_pad]
        gathered = [gin_ref, gout_ref, gcw_ref]

        def copies(k, block, to, own=False):
            return [pltpu.make_async_remote_copy(src_ref=mine[a] if own else gathered[a].at[slab(*block)],
                                                 dst_ref=gathered[a].at[slab(*block)], send_sem=send_sems.at[k, a],
                                                 recv_sem=recv_sems.at[k, a], device_id=to, device_id_type=MESH)
                    for a in range(3)]

        local = [pltpu.make_async_copy(mine[a], gathered[a].at[slab(*me)], local_sems.at[a]) for a in range(3)]
        for cp in local:
            cp.start()
        started = copies(0, me, sibling, own=True) + copies(1, me, x_nbr, own=True) + copies(2, me, y_nbr, own=True)
        for cp in started:
            cp.start()
        for cp in copies(1, x_nbr, me) + copies(2, y_nbr, me):
            cp.wait_recv()
        onward = copies(3, relay_from, relay_to) + copies(4, x_nbr, sibling) + copies(5, y_nbr, sibling)
        for cp in onward:
            cp.start()
        for cp in copies(3, diagonal, me):
            cp.wait_recv()
        last = copies(6, diagonal, sibling)
        for cp in last:
            cp.start()
        for cp in copies(0, sibling, me):
            cp.wait_recv()
        for k, origin in ((4, (1 - x, y, 1 - c)), (5, (x, 1 - y, 1 - c)), (6, (1 - x, 1 - y, 1 - c))):
            for cp in copies(k, origin, me):
                cp.wait_recv()
        for cp in started + onward + last:
            cp.wait_send()
        for cp in local:
            cp.wait()

    return pl.pallas_call(
        body, name="ag_weights",
        in_specs=[VMEM_SPEC, VMEM_SPEC, VMEM_SPEC], out_specs=[HBM_SPEC, HBM_SPEC, HBM_SPEC],
        out_shape=[jax.ShapeDtypeStruct((N_DEV, D_MODEL, SHARD_IN), BF16),
                   jax.ShapeDtypeStruct((N_DEV, SHARD_OUT, D_MODEL), BF16),
                   jax.ShapeDtypeStruct((N_DEV, 8, BLK), F32)],
        scratch_shapes=[pltpu.VMEM((D_MODEL, SHARD_IN), BF16), pltpu.VMEM((SHARD_OUT, D_MODEL), BF16),
                        pltpu.VMEM((8, BLK), F32), pltpu.SemaphoreType.DMA((N_DEV - 1, 3)),
                        pltpu.SemaphoreType.DMA((N_DEV - 1, 3)), pltpu.SemaphoreType.DMA((3,))],
        compiler_params=pltpu.CompilerParams(vmem_limit_bytes=VMEM_LIMIT),
    )(w_in, w_out, conv_w)


def _dw_in_rs(ht, dz, dw_out, small):
    seq = dz.shape[0]

    def body(cols_ref, ht_ref, dz_ref, dout_ref, sm_ref, own_ref, rin_ref, rout_ref, rsm_ref, to_sibling, landed,
             to_chip, zero_buf, d2d_send, d2d_recv, ici_send, ici_recv, side_send, side_recv, local_sems):
        del cols_ref
        step = pl.program_id(0)
        x, y, c = _coords()
        me = 4 * x + 2 * y + c
        sibling = (x, y, 1 - c)
        chips = [(1 - x, y), (x, 1 - y), (1 - x, 1 - y)]

        def d2d(n):
            return pltpu.make_async_remote_copy(src_ref=to_sibling.at[n], dst_ref=landed.at[n], send_sem=d2d_send.at[n],
                                                recv_sem=d2d_recv.at[n], device_id=sibling, device_id_type=MESH)

        def ici(n):
            return pltpu.make_async_remote_copy(src_ref=to_chip.at[n], dst_ref=rin_ref.at[n], send_sem=ici_send.at[n],
                                                recv_sem=ici_recv.at[n], device_id=(*chips[n], c), device_id_type=MESH)

        def side(k, mine):
            peer, peer_idx = _peer(k)
            src_slab, dst_slab = (peer_idx, me) if mine else (me, peer_idx)
            pairs = [(dout_ref.at[src_slab], rout_ref.at[dst_slab]), (sm_ref, rsm_ref.at[dst_slab])]
            return [pltpu.make_async_remote_copy(src_ref=src, dst_ref=dst, send_sem=side_send.at[k - 1, a],
                                                 recv_sem=side_recv.at[k - 1, a], device_id=peer, device_id_type=MESH)
                    for a, (src, dst) in enumerate(pairs)]

        local = [pltpu.make_async_copy(zero_buf, rout_ref.at[me], local_sems.at[0]),
                 pltpu.make_async_copy(sm_ref, rsm_ref.at[me], local_sems.at[1])]

        @pl.when(step == 0)
        def _():
            zero_buf[...] = jnp.zeros_like(zero_buf)
            for cp in local:
                cp.start()
            for k in range(1, N_DEV):
                for cp in side(k, mine=True):
                    cp.start()

        dw = jnp.dot(ht_ref[...], dz_ref[...], preferred_element_type=F32)
        for n, at in zip(range(4), (0, 1, 2, N_DEV - 2)):
            @pl.when(step == at)
            def _(n=n):
                to_sibling[n] = dw.astype(BF16)
                d2d(n).start()

        for n in range(3):
            @pl.when(step == 3 + n)
            def _(n=n):
                d2d(n).wait_recv()
                to_chip[n] = (dw + landed[n].astype(F32)).astype(BF16)
                ici(n).start()

        @pl.when(step == N_DEV - 1)
        def _():
            d2d(3).wait_recv()
            own_ref[...] = dw + landed[3].astype(F32)
            for n in range(3):
                ici(n).wait_recv()
            for k in range(1, N_DEV):
                for cp in side(k, mine=False):
                    cp.wait_recv()
            for n in range(4):
                d2d(n).wait_send()
            for n in range(3):
                ici(n).wait_send()
            for k in range(1, N_DEV):
                for cp in side(k, mine=True):
                    cp.wait_send()
            for cp in local:
                cp.wait()

    x, y, c = _coords()
    others = [(1 - x, y), (x, 1 - y), (1 - x, 1 - y)]
    order = [(*chip, 1 - c) for chip in others] + [(*chip, c) for chip in others] + [(x, y, 1 - c), (x, y, c)]
    cols = jnp.stack([4 * px + 2 * py + pc for px, py, pc in order]).astype(jnp.int32)
    slab = (D_MODEL, SHARD_IN)
    grid_spec = pltpu.PrefetchScalarGridSpec(
        num_scalar_prefetch=1, grid=(N_DEV,),
        in_specs=[pl.BlockSpec((D_MODEL, seq), lambda s, cols: (0, 0), pipeline_mode=pl.Buffered(1)),
                  pl.BlockSpec((seq, SHARD_IN), lambda s, cols: (0, cols[s])), HBM_SPEC, HBM_SPEC],
        out_specs=[pl.BlockSpec(slab, lambda s, cols: (0, 0)), HBM_SPEC, HBM_SPEC, HBM_SPEC],
        scratch_shapes=[pltpu.VMEM((4, *slab), BF16), pltpu.VMEM((4, *slab), BF16), pltpu.VMEM((3, *slab), BF16),
                        pltpu.VMEM((SHARD_OUT, D_MODEL), BF16),
                        pltpu.SemaphoreType.DMA((4,)), pltpu.SemaphoreType.DMA((4,)),
                        pltpu.SemaphoreType.DMA((3,)), pltpu.SemaphoreType.DMA((3,)),
                        pltpu.SemaphoreType.DMA((N_DEV - 1, 2)), pltpu.SemaphoreType.DMA((N_DEV - 1, 2)),
                        pltpu.SemaphoreType.DMA((2,))])
    return pl.pallas_call(
        body, name="dw_in_rs", grid_spec=grid_spec,
        out_shape=[jax.ShapeDtypeStruct(slab, F32),
                   jax.ShapeDtypeStruct((3, *slab), BF16),
                   jax.ShapeDtypeStruct((N_DEV, SHARD_OUT, D_MODEL), BF16),
                   jax.ShapeDtypeStruct((N_DEV, 8, D_MODEL), F32)],
        compiler_params=_params(1),
    )(cols, ht, dz, dw_out, small)


def _adamw_math(w, g, m, v):
    m = ADAM_B1 * m + (1.0 - ADAM_B1) * g
    v = ADAM_B2 * v + (1.0 - ADAM_B2) * (g * g)
    m_hat = m / (1.0 - ADAM_B1 ** ADAM_STEP)
    v_hat = v / (1.0 - ADAM_B2 ** ADAM_STEP)
    delta = -ADAM_LR * (m_hat / (jnp.sqrt(v_hat) + ADAM_EPS) + ADAM_WD * w)
    return delta, m, v


def _sum_slabs(ref, first=None):
    total = ref[0].astype(F32) if first is None else first + ref[0].astype(F32)
    for s in range(1, ref.shape[0]):
        total = total + ref[s].astype(F32)
    return total


def _adamw_slabs(parts, own, own_slab, w, m, v, name, tr):
    rows, cols = w.shape
    tile = pl.BlockSpec((tr, cols), lambda i, s: (i, 0))
    own_spec = tile if own_slab is None else pl.BlockSpec((1, tr, cols), lambda i, s: (s[0], i, 0))

    def body(s_ref, p_ref, own_ref, w_ref, m_ref, v_ref, g_ref, d_ref, nm_ref, nv_ref):
        del s_ref
        g = _sum_slabs(p_ref, own_ref[...].reshape(tr, cols))
        g_ref[...] = g
        d_ref[...], nm_ref[...], nv_ref[...] = _adamw_math(w_ref[...], g, m_ref[...], v_ref[...])

    slab = jnp.zeros((1,), jnp.int32) if own_slab is None else own_slab.reshape(1).astype(jnp.int32)
    grid_spec = pltpu.PrefetchScalarGridSpec(
        num_scalar_prefetch=1, grid=(rows // tr,),
        in_specs=[pl.BlockSpec((parts.shape[0], tr, cols), lambda i, s: (0, i, 0)), own_spec, tile, tile, tile],
        out_specs=[tile] * 4)
    return pl.pallas_call(
        body, name=name, grid_spec=grid_spec,
        out_shape=[jax.ShapeDtypeStruct((rows, cols), F32)] * 4,
        compiler_params=_params(1),
    )(slab, parts, own, w, m, v)


def _adamw_small(parts, me, pre, post, conv):
    n_conv = CONV_W // N_DEV

    def body(me_ref, p_ref, *refs):
        ins, (loss_ref, *outs) = refs[:9], refs[9:]
        sums = _sum_slabs(p_ref)
        loss_ref[...] = sums[2:3, 0:1]
        mine = pltpu.roll(sums[:, 0:CONV_W], (CONV_W - me_ref[0] * n_conv) % CONV_W, 1)[3:6, 0:n_conv]
        for n, g in enumerate((sums[0:1], sums[1:2], mine)):
            w_ref, m_ref, v_ref = ins[3 * n:3 * n + 3]
            outs[4 * n][...] = g
            for out, val in zip(outs[4 * n + 1:4 * n + 4], _adamw_math(w_ref[...], g, m_ref[...], v_ref[...])):
                out[...] = val

    row = jax.ShapeDtypeStruct((1, D_MODEL), F32)
    small = jax.ShapeDtypeStruct((3, n_conv), F32)
    return pl.pallas_call(
        body, name="adamw_small",
        in_specs=[pl.BlockSpec(memory_space=pltpu.SMEM)] + [VMEM_SPEC] * 10,
        out_shape=[jax.ShapeDtypeStruct((1, 1), F32)] + [row] * 8 + [small] * 4,
    )(me.reshape(1).astype(jnp.int32), parts, *pre, *post, *conv)


def kernel(x, norm_pre_g, w_in, conv_w, w_out, norm_post_g, loss_target, m_norm_pre_g, m_w_in, m_conv_w, m_w_out,
           m_norm_post_g, v_norm_pre_g, v_w_in, v_conv_w, v_w_out, v_norm_post_g):
    n_conv = CONV_W // N_DEV
    w_in_g, w_out_g, conv_g = _ag_weights(w_in, w_out, conv_w)
    conv_full = conv_g[:, 0:3, 0:n_conv].transpose(1, 0, 2).reshape(3, CONV_W)
    grad_x, ht, dz, dw_out, dw_out_bf, small = _local_step(x[0], loss_target[0], norm_pre_g, norm_post_g, w_in_g,
                                                           w_out_g.reshape(D_MODEL, D_MODEL), conv_full)
    own_in, r_in, r_out, r_small = _dw_in_rs(ht, dz, dw_out_bf.reshape(N_DEV, SHARD_OUT, D_MODEL), small)
    me = 4 * lax.axis_index("x") + 2 * lax.axis_index("y") + lax.axis_index("c")
    g_in, d_in, nm_in, nv_in = _adamw_slabs(r_in, own_in, None, w_in, m_w_in, v_w_in, "adamw_in", 256)
    g_out, d_out, nm_out, nv_out = _adamw_slabs(r_out, dw_out.reshape(N_DEV, SHARD_OUT, D_MODEL), me, w_out, m_w_out,
                                                v_w_out, "adamw_out", SHARD_OUT)
    vec = lambda a: a.reshape(1, D_MODEL)
    (loss, g_pre, d_pre, nm_pre, nv_pre, g_post, d_post, nm_post, nv_post, g_conv, d_conv, nm_conv,
     nv_conv) = _adamw_small(r_small, me, [vec(a) for a in (norm_pre_g, m_norm_pre_g, v_norm_pre_g)],
                             [vec(a) for a in (norm_post_g, m_norm_post_g, v_norm_post_g)],
                             (conv_w, m_conv_w, v_conv_w))
    flat = lambda a: a.reshape(D_MODEL)
    return (loss.reshape(()), grad_x[None], flat(g_pre), g_in, g_conv, g_out, flat(g_post),
            flat(d_pre), d_in, d_conv, d_out, flat(d_post),
            flat(nm_pre), nm_in, nm_conv, nm_out, flat(nm_post),
            flat(nv_pre), nv_in, nv_conv, nv_out, flat(nv_post))
```

```python
import functools

import jax
import jax.numpy as jnp
from jax import lax
from jax.experimental import pallas as pl
from jax.experimental.pallas import tpu as pltpu

F32 = jnp.float32
BF16 = jnp.bfloat16

D_MODEL = 1024
HEAD_DIM = 64
ATTN_W = 768
CONV_W = 256
IN_W = 4096
REST_W = IN_W - 3 * ATTN_W
BLK = 128
N_DEV = 8
SHARD_IN = IN_W // N_DEV
SHARD_OUT = D_MODEL // N_DEV
DILATIONS = (1, 4, 16)
ROPE_THETA = 10000.0
NORM_EPS = 1e-6
NEG = -1e30

ADAM_LR = 0.001
ADAM_B1 = 0.9
ADAM_B2 = 0.999
ADAM_EPS = 1e-08
ADAM_WD = 0.01
ADAM_STEP = 10

VMEM_LIMIT = 56 * 1024 * 1024
MESH = pl.DeviceIdType.MESH


def _params(n_grid):
    return pltpu.CompilerParams(dimension_semantics=("arbitrary",) * n_grid, vmem_limit_bytes=VMEM_LIMIT)


def _resident(shape):
    zeros = (0,) * len(shape)
    return pl.BlockSpec(shape, lambda *_: zeros, pipeline_mode=pl.Buffered(1))


def _sigmoid(a):
    return 1.0 / (1.0 + jnp.exp(-a))


def _swap_halves(t, first_half):
    return jnp.where(first_half, pltpu.roll(t, BLK - 32, 1), pltpu.roll(t, 32, 1))


def _rope_tables(seq, tm):
    half = HEAD_DIM // 2
    inv_freq = ROPE_THETA ** (-jnp.arange(half, dtype=F32) * 2.0 / HEAD_DIM)
    freq = jnp.concatenate([inv_freq] * 4)
    sign = jnp.concatenate([-jnp.ones(half, F32), jnp.ones(half, F32)] * 2)
    starts = (jnp.arange(seq // tm) * tm).astype(F32)[:, None] * freq[None, :]
    rows = jnp.arange(tm).astype(F32)[:, None] * freq[None, :]
    slab = lambda a: jnp.broadcast_to(a[:, None, :], (seq // tm, 8, BLK))
    return slab(jnp.cos(starts)), slab(jnp.sin(starts) * sign), jnp.cos(rows), jnp.sin(rows) * sign


def _rope_specs(tm):
    return [pl.BlockSpec((1, 8, BLK), lambda i: (i, 0, 0))] * 2 + [_resident((tm, BLK))] * 2


def _tile_rope(cos_start, sin_start, cos_row, sin_row):
    ca, sa, cb, sb = cos_start[0, 0:1, :], sin_start[0, 0:1, :], cos_row[...], sin_row[...]
    return ca * cb - sa * sb, sa * cb + ca * sb


N_CHUNK = ATTN_W // BLK


def _lanes(r, c):
    return slice(r * ATTN_W + c * BLK, r * ATTN_W + (c + 1) * BLK)


def _to_residues(src, chunk0, dst_refs, tmp, rows, dtype):
    assert DILATIONS == (1, 4, 16)
    dst1, dst4, dst16 = dst_refs
    n4, n16 = rows // 4, rows // 16
    for c in range(N_CHUNK):
        dst1[:, _lanes(0, c)] = src[chunk0 + c].astype(dtype)
        for r1 in range(4):
            tmp[c, r1 * n4:(r1 + 1) * n4, :] = src[chunk0 + c, pl.ds(r1, n4, stride=4), :]
        for r1 in range(4):
            dst4[:, _lanes(r1, c)] = tmp[c, r1 * n4:(r1 + 1) * n4, :].astype(dtype)
            for r2 in range(4):
                dst16[:, _lanes(4 * r2 + r1, c)] = tmp[c, pl.ds(r1 * n4 + r2, n16, stride=4), :].astype(dtype)


def _from_residue(src_ref, dst, dil, rows, accumulate, tmp=None):
    n4, n16 = rows // 4, rows // 16

    def put(where, piece):
        if accumulate:
            dst[where] += piece
        else:
            dst[where] = piece

    for c in range(N_CHUNK):
        if dil == 1:
            put((c,), src_ref[:, _lanes(0, c)].astype(F32))
            continue
        for r1 in range(4):
            if dil == 4:
                piece = src_ref[:, _lanes(r1, c)].astype(F32)
            else:
                for r2 in range(4):
                    tmp[c, pl.ds(r1 * n4 + r2, n16, stride=4), :] = src_ref[:, _lanes(4 * r2 + r1, c)].astype(F32)
                piece = tmp[c, r1 * n4:(r1 + 1) * n4, :]
            put((c, pl.ds(r1, n4, stride=4), slice(None)), piece)


def _residue_spec(tm, dil):
    return pl.BlockSpec((tm // dil, dil * ATTN_W), lambda i: (i, 0))


def _residue_shape(seq, dil, dtype):
    return jax.ShapeDtypeStruct((seq // dil, dil * ATTN_W), dtype)


def _fwd_in(x, g_pre, w_in_g, tm=512):
    seq = x.shape[0]
    n_dil = len(DILATIONS)

    def body(x_ref, g_ref, w_ref, ca_ref, sa_ref, cb_ref, sb_ref, *rest):
        qkv_refs, (zr_ref, ht_ref, qkv_scr, tmp) = rest[:3 * n_dil], rest[3 * n_dil:]
        xv = x_ref[...]
        r = lax.rsqrt(jnp.mean(xv * xv, axis=-1, keepdims=True) + NORM_EPS)
        hf = (xv * r) * g_ref[...]
        h = hf.astype(BF16)
        ht_ref[...] = h.T
        cos, sin = _tile_rope(ca_ref, sa_ref, cb_ref, sb_ref)
        first_half = (lax.broadcasted_iota(jnp.int32, (tm, BLK), 1) & 32) == 0

        def rope(t):
            return t * cos + _swap_halves(t, first_half) * sin

        def project(j):
            return jnp.dot(h, w_ref[j], preferred_element_type=F32)

        def place(j, zj):
            for n in range(SHARD_IN // BLK):
                chunk, t = j * (SHARD_IN // BLK) + n, zj[:, n * BLK:(n + 1) * BLK]
                if chunk < N_CHUNK:
                    qkv_scr[chunk] = rope(t) * HEAD_DIM ** -0.5
                elif chunk < 2 * N_CHUNK:
                    qkv_scr[chunk] = rope(t)
                elif chunk < 3 * N_CHUNK:
                    qkv_scr[chunk] = t
                else:
                    zr_ref[:, (chunk - 3 * N_CHUNK) * BLK:(chunk - 3 * N_CHUNK + 1) * BLK] = t.astype(BF16)

        ahead = project(0)
        for j in range(N_DEV):
            zj = ahead
            if j + 1 < N_DEV:
                ahead = project(j + 1)
            place(j, zj)
            for a in range(3):
                if (a + 1) * N_CHUNK - 1 in range(j * (SHARD_IN // BLK), (j + 1) * (SHARD_IN // BLK)):
                    _to_residues(qkv_scr, a * N_CHUNK, [qkv_refs[3 * n + a] for n in range(n_dil)], tmp, tm, BF16)

    row = lambda w: pl.BlockSpec((tm, w), lambda i: (i, 0))
    outs = pl.pallas_call(
        body, name="fwd_in", grid=(seq // tm,),
        in_specs=[row(D_MODEL), _resident((1, D_MODEL)), _resident((N_DEV, D_MODEL, SHARD_IN))] + _rope_specs(tm),
        out_specs=[_residue_spec(tm, dil) for dil in DILATIONS for _ in range(3)]
        + [row(REST_W), pl.BlockSpec((D_MODEL, tm), lambda i: (0, i))],
        out_shape=[_residue_shape(seq, dil, BF16) for dil in DILATIONS for _ in range(3)]
        + [jax.ShapeDtypeStruct((seq, REST_W), BF16), jax.ShapeDtypeStruct((D_MODEL, seq), BF16)],
        scratch_shapes=[pltpu.VMEM((3 * N_CHUNK, tm, BLK), F32), pltpu.VMEM((N_CHUNK, tm, BLK), F32)],
        compiler_params=_params(1),
    )(x, g_pre.reshape(1, D_MODEL), w_in_g, *_rope_tables(seq, tm))
    qkv = [tuple(outs[3 * n:3 * n + 3]) for n in range(n_dil)]
    return qkv, outs[3 * n_dil], outs[3 * n_dil + 1]


def _band_bias(first_block):
    kj = lax.broadcasted_iota(jnp.int32, (2 * BLK, BLK), 0)
    qi = lax.broadcasted_iota(jnp.int32, (2 * BLK, BLK), 1)
    valid = (kj >= qi) & (kj <= qi + BLK)
    bias = jnp.where(valid, 0.0, NEG).astype(BF16)
    bias_first = jnp.where(valid & (kj >= BLK), 0.0, NEG).astype(BF16)
    onehot = ((kj & (BLK - 1)) == qi).astype(F32).astype(BF16)
    return onehot, bias, jnp.where(first_block, bias_first, bias)


def _stack_heads(t):
    keep0 = (lax.broadcasted_iota(jnp.int32, t.shape, 1) < HEAD_DIM).astype(F32).astype(BF16)
    return jnp.concatenate([t * keep0, t * (1 - keep0)], axis=0)


def _unstack_heads(t2, head0):
    return jnp.where(head0, t2[:BLK], t2[BLK:])


def _rows_per_head(a, head0):
    b = pltpu.roll(a, HEAD_DIM, 1)
    rows = jnp.concatenate([jnp.where(head0, a, b), jnp.where(head0, b, a)], axis=0)
    return jnp.concatenate([rows, rows], axis=1)


BLOCKS_PER_STEP = 32


def _attn_specs(length, dil, max_cols=8, units=BLOCKS_PER_STEP):
    n_blocks = length // BLK
    tb = min(units, n_blocks)
    nc = max(n for n in range(1, min(units // tb, max_cols) + 1) if (dil * N_CHUNK) % n == 0)
    assert n_blocks % tb == 0
    tile = pl.BlockSpec((tb * BLK, nc * BLK), lambda c, t: (t, c))
    prev = pl.BlockSpec((BLK, nc * BLK), lambda c, t: (jnp.maximum(t * tb - 1, 0), c))
    grid = (dil * N_CHUNK // nc, n_blocks // tb)
    return tb, nc, tile, prev, grid


def _window(prev_ref, cur_ref, j, cols):
    if j == 0:
        return jnp.concatenate([prev_ref[:, cols], cur_ref[0:BLK, cols]], axis=0)
    return cur_ref[(j - 1) * BLK:(j + 1) * BLK, cols]


def _attn_fwd(q, k, v, dil):
    length = q.shape[0]
    tb, nc, tile, prev, grid = _attn_specs(length, dil, units=2 * BLOCKS_PER_STEP)

    def body(q_ref, kc_ref, kp_ref, vc_ref, vp_ref, o_ref, lse_ref):
        head0 = lax.broadcasted_iota(jnp.int32, (BLK, BLK), 1) < HEAD_DIM
        onehot, bias, bias_start = _band_bias(pl.program_id(1) == 0)
        ones = jnp.ones((2 * BLK, BLK), BF16)
        def scores(c, j):
            rows, cols = slice(j * BLK, (j + 1) * BLK), slice(c * BLK, (c + 1) * BLK)
            q2 = jnp.concatenate([_stack_heads(q_ref[rows, cols]), onehot], axis=1)
            kk = jnp.concatenate([_window(kp_ref, kc_ref, j, cols), bias_start if j == 0 else bias], axis=1)
            return (lax.dot_general(q2, kk, (((1,), (1,)), ((), ())), preferred_element_type=F32),)

        def probabilities(c, j, s):
            m = jnp.max(s, axis=1, keepdims=True)
            return m, jnp.exp(s - m).astype(BF16)

        def outputs(c, j, m, p):
            rows, cols = slice(j * BLK, (j + 1) * BLK), slice(c * BLK, (c + 1) * BLK)
            vv = jnp.concatenate([_window(vp_ref, vc_ref, j, cols), ones], axis=1)
            pv = jnp.dot(p, vv, preferred_element_type=F32)
            den = pv[:, BLK:]
            o_ref[rows, cols] = _unstack_heads(pv[:, :BLK] / den, head0).astype(BF16)
            lse_ref[rows, cols] = _unstack_heads(m + jnp.log(den), head0)

        units = [(c, j) for c in range(nc) for j in range(tb)]
        stage1, stage2 = {}, {}
        for n in range(len(units) + 2):
            if n < len(units):
                stage1[n] = scores(*units[n])
            if 0 <= n - 1 < len(units):
                stage2[n - 1] = probabilities(*units[n - 1], *stage1.pop(n - 1))
            if 0 <= n - 2 < len(units):
                outputs(*units[n - 2], *stage2.pop(n - 2))

    return pl.pallas_call(
        body, name=f"attn_fwd_d{dil}", grid=grid,
        in_specs=[tile, tile, prev, tile, prev], out_specs=[tile, tile],
        out_shape=[jax.ShapeDtypeStruct(q.shape, BF16), jax.ShapeDtypeStruct(q.shape, F32)],
        compiler_params=_params(2),
    )(q, k, k, v, v)


def _attn_bwd(q, k, v, do, lse, delta, dil):
    length = q.shape[0]
    tb, nc, tile, prev, grid = _attn_specs(length, dil, max_cols=4)
    whole = pl.BlockSpec((length, nc * BLK), lambda c, t: (0, c))

    def body(q_ref, do_ref, lse_ref, dl_ref, kc_ref, kp_ref, vc_ref, vp_ref, dq_ref, dk_ref, dv_ref):
        t = pl.program_id(1)
        head0 = lax.broadcasted_iota(jnp.int32, (BLK, BLK), 1) < HEAD_DIM
        onehot, bias, bias_start = _band_bias(t == 0)

        def scores(c, j):
            rows, cols = slice(j * BLK, (j + 1) * BLK), slice(c * BLK, (c + 1) * BLK)
            q2 = _stack_heads(q_ref[rows, cols])
            do2 = _stack_heads(do_ref[rows, cols])
            kk = _window(kp_ref, kc_ref, j, cols)
            s = lax.dot_general(jnp.concatenate([q2, onehot], axis=1),
                                jnp.concatenate([kk, bias_start if j == 0 else bias], axis=1),
                                (((1,), (1,)), ((), ())), preferred_element_type=F32)
            dp = lax.dot_general(do2, _window(vp_ref, vc_ref, j, cols), (((1,), (1,)), ((), ())),
                                 preferred_element_type=F32)
            return q2, do2, kk, s, dp

        def probabilities(c, j, q2, do2, kk, s, dp):
            rows, cols = slice(j * BLK, (j + 1) * BLK), slice(c * BLK, (c + 1) * BLK)
            p = jnp.exp(s - _rows_per_head(lse_ref[rows, cols], head0))
            ds = (p * (dp - _rows_per_head(dl_ref[rows, cols].astype(F32), head0))).astype(BF16)
            return q2, do2, kk, p.astype(BF16), ds

        def gradients(c, j, q2, do2, kk, p, ds):
            rows, cols = slice(j * BLK, (j + 1) * BLK), slice(c * BLK, (c + 1) * BLK)
            dq2 = jnp.dot(ds, kk, preferred_element_type=F32)
            dq_ref[rows, cols] = (_unstack_heads(dq2, head0) * HEAD_DIM ** -0.5).astype(BF16)
            dk2 = lax.dot_general(ds, q2, (((0,), (0,)), ((), ())), preferred_element_type=F32)
            dv2 = lax.dot_general(p, do2, (((0,), (0,)), ((), ())), preferred_element_type=F32)
            own = pl.ds(pl.multiple_of((t * tb + j) * BLK, BLK), BLK)
            dk_ref[own, cols] = dk2[BLK:].astype(BF16)
            dv_ref[own, cols] = dv2[BLK:].astype(BF16)

            def add_to_previous():
                before = pl.ds(pl.multiple_of((t * tb + j - 1) * BLK, BLK), BLK)
                dk_ref[before, cols] = (dk_ref[before, cols].astype(F32) + dk2[:BLK]).astype(BF16)
                dv_ref[before, cols] = (dv_ref[before, cols].astype(F32) + dv2[:BLK]).astype(BF16)

            if j > 0:
                add_to_previous()
            elif grid[1] > 1:
                pl.when(t > 0)(add_to_previous)

        units = [(c, j) for c in range(nc) for j in range(tb)]
        stage1 = {0: scores(*units[0])}
        for n in range(len(units)):
            stage2 = probabilities(*units[n], *stage1.pop(n))
            if n + 1 < len(units):
                stage1[n + 1] = scores(*units[n + 1])
            gradients(*units[n], *stage2)

    return pl.pallas_call(
        body, name=f"attn_bwd_d{dil}", grid=grid,
        in_specs=[tile, tile, tile, tile, tile, prev, tile, prev], out_specs=[tile, whole, whole],
        out_shape=[jax.ShapeDtypeStruct(q.shape, BF16)] * 3,
        compiler_params=_params(2),
    )(q, do, lse, delta, k, k, v, v)


HALO = 16


def _halo_specs(tm, seq):
    before = lambda w: pl.BlockSpec((HALO, w), lambda i: (jnp.maximum(i * (tm // HALO) - 1, 0), 0))
    after = lambda w: pl.BlockSpec((HALO, w), lambda i: (jnp.minimum((i + 1) * (tm // HALO), seq // HALO - 1), 0))
    return before, after


def _conv_taps(u, before, tm):
    row = lax.broadcasted_iota(jnp.int32, u.shape, 0)
    last, last2 = before[HALO - 1:HALO, :], before[HALO - 2:HALO - 1, :]
    u1 = jnp.where(row == 0, last, pltpu.roll(u, 1, 0))
    u2 = jnp.where(row == 0, last2, jnp.where(row == 1, last, pltpu.roll(u, 2, 0)))
    return u1, u2


def _attn_combine(o_parts, lse_parts, zr, conv_w, tm=256):
    seq = zr.shape[0]
    a0, h0, b0, c0, g0 = 0, ATTN_W, ATTN_W + CONV_W, ATTN_W + 2 * CONV_W, ATTN_W + 3 * CONV_W

    def body(o1, o2, o3, l1, l2, l3, zr_ref, zp_ref, w_ref, mixed_ref, o_ref, lse1, lse2, lse3, *scr):
        i = pl.program_id(0)
        for src, dst, dil in zip((o2, o3, l2, l3), scr[:4], DILATIONS[1:] * 2):
            _from_residue(src, dst, dil, tm, accumulate=False, tmp=scr[5])
        for c in range(N_CHUNK):
            cols = slice(c * BLK, (c + 1) * BLK)
            la, lb, lc = l1[:, cols], scr[2][c], scr[3][c]
            top = jnp.maximum(jnp.maximum(la, lb), lc)
            ea, eb, ec = jnp.exp(la - top), jnp.exp(lb - top), jnp.exp(lc - top)
            den = ea + eb + ec
            inv = 1.0 / den
            o = (ea * inv) * o1[:, cols].astype(F32) + (eb * inv) * scr[0][c] + (ec * inv) * scr[1][c]
            o_ref[:, cols] = o.astype(BF16)
            scr[4][c] = top + jnp.log(den)
            ga = zr_ref[:, cols].astype(F32)
            mixed_ref[:, cols] = (o * (ga * _sigmoid(ga))).astype(BF16)
        _to_residues(scr[4], 0, (lse1, lse2, lse3), scr[5], tm, F32)
        part = lambda ref, lo, hi: ref[:, lo:hi].astype(F32)
        u = part(zr_ref, c0, g0) * part(zr_ref, h0, b0)
        before = jnp.where(i > 0, part(zp_ref, c0, g0) * part(zp_ref, h0, b0), 0.0)
        u1, u2 = _conv_taps(u, before, tm)
        y = u2 * w_ref[0:1, :] + u1 * w_ref[1:2, :] + u * w_ref[2:3, :]
        gc = part(zr_ref, g0, REST_W)
        mixed_ref[:, ATTN_W:] = ((part(zr_ref, b0, c0) * y) * (gc * _sigmoid(gc))).astype(BF16)

    row = lambda w: pl.BlockSpec((tm, w), lambda i: (i, 0))
    before, _ = _halo_specs(tm, seq)
    views = [_residue_spec(tm, dil) for dil in DILATIONS]
    outs = pl.pallas_call(
        body, name="attn_combine", grid=(seq // tm,),
        in_specs=views * 2 + [row(REST_W), before(REST_W), _resident((3, CONV_W))],
        out_specs=[row(D_MODEL), row(ATTN_W)] + views,
        out_shape=[jax.ShapeDtypeStruct((seq, D_MODEL), BF16), jax.ShapeDtypeStruct((seq, ATTN_W), BF16)]
        + [_residue_shape(seq, dil, F32) for dil in DILATIONS],
        scratch_shapes=[pltpu.VMEM((N_CHUNK, tm, BLK), F32)] * 6,
        compiler_params=_params(1),
    )(*o_parts, *lse_parts, zr, zr, conv_w)
    return outs[0], outs[1], outs[2:]


def _out_loss_bwd(mixed, w_out_g, x, target, g_post, tm=512, n_parts=2):
    seq = x.shape[0]

    def body(mx_ref, w_ref, x_ref, t_ref, g_ref, dout_ref, dmx_ref, dw_ref, dwb_ref, st_ref):
        i = pl.program_id(0)
        g = g_ref[...]
        parts = [slice(n * (tm // n_parts), (n + 1) * (tm // n_parts)) for n in range(n_parts)]

        def project(rows):
            return jnp.dot(mx_ref[rows, :], w_ref[...], preferred_element_type=F32)

        def head(rows, y):
            r = lax.rsqrt(jnp.mean(y * y, axis=-1, keepdims=True) + NORM_EPS)
            yhat = y * r
            err = (x_ref[rows, :] + yhat * g) - t_ref[rows, :]
            dn = err * (1.0 / D_MODEL)
            dout_ref[rows, :] = dn
            tg = dn * g
            dy = (r * (tg - yhat * jnp.mean(tg * yhat, axis=-1, keepdims=True))).astype(BF16)
            dmx_ref[rows, :] = lax.dot_general(dy, w_ref[...], (((1,), (1,)), ((), ())),
                                               preferred_element_type=F32).astype(BF16)
            return dy, jnp.sum(dn * yhat, axis=0, keepdims=True), jnp.sum(err * err)

        ahead, done = project(parts[0]), []
        for n, rows in enumerate(parts):
            y = ahead
            if n + 1 < n_parts:
                ahead = project(parts[n + 1])
            done.append(head(rows, y))
        dy = jnp.concatenate([d[0] for d in done], axis=0)
        dw = lax.dot_general(mx_ref[...], dy, (((0,), (0,)), ((), ())), preferred_element_type=F32)
        gsum = functools.reduce(lambda a, b: a + b, [d[1] for d in done])
        lsum = jnp.broadcast_to(0.5 / D_MODEL * functools.reduce(lambda a, b: a + b, [d[2] for d in done]),
                                (1, D_MODEL))

        @pl.when(i == 0)
        def _():
            dw_ref[...] = dw
            st_ref[...] = jnp.zeros_like(st_ref)
            st_ref[0:1, :] = gsum
            st_ref[1:2, :] = lsum

        @pl.when(i > 0)
        def _():
            dw_ref[...] += dw
            st_ref[0:1, :] += gsum
            st_ref[1:2, :] += lsum

        @pl.when(i == seq // tm - 1)
        def _():
            dwb_ref[...] = dw_ref[...].astype(BF16)

    row = lambda w: pl.BlockSpec((tm, w), lambda i: (i, 0))
    whole = pl.BlockSpec((D_MODEL, D_MODEL), lambda i: (0, 0))
    return pl.pallas_call(
        body, name="out_loss_bwd", grid=(seq // tm,),
        in_specs=[row(D_MODEL), _resident((D_MODEL, D_MODEL)), row(D_MODEL), row(D_MODEL), _resident((1, D_MODEL))],
        out_specs=[row(D_MODEL), row(D_MODEL), whole, whole, pl.BlockSpec((8, D_MODEL), lambda i: (0, 0))],
        out_shape=[jax.ShapeDtypeStruct((seq, D_MODEL), F32), jax.ShapeDtypeStruct((seq, D_MODEL), BF16),
                   jax.ShapeDtypeStruct((D_MODEL, D_MODEL), F32), jax.ShapeDtypeStruct((D_MODEL, D_MODEL), BF16),
                   jax.ShapeDtypeStruct((8, D_MODEL), F32)],
        compiler_params=_params(1),
    )(mixed, w_out_g, x, target, g_post.reshape(1, D_MODEL))


def _head_sum(prod, same_head):
    hi = prod.astype(BF16)
    lo = (prod - hi.astype(F32)).astype(BF16)
    return (jnp.dot(hi, same_head, preferred_element_type=F32) + jnp.dot(lo, same_head, preferred_element_type=F32))


def _gate_bwd(dmixed, zr, o, conv_w, tm=256):
    seq = zr.shape[0]
    n_tiles = seq // tm
    n_dil = len(DILATIONS)
    a0, h0, b0, c0, g0 = 0, ATTN_W, ATTN_W + CONV_W, ATTN_W + 2 * CONV_W, ATTN_W + 3 * CONV_W

    def body(dm_ref, dmn_ref, zr_ref, zp_ref, zn_ref, o_ref, w_ref, *rest):
        do_refs, dl_refs = rest[:n_dil], rest[n_dil:2 * n_dil]
        dz_ref, dw_ref, do_scr, dl_scr, tmp = rest[2 * n_dil:]
        i = pl.program_id(0)
        part = lambda ref, lo, hi: ref[:, lo:hi].astype(F32)
        ga = part(zr_ref, a0, h0)
        sg = _sigmoid(ga)
        dattn = part(dm_ref, 0, ATTN_W)
        ov = o_ref[...].astype(F32)
        do = dattn * (ga * sg)
        dz_ref[:, a0:h0] = (dattn * ov * (sg * (1.0 + ga * (1.0 - sg)))).astype(BF16)
        li = lax.broadcasted_iota(jnp.int32, (BLK, BLK), 0) // HEAD_DIM
        lj = lax.broadcasted_iota(jnp.int32, (BLK, BLK), 1) // HEAD_DIM
        same_head = (li == lj).astype(BF16)
        prod = do * ov
        for c in range(N_CHUNK):
            cols = slice(c * BLK, (c + 1) * BLK)
            do_scr[c] = do[:, cols]
            dl_scr[c] = _head_sum(prod[:, cols], same_head)
        _to_residues(do_scr, 0, do_refs, tmp, tm, BF16)
        _to_residues(dl_scr, 0, dl_refs, tmp, tm, BF16)

        ch, cb, cc, gc = (part(zr_ref, lo, hi) for lo, hi in ((h0, b0), (b0, c0), (c0, g0), (g0, REST_W)))
        u = cc * ch
        before = jnp.where(i > 0, part(zp_ref, c0, g0) * part(zp_ref, h0, b0), 0.0)
        u1, u2 = _conv_taps(u, before, tm)
        w0, w1, w2 = w_ref[0:1, :], w_ref[1:2, :], w_ref[2:3, :]
        y = u2 * w0 + u1 * w1 + u * w2
        sc = _sigmoid(gc)
        silu_c = gc * sc
        dconv = part(dm_ref, ATTN_W, D_MODEL)
        dz_ref[:, b0:c0] = (dconv * y * silu_c).astype(BF16)
        dz_ref[:, g0:] = (dconv * (cb * y) * (sc * (1.0 + gc * (1.0 - sc)))).astype(BF16)
        dy = dconv * cb * silu_c
        gn = part(zn_ref, g0, REST_W)
        after = jnp.where(i < n_tiles - 1,
                          part(dmn_ref, ATTN_W, D_MODEL) * part(zn_ref, b0, c0) * (gn * _sigmoid(gn)), 0.0)
        row = lax.broadcasted_iota(jnp.int32, dy.shape, 0)
        nxt, nxt2 = after[0:1, :], after[1:2, :]
        dy1 = jnp.where(row == tm - 1, nxt, pltpu.roll(dy, tm - 1, 0))
        dy2 = jnp.where(row == tm - 1, nxt2, jnp.where(row == tm - 2, nxt, pltpu.roll(dy, tm - 2, 0)))
        du = dy * w2 + dy1 * w1 + dy2 * w0
        dz_ref[:, c0:g0] = (du * ch).astype(BF16)
        dz_ref[:, h0:b0] = (du * cc).astype(BF16)
        dws = [jnp.sum(dy * u2, axis=0, keepdims=True), jnp.sum(dy * u1, axis=0, keepdims=True),
               jnp.sum(dy * u, axis=0, keepdims=True)]

        @pl.when(i == 0)
        def _():
            dw_ref[...] = jnp.zeros_like(dw_ref)

        for n, part in enumerate(dws):
            dw_ref[n:n + 1, :] += part

    row_spec = lambda w: pl.BlockSpec((tm, w), lambda i: (i, 0))
    before, after = _halo_specs(tm, seq)
    views = [_residue_spec(tm, dil) for dil in DILATIONS]
    outs = pl.pallas_call(
        body, name="gate_bwd", grid=(n_tiles,),
        in_specs=[row_spec(D_MODEL), after(D_MODEL), row_spec(REST_W), before(REST_W), after(REST_W),
                  row_spec(ATTN_W), _resident((3, CONV_W))],
        out_specs=views * 2 + [row_spec(REST_W), pl.BlockSpec((8, CONV_W), lambda i: (0, 0))],
        out_shape=[_residue_shape(seq, dil, BF16) for dil in DILATIONS] * 2
        + [jax.ShapeDtypeStruct((seq, REST_W), BF16), jax.ShapeDtypeStruct((8, CONV_W), F32)],
        scratch_shapes=[pltpu.VMEM((N_CHUNK, tm, BLK), F32)] * 3,
        compiler_params=_params(1),
    )(dmixed, dmixed, zr, zr, zr, o, conv_w)
    return outs[:n_dil], outs[n_dil:2 * n_dil], outs[2 * n_dil], outs[2 * n_dil + 1]


def _in_bwd(dqs, dks, dvs, dzr, x, d_out, g_pre, w_in_g, st_post, dconv, tm=256):
    seq = x.shape[0]

    def body(q1, q2, q3, k1, k2, k3, v1, v2, v3, dzr_ref, ca_ref, sa_ref, cb_ref, sb_ref, x_ref, dout_ref, g_ref,
             w_ref, post_ref, dconv_ref, dz_ref, gx_ref, st_ref, *scratch):
        i = pl.program_id(0)
        cos, sin = _tile_rope(ca_ref, sa_ref, cb_ref, sb_ref)
        first_half = (lax.broadcasted_iota(jnp.int32, (tm, BLK), 1) & 32) == 0
        streams = [(q1, q2, q3), (k1, k2, k3), (v1, v2, v3)]
        from4, from16, tmp = scratch[0:3], scratch[3:6], scratch[6]
        per_slab = SHARD_IN // BLK

        def unrope(t):
            return t * cos - _swap_halves(t, first_half) * sin

        def to_positions(a):
            _from_residue(streams[a][1], from4[a], 4, tm, accumulate=False)
            _from_residue(streams[a][2], from16[a], 16, tm, accumulate=False, tmp=tmp)

        def assemble(j):
            for chunk in range(j * per_slab, (j + 1) * per_slab):
                a, c = divmod(chunk, N_CHUNK)
                if a < 3:
                    total = streams[a][0][:, _lanes(0, c)].astype(F32) + from4[a][c] + from16[a][c]
                    val = (unrope(total) if a < 2 else total).astype(BF16)
                else:
                    val = dzr_ref[:, (chunk - 3 * N_CHUNK) * BLK:(chunk - 3 * N_CHUNK + 1) * BLK]
                dz_ref[:, chunk * BLK:(chunk + 1) * BLK] = val
            return dz_ref[:, j * SHARD_IN:(j + 1) * SHARD_IN]

        order = [j for j in range(N_DEV) if j * per_slab >= 3 * N_CHUNK]
        order += [j for j in range(N_DEV) if j not in order]
        assert order[2] * per_slab >= 3 * N_CHUNK
        ahead = assemble(order[0])
        dh = None
        for n, j in enumerate(order):
            part = lax.dot_general(ahead, w_ref[j], (((1,), (1,)), ((), ())), preferred_element_type=F32)
            if n < 3:
                to_positions(n)
            if n + 1 < N_DEV:
                ahead = assemble(order[n + 1])
            dh = part if dh is None else dh + part
        xv = x_ref[...]
        r = lax.rsqrt(jnp.mean(xv * xv, axis=-1, keepdims=True) + NORM_EPS)
        xhat = xv * r
        tg = dh * g_ref[...]
        gx_ref[...] = dout_ref[...] + r * (tg - xhat * jnp.mean(tg * xhat, axis=-1, keepdims=True))
        gsum = jnp.sum(dh * xhat, axis=0, keepdims=True)

        @pl.when(i == 0)
        def _():
            st_ref[...] = jnp.zeros_like(st_ref)
            st_ref[1:3, :] = post_ref[0:2, :]
            st_ref[3:6, 0:CONV_W] = dconv_ref[0:3, :]

        st_ref[0:1, :] += gsum

    row = lambda w: pl.BlockSpec((tm, w), lambda i: (i, 0))
    return pl.pallas_call(
        body, name="in_bwd", grid=(seq // tm,),
        in_specs=[_residue_spec(tm, dil) for dil in DILATIONS] * 3
        + [row(REST_W)] + _rope_specs(tm) + [row(D_MODEL), row(D_MODEL), _resident((1, D_MODEL)),
                                             _resident((N_DEV, D_MODEL, SHARD_IN)), _resident((8, D_MODEL)),
                                             _resident((8, CONV_W))],
        out_specs=[row(IN_W), row(D_MODEL), pl.BlockSpec((8, D_MODEL), lambda i: (0, 0))],
        out_shape=[jax.ShapeDtypeStruct((seq, IN_W), BF16), jax.ShapeDtypeStruct((seq, D_MODEL), F32),
                   jax.ShapeDtypeStruct((8, D_MODEL), F32)],
        scratch_shapes=[pltpu.VMEM((N_CHUNK, tm, BLK), F32)] * 7,
        compiler_params=_params(1),
    )(*dqs, *dks, *dvs, dzr, *_rope_tables(seq, tm), x, d_out, g_pre.reshape(1, D_MODEL), w_in_g, st_post, dconv)


def _local_step(x, target, g_pre, g_post, w_in_g, w_out_g, conv_w):
    qkv, zr, ht = _fwd_in(x, g_pre, w_in_g)
    parts = [_attn_fwd(*qkv[n], dil) for n, dil in enumerate(DILATIONS)]
    mixed, o, lse = _attn_combine([p[0] for p in parts], [p[1] for p in parts], zr, conv_w)
    d_out, dmixed, dw_out, dw_out_bf, st_post = _out_loss_bwd(mixed, w_out_g, x, target, g_post)
    do, delta, dzr, dconv = _gate_bwd(dmixed, zr, o, conv_w)
    grads = [_attn_bwd(*qkv[n], do[n], lse[n], delta[n], dil) for n, dil in enumerate(DILATIONS)]
    dz, grad_x, small = _in_bwd([g[0] for g in grads], [g[1] for g in grads], [g[2] for g in grads], dzr,
                                x, d_out, g_pre, w_in_g, st_post, dconv)
    return grad_x, ht, dz, dw_out, dw_out_bf, small


def _coords():
    return lax.axis_index("x"), lax.axis_index("y"), lax.axis_index("c")


def _peer(k):
    x, y, c = _coords()
    px = 1 - x if k & 4 else x
    py = 1 - y if k & 2 else y
    pc = 1 - c if k & 1 else c
    return (px, py, pc), 4 * px + 2 * py + pc


HBM_SPEC = pl.BlockSpec(memory_space=pltpu.HBM)
VMEM_SPEC = pl.BlockSpec(memory_space=pltpu.VMEM)


def _ag_weights(w_in, w_out, conv_w):
    def body(win_ref, wout_ref, cw_ref, gin_ref, gout_ref, gcw_ref, win_bf, wout_bf, cw_pad, send_sems, recv_sems,
             local_sems):
        x, y, c = _coords()
        me, sibling = (x, y, c), (x, y, 1 - c)
        flip = lambda v, yes: v + yes - 2 * v * yes
        x_nbr, y_nbr, diagonal = (1 - x, y, c), (x, 1 - y, c), (1 - x, 1 - y, c)
        relay_from = (flip(x, 1 - c), flip(y, c), c)
        relay_to = (flip(x, c), flip(y, 1 - c), c)
        slab = lambda px, py, pc: 4 * px + 2 * py + pc
        win_bf[...] = win_ref[...].astype(BF16)
        wout_bf[...] = wout_ref[...].astype(BF16)
        cw_pad[...] = jnp.zeros_like(cw_pad)
        cw_pad[0:3, 0:CONV_W // N_DEV] = cw_ref[...]
        mine = [win_bf, wout_bf, cw_pad]
        gathered = [gin_ref, gout_ref, gcw_ref]

        def copies(k, block, to, own=False):
            return [pltpu.make_async_remote_copy(src_ref=mine[a] if own else gathered[a].at[slab(*block)],
                                                 dst_ref=gathered[a].at[slab(*block)], send_sem=send_sems.at[k, a],
                                                 recv_sem=recv_sems.at[k, a], device_id=to, device_id_type=MESH)
                    for a in range(3)]

        local = [pltpu.make_async_copy(mine[a], gathered[a].at[slab(*me)], local_sems.at[a]) for a in range(3)]
        for cp in local:
            cp.start()
        started = copies(0, me, sibling, own=True) + copies(1, me, x_nbr, own=True) + copies(2, me, y_nbr, own=True)
        for cp in started:
            cp.start()
        for cp in copies(1, x_nbr, me) + copies(2, y_nbr, me):
            cp.wait_recv()
        onward = copies(3, relay_from, relay_to) + copies(4, x_nbr, sibling) + copies(5, y_nbr, sibling)
        for cp in onward:
            cp.start()
        for cp in copies(3, diagonal, me):
            cp.wait_recv()
        last = copies(6, diagonal, sibling)
        for cp in last:
            cp.start()
        for cp in copies(0, sibling, me):
            cp.wait_recv()
        for k, origin in ((4, (1 - x, y, 1 - c)), (5, (x, 1 - y, 1 - c)), (6, (1 - x, 1 - y, 1 - c))):
            for cp in copies(k, origin, me):
                cp.wait_recv()
        for cp in started + onward + last:
            cp.wait_send()
        for cp in local:
            cp.wait()

    return pl.pallas_call(
        body, name="ag_weights",
        in_specs=[VMEM_SPEC, VMEM_SPEC, VMEM_SPEC], out_specs=[HBM_SPEC, HBM_SPEC, HBM_SPEC],
        out_shape=[jax.ShapeDtypeStruct((N_DEV, D_MODEL, SHARD_IN), BF16),
                   jax.ShapeDtypeStruct((N_DEV, SHARD_OUT, D_MODEL), BF16),
                   jax.ShapeDtypeStruct((N_DEV, 8, BLK), F32)],
        scratch_shapes=[pltpu.VMEM((D_MODEL, SHARD_IN), BF16), pltpu.VMEM((SHARD_OUT, D_MODEL), BF16),
                        pltpu.VMEM((8, BLK), F32), pltpu.SemaphoreType.DMA((N_DEV - 1, 3)),
                        pltpu.SemaphoreType.DMA((N_DEV - 1, 3)), pltpu.SemaphoreType.DMA((3,))],
        compiler_params=pltpu.CompilerParams(vmem_limit_bytes=VMEM_LIMIT),
    )(w_in, w_out, conv_w)


def _dw_in_rs(ht, dz, dw_out, small):
    seq = dz.shape[0]

    def body(cols_ref, ht_ref, dz_ref, dout_ref, sm_ref, own_ref, rin_ref, rout_ref, rsm_ref, to_sibling, landed,
             to_chip, zero_buf, d2d_send, d2d_recv, ici_send, ici_recv, side_send, side_recv, local_sems):
        del cols_ref
        step = pl.program_id(0)
        x, y, c = _coords()
        me = 4 * x + 2 * y + c
        sibling = (x, y, 1 - c)
        chips = [(1 - x, y), (x, 1 - y), (1 - x, 1 - y)]

        def d2d(n):
            return pltpu.make_async_remote_copy(src_ref=to_sibling.at[n], dst_ref=landed.at[n], send_sem=d2d_send.at[n],
                                                recv_sem=d2d_recv.at[n], device_id=sibling, device_id_type=MESH)

        def ici(n):
            return pltpu.make_async_remote_copy(src_ref=to_chip.at[n], dst_ref=rin_ref.at[n], send_sem=ici_send.at[n],
                                                recv_sem=ici_recv.at[n], device_id=(*chips[n], c), device_id_type=MESH)

        def side(k, mine):
            peer, peer_idx = _peer(k)
            src_slab, dst_slab = (peer_idx, me) if mine else (me, peer_idx)
            pairs = [(dout_ref.at[src_slab], rout_ref.at[dst_slab]), (sm_ref, rsm_ref.at[dst_slab])]
            return [pltpu.make_async_remote_copy(src_ref=src, dst_ref=dst, send_sem=side_send.at[k - 1, a],
                                                 recv_sem=side_recv.at[k - 1, a], device_id=peer, device_id_type=MESH)
                    for a, (src, dst) in enumerate(pairs)]

        local = [pltpu.make_async_copy(zero_buf, rout_ref.at[me], local_sems.at[0]),
                 pltpu.make_async_copy(sm_ref, rsm_ref.at[me], local_sems.at[1])]

        @pl.when(step == 0)
        def _():
            zero_buf[...] = jnp.zeros_like(zero_buf)
            for cp in local:
                cp.start()
            for k in range(1, N_DEV):
                for cp in side(k, mine=True):
                    cp.start()

        dw = jnp.dot(ht_ref[...], dz_ref[...], preferred_element_type=F32)
        for n, at in zip(range(4), (0, 1, 2, N_DEV - 2)):
            @pl.when(step == at)
            def _(n=n):
                to_sibling[n] = dw.astype(BF16)
                d2d(n).start()

        for n in range(3):
            @pl.when(step == 3 + n)
            def _(n=n):
                d2d(n).wait_recv()
                to_chip[n] = (dw + landed[n].astype(F32)).astype(BF16)
                ici(n).start()

        @pl.when(step == N_DEV - 1)
        def _():
            d2d(3).wait_recv()
            own_ref[...] = dw + landed[3].astype(F32)
            for n in range(3):
                ici(n).wait_recv()
            for k in range(1, N_DEV):
                for cp in side(k, mine=False):
                    cp.wait_recv()
            for n in range(4):
                d2d(n).wait_send()
            for n in range(3):
                ici(n).wait_send()
            for k in range(1, N_DEV):
                for cp in side(k, mine=True):
                    cp.wait_send()
            for cp in local:
                cp.wait()

    x, y, c = _coords()
    others = [(1 - x, y), (x, 1 - y), (1 - x, 1 - y)]
    order = [(*chip, 1 - c) for chip in others] + [(*chip, c) for chip in others] + [(x, y, 1 - c), (x, y, c)]
    cols = jnp.stack([4 * px + 2 * py + pc for px, py, pc in order]).astype(jnp.int32)
    slab = (D_MODEL, SHARD_IN)
    grid_spec = pltpu.PrefetchScalarGridSpec(
        num_scalar_prefetch=1, grid=(N_DEV,),
        in_specs=[pl.BlockSpec((D_MODEL, seq), lambda s, cols: (0, 0), pipeline_mode=pl.Buffered(1)),
                  pl.BlockSpec((seq, SHARD_IN), lambda s, cols: (0, cols[s])), HBM_SPEC, HBM_SPEC],
        out_specs=[pl.BlockSpec(slab, lambda s, cols: (0, 0)), HBM_SPEC, HBM_SPEC, HBM_SPEC],
        scratch_shapes=[pltpu.VMEM((4, *slab), BF16), pltpu.VMEM((4, *slab), BF16), pltpu.VMEM((3, *slab), BF16),
                        pltpu.VMEM((SHARD_OUT, D_MODEL), BF16),
                        pltpu.SemaphoreType.DMA((4,)), pltpu.SemaphoreType.DMA((4,)),
                        pltpu.SemaphoreType.DMA((3,)), pltpu.SemaphoreType.DMA((3,)),
                        pltpu.SemaphoreType.DMA((N_DEV - 1, 2)), pltpu.SemaphoreType.DMA((N_DEV - 1, 2)),
                        pltpu.SemaphoreType.DMA((2,))])
    return pl.pallas_call(
        body, name="dw_in_rs", grid_spec=grid_spec,
        out_shape=[jax.ShapeDtypeStruct(slab, F32),
                   jax.ShapeDtypeStruct((3, *slab), BF16),
                   jax.ShapeDtypeStruct((N_DEV, SHARD_OUT, D_MODEL), BF16),
                   jax.ShapeDtypeStruct((N_DEV, 8, D_MODEL), F32)],
        compiler_params=_params(1),
    )(cols, ht, dz, dw_out, small)


def _adamw_math(w, g, m, v):
    m = ADAM_B1 * m + (1.0 - ADAM_B1) * g
    v = ADAM_B2 * v + (1.0 - ADAM_B2) * (g * g)
    m_hat = m / (1.0 - ADAM_B1 ** ADAM_STEP)
    v_hat = v / (1.0 - ADAM_B2 ** ADAM_STEP)
    delta = -ADAM_LR * (m_hat / (jnp.sqrt(v_hat) + ADAM_EPS) + ADAM_WD * w)
    return delta, m, v


def _sum_slabs(ref, first=None):
    total = ref[0].astype(F32) if first is None else first + ref[0].astype(F32)
    for s in range(1, ref.shape[0]):
        total = total + ref[s].astype(F32)
    return total


def _adamw_slabs(parts, own, own_slab, w, m, v, name, tr):
    rows, cols = w.shape
    tile = pl.BlockSpec((tr, cols), lambda i, s: (i, 0))
    own_spec = tile if own_slab is None else pl.BlockSpec((1, tr, cols), lambda i, s: (s[0], i, 0))

    def body(s_ref, p_ref, own_ref, w_ref, m_ref, v_ref, g_ref, d_ref, nm_ref, nv_ref):
        del s_ref
        g = _sum_slabs(p_ref, own_ref[...].reshape(tr, cols))
        g_ref[...] = g
        d_ref[...], nm_ref[...], nv_ref[...] = _adamw_math(w_ref[...], g, m_ref[...], v_ref[...])

    slab = jnp.zeros((1,), jnp.int32) if own_slab is None else own_slab.reshape(1).astype(jnp.int32)
    grid_spec = pltpu.PrefetchScalarGridSpec(
        num_scalar_prefetch=1, grid=(rows // tr,),
        in_specs=[pl.BlockSpec((parts.shape[0], tr, cols), lambda i, s: (0, i, 0)), own_spec, tile, tile, tile],
        out_specs=[tile] * 4)
    return pl.pallas_call(
        body, name=name, grid_spec=grid_spec,
        out_shape=[jax.ShapeDtypeStruct((rows, cols), F32)] * 4,
        compiler_params=_params(1),
    )(slab, parts, own, w, m, v)


def _adamw_small(parts, me, pre, post, conv):
    n_conv = CONV_W // N_DEV

    def body(me_ref, p_ref, *refs):
        ins, (loss_ref, *outs) = refs[:9], refs[9:]
        sums = _sum_slabs(p_ref)
        loss_ref[...] = sums[2:3, 0:1]
        mine = pltpu.roll(sums[:, 0:CONV_W], (CONV_W - me_ref[0] * n_conv) % CONV_W, 1)[3:6, 0:n_conv]
        for n, g in enumerate((sums[0:1], sums[1:2], mine)):
            w_ref, m_ref, v_ref = ins[3 * n:3 * n + 3]
            outs[4 * n][...] = g
            for out, val in zip(outs[4 * n + 1:4 * n + 4], _adamw_math(w_ref[...], g, m_ref[...], v_ref[...])):
                out[...] = val

    row = jax.ShapeDtypeStruct((1, D_MODEL), F32)
    small = jax.ShapeDtypeStruct((3, n_conv), F32)
    return pl.pallas_call(
        body, name="adamw_small",
        in_specs=[pl.BlockSpec(memory_space=pltpu.SMEM)] + [VMEM_SPEC] * 10,
        out_shape=[jax.ShapeDtypeStruct((1, 1), F32)] + [row] * 8 + [small] * 4,
    )(me.reshape(1).astype(jnp.int32), parts, *pre, *post, *conv)


def kernel(x, norm_pre_g, w_in, conv_w, w_out, norm_post_g, loss_target, m_norm_pre_g, m_w_in, m_conv_w, m_w_out,
           m_norm_post_g, v_norm_pre_g, v_w_in, v_conv_w, v_w_out, v_norm_post_g):
    n_conv = CONV_W // N_DEV
    w_in_g, w_out_g, conv_g = _ag_weights(w_in, w_out, conv_w)
    conv_full = conv_g[:, 0:3, 0:n_conv].transpose(1, 0, 2).reshape(3, CONV_W)
    grad_x, ht, dz, dw_out, dw_out_bf, small = _local_step(x[0], loss_target[0], norm_pre_g, norm_post_g, w_in_g,
                                                           w_out_g.reshape(D_MODEL, D_MODEL), conv_full)
    own_in, r_in, r_out, r_small = _dw_in_rs(ht, dz, dw_out_bf.reshape(N_DEV, SHARD_OUT, D_MODEL), small)
    me = 4 * lax.axis_index("x") + 2 * lax.axis_index("y") + lax.axis_index("c")
    g_in, d_in, nm_in, nv_in = _adamw_slabs(r_in, own_in, None, w_in, m_w_in, v_w_in, "adamw_in", 256)
    g_out, d_out, nm_out, nv_out = _adamw_slabs(r_out, dw_out.reshape(N_DEV, SHARD_OUT, D_MODEL), me, w_out, m_w_out,
                                                v_w_out, "adamw_out", SHARD_OUT)
    vec = lambda a: a.reshape(1, D_MODEL)
    (loss, g_pre, d_pre, nm_pre, nv_pre, g_post, d_post, nm_post, nv_post, g_conv, d_conv, nm_conv,
     nv_conv) = _adamw_small(r_small, me, [vec(a) for a in (norm_pre_g, m_norm_pre_g, v_norm_pre_g)],
                             [vec(a) for a in (norm_post_g, m_norm_post_g, v_norm_post_g)],
                             (conv_w, m_conv_w, v_conv_w))
    flat = lambda a: a.reshape(D_MODEL)
    return (loss.reshape(()), grad_x[None], flat(g_pre), g_in, g_conv, g_out, flat(g_post),
            flat(d_pre), d_in, d_conv, d_out, flat(d_post),
            flat(nm_pre), nm_in, nm_conv, nm_out, flat(nm_post),
            flat(nv_pre), nv_in, nv_conv, nv_out, flat(nv_post))
```

```python
import functools

import jax
import jax.numpy as jnp
from jax import lax
from jax.experimental import pallas as pl
from jax.experimental.pallas import tpu as pltpu

F32 = jnp.float32
BF16 = jnp.bfloat16

D_MODEL = 1024
HEAD_DIM = 64
ATTN_W = 768
CONV_W = 256
IN_W = 4096
REST_W = IN_W - 3 * ATTN_W
BLK = 128
N_DEV = 8
SHARD_IN = IN_W // N_DEV
SHARD_OUT = D_MODEL // N_DEV
DILATIONS = (1, 4, 16)
ROPE_THETA = 10000.0
NORM_EPS = 1e-6
NEG = -1e30

ADAM_LR = 0.001
ADAM_B1 = 0.9
ADAM_B2 = 0.999
ADAM_EPS = 1e-08
ADAM_WD = 0.01
ADAM_STEP = 10

VMEM_LIMIT = 56 * 1024 * 1024
MESH = pl.DeviceIdType.MESH


def _params(n_grid):
    return pltpu.CompilerParams(dimension_semantics=("arbitrary",) * n_grid, vmem_limit_bytes=VMEM_LIMIT)


def _resident(shape):
    zeros = (0,) * len(shape)
    return pl.BlockSpec(shape, lambda *_: zeros, pipeline_mode=pl.Buffered(1))


def _sigmoid(a):
    return 1.0 / (1.0 + jnp.exp(-a))


def _swap_halves(t, first_half):
    return jnp.where(first_half, pltpu.roll(t, BLK - 32, 1), pltpu.roll(t, 32, 1))


def _rope_tables(seq, tm):
    half = HEAD_DIM // 2
    inv_freq = ROPE_THETA ** (-jnp.arange(half, dtype=F32) * 2.0 / HEAD_DIM)
    freq = jnp.concatenate([inv_freq] * 4)
    sign = jnp.concatenate([-jnp.ones(half, F32), jnp.ones(half, F32)] * 2)
    starts = (jnp.arange(seq // tm) * tm).astype(F32)[:, None] * freq[None, :]
    rows = jnp.arange(tm).astype(F32)[:, None] * freq[None, :]
    slab = lambda a: jnp.broadcast_to(a[:, None, :], (seq // tm, 8, BLK))
    return slab(jnp.cos(starts)), slab(jnp.sin(starts) * sign), jnp.cos(rows), jnp.sin(rows) * sign


def _rope_specs(tm):
    return [pl.BlockSpec((1, 8, BLK), lambda i: (i, 0, 0))] * 2 + [_resident((tm, BLK))] * 2


def _tile_rope(cos_start, sin_start, cos_row, sin_row):
    ca, sa, cb, sb = cos_start[0, 0:1, :], sin_start[0, 0:1, :], cos_row[...], sin_row[...]
    return ca * cb - sa * sb, sa * cb + ca * sb


N_CHUNK = ATTN_W // BLK


def _lanes(r, c):
    return slice(r * ATTN_W + c * BLK, r * ATTN_W + (c + 1) * BLK)


def _to_residues(src, chunk0, dst_refs, tmp, rows, dtype):
    assert DILATIONS == (1, 4, 16)
    dst1, dst4, dst16 = dst_refs
    n4, n16 = rows // 4, rows // 16
    for c in range(N_CHUNK):
        dst1[:, _lanes(0, c)] = src[chunk0 + c].astype(dtype)
        for r1 in range(4):
            tmp[c, r1 * n4:(r1 + 1) * n4, :] = src[chunk0 + c, pl.ds(r1, n4, stride=4), :]
        for r1 in range(4):
            dst4[:, _lanes(r1, c)] = tmp[c, r1 * n4:(r1 + 1) * n4, :].astype(dtype)
            for r2 in range(4):
                dst16[:, _lanes(4 * r2 + r1, c)] = tmp[c, pl.ds(r1 * n4 + r2, n16, stride=4), :].astype(dtype)


def _from_residue(src_ref, dst, dil, rows, tmp=None):
    assert dil in (4, 16)
    n4, n16 = rows // 4, rows // 16
    for c in range(N_CHUNK):
        for r1 in range(4):
            if dil == 4:
                piece = src_ref[:, _lanes(r1, c)].astype(F32)
            else:
                for r2 in range(4):
                    tmp[c, pl.ds(r1 * n4 + r2, n16, stride=4), :] = src_ref[:, _lanes(4 * r2 + r1, c)].astype(F32)
                piece = tmp[c, r1 * n4:(r1 + 1) * n4, :]
            dst[c, pl.ds(r1, n4, stride=4), :] = piece


def _residue_spec(tm, dil):
    return pl.BlockSpec((tm // dil, dil * ATTN_W), lambda i: (i, 0))


def _residue_shape(seq, dil, dtype):
    return jax.ShapeDtypeStruct((seq // dil, dil * ATTN_W), dtype)


def _fwd_in(x, g_pre, w_in_g, tm=512):
    seq = x.shape[0]
    n_dil = len(DILATIONS)

    def body(x_ref, g_ref, w_ref, ca_ref, sa_ref, cb_ref, sb_ref, *rest):
        qkv_refs, (zr_ref, ht_ref, qkv_scr, tmp) = rest[:3 * n_dil], rest[3 * n_dil:]
        xv = x_ref[...]
        r = lax.rsqrt(jnp.mean(xv * xv, axis=-1, keepdims=True) + NORM_EPS)
        hf = (xv * r) * g_ref[...]
        h = hf.astype(BF16)
        ht_ref[...] = h.T
        cos, sin = _tile_rope(ca_ref, sa_ref, cb_ref, sb_ref)
        first_half = (lax.broadcasted_iota(jnp.int32, (tm, BLK), 1) & 32) == 0

        def rope(t):
            return t * cos + _swap_halves(t, first_half) * sin

        def project(j):
            return jnp.dot(h, w_ref[j], preferred_element_type=F32)

        def place(j, zj):
            for n in range(SHARD_IN // BLK):
                chunk, t = j * (SHARD_IN // BLK) + n, zj[:, n * BLK:(n + 1) * BLK]
                if chunk < N_CHUNK:
                    qkv_scr[chunk] = rope(t) * HEAD_DIM ** -0.5
                elif chunk < 2 * N_CHUNK:
                    qkv_scr[chunk] = rope(t)
                elif chunk < 3 * N_CHUNK:
                    qkv_scr[chunk] = t
                else:
                    zr_ref[:, (chunk - 3 * N_CHUNK) * BLK:(chunk - 3 * N_CHUNK + 1) * BLK] = t.astype(BF16)

        ahead = project(0)
        for j in range(N_DEV):
            zj = ahead
            if j + 1 < N_DEV:
                ahead = project(j + 1)
            place(j, zj)
            for a in range(3):
                if (a + 1) * N_CHUNK - 1 in range(j * (SHARD_IN // BLK), (j + 1) * (SHARD_IN // BLK)):
                    _to_residues(qkv_scr, a * N_CHUNK, [qkv_refs[3 * n + a] for n in range(n_dil)], tmp, tm, BF16)

    row = lambda w: pl.BlockSpec((tm, w), lambda i: (i, 0))
    outs = pl.pallas_call(
        body, name="fwd_in", grid=(seq // tm,),
        in_specs=[row(D_MODEL), _resident((1, D_MODEL)), _resident((N_DEV, D_MODEL, SHARD_IN))] + _rope_specs(tm),
        out_specs=[_residue_spec(tm, dil) for dil in DILATIONS for _ in range(3)]
        + [row(REST_W), pl.BlockSpec((D_MODEL, tm), lambda i: (0, i))],
        out_shape=[_residue_shape(seq, dil, BF16) for dil in DILATIONS for _ in range(3)]
        + [jax.ShapeDtypeStruct((seq, REST_W), BF16), jax.ShapeDtypeStruct((D_MODEL, seq), BF16)],
        scratch_shapes=[pltpu.VMEM((3 * N_CHUNK, tm, BLK), F32), pltpu.VMEM((N_CHUNK, tm, BLK), F32)],
        compiler_params=_params(1),
    )(x, g_pre.reshape(1, D_MODEL), w_in_g, *_rope_tables(seq, tm))
    qkv = [tuple(outs[3 * n:3 * n + 3]) for n in range(n_dil)]
    return qkv, outs[3 * n_dil], outs[3 * n_dil + 1]


def _band_bias(first_block):
    kj = lax.broadcasted_iota(jnp.int32, (2 * BLK, BLK), 0)
    qi = lax.broadcasted_iota(jnp.int32, (2 * BLK, BLK), 1)
    valid = (kj >= qi) & (kj <= qi + BLK)
    bias = jnp.where(valid, 0.0, NEG).astype(BF16)
    bias_first = jnp.where(valid & (kj >= BLK), 0.0, NEG).astype(BF16)
    onehot = ((kj & (BLK - 1)) == qi).astype(F32).astype(BF16)
    return onehot, bias, jnp.where(first_block, bias_first, bias)


def _stack_heads(t):
    keep0 = (lax.broadcasted_iota(jnp.int32, t.shape, 1) < HEAD_DIM).astype(F32).astype(BF16)
    return jnp.concatenate([t * keep0, t * (1 - keep0)], axis=0)


def _unstack_heads(t2, head0):
    return jnp.where(head0, t2[:BLK], t2[BLK:])


def _rows_per_head(a, head0):
    b = pltpu.roll(a, HEAD_DIM, 1)
    rows = jnp.concatenate([jnp.where(head0, a, b), jnp.where(head0, b, a)], axis=0)
    return jnp.concatenate([rows, rows], axis=1)


FWD_UNITS, FWD_COLS = 64, 16
BWD_UNITS, BWD_COLS = 32, 4


def _attn_specs(length, dil, units, max_cols):
    n_blocks = length // BLK
    tb = min(units, n_blocks)
    nc = max(n for n in range(1, min(units // tb, max_cols) + 1) if (dil * N_CHUNK) % n == 0)
    assert n_blocks % tb == 0
    tile = pl.BlockSpec((tb * BLK, nc * BLK), lambda c, t: (t, c))
    prev = pl.BlockSpec((BLK, nc * BLK), lambda c, t: (jnp.maximum(t * tb - 1, 0), c))
    grid = (dil * N_CHUNK // nc, n_blocks // tb)
    return tb, nc, tile, prev, grid


def _window(prev_ref, cur_ref, j, cols):
    if j == 0:
        return jnp.concatenate([prev_ref[:, cols], cur_ref[0:BLK, cols]], axis=0)
    return cur_ref[(j - 1) * BLK:(j + 1) * BLK, cols]


def _attn_fwd(q, k, v, dil):
    length = q.shape[0]
    tb, nc, tile, prev, grid = _attn_specs(length, dil, FWD_UNITS, FWD_COLS)

    def body(q_ref, kc_ref, kp_ref, vc_ref, vp_ref, o_ref, lse_ref):
        head0 = lax.broadcasted_iota(jnp.int32, (BLK, BLK), 1) < HEAD_DIM
        onehot, bias, bias_start = _band_bias(pl.program_id(1) == 0)
        ones = jnp.ones((2 * BLK, BLK), BF16)

        def scores(c, j):
            rows, cols = slice(j * BLK, (j + 1) * BLK), slice(c * BLK, (c + 1) * BLK)
            q2 = jnp.concatenate([_stack_heads(q_ref[rows, cols]), onehot], axis=1)
            kk = jnp.concatenate([_window(kp_ref, kc_ref, j, cols), bias_start if j == 0 else bias], axis=1)
            return (lax.dot_general(q2, kk, (((1,), (1,)), ((), ())), preferred_element_type=F32),)

        def probabilities(c, j, s):
            m = jnp.max(s, axis=1, keepdims=True)
            return m, jnp.exp(s - m).astype(BF16)

        def outputs(c, j, m, p):
            rows, cols = slice(j * BLK, (j + 1) * BLK), slice(c * BLK, (c + 1) * BLK)
            vv = jnp.concatenate([_window(vp_ref, vc_ref, j, cols), ones], axis=1)
            pv = jnp.dot(p, vv, preferred_element_type=F32)
            den = pv[:, BLK:]
            o_ref[rows, cols] = _unstack_heads(pv[:, :BLK] / den, head0).astype(BF16)
            lse_ref[rows, cols] = _unstack_heads(m + jnp.log(den), head0)

        units = [(c, j) for c in range(nc) for j in range(tb)]
        stage1, stage2 = {}, {}
        for n in range(len(units) + 2):
            if n < len(units):
                stage1[n] = scores(*units[n])
            if 0 <= n - 1 < len(units):
                stage2[n - 1] = probabilities(*units[n - 1], *stage1.pop(n - 1))
            if 0 <= n - 2 < len(units):
                outputs(*units[n - 2], *stage2.pop(n - 2))

    return pl.pallas_call(
        body, name=f"attn_fwd_d{dil}", grid=grid,
        in_specs=[tile, tile, prev, tile, prev], out_specs=[tile, tile],
        out_shape=[jax.ShapeDtypeStruct(q.shape, BF16), jax.ShapeDtypeStruct(q.shape, F32)],
        compiler_params=_params(2),
    )(q, k, k, v, v)


def _attn_bwd(q, k, v, do, lse, delta, dil):
    length = q.shape[0]
    tb, nc, tile, prev, grid = _attn_specs(length, dil, BWD_UNITS, BWD_COLS)
    whole = pl.BlockSpec((length, nc * BLK), lambda c, t: (0, c))

    def body(q_ref, do_ref, lse_ref, dl_ref, kc_ref, kp_ref, vc_ref, vp_ref, dq_ref, dk_ref, dv_ref):
        t = pl.program_id(1)
        head0 = lax.broadcasted_iota(jnp.int32, (BLK, BLK), 1) < HEAD_DIM
        onehot, bias, bias_start = _band_bias(t == 0)

        def scores(c, j):
            rows, cols = slice(j * BLK, (j + 1) * BLK), slice(c * BLK, (c + 1) * BLK)
            q2 = _stack_heads(q_ref[rows, cols])
            do2 = _stack_heads(do_ref[rows, cols])
            kk = _window(kp_ref, kc_ref, j, cols)
            s = lax.dot_general(jnp.concatenate([q2, onehot], axis=1),
                                jnp.concatenate([kk, bias_start if j == 0 else bias], axis=1),
                                (((1,), (1,)), ((), ())), preferred_element_type=F32)
            dp = lax.dot_general(do2, _window(vp_ref, vc_ref, j, cols), (((1,), (1,)), ((), ())),
                                 preferred_element_type=F32)
            return q2, do2, kk, s, dp

        def probabilities(c, j, q2, do2, kk, s, dp):
            rows, cols = slice(j * BLK, (j + 1) * BLK), slice(c * BLK, (c + 1) * BLK)
            p = jnp.exp(s - _rows_per_head(lse_ref[rows, cols], head0))
            ds = (p * (dp - _rows_per_head(dl_ref[rows, cols].astype(F32), head0))).astype(BF16)
            return q2, do2, kk, p.astype(BF16), ds

        def gradients(c, j, q2, do2, kk, p, ds):
            rows, cols = slice(j * BLK, (j + 1) * BLK), slice(c * BLK, (c + 1) * BLK)
            dq2 = jnp.dot(ds, kk, preferred_element_type=F32)
            dq_ref[rows, cols] = (_unstack_heads(dq2, head0) * HEAD_DIM ** -0.5).astype(BF16)
            dk2 = lax.dot_general(ds, q2, (((0,), (0,)), ((), ())), preferred_element_type=F32)
            dv2 = lax.dot_general(p, do2, (((0,), (0,)), ((), ())), preferred_element_type=F32)
            own = pl.ds(pl.multiple_of((t * tb + j) * BLK, BLK), BLK)
            dk_ref[own, cols] = dk2[BLK:].astype(BF16)
            dv_ref[own, cols] = dv2[BLK:].astype(BF16)

            def add_to_previous():
                before = pl.ds(pl.multiple_of((t * tb + j - 1) * BLK, BLK), BLK)
                dk_ref[before, cols] = (dk_ref[before, cols].astype(F32) + dk2[:BLK]).astype(BF16)
                dv_ref[before, cols] = (dv_ref[before, cols].astype(F32) + dv2[:BLK]).astype(BF16)

            if j > 0:
                add_to_previous()
            elif grid[1] > 1:
                pl.when(t > 0)(add_to_previous)

        units = [(c, j) for c in range(nc) for j in range(tb)]
        stage1 = {0: scores(*units[0])}
        for n in range(len(units)):
            stage2 = probabilities(*units[n], *stage1.pop(n))
            if n + 1 < len(units):
                stage1[n + 1] = scores(*units[n + 1])
            gradients(*units[n], *stage2)

    return pl.pallas_call(
        body, name=f"attn_bwd_d{dil}", grid=grid,
        in_specs=[tile, tile, tile, tile, tile, prev, tile, prev], out_specs=[tile, whole, whole],
        out_shape=[jax.ShapeDtypeStruct(q.shape, BF16)] * 3,
        compiler_params=_params(2),
    )(q, do, lse, delta, k, k, v, v)


HALO = 16


def _halo_specs(tm, seq):
    before = lambda w: pl.BlockSpec((HALO, w), lambda i: (jnp.maximum(i * (tm // HALO) - 1, 0), 0))
    after = lambda w: pl.BlockSpec((HALO, w), lambda i: (jnp.minimum((i + 1) * (tm // HALO), seq // HALO - 1), 0))
    return before, after


def _conv_taps(u, before, tm):
    row = lax.broadcasted_iota(jnp.int32, u.shape, 0)
    last, last2 = before[HALO - 1:HALO, :], before[HALO - 2:HALO - 1, :]
    u1 = jnp.where(row == 0, last, pltpu.roll(u, 1, 0))
    u2 = jnp.where(row == 0, last2, jnp.where(row == 1, last, pltpu.roll(u, 2, 0)))
    return u1, u2


def _attn_combine(o_parts, lse_parts, zr, conv_w, tm=256):
    seq = zr.shape[0]
    a0, h0, b0, c0, g0 = 0, ATTN_W, ATTN_W + CONV_W, ATTN_W + 2 * CONV_W, ATTN_W + 3 * CONV_W

    def body(o1, o2, o3, l1, l2, l3, zr_ref, zp_ref, w_ref, mixed_ref, o_ref, lse1, lse2, lse3, *scr):
        i = pl.program_id(0)
        for src, dst, dil in zip((o2, o3, l2, l3), scr[:4], DILATIONS[1:] * 2):
            _from_residue(src, dst, dil, tm, tmp=scr[5])
        for c in range(N_CHUNK):
            cols = slice(c * BLK, (c + 1) * BLK)
            la, lb, lc = l1[:, cols], scr[2][c], scr[3][c]
            top = jnp.maximum(jnp.maximum(la, lb), lc)
            ea, eb, ec = jnp.exp(la - top), jnp.exp(lb - top), jnp.exp(lc - top)
            den = ea + eb + ec
            inv = 1.0 / den
            o = (ea * inv) * o1[:, cols].astype(F32) + (eb * inv) * scr[0][c] + (ec * inv) * scr[1][c]
            o_ref[:, cols] = o.astype(BF16)
            scr[4][c] = top + jnp.log(den)
            ga = zr_ref[:, cols].astype(F32)
            mixed_ref[:, cols] = (o * (ga * _sigmoid(ga))).astype(BF16)
        _to_residues(scr[4], 0, (lse1, lse2, lse3), scr[5], tm, F32)
        part = lambda ref, lo, hi: ref[:, lo:hi].astype(F32)
        u = part(zr_ref, c0, g0) * part(zr_ref, h0, b0)
        before = jnp.where(i > 0, part(zp_ref, c0, g0) * part(zp_ref, h0, b0), 0.0)
        u1, u2 = _conv_taps(u, before, tm)
        y = u2 * w_ref[0:1, :] + u1 * w_ref[1:2, :] + u * w_ref[2:3, :]
        gc = part(zr_ref, g0, REST_W)
        mixed_ref[:, ATTN_W:] = ((part(zr_ref, b0, c0) * y) * (gc * _sigmoid(gc))).astype(BF16)

    row = lambda w: pl.BlockSpec((tm, w), lambda i: (i, 0))
    before, _ = _halo_specs(tm, seq)
    views = [_residue_spec(tm, dil) for dil in DILATIONS]
    outs = pl.pallas_call(
        body, name="attn_combine", grid=(seq // tm,),
        in_specs=views * 2 + [row(REST_W), before(REST_W), _resident((3, CONV_W))],
        out_specs=[row(D_MODEL), row(ATTN_W)] + views,
        out_shape=[jax.ShapeDtypeStruct((seq, D_MODEL), BF16), jax.ShapeDtypeStruct((seq, ATTN_W), BF16)]
        + [_residue_shape(seq, dil, F32) for dil in DILATIONS],
        scratch_shapes=[pltpu.VMEM((N_CHUNK, tm, BLK), F32)] * 6,
        compiler_params=_params(1),
    )(*o_parts, *lse_parts, zr, zr, conv_w)
    return outs[0], outs[1], outs[2:]


def _out_loss_bwd(mixed, w_out_g, x, target, g_post, tm=512, n_parts=2):
    seq = x.shape[0]

    def body(mx_ref, w_ref, x_ref, t_ref, g_ref, dout_ref, dmx_ref, dw_ref, dwb_ref, st_ref):
        i = pl.program_id(0)
        g = g_ref[...]
        parts = [slice(n * (tm // n_parts), (n + 1) * (tm // n_parts)) for n in range(n_parts)]

        def project(rows):
            return jnp.dot(mx_ref[rows, :], w_ref[...], preferred_element_type=F32)

        def head(rows, y):
            r = lax.rsqrt(jnp.mean(y * y, axis=-1, keepdims=True) + NORM_EPS)
            yhat = y * r
            err = (x_ref[rows, :] + yhat * g) - t_ref[rows, :]
            dn = err * (1.0 / D_MODEL)
            dout_ref[rows, :] = dn
            tg = dn * g
            dy = (r * (tg - yhat * jnp.mean(tg * yhat, axis=-1, keepdims=True))).astype(BF16)
            dmx_ref[rows, :] = lax.dot_general(dy, w_ref[...], (((1,), (1,)), ((), ())),
                                               preferred_element_type=F32).astype(BF16)
            return dy, jnp.sum(dn * yhat, axis=0, keepdims=True), jnp.sum(err * err)

        ahead, done = project(parts[0]), []
        for n, rows in enumerate(parts):
            y = ahead
            if n + 1 < n_parts:
                ahead = project(parts[n + 1])
            done.append(head(rows, y))
        dy = jnp.concatenate([d[0] for d in done], axis=0)
        dw = lax.dot_general(mx_ref[...], dy, (((0,), (0,)), ((), ())), preferred_element_type=F32)
        gsum = functools.reduce(lambda a, b: a + b, [d[1] for d in done])
        lsum = jnp.broadcast_to(0.5 / D_MODEL * functools.reduce(lambda a, b: a + b, [d[2] for d in done]),
                                (1, D_MODEL))

        @pl.when(i == 0)
        def _():
            dw_ref[...] = dw
            st_ref[...] = jnp.zeros_like(st_ref)
            st_ref[0:1, :] = gsum
            st_ref[1:2, :] = lsum

        @pl.when(i > 0)
        def _():
            dw_ref[...] += dw
            st_ref[0:1, :] += gsum
            st_ref[1:2, :] += lsum

        @pl.when(i == seq // tm - 1)
        def _():
            dwb_ref[...] = dw_ref[...].astype(BF16)

    row = lambda w: pl.BlockSpec((tm, w), lambda i: (i, 0))
    whole = pl.BlockSpec((D_MODEL, D_MODEL), lambda i: (0, 0))
    return pl.pallas_call(
        body, name="out_loss_bwd", grid=(seq // tm,),
        in_specs=[row(D_MODEL), _resident((D_MODEL, D_MODEL)), row(D_MODEL), row(D_MODEL), _resident((1, D_MODEL))],
        out_specs=[row(D_MODEL), row(D_MODEL), whole, whole, pl.BlockSpec((8, D_MODEL), lambda i: (0, 0))],
        out_shape=[jax.ShapeDtypeStruct((seq, D_MODEL), F32), jax.ShapeDtypeStruct((seq, D_MODEL), BF16),
                   jax.ShapeDtypeStruct((D_MODEL, D_MODEL), F32), jax.ShapeDtypeStruct((D_MODEL, D_MODEL), BF16),
                   jax.ShapeDtypeStruct((8, D_MODEL), F32)],
        compiler_params=_params(1),
    )(mixed, w_out_g, x, target, g_post.reshape(1, D_MODEL))


def _head_sum(prod, same_head):
    hi = prod.astype(BF16)
    lo = (prod - hi.astype(F32)).astype(BF16)
    return (jnp.dot(hi, same_head, preferred_element_type=F32) + jnp.dot(lo, same_head, preferred_element_type=F32))


def _gate_bwd(dmixed, zr, o, conv_w, tm=256):
    seq = zr.shape[0]
    n_tiles = seq // tm
    n_dil = len(DILATIONS)
    a0, h0, b0, c0, g0 = 0, ATTN_W, ATTN_W + CONV_W, ATTN_W + 2 * CONV_W, ATTN_W + 3 * CONV_W

    def body(dm_ref, dmn_ref, zr_ref, zp_ref, zn_ref, o_ref, w_ref, *rest):
        do_refs, dl_refs = rest[:n_dil], rest[n_dil:2 * n_dil]
        dz_ref, dw_ref, do_scr, dl_scr, tmp = rest[2 * n_dil:]
        i = pl.program_id(0)
        part = lambda ref, lo, hi: ref[:, lo:hi].astype(F32)
        ga = part(zr_ref, a0, h0)
        sg = _sigmoid(ga)
        dattn = part(dm_ref, 0, ATTN_W)
        ov = o_ref[...].astype(F32)
        do = dattn * (ga * sg)
        dz_ref[:, a0:h0] = (dattn * ov * (sg * (1.0 + ga * (1.0 - sg)))).astype(BF16)
        li = lax.broadcasted_iota(jnp.int32, (BLK, BLK), 0) // HEAD_DIM
        lj = lax.broadcasted_iota(jnp.int32, (BLK, BLK), 1) // HEAD_DIM
        same_head = (li == lj).astype(BF16)
        prod = do * ov
        for c in range(N_CHUNK):
            cols = slice(c * BLK, (c + 1) * BLK)
            do_scr[c] = do[:, cols]
            dl_scr[c] = _head_sum(prod[:, cols], same_head)
        _to_residues(do_scr, 0, do_refs, tmp, tm, BF16)
        _to_residues(dl_scr, 0, dl_refs, tmp, tm, BF16)

        ch, cb, cc, gc = (part(zr_ref, lo, hi) for lo, hi in ((h0, b0), (b0, c0), (c0, g0), (g0, REST_W)))
        u = cc * ch
        before = jnp.where(i > 0, part(zp_ref, c0, g0) * part(zp_ref, h0, b0), 0.0)
        u1, u2 = _conv_taps(u, before, tm)
        w0, w1, w2 = w_ref[0:1, :], w_ref[1:2, :], w_ref[2:3, :]
        y = u2 * w0 + u1 * w1 + u * w2
        sc = _sigmoid(gc)
        silu_c = gc * sc
        dconv = part(dm_ref, ATTN_W, D_MODEL)
        dz_ref[:, b0:c0] = (dconv * y * silu_c).astype(BF16)
        dz_ref[:, g0:] = (dconv * (cb * y) * (sc * (1.0 + gc * (1.0 - sc)))).astype(BF16)
        dy = dconv * cb * silu_c
        gn = part(zn_ref, g0, REST_W)
        after = jnp.where(i < n_tiles - 1,
                          part(dmn_ref, ATTN_W, D_MODEL) * part(zn_ref, b0, c0) * (gn * _sigmoid(gn)), 0.0)
        row = lax.broadcasted_iota(jnp.int32, dy.shape, 0)
        nxt, nxt2 = after[0:1, :], after[1:2, :]
        dy1 = jnp.where(row == tm - 1, nxt, pltpu.roll(dy, tm - 1, 0))
        dy2 = jnp.where(row == tm - 1, nxt2, jnp.where(row == tm - 2, nxt, pltpu.roll(dy, tm - 2, 0)))
        du = dy * w2 + dy1 * w1 + dy2 * w0
        dz_ref[:, c0:g0] = (du * ch).astype(BF16)
        dz_ref[:, h0:b0] = (du * cc).astype(BF16)
        dws = [jnp.sum(dy * u2, axis=0, keepdims=True), jnp.sum(dy * u1, axis=0, keepdims=True),
               jnp.sum(dy * u, axis=0, keepdims=True)]

        @pl.when(i == 0)
        def _():
            dw_ref[...] = jnp.zeros_like(dw_ref)

        for n, part in enumerate(dws):
            dw_ref[n:n + 1, :] += part

    row_spec = lambda w: pl.BlockSpec((tm, w), lambda i: (i, 0))
    before, after = _halo_specs(tm, seq)
    views = [_residue_spec(tm, dil) for dil in DILATIONS]
    outs = pl.pallas_call(
        body, name="gate_bwd", grid=(n_tiles,),
        in_specs=[row_spec(D_MODEL), after(D_MODEL), row_spec(REST_W), before(REST_W), after(REST_W),
                  row_spec(ATTN_W), _resident((3, CONV_W))],
        out_specs=views * 2 + [row_spec(REST_W), pl.BlockSpec((8, CONV_W), lambda i: (0, 0))],
        out_shape=[_residue_shape(seq, dil, BF16) for dil in DILATIONS] * 2
        + [jax.ShapeDtypeStruct((seq, REST_W), BF16), jax.ShapeDtypeStruct((8, CONV_W), F32)],
        scratch_shapes=[pltpu.VMEM((N_CHUNK, tm, BLK), F32)] * 3,
        compiler_params=_params(1),
    )(dmixed, dmixed, zr, zr, zr, o, conv_w)
    return outs[:n_dil], outs[n_dil:2 * n_dil], outs[2 * n_dil], outs[2 * n_dil + 1]


def _in_bwd(dqs, dks, dvs, dzr, x, d_out, g_pre, w_in_g, st_post, dconv, tm=256):
    seq = x.shape[0]

    def body(q1, q2, q3, k1, k2, k3, v1, v2, v3, dzr_ref, ca_ref, sa_ref, cb_ref, sb_ref, x_ref, dout_ref, g_ref,
             w_ref, post_ref, dconv_ref, dz_ref, gx_ref, st_ref, *scratch):
        i = pl.program_id(0)
        cos, sin = _tile_rope(ca_ref, sa_ref, cb_ref, sb_ref)
        first_half = (lax.broadcasted_iota(jnp.int32, (tm, BLK), 1) & 32) == 0
        streams = [(q1, q2, q3), (k1, k2, k3), (v1, v2, v3)]
        from4, from16, tmp = scratch[0:3], scratch[3:6], scratch[6]
        per_slab = SHARD_IN // BLK

        def unrope(t):
            return t * cos - _swap_halves(t, first_half) * sin

        def to_positions(a):
            _from_residue(streams[a][1], from4[a], 4, tm)
            _from_residue(streams[a][2], from16[a], 16, tm, tmp=tmp)

        def assemble(j):
            for chunk in range(j * per_slab, (j + 1) * per_slab):
                a, c = divmod(chunk, N_CHUNK)
                if a < 3:
                    total = streams[a][0][:, _lanes(0, c)].astype(F32) + from4[a][c] + from16[a][c]
                    val = (unrope(total) if a < 2 else total).astype(BF16)
                else:
                    val = dzr_ref[:, (chunk - 3 * N_CHUNK) * BLK:(chunk - 3 * N_CHUNK + 1) * BLK]
                dz_ref[:, chunk * BLK:(chunk + 1) * BLK] = val
            return dz_ref[:, j * SHARD_IN:(j + 1) * SHARD_IN]

        order = [j for j in range(N_DEV) if j * per_slab >= 3 * N_CHUNK]
        order += [j for j in range(N_DEV) if j not in order]
        assert order[2] * per_slab >= 3 * N_CHUNK
        ahead = assemble(order[0])
        dh = None
        for n, j in enumerate(order):
            part = lax.dot_general(ahead, w_ref[j], (((1,), (1,)), ((), ())), preferred_element_type=F32)
            if n < 3:
                to_positions(n)
            if n + 1 < N_DEV:
                ahead = assemble(order[n + 1])
            dh = part if dh is None else dh + part
        xv = x_ref[...]
        r = lax.rsqrt(jnp.mean(xv * xv, axis=-1, keepdims=True) + NORM_EPS)
        xhat = xv * r
        tg = dh * g_ref[...]
        gx_ref[...] = dout_ref[...] + r * (tg - xhat * jnp.mean(tg * xhat, axis=-1, keepdims=True))
        gsum = jnp.sum(dh * xhat, axis=0, keepdims=True)

        @pl.when(i == 0)
        def _():
            st_ref[...] = jnp.zeros_like(st_ref)
            st_ref[1:3, :] = post_ref[0:2, :]
            st_ref[3:6, 0:CONV_W] = dconv_ref[0:3, :]

        st_ref[0:1, :] += gsum

    row = lambda w: pl.BlockSpec((tm, w), lambda i: (i, 0))
    return pl.pallas_call(
        body, name="in_bwd", grid=(seq // tm,),
        in_specs=[_residue_spec(tm, dil) for dil in DILATIONS] * 3
        + [row(REST_W)] + _rope_specs(tm) + [row(D_MODEL), row(D_MODEL), _resident((1, D_MODEL)),
                                             _resident((N_DEV, D_MODEL, SHARD_IN)), _resident((8, D_MODEL)),
                                             _resident((8, CONV_W))],
        out_specs=[row(IN_W), row(D_MODEL), pl.BlockSpec((8, D_MODEL), lambda i: (0, 0))],
        out_shape=[jax.ShapeDtypeStruct((seq, IN_W), BF16), jax.ShapeDtypeStruct((seq, D_MODEL), F32),
                   jax.ShapeDtypeStruct((8, D_MODEL), F32)],
        scratch_shapes=[pltpu.VMEM((N_CHUNK, tm, BLK), F32)] * 7,
        compiler_params=_params(1),
    )(*dqs, *dks, *dvs, dzr, *_rope_tables(seq, tm), x, d_out, g_pre.reshape(1, D_MODEL), w_in_g, st_post, dconv)


def _local_step(x, target, g_pre, g_post, w_in_g, w_out_g, conv_w):
    qkv, zr, ht = _fwd_in(x, g_pre, w_in_g)
    parts = [_attn_fwd(*qkv[n], dil) for n, dil in enumerate(DILATIONS)]
    mixed, o, lse = _attn_combine([p[0] for p in parts], [p[1] for p in parts], zr, conv_w)
    d_out, dmixed, dw_out, dw_out_bf, st_post = _out_loss_bwd(mixed, w_out_g, x, target, g_post)
    do, delta, dzr, dconv = _gate_bwd(dmixed, zr, o, conv_w)
    grads = [_attn_bwd(*qkv[n], do[n], lse[n], delta[n], dil) for n, dil in enumerate(DILATIONS)]
    dz, grad_x, small = _in_bwd([g[0] for g in grads], [g[1] for g in grads], [g[2] for g in grads], dzr,
                                x, d_out, g_pre, w_in_g, st_post, dconv)
    return grad_x, ht, dz, dw_out, dw_out_bf, small


def _coords():
    return lax.axis_index("x"), lax.axis_index("y"), lax.axis_index("c")


def _peer(k):
    x, y, c = _coords()
    px = 1 - x if k & 4 else x
    py = 1 - y if k & 2 else y
    pc = 1 - c if k & 1 else c
    return (px, py, pc), 4 * px + 2 * py + pc


HBM_SPEC = pl.BlockSpec(memory_space=pltpu.HBM)
VMEM_SPEC = pl.BlockSpec(memory_space=pltpu.VMEM)


def _ag_weights(w_in, w_out, conv_w):
    def body(win_ref, wout_ref, cw_ref, gin_ref, gout_ref, gcw_ref, win_bf, wout_bf, cw_pad, send_sems, recv_sems,
             local_sems):
        x, y, c = _coords()
        me, sibling = (x, y, c), (x, y, 1 - c)
        flip = lambda v, yes: v + yes - 2 * v * yes
        x_nbr, y_nbr, diagonal = (1 - x, y, c), (x, 1 - y, c), (1 - x, 1 - y, c)
        relay_from = (flip(x, 1 - c), flip(y, c), c)
        relay_to = (flip(x, c), flip(y, 1 - c), c)
        slab = lambda px, py, pc: 4 * px + 2 * py + pc
        win_bf[...] = win_ref[...].astype(BF16)
        wout_bf[...] = wout_ref[...].astype(BF16)
        cw_pad[...] = jnp.zeros_like(cw_pad)
        cw_pad[0:3, 0:CONV_W // N_DEV] = cw_ref[...]
        mine = [win_bf, wout_bf, cw_pad]
        gathered = [gin_ref, gout_ref, gcw_ref]

        def copies(k, block, to, own=False):
            return [pltpu.make_async_remote_copy(src_ref=mine[a] if own else gathered[a].at[slab(*block)],
                                                 dst_ref=gathered[a].at[slab(*block)], send_sem=send_sems.at[k, a],
                                                 recv_sem=recv_sems.at[k, a], device_id=to, device_id_type=MESH)
                    for a in range(3)]

        local = [pltpu.make_async_copy(mine[a], gathered[a].at[slab(*me)], local_sems.at[a]) for a in range(3)]
        for cp in local:
            cp.start()
        started = copies(0, me, sibling, own=True) + copies(1, me, x_nbr, own=True) + copies(2, me, y_nbr, own=True)
        for cp in started:
            cp.start()
        for cp in copies(1, x_nbr, me) + copies(2, y_nbr, me):
            cp.wait_recv()
        onward = copies(3, relay_from, relay_to) + copies(4, x_nbr, sibling) + copies(5, y_nbr, sibling)
        for cp in onward:
            cp.start()
        for cp in copies(3, diagonal, me):
            cp.wait_recv()
        last = copies(6, diagonal, sibling)
        for cp in last:
            cp.start()
        for cp in copies(0, sibling, me):
            cp.wait_recv()
        for k, origin in ((4, (1 - x, y, 1 - c)), (5, (x, 1 - y, 1 - c)), (6, (1 - x, 1 - y, 1 - c))):
            for cp in copies(k, origin, me):
                cp.wait_recv()
        for cp in started + onward + last:
            cp.wait_send()
        for cp in local:
            cp.wait()

    return pl.pallas_call(
        body, name="ag_weights",
        in_specs=[VMEM_SPEC, VMEM_SPEC, VMEM_SPEC], out_specs=[HBM_SPEC, HBM_SPEC, HBM_SPEC],
        out_shape=[jax.ShapeDtypeStruct((N_DEV, D_MODEL, SHARD_IN), BF16),
                   jax.ShapeDtypeStruct((N_DEV, SHARD_OUT, D_MODEL), BF16),
                   jax.ShapeDtypeStruct((N_DEV, 8, BLK), F32)],
        scratch_shapes=[pltpu.VMEM((D_MODEL, SHARD_IN), BF16), pltpu.VMEM((SHARD_OUT, D_MODEL), BF16),
                        pltpu.VMEM((8, BLK), F32), pltpu.SemaphoreType.DMA((N_DEV - 1, 3)),
                        pltpu.SemaphoreType.DMA((N_DEV - 1, 3)), pltpu.SemaphoreType.DMA((3,))],
        compiler_params=pltpu.CompilerParams(vmem_limit_bytes=VMEM_LIMIT),
    )(w_in, w_out, conv_w)


def _dw_in_rs(ht, dz, dw_out, small):
    seq = dz.shape[0]

    def body(cols_ref, ht_ref, dz_ref, dout_ref, sm_ref, own_ref, rin_ref, rout_ref, rsm_ref, to_sibling, landed,
             to_chip, zero_buf, d2d_send, d2d_recv, ici_send, ici_recv, side_send, side_recv, local_sems):
        del cols_ref
        step = pl.program_id(0)
        x, y, c = _coords()
        me = 4 * x + 2 * y + c
        sibling = (x, y, 1 - c)
        chips = [(1 - x, y), (x, 1 - y), (1 - x, 1 - y)]

        def d2d(n):
            return pltpu.make_async_remote_copy(src_ref=to_sibling.at[n], dst_ref=landed.at[n], send_sem=d2d_send.at[n],
                                                recv_sem=d2d_recv.at[n], device_id=sibling, device_id_type=MESH)

        def ici(n):
            return pltpu.make_async_remote_copy(src_ref=to_chip.at[n], dst_ref=rin_ref.at[n], send_sem=ici_send.at[n],
                                                recv_sem=ici_recv.at[n], device_id=(*chips[n], c), device_id_type=MESH)

        def side(k, mine):
            peer, peer_idx = _peer(k)
            src_slab, dst_slab = (peer_idx, me) if mine else (me, peer_idx)
            pairs = [(dout_ref.at[src_slab], rout_ref.at[dst_slab]), (sm_ref, rsm_ref.at[dst_slab])]
            return [pltpu.make_async_remote_copy(src_ref=src, dst_ref=dst, send_sem=side_send.at[k - 1, a],
                                                 recv_sem=side_recv.at[k - 1, a], device_id=peer, device_id_type=MESH)
                    for a, (src, dst) in enumerate(pairs)]

        local = [pltpu.make_async_copy(zero_buf, rout_ref.at[me], local_sems.at[0]),
                 pltpu.make_async_copy(sm_ref, rsm_ref.at[me], local_sems.at[1])]

        @pl.when(step == 0)
        def _():
            zero_buf[...] = jnp.zeros_like(zero_buf)
            for cp in local:
                cp.start()
            for k in range(1, N_DEV):
                for cp in side(k, mine=True):
                    cp.start()

        dw = jnp.dot(ht_ref[...], dz_ref[...], preferred_element_type=F32)
        for n, at in zip(range(4), (0, 1, 2, N_DEV - 2)):
            @pl.when(step == at)
            def _(n=n):
                to_sibling[n] = dw.astype(BF16)
                d2d(n).start()

        for n in range(3):
            @pl.when(step == 3 + n)
            def _(n=n):
                d2d(n).wait_recv()
                to_chip[n] = (dw + landed[n].astype(F32)).astype(BF16)
                ici(n).start()

        @pl.when(step == N_DEV - 1)
        def _():
            d2d(3).wait_recv()
            own_ref[...] = dw + landed[3].astype(F32)
            for n in range(3):
                ici(n).wait_recv()
            for k in range(1, N_DEV):
                for cp in side(k, mine=False):
                    cp.wait_recv()
            for n in range(4):
                d2d(n).wait_send()
            for n in range(3):
                ici(n).wait_send()
            for k in range(1, N_DEV):
                for cp in side(k, mine=True):
                    cp.wait_send()
            for cp in local:
                cp.wait()

    x, y, c = _coords()
    others = [(1 - x, y), (x, 1 - y), (1 - x, 1 - y)]
    order = [(*chip, 1 - c) for chip in others] + [(*chip, c) for chip in others] + [(x, y, 1 - c), (x, y, c)]
    cols = jnp.stack([4 * px + 2 * py + pc for px, py, pc in order]).astype(jnp.int32)
    slab = (D_MODEL, SHARD_IN)
    grid_spec = pltpu.PrefetchScalarGridSpec(
        num_scalar_prefetch=1, grid=(N_DEV,),
        in_specs=[pl.BlockSpec((D_MODEL, seq), lambda s, cols: (0, 0), pipeline_mode=pl.Buffered(1)),
                  pl.BlockSpec((seq, SHARD_IN), lambda s, cols: (0, cols[s])), HBM_SPEC, HBM_SPEC],
        out_specs=[pl.BlockSpec(slab, lambda s, cols: (0, 0)), HBM_SPEC, HBM_SPEC, HBM_SPEC],
        scratch_shapes=[pltpu.VMEM((4, *slab), BF16), pltpu.VMEM((4, *slab), BF16), pltpu.VMEM((3, *slab), BF16),
                        pltpu.VMEM((SHARD_OUT, D_MODEL), BF16),
                        pltpu.SemaphoreType.DMA((4,)), pltpu.SemaphoreType.DMA((4,)),
                        pltpu.SemaphoreType.DMA((3,)), pltpu.SemaphoreType.DMA((3,)),
                        pltpu.SemaphoreType.DMA((N_DEV - 1, 2)), pltpu.SemaphoreType.DMA((N_DEV - 1, 2)),
                        pltpu.SemaphoreType.DMA((2,))])
    return pl.pallas_call(
        body, name="dw_in_rs", grid_spec=grid_spec,
        out_shape=[jax.ShapeDtypeStruct(slab, F32),
                   jax.ShapeDtypeStruct((3, *slab), BF16),
                   jax.ShapeDtypeStruct((N_DEV, SHARD_OUT, D_MODEL), BF16),
                   jax.ShapeDtypeStruct((N_DEV, 8, D_MODEL), F32)],
        compiler_params=_params(1),
    )(cols, ht, dz, dw_out, small)


def _adamw_math(w, g, m, v):
    m = ADAM_B1 * m + (1.0 - ADAM_B1) * g
    v = ADAM_B2 * v + (1.0 - ADAM_B2) * (g * g)
    m_hat = m / (1.0 - ADAM_B1 ** ADAM_STEP)
    v_hat = v / (1.0 - ADAM_B2 ** ADAM_STEP)
    delta = -ADAM_LR * (m_hat / (jnp.sqrt(v_hat) + ADAM_EPS) + ADAM_WD * w)
    return delta, m, v


def _sum_slabs(ref, first=None):
    total = ref[0].astype(F32) if first is None else first + ref[0].astype(F32)
    for s in range(1, ref.shape[0]):
        total = total + ref[s].astype(F32)
    return total


def _adamw_slabs(parts, own, own_slab, w, m, v, name, tr):
    rows, cols = w.shape
    tile = pl.BlockSpec((tr, cols), lambda i, s: (i, 0))
    own_spec = tile if own_slab is None else pl.BlockSpec((1, tr, cols), lambda i, s: (s[0], i, 0))

    def body(s_ref, p_ref, own_ref, w_ref, m_ref, v_ref, g_ref, d_ref, nm_ref, nv_ref):
        del s_ref
        g = _sum_slabs(p_ref, own_ref[...].reshape(tr, cols))
        g_ref[...] = g
        d_ref[...], nm_ref[...], nv_ref[...] = _adamw_math(w_ref[...], g, m_ref[...], v_ref[...])

    slab = jnp.zeros((1,), jnp.int32) if own_slab is None else own_slab.reshape(1).astype(jnp.int32)
    grid_spec = pltpu.PrefetchScalarGridSpec(
        num_scalar_prefetch=1, grid=(rows // tr,),
        in_specs=[pl.BlockSpec((parts.shape[0], tr, cols), lambda i, s: (0, i, 0)), own_spec, tile, tile, tile],
        out_specs=[tile] * 4)
    return pl.pallas_call(
        body, name=name, grid_spec=grid_spec,
        out_shape=[jax.ShapeDtypeStruct((rows, cols), F32)] * 4,
        compiler_params=_params(1),
    )(slab, parts, own, w, m, v)


def _adamw_small(parts, me, pre, post, conv):
    n_conv = CONV_W // N_DEV

    def body(me_ref, p_ref, *refs):
        ins, (loss_ref, *outs) = refs[:9], refs[9:]
        sums = _sum_slabs(p_ref)
        loss_ref[...] = sums[2:3, 0:1]
        mine = pltpu.roll(sums[:, 0:CONV_W], (CONV_W - me_ref[0] * n_conv) % CONV_W, 1)[3:6, 0:n_conv]
        for n, g in enumerate((sums[0:1], sums[1:2], mine)):
            w_ref, m_ref, v_ref = ins[3 * n:3 * n + 3]
            outs[4 * n][...] = g
            for out, val in zip(outs[4 * n + 1:4 * n + 4], _adamw_math(w_ref[...], g, m_ref[...], v_ref[...])):
                out[...] = val

    row = jax.ShapeDtypeStruct((1, D_MODEL), F32)
    small = jax.ShapeDtypeStruct((3, n_conv), F32)
    return pl.pallas_call(
        body, name="adamw_small",
        in_specs=[pl.BlockSpec(memory_space=pltpu.SMEM)] + [VMEM_SPEC] * 10,
        out_shape=[jax.ShapeDtypeStruct((1, 1), F32)] + [row] * 8 + [small] * 4,
    )(me.reshape(1).astype(jnp.int32), parts, *pre, *post, *conv)


def kernel(x, norm_pre_g, w_in, conv_w, w_out, norm_post_g, loss_target, m_norm_pre_g, m_w_in, m_conv_w, m_w_out,
           m_norm_post_g, v_norm_pre_g, v_w_in, v_conv_w, v_w_out, v_norm_post_g):
    n_conv = CONV_W // N_DEV
    w_in_g, w_out_g, conv_g = _ag_weights(w_in, w_out, conv_w)
    conv_full = conv_g[:, 0:3, 0:n_conv].transpose(1, 0, 2).reshape(3, CONV_W)
    grad_x, ht, dz, dw_out, dw_out_bf, small = _local_step(x[0], loss_target[0], norm_pre_g, norm_post_g, w_in_g,
                                                           w_out_g.reshape(D_MODEL, D_MODEL), conv_full)
    own_in, r_in, r_out, r_small = _dw_in_rs(ht, dz, dw_out_bf.reshape(N_DEV, SHARD_OUT, D_MODEL), small)
    me = 4 * lax.axis_index("x") + 2 * lax.axis_index("y") + lax.axis_index("c")
    g_in, d_in, nm_in, nv_in = _adamw_slabs(r_in, own_in, None, w_in, m_w_in, v_w_in, "adamw_in", 256)
    g_out, d_out, nm_out, nv_out = _adamw_slabs(r_out, dw_out.reshape(N_DEV, SHARD_OUT, D_MODEL), me, w_out, m_w_out,
                                                v_w_out, "adamw_out", SHARD_OUT)
    vec = lambda a: a.reshape(1, D_MODEL)
    (loss, g_pre, d_pre, nm_pre, nv_pre, g_post, d_post, nm_post, nv_post, g_conv, d_conv, nm_conv,
     nv_conv) = _adamw_small(r_small, me, [vec(a) for a in (norm_pre_g, m_norm_pre_g, v_norm_pre_g)],
                             [vec(a) for a in (norm_post_g, m_norm_post_g, v_norm_post_g)],
                             (conv_w, m_conv_w, v_conv_w))
    flat = lambda a: a.reshape(D_MODEL)
    return (loss.reshape(()), grad_x[None], flat(g_pre), g_in, g_conv, g_out, flat(g_post),
            flat(d_pre), d_in, d_conv, d_out, flat(d_post),
            flat(nm_pre), nm_in, nm_conv, nm_out, flat(nm_post),
            flat(nv_pre), nv_in, nv_conv, nv_out, flat(nv_post))
```

```python
import functools

import jax
import jax.numpy as jnp
from jax import lax
from jax.experimental import pallas as pl
from jax.experimental.pallas import tpu as pltpu

F32 = jnp.float32
BF16 = jnp.bfloat16

D_MODEL = 1024
HEAD_DIM = 64
ATTN_W = 768
CONV_W = 256
IN_W = 4096
REST_W = IN_W - 3 * ATTN_W
BLK = 128
N_DEV = 8
SHARD_IN = IN_W // N_DEV
SHARD_OUT = D_MODEL // N_DEV
DILATIONS = (1, 4, 16)
ROPE_THETA = 10000.0
NORM_EPS = 1e-6
NEG = -1e30

ADAM_LR = 0.001
ADAM_B1 = 0.9
ADAM_B2 = 0.999
ADAM_EPS = 1e-08
ADAM_WD = 0.01
ADAM_STEP = 10

VMEM_LIMIT = 56 * 1024 * 1024
MESH = pl.DeviceIdType.MESH


def _params(n_grid):
    return pltpu.CompilerParams(dimension_semantics=("arbitrary",) * n_grid, vmem_limit_bytes=VMEM_LIMIT)


def _resident(shape):
    zeros = (0,) * len(shape)
    return pl.BlockSpec(shape, lambda *_: zeros, pipeline_mode=pl.Buffered(1))


def _sigmoid(a):
    return 1.0 / (1.0 + jnp.exp(-a))


def _swap_halves(t, first_half):
    return jnp.where(first_half, pltpu.roll(t, BLK - 32, 1), pltpu.roll(t, 32, 1))


def _rope_tables(seq, tm):
    half = HEAD_DIM // 2
    inv_freq = ROPE_THETA ** (-jnp.arange(half, dtype=F32) * 2.0 / HEAD_DIM)
    freq = jnp.concatenate([inv_freq] * 4)
    sign = jnp.concatenate([-jnp.ones(half, F32), jnp.ones(half, F32)] * 2)
    starts = (jnp.arange(seq // tm) * tm).astype(F32)[:, None] * freq[None, :]
    rows = jnp.arange(tm).astype(F32)[:, None] * freq[None, :]
    slab = lambda a: jnp.broadcast_to(a[:, None, :], (seq // tm, 8, BLK))
    return slab(jnp.cos(starts)), slab(jnp.sin(starts) * sign), jnp.cos(rows), jnp.sin(rows) * sign


def _rope_specs(tm):
    return [pl.BlockSpec((1, 8, BLK), lambda i: (i, 0, 0))] * 2 + [_resident((tm, BLK))] * 2


def _tile_rope(cos_start, sin_start, cos_row, sin_row):
    ca, sa, cb, sb = cos_start[0, 0:1, :], sin_start[0, 0:1, :], cos_row[...], sin_row[...]
    return ca * cb - sa * sb, sa * cb + ca * sb


N_CHUNK = ATTN_W // BLK


def _lanes(r, c):
    return slice(r * ATTN_W + c * BLK, r * ATTN_W + (c + 1) * BLK)


def _to_residues(src, chunk0, dst_refs, tmp, rows, dtype):
    assert DILATIONS == (1, 4, 16)
    dst1, dst4, dst16 = dst_refs
    n4, n16 = rows // 4, rows // 16
    for c in range(N_CHUNK):
        dst1[:, _lanes(0, c)] = src[chunk0 + c].astype(dtype)
        for r1 in range(4):
            tmp[c, r1 * n4:(r1 + 1) * n4, :] = src[chunk0 + c, pl.ds(r1, n4, stride=4), :]
        for r1 in range(4):
            dst4[:, _lanes(r1, c)] = tmp[c, r1 * n4:(r1 + 1) * n4, :].astype(dtype)
            for r2 in range(4):
                dst16[:, _lanes(4 * r2 + r1, c)] = tmp[c, pl.ds(r1 * n4 + r2, n16, stride=4), :].astype(dtype)


def _from_residue(src_ref, dst, dil, rows, tmp=None):
    assert dil in (4, 16)
    n4, n16 = rows // 4, rows // 16
    for c in range(N_CHUNK):
        for r1 in range(4):
            if dil == 4:
                piece = src_ref[:, _lanes(r1, c)].astype(F32)
            else:
                for r2 in range(4):
                    tmp[c, pl.ds(r1 * n4 + r2, n16, stride=4), :] = src_ref[:, _lanes(4 * r2 + r1, c)].astype(F32)
                piece = tmp[c, r1 * n4:(r1 + 1) * n4, :]
            dst[c, pl.ds(r1, n4, stride=4), :] = piece


def _residue_spec(tm, dil):
    return pl.BlockSpec((tm // dil, dil * ATTN_W), lambda i: (i, 0))


def _residue_shape(seq, dil, dtype):
    return jax.ShapeDtypeStruct((seq // dil, dil * ATTN_W), dtype)


def _fwd_in(x, g_pre, w_in_g, tm=512):
    seq = x.shape[0]
    n_dil = len(DILATIONS)

    def body(x_ref, g_ref, w_ref, ca_ref, sa_ref, cb_ref, sb_ref, *rest):
        qkv_refs, (zr_ref, ht_ref, qkv_scr, tmp) = rest[:3 * n_dil], rest[3 * n_dil:]
        xv = x_ref[...]
        r = lax.rsqrt(jnp.mean(xv * xv, axis=-1, keepdims=True) + NORM_EPS)
        hf = (xv * r) * g_ref[...]
        h = hf.astype(BF16)
        ht_ref[...] = h.T
        cos, sin = _tile_rope(ca_ref, sa_ref, cb_ref, sb_ref)
        first_half = (lax.broadcasted_iota(jnp.int32, (tm, BLK), 1) & 32) == 0

        def rope(t):
            return t * cos + _swap_halves(t, first_half) * sin

        def project(j):
            return jnp.dot(h, w_ref[j], preferred_element_type=F32)

        def place(j, zj):
            for n in range(SHARD_IN // BLK):
                chunk, t = j * (SHARD_IN // BLK) + n, zj[:, n * BLK:(n + 1) * BLK]
                if chunk < N_CHUNK:
                    qkv_scr[chunk] = rope(t) * HEAD_DIM ** -0.5
                elif chunk < 2 * N_CHUNK:
                    qkv_scr[chunk] = rope(t)
                elif chunk < 3 * N_CHUNK:
                    qkv_scr[chunk] = t
                else:
                    zr_ref[:, (chunk - 3 * N_CHUNK) * BLK:(chunk - 3 * N_CHUNK + 1) * BLK] = t.astype(BF16)

        ahead = project(0)
        for j in range(N_DEV):
            zj = ahead
            if j + 1 < N_DEV:
                ahead = project(j + 1)
            place(j, zj)
            for a in range(3):
                if (a + 1) * N_CHUNK - 1 in range(j * (SHARD_IN // BLK), (j + 1) * (SHARD_IN // BLK)):
                    _to_residues(qkv_scr, a * N_CHUNK, [qkv_refs[3 * n + a] for n in range(n_dil)], tmp, tm, BF16)

    row = lambda w: pl.BlockSpec((tm, w), lambda i: (i, 0))
    outs = pl.pallas_call(
        body, name="fwd_in", grid=(seq // tm,),
        in_specs=[row(D_MODEL), _resident((1, D_MODEL)), _resident((N_DEV, D_MODEL, SHARD_IN))] + _rope_specs(tm),
        out_specs=[_residue_spec(tm, dil) for dil in DILATIONS for _ in range(3)]
        + [row(REST_W), pl.BlockSpec((D_MODEL, tm), lambda i: (0, i))],
        out_shape=[_residue_shape(seq, dil, BF16) for dil in DILATIONS for _ in range(3)]
        + [jax.ShapeDtypeStruct((seq, REST_W), BF16), jax.ShapeDtypeStruct((D_MODEL, seq), BF16)],
        scratch_shapes=[pltpu.VMEM((3 * N_CHUNK, tm, BLK), F32), pltpu.VMEM((N_CHUNK, tm, BLK), F32)],
        compiler_params=_params(1),
    )(x, g_pre.reshape(1, D_MODEL), w_in_g, *_rope_tables(seq, tm))
    qkv = [tuple(outs[3 * n:3 * n + 3]) for n in range(n_dil)]
    return qkv, outs[3 * n_dil], outs[3 * n_dil + 1]


def _band_bias(first_block):
    kj = lax.broadcasted_iota(jnp.int32, (2 * BLK, BLK), 0)
    qi = lax.broadcasted_iota(jnp.int32, (2 * BLK, BLK), 1)
    valid = (kj >= qi) & (kj <= qi + BLK)
    bias = jnp.where(valid, 0.0, NEG).astype(BF16)
    bias_first = jnp.where(valid & (kj >= BLK), 0.0, NEG).astype(BF16)
    onehot = ((kj & (BLK - 1)) == qi).astype(F32).astype(BF16)
    return onehot, bias, jnp.where(first_block, bias_first, bias)


def _stack_heads(t):
    keep0 = (lax.broadcasted_iota(jnp.int32, t.shape, 1) < HEAD_DIM).astype(F32).astype(BF16)
    return jnp.concatenate([t * keep0, t * (1 - keep0)], axis=0)


def _unstack_heads(t2, head0):
    return jnp.where(head0, t2[:BLK], t2[BLK:])


def _rows_per_head(a, head0):
    b = pltpu.roll(a, HEAD_DIM, 1)
    rows = jnp.concatenate([jnp.where(head0, a, b), jnp.where(head0, b, a)], axis=0)
    return jnp.concatenate([rows, rows], axis=1)


FWD_UNITS, FWD_COLS = 64, 8
BWD_UNITS, BWD_COLS = 32, 4


def _attn_specs(length, dil, units, max_cols):
    n_blocks = length // BLK
    tb = min(units, n_blocks)
    nc = max(n for n in range(1, min(units // tb, max_cols) + 1) if (dil * N_CHUNK) % n == 0)
    assert n_blocks % tb == 0
    tile = pl.BlockSpec((tb * BLK, nc * BLK), lambda c, t: (t, c))
    prev = pl.BlockSpec((BLK, nc * BLK), lambda c, t: (jnp.maximum(t * tb - 1, 0), c))
    grid = (dil * N_CHUNK // nc, n_blocks // tb)
    return tb, nc, tile, prev, grid


def _window(prev_ref, cur_ref, j, cols):
    if j == 0:
        return jnp.concatenate([prev_ref[:, cols], cur_ref[0:BLK, cols]], axis=0)
    return cur_ref[(j - 1) * BLK:(j + 1) * BLK, cols]


def _attn_fwd(q, k, v, dil):
    length = q.shape[0]
    tb, nc, tile, prev, grid = _attn_specs(length, dil, FWD_UNITS, FWD_COLS)

    def body(q_ref, kc_ref, kp_ref, vc_ref, vp_ref, o_ref, lse_ref):
        head0 = lax.broadcasted_iota(jnp.int32, (BLK, BLK), 1) < HEAD_DIM
        onehot, bias, bias_start = _band_bias(pl.program_id(1) == 0)
        ones = jnp.ones((2 * BLK, BLK), BF16)

        def scores(c, j):
            rows, cols = slice(j * BLK, (j + 1) * BLK), slice(c * BLK, (c + 1) * BLK)
            q2 = jnp.concatenate([_stack_heads(q_ref[rows, cols]), onehot], axis=1)
            kk = jnp.concatenate([_window(kp_ref, kc_ref, j, cols), bias_start if j == 0 else bias], axis=1)
            return (lax.dot_general(q2, kk, (((1,), (1,)), ((), ())), preferred_element_type=F32),)

        def probabilities(c, j, s):
            m = jnp.max(s, axis=1, keepdims=True)
            return m, jnp.exp(s - m).astype(BF16)

        def outputs(c, j, m, p):
            rows, cols = slice(j * BLK, (j + 1) * BLK), slice(c * BLK, (c + 1) * BLK)
            vv = jnp.concatenate([_window(vp_ref, vc_ref, j, cols), ones], axis=1)
            pv = jnp.dot(p, vv, preferred_element_type=F32)
            den = pv[:, BLK:]
            o_ref[rows, cols] = _unstack_heads(pv[:, :BLK] / den, head0).astype(BF16)
            lse_ref[rows, cols] = _unstack_heads(m + jnp.log(den), head0)

        units = [(c, j) for c in range(nc) for j in range(tb)]
        stage1, stage2 = {}, {}
        for n in range(len(units) + 2):
            if n < len(units):
                stage1[n] = scores(*units[n])
            if 0 <= n - 1 < len(units):
                stage2[n - 1] = probabilities(*units[n - 1], *stage1.pop(n - 1))
            if 0 <= n - 2 < len(units):
                outputs(*units[n - 2], *stage2.pop(n - 2))

    return pl.pallas_call(
        body, name=f"attn_fwd_d{dil}", grid=grid,
        in_specs=[tile, tile, prev, tile, prev], out_specs=[tile, tile],
        out_shape=[jax.ShapeDtypeStruct(q.shape, BF16), jax.ShapeDtypeStruct(q.shape, F32)],
        compiler_params=_params(2),
    )(q, k, k, v, v)


def _attn_bwd(q, k, v, do, lse, delta, dil):
    length = q.shape[0]
    tb, nc, tile, prev, grid = _attn_specs(length, dil, BWD_UNITS, BWD_COLS)
    whole = pl.BlockSpec((length, nc * BLK), lambda c, t: (0, c))

    def body(q_ref, do_ref, lse_ref, dl_ref, kc_ref, kp_ref, vc_ref, vp_ref, dq_ref, dk_ref, dv_ref):
        t = pl.program_id(1)
        head0 = lax.broadcasted_iota(jnp.int32, (BLK, BLK), 1) < HEAD_DIM
        onehot, bias, bias_start = _band_bias(t == 0)

        def scores(c, j):
            rows, cols = slice(j * BLK, (j + 1) * BLK), slice(c * BLK, (c + 1) * BLK)
            q2 = _stack_heads(q_ref[rows, cols])
            do2 = _stack_heads(do_ref[rows, cols])
            kk = _window(kp_ref, kc_ref, j, cols)
            s = lax.dot_general(jnp.concatenate([q2, onehot], axis=1),
                                jnp.concatenate([kk, bias_start if j == 0 else bias], axis=1),
                                (((1,), (1,)), ((), ())), preferred_element_type=F32)
            dp = lax.dot_general(do2, _window(vp_ref, vc_ref, j, cols), (((1,), (1,)), ((), ())),
                                 preferred_element_type=F32)
            return q2, do2, kk, s, dp

        def probabilities(c, j, q2, do2, kk, s, dp):
            rows, cols = slice(j * BLK, (j + 1) * BLK), slice(c * BLK, (c + 1) * BLK)
            p = jnp.exp(s - _rows_per_head(lse_ref[rows, cols], head0))
            ds = (p * (dp - _rows_per_head(dl_ref[rows, cols].astype(F32), head0))).astype(BF16)
            return q2, do2, kk, p.astype(BF16), ds

        def gradients(c, j, q2, do2, kk, p, ds):
            rows, cols = slice(j * BLK, (j + 1) * BLK), slice(c * BLK, (c + 1) * BLK)
            dq2 = jnp.dot(ds, kk, preferred_element_type=F32)
            dq_ref[rows, cols] = (_unstack_heads(dq2, head0) * HEAD_DIM ** -0.5).astype(BF16)
            dk2 = lax.dot_general(ds, q2, (((0,), (0,)), ((), ())), preferred_element_type=F32)
            dv2 = lax.dot_general(p, do2, (((0,), (0,)), ((), ())), preferred_element_type=F32)
            own = pl.ds(pl.multiple_of((t * tb + j) * BLK, BLK), BLK)
            dk_ref[own, cols] = dk2[BLK:].astype(BF16)
            dv_ref[own, cols] = dv2[BLK:].astype(BF16)

            def add_to_previous():
                before = pl.ds(pl.multiple_of((t * tb + j - 1) * BLK, BLK), BLK)
                dk_ref[before, cols] = (dk_ref[before, cols].astype(F32) + dk2[:BLK]).astype(BF16)
                dv_ref[before, cols] = (dv_ref[before, cols].astype(F32) + dv2[:BLK]).astype(BF16)

            if j > 0:
                add_to_previous()
            elif grid[1] > 1:
                pl.when(t > 0)(add_to_previous)

        units = [(c, j) for c in range(nc) for j in range(tb)]
        stage1 = {0: scores(*units[0])}
        for n in range(len(units)):
            stage2 = probabilities(*units[n], *stage1.pop(n))
            if n + 1 < len(units):
                stage1[n + 1] = scores(*units[n + 1])
            gradients(*units[n], *stage2)

    return pl.pallas_call(
        body, name=f"attn_bwd_d{dil}", grid=grid,
        in_specs=[tile, tile, tile, tile, tile, prev, tile, prev], out_specs=[tile, whole, whole],
        out_shape=[jax.ShapeDtypeStruct(q.shape, BF16)] * 3,
        compiler_params=_params(2),
    )(q, do, lse, delta, k, k, v, v)


HALO = 16


def _halo_specs(tm, seq):
    before = lambda w: pl.BlockSpec((HALO, w), lambda i: (jnp.maximum(i * (tm // HALO) - 1, 0), 0))
    after = lambda w: pl.BlockSpec((HALO, w), lambda i: (jnp.minimum((i + 1) * (tm // HALO), seq // HALO - 1), 0))
    return before, after


def _conv_taps(u, before, tm):
    row = lax.broadcasted_iota(jnp.int32, u.shape, 0)
    last, last2 = before[HALO - 1:HALO, :], before[HALO - 2:HALO - 1, :]
    u1 = jnp.where(row == 0, last, pltpu.roll(u, 1, 0))
    u2 = jnp.where(row == 0, last2, jnp.where(row == 1, last, pltpu.roll(u, 2, 0)))
    return u1, u2


def _attn_combine(o_parts, lse_parts, zr, conv_w, tm=256):
    seq = zr.shape[0]
    a0, h0, b0, c0, g0 = 0, ATTN_W, ATTN_W + CONV_W, ATTN_W + 2 * CONV_W, ATTN_W + 3 * CONV_W

    def body(o1, o2, o3, l1, l2, l3, zr_ref, zp_ref, w_ref, mixed_ref, o_ref, lse1, lse2, lse3, *scr):
        i = pl.program_id(0)
        for src, dst, dil in zip((o2, o3, l2, l3), scr[:4], DILATIONS[1:] * 2):
            _from_residue(src, dst, dil, tm, tmp=scr[5])
        for c in range(N_CHUNK):
            cols = slice(c * BLK, (c + 1) * BLK)
            la, lb, lc = l1[:, cols], scr[2][c], scr[3][c]
            top = jnp.maximum(jnp.maximum(la, lb), lc)
            ea, eb, ec = jnp.exp(la - top), jnp.exp(lb - top), jnp.exp(lc - top)
            den = ea + eb + ec
            inv = 1.0 / den
            o = (ea * inv) * o1[:, cols].astype(F32) + (eb * inv) * scr[0][c] + (ec * inv) * scr[1][c]
            o_ref[:, cols] = o.astype(BF16)
            scr[4][c] = top + jnp.log(den)
            ga = zr_ref[:, cols].astype(F32)
            mixed_ref[:, cols] = (o * (ga * _sigmoid(ga))).astype(BF16)
        _to_residues(scr[4], 0, (lse1, lse2, lse3), scr[5], tm, F32)
        part = lambda ref, lo, hi: ref[:, lo:hi].astype(F32)
        u = part(zr_ref, c0, g0) * part(zr_ref, h0, b0)
        before = jnp.where(i > 0, part(zp_ref, c0, g0) * part(zp_ref, h0, b0), 0.0)
        u1, u2 = _conv_taps(u, before, tm)
        y = u2 * w_ref[0:1, :] + u1 * w_ref[1:2, :] + u * w_ref[2:3, :]
        gc = part(zr_ref, g0, REST_W)
        mixed_ref[:, ATTN_W:] = ((part(zr_ref, b0, c0) * y) * (gc * _sigmoid(gc))).astype(BF16)

    row = lambda w: pl.BlockSpec((tm, w), lambda i: (i, 0))
    before, _ = _halo_specs(tm, seq)
    views = [_residue_spec(tm, dil) for dil in DILATIONS]
    outs = pl.pallas_call(
        body, name="attn_combine", grid=(seq // tm,),
        in_specs=views * 2 + [row(REST_W), before(REST_W), _resident((3, CONV_W))],
        out_specs=[row(D_MODEL), row(ATTN_W)] + views,
        out_shape=[jax.ShapeDtypeStruct((seq, D_MODEL), BF16), jax.ShapeDtypeStruct((seq, ATTN_W), BF16)]
        + [_residue_shape(seq, dil, F32) for dil in DILATIONS],
        scratch_shapes=[pltpu.VMEM((N_CHUNK, tm, BLK), F32)] * 6,
        compiler_params=_params(1),
    )(*o_parts, *lse_parts, zr, zr, conv_w)
    return outs[0], outs[1], outs[2:]


def _out_loss_bwd(mixed, w_out_g, x, target, g_post, tm=1024, n_parts=4):
    seq = x.shape[0]

    def body(mx_ref, w_ref, x_ref, t_ref, g_ref, dout_ref, dmx_ref, dw_ref, dwb_ref, st_ref):
        i = pl.program_id(0)
        g = g_ref[...]
        parts = [slice(n * (tm // n_parts), (n + 1) * (tm // n_parts)) for n in range(n_parts)]

        def project(rows):
            return jnp.dot(mx_ref[rows, :], w_ref[...], preferred_element_type=F32)

        def head(rows, y):
            r = lax.rsqrt(jnp.mean(y * y, axis=-1, keepdims=True) + NORM_EPS)
            yhat = y * r
            err = (x_ref[rows, :] + yhat * g) - t_ref[rows, :]
            dn = err * (1.0 / D_MODEL)
            dout_ref[rows, :] = dn
            tg = dn * g
            dy = (r * (tg - yhat * jnp.mean(tg * yhat, axis=-1, keepdims=True))).astype(BF16)
            dmx_ref[rows, :] = lax.dot_general(dy, w_ref[...], (((1,), (1,)), ((), ())),
                                               preferred_element_type=F32).astype(BF16)
            return dy, jnp.sum(dn * yhat, axis=0, keepdims=True), jnp.sum(err * err)

        ahead, done = project(parts[0]), []
        for n, rows in enumerate(parts):
            y = ahead
            if n + 1 < n_parts:
                ahead = project(parts[n + 1])
            done.append(head(rows, y))
        dy = jnp.concatenate([d[0] for d in done], axis=0)
        dw = lax.dot_general(mx_ref[...], dy, (((0,), (0,)), ((), ())), preferred_element_type=F32)
        gsum = functools.reduce(lambda a, b: a + b, [d[1] for d in done])
        lsum = jnp.broadcast_to(0.5 / D_MODEL * functools.reduce(lambda a, b: a + b, [d[2] for d in done]),
                                (1, D_MODEL))

        @pl.when(i == 0)
        def _():
            dw_ref[...] = dw
            st_ref[...] = jnp.zeros_like(st_ref)
            st_ref[0:1, :] = gsum
            st_ref[1:2, :] = lsum

        @pl.when(i > 0)
        def _():
            dw_ref[...] += dw
            st_ref[0:1, :] += gsum
            st_ref[1:2, :] += lsum

        @pl.when(i == seq // tm - 1)
        def _():
            dwb_ref[...] = dw_ref[...].astype(BF16)

    row = lambda w: pl.BlockSpec((tm, w), lambda i: (i, 0))
    whole = pl.BlockSpec((D_MODEL, D_MODEL), lambda i: (0, 0))
    return pl.pallas_call(
        body, name="out_loss_bwd", grid=(seq // tm,),
        in_specs=[row(D_MODEL), _resident((D_MODEL, D_MODEL)), row(D_MODEL), row(D_MODEL), _resident((1, D_MODEL))],
        out_specs=[row(D_MODEL), row(D_MODEL), whole, whole, pl.BlockSpec((8, D_MODEL), lambda i: (0, 0))],
        out_shape=[jax.ShapeDtypeStruct((seq, D_MODEL), F32), jax.ShapeDtypeStruct((seq, D_MODEL), BF16),
                   jax.ShapeDtypeStruct((D_MODEL, D_MODEL), F32), jax.ShapeDtypeStruct((D_MODEL, D_MODEL), BF16),
                   jax.ShapeDtypeStruct((8, D_MODEL), F32)],
        compiler_params=_params(1),
    )(mixed, w_out_g, x, target, g_post.reshape(1, D_MODEL))


def _head_sum(prod, same_head):
    hi = prod.astype(BF16)
    lo = (prod - hi.astype(F32)).astype(BF16)
    return (jnp.dot(hi, same_head, preferred_element_type=F32) + jnp.dot(lo, same_head, preferred_element_type=F32))


def _gate_bwd(dmixed, zr, o, conv_w, tm=256):
    seq = zr.shape[0]
    n_tiles = seq // tm
    n_dil = len(DILATIONS)
    a0, h0, b0, c0, g0 = 0, ATTN_W, ATTN_W + CONV_W, ATTN_W + 2 * CONV_W, ATTN_W + 3 * CONV_W

    def body(dm_ref, dmn_ref, zr_ref, zp_ref, zn_ref, o_ref, w_ref, *rest):
        do_refs, dl_refs = rest[:n_dil], rest[n_dil:2 * n_dil]
        dz_ref, dw_ref, do_scr, dl_scr, tmp = rest[2 * n_dil:]
        i = pl.program_id(0)
        part = lambda ref, lo, hi: ref[:, lo:hi].astype(F32)
        ga = part(zr_ref, a0, h0)
        sg = _sigmoid(ga)
        dattn = part(dm_ref, 0, ATTN_W)
        ov = o_ref[...].astype(F32)
        do = dattn * (ga * sg)
        dz_ref[:, a0:h0] = (dattn * ov * (sg * (1.0 + ga * (1.0 - sg)))).astype(BF16)
        li = lax.broadcasted_iota(jnp.int32, (BLK, BLK), 0) // HEAD_DIM
        lj = lax.broadcasted_iota(jnp.int32, (BLK, BLK), 1) // HEAD_DIM
        same_head = (li == lj).astype(BF16)
        prod = do * ov
        for c in range(N_CHUNK):
            cols = slice(c * BLK, (c + 1) * BLK)
            do_scr[c] = do[:, cols]
            dl_scr[c] = _head_sum(prod[:, cols], same_head)
        _to_residues(do_scr, 0, do_refs, tmp, tm, BF16)
        _to_residues(dl_scr, 0, dl_refs, tmp, tm, BF16)

        ch, cb, cc, gc = (part(zr_ref, lo, hi) for lo, hi in ((h0, b0), (b0, c0), (c0, g0), (g0, REST_W)))
        u = cc * ch
        before = jnp.where(i > 0, part(zp_ref, c0, g0) * part(zp_ref, h0, b0), 0.0)
        u1, u2 = _conv_taps(u, before, tm)
        w0, w1, w2 = w_ref[0:1, :], w_ref[1:2, :], w_ref[2:3, :]
        y = u2 * w0 + u1 * w1 + u * w2
        sc = _sigmoid(gc)
        silu_c = gc * sc
        dconv = part(dm_ref, ATTN_W, D_MODEL)
        dz_ref[:, b0:c0] = (dconv * y * silu_c).astype(BF16)
        dz_ref[:, g0:] = (dconv * (cb * y) * (sc * (1.0 + gc * (1.0 - sc)))).astype(BF16)
        dy = dconv * cb * silu_c
        gn = part(zn_ref, g0, REST_W)
        after = jnp.where(i < n_tiles - 1,
                          part(dmn_ref, ATTN_W, D_MODEL) * part(zn_ref, b0, c0) * (gn * _sigmoid(gn)), 0.0)
        row = lax.broadcasted_iota(jnp.int32, dy.shape, 0)
        nxt, nxt2 = after[0:1, :], after[1:2, :]
        dy1 = jnp.where(row == tm - 1, nxt, pltpu.roll(dy, tm - 1, 0))
        dy2 = jnp.where(row == tm - 1, nxt2, jnp.where(row == tm - 2, nxt, pltpu.roll(dy, tm - 2, 0)))
        du = dy * w2 + dy1 * w1 + dy2 * w0
        dz_ref[:, c0:g0] = (du * ch).astype(BF16)
        dz_ref[:, h0:b0] = (du * cc).astype(BF16)
        dws = [jnp.sum(dy * u2, axis=0, keepdims=True), jnp.sum(dy * u1, axis=0, keepdims=True),
               jnp.sum(dy * u, axis=0, keepdims=True)]

        @pl.when(i == 0)
        def _():
            dw_ref[...] = jnp.zeros_like(dw_ref)

        for n, part in enumerate(dws):
            dw_ref[n:n + 1, :] += part

    row_spec = lambda w: pl.BlockSpec((tm, w), lambda i: (i, 0))
    before, after = _halo_specs(tm, seq)
    views = [_residue_spec(tm, dil) for dil in DILATIONS]
    outs = pl.pallas_call(
        body, name="gate_bwd", grid=(n_tiles,),
        in_specs=[row_spec(D_MODEL), after(D_MODEL), row_spec(REST_W), before(REST_W), after(REST_W),
                  row_spec(ATTN_W), _resident((3, CONV_W))],
        out_specs=views * 2 + [row_spec(REST_W), pl.BlockSpec((8, CONV_W), lambda i: (0, 0))],
        out_shape=[_residue_shape(seq, dil, BF16) for dil in DILATIONS] * 2
        + [jax.ShapeDtypeStruct((seq, REST_W), BF16), jax.ShapeDtypeStruct((8, CONV_W), F32)],
        scratch_shapes=[pltpu.VMEM((N_CHUNK, tm, BLK), F32)] * 3,
        compiler_params=_params(1),
    )(dmixed, dmixed, zr, zr, zr, o, conv_w)
    return outs[:n_dil], outs[n_dil:2 * n_dil], outs[2 * n_dil], outs[2 * n_dil + 1]


def _in_bwd(dqs, dks, dvs, dzr, x, d_out, g_pre, w_in_g, st_post, dconv, tm=256):
    seq = x.shape[0]

    def body(q1, q2, q3, k1, k2, k3, v1, v2, v3, dzr_ref, ca_ref, sa_ref, cb_ref, sb_ref, x_ref, dout_ref, g_ref,
             w_ref, post_ref, dconv_ref, dz_ref, gx_ref, st_ref, *scratch):
        i = pl.program_id(0)
        cos, sin = _tile_rope(ca_ref, sa_ref, cb_ref, sb_ref)
        first_half = (lax.broadcasted_iota(jnp.int32, (tm, BLK), 1) & 32) == 0
        streams = [(q1, q2, q3), (k1, k2, k3), (v1, v2, v3)]
        from4, from16, tmp = scratch[0:3], scratch[3:6], scratch[6]
        per_slab = SHARD_IN // BLK

        def unrope(t):
            return t * cos - _swap_halves(t, first_half) * sin

        def to_positions(a):
            _from_residue(streams[a][1], from4[a], 4, tm)
            _from_residue(streams[a][2], from16[a], 16, tm, tmp=tmp)

        def assemble(j):
            for chunk in range(j * per_slab, (j + 1) * per_slab):
                a, c = divmod(chunk, N_CHUNK)
                if a < 3:
                    total = streams[a][0][:, _lanes(0, c)].astype(F32) + from4[a][c] + from16[a][c]
                    val = (unrope(total) if a < 2 else total).astype(BF16)
                else:
                    val = dzr_ref[:, (chunk - 3 * N_CHUNK) * BLK:(chunk - 3 * N_CHUNK + 1) * BLK]
                dz_ref[:, chunk * BLK:(chunk + 1) * BLK] = val
            return dz_ref[:, j * SHARD_IN:(j + 1) * SHARD_IN]

        order = [j for j in range(N_DEV) if j * per_slab >= 3 * N_CHUNK]
        order += [j for j in range(N_DEV) if j not in order]
        assert order[2] * per_slab >= 3 * N_CHUNK
        ahead = assemble(order[0])
        dh = None
        for n, j in enumerate(order):
            part = lax.dot_general(ahead, w_ref[j], (((1,), (1,)), ((), ())), preferred_element_type=F32)
            if n < 3:
                to_positions(n)
            if n + 1 < N_DEV:
                ahead = assemble(order[n + 1])
            dh = part if dh is None else dh + part
        xv = x_ref[...]
        r = lax.rsqrt(jnp.mean(xv * xv, axis=-1, keepdims=True) + NORM_EPS)
        xhat = xv * r
        tg = dh * g_ref[...]
        gx_ref[...] = dout_ref[...] + r * (tg - xhat * jnp.mean(tg * xhat, axis=-1, keepdims=True))
        gsum = jnp.sum(dh * xhat, axis=0, keepdims=True)

        @pl.when(i == 0)
        def _():
            st_ref[...] = jnp.zeros_like(st_ref)
            st_ref[1:3, :] = post_ref[0:2, :]
            st_ref[3:6, 0:CONV_W] = dconv_ref[0:3, :]

        st_ref[0:1, :] += gsum

    row = lambda w: pl.BlockSpec((tm, w), lambda i: (i, 0))
    return pl.pallas_call(
        body, name="in_bwd", grid=(seq // tm,),
        in_specs=[_residue_spec(tm, dil) for dil in DILATIONS] * 3
        + [row(REST_W)] + _rope_specs(tm) + [row(D_MODEL), row(D_MODEL), _resident((1, D_MODEL)),
                                             _resident((N_DEV, D_MODEL, SHARD_IN)), _resident((8, D_MODEL)),
                                             _resident((8, CONV_W))],
        out_specs=[row(IN_W), row(D_MODEL), pl.BlockSpec((8, D_MODEL), lambda i: (0, 0))],
        out_shape=[jax.ShapeDtypeStruct((seq, IN_W), BF16), jax.ShapeDtypeStruct((seq, D_MODEL), F32),
                   jax.ShapeDtypeStruct((8, D_MODEL), F32)],
        scratch_shapes=[pltpu.VMEM((N_CHUNK, tm, BLK), F32)] * 7,
        compiler_params=_params(1),
    )(*dqs, *dks, *dvs, dzr, *_rope_tables(seq, tm), x, d_out, g_pre.reshape(1, D_MODEL), w_in_g, st_post, dconv)


def _local_step(x, target, g_pre, g_post, w_in_g, w_out_g, conv_w):
    qkv, zr, ht = _fwd_in(x, g_pre, w_in_g)
    parts = [_attn_fwd(*qkv[n], dil) for n, dil in enumerate(DILATIONS)]
    mixed, o, lse = _attn_combine([p[0] for p in parts], [p[1] for p in parts], zr, conv_w)
    d_out, dmixed, dw_out, dw_out_bf, st_post = _out_loss_bwd(mixed, w_out_g, x, target, g_post)
    do, delta, dzr, dconv = _gate_bwd(dmixed, zr, o, conv_w)
    grads = [_attn_bwd(*qkv[n], do[n], lse[n], delta[n], dil) for n, dil in enumerate(DILATIONS)]
    dz, grad_x, small = _in_bwd([g[0] for g in grads], [g[1] for g in grads], [g[2] for g in grads], dzr,
                                x, d_out, g_pre, w_in_g, st_post, dconv)
    return grad_x, ht, dz, dw_out, dw_out_bf, small


def _coords():
    return lax.axis_index("x"), lax.axis_index("y"), lax.axis_index("c")


def _peer(k):
    x, y, c = _coords()
    px = 1 - x if k & 4 else x
    py = 1 - y if k & 2 else y
    pc = 1 - c if k & 1 else c
    return (px, py, pc), 4 * px + 2 * py + pc


HBM_SPEC = pl.BlockSpec(memory_space=pltpu.HBM)
VMEM_SPEC = pl.BlockSpec(memory_space=pltpu.VMEM)


def _ag_weights(w_in, w_out, conv_w):
    def body(win_ref, wout_ref, cw_ref, gin_ref, gout_ref, gcw_ref, win_bf, wout_bf, cw_pad, send_sems, recv_sems,
             local_sems):
        x, y, c = _coords()
        me, sibling = (x, y, c), (x, y, 1 - c)
        flip = lambda v, yes: v + yes - 2 * v * yes
        x_nbr, y_nbr, diagonal = (1 - x, y, c), (x, 1 - y, c), (1 - x, 1 - y, c)
        relay_from = (flip(x, 1 - c), flip(y, c), c)
        relay_to = (flip(x, c), flip(y, 1 - c), c)
        slab = lambda px, py, pc: 4 * px + 2 * py + pc
        win_bf[...] = win_ref[...].astype(BF16)
        wout_bf[...] = wout_ref[...].astype(BF16)
        cw_pad[...] = jnp.zeros_like(cw_pad)
        cw_pad[0:3, 0:CONV_W // N_DEV] = cw_ref[...]
        mine = [win_bf, wout_bf, cw_pad]
        gathered = [gin_ref, gout_ref, gcw_ref]

        def copies(k, block, to, own=False):
            return [pltpu.make_async_remote_copy(src_ref=mine[a] if own else gathered[a].at[slab(*block)],
                                                 dst_ref=gathered[a].at[slab(*block)], send_sem=send_sems.at[k, a],
                                                 recv_sem=recv_sems.at[k, a], device_id=to, device_id_type=MESH)
                    for a in range(3)]

        local = [pltpu.make_async_copy(mine[a], gathered[a].at[slab(*me)], local_sems.at[a]) for a in range(3)]
        for cp in local:
            cp.start()
        started = copies(0, me, sibling, own=True) + copies(1, me, x_nbr, own=True) + copies(2, me, y_nbr, own=True)
        for cp in started:
            cp.start()
        for cp in copies(1, x_nbr, me) + copies(2, y_nbr, me):
            cp.wait_recv()
        onward = copies(3, relay_from, relay_to) + copies(4, x_nbr, sibling) + copies(5, y_nbr, sibling)
        for cp in onward:
            cp.start()
        for cp in copies(3, diagonal, me):
            cp.wait_recv()
        last = copies(6, diagonal, sibling)
        for cp in last:
            cp.start()
        for cp in copies(0, sibling, me):
            cp.wait_recv()
        for k, origin in ((4, (1 - x, y, 1 - c)), (5, (x, 1 - y, 1 - c)), (6, (1 - x, 1 - y, 1 - c))):
            for cp in copies(k, origin, me):
                cp.wait_recv()
        for cp in started + onward + last:
            cp.wait_send()
        for cp in local:
            cp.wait()

    return pl.pallas_call(
        body, name="ag_weights",
        in_specs=[VMEM_SPEC, VMEM_SPEC, VMEM_SPEC], out_specs=[HBM_SPEC, HBM_SPEC, HBM_SPEC],
        out_shape=[jax.ShapeDtypeStruct((N_DEV, D_MODEL, SHARD_IN), BF16),
                   jax.ShapeDtypeStruct((N_DEV, SHARD_OUT, D_MODEL), BF16),
                   jax.ShapeDtypeStruct((N_DEV, 8, BLK), F32)],
        scratch_shapes=[pltpu.VMEM((D_MODEL, SHARD_IN), BF16), pltpu.VMEM((SHARD_OUT, D_MODEL), BF16),
                        pltpu.VMEM((8, BLK), F32), pltpu.SemaphoreType.DMA((N_DEV - 1, 3)),
                        pltpu.SemaphoreType.DMA((N_DEV - 1, 3)), pltpu.SemaphoreType.DMA((3,))],
        compiler_params=pltpu.CompilerParams(vmem_limit_bytes=VMEM_LIMIT),
    )(w_in, w_out, conv_w)


def _dw_in_rs(ht, dz, dw_out, small):
    seq = dz.shape[0]

    def body(cols_ref, ht_ref, dz_ref, dout_ref, sm_ref, own_ref, rin_ref, rout_ref, rsm_ref, to_sibling, landed,
             to_chip, zero_buf, d2d_send, d2d_recv, ici_send, ici_recv, side_send, side_recv, local_sems):
        del cols_ref
        step = pl.program_id(0)
        x, y, c = _coords()
        me = 4 * x + 2 * y + c
        sibling = (x, y, 1 - c)
        chips = [(1 - x, y), (x, 1 - y), (1 - x, 1 - y)]

        def d2d(n):
            return pltpu.make_async_remote_copy(src_ref=to_sibling.at[n], dst_ref=landed.at[n], send_sem=d2d_send.at[n],
                                                recv_sem=d2d_recv.at[n], device_id=sibling, device_id_type=MESH)

        def ici(n):
            return pltpu.make_async_remote_copy(src_ref=to_chip.at[n], dst_ref=rin_ref.at[n], send_sem=ici_send.at[n],
                                                recv_sem=ici_recv.at[n], device_id=(*chips[n], c), device_id_type=MESH)

        def side(k, mine):
            peer, peer_idx = _peer(k)
            src_slab, dst_slab = (peer_idx, me) if mine else (me, peer_idx)
            pairs = [(dout_ref.at[src_slab], rout_ref.at[dst_slab]), (sm_ref, rsm_ref.at[dst_slab])]
            return [pltpu.make_async_remote_copy(src_ref=src, dst_ref=dst, send_sem=side_send.at[k - 1, a],
                                                 recv_sem=side_recv.at[k - 1, a], device_id=peer, device_id_type=MESH)
                    for a, (src, dst) in enumerate(pairs)]

        local = [pltpu.make_async_copy(zero_buf, rout_ref.at[me], local_sems.at[0]),
                 pltpu.make_async_copy(sm_ref, rsm_ref.at[me], local_sems.at[1])]

        @pl.when(step == 0)
        def _():
            zero_buf[...] = jnp.zeros_like(zero_buf)
            for cp in local:
                cp.start()
            for k in range(1, N_DEV):
                for cp in side(k, mine=True):
                    cp.start()

        dw = jnp.dot(ht_ref[...], dz_ref[...], preferred_element_type=F32)
        for n, at in zip(range(4), (0, 1, 2, N_DEV - 2)):
            @pl.when(step == at)
            def _(n=n):
                to_sibling[n] = dw.astype(BF16)
                d2d(n).start()

        for n in range(3):
            @pl.when(step == 3 + n)
            def _(n=n):
                d2d(n).wait_recv()
                to_chip[n] = (dw + landed[n].astype(F32)).astype(BF16)
                ici(n).start()

        @pl.when(step == N_DEV - 1)
        def _():
            d2d(3).wait_recv()
            own_ref[...] = dw + landed[3].astype(F32)
            for n in range(3):
                ici(n).wait_recv()
            for k in range(1, N_DEV):
                for cp in side(k, mine=False):
                    cp.wait_recv()
            for n in range(4):
                d2d(n).wait_send()
            for n in range(3):
                ici(n).wait_send()
            for k in range(1, N_DEV):
                for cp in side(k, mine=True):
                    cp.wait_send()
            for cp in local:
                cp.wait()

    x, y, c = _coords()
    others = [(1 - x, y), (x, 1 - y), (1 - x, 1 - y)]
    order = [(*chip, 1 - c) for chip in others] + [(*chip, c) for chip in others] + [(x, y, 1 - c), (x, y, c)]
    cols = jnp.stack([4 * px + 2 * py + pc for px, py, pc in order]).astype(jnp.int32)
    slab = (D_MODEL, SHARD_IN)
    grid_spec = pltpu.PrefetchScalarGridSpec(
        num_scalar_prefetch=1, grid=(N_DEV,),
        in_specs=[pl.BlockSpec((D_MODEL, seq), lambda s, cols: (0, 0), pipeline_mode=pl.Buffered(1)),
                  pl.BlockSpec((seq, SHARD_IN), lambda s, cols: (0, cols[s])), HBM_SPEC, HBM_SPEC],
        out_specs=[pl.BlockSpec(slab, lambda s, cols: (0, 0)), HBM_SPEC, HBM_SPEC, HBM_SPEC],
        scratch_shapes=[pltpu.VMEM((4, *slab), BF16), pltpu.VMEM((4, *slab), BF16), pltpu.VMEM((3, *slab), BF16),
                        pltpu.VMEM((SHARD_OUT, D_MODEL), BF16),
                        pltpu.SemaphoreType.DMA((4,)), pltpu.SemaphoreType.DMA((4,)),
                        pltpu.SemaphoreType.DMA((3,)), pltpu.SemaphoreType.DMA((3,)),
                        pltpu.SemaphoreType.DMA((N_DEV - 1, 2)), pltpu.SemaphoreType.DMA((N_DEV - 1, 2)),
                        pltpu.SemaphoreType.DMA((2,))])
    return pl.pallas_call(
        body, name="dw_in_rs", grid_spec=grid_spec,
        out_shape=[jax.ShapeDtypeStruct(slab, F32),
                   jax.ShapeDtypeStruct((3, *slab), BF16),
                   jax.ShapeDtypeStruct((N_DEV, SHARD_OUT, D_MODEL), BF16),
                   jax.ShapeDtypeStruct((N_DEV, 8, D_MODEL), F32)],
        compiler_params=_params(1),
    )(cols, ht, dz, dw_out, small)


def _adamw_math(w, g, m, v):
    m = ADAM_B1 * m + (1.0 - ADAM_B1) * g
    v = ADAM_B2 * v + (1.0 - ADAM_B2) * (g * g)
    m_hat = m / (1.0 - ADAM_B1 ** ADAM_STEP)
    v_hat = v / (1.0 - ADAM_B2 ** ADAM_STEP)
    delta = -ADAM_LR * (m_hat / (jnp.sqrt(v_hat) + ADAM_EPS) + ADAM_WD * w)
    return delta, m, v


def _sum_slabs(ref, first=None):
    total = ref[0].astype(F32) if first is None else first + ref[0].astype(F32)
    for s in range(1, ref.shape[0]):
        total = total + ref[s].astype(F32)
    return total


def _adamw_slabs(parts, own, own_slab, w, m, v, name, tr):
    rows, cols = w.shape
    tile = pl.BlockSpec((tr, cols), lambda i, s: (i, 0))
    own_spec = tile if own_slab is None else pl.BlockSpec((1, tr, cols), lambda i, s: (s[0], i, 0))

    def body(s_ref, p_ref, own_ref, w_ref, m_ref, v_ref, g_ref, d_ref, nm_ref, nv_ref):
        del s_ref
        g = _sum_slabs(p_ref, own_ref[...].reshape(tr, cols))
        g_ref[...] = g
        d_ref[...], nm_ref[...], nv_ref[...] = _adamw_math(w_ref[...], g, m_ref[...], v_ref[...])

    slab = jnp.zeros((1,), jnp.int32) if own_slab is None else own_slab.reshape(1).astype(jnp.int32)
    grid_spec = pltpu.PrefetchScalarGridSpec(
        num_scalar_prefetch=1, grid=(rows // tr,),
        in_specs=[pl.BlockSpec((parts.shape[0], tr, cols), lambda i, s: (0, i, 0)), own_spec, tile, tile, tile],
        out_specs=[tile] * 4)
    return pl.pallas_call(
        body, name=name, grid_spec=grid_spec,
        out_shape=[jax.ShapeDtypeStruct((rows, cols), F32)] * 4,
        compiler_params=_params(1),
    )(slab, parts, own, w, m, v)


def _adamw_small(parts, me, pre, post, conv):
    n_conv = CONV_W // N_DEV

    def body(me_ref, p_ref, *refs):
        ins, (loss_ref, *outs) = refs[:9], refs[9:]
        sums = _sum_slabs(p_ref)
        loss_ref[...] = sums[2:3, 0:1]
        mine = pltpu.roll(sums[:, 0:CONV_W], (CONV_W - me_ref[0] * n_conv) % CONV_W, 1)[3:6, 0:n_conv]
        for n, g in enumerate((sums[0:1], sums[1:2], mine)):
            w_ref, m_ref, v_ref = ins[3 * n:3 * n + 3]
            outs[4 * n][...] = g
            for out, val in zip(outs[4 * n + 1:4 * n + 4], _adamw_math(w_ref[...], g, m_ref[...], v_ref[...])):
                out[...] = val

    row = jax.ShapeDtypeStruct((1, D_MODEL), F32)
    small = jax.ShapeDtypeStruct((3, n_conv), F32)
    return pl.pallas_call(
        body, name="adamw_small",
        in_specs=[pl.BlockSpec(memory_space=pltpu.SMEM)] + [VMEM_SPEC] * 10,
        out_shape=[jax.ShapeDtypeStruct((1, 1), F32)] + [row] * 8 + [small] * 4,
    )(me.reshape(1).astype(jnp.int32), parts, *pre, *post, *conv)


def kernel(x, norm_pre_g, w_in, conv_w, w_out, norm_post_g, loss_target, m_norm_pre_g, m_w_in, m_conv_w, m_w_out,
           m_norm_post_g, v_norm_pre_g, v_w_in, v_conv_w, v_w_out, v_norm_post_g):
    n_conv = CONV_W // N_DEV
    w_in_g, w_out_g, conv_g = _ag_weights(w_in, w_out, conv_w)
    conv_full = conv_g[:, 0:3, 0:n_conv].transpose(1, 0, 2).reshape(3, CONV_W)
    grad_x, ht, dz, dw_out, dw_out_bf, small = _local_step(x[0], loss_target[0], norm_pre_g, norm_post_g, w_in_g,
                                                           w_out_g.reshape(D_MODEL, D_MODEL), conv_full)
    own_in, r_in, r_out, r_small = _dw_in_rs(ht, dz, dw_out_bf.reshape(N_DEV, SHARD_OUT, D_MODEL), small)
    me = 4 * lax.axis_index("x") + 2 * lax.axis_index("y") + lax.axis_index("c")
    g_in, d_in, nm_in, nv_in = _adamw_slabs(r_in, own_in, None, w_in, m_w_in, v_w_in, "adamw_in", 256)
    g_out, d_out, nm_out, nv_out = _adamw_slabs(r_out, dw_out.reshape(N_DEV, SHARD_OUT, D_MODEL), me, w_out, m_w_out,
                                                v_w_out, "adamw_out", SHARD_OUT)
    vec = lambda a: a.reshape(1, D_MODEL)
    (loss, g_pre, d_pre, nm_pre, nv_pre, g_post, d_post, nm_post, nv_post, g_conv, d_conv, nm_conv,
     nv_conv) = _adamw_small(r_small, me, [vec(a) for a in (norm_pre_g, m_norm_pre_g, v_norm_pre_g)],
                             [vec(a) for a in (norm_post_g, m_norm_post_g, v_norm_post_g)],
                             (conv_w, m_conv_w, v_conv_w))
    flat = lambda a: a.reshape(D_MODEL)
    return (loss.reshape(()), grad_x[None], flat(g_pre), g_in, g_conv, g_out, flat(g_post),
            flat(d_pre), d_in, d_conv, d_out, flat(d_post),
            flat(nm_pre), nm_in, nm_conv, nm_out, flat(nm_post),
            flat(nv_pre), nv_in, nv_conv, nv_out, flat(nv_post))
```

```python
import functools

import jax
import jax.numpy as jnp
from jax import lax
from jax.experimental import pallas as pl
from jax.experimental.pallas import tpu as pltpu

F32 = jnp.float32
BF16 = jnp.bfloat16

D_MODEL = 1024
HEAD_DIM = 64
ATTN_W = 768
CONV_W = 256
IN_W = 4096
REST_W = IN_W - 3 * ATTN_W
BLK = 128
N_DEV = 8
SHARD_IN = IN_W // N_DEV
SHARD_OUT = D_MODEL // N_DEV
DILATIONS = (1, 4, 16)
ROPE_THETA = 10000.0
NORM_EPS = 1e-6
NEG = -1e30

ADAM_LR = 0.001
ADAM_B1 = 0.9
ADAM_B2 = 0.999
ADAM_EPS = 1e-08
ADAM_WD = 0.01
ADAM_STEP = 10

VMEM_LIMIT = 56 * 1024 * 1024
MESH = pl.DeviceIdType.MESH


def _params(n_grid):
    return pltpu.CompilerParams(dimension_semantics=("arbitrary",) * n_grid, vmem_limit_bytes=VMEM_LIMIT)


def _resident(shape):
    zeros = (0,) * len(shape)
    return pl.BlockSpec(shape, lambda *_: zeros, pipeline_mode=pl.Buffered(1))


def _sigmoid(a):
    return 1.0 / (1.0 + jnp.exp(-a))


def _swap_halves(t, first_half):
    return jnp.where(first_half, pltpu.roll(t, BLK - 32, 1), pltpu.roll(t, 32, 1))


def _rope_tables(seq, tm):
    half = HEAD_DIM // 2
    inv_freq = ROPE_THETA ** (-jnp.arange(half, dtype=F32) * 2.0 / HEAD_DIM)
    freq = jnp.concatenate([inv_freq] * 4)
    sign = jnp.concatenate([-jnp.ones(half, F32), jnp.ones(half, F32)] * 2)
    starts = (jnp.arange(seq // tm) * tm).astype(F32)[:, None] * freq[None, :]
    rows = jnp.arange(tm).astype(F32)[:, None] * freq[None, :]
    slab = lambda a: jnp.broadcast_to(a[:, None, :], (seq // tm, 8, BLK))
    return slab(jnp.cos(starts)), slab(jnp.sin(starts) * sign), jnp.cos(rows), jnp.sin(rows) * sign


def _rope_specs(tm):
    return [pl.BlockSpec((1, 8, BLK), lambda i: (i, 0, 0))] * 2 + [_resident((tm, BLK))] * 2


def _tile_rope(cos_start, sin_start, cos_row, sin_row):
    ca, sa, cb, sb = cos_start[0, 0:1, :], sin_start[0, 0:1, :], cos_row[...], sin_row[...]
    return ca * cb - sa * sb, sa * cb + ca * sb


N_CHUNK = ATTN_W // BLK


def _lanes(r, c):
    return slice(r * ATTN_W + c * BLK, r * ATTN_W + (c + 1) * BLK)


def _to_residues(src, chunk0, dst_refs, tmp, rows, dtype):
    assert DILATIONS == (1, 4, 16)
    dst1, dst4, dst16 = dst_refs
    n4, n16 = rows // 4, rows // 16
    for c in range(N_CHUNK):
        dst1[:, _lanes(0, c)] = src[chunk0 + c].astype(dtype)
        for r1 in range(4):
            tmp[c, r1 * n4:(r1 + 1) * n4, :] = src[chunk0 + c, pl.ds(r1, n4, stride=4), :]
        for r1 in range(4):
            dst4[:, _lanes(r1, c)] = tmp[c, r1 * n4:(r1 + 1) * n4, :].astype(dtype)
            for r2 in range(4):
                dst16[:, _lanes(4 * r2 + r1, c)] = tmp[c, pl.ds(r1 * n4 + r2, n16, stride=4), :].astype(dtype)


def _from_residue(src_ref, dst, dil, rows, tmp=None):
    assert dil in (4, 16)
    n4, n16 = rows // 4, rows // 16
    for c in range(N_CHUNK):
        for r1 in range(4):
            if dil == 4:
                piece = src_ref[:, _lanes(r1, c)].astype(F32)
            else:
                for r2 in range(4):
                    tmp[c, pl.ds(r1 * n4 + r2, n16, stride=4), :] = src_ref[:, _lanes(4 * r2 + r1, c)].astype(F32)
                piece = tmp[c, r1 * n4:(r1 + 1) * n4, :]
            dst[c, pl.ds(r1, n4, stride=4), :] = piece


def _residue_spec(tm, dil):
    return pl.BlockSpec((tm // dil, dil * ATTN_W), lambda i: (i, 0))


def _residue_shape(seq, dil, dtype):
    return jax.ShapeDtypeStruct((seq // dil, dil * ATTN_W), dtype)


def _fwd_in(h, w_in_g, tm=512):
    seq = h.shape[0]
    n_dil = len(DILATIONS)

    def body(h_ref, w_ref, ca_ref, sa_ref, cb_ref, sb_ref, *rest):
        qkv_refs, (zr_ref, qkv_scr, tmp) = rest[:3 * n_dil], rest[3 * n_dil:]
        h = h_ref[...]
        cos, sin = _tile_rope(ca_ref, sa_ref, cb_ref, sb_ref)
        first_half = (lax.broadcasted_iota(jnp.int32, (tm, BLK), 1) & 32) == 0

        def rope(t):
            return t * cos + _swap_halves(t, first_half) * sin

        def project(j):
            return jnp.dot(h, w_ref[j], preferred_element_type=F32)

        def place(j, zj):
            for n in range(SHARD_IN // BLK):
                chunk, t = j * (SHARD_IN // BLK) + n, zj[:, n * BLK:(n + 1) * BLK]
                if chunk < N_CHUNK:
                    qkv_scr[chunk] = rope(t) * HEAD_DIM ** -0.5
                elif chunk < 2 * N_CHUNK:
                    qkv_scr[chunk] = rope(t)
                elif chunk < 3 * N_CHUNK:
                    qkv_scr[chunk] = t
                else:
                    zr_ref[:, (chunk - 3 * N_CHUNK) * BLK:(chunk - 3 * N_CHUNK + 1) * BLK] = t.astype(BF16)

        ahead = project(0)
        for j in range(N_DEV):
            zj = ahead
            if j + 1 < N_DEV:
                ahead = project(j + 1)
            place(j, zj)
            for a in range(3):
                if (a + 1) * N_CHUNK - 1 in range(j * (SHARD_IN // BLK), (j + 1) * (SHARD_IN // BLK)):
                    _to_residues(qkv_scr, a * N_CHUNK, [qkv_refs[3 * n + a] for n in range(n_dil)], tmp, tm, BF16)

    row = lambda w: pl.BlockSpec((tm, w), lambda i: (i, 0))
    outs = pl.pallas_call(
        body, name="fwd_in", grid=(seq // tm,),
        in_specs=[row(D_MODEL), _resident((N_DEV, D_MODEL, SHARD_IN))] + _rope_specs(tm),
        out_specs=[_residue_spec(tm, dil) for dil in DILATIONS for _ in range(3)] + [row(REST_W)],
        out_shape=[_residue_shape(seq, dil, BF16) for dil in DILATIONS for _ in range(3)]
        + [jax.ShapeDtypeStruct((seq, REST_W), BF16)],
        scratch_shapes=[pltpu.VMEM((3 * N_CHUNK, tm, BLK), F32), pltpu.VMEM((N_CHUNK, tm, BLK), F32)],
        compiler_params=_params(1),
    )(h, w_in_g, *_rope_tables(seq, tm))
    qkv = [tuple(outs[3 * n:3 * n + 3]) for n in range(n_dil)]
    return qkv, outs[3 * n_dil]


def _band_bias(first_block):
    kj = lax.broadcasted_iota(jnp.int32, (2 * BLK, BLK), 0)
    qi = lax.broadcasted_iota(jnp.int32, (2 * BLK, BLK), 1)
    valid = (kj >= qi) & (kj <= qi + BLK)
    bias = jnp.where(valid, 0.0, NEG).astype(BF16)
    bias_first = jnp.where(valid & (kj >= BLK), 0.0, NEG).astype(BF16)
    onehot = ((kj & (BLK - 1)) == qi).astype(F32).astype(BF16)
    return onehot, bias, jnp.where(first_block, bias_first, bias)


def _stack_heads(t):
    keep0 = (lax.broadcasted_iota(jnp.int32, t.shape, 1) < HEAD_DIM).astype(F32).astype(BF16)
    return jnp.concatenate([t * keep0, t * (1 - keep0)], axis=0)


def _unstack_heads(t2, head0):
    return jnp.where(head0, t2[:BLK], t2[BLK:])


def _rows_per_head(a, head0):
    b = pltpu.roll(a, HEAD_DIM, 1)
    rows = jnp.concatenate([jnp.where(head0, a, b), jnp.where(head0, b, a)], axis=0)
    return jnp.concatenate([rows, rows], axis=1)


FWD_UNITS, FWD_COLS = 64, 8
BWD_UNITS, BWD_COLS = 32, 4


def _attn_specs(length, dil, units, max_cols):
    n_blocks = length // BLK
    tb = min(units, n_blocks)
    nc = max(n for n in range(1, min(units // tb, max_cols) + 1) if (dil * N_CHUNK) % n == 0)
    assert n_blocks % tb == 0
    tile = pl.BlockSpec((tb * BLK, nc * BLK), lambda c, t: (t, c))
    prev = pl.BlockSpec((BLK, nc * BLK), lambda c, t: (jnp.maximum(t * tb - 1, 0), c))
    grid = (dil * N_CHUNK // nc, n_blocks // tb)
    return tb, nc, tile, prev, grid


def _window(prev_ref, cur_ref, j, cols):
    if j == 0:
        return jnp.concatenate([prev_ref[:, cols], cur_ref[0:BLK, cols]], axis=0)
    return cur_ref[(j - 1) * BLK:(j + 1) * BLK, cols]


def _attn_fwd(q, k, v, dil):
    length = q.shape[0]
    tb, nc, tile, prev, grid = _attn_specs(length, dil, FWD_UNITS, FWD_COLS)

    def body(q_ref, kc_ref, kp_ref, vc_ref, vp_ref, o_ref, lse_ref):
        head0 = lax.broadcasted_iota(jnp.int32, (BLK, BLK), 1) < HEAD_DIM
        onehot, bias, bias_start = _band_bias(pl.program_id(1) == 0)
        ones = jnp.ones((2 * BLK, BLK), BF16)

        def scores(c, j):
            rows, cols = slice(j * BLK, (j + 1) * BLK), slice(c * BLK, (c + 1) * BLK)
            q2 = jnp.concatenate([_stack_heads(q_ref[rows, cols]), onehot], axis=1)
            kk = jnp.concatenate([_window(kp_ref, kc_ref, j, cols), bias_start if j == 0 else bias], axis=1)
            return (lax.dot_general(q2, kk, (((1,), (1,)), ((), ())), preferred_element_type=F32),)

        def probabilities(c, j, s):
            m = jnp.max(s, axis=1, keepdims=True)
            return m, jnp.exp(s - m).astype(BF16)

        def outputs(c, j, m, p):
            rows, cols = slice(j * BLK, (j + 1) * BLK), slice(c * BLK, (c + 1) * BLK)
            vv = jnp.concatenate([_window(vp_ref, vc_ref, j, cols), ones], axis=1)
            pv = jnp.dot(p, vv, preferred_element_type=F32)
            den = pv[:, BLK:]
            o_ref[rows, cols] = _unstack_heads(pv[:, :BLK] / den, head0).astype(BF16)
            lse_ref[rows, cols] = _unstack_heads(m + jnp.log(den), head0)

        units = [(c, j) for c in range(nc) for j in range(tb)]
        stage1, stage2 = {}, {}
        for n in range(len(units) + 2):
            if n < len(units):
                stage1[n] = scores(*units[n])
            if 0 <= n - 1 < len(units):
                stage2[n - 1] = probabilities(*units[n - 1], *stage1.pop(n - 1))
            if 0 <= n - 2 < len(units):
                outputs(*units[n - 2], *stage2.pop(n - 2))

    return pl.pallas_call(
        body, name=f"attn_fwd_d{dil}", grid=grid,
        in_specs=[tile, tile, prev, tile, prev], out_specs=[tile, tile],
        out_shape=[jax.ShapeDtypeStruct(q.shape, BF16), jax.ShapeDtypeStruct(q.shape, F32)],
        compiler_params=_params(2),
    )(q, k, k, v, v)


def _attn_bwd(q, k, v, do, lse, delta, dil):
    length = q.shape[0]
    tb, nc, tile, prev, grid = _attn_specs(length, dil, BWD_UNITS, BWD_COLS)
    whole = pl.BlockSpec((length, nc * BLK), lambda c, t: (0, c))

    def body(q_ref, do_ref, lse_ref, dl_ref, kc_ref, kp_ref, vc_ref, vp_ref, dq_ref, dk_ref, dv_ref):
        t = pl.program_id(1)
        head0 = lax.broadcasted_iota(jnp.int32, (BLK, BLK), 1) < HEAD_DIM
        onehot, bias, bias_start = _band_bias(t == 0)

        def scores(c, j):
            rows, cols = slice(j * BLK, (j + 1) * BLK), slice(c * BLK, (c + 1) * BLK)
            q2 = _stack_heads(q_ref[rows, cols])
            do2 = _stack_heads(do_ref[rows, cols])
            kk = _window(kp_ref, kc_ref, j, cols)
            s = lax.dot_general(jnp.concatenate([q2, onehot], axis=1),
                                jnp.concatenate([kk, bias_start if j == 0 else bias], axis=1),
                                (((1,), (1,)), ((), ())), preferred_element_type=F32)
            dp = lax.dot_general(do2, _window(vp_ref, vc_ref, j, cols), (((1,), (1,)), ((), ())),
                                 preferred_element_type=F32)
            return q2, do2, kk, s, dp

        def probabilities(c, j, q2, do2, kk, s, dp):
            rows, cols = slice(j * BLK, (j + 1) * BLK), slice(c * BLK, (c + 1) * BLK)
            p = jnp.exp(s - _rows_per_head(lse_ref[rows, cols], head0))
            ds = (p * (dp - _rows_per_head(dl_ref[rows, cols].astype(F32), head0))).astype(BF16)
            return q2, do2, kk, p.astype(BF16), ds

        def gradients(c, j, q2, do2, kk, p, ds):
            rows, cols = slice(j * BLK, (j + 1) * BLK), slice(c * BLK, (c + 1) * BLK)
            dq2 = jnp.dot(ds, kk, preferred_element_type=F32)
            dq_ref[rows, cols] = (_unstack_heads(dq2, head0) * HEAD_DIM ** -0.5).astype(BF16)
            dk2 = lax.dot_general(ds, q2, (((0,), (0,)), ((), ())), preferred_element_type=F32)
            dv2 = lax.dot_general(p, do2, (((0,), (0,)), ((), ())), preferred_element_type=F32)
            own = pl.ds(pl.multiple_of((t * tb + j) * BLK, BLK), BLK)
            dk_ref[own, cols] = dk2[BLK:].astype(BF16)
            dv_ref[own, cols] = dv2[BLK:].astype(BF16)

            def add_to_previous():
                before = pl.ds(pl.multiple_of((t * tb + j - 1) * BLK, BLK), BLK)
                dk_ref[before, cols] = (dk_ref[before, cols].astype(F32) + dk2[:BLK]).astype(BF16)
                dv_ref[before, cols] = (dv_ref[before, cols].astype(F32) + dv2[:BLK]).astype(BF16)

            if j > 0:
                add_to_previous()
            elif grid[1] > 1:
                pl.when(t > 0)(add_to_previous)

        units = [(c, j) for c in range(nc) for j in range(tb)]
        stage1 = {0: scores(*units[0])}
        for n in range(len(units)):
            stage2 = probabilities(*units[n], *stage1.pop(n))
            if n + 1 < len(units):
                stage1[n + 1] = scores(*units[n + 1])
            gradients(*units[n], *stage2)

    return pl.pallas_call(
        body, name=f"attn_bwd_d{dil}", grid=grid,
        in_specs=[tile, tile, tile, tile, tile, prev, tile, prev], out_specs=[tile, whole, whole],
        out_shape=[jax.ShapeDtypeStruct(q.shape, BF16)] * 3,
        compiler_params=_params(2),
    )(q, do, lse, delta, k, k, v, v)


HALO = 16


def _halo_specs(tm, seq):
    before = lambda w: pl.BlockSpec((HALO, w), lambda i: (jnp.maximum(i * (tm // HALO) - 1, 0), 0))
    after = lambda w: pl.BlockSpec((HALO, w), lambda i: (jnp.minimum((i + 1) * (tm // HALO), seq // HALO - 1), 0))
    return before, after


def _conv_taps(u, before, tm):
    row = lax.broadcasted_iota(jnp.int32, u.shape, 0)
    last, last2 = before[HALO - 1:HALO, :], before[HALO - 2:HALO - 1, :]
    u1 = jnp.where(row == 0, last, pltpu.roll(u, 1, 0))
    u2 = jnp.where(row == 0, last2, jnp.where(row == 1, last, pltpu.roll(u, 2, 0)))
    return u1, u2


def _attn_combine(o_parts, lse_parts, zr, conv_w, tm=256):
    seq = zr.shape[0]
    a0, h0, b0, c0, g0 = 0, ATTN_W, ATTN_W + CONV_W, ATTN_W + 2 * CONV_W, ATTN_W + 3 * CONV_W

    def body(o1, o2, o3, l1, l2, l3, zr_ref, zp_ref, w_ref, mixed_ref, o_ref, lse1, lse2, lse3, *scr):
        i = pl.program_id(0)
        for src, dst, dil in zip((o2, o3, l2, l3), scr[:4], DILATIONS[1:] * 2):
            _from_residue(src, dst, dil, tm, tmp=scr[5])
        for c in range(N_CHUNK):
            cols = slice(c * BLK, (c + 1) * BLK)
            la, lb, lc = l1[:, cols], scr[2][c], scr[3][c]
            top = jnp.maximum(jnp.maximum(la, lb), lc)
            ea, eb, ec = jnp.exp(la - top), jnp.exp(lb - top), jnp.exp(lc - top)
            den = ea + eb + ec
            inv = 1.0 / den
            o = (ea * inv) * o1[:, cols].astype(F32) + (eb * inv) * scr[0][c] + (ec * inv) * scr[1][c]
            o_ref[:, cols] = o.astype(BF16)
            scr[4][c] = top + jnp.log(den)
            ga = zr_ref[:, cols].astype(F32)
            mixed_ref[:, cols] = (o * (ga * _sigmoid(ga))).astype(BF16)
        _to_residues(scr[4], 0, (lse1, lse2, lse3), scr[5], tm, F32)
        part = lambda ref, lo, hi: ref[:, lo:hi].astype(F32)
        u = part(zr_ref, c0, g0) * part(zr_ref, h0, b0)
        before = jnp.where(i > 0, part(zp_ref, c0, g0) * part(zp_ref, h0, b0), 0.0)
        u1, u2 = _conv_taps(u, before, tm)
        y = u2 * w_ref[0:1, :] + u1 * w_ref[1:2, :] + u * w_ref[2:3, :]
        gc = part(zr_ref, g0, REST_W)
        mixed_ref[:, ATTN_W:] = ((part(zr_ref, b0, c0) * y) * (gc * _sigmoid(gc))).astype(BF16)

    row = lambda w: pl.BlockSpec((tm, w), lambda i: (i, 0))
    before, _ = _halo_specs(tm, seq)
    views = [_residue_spec(tm, dil) for dil in DILATIONS]
    outs = pl.pallas_call(
        body, name="attn_combine", grid=(seq // tm,),
        in_specs=views * 2 + [row(REST_W), before(REST_W), _resident((3, CONV_W))],
        out_specs=[row(D_MODEL), row(ATTN_W)] + views,
        out_shape=[jax.ShapeDtypeStruct((seq, D_MODEL), BF16), jax.ShapeDtypeStruct((seq, ATTN_W), BF16)]
        + [_residue_shape(seq, dil, F32) for dil in DILATIONS],
        scratch_shapes=[pltpu.VMEM((N_CHUNK, tm, BLK), F32)] * 6,
        compiler_params=_params(1),
    )(*o_parts, *lse_parts, zr, zr, conv_w)
    return outs[0], outs[1], outs[2:]


def _out_loss_bwd(mixed, w_out_g, x, target, g_post, tm=512, n_parts=2):
    seq = x.shape[0]

    def body(mx_ref, w_ref, x_ref, t_ref, g_ref, dout_ref, dmx_ref, dw_ref, dwb_ref, st_ref):
        i = pl.program_id(0)
        g = g_ref[...]
        parts = [slice(n * (tm // n_parts), (n + 1) * (tm // n_parts)) for n in range(n_parts)]

        def project(rows):
            return jnp.dot(mx_ref[rows, :], w_ref[...], preferred_element_type=F32)

        def head(rows, y):
            r = lax.rsqrt(jnp.mean(y * y, axis=-1, keepdims=True) + NORM_EPS)
            yhat = y * r
            err = (x_ref[rows, :] + yhat * g) - t_ref[rows, :]
            dn = err * (1.0 / D_MODEL)
            dout_ref[rows, :] = dn
            tg = dn * g
            dy = (r * (tg - yhat * jnp.mean(tg * yhat, axis=-1, keepdims=True))).astype(BF16)
            dmx_ref[rows, :] = lax.dot_general(dy, w_ref[...], (((1,), (1,)), ((), ())),
                                               preferred_element_type=F32).astype(BF16)
            return dy, jnp.sum(dn * yhat, axis=0, keepdims=True), jnp.sum(err * err)

        ahead, done = project(parts[0]), []
        for n, rows in enumerate(parts):
            y = ahead
            if n + 1 < n_parts:
                ahead = project(parts[n + 1])
            done.append(head(rows, y))
        dy = jnp.concatenate([d[0] for d in done], axis=0)
        dw = lax.dot_general(mx_ref[...], dy, (((0,), (0,)), ((), ())), preferred_element_type=F32)
        gsum = functools.reduce(lambda a, b: a + b, [d[1] for d in done])
        lsum = jnp.broadcast_to(0.5 / D_MODEL * functools.reduce(lambda a, b: a + b, [d[2] for d in done]),
                                (1, D_MODEL))

        @pl.when(i == 0)
        def _():
            dw_ref[...] = dw
            st_ref[...] = jnp.zeros_like(st_ref)
            st_ref[0:1, :] = gsum
            st_ref[1:2, :] = lsum

        @pl.when(i > 0)
        def _():
            dw_ref[...] += dw
            st_ref[0:1, :] += gsum
            st_ref[1:2, :] += lsum

        @pl.when(i == seq // tm - 1)
        def _():
            dwb_ref[...] = dw_ref[...].astype(BF16)

    row = lambda w: pl.BlockSpec((tm, w), lambda i: (i, 0))
    whole = pl.BlockSpec((D_MODEL, D_MODEL), lambda i: (0, 0))
    return pl.pallas_call(
        body, name="out_loss_bwd", grid=(seq // tm,),
        in_specs=[row(D_MODEL), _resident((D_MODEL, D_MODEL)), row(D_MODEL), row(D_MODEL), _resident((1, D_MODEL))],
        out_specs=[row(D_MODEL), row(D_MODEL), whole, whole, pl.BlockSpec((8, D_MODEL), lambda i: (0, 0))],
        out_shape=[jax.ShapeDtypeStruct((seq, D_MODEL), F32), jax.ShapeDtypeStruct((seq, D_MODEL), BF16),
                   jax.ShapeDtypeStruct((D_MODEL, D_MODEL), F32), jax.ShapeDtypeStruct((D_MODEL, D_MODEL), BF16),
                   jax.ShapeDtypeStruct((8, D_MODEL), F32)],
        compiler_params=_params(1),
    )(mixed, w_out_g, x, target, g_post.reshape(1, D_MODEL))


def _head_sum(prod, same_head):
    hi = prod.astype(BF16)
    lo = (prod - hi.astype(F32)).astype(BF16)
    return (jnp.dot(hi, same_head, preferred_element_type=F32) + jnp.dot(lo, same_head, preferred_element_type=F32))


def _gate_bwd(dmixed, zr, o, conv_w, tm=256):
    seq = zr.shape[0]
    n_tiles = seq // tm
    n_dil = len(DILATIONS)
    a0, h0, b0, c0, g0 = 0, ATTN_W, ATTN_W + CONV_W, ATTN_W + 2 * CONV_W, ATTN_W + 3 * CONV_W

    def body(dm_ref, dmn_ref, zr_ref, zp_ref, zn_ref, o_ref, w_ref, *rest):
        do_refs, dl_refs = rest[:n_dil], rest[n_dil:2 * n_dil]
        dz_ref, dw_ref, do_scr, dl_scr, tmp = rest[2 * n_dil:]
        i = pl.program_id(0)
        part = lambda ref, lo, hi: ref[:, lo:hi].astype(F32)
        ga = part(zr_ref, a0, h0)
        sg = _sigmoid(ga)
        dattn = part(dm_ref, 0, ATTN_W)
        ov = o_ref[...].astype(F32)
        do = dattn * (ga * sg)
        dz_ref[:, a0:h0] = (dattn * ov * (sg * (1.0 + ga * (1.0 - sg)))).astype(BF16)
        li = lax.broadcasted_iota(jnp.int32, (BLK, BLK), 0) // HEAD_DIM
        lj = lax.broadcasted_iota(jnp.int32, (BLK, BLK), 1) // HEAD_DIM
        same_head = (li == lj).astype(BF16)
        prod = do * ov
        for c in range(N_CHUNK):
            cols = slice(c * BLK, (c + 1) * BLK)
            do_scr[c] = do[:, cols]
            dl_scr[c] = _head_sum(prod[:, cols], same_head)
        _to_residues(do_scr, 0, do_refs, tmp, tm, BF16)
        _to_residues(dl_scr, 0, dl_refs, tmp, tm, BF16)

        ch, cb, cc, gc = (part(zr_ref, lo, hi) for lo, hi in ((h0, b0), (b0, c0), (c0, g0), (g0, REST_W)))
        u = cc * ch
        before = jnp.where(i > 0, part(zp_ref, c0, g0) * part(zp_ref, h0, b0), 0.0)
        u1, u2 = _conv_taps(u, before, tm)
        w0, w1, w2 = w_ref[0:1, :], w_ref[1:2, :], w_ref[2:3, :]
        y = u2 * w0 + u1 * w1 + u * w2
        sc = _sigmoid(gc)
        silu_c = gc * sc
        dconv = part(dm_ref, ATTN_W, D_MODEL)
        dz_ref[:, b0:c0] = (dconv * y * silu_c).astype(BF16)
        dz_ref[:, g0:] = (dconv * (cb * y) * (sc * (1.0 + gc * (1.0 - sc)))).astype(BF16)
        dy = dconv * cb * silu_c
        gn = part(zn_ref, g0, REST_W)
        after = jnp.where(i < n_tiles - 1,
                          part(dmn_ref, ATTN_W, D_MODEL) * part(zn_ref, b0, c0) * (gn * _sigmoid(gn)), 0.0)
        row = lax.broadcasted_iota(jnp.int32, dy.shape, 0)
        nxt, nxt2 = after[0:1, :], after[1:2, :]
        dy1 = jnp.where(row == tm - 1, nxt, pltpu.roll(dy, tm - 1, 0))
        dy2 = jnp.where(row == tm - 1, nxt2, jnp.where(row == tm - 2, nxt, pltpu.roll(dy, tm - 2, 0)))
        du = dy * w2 + dy1 * w1 + dy2 * w0
        dz_ref[:, c0:g0] = (du * ch).astype(BF16)
        dz_ref[:, h0:b0] = (du * cc).astype(BF16)
        dws = [jnp.sum(dy * u2, axis=0, keepdims=True), jnp.sum(dy * u1, axis=0, keepdims=True),
               jnp.sum(dy * u, axis=0, keepdims=True)]

        @pl.when(i == 0)
        def _():
            dw_ref[...] = jnp.zeros_like(dw_ref)

        for n, part in enumerate(dws):
            dw_ref[n:n + 1, :] += part

    row_spec = lambda w: pl.BlockSpec((tm, w), lambda i: (i, 0))
    before, after = _halo_specs(tm, seq)
    views = [_residue_spec(tm, dil) for dil in DILATIONS]
    outs = pl.pallas_call(
        body, name="gate_bwd", grid=(n_tiles,),
        in_specs=[row_spec(D_MODEL), after(D_MODEL), row_spec(REST_W), before(REST_W), after(REST_W),
                  row_spec(ATTN_W), _resident((3, CONV_W))],
        out_specs=views * 2 + [row_spec(REST_W), pl.BlockSpec((8, CONV_W), lambda i: (0, 0))],
        out_shape=[_residue_shape(seq, dil, BF16) for dil in DILATIONS] * 2
        + [jax.ShapeDtypeStruct((seq, REST_W), BF16), jax.ShapeDtypeStruct((8, CONV_W), F32)],
        scratch_shapes=[pltpu.VMEM((N_CHUNK, tm, BLK), F32)] * 3,
        compiler_params=_params(1),
    )(dmixed, dmixed, zr, zr, zr, o, conv_w)
    return outs[:n_dil], outs[n_dil:2 * n_dil], outs[2 * n_dil], outs[2 * n_dil + 1]


def _in_bwd(dqs, dks, dvs, dzr, x, d_out, g_pre, w_in_g, st_post, dconv, tm=256):
    seq = x.shape[0]

    def body(q1, q2, q3, k1, k2, k3, v1, v2, v3, dzr_ref, ca_ref, sa_ref, cb_ref, sb_ref, x_ref, dout_ref, g_ref,
             w_ref, post_ref, dconv_ref, dz_ref, gx_ref, st_ref, *scratch):
        i = pl.program_id(0)
        cos, sin = _tile_rope(ca_ref, sa_ref, cb_ref, sb_ref)
        first_half = (lax.broadcasted_iota(jnp.int32, (tm, BLK), 1) & 32) == 0
        streams = [(q1, q2, q3), (k1, k2, k3), (v1, v2, v3)]
        from4, from16, tmp = scratch[0:3], scratch[3:6], scratch[6]
        per_slab = SHARD_IN // BLK

        def unrope(t):
            return t * cos - _swap_halves(t, first_half) * sin

        def to_positions(a):
            _from_residue(streams[a][1], from4[a], 4, tm)
            _from_residue(streams[a][2], from16[a], 16, tm, tmp=tmp)

        def assemble(j):
            for chunk in range(j * per_slab, (j + 1) * per_slab):
                a, c = divmod(chunk, N_CHUNK)
                if a < 3:
                    total = streams[a][0][:, _lanes(0, c)].astype(F32) + from4[a][c] + from16[a][c]
                    val = (unrope(total) if a < 2 else total).astype(BF16)
                else:
                    val = dzr_ref[:, (chunk - 3 * N_CHUNK) * BLK:(chunk - 3 * N_CHUNK + 1) * BLK]
                dz_ref[:, chunk * BLK:(chunk + 1) * BLK] = val
            return dz_ref[:, j * SHARD_IN:(j + 1) * SHARD_IN]

        order = [j for j in range(N_DEV) if j * per_slab >= 3 * N_CHUNK]
        order += [j for j in range(N_DEV) if j not in order]
        assert order[2] * per_slab >= 3 * N_CHUNK
        ahead = assemble(order[0])
        dh = None
        for n, j in enumerate(order):
            part = lax.dot_general(ahead, w_ref[j], (((1,), (1,)), ((), ())), preferred_element_type=F32)
            if n < 3:
                to_positions(n)
            if n + 1 < N_DEV:
                ahead = assemble(order[n + 1])
            dh = part if dh is None else dh + part
        xv = x_ref[...]
        r = lax.rsqrt(jnp.mean(xv * xv, axis=-1, keepdims=True) + NORM_EPS)
        xhat = xv * r
        tg = dh * g_ref[...]
        gx_ref[...] = dout_ref[...] + r * (tg - xhat * jnp.mean(tg * xhat, axis=-1, keepdims=True))
        gsum = jnp.sum(dh * xhat, axis=0, keepdims=True)

        @pl.when(i == 0)
        def _():
            st_ref[...] = jnp.zeros_like(st_ref)
            st_ref[1:3, :] = post_ref[0:2, :]
            st_ref[3:6, 0:CONV_W] = dconv_ref[0:3, :]

        st_ref[0:1, :] += gsum

    row = lambda w: pl.BlockSpec((tm, w), lambda i: (i, 0))
    return pl.pallas_call(
        body, name="in_bwd", grid=(seq // tm,),
        in_specs=[_residue_spec(tm, dil) for dil in DILATIONS] * 3
        + [row(REST_W)] + _rope_specs(tm) + [row(D_MODEL), row(D_MODEL), _resident((1, D_MODEL)),
                                             _resident((N_DEV, D_MODEL, SHARD_IN)), _resident((8, D_MODEL)),
                                             _resident((8, CONV_W))],
        out_specs=[row(IN_W), row(D_MODEL), pl.BlockSpec((8, D_MODEL), lambda i: (0, 0))],
        out_shape=[jax.ShapeDtypeStruct((seq, IN_W), BF16), jax.ShapeDtypeStruct((seq, D_MODEL), F32),
                   jax.ShapeDtypeStruct((8, D_MODEL), F32)],
        scratch_shapes=[pltpu.VMEM((N_CHUNK, tm, BLK), F32)] * 7,
        compiler_params=_params(1),
    )(*dqs, *dks, *dvs, dzr, *_rope_tables(seq, tm), x, d_out, g_pre.reshape(1, D_MODEL), w_in_g, st_post, dconv)


def _local_step(h, x, target, g_pre, g_post, w_in_g, w_out_g, conv_w):
    qkv, zr = _fwd_in(h, w_in_g)
    parts = [_attn_fwd(*qkv[n], dil) for n, dil in enumerate(DILATIONS)]
    mixed, o, lse = _attn_combine([p[0] for p in parts], [p[1] for p in parts], zr, conv_w)
    d_out, dmixed, dw_out, dw_out_bf, st_post = _out_loss_bwd(mixed, w_out_g, x, target, g_post)
    do, delta, dzr, dconv = _gate_bwd(dmixed, zr, o, conv_w)
    grads = [_attn_bwd(*qkv[n], do[n], lse[n], delta[n], dil) for n, dil in enumerate(DILATIONS)]
    dz, grad_x, small = _in_bwd([g[0] for g in grads], [g[1] for g in grads], [g[2] for g in grads], dzr,
                                x, d_out, g_pre, w_in_g, st_post, dconv)
    return grad_x, dz, dw_out, dw_out_bf, small


def _coords():
    return lax.axis_index("x"), lax.axis_index("y"), lax.axis_index("c")


def _peer(k):
    x, y, c = _coords()
    px = 1 - x if k & 4 else x
    py = 1 - y if k & 2 else y
    pc = 1 - c if k & 1 else c
    return (px, py, pc), 4 * px + 2 * py + pc


HBM_SPEC = pl.BlockSpec(memory_space=pltpu.HBM)
VMEM_SPEC = pl.BlockSpec(memory_space=pltpu.VMEM)


def _ag_weights(w_in, w_out, conv_w, xs, g_pre, tm=512):
    seq = xs.shape[0]
    n_steps = seq // tm
    assert n_steps >= 3

    def body(win_ref, wout_ref, cw_ref, x_ref, g_ref, gin_ref, gout_ref, gcw_ref, h_ref, ht_ref, win_bf, wout_bf,
             cw_pad, send_sems, recv_sems, local_sems):
        step = pl.program_id(0)
        x, y, c = _coords()
        me, sibling = (x, y, c), (x, y, 1 - c)
        flip = lambda v, yes: v + yes - 2 * v * yes
        x_nbr, y_nbr, diagonal = (1 - x, y, c), (x, 1 - y, c), (1 - x, 1 - y, c)
        relay_from = (flip(x, 1 - c), flip(y, c), c)
        relay_to = (flip(x, c), flip(y, 1 - c), c)
        slab = lambda px, py, pc: 4 * px + 2 * py + pc
        mine = [win_bf, wout_bf, cw_pad]
        gathered = [gin_ref, gout_ref, gcw_ref]

        def copies(k, block, to, own=False):
            return [pltpu.make_async_remote_copy(src_ref=mine[a] if own else gathered[a].at[slab(*block)],
                                                 dst_ref=gathered[a].at[slab(*block)], send_sem=send_sems.at[k, a],
                                                 recv_sem=recv_sems.at[k, a], device_id=to, device_id_type=MESH)
                    for a in range(3)]

        local = lambda: [pltpu.make_async_copy(mine[a], gathered[a].at[slab(*me)], local_sems.at[a]) for a in range(3)]
        first = lambda: copies(0, me, sibling, own=True) + copies(1, me, x_nbr, own=True) + copies(2, me, y_nbr, own=True)
        onward = lambda: copies(3, relay_from, relay_to) + copies(4, x_nbr, sibling) + copies(5, y_nbr, sibling)
        last = lambda: copies(6, diagonal, sibling)

        @pl.when(step == 0)
        def _():
            win_bf[...] = win_ref[...].astype(BF16)
            wout_bf[...] = wout_ref[...].astype(BF16)
            cw_pad[...] = jnp.zeros_like(cw_pad)
            cw_pad[0:3, 0:CONV_W // N_DEV] = cw_ref[...]
            for cp in local() + first():
                cp.start()

        xv = x_ref[...]
        r = lax.rsqrt(jnp.mean(xv * xv, axis=-1, keepdims=True) + NORM_EPS)
        h = ((xv * r) * g_ref[...]).astype(BF16)
        h_ref[...] = h
        ht_ref[...] = h.T

        @pl.when(step == n_steps // 2)
        def _():
            for cp in copies(1, x_nbr, me) + copies(2, y_nbr, me):
                cp.wait_recv()
            for cp in onward():
                cp.start()

        @pl.when(step == n_steps - 1)
        def _():
            for cp in copies(3, diagonal, me):
                cp.wait_recv()
            for cp in last():
                cp.start()
            for cp in copies(0, sibling, me):
                cp.wait_recv()
            for k, origin in ((4, (1 - x, y, 1 - c)), (5, (x, 1 - y, 1 - c)), (6, (1 - x, 1 - y, 1 - c))):
                for cp in copies(k, origin, me):
                    cp.wait_recv()
            for cp in first() + onward() + last():
                cp.wait_send()
            for cp in local():
                cp.wait()

    row = pl.BlockSpec((tm, D_MODEL), lambda i: (i, 0))
    return pl.pallas_call(
        body, name="ag_weights", grid=(n_steps,),
        in_specs=[_resident((D_MODEL, SHARD_IN)), _resident((SHARD_OUT, D_MODEL)), _resident((3, CONV_W // N_DEV)), row,
                  _resident((1, D_MODEL))],
        out_specs=[HBM_SPEC, HBM_SPEC, HBM_SPEC, row, pl.BlockSpec((D_MODEL, tm), lambda i: (0, i))],
        out_shape=[jax.ShapeDtypeStruct((N_DEV, D_MODEL, SHARD_IN), BF16),
                   jax.ShapeDtypeStruct((N_DEV, SHARD_OUT, D_MODEL), BF16),
                   jax.ShapeDtypeStruct((N_DEV, 8, BLK), F32),
                   jax.ShapeDtypeStruct((seq, D_MODEL), BF16), jax.ShapeDtypeStruct((D_MODEL, seq), BF16)],
        scratch_shapes=[pltpu.VMEM((D_MODEL, SHARD_IN), BF16), pltpu.VMEM((SHARD_OUT, D_MODEL), BF16),
                        pltpu.VMEM((8, BLK), F32), pltpu.SemaphoreType.DMA((N_DEV - 1, 3)),
                        pltpu.SemaphoreType.DMA((N_DEV - 1, 3)), pltpu.SemaphoreType.DMA((3,))],
        compiler_params=_params(1),
    )(w_in, w_out, conv_w, xs, g_pre.reshape(1, D_MODEL))


def _dw_in_rs(ht, dz, dw_out, small):
    seq = dz.shape[0]

    def body(cols_ref, ht_ref, dz_ref, dout_ref, sm_ref, own_ref, rin_ref, rout_ref, rsm_ref, to_sibling, landed,
             to_chip, zero_buf, d2d_send, d2d_recv, ici_send, ici_recv, side_send, side_recv, local_sems):
        del cols_ref
        step = pl.program_id(0)
        x, y, c = _coords()
        me = 4 * x + 2 * y + c
        sibling = (x, y, 1 - c)
        chips = [(1 - x, y), (x, 1 - y), (1 - x, 1 - y)]

        def d2d(n):
            return pltpu.make_async_remote_copy(src_ref=to_sibling.at[n], dst_ref=landed.at[n], send_sem=d2d_send.at[n],
                                                recv_sem=d2d_recv.at[n], device_id=sibling, device_id_type=MESH)

        def ici(n):
            return pltpu.make_async_remote_copy(src_ref=to_chip.at[n], dst_ref=rin_ref.at[n], send_sem=ici_send.at[n],
                                                recv_sem=ici_recv.at[n], device_id=(*chips[n], c), device_id_type=MESH)

        def side(k, mine):
            peer, peer_idx = _peer(k)
            src_slab, dst_slab = (peer_idx, me) if mine else (me, peer_idx)
            pairs = [(dout_ref.at[src_slab], rout_ref.at[dst_slab]), (sm_ref, rsm_ref.at[dst_slab])]
            return [pltpu.make_async_remote_copy(src_ref=src, dst_ref=dst, send_sem=side_send.at[k - 1, a],
                                                 recv_sem=side_recv.at[k - 1, a], device_id=peer, device_id_type=MESH)
                    for a, (src, dst) in enumerate(pairs)]

        local = [pltpu.make_async_copy(zero_buf, rout_ref.at[me], local_sems.at[0]),
                 pltpu.make_async_copy(sm_ref, rsm_ref.at[me], local_sems.at[1])]

        @pl.when(step == 0)
        def _():
            zero_buf[...] = jnp.zeros_like(zero_buf)
            for cp in local:
                cp.start()
            for k in range(1, N_DEV):
                for cp in side(k, mine=True):
                    cp.start()

        dw = jnp.dot(ht_ref[...], dz_ref[...], preferred_element_type=F32)
        for n, at in zip(range(4), (0, 1, 2, N_DEV - 2)):
            @pl.when(step == at)
            def _(n=n):
                to_sibling[n] = dw.astype(BF16)
                d2d(n).start()

        for n in range(3):
            @pl.when(step == 3 + n)
            def _(n=n):
                d2d(n).wait_recv()
                to_chip[n] = (dw + landed[n].astype(F32)).astype(BF16)
                ici(n).start()

        @pl.when(step == N_DEV - 1)
        def _():
            d2d(3).wait_recv()
            own_ref[...] = dw + landed[3].astype(F32)
            for n in range(3):
                ici(n).wait_recv()
            for k in range(1, N_DEV):
                for cp in side(k, mine=False):
                    cp.wait_recv()
            for n in range(4):
                d2d(n).wait_send()
            for n in range(3):
                ici(n).wait_send()
            for k in range(1, N_DEV):
                for cp in side(k, mine=True):
                    cp.wait_send()
            for cp in local:
                cp.wait()

    x, y, c = _coords()
    others = [(1 - x, y), (x, 1 - y), (1 - x, 1 - y)]
    order = [(*chip, 1 - c) for chip in others] + [(*chip, c) for chip in others] + [(x, y, 1 - c), (x, y, c)]
    cols = jnp.stack([4 * px + 2 * py + pc for px, py, pc in order]).astype(jnp.int32)
    slab = (D_MODEL, SHARD_IN)
    grid_spec = pltpu.PrefetchScalarGridSpec(
        num_scalar_prefetch=1, grid=(N_DEV,),
        in_specs=[pl.BlockSpec((D_MODEL, seq), lambda s, cols: (0, 0), pipeline_mode=pl.Buffered(1)),
                  pl.BlockSpec((seq, SHARD_IN), lambda s, cols: (0, cols[s])), HBM_SPEC, HBM_SPEC],
        out_specs=[pl.BlockSpec(slab, lambda s, cols: (0, 0)), HBM_SPEC, HBM_SPEC, HBM_SPEC],
        scratch_shapes=[pltpu.VMEM((4, *slab), BF16), pltpu.VMEM((4, *slab), BF16), pltpu.VMEM((3, *slab), BF16),
                        pltpu.VMEM((SHARD_OUT, D_MODEL), BF16),
                        pltpu.SemaphoreType.DMA((4,)), pltpu.SemaphoreType.DMA((4,)),
                        pltpu.SemaphoreType.DMA((3,)), pltpu.SemaphoreType.DMA((3,)),
                        pltpu.SemaphoreType.DMA((N_DEV - 1, 2)), pltpu.SemaphoreType.DMA((N_DEV - 1, 2)),
                        pltpu.SemaphoreType.DMA((2,))])
    return pl.pallas_call(
        body, name="dw_in_rs", grid_spec=grid_spec,
        out_shape=[jax.ShapeDtypeStruct(slab, F32),
                   jax.ShapeDtypeStruct((3, *slab), BF16),
                   jax.ShapeDtypeStruct((N_DEV, SHARD_OUT, D_MODEL), BF16),
                   jax.ShapeDtypeStruct((N_DEV, 8, D_MODEL), F32)],
        compiler_params=_params(1),
    )(cols, ht, dz, dw_out, small)


def _adamw_math(w, g, m, v):
    m = ADAM_B1 * m + (1.0 - ADAM_B1) * g
    v = ADAM_B2 * v + (1.0 - ADAM_B2) * (g * g)
    m_hat = m / (1.0 - ADAM_B1 ** ADAM_STEP)
    v_hat = v / (1.0 - ADAM_B2 ** ADAM_STEP)
    delta = -ADAM_LR * (m_hat / (jnp.sqrt(v_hat) + ADAM_EPS) + ADAM_WD * w)
    return delta, m, v


def _sum_slabs(ref, first=None):
    total = ref[0].astype(F32) if first is None else first + ref[0].astype(F32)
    for s in range(1, ref.shape[0]):
        total = total + ref[s].astype(F32)
    return total


def _adamw_slabs(parts, own, own_slab, w, m, v, name, tr):
    rows, cols = w.shape
    tile = pl.BlockSpec((tr, cols), lambda i, s: (i, 0))
    own_spec = tile if own_slab is None else pl.BlockSpec((1, tr, cols), lambda i, s: (s[0], i, 0))

    def body(s_ref, p_ref, own_ref, w_ref, m_ref, v_ref, g_ref, d_ref, nm_ref, nv_ref):
        del s_ref
        g = _sum_slabs(p_ref, own_ref[...].reshape(tr, cols))
        g_ref[...] = g
        d_ref[...], nm_ref[...], nv_ref[...] = _adamw_math(w_ref[...], g, m_ref[...], v_ref[...])

    slab = jnp.zeros((1,), jnp.int32) if own_slab is None else own_slab.reshape(1).astype(jnp.int32)
    grid_spec = pltpu.PrefetchScalarGridSpec(
        num_scalar_prefetch=1, grid=(rows // tr,),
        in_specs=[pl.BlockSpec((parts.shape[0], tr, cols), lambda i, s: (0, i, 0)), own_spec, tile, tile, tile],
        out_specs=[tile] * 4)
    return pl.pallas_call(
        body, name=name, grid_spec=grid_spec,
        out_shape=[jax.ShapeDtypeStruct((rows, cols), F32)] * 4,
        compiler_params=_params(1),
    )(slab, parts, own, w, m, v)


def _adamw_small(parts, me, pre, post, conv):
    n_conv = CONV_W // N_DEV

    def body(me_ref, p_ref, *refs):
        ins, (loss_ref, *outs) = refs[:9], refs[9:]
        sums = _sum_slabs(p_ref)
        loss_ref[...] = sums[2:3, 0:1]
        mine = pltpu.roll(sums[:, 0:CONV_W], (CONV_W - me_ref[0] * n_conv) % CONV_W, 1)[3:6, 0:n_conv]
        for n, g in enumerate((sums[0:1], sums[1:2], mine)):
            w_ref, m_ref, v_ref = ins[3 * n:3 * n + 3]
            outs[4 * n][...] = g
            for out, val in zip(outs[4 * n + 1:4 * n + 4], _adamw_math(w_ref[...], g, m_ref[...], v_ref[...])):
                out[...] = val

    row = jax.ShapeDtypeStruct((1, D_MODEL), F32)
    small = jax.ShapeDtypeStruct((3, n_conv), F32)
    return pl.pallas_call(
        body, name="adamw_small",
        in_specs=[pl.BlockSpec(memory_space=pltpu.SMEM)] + [VMEM_SPEC] * 10,
        out_shape=[jax.ShapeDtypeStruct((1, 1), F32)] + [row] * 8 + [small] * 4,
    )(me.reshape(1).astype(jnp.int32), parts, *pre, *post, *conv)


def kernel(x, norm_pre_g, w_in, conv_w, w_out, norm_post_g, loss_target, m_norm_pre_g, m_w_in, m_conv_w, m_w_out,
           m_norm_post_g, v_norm_pre_g, v_w_in, v_conv_w, v_w_out, v_norm_post_g):
    n_conv = CONV_W // N_DEV
    w_in_g, w_out_g, conv_g, h, ht = _ag_weights(w_in, w_out, conv_w, x[0], norm_pre_g)
    conv_full = conv_g[:, 0:3, 0:n_conv].transpose(1, 0, 2).reshape(3, CONV_W)
    grad_x, dz, dw_out, dw_out_bf, small = _local_step(h, x[0], loss_target[0], norm_pre_g, norm_post_g, w_in_g,
                                                       w_out_g.reshape(D_MODEL, D_MODEL), conv_full)
    own_in, r_in, r_out, r_small = _dw_in_rs(ht, dz, dw_out_bf.reshape(N_DEV, SHARD_OUT, D_MODEL), small)
    me = 4 * lax.axis_index("x") + 2 * lax.axis_index("y") + lax.axis_index("c")
    g_in, d_in, nm_in, nv_in = _adamw_slabs(r_in, own_in, None, w_in, m_w_in, v_w_in, "adamw_in", 256)
    g_out, d_out, nm_out, nv_out = _adamw_slabs(r_out, dw_out.reshape(N_DEV, SHARD_OUT, D_MODEL), me, w_out, m_w_out,
                                                v_w_out, "adamw_out", SHARD_OUT)
    vec = lambda a: a.reshape(1, D_MODEL)
    (loss, g_pre, d_pre, nm_pre, nv_pre, g_post, d_post, nm_post, nv_post, g_conv, d_conv, nm_conv,
     nv_conv) = _adamw_small(r_small, me, [vec(a) for a in (norm_pre_g, m_norm_pre_g, v_norm_pre_g)],
                             [vec(a) for a in (norm_post_g, m_norm_post_g, v_norm_post_g)],
                             (conv_w, m_conv_w, v_conv_w))
    flat = lambda a: a.reshape(D_MODEL)
    return (loss.reshape(()), grad_x[None], flat(g_pre), g_in, g_conv, g_out, flat(g_post),
            flat(d_pre), d_in, d_conv, d_out, flat(d_post),
            flat(nm_pre), nm_in, nm_conv, nm_out, flat(nm_post),
            flat(nv_pre), nv_in, nv_conv, nv_out, flat(nv_post))
```

```python
import functools

import jax
import jax.numpy as jnp
from jax import lax
from jax.experimental import pallas as pl
from jax.experimental.pallas import tpu as pltpu

F32 = jnp.float32
BF16 = jnp.bfloat16

D_MODEL = 1024
HEAD_DIM = 64
ATTN_W = 768
CONV_W = 256
IN_W = 4096
REST_W = IN_W - 3 * ATTN_W
BLK = 128
N_DEV = 8
SHARD_IN = IN_W // N_DEV
SHARD_OUT = D_MODEL // N_DEV
DILATIONS = (1, 4, 16)
ROPE_THETA = 10000.0
NORM_EPS = 1e-6
NEG = -1e30

ADAM_LR = 0.001
ADAM_B1 = 0.9
ADAM_B2 = 0.999
ADAM_EPS = 1e-08
ADAM_WD = 0.01
ADAM_STEP = 10

VMEM_LIMIT = 56 * 1024 * 1024
MESH = pl.DeviceIdType.MESH


def _params(n_grid):
    return pltpu.CompilerParams(dimension_semantics=("arbitrary",) * n_grid, vmem_limit_bytes=VMEM_LIMIT)


def _resident(shape):
    zeros = (0,) * len(shape)
    return pl.BlockSpec(shape, lambda *_: zeros, pipeline_mode=pl.Buffered(1))


def _sigmoid(a):
    return 1.0 / (1.0 + jnp.exp(-a))


def _swap_halves(t, first_half):
    return jnp.where(first_half, pltpu.roll(t, BLK - 32, 1), pltpu.roll(t, 32, 1))


def _rope_tables(seq, tm):
    half = HEAD_DIM // 2
    inv_freq = ROPE_THETA ** (-jnp.arange(half, dtype=F32) * 2.0 / HEAD_DIM)
    freq = jnp.concatenate([inv_freq] * 4)
    sign = jnp.concatenate([-jnp.ones(half, F32), jnp.ones(half, F32)] * 2)
    starts = (jnp.arange(seq // tm) * tm).astype(F32)[:, None] * freq[None, :]
    rows = jnp.arange(tm).astype(F32)[:, None] * freq[None, :]
    slab = lambda a: jnp.broadcast_to(a[:, None, :], (seq // tm, 8, BLK))
    return slab(jnp.cos(starts)), slab(jnp.sin(starts) * sign), jnp.cos(rows), jnp.sin(rows) * sign


def _rope_specs(tm):
    return [pl.BlockSpec((1, 8, BLK), lambda i: (i, 0, 0))] * 2 + [_resident((tm, BLK))] * 2


def _tile_rope(cos_start, sin_start, cos_row, sin_row):
    ca, sa, cb, sb = cos_start[0, 0:1, :], sin_start[0, 0:1, :], cos_row[...], sin_row[...]
    return ca * cb - sa * sb, sa * cb + ca * sb


N_CHUNK = ATTN_W // BLK


def _lanes(r, c):
    return slice(r * ATTN_W + c * BLK, r * ATTN_W + (c + 1) * BLK)


def _to_residues(src, chunk0, dst_refs, tmp, rows, dtype):
    assert DILATIONS == (1, 4, 16)
    dst1, dst4, dst16 = dst_refs
    n4, n16 = rows // 4, rows // 16
    for c in range(N_CHUNK):
        dst1[:, _lanes(0, c)] = src[chunk0 + c].astype(dtype)
        for r1 in range(4):
            tmp[c, r1 * n4:(r1 + 1) * n4, :] = src[chunk0 + c, pl.ds(r1, n4, stride=4), :]
        for r1 in range(4):
            dst4[:, _lanes(r1, c)] = tmp[c, r1 * n4:(r1 + 1) * n4, :].astype(dtype)
            for r2 in range(4):
                dst16[:, _lanes(4 * r2 + r1, c)] = tmp[c, pl.ds(r1 * n4 + r2, n16, stride=4), :].astype(dtype)


def _from_residue(src_ref, dst, dil, rows, tmp=None):
    assert dil in (4, 16)
    n4, n16 = rows // 4, rows // 16
    for c in range(N_CHUNK):
        for r1 in range(4):
            if dil == 4:
                piece = src_ref[:, _lanes(r1, c)].astype(F32)
            else:
                for r2 in range(4):
                    tmp[c, pl.ds(r1 * n4 + r2, n16, stride=4), :] = src_ref[:, _lanes(4 * r2 + r1, c)].astype(F32)
                piece = tmp[c, r1 * n4:(r1 + 1) * n4, :]
            dst[c, pl.ds(r1, n4, stride=4), :] = piece


def _residue_spec(tm, dil):
    return pl.BlockSpec((tm // dil, dil * ATTN_W), lambda i: (i, 0))


def _residue_shape(seq, dil, dtype):
    return jax.ShapeDtypeStruct((seq // dil, dil * ATTN_W), dtype)


def _fwd_in(x, g_pre, w_in_g, tm=512):
    seq = x.shape[0]
    n_dil = len(DILATIONS)

    def body(x_ref, g_ref, w_ref, ca_ref, sa_ref, cb_ref, sb_ref, *rest):
        qkv_refs, (zr_ref, ht_ref, qkv_scr, tmp) = rest[:3 * n_dil], rest[3 * n_dil:]
        xv = x_ref[...]
        r = lax.rsqrt(jnp.mean(xv * xv, axis=-1, keepdims=True) + NORM_EPS)
        hf = (xv * r) * g_ref[...]
        h = hf.astype(BF16)
        ht_ref[...] = h.T
        cos, sin = _tile_rope(ca_ref, sa_ref, cb_ref, sb_ref)
        first_half = (lax.broadcasted_iota(jnp.int32, (tm, BLK), 1) & 32) == 0

        def rope(t):
            return t * cos + _swap_halves(t, first_half) * sin

        def project(j):
            return jnp.dot(h, w_ref[j], preferred_element_type=F32)

        def place(j, zj):
            for n in range(SHARD_IN // BLK):
                chunk, t = j * (SHARD_IN // BLK) + n, zj[:, n * BLK:(n + 1) * BLK]
                if chunk < N_CHUNK:
                    qkv_scr[chunk] = rope(t) * HEAD_DIM ** -0.5
                elif chunk < 2 * N_CHUNK:
                    qkv_scr[chunk] = rope(t)
                elif chunk < 3 * N_CHUNK:
                    qkv_scr[chunk] = t
                else:
                    zr_ref[:, (chunk - 3 * N_CHUNK) * BLK:(chunk - 3 * N_CHUNK + 1) * BLK] = t.astype(BF16)

        ahead = project(0)
        for j in range(N_DEV):
            zj = ahead
            if j + 1 < N_DEV:
                ahead = project(j + 1)
            place(j, zj)
            for a in range(3):
                if (a + 1) * N_CHUNK - 1 in range(j * (SHARD_IN // BLK), (j + 1) * (SHARD_IN // BLK)):
                    _to_residues(qkv_scr, a * N_CHUNK, [qkv_refs[3 * n + a] for n in range(n_dil)], tmp, tm, BF16)

    row = lambda w: pl.BlockSpec((tm, w), lambda i: (i, 0))
    outs = pl.pallas_call(
        body, name="fwd_in", grid=(seq // tm,),
        in_specs=[row(D_MODEL), _resident((1, D_MODEL)), _resident((N_DEV, D_MODEL, SHARD_IN))] + _rope_specs(tm),
        out_specs=[_residue_spec(tm, dil) for dil in DILATIONS for _ in range(3)]
        + [row(REST_W), pl.BlockSpec((D_MODEL, tm), lambda i: (0, i))],
        out_shape=[_residue_shape(seq, dil, BF16) for dil in DILATIONS for _ in range(3)]
        + [jax.ShapeDtypeStruct((seq, REST_W), BF16), jax.ShapeDtypeStruct((D_MODEL, seq), BF16)],
        scratch_shapes=[pltpu.VMEM((3 * N_CHUNK, tm, BLK), F32), pltpu.VMEM((N_CHUNK, tm, BLK), F32)],
        compiler_params=_params(1),
    )(x, g_pre.reshape(1, D_MODEL), w_in_g, *_rope_tables(seq, tm))
    qkv = [tuple(outs[3 * n:3 * n + 3]) for n in range(n_dil)]
    return qkv, outs[3 * n_dil], outs[3 * n_dil + 1]


def _band_bias(first_block):
    kj = lax.broadcasted_iota(jnp.int32, (2 * BLK, BLK), 0)
    qi = lax.broadcasted_iota(jnp.int32, (2 * BLK, BLK), 1)
    valid = (kj >= qi) & (kj <= qi + BLK)
    bias = jnp.where(valid, 0.0, NEG).astype(BF16)
    bias_first = jnp.where(valid & (kj >= BLK), 0.0, NEG).astype(BF16)
    onehot = ((kj & (BLK - 1)) == qi).astype(F32).astype(BF16)
    return onehot, bias, jnp.where(first_block, bias_first, bias)


def _stack_heads(t):
    keep0 = (lax.broadcasted_iota(jnp.int32, t.shape, 1) < HEAD_DIM).astype(F32).astype(BF16)
    return jnp.concatenate([t * keep0, t * (1 - keep0)], axis=0)


def _unstack_heads(t2, head0):
    return jnp.where(head0, t2[:BLK], t2[BLK:])


def _rows_per_head(a, head0):
    b = pltpu.roll(a, HEAD_DIM, 1)
    rows = jnp.concatenate([jnp.where(head0, a, b), jnp.where(head0, b, a)], axis=0)
    return jnp.concatenate([rows, rows], axis=1)


ATTN_UNITS, ATTN_COLS = 32, 4


def _attn_specs(length, dil, units, max_cols):
    n_blocks = length // BLK
    tb = min(units, n_blocks)
    nc = max(n for n in range(1, min(units // tb, max_cols) + 1) if (dil * N_CHUNK) % n == 0)
    assert n_blocks % tb == 0
    tile = pl.BlockSpec((tb * BLK, nc * BLK), lambda c, t: (t, c))
    prev = pl.BlockSpec((BLK, nc * BLK), lambda c, t: (jnp.maximum(t * tb - 1, 0), c))
    grid = (dil * N_CHUNK // nc, n_blocks // tb)
    return tb, nc, tile, prev, grid


def _window(prev_ref, cur_ref, j, cols):
    if j == 0:
        return jnp.concatenate([prev_ref[:, cols], cur_ref[0:BLK, cols]], axis=0)
    return cur_ref[(j - 1) * BLK:(j + 1) * BLK, cols]


def _attn_fwd(q, k, v, dil):
    length = q.shape[0]
    tb, nc, tile, prev, grid = _attn_specs(length, dil, ATTN_UNITS, ATTN_COLS)
    per_step = tb * nc

    def body(q_ref, kc_ref, kp_ref, vc_ref, vp_ref, o_ref, lse_ref, m_ref, p_ref):
        head0 = lax.broadcasted_iota(jnp.int32, (BLK, BLK), 1) < HEAD_DIM
        onehot, bias, bias_start = _band_bias(pl.program_id(1) == 0)
        ones = jnp.ones((2 * BLK, BLK), BF16)

        def scores(c, j):
            rows, cols = slice(j * BLK, (j + 1) * BLK), slice(c * BLK, (c + 1) * BLK)
            q2 = jnp.concatenate([_stack_heads(q_ref[rows, cols]), onehot], axis=1)
            kk = jnp.concatenate([_window(kp_ref, kc_ref, j, cols), bias_start if j == 0 else bias], axis=1)
            return (lax.dot_general(q2, kk, (((1,), (1,)), ((), ())), preferred_element_type=F32),)

        def probabilities(c, j, s):
            m = jnp.max(s, axis=1, keepdims=True)
            return m, jnp.exp(s - m).astype(BF16)

        def outputs(c, j, m, p):
            rows, cols = slice(j * BLK, (j + 1) * BLK), slice(c * BLK, (c + 1) * BLK)
            vv = jnp.concatenate([_window(vp_ref, vc_ref, j, cols), ones], axis=1)
            pv = jnp.dot(p, vv, preferred_element_type=F32)
            den = pv[:, BLK:]
            o_ref[rows, cols] = _unstack_heads(pv[:, :BLK] / den, head0).astype(BF16)
            log_den = jnp.log(den)
            lse_own = _unstack_heads(m + log_den, head0)
            lse_ref[rows, cols] = lse_own
            m_ref[rows, cols] = lse_own - _unstack_heads(log_den, head0)
            unit = c * tb + j
            p_ref[unit * 2 * BLK:(unit + 1) * 2 * BLK, :] = p

        units = [(c, j) for c in range(nc) for j in range(tb)]
        stage1, stage2 = {}, {}
        for n in range(len(units) + 2):
            if n < len(units):
                stage1[n] = scores(*units[n])
            if 0 <= n - 1 < len(units):
                stage2[n - 1] = probabilities(*units[n - 1], *stage1.pop(n - 1))
            if 0 <= n - 2 < len(units):
                outputs(*units[n - 2], *stage2.pop(n - 2))

    saved = pl.BlockSpec((per_step * 2 * BLK, 2 * BLK), lambda c, t: (c * grid[1] + t, 0))
    return pl.pallas_call(
        body, name=f"attn_fwd_d{dil}", grid=grid,
        in_specs=[tile, tile, prev, tile, prev], out_specs=[tile, tile, tile, saved],
        out_shape=[jax.ShapeDtypeStruct(q.shape, BF16), jax.ShapeDtypeStruct(q.shape, F32),
                   jax.ShapeDtypeStruct(q.shape, F32),
                   jax.ShapeDtypeStruct((grid[0] * grid[1] * per_step * 2 * BLK, 2 * BLK), BF16)],
        compiler_params=_params(2),
    )(q, k, k, v, v)


def _attn_bwd(q, k, v, do, lse, delta, m_own, p_own, dil):
    length = q.shape[0]
    tb, nc, tile, prev, grid = _attn_specs(length, dil, ATTN_UNITS, ATTN_COLS)
    whole = pl.BlockSpec((length, nc * BLK), lambda c, t: (0, c))
    saved = pl.BlockSpec((tb * nc * 2 * BLK, 2 * BLK), lambda c, t: (c * grid[1] + t, 0))

    def body(q_ref, do_ref, lse_ref, dl_ref, m_ref, p_ref, kc_ref, kp_ref, vc_ref, vp_ref, dq_ref, dk_ref, dv_ref):
        t = pl.program_id(1)
        head0 = lax.broadcasted_iota(jnp.int32, (BLK, BLK), 1) < HEAD_DIM

        def scores(c, j):
            rows, cols = slice(j * BLK, (j + 1) * BLK), slice(c * BLK, (c + 1) * BLK)
            q2 = _stack_heads(q_ref[rows, cols])
            do2 = _stack_heads(do_ref[rows, cols])
            kk = _window(kp_ref, kc_ref, j, cols) * jnp.asarray(HEAD_DIM ** -0.5, BF16)
            dp = lax.dot_general(do2, _window(vp_ref, vc_ref, j, cols), (((1,), (1,)), ((), ())),
                                 preferred_element_type=F32)
            return q2, do2, kk, dp

        def probabilities(c, j, q2, do2, kk, dp):
            rows, cols = slice(j * BLK, (j + 1) * BLK), slice(c * BLK, (c + 1) * BLK)
            unit = c * tb + j
            rescale = jnp.exp(m_ref[rows, cols] - lse_ref[rows, cols])
            p = p_ref[unit * 2 * BLK:(unit + 1) * 2 * BLK, :].astype(F32) * _rows_per_head(rescale, head0)
            ds = (p * (dp - _rows_per_head(dl_ref[rows, cols].astype(F32), head0))).astype(BF16)
            return q2, do2, kk, p.astype(BF16), ds

        def gradients(c, j, q2, do2, kk, p, ds):
            rows, cols = slice(j * BLK, (j + 1) * BLK), slice(c * BLK, (c + 1) * BLK)
            dq2 = jnp.dot(ds, kk, preferred_element_type=F32)
            dq_ref[rows, cols] = _unstack_heads(dq2, head0).astype(BF16)
            dk2 = lax.dot_general(ds, q2, (((0,), (0,)), ((), ())), preferred_element_type=F32)
            dv2 = lax.dot_general(p, do2, (((0,), (0,)), ((), ())), preferred_element_type=F32)
            own = pl.ds(pl.multiple_of((t * tb + j) * BLK, BLK), BLK)
            dk_ref[own, cols] = dk2[BLK:].astype(BF16)
            dv_ref[own, cols] = dv2[BLK:].astype(BF16)

            def add_to_previous():
                before = pl.ds(pl.multiple_of((t * tb + j - 1) * BLK, BLK), BLK)
                dk_ref[before, cols] = (dk_ref[before, cols].astype(F32) + dk2[:BLK]).astype(BF16)
                dv_ref[before, cols] = (dv_ref[before, cols].astype(F32) + dv2[:BLK]).astype(BF16)

            if j > 0:
                add_to_previous()
            elif grid[1] > 1:
                pl.when(t > 0)(add_to_previous)

        units = [(c, j) for c in range(nc) for j in range(tb)]
        stage1 = {0: scores(*units[0])}
        for n in range(len(units)):
            stage2 = probabilities(*units[n], *stage1.pop(n))
            if n + 1 < len(units):
                stage1[n + 1] = scores(*units[n + 1])
            gradients(*units[n], *stage2)

    return pl.pallas_call(
        body, name=f"attn_bwd_d{dil}", grid=grid,
        in_specs=[tile, tile, tile, tile, tile, saved, tile, prev, tile, prev], out_specs=[tile, whole, whole],
        out_shape=[jax.ShapeDtypeStruct(q.shape, BF16)] * 3,
        compiler_params=_params(2),
    )(q, do, lse, delta, m_own, p_own, k, k, v, v)


HALO = 16


def _halo_specs(tm, seq):
    before = lambda w: pl.BlockSpec((HALO, w), lambda i: (jnp.maximum(i * (tm // HALO) - 1, 0), 0))
    after = lambda w: pl.BlockSpec((HALO, w), lambda i: (jnp.minimum((i + 1) * (tm // HALO), seq // HALO - 1), 0))
    return before, after


def _conv_taps(u, before, tm):
    row = lax.broadcasted_iota(jnp.int32, u.shape, 0)
    last, last2 = before[HALO - 1:HALO, :], before[HALO - 2:HALO - 1, :]
    u1 = jnp.where(row == 0, last, pltpu.roll(u, 1, 0))
    u2 = jnp.where(row == 0, last2, jnp.where(row == 1, last, pltpu.roll(u, 2, 0)))
    return u1, u2


def _attn_combine(o_parts, lse_parts, zr, conv_w, tm=256):
    seq = zr.shape[0]
    a0, h0, b0, c0, g0 = 0, ATTN_W, ATTN_W + CONV_W, ATTN_W + 2 * CONV_W, ATTN_W + 3 * CONV_W

    def body(o1, o2, o3, l1, l2, l3, zr_ref, zp_ref, w_ref, mixed_ref, o_ref, lse1, lse2, lse3, *scr):
        i = pl.program_id(0)
        for src, dst, dil in zip((o2, o3, l2, l3), scr[:4], DILATIONS[1:] * 2):
            _from_residue(src, dst, dil, tm, tmp=scr[5])
        for c in range(N_CHUNK):
            cols = slice(c * BLK, (c + 1) * BLK)
            la, lb, lc = l1[:, cols], scr[2][c], scr[3][c]
            top = jnp.maximum(jnp.maximum(la, lb), lc)
            ea, eb, ec = jnp.exp(la - top), jnp.exp(lb - top), jnp.exp(lc - top)
            den = ea + eb + ec
            inv = 1.0 / den
            o = (ea * inv) * o1[:, cols].astype(F32) + (eb * inv) * scr[0][c] + (ec * inv) * scr[1][c]
            o_ref[:, cols] = o.astype(BF16)
            scr[4][c] = top + jnp.log(den)
            ga = zr_ref[:, cols].astype(F32)
            mixed_ref[:, cols] = (o * (ga * _sigmoid(ga))).astype(BF16)
        _to_residues(scr[4], 0, (lse1, lse2, lse3), scr[5], tm, F32)
        part = lambda ref, lo, hi: ref[:, lo:hi].astype(F32)
        u = part(zr_ref, c0, g0) * part(zr_ref, h0, b0)
        before = jnp.where(i > 0, part(zp_ref, c0, g0) * part(zp_ref, h0, b0), 0.0)
        u1, u2 = _conv_taps(u, before, tm)
        y = u2 * w_ref[0:1, :] + u1 * w_ref[1:2, :] + u * w_ref[2:3, :]
        gc = part(zr_ref, g0, REST_W)
        mixed_ref[:, ATTN_W:] = ((part(zr_ref, b0, c0) * y) * (gc * _sigmoid(gc))).astype(BF16)

    row = lambda w: pl.BlockSpec((tm, w), lambda i: (i, 0))
    before, _ = _halo_specs(tm, seq)
    views = [_residue_spec(tm, dil) for dil in DILATIONS]
    outs = pl.pallas_call(
        body, name="attn_combine", grid=(seq // tm,),
        in_specs=views * 2 + [row(REST_W), before(REST_W), _resident((3, CONV_W))],
        out_specs=[row(D_MODEL), row(ATTN_W)] + views,
        out_shape=[jax.ShapeDtypeStruct((seq, D_MODEL), BF16), jax.ShapeDtypeStruct((seq, ATTN_W), BF16)]
        + [_residue_shape(seq, dil, F32) for dil in DILATIONS],
        scratch_shapes=[pltpu.VMEM((N_CHUNK, tm, BLK), F32)] * 6,
        compiler_params=_params(1),
    )(*o_parts, *lse_parts, zr, zr, conv_w)
    return outs[0], outs[1], outs[2:]


def _out_loss_bwd(mixed, w_out_g, x, target, g_post, tm=512, n_parts=2):
    seq = x.shape[0]

    def body(mx_ref, w_ref, x_ref, t_ref, g_ref, dout_ref, dmx_ref, dw_ref, dwb_ref, st_ref):
        i = pl.program_id(0)
        g = g_ref[...]
        parts = [slice(n * (tm // n_parts), (n + 1) * (tm // n_parts)) for n in range(n_parts)]

        def project(rows):
            return jnp.dot(mx_ref[rows, :], w_ref[...], preferred_element_type=F32)

        def head(rows, y):
            r = lax.rsqrt(jnp.mean(y * y, axis=-1, keepdims=True) + NORM_EPS)
            yhat = y * r
            err = (x_ref[rows, :] + yhat * g) - t_ref[rows, :]
            dn = err * (1.0 / D_MODEL)
            dout_ref[rows, :] = dn
            tg = dn * g
            dy = (r * (tg - yhat * jnp.mean(tg * yhat, axis=-1, keepdims=True))).astype(BF16)
            dmx_ref[rows, :] = lax.dot_general(dy, w_ref[...], (((1,), (1,)), ((), ())),
                                               preferred_element_type=F32).astype(BF16)
            return dy, jnp.sum(dn * yhat, axis=0, keepdims=True), jnp.sum(err * err)

        ahead, done = project(parts[0]), []
        for n, rows in enumerate(parts):
            y = ahead
            if n + 1 < n_parts:
                ahead = project(parts[n + 1])
            done.append(head(rows, y))
        dy = jnp.concatenate([d[0] for d in done], axis=0)
        dw = lax.dot_general(mx_ref[...], dy, (((0,), (0,)), ((), ())), preferred_element_type=F32)
        gsum = functools.reduce(lambda a, b: a + b, [d[1] for d in done])
        lsum = jnp.broadcast_to(0.5 / D_MODEL * functools.reduce(lambda a, b: a + b, [d[2] for d in done]),
                                (1, D_MODEL))

        @pl.when(i == 0)
        def _():
            dw_ref[...] = dw
            st_ref[...] = jnp.zeros_like(st_ref)
            st_ref[0:1, :] = gsum
            st_ref[1:2, :] = lsum

        @pl.when(i > 0)
        def _():
            dw_ref[...] += dw
            st_ref[0:1, :] += gsum
            st_ref[1:2, :] += lsum

        @pl.when(i == seq // tm - 1)
        def _():
            dwb_ref[...] = dw_ref[...].astype(BF16)

    row = lambda w: pl.BlockSpec((tm, w), lambda i: (i, 0))
    whole = pl.BlockSpec((D_MODEL, D_MODEL), lambda i: (0, 0))
    return pl.pallas_call(
        body, name="out_loss_bwd", grid=(seq // tm,),
        in_specs=[row(D_MODEL), _resident((D_MODEL, D_MODEL)), row(D_MODEL), row(D_MODEL), _resident((1, D_MODEL))],
        out_specs=[row(D_MODEL), row(D_MODEL), whole, whole, pl.BlockSpec((8, D_MODEL), lambda i: (0, 0))],
        out_shape=[jax.ShapeDtypeStruct((seq, D_MODEL), F32), jax.ShapeDtypeStruct((seq, D_MODEL), BF16),
                   jax.ShapeDtypeStruct((D_MODEL, D_MODEL), F32), jax.ShapeDtypeStruct((D_MODEL, D_MODEL), BF16),
                   jax.ShapeDtypeStruct((8, D_MODEL), F32)],
        compiler_params=_params(1),
    )(mixed, w_out_g, x, target, g_post.reshape(1, D_MODEL))


def _head_sum(prod, same_head):
    hi = prod.astype(BF16)
    lo = (prod - hi.astype(F32)).astype(BF16)
    return (jnp.dot(hi, same_head, preferred_element_type=F32) + jnp.dot(lo, same_head, preferred_element_type=F32))


def _gate_bwd(dmixed, zr, o, conv_w, tm=256):
    seq = zr.shape[0]
    n_tiles = seq // tm
    n_dil = len(DILATIONS)
    a0, h0, b0, c0, g0 = 0, ATTN_W, ATTN_W + CONV_W, ATTN_W + 2 * CONV_W, ATTN_W + 3 * CONV_W

    def body(dm_ref, dmn_ref, zr_ref, zp_ref, zn_ref, o_ref, w_ref, *rest):
        do_refs, dl_refs = rest[:n_dil], rest[n_dil:2 * n_dil]
        dz_ref, dw_ref, do_scr, dl_scr, tmp = rest[2 * n_dil:]
        i = pl.program_id(0)
        part = lambda ref, lo, hi: ref[:, lo:hi].astype(F32)
        ga = part(zr_ref, a0, h0)
        sg = _sigmoid(ga)
        dattn = part(dm_ref, 0, ATTN_W)
        ov = o_ref[...].astype(F32)
        do = dattn * (ga * sg)
        dz_ref[:, a0:h0] = (dattn * ov * (sg * (1.0 + ga * (1.0 - sg)))).astype(BF16)
        li = lax.broadcasted_iota(jnp.int32, (BLK, BLK), 0) // HEAD_DIM
        lj = lax.broadcasted_iota(jnp.int32, (BLK, BLK), 1) // HEAD_DIM
        same_head = (li == lj).astype(BF16)
        prod = do * ov
        for c in range(N_CHUNK):
            cols = slice(c * BLK, (c + 1) * BLK)
            do_scr[c] = do[:, cols]
            dl_scr[c] = _head_sum(prod[:, cols], same_head)
        _to_residues(do_scr, 0, do_refs, tmp, tm, BF16)
        _to_residues(dl_scr, 0, dl_refs, tmp, tm, BF16)

        ch, cb, cc, gc = (part(zr_ref, lo, hi) for lo, hi in ((h0, b0), (b0, c0), (c0, g0), (g0, REST_W)))
        u = cc * ch
        before = jnp.where(i > 0, part(zp_ref, c0, g0) * part(zp_ref, h0, b0), 0.0)
        u1, u2 = _conv_taps(u, before, tm)
        w0, w1, w2 = w_ref[0:1, :], w_ref[1:2, :], w_ref[2:3, :]
        y = u2 * w0 + u1 * w1 + u * w2
        sc = _sigmoid(gc)
        silu_c = gc * sc
        dconv = part(dm_ref, ATTN_W, D_MODEL)
        dz_ref[:, b0:c0] = (dconv * y * silu_c).astype(BF16)
        dz_ref[:, g0:] = (dconv * (cb * y) * (sc * (1.0 + gc * (1.0 - sc)))).astype(BF16)
        dy = dconv * cb * silu_c
        gn = part(zn_ref, g0, REST_W)
        after = jnp.where(i < n_tiles - 1,
                          part(dmn_ref, ATTN_W, D_MODEL) * part(zn_ref, b0, c0) * (gn * _sigmoid(gn)), 0.0)
        row = lax.broadcasted_iota(jnp.int32, dy.shape, 0)
        nxt, nxt2 = after[0:1, :], after[1:2, :]
        dy1 = jnp.where(row == tm - 1, nxt, pltpu.roll(dy, tm - 1, 0))
        dy2 = jnp.where(row == tm - 1, nxt2, jnp.where(row == tm - 2, nxt, pltpu.roll(dy, tm - 2, 0)))
        du = dy * w2 + dy1 * w1 + dy2 * w0
        dz_ref[:, c0:g0] = (du * ch).astype(BF16)
        dz_ref[:, h0:b0] = (du * cc).astype(BF16)
        dws = [jnp.sum(dy * u2, axis=0, keepdims=True), jnp.sum(dy * u1, axis=0, keepdims=True),
               jnp.sum(dy * u, axis=0, keepdims=True)]

        @pl.when(i == 0)
        def _():
            dw_ref[...] = jnp.zeros_like(dw_ref)

        for n, part in enumerate(dws):
            dw_ref[n:n + 1, :] += part

    row_spec = lambda w: pl.BlockSpec((tm, w), lambda i: (i, 0))
    before, after = _halo_specs(tm, seq)
    views = [_residue_spec(tm, dil) for dil in DILATIONS]
    outs = pl.pallas_call(
        body, name="gate_bwd", grid=(n_tiles,),
        in_specs=[row_spec(D_MODEL), after(D_MODEL), row_spec(REST_W), before(REST_W), after(REST_W),
                  row_spec(ATTN_W), _resident((3, CONV_W))],
        out_specs=views * 2 + [row_spec(REST_W), pl.BlockSpec((8, CONV_W), lambda i: (0, 0))],
        out_shape=[_residue_shape(seq, dil, BF16) for dil in DILATIONS] * 2
        + [jax.ShapeDtypeStruct((seq, REST_W), BF16), jax.ShapeDtypeStruct((8, CONV_W), F32)],
        scratch_shapes=[pltpu.VMEM((N_CHUNK, tm, BLK), F32)] * 3,
        compiler_params=_params(1),
    )(dmixed, dmixed, zr, zr, zr, o, conv_w)
    return outs[:n_dil], outs[n_dil:2 * n_dil], outs[2 * n_dil], outs[2 * n_dil + 1]


def _in_bwd(dqs, dks, dvs, dzr, x, d_out, g_pre, w_in_g, st_post, dconv, tm=256):
    seq = x.shape[0]

    def body(q1, q2, q3, k1, k2, k3, v1, v2, v3, dzr_ref, ca_ref, sa_ref, cb_ref, sb_ref, x_ref, dout_ref, g_ref,
             w_ref, post_ref, dconv_ref, dz_ref, gx_ref, st_ref, *scratch):
        i = pl.program_id(0)
        cos, sin = _tile_rope(ca_ref, sa_ref, cb_ref, sb_ref)
        first_half = (lax.broadcasted_iota(jnp.int32, (tm, BLK), 1) & 32) == 0
        streams = [(q1, q2, q3), (k1, k2, k3), (v1, v2, v3)]
        from4, from16, tmp = scratch[0:3], scratch[3:6], scratch[6]
        per_slab = SHARD_IN // BLK

        def unrope(t):
            return t * cos - _swap_halves(t, first_half) * sin

        def to_positions(a):
            _from_residue(streams[a][1], from4[a], 4, tm)
            _from_residue(streams[a][2], from16[a], 16, tm, tmp=tmp)

        def assemble(j):
            for chunk in range(j * per_slab, (j + 1) * per_slab):
                a, c = divmod(chunk, N_CHUNK)
                if a < 3:
                    total = streams[a][0][:, _lanes(0, c)].astype(F32) + from4[a][c] + from16[a][c]
                    val = (unrope(total) if a < 2 else total).astype(BF16)
                else:
                    val = dzr_ref[:, (chunk - 3 * N_CHUNK) * BLK:(chunk - 3 * N_CHUNK + 1) * BLK]
                dz_ref[:, chunk * BLK:(chunk + 1) * BLK] = val
            return dz_ref[:, j * SHARD_IN:(j + 1) * SHARD_IN]

        order = [j for j in range(N_DEV) if j * per_slab >= 3 * N_CHUNK]
        order += [j for j in range(N_DEV) if j not in order]
        assert order[2] * per_slab >= 3 * N_CHUNK
        ahead = assemble(order[0])
        dh = None
        for n, j in enumerate(order):
            part = lax.dot_general(ahead, w_ref[j], (((1,), (1,)), ((), ())), preferred_element_type=F32)
            if n < 3:
                to_positions(n)
            if n + 1 < N_DEV:
                ahead = assemble(order[n + 1])
            dh = part if dh is None else dh + part
        xv = x_ref[...]
        r = lax.rsqrt(jnp.mean(xv * xv, axis=-1, keepdims=True) + NORM_EPS)
        xhat = xv * r
        tg = dh * g_ref[...]
        gx_ref[...] = dout_ref[...] + r * (tg - xhat * jnp.mean(tg * xhat, axis=-1, keepdims=True))
        gsum = jnp.sum(dh * xhat, axis=0, keepdims=True)

        @pl.when(i == 0)
        def _():
            st_ref[...] = jnp.zeros_like(st_ref)
            st_ref[1:3, :] = post_ref[0:2, :]
            st_ref[3:6, 0:CONV_W] = dconv_ref[0:3, :]

        st_ref[0:1, :] += gsum

    row = lambda w: pl.BlockSpec((tm, w), lambda i: (i, 0))
    return pl.pallas_call(
        body, name="in_bwd", grid=(seq // tm,),
        in_specs=[_residue_spec(tm, dil) for dil in DILATIONS] * 3
        + [row(REST_W)] + _rope_specs(tm) + [row(D_MODEL), row(D_MODEL), _resident((1, D_MODEL)),
                                             _resident((N_DEV, D_MODEL, SHARD_IN)), _resident((8, D_MODEL)),
                                             _resident((8, CONV_W))],
        out_specs=[row(IN_W), row(D_MODEL), pl.BlockSpec((8, D_MODEL), lambda i: (0, 0))],
        out_shape=[jax.ShapeDtypeStruct((seq, IN_W), BF16), jax.ShapeDtypeStruct((seq, D_MODEL), F32),
                   jax.ShapeDtypeStruct((8, D_MODEL), F32)],
        scratch_shapes=[pltpu.VMEM((N_CHUNK, tm, BLK), F32)] * 7,
        compiler_params=_params(1),
    )(*dqs, *dks, *dvs, dzr, *_rope_tables(seq, tm), x, d_out, g_pre.reshape(1, D_MODEL), w_in_g, st_post, dconv)


def _local_step(x, target, g_pre, g_post, w_in_g, w_out_g, conv_w):
    qkv, zr, ht = _fwd_in(x, g_pre, w_in_g)
    parts = [_attn_fwd(*qkv[n], dil) for n, dil in enumerate(DILATIONS)]
    mixed, o, lse = _attn_combine([p[0] for p in parts], [p[1] for p in parts], zr, conv_w)
    d_out, dmixed, dw_out, dw_out_bf, st_post = _out_loss_bwd(mixed, w_out_g, x, target, g_post)
    do, delta, dzr, dconv = _gate_bwd(dmixed, zr, o, conv_w)
    grads = [_attn_bwd(*qkv[n], do[n], lse[n], delta[n], parts[n][2], parts[n][3], dil)
             for n, dil in enumerate(DILATIONS)]
    dz, grad_x, small = _in_bwd([g[0] for g in grads], [g[1] for g in grads], [g[2] for g in grads], dzr,
                                x, d_out, g_pre, w_in_g, st_post, dconv)
    return grad_x, ht, dz, dw_out, dw_out_bf, small


def _coords():
    return lax.axis_index("x"), lax.axis_index("y"), lax.axis_index("c")


def _peer(k):
    x, y, c = _coords()
    px = 1 - x if k & 4 else x
    py = 1 - y if k & 2 else y
    pc = 1 - c if k & 1 else c
    return (px, py, pc), 4 * px + 2 * py + pc


HBM_SPEC = pl.BlockSpec(memory_space=pltpu.HBM)
VMEM_SPEC = pl.BlockSpec(memory_space=pltpu.VMEM)


def _ag_weights(w_in, w_out, conv_w):
    def body(win_ref, wout_ref, cw_ref, gin_ref, gout_ref, gcw_ref, win_bf, wout_bf, cw_pad, send_sems, recv_sems,
             local_sems):
        x, y, c = _coords()
        me, sibling = (x, y, c), (x, y, 1 - c)
        flip = lambda v, yes: v + yes - 2 * v * yes
        x_nbr, y_nbr, diagonal = (1 - x, y, c), (x, 1 - y, c), (1 - x, 1 - y, c)
        relay_from = (flip(x, 1 - c), flip(y, c), c)
        relay_to = (flip(x, c), flip(y, 1 - c), c)
        slab = lambda px, py, pc: 4 * px + 2 * py + pc
        win_bf[...] = win_ref[...].astype(BF16)
        wout_bf[...] = wout_ref[...].astype(BF16)
        cw_pad[...] = jnp.zeros_like(cw_pad)
        cw_pad[0:3, 0:CONV_W // N_DEV] = cw_ref[...]
        mine = [win_bf, wout_bf, cw_pad]
        gathered = [gin_ref, gout_ref, gcw_ref]

        def copies(k, block, to, own=False):
            return [pltpu.make_async_remote_copy(src_ref=mine[a] if own else gathered[a].at[slab(*block)],
                                                 dst_ref=gathered[a].at[slab(*block)], send_sem=send_sems.at[k, a],
                                                 recv_sem=recv_sems.at[k, a], device_id=to, device_id_type=MESH)
                    for a in range(3)]

        local = [pltpu.make_async_copy(mine[a], gathered[a].at[slab(*me)], local_sems.at[a]) for a in range(3)]
        for cp in local:
            cp.start()
        started = copies(0, me, sibling, own=True) + copies(1, me, x_nbr, own=True) + copies(2, me, y_nbr, own=True)
        for cp in started:
            cp.start()
        for cp in copies(1, x_nbr, me) + copies(2, y_nbr, me):
            cp.wait_recv()
        onward = copies(3, relay_from, relay_to) + copies(4, x_nbr, sibling) + copies(5, y_nbr, sibling)
        for cp in onward:
            cp.start()
        for cp in copies(3, diagonal, me):
            cp.wait_recv()
        last = copies(6, diagonal, sibling)
        for cp in last:
            cp.start()
        for cp in copies(0, sibling, me):
            cp.wait_recv()
        for k, origin in ((4, (1 - x, y, 1 - c)), (5, (x, 1 - y, 1 - c)), (6, (1 - x, 1 - y, 1 - c))):
            for cp in copies(k, origin, me):
                cp.wait_recv()
        for cp in started + onward + last:
            cp.wait_send()
        for cp in local:
            cp.wait()

    return pl.pallas_call(
        body, name="ag_weights",
        in_specs=[VMEM_SPEC, VMEM_SPEC, VMEM_SPEC], out_specs=[HBM_SPEC, HBM_SPEC, HBM_SPEC],
        out_shape=[jax.ShapeDtypeStruct((N_DEV, D_MODEL, SHARD_IN), BF16),
                   jax.ShapeDtypeStruct((N_DEV, SHARD_OUT, D_MODEL), BF16),
                   jax.ShapeDtypeStruct((N_DEV, 8, BLK), F32)],
        scratch_shapes=[pltpu.VMEM((D_MODEL, SHARD_IN), BF16), pltpu.VMEM((SHARD_OUT, D_MODEL), BF16),
                        pltpu.VMEM((8, BLK), F32), pltpu.SemaphoreType.DMA((N_DEV - 1, 3)),
                        pltpu.SemaphoreType.DMA((N_DEV - 1, 3)), pltpu.SemaphoreType.DMA((3,))],
        compiler_params=pltpu.CompilerParams(vmem_limit_bytes=VMEM_LIMIT),
    )(w_in, w_out, conv_w)


def _dw_in_rs(ht, dz, dw_out, small):
    seq = dz.shape[0]

    def body(cols_ref, ht_ref, dz_ref, dout_ref, sm_ref, own_ref, rin_ref, rout_ref, rsm_ref, to_sibling, landed,
             to_chip, zero_buf, d2d_send, d2d_recv, ici_send, ici_recv, side_send, side_recv, local_sems):
        del cols_ref
        step = pl.program_id(0)
        x, y, c = _coords()
        me = 4 * x + 2 * y + c
        sibling = (x, y, 1 - c)
        chips = [(1 - x, y), (x, 1 - y), (1 - x, 1 - y)]

        def d2d(n):
            return pltpu.make_async_remote_copy(src_ref=to_sibling.at[n], dst_ref=landed.at[n], send_sem=d2d_send.at[n],
                                                recv_sem=d2d_recv.at[n], device_id=sibling, device_id_type=MESH)

        def ici(n):
            return pltpu.make_async_remote_copy(src_ref=to_chip.at[n], dst_ref=rin_ref.at[n], send_sem=ici_send.at[n],
                                                recv_sem=ici_recv.at[n], device_id=(*chips[n], c), device_id_type=MESH)

        def side(k, mine):
            peer, peer_idx = _peer(k)
            src_slab, dst_slab = (peer_idx, me) if mine else (me, peer_idx)
            pairs = [(dout_ref.at[src_slab], rout_ref.at[dst_slab]), (sm_ref, rsm_ref.at[dst_slab])]
            return [pltpu.make_async_remote_copy(src_ref=src, dst_ref=dst, send_sem=side_send.at[k - 1, a],
                                                 recv_sem=side_recv.at[k - 1, a], device_id=peer, device_id_type=MESH)
                    for a, (src, dst) in enumerate(pairs)]

        local = [pltpu.make_async_copy(zero_buf, rout_ref.at[me], local_sems.at[0]),
                 pltpu.make_async_copy(sm_ref, rsm_ref.at[me], local_sems.at[1])]

        @pl.when(step == 0)
        def _():
            zero_buf[...] = jnp.zeros_like(zero_buf)
            for cp in local:
                cp.start()
            for k in range(1, N_DEV):
                for cp in side(k, mine=True):
                    cp.start()

        dw = jnp.dot(ht_ref[...], dz_ref[...], preferred_element_type=F32)
        for n, at in zip(range(4), (0, 1, 2, N_DEV - 2)):
            @pl.when(step == at)
            def _(n=n):
                to_sibling[n] = dw.astype(BF16)
                d2d(n).start()

        for n in range(3):
            @pl.when(step == 3 + n)
            def _(n=n):
                d2d(n).wait_recv()
                to_chip[n] = (dw + landed[n].astype(F32)).astype(BF16)
                ici(n).start()

        @pl.when(step == N_DEV - 1)
        def _():
            d2d(3).wait_recv()
            own_ref[...] = dw + landed[3].astype(F32)
            for n in range(3):
                ici(n).wait_recv()
            for k in range(1, N_DEV):
                for cp in side(k, mine=False):
                    cp.wait_recv()
            for n in range(4):
                d2d(n).wait_send()
            for n in range(3):
                ici(n).wait_send()
            for k in range(1, N_DEV):
                for cp in side(k, mine=True):
                    cp.wait_send()
            for cp in local:
                cp.wait()

    x, y, c = _coords()
    others = [(1 - x, y), (x, 1 - y), (1 - x, 1 - y)]
    order = [(*chip, 1 - c) for chip in others] + [(*chip, c) for chip in others] + [(x, y, 1 - c), (x, y, c)]
    cols = jnp.stack([4 * px + 2 * py + pc for px, py, pc in order]).astype(jnp.int32)
    slab = (D_MODEL, SHARD_IN)
    grid_spec = pltpu.PrefetchScalarGridSpec(
        num_scalar_prefetch=1, grid=(N_DEV,),
        in_specs=[pl.BlockSpec((D_MODEL, seq), lambda s, cols: (0, 0), pipeline_mode=pl.Buffered(1)),
                  pl.BlockSpec((seq, SHARD_IN), lambda s, cols: (0, cols[s])), HBM_SPEC, HBM_SPEC],
        out_specs=[pl.BlockSpec(slab, lambda s, cols: (0, 0)), HBM_SPEC, HBM_SPEC, HBM_SPEC],
        scratch_shapes=[pltpu.VMEM((4, *slab), BF16), pltpu.VMEM((4, *slab), BF16), pltpu.VMEM((3, *slab), BF16),
                        pltpu.VMEM((SHARD_OUT, D_MODEL), BF16),
                        pltpu.SemaphoreType.DMA((4,)), pltpu.SemaphoreType.DMA((4,)),
                        pltpu.SemaphoreType.DMA((3,)), pltpu.SemaphoreType.DMA((3,)),
                        pltpu.SemaphoreType.DMA((N_DEV - 1, 2)), pltpu.SemaphoreType.DMA((N_DEV - 1, 2)),
                        pltpu.SemaphoreType.DMA((2,))])
    return pl.pallas_call(
        body, name="dw_in_rs", grid_spec=grid_spec,
        out_shape=[jax.ShapeDtypeStruct(slab, F32),
                   jax.ShapeDtypeStruct((3, *slab), BF16),
                   jax.ShapeDtypeStruct((N_DEV, SHARD_OUT, D_MODEL), BF16),
                   jax.ShapeDtypeStruct((N_DEV, 8, D_MODEL), F32)],
        compiler_params=_params(1),
    )(cols, ht, dz, dw_out, small)


def _adamw_math(w, g, m, v):
    m = ADAM_B1 * m + (1.0 - ADAM_B1) * g
    v = ADAM_B2 * v + (1.0 - ADAM_B2) * (g * g)
    m_hat = m / (1.0 - ADAM_B1 ** ADAM_STEP)
    v_hat = v / (1.0 - ADAM_B2 ** ADAM_STEP)
    delta = -ADAM_LR * (m_hat / (jnp.sqrt(v_hat) + ADAM_EPS) + ADAM_WD * w)
    return delta, m, v


def _sum_slabs(ref, first=None):
    total = ref[0].astype(F32) if first is None else first + ref[0].astype(F32)
    for s in range(1, ref.shape[0]):
        total = total + ref[s].astype(F32)
    return total


def _adamw_slabs(parts, own, own_slab, w, m, v, name, tr):
    rows, cols = w.shape
    tile = pl.BlockSpec((tr, cols), lambda i, s: (i, 0))
    own_spec = tile if own_slab is None else pl.BlockSpec((1, tr, cols), lambda i, s: (s[0], i, 0))

    def body(s_ref, p_ref, own_ref, w_ref, m_ref, v_ref, g_ref, d_ref, nm_ref, nv_ref):
        del s_ref
        g = _sum_slabs(p_ref, own_ref[...].reshape(tr, cols))
        g_ref[...] = g
        d_ref[...], nm_ref[...], nv_ref[...] = _adamw_math(w_ref[...], g, m_ref[...], v_ref[...])

    slab = jnp.zeros((1,), jnp.int32) if own_slab is None else own_slab.reshape(1).astype(jnp.int32)
    grid_spec = pltpu.PrefetchScalarGridSpec(
        num_scalar_prefetch=1, grid=(rows // tr,),
        in_specs=[pl.BlockSpec((parts.shape[0], tr, cols), lambda i, s: (0, i, 0)), own_spec, tile, tile, tile],
        out_specs=[tile] * 4)
    return pl.pallas_call(
        body, name=name, grid_spec=grid_spec,
        out_shape=[jax.ShapeDtypeStruct((rows, cols), F32)] * 4,
        compiler_params=_params(1),
    )(slab, parts, own, w, m, v)


def _adamw_small(parts, me, pre, post, conv):
    n_conv = CONV_W // N_DEV

    def body(me_ref, p_ref, *refs):
        ins, (loss_ref, *outs) = refs[:9], refs[9:]
        sums = _sum_slabs(p_ref)
        loss_ref[...] = sums[2:3, 0:1]
        mine = pltpu.roll(sums[:, 0:CONV_W], (CONV_W - me_ref[0] * n_conv) % CONV_W, 1)[3:6, 0:n_conv]
        for n, g in enumerate((sums[0:1], sums[1:2], mine)):
            w_ref, m_ref, v_ref = ins[3 * n:3 * n + 3]
            outs[4 * n][...] = g
            for out, val in zip(outs[4 * n + 1:4 * n + 4], _adamw_math(w_ref[...], g, m_ref[...], v_ref[...])):
                out[...] = val

    row = jax.ShapeDtypeStruct((1, D_MODEL), F32)
    small = jax.ShapeDtypeStruct((3, n_conv), F32)
    return pl.pallas_call(
        body, name="adamw_small",
        in_specs=[pl.BlockSpec(memory_space=pltpu.SMEM)] + [VMEM_SPEC] * 10,
        out_shape=[jax.ShapeDtypeStruct((1, 1), F32)] + [row] * 8 + [small] * 4,
    )(me.reshape(1).astype(jnp.int32), parts, *pre, *post, *conv)


def kernel(x, norm_pre_g, w_in, conv_w, w_out, norm_post_g, loss_target, m_norm_pre_g, m_w_in, m_conv_w, m_w_out,
           m_norm_post_g, v_norm_pre_g, v_w_in, v_conv_w, v_w_out, v_norm_post_g):
    n_conv = CONV_W // N_DEV
    w_in_g, w_out_g, conv_g = _ag_weights(w_in, w_out, conv_w)
    conv_full = conv_g[:, 0:3, 0:n_conv].transpose(1, 0, 2).reshape(3, CONV_W)
    grad_x, ht, dz, dw_out, dw_out_bf, small = _local_step(x[0], loss_target[0], norm_pre_g, norm_post_g, w_in_g,
                                                           w_out_g.reshape(D_MODEL, D_MODEL), conv_full)
    own_in, r_in, r_out, r_small = _dw_in_rs(ht, dz, dw_out_bf.reshape(N_DEV, SHARD_OUT, D_MODEL), small)
    me = 4 * lax.axis_index("x") + 2 * lax.axis_index("y") + lax.axis_index("c")
    g_in, d_in, nm_in, nv_in = _adamw_slabs(r_in, own_in, None, w_in, m_w_in, v_w_in, "adamw_in", 256)
    g_out, d_out, nm_out, nv_out = _adamw_slabs(r_out, dw_out.reshape(N_DEV, SHARD_OUT, D_MODEL), me, w_out, m_w_out,
                                                v_w_out, "adamw_out", SHARD_OUT)
    vec = lambda a: a.reshape(1, D_MODEL)
    (loss, g_pre, d_pre, nm_pre, nv_pre, g_post, d_post, nm_post, nv_post, g_conv, d_conv, nm_conv,
     nv_conv) = _adamw_small(r_small, me, [vec(a) for a in (norm_pre_g, m_norm_pre_g, v_norm_pre_g)],
                             [vec(a) for a in (norm_post_g, m_norm_post_g, v_norm_post_g)],
                             (conv_w, m_conv_w, v_conv_w))
    flat = lambda a: a.reshape(D_MODEL)
    return (loss.reshape(()), grad_x[None], flat(g_pre), g_in, g_conv, g_out, flat(g_post),
            flat(d_pre), d_in, d_conv, d_out, flat(d_post),
            flat(nm_pre), nm_in, nm_conv, nm_out, flat(nm_post),
            flat(nv_pre), nv_in, nv_conv, nv_out, flat(nv_post))
```

```python
import functools

import jax
import jax.numpy as jnp
from jax import lax
from jax.experimental import pallas as pl
from jax.experimental.pallas import tpu as pltpu

F32 = jnp.float32
BF16 = jnp.bfloat16

D_MODEL = 1024
HEAD_DIM = 64
ATTN_W = 768
CONV_W = 256
IN_W = 4096
REST_W = IN_W - 3 * ATTN_W
BLK = 128
N_DEV = 8
SHARD_IN = IN_W // N_DEV
SHARD_OUT = D_MODEL // N_DEV
DILATIONS = (1, 4, 16)
ROPE_THETA = 10000.0
NORM_EPS = 1e-6
NEG = -1e30

ADAM_LR = 0.001
ADAM_B1 = 0.9
ADAM_B2 = 0.999
ADAM_EPS = 1e-08
ADAM_WD = 0.01
ADAM_STEP = 10

VMEM_LIMIT = 56 * 1024 * 1024
MESH = pl.DeviceIdType.MESH


def _params(n_grid):
    return pltpu.CompilerParams(dimension_semantics=("arbitrary",) * n_grid, vmem_limit_bytes=VMEM_LIMIT)


def _resident(shape):
    zeros = (0,) * len(shape)
    return pl.BlockSpec(shape, lambda *_: zeros, pipeline_mode=pl.Buffered(1))


def _sigmoid(a):
    return 1.0 / (1.0 + jnp.exp(-a))


def _swap_halves(t, first_half):
    return jnp.where(first_half, pltpu.roll(t, BLK - 32, 1), pltpu.roll(t, 32, 1))


def _rope_tables(seq, tm):
    half = HEAD_DIM // 2
    inv_freq = ROPE_THETA ** (-jnp.arange(half, dtype=F32) * 2.0 / HEAD_DIM)
    freq = jnp.concatenate([inv_freq] * 4)
    sign = jnp.concatenate([-jnp.ones(half, F32), jnp.ones(half, F32)] * 2)
    starts = (jnp.arange(seq // tm) * tm).astype(F32)[:, None] * freq[None, :]
    rows = jnp.arange(tm).astype(F32)[:, None] * freq[None, :]
    slab = lambda a: jnp.broadcast_to(a[:, None, :], (seq // tm, 8, BLK))
    return slab(jnp.cos(starts)), slab(jnp.sin(starts) * sign), jnp.cos(rows), jnp.sin(rows) * sign


def _rope_specs(tm):
    return [pl.BlockSpec((1, 8, BLK), lambda i: (i, 0, 0))] * 2 + [_resident((tm, BLK))] * 2


def _tile_rope(cos_start, sin_start, cos_row, sin_row):
    ca, sa, cb, sb = cos_start[0, 0:1, :], sin_start[0, 0:1, :], cos_row[...], sin_row[...]
    return ca * cb - sa * sb, sa * cb + ca * sb


N_CHUNK = ATTN_W // BLK


def _lanes(r, c):
    return slice(r * ATTN_W + c * BLK, r * ATTN_W + (c + 1) * BLK)


def _to_residues(src, chunk0, dst_refs, tmp, rows, dtype):
    assert DILATIONS == (1, 4, 16)
    dst1, dst4, dst16 = dst_refs
    n4, n16 = rows // 4, rows // 16
    for c in range(N_CHUNK):
        dst1[:, _lanes(0, c)] = src[chunk0 + c].astype(dtype)
        for r1 in range(4):
            tmp[c, r1 * n4:(r1 + 1) * n4, :] = src[chunk0 + c, pl.ds(r1, n4, stride=4), :]
        for r1 in range(4):
            dst4[:, _lanes(r1, c)] = tmp[c, r1 * n4:(r1 + 1) * n4, :].astype(dtype)
            for r2 in range(4):
                dst16[:, _lanes(4 * r2 + r1, c)] = tmp[c, pl.ds(r1 * n4 + r2, n16, stride=4), :].astype(dtype)


def _from_residue(src_ref, dst, dil, rows, tmp=None):
    assert dil in (4, 16)
    n4, n16 = rows // 4, rows // 16
    for c in range(N_CHUNK):
        for r1 in range(4):
            if dil == 4:
                piece = src_ref[:, _lanes(r1, c)].astype(F32)
            else:
                for r2 in range(4):
                    tmp[c, pl.ds(r1 * n4 + r2, n16, stride=4), :] = src_ref[:, _lanes(4 * r2 + r1, c)].astype(F32)
                piece = tmp[c, r1 * n4:(r1 + 1) * n4, :]
            dst[c, pl.ds(r1, n4, stride=4), :] = piece


def _residue_spec(tm, dil):
    return pl.BlockSpec((tm // dil, dil * ATTN_W), lambda i: (i, 0))


def _residue_shape(seq, dil, dtype):
    return jax.ShapeDtypeStruct((seq // dil, dil * ATTN_W), dtype)


def _fwd_in(x, g_pre, w_in_g, tm=512):
    seq = x.shape[0]
    n_dil = len(DILATIONS)

    def body(x_ref, g_ref, w_ref, ca_ref, sa_ref, cb_ref, sb_ref, *rest):
        qkv_refs, (zr_ref, ht_ref, qkv_scr, tmp) = rest[:3 * n_dil], rest[3 * n_dil:]
        xv = x_ref[...]
        r = lax.rsqrt(jnp.mean(xv * xv, axis=-1, keepdims=True) + NORM_EPS)
        hf = (xv * r) * g_ref[...]
        h = hf.astype(BF16)
        ht_ref[...] = h.T
        cos, sin = _tile_rope(ca_ref, sa_ref, cb_ref, sb_ref)
        first_half = (lax.broadcasted_iota(jnp.int32, (tm, BLK), 1) & 32) == 0

        def rope(t):
            return t * cos + _swap_halves(t, first_half) * sin

        def project(j):
            return jnp.dot(h, w_ref[j], preferred_element_type=F32)

        def place(j, zj):
            for n in range(SHARD_IN // BLK):
                chunk, t = j * (SHARD_IN // BLK) + n, zj[:, n * BLK:(n + 1) * BLK]
                if chunk < N_CHUNK:
                    qkv_scr[chunk] = rope(t) * HEAD_DIM ** -0.5
                elif chunk < 2 * N_CHUNK:
                    qkv_scr[chunk] = rope(t)
                elif chunk < 3 * N_CHUNK:
                    qkv_scr[chunk] = t
                else:
                    zr_ref[:, (chunk - 3 * N_CHUNK) * BLK:(chunk - 3 * N_CHUNK + 1) * BLK] = t.astype(BF16)

        ahead = project(0)
        for j in range(N_DEV):
            zj = ahead
            if j + 1 < N_DEV:
                ahead = project(j + 1)
            place(j, zj)
            for a in range(3):
                if (a + 1) * N_CHUNK - 1 in range(j * (SHARD_IN // BLK), (j + 1) * (SHARD_IN // BLK)):
                    _to_residues(qkv_scr, a * N_CHUNK, [qkv_refs[3 * n + a] for n in range(n_dil)], tmp, tm, BF16)

    row = lambda w: pl.BlockSpec((tm, w), lambda i: (i, 0))
    outs = pl.pallas_call(
        body, name="fwd_in", grid=(seq // tm,),
        in_specs=[row(D_MODEL), _resident((1, D_MODEL)), _resident((N_DEV, D_MODEL, SHARD_IN))] + _rope_specs(tm),
        out_specs=[_residue_spec(tm, dil) for dil in DILATIONS for _ in range(3)]
        + [row(REST_W), pl.BlockSpec((D_MODEL, tm), lambda i: (0, i))],
        out_shape=[_residue_shape(seq, dil, BF16) for dil in DILATIONS for _ in range(3)]
        + [jax.ShapeDtypeStruct((seq, REST_W), BF16), jax.ShapeDtypeStruct((D_MODEL, seq), BF16)],
        scratch_shapes=[pltpu.VMEM((3 * N_CHUNK, tm, BLK), F32), pltpu.VMEM((N_CHUNK, tm, BLK), F32)],
        compiler_params=_params(1),
    )(x, g_pre.reshape(1, D_MODEL), w_in_g, *_rope_tables(seq, tm))
    qkv = [tuple(outs[3 * n:3 * n + 3]) for n in range(n_dil)]
    return qkv, outs[3 * n_dil], outs[3 * n_dil + 1]


def _band_bias(first_block):
    kj = lax.broadcasted_iota(jnp.int32, (2 * BLK, BLK), 0)
    qi = lax.broadcasted_iota(jnp.int32, (2 * BLK, BLK), 1)
    valid = (kj >= qi) & (kj <= qi + BLK)
    bias = jnp.where(valid, 0.0, NEG).astype(BF16)
    bias_first = jnp.where(valid & (kj >= BLK), 0.0, NEG).astype(BF16)
    onehot = ((kj & (BLK - 1)) == qi).astype(F32).astype(BF16)
    return onehot, bias, jnp.where(first_block, bias_first, bias)


def _stack_heads(t):
    keep0 = (lax.broadcasted_iota(jnp.int32, t.shape, 1) < HEAD_DIM).astype(F32).astype(BF16)
    return jnp.concatenate([t * keep0, t * (1 - keep0)], axis=0)


def _unstack_heads(t2, head0):
    return jnp.where(head0, t2[:BLK], t2[BLK:])


def _rows_per_head(a, head0):
    b = pltpu.roll(a, HEAD_DIM, 1)
    rows = jnp.concatenate([jnp.where(head0, a, b), jnp.where(head0, b, a)], axis=0)
    return jnp.concatenate([rows, rows], axis=1)


FWD_UNITS, FWD_COLS = 64, 8
BWD_UNITS, BWD_COLS = 32, 4


def _attn_specs(length, dil, units, max_cols):
    n_blocks = length // BLK
    tb = min(units, n_blocks)
    nc = max(n for n in range(1, min(units // tb, max_cols) + 1) if (dil * N_CHUNK) % n == 0)
    assert n_blocks % tb == 0
    tile = pl.BlockSpec((tb * BLK, nc * BLK), lambda c, t: (t, c))
    prev = pl.BlockSpec((BLK, nc * BLK), lambda c, t: (jnp.maximum(t * tb - 1, 0), c))
    grid = (dil * N_CHUNK // nc, n_blocks // tb)
    return tb, nc, tile, prev, grid


def _window(prev_ref, cur_ref, j, cols):
    if j == 0:
        return jnp.concatenate([prev_ref[:, cols], cur_ref[0:BLK, cols]], axis=0)
    return cur_ref[(j - 1) * BLK:(j + 1) * BLK, cols]


def _attn_fwd(q, k, v, dil):
    length = q.shape[0]
    tb, nc, tile, prev, grid = _attn_specs(length, dil, FWD_UNITS, FWD_COLS)

    def body(q_ref, kc_ref, kp_ref, vc_ref, vp_ref, o_ref, lse_ref):
        head0 = lax.broadcasted_iota(jnp.int32, (BLK, BLK), 1) < HEAD_DIM
        onehot, bias, bias_start = _band_bias(pl.program_id(1) == 0)
        ones = jnp.ones((2 * BLK, BLK), BF16)

        def scores(c, j):
            rows, cols = slice(j * BLK, (j + 1) * BLK), slice(c * BLK, (c + 1) * BLK)
            q2 = jnp.concatenate([_stack_heads(q_ref[rows, cols]), onehot], axis=1)
            kk = jnp.concatenate([_window(kp_ref, kc_ref, j, cols), bias_start if j == 0 else bias], axis=1)
            return (lax.dot_general(q2, kk, (((1,), (1,)), ((), ())), preferred_element_type=F32),)

        def probabilities(c, j, s):
            m = jnp.max(s, axis=1, keepdims=True)
            return m, jnp.exp(s - m).astype(BF16)

        def outputs(c, j, m, p):
            rows, cols = slice(j * BLK, (j + 1) * BLK), slice(c * BLK, (c + 1) * BLK)
            vv = jnp.concatenate([_window(vp_ref, vc_ref, j, cols), ones], axis=1)
            pv = jnp.dot(p, vv, preferred_element_type=F32)
            den = pv[:, BLK:]
            o_ref[rows, cols] = _unstack_heads(pv[:, :BLK] / den, head0).astype(BF16)
            lse_ref[rows, cols] = _unstack_heads(m + jnp.log(den), head0)

        units = [(c, j) for c in range(nc) for j in range(tb)]
        stage1, stage2 = {}, {}
        for n in range(len(units) + 2):
            if n < len(units):
                stage1[n] = scores(*units[n])
            if 0 <= n - 1 < len(units):
                stage2[n - 1] = probabilities(*units[n - 1], *stage1.pop(n - 1))
            if 0 <= n - 2 < len(units):
                outputs(*units[n - 2], *stage2.pop(n - 2))

    return pl.pallas_call(
        body, name=f"attn_fwd_d{dil}", grid=grid,
        in_specs=[tile, tile, prev, tile, prev], out_specs=[tile, tile],
        out_shape=[jax.ShapeDtypeStruct(q.shape, BF16), jax.ShapeDtypeStruct(q.shape, F32)],
        compiler_params=_params(2),
    )(q, k, k, v, v)


def _attn_bwd(q, k, v, do, lse, delta, dil):
    length = q.shape[0]
    tb, nc, tile, prev, grid = _attn_specs(length, dil, BWD_UNITS, BWD_COLS)
    whole = pl.BlockSpec((length, nc * BLK), lambda c, t: (0, c))

    def body(q_ref, do_ref, lse_ref, dl_ref, kc_ref, kp_ref, vc_ref, vp_ref, dq_ref, dk_ref, dv_ref):
        t = pl.program_id(1)
        head0 = lax.broadcasted_iota(jnp.int32, (BLK, BLK), 1) < HEAD_DIM
        onehot, bias, bias_start = _band_bias(t == 0)

        def scores(c, j):
            rows, cols = slice(j * BLK, (j + 1) * BLK), slice(c * BLK, (c + 1) * BLK)
            q2 = _stack_heads(q_ref[rows, cols])
            do2 = _stack_heads(do_ref[rows, cols])
            kk = _window(kp_ref, kc_ref, j, cols)
            s = lax.dot_general(jnp.concatenate([q2, onehot], axis=1),
                                jnp.concatenate([kk, bias_start if j == 0 else bias], axis=1),
                                (((1,), (1,)), ((), ())), preferred_element_type=F32)
            dp = lax.dot_general(do2, _window(vp_ref, vc_ref, j, cols), (((1,), (1,)), ((), ())),
                                 preferred_element_type=F32)
            return q2, do2, kk, s, dp

        def probabilities(c, j, q2, do2, kk, s, dp):
            rows, cols = slice(j * BLK, (j + 1) * BLK), slice(c * BLK, (c + 1) * BLK)
            p = jnp.exp(s - _rows_per_head(lse_ref[rows, cols], head0))
            ds = (p * (dp - _rows_per_head(dl_ref[rows, cols].astype(F32), head0))).astype(BF16)
            return q2, do2, kk, p.astype(BF16), ds

        def gradients(c, j, q2, do2, kk, p, ds):
            rows, cols = slice(j * BLK, (j + 1) * BLK), slice(c * BLK, (c + 1) * BLK)
            dq2 = jnp.dot(ds, kk, preferred_element_type=F32)
            dq_ref[rows, cols] = (_unstack_heads(dq2, head0) * HEAD_DIM ** -0.5).astype(BF16)
            dk2 = lax.dot_general(ds, q2, (((0,), (0,)), ((), ())), preferred_element_type=F32)
            dv2 = lax.dot_general(p, do2, (((0,), (0,)), ((), ())), preferred_element_type=F32)
            own = pl.ds(pl.multiple_of((t * tb + j) * BLK, BLK), BLK)
            dk_ref[own, cols] = dk2[BLK:].astype(BF16)
            dv_ref[own, cols] = dv2[BLK:].astype(BF16)

            def add_to_previous():
                before = pl.ds(pl.multiple_of((t * tb + j - 1) * BLK, BLK), BLK)
                dk_ref[before, cols] = (dk_ref[before, cols].astype(F32) + dk2[:BLK]).astype(BF16)
                dv_ref[before, cols] = (dv_ref[before, cols].astype(F32) + dv2[:BLK]).astype(BF16)

            if j > 0:
                add_to_previous()
            elif grid[1] > 1:
                pl.when(t > 0)(add_to_previous)

        units = [(c, j) for c in range(nc) for j in range(tb)]
        stage1 = {0: scores(*units[0])}
        for n in range(len(units)):
            stage2 = probabilities(*units[n], *stage1.pop(n))
            if n + 1 < len(units):
                stage1[n + 1] = scores(*units[n + 1])
            gradients(*units[n], *stage2)

    return pl.pallas_call(
        body, name=f"attn_bwd_d{dil}", grid=grid,
        in_specs=[tile, tile, tile, tile, tile, prev, tile, prev], out_specs=[tile, whole, whole],
        out_shape=[jax.ShapeDtypeStruct(q.shape, BF16)] * 3,
        compiler_params=_params(2),
    )(q, do, lse, delta, k, k, v, v)


HALO = 16


def _halo_specs(tm, seq):
    before = lambda w: pl.BlockSpec((HALO, w), lambda i: (jnp.maximum(i * (tm // HALO) - 1, 0), 0))
    after = lambda w: pl.BlockSpec((HALO, w), lambda i: (jnp.minimum((i + 1) * (tm // HALO), seq // HALO - 1), 0))
    return before, after


def _conv_taps(u, before, tm):
    row = lax.broadcasted_iota(jnp.int32, u.shape, 0)
    last, last2 = before[HALO - 1:HALO, :], before[HALO - 2:HALO - 1, :]
    u1 = jnp.where(row == 0, last, pltpu.roll(u, 1, 0))
    u2 = jnp.where(row == 0, last2, jnp.where(row == 1, last, pltpu.roll(u, 2, 0)))
    return u1, u2


def _attn_combine(o_parts, lse_parts, zr, conv_w, tm=256):
    seq = zr.shape[0]
    a0, h0, b0, c0, g0 = 0, ATTN_W, ATTN_W + CONV_W, ATTN_W + 2 * CONV_W, ATTN_W + 3 * CONV_W

    def body(o1, o2, o3, l1, l2, l3, zr_ref, zp_ref, w_ref, mixed_ref, o_ref, lse1, lse2, lse3, *scr):
        i = pl.program_id(0)
        for src, dst, dil in zip((o2, o3, l2, l3), scr[:4], DILATIONS[1:] * 2):
            _from_residue(src, dst, dil, tm, tmp=scr[5])
        for c in range(N_CHUNK):
            cols = slice(c * BLK, (c + 1) * BLK)
            la, lb, lc = l1[:, cols], scr[2][c], scr[3][c]
            top = jnp.maximum(jnp.maximum(la, lb), lc)
            ea, eb, ec = jnp.exp(la - top), jnp.exp(lb - top), jnp.exp(lc - top)
            den = ea + eb + ec
            inv = 1.0 / den
            o = (ea * inv) * o1[:, cols].astype(F32) + (eb * inv) * scr[0][c] + (ec * inv) * scr[1][c]
            o_ref[:, cols] = o.astype(BF16)
            scr[4][c] = top + jnp.log(den)
            ga = zr_ref[:, cols].astype(F32)
            mixed_ref[:, cols] = (o * (ga * _sigmoid(ga))).astype(BF16)
        _to_residues(scr[4], 0, (lse1, lse2, lse3), scr[5], tm, F32)
        part = lambda ref, lo, hi: ref[:, lo:hi].astype(F32)
        u = part(zr_ref, c0, g0) * part(zr_ref, h0, b0)
        before = jnp.where(i > 0, part(zp_ref, c0, g0) * part(zp_ref, h0, b0), 0.0)
        u1, u2 = _conv_taps(u, before, tm)
        y = u2 * w_ref[0:1, :] + u1 * w_ref[1:2, :] + u * w_ref[2:3, :]
        gc = part(zr_ref, g0, REST_W)
        mixed_ref[:, ATTN_W:] = ((part(zr_ref, b0, c0) * y) * (gc * _sigmoid(gc))).astype(BF16)

    row = lambda w: pl.BlockSpec((tm, w), lambda i: (i, 0))
    before, _ = _halo_specs(tm, seq)
    views = [_residue_spec(tm, dil) for dil in DILATIONS]
    outs = pl.pallas_call(
        body, name="attn_combine", grid=(seq // tm,),
        in_specs=views * 2 + [row(REST_W), before(REST_W), _resident((3, CONV_W))],
        out_specs=[row(D_MODEL), row(ATTN_W)] + views,
        out_shape=[jax.ShapeDtypeStruct((seq, D_MODEL), BF16), jax.ShapeDtypeStruct((seq, ATTN_W), BF16)]
        + [_residue_shape(seq, dil, F32) for dil in DILATIONS],
        scratch_shapes=[pltpu.VMEM((N_CHUNK, tm, BLK), F32)] * 6,
        compiler_params=_params(1),
    )(*o_parts, *lse_parts, zr, zr, conv_w)
    return outs[0], outs[1], outs[2:]


def _out_loss_bwd(mixed, w_out_g, x, target, g_post, tm=512, n_parts=2):
    seq = x.shape[0]

    def body(mx_ref, w_ref, x_ref, t_ref, g_ref, dout_ref, dmx_ref, dw_ref, dwb_ref, st_ref):
        i = pl.program_id(0)
        g = g_ref[...]
        parts = [slice(n * (tm // n_parts), (n + 1) * (tm // n_parts)) for n in range(n_parts)]

        def project(rows):
            return jnp.dot(mx_ref[rows, :], w_ref[...], preferred_element_type=F32)

        def head(rows, y):
            r = lax.rsqrt(jnp.mean(y * y, axis=-1, keepdims=True) + NORM_EPS)
            yhat = y * r
            err = (x_ref[rows, :] + yhat * g) - t_ref[rows, :]
            dn = err * (1.0 / D_MODEL)
            dout_ref[rows, :] = dn.astype(BF16)
            tg = dn * g
            dy = (r * (tg - yhat * jnp.mean(tg * yhat, axis=-1, keepdims=True))).astype(BF16)
            dmx_ref[rows, :] = lax.dot_general(dy, w_ref[...], (((1,), (1,)), ((), ())),
                                               preferred_element_type=F32).astype(BF16)
            return dy, jnp.sum(dn * yhat, axis=0, keepdims=True), jnp.sum(err * err)

        ahead, done = project(parts[0]), []
        for n, rows in enumerate(parts):
            y = ahead
            if n + 1 < n_parts:
                ahead = project(parts[n + 1])
            done.append(head(rows, y))
        dy = jnp.concatenate([d[0] for d in done], axis=0)
        dw = lax.dot_general(mx_ref[...], dy, (((0,), (0,)), ((), ())), preferred_element_type=F32)
        gsum = functools.reduce(lambda a, b: a + b, [d[1] for d in done])
        lsum = jnp.broadcast_to(0.5 / D_MODEL * functools.reduce(lambda a, b: a + b, [d[2] for d in done]),
                                (1, D_MODEL))

        @pl.when(i == 0)
        def _():
            dw_ref[...] = dw
            st_ref[...] = jnp.zeros_like(st_ref)
            st_ref[0:1, :] = gsum
            st_ref[1:2, :] = lsum

        @pl.when(i > 0)
        def _():
            dw_ref[...] += dw
            st_ref[0:1, :] += gsum
            st_ref[1:2, :] += lsum

        @pl.when(i == seq // tm - 1)
        def _():
            dwb_ref[...] = dw_ref[...].astype(BF16)

    row = lambda w: pl.BlockSpec((tm, w), lambda i: (i, 0))
    whole = pl.BlockSpec((D_MODEL, D_MODEL), lambda i: (0, 0))
    return pl.pallas_call(
        body, name="out_loss_bwd", grid=(seq // tm,),
        in_specs=[row(D_MODEL), _resident((D_MODEL, D_MODEL)), row(D_MODEL), row(D_MODEL), _resident((1, D_MODEL))],
        out_specs=[row(D_MODEL), row(D_MODEL), whole, whole, pl.BlockSpec((8, D_MODEL), lambda i: (0, 0))],
        out_shape=[jax.ShapeDtypeStruct((seq, D_MODEL), BF16), jax.ShapeDtypeStruct((seq, D_MODEL), BF16),
                   jax.ShapeDtypeStruct((D_MODEL, D_MODEL), F32), jax.ShapeDtypeStruct((D_MODEL, D_MODEL), BF16),
                   jax.ShapeDtypeStruct((8, D_MODEL), F32)],
        compiler_params=_params(1),
    )(mixed, w_out_g, x, target, g_post.reshape(1, D_MODEL))


def _head_sum(prod, same_head):
    hi = prod.astype(BF16)
    lo = (prod - hi.astype(F32)).astype(BF16)
    return (jnp.dot(hi, same_head, preferred_element_type=F32) + jnp.dot(lo, same_head, preferred_element_type=F32))


def _gate_bwd(dmixed, zr, o, conv_w, tm=256):
    seq = zr.shape[0]
    n_tiles = seq // tm
    n_dil = len(DILATIONS)
    a0, h0, b0, c0, g0 = 0, ATTN_W, ATTN_W + CONV_W, ATTN_W + 2 * CONV_W, ATTN_W + 3 * CONV_W

    def body(dm_ref, dmn_ref, zr_ref, zp_ref, zn_ref, o_ref, w_ref, *rest):
        do_refs, dl_refs = rest[:n_dil], rest[n_dil:2 * n_dil]
        dz_ref, dw_ref, do_scr, dl_scr, tmp = rest[2 * n_dil:]
        i = pl.program_id(0)
        part = lambda ref, lo, hi: ref[:, lo:hi].astype(F32)
        ga = part(zr_ref, a0, h0)
        sg = _sigmoid(ga)
        dattn = part(dm_ref, 0, ATTN_W)
        ov = o_ref[...].astype(F32)
        do = dattn * (ga * sg)
        dz_ref[:, a0:h0] = (dattn * ov * (sg * (1.0 + ga * (1.0 - sg)))).astype(BF16)
        li = lax.broadcasted_iota(jnp.int32, (BLK, BLK), 0) // HEAD_DIM
        lj = lax.broadcasted_iota(jnp.int32, (BLK, BLK), 1) // HEAD_DIM
        same_head = (li == lj).astype(BF16)
        prod = do * ov
        for c in range(N_CHUNK):
            cols = slice(c * BLK, (c + 1) * BLK)
            do_scr[c] = do[:, cols]
            dl_scr[c] = _head_sum(prod[:, cols], same_head)
        _to_residues(do_scr, 0, do_refs, tmp, tm, BF16)
        _to_residues(dl_scr, 0, dl_refs, tmp, tm, BF16)

        ch, cb, cc, gc = (part(zr_ref, lo, hi) for lo, hi in ((h0, b0), (b0, c0), (c0, g0), (g0, REST_W)))
        u = cc * ch
        before = jnp.where(i > 0, part(zp_ref, c0, g0) * part(zp_ref, h0, b0), 0.0)
        u1, u2 = _conv_taps(u, before, tm)
        w0, w1, w2 = w_ref[0:1, :], w_ref[1:2, :], w_ref[2:3, :]
        y = u2 * w0 + u1 * w1 + u * w2
        sc = _sigmoid(gc)
        silu_c = gc * sc
        dconv = part(dm_ref, ATTN_W, D_MODEL)
        dz_ref[:, b0:c0] = (dconv * y * silu_c).astype(BF16)
        dz_ref[:, g0:] = (dconv * (cb * y) * (sc * (1.0 + gc * (1.0 - sc)))).astype(BF16)
        dy = dconv * cb * silu_c
        gn = part(zn_ref, g0, REST_W)
        after = jnp.where(i < n_tiles - 1,
                          part(dmn_ref, ATTN_W, D_MODEL) * part(zn_ref, b0, c0) * (gn * _sigmoid(gn)), 0.0)
        row = lax.broadcasted_iota(jnp.int32, dy.shape, 0)
        nxt, nxt2 = after[0:1, :], after[1:2, :]
        dy1 = jnp.where(row == tm - 1, nxt, pltpu.roll(dy, tm - 1, 0))
        dy2 = jnp.where(row == tm - 1, nxt2, jnp.where(row == tm - 2, nxt, pltpu.roll(dy, tm - 2, 0)))
        du = dy * w2 + dy1 * w1 + dy2 * w0
        dz_ref[:, c0:g0] = (du * ch).astype(BF16)
        dz_ref[:, h0:b0] = (du * cc).astype(BF16)
        dws = [jnp.sum(dy * u2, axis=0, keepdims=True), jnp.sum(dy * u1, axis=0, keepdims=True),
               jnp.sum(dy * u, axis=0, keepdims=True)]

        @pl.when(i == 0)
        def _():
            dw_ref[...] = jnp.zeros_like(dw_ref)

        for n, part in enumerate(dws):
            dw_ref[n:n + 1, :] += part

    row_spec = lambda w: pl.BlockSpec((tm, w), lambda i: (i, 0))
    before, after = _halo_specs(tm, seq)
    views = [_residue_spec(tm, dil) for dil in DILATIONS]
    outs = pl.pallas_call(
        body, name="gate_bwd", grid=(n_tiles,),
        in_specs=[row_spec(D_MODEL), after(D_MODEL), row_spec(REST_W), before(REST_W), after(REST_W),
                  row_spec(ATTN_W), _resident((3, CONV_W))],
        out_specs=views * 2 + [row_spec(REST_W), pl.BlockSpec((8, CONV_W), lambda i: (0, 0))],
        out_shape=[_residue_shape(seq, dil, BF16) for dil in DILATIONS] * 2
        + [jax.ShapeDtypeStruct((seq, REST_W), BF16), jax.ShapeDtypeStruct((8, CONV_W), F32)],
        scratch_shapes=[pltpu.VMEM((N_CHUNK, tm, BLK), F32)] * 3,
        compiler_params=_params(1),
    )(dmixed, dmixed, zr, zr, zr, o, conv_w)
    return outs[:n_dil], outs[n_dil:2 * n_dil], outs[2 * n_dil], outs[2 * n_dil + 1]


def _in_bwd(dqs, dks, dvs, dzr, x, d_out, g_pre, w_in_g, st_post, dconv, tm=256):
    seq = x.shape[0]

    def body(q1, q2, q3, k1, k2, k3, v1, v2, v3, dzr_ref, ca_ref, sa_ref, cb_ref, sb_ref, x_ref, dout_ref, g_ref,
             w_ref, post_ref, dconv_ref, dz_ref, gx_ref, st_ref, *scratch):
        i = pl.program_id(0)
        cos, sin = _tile_rope(ca_ref, sa_ref, cb_ref, sb_ref)
        first_half = (lax.broadcasted_iota(jnp.int32, (tm, BLK), 1) & 32) == 0
        streams = [(q1, q2, q3), (k1, k2, k3), (v1, v2, v3)]
        from4, from16, tmp = scratch[0:3], scratch[3:6], scratch[6]
        per_slab = SHARD_IN // BLK

        def unrope(t):
            return t * cos - _swap_halves(t, first_half) * sin

        def to_positions(a):
            _from_residue(streams[a][1], from4[a], 4, tm)
            _from_residue(streams[a][2], from16[a], 16, tm, tmp=tmp)

        def assemble(j):
            for chunk in range(j * per_slab, (j + 1) * per_slab):
                a, c = divmod(chunk, N_CHUNK)
                if a < 3:
                    total = streams[a][0][:, _lanes(0, c)].astype(F32) + from4[a][c] + from16[a][c]
                    val = (unrope(total) if a < 2 else total).astype(BF16)
                else:
                    val = dzr_ref[:, (chunk - 3 * N_CHUNK) * BLK:(chunk - 3 * N_CHUNK + 1) * BLK]
                dz_ref[:, chunk * BLK:(chunk + 1) * BLK] = val
            return dz_ref[:, j * SHARD_IN:(j + 1) * SHARD_IN]

        order = [j for j in range(N_DEV) if j * per_slab >= 3 * N_CHUNK]
        order += [j for j in range(N_DEV) if j not in order]
        assert order[2] * per_slab >= 3 * N_CHUNK
        ahead = assemble(order[0])
        dh = None
        for n, j in enumerate(order):
            part = lax.dot_general(ahead, w_ref[j], (((1,), (1,)), ((), ())), preferred_element_type=F32)
            if n < 3:
                to_positions(n)
            if n + 1 < N_DEV:
                ahead = assemble(order[n + 1])
            dh = part if dh is None else dh + part
        xv = x_ref[...]
        r = lax.rsqrt(jnp.mean(xv * xv, axis=-1, keepdims=True) + NORM_EPS)
        xhat = xv * r
        tg = dh * g_ref[...]
        gx_ref[...] = dout_ref[...].astype(F32) + r * (tg - xhat * jnp.mean(tg * xhat, axis=-1, keepdims=True))
        gsum = jnp.sum(dh * xhat, axis=0, keepdims=True)

        @pl.when(i == 0)
        def _():
            st_ref[...] = jnp.zeros_like(st_ref)
            st_ref[1:3, :] = post_ref[0:2, :]
            st_ref[3:6, 0:CONV_W] = dconv_ref[0:3, :]

        st_ref[0:1, :] += gsum

    row = lambda w: pl.BlockSpec((tm, w), lambda i: (i, 0))
    return pl.pallas_call(
        body, name="in_bwd", grid=(seq // tm,),
        in_specs=[_residue_spec(tm, dil) for dil in DILATIONS] * 3
        + [row(REST_W)] + _rope_specs(tm) + [row(D_MODEL), row(D_MODEL), _resident((1, D_MODEL)),
                                             _resident((N_DEV, D_MODEL, SHARD_IN)), _resident((8, D_MODEL)),
                                             _resident((8, CONV_W))],
        out_specs=[row(IN_W), row(D_MODEL), pl.BlockSpec((8, D_MODEL), lambda i: (0, 0))],
        out_shape=[jax.ShapeDtypeStruct((seq, IN_W), BF16), jax.ShapeDtypeStruct((seq, D_MODEL), F32),
                   jax.ShapeDtypeStruct((8, D_MODEL), F32)],
        scratch_shapes=[pltpu.VMEM((N_CHUNK, tm, BLK), F32)] * 7,
        compiler_params=_params(1),
    )(*dqs, *dks, *dvs, dzr, *_rope_tables(seq, tm), x, d_out, g_pre.reshape(1, D_MODEL), w_in_g, st_post, dconv)


def _local_step(x, target, g_pre, g_post, w_in_g, w_out_g, conv_w):
    qkv, zr, ht = _fwd_in(x, g_pre, w_in_g)
    parts = [_attn_fwd(*qkv[n], dil) for n, dil in enumerate(DILATIONS)]
    mixed, o, lse = _attn_combine([p[0] for p in parts], [p[1] for p in parts], zr, conv_w)
    d_out, dmixed, dw_out, dw_out_bf, st_post = _out_loss_bwd(mixed, w_out_g, x, target, g_post)
    do, delta, dzr, dconv = _gate_bwd(dmixed, zr, o, conv_w)
    grads = [_attn_bwd(*qkv[n], do[n], lse[n], delta[n], dil) for n, dil in enumerate(DILATIONS)]
    dz, grad_x, small = _in_bwd([g[0] for g in grads], [g[1] for g in grads], [g[2] for g in grads], dzr,
                                x, d_out, g_pre, w_in_g, st_post, dconv)
    return grad_x, ht, dz, dw_out, dw_out_bf, small


def _coords():
    return lax.axis_index("x"), lax.axis_index("y"), lax.axis_index("c")


def _peer(k):
    x, y, c = _coords()
    px = 1 - x if k & 4 else x
    py = 1 - y if k & 2 else y
    pc = 1 - c if k & 1 else c
    return (px, py, pc), 4 * px + 2 * py + pc


HBM_SPEC = pl.BlockSpec(memory_space=pltpu.HBM)
VMEM_SPEC = pl.BlockSpec(memory_space=pltpu.VMEM)


def _ag_weights(w_in, w_out, conv_w):
    def body(win_ref, wout_ref, cw_ref, gin_ref, gout_ref, gcw_ref, win_bf, wout_bf, cw_pad, send_sems, recv_sems,
             local_sems):
        x, y, c = _coords()
        me, sibling = (x, y, c), (x, y, 1 - c)
        flip = lambda v, yes: v + yes - 2 * v * yes
        x_nbr, y_nbr, diagonal = (1 - x, y, c), (x, 1 - y, c), (1 - x, 1 - y, c)
        relay_from = (flip(x, 1 - c), flip(y, c), c)
        relay_to = (flip(x, c), flip(y, 1 - c), c)
        slab = lambda px, py, pc: 4 * px + 2 * py + pc
        win_bf[...] = win_ref[...].astype(BF16)
        wout_bf[...] = wout_ref[...].astype(BF16)
        cw_pad[...] = jnp.zeros_like(cw_pad)
        cw_pad[0:3, 0:CONV_W // N_DEV] = cw_ref[...]
        mine = [win_bf, wout_bf, cw_pad]
        gathered = [gin_ref, gout_ref, gcw_ref]

        def copies(k, block, to, own=False):
            return [pltpu.make_async_remote_copy(src_ref=mine[a] if own else gathered[a].at[slab(*block)],
                                                 dst_ref=gathered[a].at[slab(*block)], send_sem=send_sems.at[k, a],
                                                 recv_sem=recv_sems.at[k, a], device_id=to, device_id_type=MESH)
                    for a in range(3)]

        local = [pltpu.make_async_copy(mine[a], gathered[a].at[slab(*me)], local_sems.at[a]) for a in range(3)]
        for cp in local:
            cp.start()
        started = copies(0, me, sibling, own=True) + copies(1, me, x_nbr, own=True) + copies(2, me, y_nbr, own=True)
        for cp in started:
            cp.start()
        for cp in copies(1, x_nbr, me) + copies(2, y_nbr, me):
            cp.wait_recv()
        onward = copies(3, relay_from, relay_to) + copies(4, x_nbr, sibling) + copies(5, y_nbr, sibling)
        for cp in onward:
            cp.start()
        for cp in copies(3, diagonal, me):
            cp.wait_recv()
        last = copies(6, diagonal, sibling)
        for cp in last:
            cp.start()
        for cp in copies(0, sibling, me):
            cp.wait_recv()
        for k, origin in ((4, (1 - x, y, 1 - c)), (5, (x, 1 - y, 1 - c)), (6, (1 - x, 1 - y, 1 - c))):
            for cp in copies(k, origin, me):
                cp.wait_recv()
        for cp in started + onward + last:
            cp.wait_send()
        for cp in local:
            cp.wait()

    return pl.pallas_call(
        body, name="ag_weights",
        in_specs=[VMEM_SPEC, VMEM_SPEC, VMEM_SPEC], out_specs=[HBM_SPEC, HBM_SPEC, HBM_SPEC],
        out_shape=[jax.ShapeDtypeStruct((N_DEV, D_MODEL, SHARD_IN), BF16),
                   jax.ShapeDtypeStruct((N_DEV, SHARD_OUT, D_MODEL), BF16),
                   jax.ShapeDtypeStruct((N_DEV, 8, BLK), F32)],
        scratch_shapes=[pltpu.VMEM((D_MODEL, SHARD_IN), BF16), pltpu.VMEM((SHARD_OUT, D_MODEL), BF16),
                        pltpu.VMEM((8, BLK), F32), pltpu.SemaphoreType.DMA((N_DEV - 1, 3)),
                        pltpu.SemaphoreType.DMA((N_DEV - 1, 3)), pltpu.SemaphoreType.DMA((3,))],
        compiler_params=pltpu.CompilerParams(vmem_limit_bytes=VMEM_LIMIT),
    )(w_in, w_out, conv_w)


def _dw_in_rs(ht, dz, dw_out, small):
    seq = dz.shape[0]

    def body(cols_ref, ht_ref, dz_ref, dout_ref, sm_ref, own_ref, rin_ref, rout_ref, rsm_ref, to_sibling, landed,
             to_chip, zero_buf, d2d_send, d2d_recv, ici_send, ici_recv, side_send, side_recv, local_sems):
        del cols_ref
        step = pl.program_id(0)
        x, y, c = _coords()
        me = 4 * x + 2 * y + c
        sibling = (x, y, 1 - c)
        chips = [(1 - x, y), (x, 1 - y), (1 - x, 1 - y)]

        def d2d(n):
            return pltpu.make_async_remote_copy(src_ref=to_sibling.at[n], dst_ref=landed.at[n], send_sem=d2d_send.at[n],
                                                recv_sem=d2d_recv.at[n], device_id=sibling, device_id_type=MESH)

        def ici(n):
            return pltpu.make_async_remote_copy(src_ref=to_chip.at[n], dst_ref=rin_ref.at[n], send_sem=ici_send.at[n],
                                                recv_sem=ici_recv.at[n], device_id=(*chips[n], c), device_id_type=MESH)

        def side(k, mine):
            peer, peer_idx = _peer(k)
            src_slab, dst_slab = (peer_idx, me) if mine else (me, peer_idx)
            pairs = [(dout_ref.at[src_slab], rout_ref.at[dst_slab]), (sm_ref, rsm_ref.at[dst_slab])]
            return [pltpu.make_async_remote_copy(src_ref=src, dst_ref=dst, send_sem=side_send.at[k - 1, a],
                                                 recv_sem=side_recv.at[k - 1, a], device_id=peer, device_id_type=MESH)
                    for a, (src, dst) in enumerate(pairs)]

        local = [pltpu.make_async_copy(zero_buf, rout_ref.at[me], local_sems.at[0]),
                 pltpu.make_async_copy(sm_ref, rsm_ref.at[me], local_sems.at[1])]

        @pl.when(step == 0)
        def _():
            zero_buf[...] = jnp.zeros_like(zero_buf)
            for cp in local:
                cp.start()
            for k in range(1, N_DEV):
                for cp in side(k, mine=True):
                    cp.start()

        dw = jnp.dot(ht_ref[...], dz_ref[...], preferred_element_type=F32)
        for n, at in zip(range(4), (0, 1, 2, N_DEV - 2)):
            @pl.when(step == at)
            def _(n=n):
                to_sibling[n] = dw.astype(BF16)
                d2d(n).start()

        for n in range(3):
            @pl.when(step == 3 + n)
            def _(n=n):
                d2d(n).wait_recv()
                to_chip[n] = (dw + landed[n].astype(F32)).astype(BF16)
                ici(n).start()

        @pl.when(step == N_DEV - 1)
        def _():
            d2d(3).wait_recv()
            own_ref[...] = dw + landed[3].astype(F32)
            for n in range(3):
                ici(n).wait_recv()
            for k in range(1, N_DEV):
                for cp in side(k, mine=False):
                    cp.wait_recv()
            for n in range(4):
                d2d(n).wait_send()
            for n in range(3):
                ici(n).wait_send()
            for k in range(1, N_DEV):
                for cp in side(k, mine=True):
                    cp.wait_send()
            for cp in local:
                cp.wait()

    x, y, c = _coords()
    others = [(1 - x, y), (x, 1 - y), (1 - x, 1 - y)]
    order = [(*chip, 1 - c) for chip in others] + [(*chip, c) for chip in others] + [(x, y, 1 - c), (x, y, c)]
    cols = jnp.stack([4 * px + 2 * py + pc for px, py, pc in order]).astype(jnp.int32)
    slab = (D_MODEL, SHARD_IN)
    grid_spec = pltpu.PrefetchScalarGridSpec(
        num_scalar_prefetch=1, grid=(N_DEV,),
        in_specs=[pl.BlockSpec((D_MODEL, seq), lambda s, cols: (0, 0), pipeline_mode=pl.Buffered(1)),
                  pl.BlockSpec((seq, SHARD_IN), lambda s, cols: (0, cols[s])), HBM_SPEC, HBM_SPEC],
        out_specs=[pl.BlockSpec(slab, lambda s, cols: (0, 0)), HBM_SPEC, HBM_SPEC, HBM_SPEC],
        scratch_shapes=[pltpu.VMEM((4, *slab), BF16), pltpu.VMEM((4, *slab), BF16), pltpu.VMEM((3, *slab), BF16),
                        pltpu.VMEM((SHARD_OUT, D_MODEL), BF16),
                        pltpu.SemaphoreType.DMA((4,)), pltpu.SemaphoreType.DMA((4,)),
                        pltpu.SemaphoreType.DMA((3,)), pltpu.SemaphoreType.DMA((3,)),
                        pltpu.SemaphoreType.DMA((N_DEV - 1, 2)), pltpu.SemaphoreType.DMA((N_DEV - 1, 2)),
                        pltpu.SemaphoreType.DMA((2,))])
    return pl.pallas_call(
        body, name="dw_in_rs", grid_spec=grid_spec,
        out_shape=[jax.ShapeDtypeStruct(slab, F32),
                   jax.ShapeDtypeStruct((3, *slab), BF16),
                   jax.ShapeDtypeStruct((N_DEV, SHARD_OUT, D_MODEL), BF16),
                   jax.ShapeDtypeStruct((N_DEV, 8, D_MODEL), F32)],
        compiler_params=_params(1),
    )(cols, ht, dz, dw_out, small)


def _adamw_math(w, g, m, v):
    m = ADAM_B1 * m + (1.0 - ADAM_B1) * g
    v = ADAM_B2 * v + (1.0 - ADAM_B2) * (g * g)
    m_hat = m / (1.0 - ADAM_B1 ** ADAM_STEP)
    v_hat = v / (1.0 - ADAM_B2 ** ADAM_STEP)
    delta = -ADAM_LR * (m_hat / (jnp.sqrt(v_hat) + ADAM_EPS) + ADAM_WD * w)
    return delta, m, v


def _sum_slabs(ref, first=None):
    total = ref[0].astype(F32) if first is None else first + ref[0].astype(F32)
    for s in range(1, ref.shape[0]):
        total = total + ref[s].astype(F32)
    return total


def _adamw_slabs(parts, own, own_slab, w, m, v, name, tr):
    rows, cols = w.shape
    tile = pl.BlockSpec((tr, cols), lambda i, s: (i, 0))
    own_spec = tile if own_slab is None else pl.BlockSpec((1, tr, cols), lambda i, s: (s[0], i, 0))

    def body(s_ref, p_ref, own_ref, w_ref, m_ref, v_ref, g_ref, d_ref, nm_ref, nv_ref):
        del s_ref
        g = _sum_slabs(p_ref, own_ref[...].reshape(tr, cols))
        g_ref[...] = g
        d_ref[...], nm_ref[...], nv_ref[...] = _adamw_math(w_ref[...], g, m_ref[...], v_ref[...])

    slab = jnp.zeros((1,), jnp.int32) if own_slab is None else own_slab.reshape(1).astype(jnp.int32)
    grid_spec = pltpu.PrefetchScalarGridSpec(
        num_scalar_prefetch=1, grid=(rows // tr,),
        in_specs=[pl.BlockSpec((parts.shape[0], tr, cols), lambda i, s: (0, i, 0)), own_spec, tile, tile, tile],
        out_specs=[tile] * 4)
    return pl.pallas_call(
        body, name=name, grid_spec=grid_spec,
        out_shape=[jax.ShapeDtypeStruct((rows, cols), F32)] * 4,
        compiler_params=_params(1),
    )(slab, parts, own, w, m, v)


def _adamw_small(parts, me, pre, post, conv):
    n_conv = CONV_W // N_DEV

    def body(me_ref, p_ref, *refs):
        ins, (loss_ref, *outs) = refs[:9], refs[9:]
        sums = _sum_slabs(p_ref)
        loss_ref[...] = sums[2:3, 0:1]
        mine = pltpu.roll(sums[:, 0:CONV_W], (CONV_W - me_ref[0] * n_conv) % CONV_W, 1)[3:6, 0:n_conv]
        for n, g in enumerate((sums[0:1], sums[1:2], mine)):
            w_ref, m_ref, v_ref = ins[3 * n:3 * n + 3]
            outs[4 * n][...] = g
            for out, val in zip(outs[4 * n + 1:4 * n + 4], _adamw_math(w_ref[...], g, m_ref[...], v_ref[...])):
                out[...] = val

    row = jax.ShapeDtypeStruct((1, D_MODEL), F32)
    small = jax.ShapeDtypeStruct((3, n_conv), F32)
    return pl.pallas_call(
        body, name="adamw_small",
        in_specs=[pl.BlockSpec(memory_space=pltpu.SMEM)] + [VMEM_SPEC] * 10,
        out_shape=[jax.ShapeDtypeStruct((1, 1), F32)] + [row] * 8 + [small] * 4,
    )(me.reshape(1).astype(jnp.int32), parts, *pre, *post, *conv)


def kernel(x, norm_pre_g, w_in, conv_w, w_out, norm_post_g, loss_target, m_norm_pre_g, m_w_in, m_conv_w, m_w_out,
           m_norm_post_g, v_norm_pre_g, v_w_in, v_conv_w, v_w_out, v_norm_post_g):
    n_conv = CONV_W // N_DEV
    w_in_g, w_out_g, conv_g = _ag_weights(w_in, w_out, conv_w)
    conv_full = conv_g[:, 0:3, 0:n_conv].transpose(1, 0, 2).reshape(3, CONV_W)
    grad_x, ht, dz, dw_out, dw_out_bf, small = _local_step(x[0], loss_target[0], norm_pre_g, norm_post_g, w_in_g,
                                                           w_out_g.reshape(D_MODEL, D_MODEL), conv_full)
    own_in, r_in, r_out, r_small = _dw_in_rs(ht, dz, dw_out_bf.reshape(N_DEV, SHARD_OUT, D_MODEL), small)
    me = 4 * lax.axis_index("x") + 2 * lax.axis_index("y") + lax.axis_index("c")
    g_in, d_in, nm_in, nv_in = _adamw_slabs(r_in, own_in, None, w_in, m_w_in, v_w_in, "adamw_in", 256)
    g_out, d_out, nm_out, nv_out = _adamw_slabs(r_out, dw_out.reshape(N_DEV, SHARD_OUT, D_MODEL), me, w_out, m_w_out,
                                                v_w_out, "adamw_out", SHARD_OUT)
    vec = lambda a: a.reshape(1, D_MODEL)
    (loss, g_pre, d_pre, nm_pre, nv_pre, g_post, d_post, nm_post, nv_post, g_conv, d_conv, nm_conv,
     nv_conv) = _adamw_small(r_small, me, [vec(a) for a in (norm_pre_g, m_norm_pre_g, v_norm_pre_g)],
                             [vec(a) for a in (norm_post_g, m_norm_post_g, v_norm_post_g)],
                             (conv_w, m_conv_w, v_conv_w))
    flat = lambda a: a.reshape(D_MODEL)
    return (loss.reshape(()), grad_x[None], flat(g_pre), g_in, g_conv, g_out, flat(g_post),
            flat(d_pre), d_in, d_conv, d_out, flat(d_post),
            flat(nm_pre), nm_in, nm_conv, nm_out, flat(nm_post),
            flat(nv_pre), nv_in, nv_conv, nv_out, flat(nv_post))
```

```python
import functools

import jax
import jax.numpy as jnp
from jax import lax
from jax.experimental import pallas as pl
from jax.experimental.pallas import tpu as pltpu

F32 = jnp.float32
BF16 = jnp.bfloat16

D_MODEL = 1024
HEAD_DIM = 64
ATTN_W = 768
CONV_W = 256
IN_W = 4096
REST_W = IN_W - 3 * ATTN_W
BLK = 128
N_DEV = 8
SHARD_IN = IN_W // N_DEV
SHARD_OUT = D_MODEL // N_DEV
DILATIONS = (1, 4, 16)
ROPE_THETA = 10000.0
NORM_EPS = 1e-6
NEG = -1e30

ADAM_LR = 0.001
ADAM_B1 = 0.9
ADAM_B2 = 0.999
ADAM_EPS = 1e-08
ADAM_WD = 0.01
ADAM_STEP = 10

VMEM_LIMIT = 56 * 1024 * 1024
MESH = pl.DeviceIdType.MESH


def _params(n_grid):
    return pltpu.CompilerParams(dimension_semantics=("arbitrary",) * n_grid, vmem_limit_bytes=VMEM_LIMIT)


def _resident(shape):
    zeros = (0,) * len(shape)
    return pl.BlockSpec(shape, lambda *_: zeros, pipeline_mode=pl.Buffered(1))


def _sigmoid(a):
    return 1.0 / (1.0 + jnp.exp(-a))


def _swap_halves(t, first_half):
    return jnp.where(first_half, pltpu.roll(t, BLK - 32, 1), pltpu.roll(t, 32, 1))


def _rope_tables(seq, tm):
    half = HEAD_DIM // 2
    inv_freq = ROPE_THETA ** (-jnp.arange(half, dtype=F32) * 2.0 / HEAD_DIM)
    freq = jnp.concatenate([inv_freq] * 4)
    sign = jnp.concatenate([-jnp.ones(half, F32), jnp.ones(half, F32)] * 2)
    starts = (jnp.arange(seq // tm) * tm).astype(F32)[:, None] * freq[None, :]
    rows = jnp.arange(tm).astype(F32)[:, None] * freq[None, :]
    slab = lambda a: jnp.broadcast_to(a[:, None, :], (seq // tm, 8, BLK))
    return slab(jnp.cos(starts)), slab(jnp.sin(starts) * sign), jnp.cos(rows), jnp.sin(rows) * sign


def _rope_specs(tm):
    return [pl.BlockSpec((1, 8, BLK), lambda i: (i, 0, 0))] * 2 + [_resident((tm, BLK))] * 2


def _tile_rope(cos_start, sin_start, cos_row, sin_row):
    ca, sa, cb, sb = cos_start[0, 0:1, :], sin_start[0, 0:1, :], cos_row[...], sin_row[...]
    return ca * cb - sa * sb, sa * cb + ca * sb


N_CHUNK = ATTN_W // BLK


def _lanes(r, c):
    return slice(r * ATTN_W + c * BLK, r * ATTN_W + (c + 1) * BLK)


def _to_residues(src, chunk0, dst_refs, tmp, rows, dtype):
    assert DILATIONS == (1, 4, 16)
    dst1, dst4, dst16 = dst_refs
    n4, n16 = rows // 4, rows // 16
    for c in range(N_CHUNK):
        if dst1 is not None:
            dst1[:, _lanes(0, c)] = src[chunk0 + c].astype(dtype)
        for r1 in range(4):
            tmp[c, r1 * n4:(r1 + 1) * n4, :] = src[chunk0 + c, pl.ds(r1, n4, stride=4), :]
        for r1 in range(4):
            if dst4 is not None:
                dst4[:, _lanes(r1, c)] = tmp[c, r1 * n4:(r1 + 1) * n4, :].astype(dtype)
            for r2 in range(4 if dst16 is not None else 0):
                dst16[:, _lanes(4 * r2 + r1, c)] = tmp[c, pl.ds(r1 * n4 + r2, n16, stride=4), :].astype(dtype)


def _from_residue(src_ref, dst, dil, rows, tmp=None):
    assert dil in (4, 16)
    n4, n16 = rows // 4, rows // 16
    for c in range(N_CHUNK):
        for r1 in range(4):
            if dil == 4:
                piece = src_ref[:, _lanes(r1, c)].astype(F32)
            else:
                for r2 in range(4):
                    tmp[c, pl.ds(r1 * n4 + r2, n16, stride=4), :] = src_ref[:, _lanes(4 * r2 + r1, c)].astype(F32)
                piece = tmp[c, r1 * n4:(r1 + 1) * n4, :]
            dst[c, pl.ds(r1, n4, stride=4), :] = piece


def _residue_spec(tm, dil):
    return pl.BlockSpec((tm // dil, dil * ATTN_W), lambda i: (i, 0))


def _residue_shape(seq, dil, dtype):
    return jax.ShapeDtypeStruct((seq // dil, dil * ATTN_W), dtype)


def _fwd_in(x, g_pre, w_in_g, tm=512):
    seq = x.shape[0]
    n_dil = len(DILATIONS)

    def body(x_ref, g_ref, w_ref, ca_ref, sa_ref, cb_ref, sb_ref, *rest):
        qkv_refs, (zr_ref, ht_ref, qkv_scr, tmp) = rest[:3 * n_dil], rest[3 * n_dil:]
        xv = x_ref[...]
        r = lax.rsqrt(jnp.mean(xv * xv, axis=-1, keepdims=True) + NORM_EPS)
        hf = (xv * r) * g_ref[...]
        h = hf.astype(BF16)
        ht_ref[...] = h.T
        cos, sin = _tile_rope(ca_ref, sa_ref, cb_ref, sb_ref)
        first_half = (lax.broadcasted_iota(jnp.int32, (tm, BLK), 1) & 32) == 0

        def rope(t):
            return t * cos + _swap_halves(t, first_half) * sin

        def project(j):
            return jnp.dot(h, w_ref[j], preferred_element_type=F32)

        def place(j, zj):
            for n in range(SHARD_IN // BLK):
                chunk, t = j * (SHARD_IN // BLK) + n, zj[:, n * BLK:(n + 1) * BLK]
                if chunk < N_CHUNK:
                    qkv_scr[chunk] = rope(t) * HEAD_DIM ** -0.5
                elif chunk < 2 * N_CHUNK:
                    qkv_scr[chunk] = rope(t)
                elif chunk < 3 * N_CHUNK:
                    qkv_scr[chunk] = t
                else:
                    zr_ref[:, (chunk - 3 * N_CHUNK) * BLK:(chunk - 3 * N_CHUNK + 1) * BLK] = t.astype(BF16)

        ahead = project(0)
        for j in range(N_DEV):
            zj = ahead
            if j + 1 < N_DEV:
                ahead = project(j + 1)
            place(j, zj)
            for a in range(3):
                if (a + 1) * N_CHUNK - 1 in range(j * (SHARD_IN // BLK), (j + 1) * (SHARD_IN // BLK)):
                    _to_residues(qkv_scr, a * N_CHUNK, [qkv_refs[3 * n + a] for n in range(n_dil)], tmp, tm, BF16)

    row = lambda w: pl.BlockSpec((tm, w), lambda i: (i, 0))
    outs = pl.pallas_call(
        body, name="fwd_in", grid=(seq // tm,),
        in_specs=[row(D_MODEL), _resident((1, D_MODEL)), _resident((N_DEV, D_MODEL, SHARD_IN))] + _rope_specs(tm),
        out_specs=[_residue_spec(tm, dil) for dil in DILATIONS for _ in range(3)]
        + [row(REST_W), pl.BlockSpec((D_MODEL, tm), lambda i: (0, i))],
        out_shape=[_residue_shape(seq, dil, BF16) for dil in DILATIONS for _ in range(3)]
        + [jax.ShapeDtypeStruct((seq, REST_W), BF16), jax.ShapeDtypeStruct((D_MODEL, seq), BF16)],
        scratch_shapes=[pltpu.VMEM((3 * N_CHUNK, tm, BLK), F32), pltpu.VMEM((N_CHUNK, tm, BLK), F32)],
        compiler_params=_params(1),
    )(x, g_pre.reshape(1, D_MODEL), w_in_g, *_rope_tables(seq, tm))
    qkv = [tuple(outs[3 * n:3 * n + 3]) for n in range(n_dil)]
    return qkv, outs[3 * n_dil], outs[3 * n_dil + 1]


def _band_bias(first_block):
    kj = lax.broadcasted_iota(jnp.int32, (2 * BLK, BLK), 0)
    qi = lax.broadcasted_iota(jnp.int32, (2 * BLK, BLK), 1)
    valid = (kj >= qi) & (kj <= qi + BLK)
    bias = jnp.where(valid, 0.0, NEG).astype(BF16)
    bias_first = jnp.where(valid & (kj >= BLK), 0.0, NEG).astype(BF16)
    onehot = ((kj & (BLK - 1)) == qi).astype(F32).astype(BF16)
    return onehot, bias, jnp.where(first_block, bias_first, bias)


def _stack_heads(t):
    keep0 = (lax.broadcasted_iota(jnp.int32, t.shape, 1) < HEAD_DIM).astype(F32).astype(BF16)
    return jnp.concatenate([t * keep0, t * (1 - keep0)], axis=0)


def _unstack_heads(t2, head0):
    return jnp.where(head0, t2[:BLK], t2[BLK:])


def _rows_per_head(a, head0):
    b = pltpu.roll(a, HEAD_DIM, 1)
    rows = jnp.concatenate([jnp.where(head0, a, b), jnp.where(head0, b, a)], axis=0)
    return jnp.concatenate([rows, rows], axis=1)


FWD_UNITS, FWD_COLS = 64, 8
BWD_UNITS, BWD_COLS = 32, 4


def _attn_specs(length, dil, units, max_cols):
    n_blocks = length // BLK
    tb = min(units, n_blocks)
    nc = max(n for n in range(1, min(units // tb, max_cols) + 1) if (dil * N_CHUNK) % n == 0)
    assert n_blocks % tb == 0
    tile = pl.BlockSpec((tb * BLK, nc * BLK), lambda c, t: (t, c))
    prev = pl.BlockSpec((BLK, nc * BLK), lambda c, t: (jnp.maximum(t * tb - 1, 0), c))
    grid = (dil * N_CHUNK // nc, n_blocks // tb)
    return tb, nc, tile, prev, grid


def _window(prev_ref, cur_ref, j, cols):
    if j == 0:
        return jnp.concatenate([prev_ref[:, cols], cur_ref[0:BLK, cols]], axis=0)
    return cur_ref[(j - 1) * BLK:(j + 1) * BLK, cols]


def _attn_fwd(q, k, v, dil):
    length = q.shape[0]
    tb, nc, tile, prev, grid = _attn_specs(length, dil, FWD_UNITS, FWD_COLS)

    def body(q_ref, kc_ref, kp_ref, vc_ref, vp_ref, o_ref, lse_ref):
        head0 = lax.broadcasted_iota(jnp.int32, (BLK, BLK), 1) < HEAD_DIM
        onehot, bias, bias_start = _band_bias(pl.program_id(1) == 0)
        ones = jnp.ones((2 * BLK, BLK), BF16)

        def scores(c, j):
            rows, cols = slice(j * BLK, (j + 1) * BLK), slice(c * BLK, (c + 1) * BLK)
            q2 = jnp.concatenate([_stack_heads(q_ref[rows, cols]), onehot], axis=1)
            kk = jnp.concatenate([_window(kp_ref, kc_ref, j, cols), bias_start if j == 0 else bias], axis=1)
            return (lax.dot_general(q2, kk, (((1,), (1,)), ((), ())), preferred_element_type=F32),)

        def probabilities(c, j, s):
            m = jnp.max(s, axis=1, keepdims=True)
            return m, jnp.exp(s - m).astype(BF16)

        def outputs(c, j, m, p):
            rows, cols = slice(j * BLK, (j + 1) * BLK), slice(c * BLK, (c + 1) * BLK)
            vv = jnp.concatenate([_window(vp_ref, vc_ref, j, cols), ones], axis=1)
            pv = jnp.dot(p, vv, preferred_element_type=F32)
            den = pv[:, BLK:]
            o_ref[rows, cols] = _unstack_heads(pv[:, :BLK] / den, head0).astype(BF16)
            lse_ref[rows, cols] = _unstack_heads(m + jnp.log(den), head0)

        units = [(c, j) for c in range(nc) for j in range(tb)]
        stage1, stage2 = {}, {}
        for n in range(len(units) + 2):
            if n < len(units):
                stage1[n] = scores(*units[n])
            if 0 <= n - 1 < len(units):
                stage2[n - 1] = probabilities(*units[n - 1], *stage1.pop(n - 1))
            if 0 <= n - 2 < len(units):
                outputs(*units[n - 2], *stage2.pop(n - 2))

    return pl.pallas_call(
        body, name=f"attn_fwd_d{dil}", grid=grid,
        in_specs=[tile, tile, prev, tile, prev], out_specs=[tile, tile],
        out_shape=[jax.ShapeDtypeStruct(q.shape, BF16), jax.ShapeDtypeStruct(q.shape, F32)],
        compiler_params=_params(2),
    )(q, k, k, v, v)


def _attn_bwd(q, k, v, do, lse_own, log_w, delta, dil):
    length = q.shape[0]
    tb, nc, tile, prev, grid = _attn_specs(length, dil, BWD_UNITS, BWD_COLS)
    whole = pl.BlockSpec((length, nc * BLK), lambda c, t: (0, c))

    def body(q_ref, do_ref, lse_ref, lw_ref, dl_ref, kc_ref, kp_ref, vc_ref, vp_ref, dq_ref, dk_ref, dv_ref):
        t = pl.program_id(1)
        head0 = lax.broadcasted_iota(jnp.int32, (BLK, BLK), 1) < HEAD_DIM
        onehot, bias, bias_start = _band_bias(t == 0)

        def scores(c, j):
            rows, cols = slice(j * BLK, (j + 1) * BLK), slice(c * BLK, (c + 1) * BLK)
            q2 = _stack_heads(q_ref[rows, cols])
            do2 = _stack_heads(do_ref[rows, cols])
            kk = _window(kp_ref, kc_ref, j, cols)
            s = lax.dot_general(jnp.concatenate([q2, onehot], axis=1),
                                jnp.concatenate([kk, bias_start if j == 0 else bias], axis=1),
                                (((1,), (1,)), ((), ())), preferred_element_type=F32)
            dp = lax.dot_general(do2, _window(vp_ref, vc_ref, j, cols), (((1,), (1,)), ((), ())),
                                 preferred_element_type=F32)
            return q2, do2, kk, s, dp

        def probabilities(c, j, q2, do2, kk, s, dp):
            rows, cols = slice(j * BLK, (j + 1) * BLK), slice(c * BLK, (c + 1) * BLK)
            lse_tot = lse_ref[rows, cols] - lw_ref[rows, cols].astype(F32)
            p = jnp.exp(s - _rows_per_head(lse_tot, head0))
            ds = (p * (dp - _rows_per_head(dl_ref[rows, cols].astype(F32), head0))).astype(BF16)
            return q2, do2, kk, p.astype(BF16), ds

        def gradients(c, j, q2, do2, kk, p, ds):
            rows, cols = slice(j * BLK, (j + 1) * BLK), slice(c * BLK, (c + 1) * BLK)
            dq2 = jnp.dot(ds, kk, preferred_element_type=F32)
            dq_ref[rows, cols] = (_unstack_heads(dq2, head0) * HEAD_DIM ** -0.5).astype(BF16)
            dk2 = lax.dot_general(ds, q2, (((0,), (0,)), ((), ())), preferred_element_type=F32)
            dv2 = lax.dot_general(p, do2, (((0,), (0,)), ((), ())), preferred_element_type=F32)
            own = pl.ds(pl.multiple_of((t * tb + j) * BLK, BLK), BLK)
            dk_ref[own, cols] = dk2[BLK:].astype(BF16)
            dv_ref[own, cols] = dv2[BLK:].astype(BF16)

            def add_to_previous():
                before = pl.ds(pl.multiple_of((t * tb + j - 1) * BLK, BLK), BLK)
                dk_ref[before, cols] = (dk_ref[before, cols].astype(F32) + dk2[:BLK]).astype(BF16)
                dv_ref[before, cols] = (dv_ref[before, cols].astype(F32) + dv2[:BLK]).astype(BF16)

            if j > 0:
                add_to_previous()
            elif grid[1] > 1:
                pl.when(t > 0)(add_to_previous)

        units = [(c, j) for c in range(nc) for j in range(tb)]
        stage1 = {0: scores(*units[0])}
        for n in range(len(units)):
            stage2 = probabilities(*units[n], *stage1.pop(n))
            if n + 1 < len(units):
                stage1[n + 1] = scores(*units[n + 1])
            gradients(*units[n], *stage2)

    return pl.pallas_call(
        body, name=f"attn_bwd_d{dil}", grid=grid,
        in_specs=[tile, tile, tile, tile, tile, tile, prev, tile, prev], out_specs=[tile, whole, whole],
        out_shape=[jax.ShapeDtypeStruct(q.shape, BF16)] * 3,
        compiler_params=_params(2),
    )(q, do, lse_own, log_w, delta, k, k, v, v)


HALO = 16


def _halo_specs(tm, seq):
    before = lambda w: pl.BlockSpec((HALO, w), lambda i: (jnp.maximum(i * (tm // HALO) - 1, 0), 0))
    after = lambda w: pl.BlockSpec((HALO, w), lambda i: (jnp.minimum((i + 1) * (tm // HALO), seq // HALO - 1), 0))
    return before, after


def _conv_taps(u, before, tm):
    row = lax.broadcasted_iota(jnp.int32, u.shape, 0)
    last, last2 = before[HALO - 1:HALO, :], before[HALO - 2:HALO - 1, :]
    u1 = jnp.where(row == 0, last, pltpu.roll(u, 1, 0))
    u2 = jnp.where(row == 0, last2, jnp.where(row == 1, last, pltpu.roll(u, 2, 0)))
    return u1, u2


def _attn_combine(o_parts, lse_parts, zr, conv_w, tm=256):
    seq = zr.shape[0]
    a0, h0, b0, c0, g0 = 0, ATTN_W, ATTN_W + CONV_W, ATTN_W + 2 * CONV_W, ATTN_W + 3 * CONV_W

    def body(o1, o2, o3, l1, l2, l3, zr_ref, zp_ref, w_ref, mixed_ref, o_ref, lse1, lse2, lse3, *scr):
        i = pl.program_id(0)
        for src, dst, dil in zip((o2, o3, l2, l3), scr[:4], DILATIONS[1:] * 2):
            _from_residue(src, dst, dil, tm, tmp=scr[5])
        for c in range(N_CHUNK):
            cols = slice(c * BLK, (c + 1) * BLK)
            la, lb, lc = l1[:, cols], scr[2][c], scr[3][c]
            top = jnp.maximum(jnp.maximum(la, lb), lc)
            ea, eb, ec = jnp.exp(la - top), jnp.exp(lb - top), jnp.exp(lc - top)
            den = ea + eb + ec
            inv = 1.0 / den
            o = (ea * inv) * o1[:, cols].astype(F32) + (eb * inv) * scr[0][c] + (ec * inv) * scr[1][c]
            o_ref[:, cols] = o.astype(BF16)
            lse_tot = top + jnp.log(den)
            lse1[:, cols] = (la - lse_tot).astype(BF16)
            scr[2][c] = lb - lse_tot
            scr[3][c] = lc - lse_tot
            ga = zr_ref[:, cols].astype(F32)
            mixed_ref[:, cols] = (o * (ga * _sigmoid(ga))).astype(BF16)
        _to_residues(scr[2], 0, (None, lse2, None), scr[4], tm, BF16)
        _to_residues(scr[3], 0, (None, None, lse3), scr[5], tm, BF16)
        part = lambda ref, lo, hi: ref[:, lo:hi].astype(F32)
        u = part(zr_ref, c0, g0) * part(zr_ref, h0, b0)
        before = jnp.where(i > 0, part(zp_ref, c0, g0) * part(zp_ref, h0, b0), 0.0)
        u1, u2 = _conv_taps(u, before, tm)
        y = u2 * w_ref[0:1, :] + u1 * w_ref[1:2, :] + u * w_ref[2:3, :]
        gc = part(zr_ref, g0, REST_W)
        mixed_ref[:, ATTN_W:] = ((part(zr_ref, b0, c0) * y) * (gc * _sigmoid(gc))).astype(BF16)

    row = lambda w: pl.BlockSpec((tm, w), lambda i: (i, 0))
    before, _ = _halo_specs(tm, seq)
    views = [_residue_spec(tm, dil) for dil in DILATIONS]
    outs = pl.pallas_call(
        body, name="attn_combine", grid=(seq // tm,),
        in_specs=views * 2 + [row(REST_W), before(REST_W), _resident((3, CONV_W))],
        out_specs=[row(D_MODEL), row(ATTN_W)] + views,
        out_shape=[jax.ShapeDtypeStruct((seq, D_MODEL), BF16), jax.ShapeDtypeStruct((seq, ATTN_W), BF16)]
        + [_residue_shape(seq, dil, BF16) for dil in DILATIONS],
        scratch_shapes=[pltpu.VMEM((N_CHUNK, tm, BLK), F32)] * 6,
        compiler_params=_params(1),
    )(*o_parts, *lse_parts, zr, zr, conv_w)
    return outs[0], outs[1], outs[2:]


def _out_loss_bwd(mixed, w_out_g, x, target, g_post, tm=512, n_parts=2):
    seq = x.shape[0]

    def body(mx_ref, w_ref, x_ref, t_ref, g_ref, dout_ref, dmx_ref, dw_ref, dwb_ref, st_ref):
        i = pl.program_id(0)
        g = g_ref[...]
        parts = [slice(n * (tm // n_parts), (n + 1) * (tm // n_parts)) for n in range(n_parts)]

        def project(rows):
            return jnp.dot(mx_ref[rows, :], w_ref[...], preferred_element_type=F32)

        def head(rows, y):
            r = lax.rsqrt(jnp.mean(y * y, axis=-1, keepdims=True) + NORM_EPS)
            yhat = y * r
            err = (x_ref[rows, :] + yhat * g) - t_ref[rows, :]
            dn = err * (1.0 / D_MODEL)
            dout_ref[rows, :] = dn
            tg = dn * g
            dy = (r * (tg - yhat * jnp.mean(tg * yhat, axis=-1, keepdims=True))).astype(BF16)
            dmx_ref[rows, :] = lax.dot_general(dy, w_ref[...], (((1,), (1,)), ((), ())),
                                               preferred_element_type=F32).astype(BF16)
            return dy, jnp.sum(dn * yhat, axis=0, keepdims=True), jnp.sum(err * err)

        ahead, done = project(parts[0]), []
        for n, rows in enumerate(parts):
            y = ahead
            if n + 1 < n_parts:
                ahead = project(parts[n + 1])
            done.append(head(rows, y))
        dy = jnp.concatenate([d[0] for d in done], axis=0)
        dw = lax.dot_general(mx_ref[...], dy, (((0,), (0,)), ((), ())), preferred_element_type=F32)
        gsum = functools.reduce(lambda a, b: a + b, [d[1] for d in done])
        lsum = jnp.broadcast_to(0.5 / D_MODEL * functools.reduce(lambda a, b: a + b, [d[2] for d in done]),
                                (1, D_MODEL))

        @pl.when(i == 0)
        def _():
            dw_ref[...] = dw
            st_ref[...] = jnp.zeros_like(st_ref)
            st_ref[0:1, :] = gsum
            st_ref[1:2, :] = lsum

        @pl.when(i > 0)
        def _():
            dw_ref[...] += dw
            st_ref[0:1, :] += gsum
            st_ref[1:2, :] += lsum

        @pl.when(i == seq // tm - 1)
        def _():
            dwb_ref[...] = dw_ref[...].astype(BF16)

    row = lambda w: pl.BlockSpec((tm, w), lambda i: (i, 0))
    whole = pl.BlockSpec((D_MODEL, D_MODEL), lambda i: (0, 0))
    return pl.pallas_call(
        body, name="out_loss_bwd", grid=(seq // tm,),
        in_specs=[row(D_MODEL), _resident((D_MODEL, D_MODEL)), row(D_MODEL), row(D_MODEL), _resident((1, D_MODEL))],
        out_specs=[row(D_MODEL), row(D_MODEL), whole, whole, pl.BlockSpec((8, D_MODEL), lambda i: (0, 0))],
        out_shape=[jax.ShapeDtypeStruct((seq, D_MODEL), F32), jax.ShapeDtypeStruct((seq, D_MODEL), BF16),
                   jax.ShapeDtypeStruct((D_MODEL, D_MODEL), F32), jax.ShapeDtypeStruct((D_MODEL, D_MODEL), BF16),
                   jax.ShapeDtypeStruct((8, D_MODEL), F32)],
        compiler_params=_params(1),
    )(mixed, w_out_g, x, target, g_post.reshape(1, D_MODEL))


def _head_sum(prod, same_head):
    hi = prod.astype(BF16)
    lo = (prod - hi.astype(F32)).astype(BF16)
    return (jnp.dot(hi, same_head, preferred_element_type=F32) + jnp.dot(lo, same_head, preferred_element_type=F32))


def _gate_bwd(dmixed, zr, o, conv_w, tm=256):
    seq = zr.shape[0]
    n_tiles = seq // tm
    n_dil = len(DILATIONS)
    a0, h0, b0, c0, g0 = 0, ATTN_W, ATTN_W + CONV_W, ATTN_W + 2 * CONV_W, ATTN_W + 3 * CONV_W

    def body(dm_ref, dmn_ref, zr_ref, zp_ref, zn_ref, o_ref, w_ref, *rest):
        do_refs, dl_refs = rest[:n_dil], rest[n_dil:2 * n_dil]
        dz_ref, dw_ref, do_scr, dl_scr, tmp = rest[2 * n_dil:]
        i = pl.program_id(0)
        part = lambda ref, lo, hi: ref[:, lo:hi].astype(F32)
        ga = part(zr_ref, a0, h0)
        sg = _sigmoid(ga)
        dattn = part(dm_ref, 0, ATTN_W)
        ov = o_ref[...].astype(F32)
        do = dattn * (ga * sg)
        dz_ref[:, a0:h0] = (dattn * ov * (sg * (1.0 + ga * (1.0 - sg)))).astype(BF16)
        li = lax.broadcasted_iota(jnp.int32, (BLK, BLK), 0) // HEAD_DIM
        lj = lax.broadcasted_iota(jnp.int32, (BLK, BLK), 1) // HEAD_DIM
        same_head = (li == lj).astype(BF16)
        prod = do * ov
        for c in range(N_CHUNK):
            cols = slice(c * BLK, (c + 1) * BLK)
            do_scr[c] = do[:, cols]
            dl_scr[c] = _head_sum(prod[:, cols], same_head)
        _to_residues(do_scr, 0, do_refs, tmp, tm, BF16)
        _to_residues(dl_scr, 0, dl_refs, tmp, tm, BF16)

        ch, cb, cc, gc = (part(zr_ref, lo, hi) for lo, hi in ((h0, b0), (b0, c0), (c0, g0), (g0, REST_W)))
        u = cc * ch
        before = jnp.where(i > 0, part(zp_ref, c0, g0) * part(zp_ref, h0, b0), 0.0)
        u1, u2 = _conv_taps(u, before, tm)
        w0, w1, w2 = w_ref[0:1, :], w_ref[1:2, :], w_ref[2:3, :]
        y = u2 * w0 + u1 * w1 + u * w2
        sc = _sigmoid(gc)
        silu_c = gc * sc
        dconv = part(dm_ref, ATTN_W, D_MODEL)
        dz_ref[:, b0:c0] = (dconv * y * silu_c).astype(BF16)
        dz_ref[:, g0:] = (dconv * (cb * y) * (sc * (1.0 + gc * (1.0 - sc)))).astype(BF16)
        dy = dconv * cb * silu_c
        gn = part(zn_ref, g0, REST_W)
        after = jnp.where(i < n_tiles - 1,
                          part(dmn_ref, ATTN_W, D_MODEL) * part(zn_ref, b0, c0) * (gn * _sigmoid(gn)), 0.0)
        row = lax.broadcasted_iota(jnp.int32, dy.shape, 0)
        nxt, nxt2 = after[0:1, :], after[1:2, :]
        dy1 = jnp.where(row == tm - 1, nxt, pltpu.roll(dy, tm - 1, 0))
        dy2 = jnp.where(row == tm - 1, nxt2, jnp.where(row == tm - 2, nxt, pltpu.roll(dy, tm - 2, 0)))
        du = dy * w2 + dy1 * w1 + dy2 * w0
        dz_ref[:, c0:g0] = (du * ch).astype(BF16)
        dz_ref[:, h0:b0] = (du * cc).astype(BF16)
        dws = [jnp.sum(dy * u2, axis=0, keepdims=True), jnp.sum(dy * u1, axis=0, keepdims=True),
               jnp.sum(dy * u, axis=0, keepdims=True)]

        @pl.when(i == 0)
        def _():
            dw_ref[...] = jnp.zeros_like(dw_ref)

        for n, part in enumerate(dws):
            dw_ref[n:n + 1, :] += part

    row_spec = lambda w: pl.BlockSpec((tm, w), lambda i: (i, 0))
    before, after = _halo_specs(tm, seq)
    views = [_residue_spec(tm, dil) for dil in DILATIONS]
    outs = pl.pallas_call(
        body, name="gate_bwd", grid=(n_tiles,),
        in_specs=[row_spec(D_MODEL), after(D_MODEL), row_spec(REST_W), before(REST_W), after(REST_W),
                  row_spec(ATTN_W), _resident((3, CONV_W))],
        out_specs=views * 2 + [row_spec(REST_W), pl.BlockSpec((8, CONV_W), lambda i: (0, 0))],
        out_shape=[_residue_shape(seq, dil, BF16) for dil in DILATIONS] * 2
        + [jax.ShapeDtypeStruct((seq, REST_W), BF16), jax.ShapeDtypeStruct((8, CONV_W), F32)],
        scratch_shapes=[pltpu.VMEM((N_CHUNK, tm, BLK), F32)] * 3,
        compiler_params=_params(1),
    )(dmixed, dmixed, zr, zr, zr, o, conv_w)
    return outs[:n_dil], outs[n_dil:2 * n_dil], outs[2 * n_dil], outs[2 * n_dil + 1]


def _in_bwd(dqs, dks, dvs, dzr, x, d_out, g_pre, w_in_g, st_post, dconv, tm=256):
    seq = x.shape[0]

    def body(q1, q2, q3, k1, k2, k3, v1, v2, v3, dzr_ref, ca_ref, sa_ref, cb_ref, sb_ref, x_ref, dout_ref, g_ref,
             w_ref, post_ref, dconv_ref, dz_ref, gx_ref, st_ref, *scratch):
        i = pl.program_id(0)
        cos, sin = _tile_rope(ca_ref, sa_ref, cb_ref, sb_ref)
        first_half = (lax.broadcasted_iota(jnp.int32, (tm, BLK), 1) & 32) == 0
        streams = [(q1, q2, q3), (k1, k2, k3), (v1, v2, v3)]
        from4, from16, tmp = scratch[0:3], scratch[3:6], scratch[6]
        per_slab = SHARD_IN // BLK

        def unrope(t):
            return t * cos - _swap_halves(t, first_half) * sin

        def to_positions(a):
            _from_residue(streams[a][1], from4[a], 4, tm)
            _from_residue(streams[a][2], from16[a], 16, tm, tmp=tmp)

        def assemble(j):
            for chunk in range(j * per_slab, (j + 1) * per_slab):
                a, c = divmod(chunk, N_CHUNK)
                if a < 3:
                    total = streams[a][0][:, _lanes(0, c)].astype(F32) + from4[a][c] + from16[a][c]
                    val = (unrope(total) if a < 2 else total).astype(BF16)
                else:
                    val = dzr_ref[:, (chunk - 3 * N_CHUNK) * BLK:(chunk - 3 * N_CHUNK + 1) * BLK]
                dz_ref[:, chunk * BLK:(chunk + 1) * BLK] = val
            return dz_ref[:, j * SHARD_IN:(j + 1) * SHARD_IN]

        order = [j for j in range(N_DEV) if j * per_slab >= 3 * N_CHUNK]
        order += [j for j in range(N_DEV) if j not in order]
        assert order[2] * per_slab >= 3 * N_CHUNK
        ahead = assemble(order[0])
        dh = None
        for n, j in enumerate(order):
            part = lax.dot_general(ahead, w_ref[j], (((1,), (1,)), ((), ())), preferred_element_type=F32)
            if n < 3:
                to_positions(n)
            if n + 1 < N_DEV:
                ahead = assemble(order[n + 1])
            dh = part if dh is None else dh + part
        xv = x_ref[...]
        r = lax.rsqrt(jnp.mean(xv * xv, axis=-1, keepdims=True) + NORM_EPS)
        xhat = xv * r
        tg = dh * g_ref[...]
        gx_ref[...] = dout_ref[...] + r * (tg - xhat * jnp.mean(tg * xhat, axis=-1, keepdims=True))
        gsum = jnp.sum(dh * xhat, axis=0, keepdims=True)

        @pl.when(i == 0)
        def _():
            st_ref[...] = jnp.zeros_like(st_ref)
            st_ref[1:3, :] = post_ref[0:2, :]
            st_ref[3:6, 0:CONV_W] = dconv_ref[0:3, :]

        st_ref[0:1, :] += gsum

    row = lambda w: pl.BlockSpec((tm, w), lambda i: (i, 0))
    return pl.pallas_call(
        body, name="in_bwd", grid=(seq // tm,),
        in_specs=[_residue_spec(tm, dil) for dil in DILATIONS] * 3
        + [row(REST_W)] + _rope_specs(tm) + [row(D_MODEL), row(D_MODEL), _resident((1, D_MODEL)),
                                             _resident((N_DEV, D_MODEL, SHARD_IN)), _resident((8, D_MODEL)),
                                             _resident((8, CONV_W))],
        out_specs=[row(IN_W), row(D_MODEL), pl.BlockSpec((8, D_MODEL), lambda i: (0, 0))],
        out_shape=[jax.ShapeDtypeStruct((seq, IN_W), BF16), jax.ShapeDtypeStruct((seq, D_MODEL), F32),
                   jax.ShapeDtypeStruct((8, D_MODEL), F32)],
        scratch_shapes=[pltpu.VMEM((N_CHUNK, tm, BLK), F32)] * 7,
        compiler_params=_params(1),
    )(*dqs, *dks, *dvs, dzr, *_rope_tables(seq, tm), x, d_out, g_pre.reshape(1, D_MODEL), w_in_g, st_post, dconv)


def _local_step(x, target, g_pre, g_post, w_in_g, w_out_g, conv_w):
    qkv, zr, ht = _fwd_in(x, g_pre, w_in_g)
    parts = [_attn_fwd(*qkv[n], dil) for n, dil in enumerate(DILATIONS)]
    mixed, o, lse = _attn_combine([p[0] for p in parts], [p[1] for p in parts], zr, conv_w)
    d_out, dmixed, dw_out, dw_out_bf, st_post = _out_loss_bwd(mixed, w_out_g, x, target, g_post)
    do, delta, dzr, dconv = _gate_bwd(dmixed, zr, o, conv_w)
    grads = [_attn_bwd(*qkv[n], do[n], parts[n][1], lse[n], delta[n], dil) for n, dil in enumerate(DILATIONS)]
    dz, grad_x, small = _in_bwd([g[0] for g in grads], [g[1] for g in grads], [g[2] for g in grads], dzr,
                                x, d_out, g_pre, w_in_g, st_post, dconv)
    return grad_x, ht, dz, dw_out, dw_out_bf, small


def _coords():
    return lax.axis_index("x"), lax.axis_index("y"), lax.axis_index("c")


def _peer(k):
    x, y, c = _coords()
    px = 1 - x if k & 4 else x
    py = 1 - y if k & 2 else y
    pc = 1 - c if k & 1 else c
    return (px, py, pc), 4 * px + 2 * py + pc


HBM_SPEC = pl.BlockSpec(memory_space=pltpu.HBM)
VMEM_SPEC = pl.BlockSpec(memory_space=pltpu.VMEM)


def _ag_weights(w_in, w_out, conv_w):
    def body(win_ref, wout_ref, cw_ref, gin_ref, gout_ref, gcw_ref, win_bf, wout_bf, cw_pad, send_sems, recv_sems,
             local_sems):
        x, y, c = _coords()
        me, sibling = (x, y, c), (x, y, 1 - c)
        flip = lambda v, yes: v + yes - 2 * v * yes
        x_nbr, y_nbr, diagonal = (1 - x, y, c), (x, 1 - y, c), (1 - x, 1 - y, c)
        relay_from = (flip(x, 1 - c), flip(y, c), c)
        relay_to = (flip(x, c), flip(y, 1 - c), c)
        slab = lambda px, py, pc: 4 * px + 2 * py + pc
        win_bf[...] = win_ref[...].astype(BF16)
        wout_bf[...] = wout_ref[...].astype(BF16)
        cw_pad[...] = jnp.zeros_like(cw_pad)
        cw_pad[0:3, 0:CONV_W // N_DEV] = cw_ref[...]
        mine = [win_bf, wout_bf, cw_pad]
        gathered = [gin_ref, gout_ref, gcw_ref]

        def copies(k, block, to, own=False):
            return [pltpu.make_async_remote_copy(src_ref=mine[a] if own else gathered[a].at[slab(*block)],
                                                 dst_ref=gathered[a].at[slab(*block)], send_sem=send_sems.at[k, a],
                                                 recv_sem=recv_sems.at[k, a], device_id=to, device_id_type=MESH)
                    for a in range(3)]

        local = [pltpu.make_async_copy(mine[a], gathered[a].at[slab(*me)], local_sems.at[a]) for a in range(3)]
        for cp in local:
            cp.start()
        started = copies(0, me, sibling, own=True) + copies(1, me, x_nbr, own=True) + copies(2, me, y_nbr, own=True)
        for cp in started:
            cp.start()
        for cp in copies(1, x_nbr, me) + copies(2, y_nbr, me):
            cp.wait_recv()
        onward = copies(3, relay_from, relay_to) + copies(4, x_nbr, sibling) + copies(5, y_nbr, sibling)
        for cp in onward:
            cp.start()
        for cp in copies(3, diagonal, me):
            cp.wait_recv()
        last = copies(6, diagonal, sibling)
        for cp in last:
            cp.start()
        for cp in copies(0, sibling, me):
            cp.wait_recv()
        for k, origin in ((4, (1 - x, y, 1 - c)), (5, (x, 1 - y, 1 - c)), (6, (1 - x, 1 - y, 1 - c))):
            for cp in copies(k, origin, me):
                cp.wait_recv()
        for cp in started + onward + last:
            cp.wait_send()
        for cp in local:
            cp.wait()

    return pl.pallas_call(
        body, name="ag_weights",
        in_specs=[VMEM_SPEC, VMEM_SPEC, VMEM_SPEC], out_specs=[HBM_SPEC, HBM_SPEC, HBM_SPEC],
        out_shape=[jax.ShapeDtypeStruct((N_DEV, D_MODEL, SHARD_IN), BF16),
                   jax.ShapeDtypeStruct((N_DEV, SHARD_OUT, D_MODEL), BF16),
                   jax.ShapeDtypeStruct((N_DEV, 8, BLK), F32)],
        scratch_shapes=[pltpu.VMEM((D_MODEL, SHARD_IN), BF16), pltpu.VMEM((SHARD_OUT, D_MODEL), BF16),
                        pltpu.VMEM((8, BLK), F32), pltpu.SemaphoreType.DMA((N_DEV - 1, 3)),
                        pltpu.SemaphoreType.DMA((N_DEV - 1, 3)), pltpu.SemaphoreType.DMA((3,))],
        compiler_params=pltpu.CompilerParams(vmem_limit_bytes=VMEM_LIMIT),
    )(w_in, w_out, conv_w)


def _dw_in_rs(ht, dz, dw_out, small):
    seq = dz.shape[0]

    def body(cols_ref, ht_ref, dz_ref, dout_ref, sm_ref, own_ref, rin_ref, rout_ref, rsm_ref, to_sibling, landed,
             to_chip, zero_buf, d2d_send, d2d_recv, ici_send, ici_recv, side_send, side_recv, local_sems):
        del cols_ref
        step = pl.program_id(0)
        x, y, c = _coords()
        me = 4 * x + 2 * y + c
        sibling = (x, y, 1 - c)
        chips = [(1 - x, y), (x, 1 - y), (1 - x, 1 - y)]

        def d2d(n):
            return pltpu.make_async_remote_copy(src_ref=to_sibling.at[n], dst_ref=landed.at[n], send_sem=d2d_send.at[n],
                                                recv_sem=d2d_recv.at[n], device_id=sibling, device_id_type=MESH)

        def ici(n):
            return pltpu.make_async_remote_copy(src_ref=to_chip.at[n], dst_ref=rin_ref.at[n], send_sem=ici_send.at[n],
                                                recv_sem=ici_recv.at[n], device_id=(*chips[n], c), device_id_type=MESH)

        def side(k, mine):
            peer, peer_idx = _peer(k)
            src_slab, dst_slab = (peer_idx, me) if mine else (me, peer_idx)
            pairs = [(dout_ref.at[src_slab], rout_ref.at[dst_slab]), (sm_ref, rsm_ref.at[dst_slab])]
            return [pltpu.make_async_remote_copy(src_ref=src, dst_ref=dst, send_sem=side_send.at[k - 1, a],
                                                 recv_sem=side_recv.at[k - 1, a], device_id=peer, device_id_type=MESH)
                    for a, (src, dst) in enumerate(pairs)]

        local = [pltpu.make_async_copy(zero_buf, rout_ref.at[me], local_sems.at[0]),
                 pltpu.make_async_copy(sm_ref, rsm_ref.at[me], local_sems.at[1])]

        @pl.when(step == 0)
        def _():
            zero_buf[...] = jnp.zeros_like(zero_buf)
            for cp in local:
                cp.start()
            for k in range(1, N_DEV):
                for cp in side(k, mine=True):
                    cp.start()

        dw = jnp.dot(ht_ref[...], dz_ref[...], preferred_element_type=F32)
        for n, at in zip(range(4), (0, 1, 2, N_DEV - 2)):
            @pl.when(step == at)
            def _(n=n):
                to_sibling[n] = dw.astype(BF16)
                d2d(n).start()

        for n in range(3):
            @pl.when(step == 3 + n)
            def _(n=n):
                d2d(n).wait_recv()
                to_chip[n] = (dw + landed[n].astype(F32)).astype(BF16)
                ici(n).start()

        @pl.when(step == N_DEV - 1)
        def _():
            d2d(3).wait_recv()
            own_ref[...] = dw + landed[3].astype(F32)
            for n in range(3):
                ici(n).wait_recv()
            for k in range(1, N_DEV):
                for cp in side(k, mine=False):
                    cp.wait_recv()
            for n in range(4):
                d2d(n).wait_send()
            for n in range(3):
                ici(n).wait_send()
            for k in range(1, N_DEV):
                for cp in side(k, mine=True):
                    cp.wait_send()
            for cp in local:
                cp.wait()

    x, y, c = _coords()
    others = [(1 - x, y), (x, 1 - y), (1 - x, 1 - y)]
    order = [(*chip, 1 - c) for chip in others] + [(*chip, c) for chip in others] + [(x, y, 1 - c), (x, y, c)]
    cols = jnp.stack([4 * px + 2 * py + pc for px, py, pc in order]).astype(jnp.int32)
    slab = (D_MODEL, SHARD_IN)
    grid_spec = pltpu.PrefetchScalarGridSpec(
        num_scalar_prefetch=1, grid=(N_DEV,),
        in_specs=[pl.BlockSpec((D_MODEL, seq), lambda s, cols: (0, 0), pipeline_mode=pl.Buffered(1)),
                  pl.BlockSpec((seq, SHARD_IN), lambda s, cols: (0, cols[s])), HBM_SPEC, HBM_SPEC],
        out_specs=[pl.BlockSpec(slab, lambda s, cols: (0, 0)), HBM_SPEC, HBM_SPEC, HBM_SPEC],
        scratch_shapes=[pltpu.VMEM((4, *slab), BF16), pltpu.VMEM((4, *slab), BF16), pltpu.VMEM((3, *slab), BF16),
                        pltpu.VMEM((SHARD_OUT, D_MODEL), BF16),
                        pltpu.SemaphoreType.DMA((4,)), pltpu.SemaphoreType.DMA((4,)),
                        pltpu.SemaphoreType.DMA((3,)), pltpu.SemaphoreType.DMA((3,)),
                        pltpu.SemaphoreType.DMA((N_DEV - 1, 2)), pltpu.SemaphoreType.DMA((N_DEV - 1, 2)),
                        pltpu.SemaphoreType.DMA((2,))])
    return pl.pallas_call(
        body, name="dw_in_rs", grid_spec=grid_spec,
        out_shape=[jax.ShapeDtypeStruct(slab, F32),
                   jax.ShapeDtypeStruct((3, *slab), BF16),
                   jax.ShapeDtypeStruct((N_DEV, SHARD_OUT, D_MODEL), BF16),
                   jax.ShapeDtypeStruct((N_DEV, 8, D_MODEL), F32)],
        compiler_params=_params(1),
    )(cols, ht, dz, dw_out, small)


def _adamw_math(w, g, m, v):
    m = ADAM_B1 * m + (1.0 - ADAM_B1) * g
    v = ADAM_B2 * v + (1.0 - ADAM_B2) * (g * g)
    m_hat = m / (1.0 - ADAM_B1 ** ADAM_STEP)
    v_hat = v / (1.0 - ADAM_B2 ** ADAM_STEP)
    delta = -ADAM_LR * (m_hat / (jnp.sqrt(v_hat) + ADAM_EPS) + ADAM_WD * w)
    return delta, m, v


def _sum_slabs(ref, first=None):
    total = ref[0].astype(F32) if first is None else first + ref[0].astype(F32)
    for s in range(1, ref.shape[0]):
        total = total + ref[s].astype(F32)
    return total


def _adamw_slabs(parts, own, own_slab, w, m, v, name, tr):
    rows, cols = w.shape
    tile = pl.BlockSpec((tr, cols), lambda i, s: (i, 0))
    own_spec = tile if own_slab is None else pl.BlockSpec((1, tr, cols), lambda i, s: (s[0], i, 0))

    def body(s_ref, p_ref, own_ref, w_ref, m_ref, v_ref, g_ref, d_ref, nm_ref, nv_ref):
        del s_ref
        g = _sum_slabs(p_ref, own_ref[...].reshape(tr, cols))
        g_ref[...] = g
        d_ref[...], nm_ref[...], nv_ref[...] = _adamw_math(w_ref[...], g, m_ref[...], v_ref[...])

    slab = jnp.zeros((1,), jnp.int32) if own_slab is None else own_slab.reshape(1).astype(jnp.int32)
    grid_spec = pltpu.PrefetchScalarGridSpec(
        num_scalar_prefetch=1, grid=(rows // tr,),
        in_specs=[pl.BlockSpec((parts.shape[0], tr, cols), lambda i, s: (0, i, 0)), own_spec, tile, tile, tile],
        out_specs=[tile] * 4)
    return pl.pallas_call(
        body, name=name, grid_spec=grid_spec,
        out_shape=[jax.ShapeDtypeStruct((rows, cols), F32)] * 4,
        compiler_params=_params(1),
    )(slab, parts, own, w, m, v)


def _adamw_small(parts, me, pre, post, conv):
    n_conv = CONV_W // N_DEV

    def body(me_ref, p_ref, *refs):
        ins, (loss_ref, *outs) = refs[:9], refs[9:]
        sums = _sum_slabs(p_ref)
        loss_ref[...] = sums[2:3, 0:1]
        mine = pltpu.roll(sums[:, 0:CONV_W], (CONV_W - me_ref[0] * n_conv) % CONV_W, 1)[3:6, 0:n_conv]
        for n, g in enumerate((sums[0:1], sums[1:2], mine)):
            w_ref, m_ref, v_ref = ins[3 * n:3 * n + 3]
            outs[4 * n][...] = g
            for out, val in zip(outs[4 * n + 1:4 * n + 4], _adamw_math(w_ref[...], g, m_ref[...], v_ref[...])):
                out[...] = val

    row = jax.ShapeDtypeStruct((1, D_MODEL), F32)
    small = jax.ShapeDtypeStruct((3, n_conv), F32)
    return pl.pallas_call(
        body, name="adamw_small",
        in_specs=[pl.BlockSpec(memory_space=pltpu.SMEM)] + [VMEM_SPEC] * 10,
        out_shape=[jax.ShapeDtypeStruct((1, 1), F32)] + [row] * 8 + [small] * 4,
    )(me.reshape(1).astype(jnp.int32), parts, *pre, *post, *conv)


def kernel(x, norm_pre_g, w_in, conv_w, w_out, norm_post_g, loss_target, m_norm_pre_g, m_w_in, m_conv_w, m_w_out,
           m_norm_post_g, v_norm_pre_g, v_w_in, v_conv_w, v_w_out, v_norm_post_g):
    n_conv = CONV_W // N_DEV
    w_in_g, w_out_g, conv_g = _ag_weights(w_in, w_out, conv_w)
    conv_full = conv_g[:, 0:3, 0:n_conv].transpose(1, 0, 2).reshape(3, CONV_W)
    grad_x, ht, dz, dw_out, dw_out_bf, small = _local_step(x[0], loss_target[0], norm_pre_g, norm_post_g, w_in_g,
                                                           w_out_g.reshape(D_MODEL, D_MODEL), conv_full)
    own_in, r_in, r_out, r_small = _dw_in_rs(ht, dz, dw_out_bf.reshape(N_DEV, SHARD_OUT, D_MODEL), small)
    me = 4 * lax.axis_index("x") + 2 * lax.axis_index("y") + lax.axis_index("c")
    g_in, d_in, nm_in, nv_in = _adamw_slabs(r_in, own_in, None, w_in, m_w_in, v_w_in, "adamw_in", 256)
    g_out, d_out, nm_out, nv_out = _adamw_slabs(r_out, dw_out.reshape(N_DEV, SHARD_OUT, D_MODEL), me, w_out, m_w_out,
                                                v_w_out, "adamw_out", SHARD_OUT)
    vec = lambda a: a.reshape(1, D_MODEL)
    (loss, g_pre, d_pre, nm_pre, nv_pre, g_post, d_post, nm_post, nv_post, g_conv, d_conv, nm_conv,
     nv_conv) = _adamw_small(r_small, me, [vec(a) for a in (norm_pre_g, m_norm_pre_g, v_norm_pre_g)],
                             [vec(a) for a in (norm_post_g, m_norm_post_g, v_norm_post_g)],
                             (conv_w, m_conv_w, v_conv_w))
    flat = lambda a: a.reshape(D_MODEL)
    return (loss.reshape(()), grad_x[None], flat(g_pre), g_in, g_conv, g_out, flat(g_post),
            flat(d_pre), d_in, d_conv, d_out, flat(d_post),
            flat(nm_pre), nm_in, nm_conv, nm_out, flat(nm_post),
            flat(nv_pre), nv_in, nv_conv, nv_out, flat(nv_post))
```
